```python
import math
import jax, jax.numpy as jnp
from jax import lax
import numpy as np

D_MODEL = 1024
BATCH = 8
SEQ = 4096
DEPTH = 2

GRID_W = 64
CTX_LEN = 256
N_MOD = 6
NORM_EPS = 1e-6
SHORT_CONV = 3

RW_HEADS = 6
RW_HEAD_DIM = 64
RW_WIDTH = RW_HEADS * RW_HEAD_DIM
RW_DECAY_RANK = 64
RW_A_RANK = 64
RW_GATE_RANK = 128
RW_DECAY_SCALE = 0.6065306597
RW_GN_EPS = 64e-5
L2_EPS = 1e-12

MLA_HEADS = 6
MLA_Q_RANK = 256
MLA_KV_RANK = 128
MLA_NOPE_DIM = 64
MLA_ROPE_DIM = 32
MLA_V_DIM = 64
MLA_QK_DIM = MLA_NOPE_DIM + MLA_ROPE_DIM
MLA_WIDTH = MLA_HEADS * MLA_V_DIM
AXIS_ROPE_DIM = MLA_ROPE_DIM // 2
ROPE_THETA = 10000.0
Q_BLOCK = 128

HY_WIDTH = 256
HY_GROUPS = 4
HY_ORDER = 2
HY_POS_BANDS = 16
HY_POS_DIM = 1 + 2 * HY_POS_BANDS
HY_FILTER_HIDDEN = 64
HY_SHORT_DECAY_PCT = 0.3
HY_LONG_DECAY_PCT = 1.5
HY_DECAY_TARGET = 1e-2

PEER_HEADS = 8
PEER_N_KEYS = 128
PEER_N_EXPERTS = PEER_N_KEYS * PEER_N_KEYS
PEER_TOPK = 16
PEER_QUERY_DIM = 256
PEER_HALF = PEER_QUERY_DIM // 2
PEER_CHUNK = 128

RW_PROJ = 3 * RW_WIDTH + RW_DECAY_RANK + RW_A_RANK + RW_GATE_RANK
MLA_PROJ = MLA_Q_RANK + MLA_KV_RANK + MLA_ROPE_DIM
HY_PROJ = (HY_ORDER + 1) * HY_WIDTH
IN_PROJ = RW_PROJ + MLA_PROJ + HY_PROJ
MIX_WIDTH = RW_WIDTH + MLA_WIDTH + HY_WIDTH

kernel_name = 'hybrid_rwkv7_mla_hyena_peer_diffusion'


def rms_norm(x, gain):
    xf = x.astype(jnp.float32)
    y = xf * lax.rsqrt(jnp.mean(xf * xf, axis=-1, keepdims=True) + NORM_EPS)
    return (y * gain.astype(jnp.float32)).astype(x.dtype)


def short_conv(x, w):
    xp = jnp.pad(x, ((0, 0), (1, 1), (0, 0)))
    return xp[:, :-2] * w[0] + xp[:, 1:-1] * w[1] + xp[:, 2:] * w[2]


def _heads(t):
    return t.reshape(t.shape[:-1] + (RW_HEADS, RW_HEAD_DIM))


def rwkv7_prepare(p, conv_w, decay_up, decay0, a_up, a0, gate_up, k_k, k_a):
    z = short_conv(p, conv_w)
    o1, o2, o3 = RW_WIDTH, 2 * RW_WIDTH, 3 * RW_WIDTH
    o4 = o3 + RW_DECAY_RANK
    o5 = o4 + RW_A_RANK
    r, k, v = z[..., :o1], z[..., o1:o2], z[..., o2:o3]
    d_lo, a_lo, g_lo = z[..., o3:o4], z[..., o4:o5], z[..., o5:]
    decay = jnp.exp(-RW_DECAY_SCALE * jax.nn.sigmoid(
        decay0[:, None, None, :] + jnp.einsum('blr,nrc->nblc', jnp.tanh(d_lo), decay_up)))
    a = jax.nn.sigmoid(a0[:, None, None, :] + jnp.einsum('blr,nrc->nblc', a_lo, a_up))
    g = jnp.einsum('blr,rc->blc', jax.nn.sigmoid(g_lo), gate_up)
    kk = _heads(k * k_k).astype(jnp.float32)
    kk = kk * lax.rsqrt(jnp.sum(kk * kk, axis=-1, keepdims=True) + L2_EPS)
    k_rep = _heads(k)[None] * (1.0 + (_heads(a) - 1.0) * _heads(k_a))
    return _heads(r), _heads(k), _heads(v), _heads(decay), _heads(a), kk, k_rep, g


def rwkv7_scan(state0, r, decay, kk, a, k_rep, v, reverse):
    def step(S, inp):
        r_t, w_t, kk_t, a_t, k_t, v_t = inp
        S = (S * w_t[:, :, None, :]
             - jnp.einsum('bhvk,bhk->bhv', S, kk_t)[..., None] * (kk_t * a_t)[:, :, None, :]
             + v_t[..., None] * k_t[:, :, None, :])
        return S, jnp.einsum('bhvk,bhk->bhv', S, r_t)
    seq = tuple(jnp.moveaxis(t.astype(jnp.float32), 1, 0) for t in (r, decay, kk, a, k_rep, v))
    s_fin, y = lax.scan(step, state0, seq, reverse=reverse)
    return s_fin, jnp.moveaxis(y, 0, 1)


def rwkv7_readout(y, r, k, v, g, r_k, gn_g, gn_b):
    B, L = y.shape[:2]
    mu = jnp.mean(y, axis=-1, keepdims=True)
    var = jnp.mean(jnp.square(y - mu), axis=-1, keepdims=True)
    yn = ((y - mu) * lax.rsqrt(var + RW_GN_EPS)).reshape(B, L, RW_WIDTH) * gn_g + gn_b
    bonus = jnp.sum(r * k * r_k.reshape(RW_HEADS, RW_HEAD_DIM), axis=-1, keepdims=True) * v
    return ((yn + bonus.reshape(B, L, RW_WIDTH)) * g).astype(g.dtype)


def rwkv7_mixer(p_lat, p_ctx, conv_w, decay_up, decay0, a_up, a0, gate_up, k_k, k_a, r_k, gn_g, gn_b, need_ctx):
    prm = (conv_w, decay_up, decay0, a_up, a0, gate_up, k_k, k_a)
    lr, lk, lv, ldec, la, lkk, lkr, lg = rwkv7_prepare(p_lat, *prm)
    cr, ck, cv, cdec, ca, ckk, ckr, cg = rwkv7_prepare(p_ctx, *prm)
    B = p_lat.shape[0]
    ys_l, ys_c = [], []
    for d, rev in enumerate((False, True)):
        s0 = jnp.zeros((B, RW_HEADS, RW_HEAD_DIM, RW_HEAD_DIM), jnp.float32)
        s_ctx, yc = rwkv7_scan(s0, cr, cdec[d], ckk, ca[d], ckr[d], cv, rev)
        _, yl = rwkv7_scan(s_ctx, lr, ldec[d], lkk, la[d], lkr[d], lv, rev)
        ys_l.append(yl)
        ys_c.append(yc)
    out_l = rwkv7_readout(ys_l[0] + ys_l[1], lr, lk, lv, lg, r_k, gn_g, gn_b)
    out_c = rwkv7_readout(ys_c[0] + ys_c[1], cr, ck, cv, cg, r_k, gn_g, gn_b) if need_ctx else None
    return out_l, out_c


def axial_rope_tables(L):
    rows = L // GRID_W
    row, col = jnp.meshgrid(jnp.arange(rows), jnp.arange(GRID_W), indexing='ij')
    inv = ROPE_THETA ** (-jnp.arange(0, AXIS_ROPE_DIM, 2, dtype=jnp.float32) / AXIS_ROPE_DIM)
    pos = jnp.stack([row.reshape(-1), col.reshape(-1)], axis=-1).astype(jnp.float32)
    ang = pos[:, :, None] * inv[None, None, :]
    return jnp.cos(ang), jnp.sin(ang)


def apply_axial_rope(x, rope):
    cos, sin = rope
    B, L, H, _ = x.shape
    xa = x.reshape(B, L, H, 2, AXIS_ROPE_DIM).astype(jnp.float32)
    half = AXIS_ROPE_DIM // 2
    x1, x2 = xa[..., :half], xa[..., half:]
    cs, sn = cos[None, :, None], sin[None, :, None]
    out = jnp.concatenate([x1 * cs - x2 * sn, x2 * cs + x1 * sn], axis=-1)
    return out.reshape(B, L, H, MLA_ROPE_DIM).astype(x.dtype)


def mla_qkv(p, rope, q_norm, w_uq, kv_norm, w_ukv, q_gain, k_gain):
    B, L, _ = p.shape
    c_q = p[..., :MLA_Q_RANK]
    c_kv = p[..., MLA_Q_RANK:MLA_Q_RANK + MLA_KV_RANK]
    k_rope = p[..., MLA_Q_RANK + MLA_KV_RANK:]
    q = (rms_norm(c_q, q_norm) @ w_uq).reshape(B, L, MLA_HEADS, MLA_QK_DIM)
    kv = (rms_norm(c_kv, kv_norm) @ w_ukv).reshape(B, L, MLA_HEADS, MLA_NOPE_DIM + MLA_V_DIM)
    k = jnp.concatenate([kv[..., :MLA_NOPE_DIM],
                         jnp.broadcast_to(k_rope[:, :, None, :], (B, L, MLA_HEADS, MLA_ROPE_DIM))], axis=-1)
    v = kv[..., MLA_NOPE_DIM:]
    q = rms_norm(q, q_gain)
    k = rms_norm(k, k_gain)
    if rope is not None:
        q = jnp.concatenate([q[..., :MLA_NOPE_DIM], apply_axial_rope(q[..., MLA_NOPE_DIM:], rope)], axis=-1)
        k = jnp.concatenate([k[..., :MLA_NOPE_DIM], apply_axial_rope(k[..., MLA_NOPE_DIM:], rope)], axis=-1)
    return q, k, v


def softmax_attend(q, k, v):
    s = jnp.einsum('bqhd,bkhd->bhqk', q, k, preferred_element_type=jnp.float32) * (MLA_QK_DIM ** -0.5)
    pr = jax.nn.softmax(s, axis=-1)
    return jnp.einsum('bhqk,bkhd->bqhd', pr.astype(v.dtype), v)


def mla_mixer(p_lat, p_ctx, rope, q_norm, w_uq, kv_norm, w_ukv, q_gain, k_gain, need_ctx):
    prm = (q_norm, w_uq, kv_norm, w_ukv, q_gain, k_gain)
    q_l, k_l, v_l = mla_qkv(p_lat, rope, *prm)
    q_c, k_c, v_c = mla_qkv(p_ctx, None, *prm)
    k_all = jnp.concatenate([k_l, k_c], axis=1)
    v_all = jnp.concatenate([v_l, v_c], axis=1)
    B, L = p_lat.shape[:2]
    q_blocks = jnp.moveaxis(q_l.reshape(B, L // Q_BLOCK, Q_BLOCK, MLA_HEADS, MLA_QK_DIM), 1, 0)
    y_l = lax.map(lambda qb: softmax_attend(qb, k_all, v_all), q_blocks)
    y_l = jnp.moveaxis(y_l, 0, 1).reshape(B, L, MLA_WIDTH)
    y_c = softmax_attend(q_c, k_c, v_c).reshape(B, p_ctx.shape[1], MLA_WIDTH) if need_ctx else None
    return y_l, y_c


def hyena_filters(L, w1, b1, freq1, w2, b2, freq2, w3, b3):
    tn = jnp.arange(L, dtype=jnp.float32) / L
    bands = jnp.arange(1, HY_POS_BANDS + 1, dtype=jnp.float32)
    ang = 2.0 * math.pi * tn[:, None] * bands[None, :]
    z = jnp.concatenate([tn[:, None], jnp.cos(ang), jnp.sin(ang)], axis=-1)
    h = jnp.sin(freq1 * (z @ w1 + b1))
    h = jnp.sin(freq2 * (h @ w2 + b2))
    h = (h @ w3 + b3).reshape(L, HY_ORDER, 2, HY_WIDTH)
    rates = jnp.abs(jnp.linspace(math.log(HY_DECAY_TARGET) / HY_LONG_DECAY_PCT,
                                 math.log(HY_DECAY_TARGET) / HY_SHORT_DECAY_PCT, HY_WIDTH))
    h = h * jnp.exp(-tn[:, None] * rates[None, :])[:, None, None, :]
    zero = jnp.zeros((1, HY_ORDER, HY_WIDTH), h.dtype)
    h_full = jnp.concatenate([h[:, :, 0], zero, h[:0:-1, :, 1]], axis=0)
    return h_full * lax.rsqrt(jnp.sum(jnp.square(h_full), axis=0, keepdims=True))


def fft_long_conv(u, h_full, bias):
    L = u.shape[1]
    uf = jnp.fft.rfft(u.astype(jnp.float32), n=2 * L, axis=1)
    hf = jnp.fft.rfft(h_full, n=2 * L, axis=0)
    y = jnp.fft.irfft(uf * hf[None], n=2 * L, axis=1)[:, :L]
    return (y + u.astype(jnp.float32) * bias.astype(jnp.float32)).astype(u.dtype)


def hyena_mixer(p, conv_w, w1, b1, freq1, w2, b2, freq2, w3, b3, bias):
    L = p.shape[1]
    z = short_conv(p, conv_w)
    gates = (z[..., :HY_WIDTH], z[..., HY_WIDTH:2 * HY_WIDTH])
    y = z[..., 2 * HY_WIDTH:]
    h_full = hyena_filters(L, w1, b1, freq1, w2, b2, freq2, w3, b3)
    for o in range(HY_ORDER):
        y = gates[o] * fft_long_conv(y, h_full[:, o], bias[o])
    return y


def peer_ffn(h, w_q, sub_keys, exp_u, exp_v):
    B, L, D = h.shape
    chunks = h.reshape(B * L // PEER_CHUNK, PEER_CHUNK, D)

    def retrieve(xc):
        q = (xc @ w_q).reshape(PEER_CHUNK, PEER_HEADS, 2, PEER_HALF)
        s = jnp.einsum('thpd,hpnd->thpn', q, sub_keys, preferred_element_type=jnp.float32)
        sv, si = lax.top_k(s, PEER_TOPK)
        cand_s = (sv[:, :, 0, :, None] + sv[:, :, 1, None, :]).reshape(PEER_CHUNK, PEER_HEADS, PEER_TOPK * PEER_TOPK)
        cand_i = (si[:, :, 0, :, None] * PEER_N_KEYS + si[:, :, 1, None, :]).reshape(PEER_CHUNK, PEER_HEADS, PEER_TOPK * PEER_TOPK)
        top_s, top_j = lax.top_k(cand_s, PEER_TOPK)
        e_idx = jnp.take_along_axis(cand_i, top_j, axis=-1)
        gate = jax.nn.softmax(top_s, axis=-1)
        act = jax.nn.gelu(jnp.einsum('thkd,td->thk', exp_u[e_idx], xc).astype(jnp.float32), approximate=False)
        return jnp.einsum('thk,thkd->td', (gate * act).astype(xc.dtype), exp_v[e_idx])

    return lax.map(retrieve, chunks).reshape(B, L, D)


def setup_inputs(seed: int = 0) -> dict:
    key = jax.random.key(seed)
    keys = iter(jax.random.split(key, 64))

    def nrm(shape, scale):
        return scale * jax.random.normal(next(keys), shape, jnp.float32)

    def gain(shape):
        return 1.0 + nrm(shape, 0.02)

    centre = jnp.array([0.0, 1.0, 0.0], jnp.float32)[None, :, None]
    D = D_MODEL
    HF = HY_FILTER_HIDDEN
    return {
        'x': nrm((BATCH, SEQ, D), 1.0),
        'c': nrm((BATCH, D), 1.0),
        'ctx': nrm((BATCH, CTX_LEN, D), 1.0),
        'c_ctx': nrm((D,), 1.0),
        'mod_w': nrm((DEPTH, D, N_MOD * D), 0.5 * D ** -0.5),
        'mod_b': nrm((DEPTH, N_MOD * D), 0.02),
        'mix_norm': gain((DEPTH, D)),
        'w_in': nrm((DEPTH, D, IN_PROJ), D ** -0.5),
        'w_out': nrm((DEPTH, MIX_WIDTH, D), MIX_WIDTH ** -0.5),
        'rw_conv': centre + nrm((DEPTH, SHORT_CONV, RW_PROJ), 0.2),
        'rw_decay_up': nrm((DEPTH, 2, RW_DECAY_RANK, RW_WIDTH), 0.1),
        'rw_decay0': nrm((DEPTH, 2, RW_WIDTH), 0.5),
        'rw_a_up': nrm((DEPTH, 2, RW_A_RANK, RW_WIDTH), 0.5 * RW_A_RANK ** -0.5),
        'rw_a0': nrm((DEPTH, 2, RW_WIDTH), 0.5),
        'rw_gate_up': nrm((DEPTH, RW_GATE_RANK, RW_WIDTH), RW_GATE_RANK ** -0.5),
        'rw_k_k': 1.0 + nrm((DEPTH, RW_WIDTH), 0.1),
        'rw_k_a': 1.0 + nrm((DEPTH, RW_WIDTH), 0.1),
        'rw_r_k': nrm((DEPTH, RW_WIDTH), 0.1),
        'rw_gn_g': gain((DEPTH, RW_WIDTH)),
        'rw_gn_b': nrm((DEPTH, RW_WIDTH), 0.02),
        'mla_q_norm': gain((DEPTH, MLA_Q_RANK)),
        'mla_w_uq': nrm((DEPTH, MLA_Q_RANK, MLA_HEADS * MLA_QK_DIM), MLA_Q_RANK ** -0.5),
        'mla_kv_norm': gain((DEPTH, MLA_KV_RANK)),
        'mla_w_ukv': nrm((DEPTH, MLA_KV_RANK, MLA_HEADS * (MLA_NOPE_DIM + MLA_V_DIM)), MLA_KV_RANK ** -0.5),
        'mla_q_gain': gain((DEPTH, MLA_QK_DIM)),
        'mla_k_gain': gain((DEPTH, MLA_QK_DIM)),
        'hy_conv': centre + nrm((DEPTH, SHORT_CONV, HY_PROJ), 0.2),
        'hy_w1': nrm((DEPTH, HY_POS_DIM, HF), 1.0),
        'hy_b1': nrm((DEPTH, HF), 0.1),
        'hy_freq1': 1.0 + nrm((DEPTH, HF), 0.1),
        'hy_w2': nrm((DEPTH, HF, HF), HF ** -0.5),
        'hy_b2': nrm((DEPTH, HF), 0.1),
        'hy_freq2': 1.0 + nrm((DEPTH, HF), 0.1),
        'hy_w3': nrm((DEPTH, HF, HY_ORDER * 2 * HY_WIDTH), HF ** -0.5),
        'hy_b3': nrm((DEPTH, HY_ORDER * 2 * HY_WIDTH), 0.02),
        'hy_bias': nrm((DEPTH, HY_ORDER, HY_WIDTH), 0.1),
        'ffn_norm': gain((DEPTH, D)),
        'peer_wq': nrm((DEPTH, D, PEER_HEADS * PEER_QUERY_DIM), D ** -0.5),
        'peer_keys': nrm((DEPTH, PEER_HEADS, 2, PEER_N_KEYS, PEER_HALF), PEER_HALF ** -0.5),
        'peer_u': nrm((DEPTH, PEER_N_EXPERTS, D), D ** -0.5),
        'peer_v': nrm((DEPTH, PEER_N_EXPERTS, D), 0.5),
    }


def reference(x, c, ctx, c_ctx, mod_w, mod_b, mix_norm, w_in, w_out,
              rw_conv, rw_decay_up, rw_decay0, rw_a_up, rw_a0, rw_gate_up, rw_k_k, rw_k_a, rw_r_k, rw_gn_g, rw_gn_b,
              mla_q_norm, mla_w_uq, mla_kv_norm, mla_w_ukv, mla_q_gain, mla_k_gain,
              hy_conv, hy_w1, hy_b1, hy_freq1, hy_w2, hy_b2, hy_freq2, hy_w3, hy_b3, hy_bias,
              ffn_norm, peer_wq, peer_keys, peer_u, peer_v):
    L = x.shape[1]
    rope = axial_rope_tables(L)
    s_rw, s_mla = RW_PROJ, RW_PROJ + MLA_PROJ
    for li in range(DEPTH):
        need_ctx = li < DEPTH - 1
        mod_l = (jax.nn.silu(c) @ mod_w[li] + mod_b[li])[:, None, :]
        mod_c = (jax.nn.silu(c_ctx) @ mod_w[li] + mod_b[li])[None, None, :]
        shm_l, scm_l, gm_l, shf_l, scf_l, gf_l = jnp.split(mod_l, N_MOD, axis=-1)
        shm_c, scm_c, gm_c, shf_c, scf_c, gf_c = jnp.split(mod_c, N_MOD, axis=-1)

        p_l = (rms_norm(x, mix_norm[li]) * (1.0 + scm_l) + shm_l) @ w_in[li]
        p_c = (rms_norm(ctx, mix_norm[li]) * (1.0 + scm_c) + shm_c) @ w_in[li]
        rw_l, rw_c = rwkv7_mixer(p_l[..., :s_rw], p_c[..., :s_rw], rw_conv[li], rw_decay_up[li], rw_decay0[li],
                                 rw_a_up[li], rw_a0[li], rw_gate_up[li], rw_k_k[li], rw_k_a[li], rw_r_k[li],
                                 rw_gn_g[li], rw_gn_b[li], need_ctx)
        ml_l, ml_c = mla_mixer(p_l[..., s_rw:s_mla], p_c[..., s_rw:s_mla], rope, mla_q_norm[li], mla_w_uq[li],
                               mla_kv_norm[li], mla_w_ukv[li], mla_q_gain[li], mla_k_gain[li], need_ctx)
        hy_prm = (hy_conv[li], hy_w1[li], hy_b1[li], hy_freq1[li], hy_w2[li], hy_b2[li], hy_freq2[li],
                  hy_w3[li], hy_b3[li], hy_bias[li])
        hy_l = hyena_mixer(p_l[..., s_mla:], *hy_prm)
        x = x + gm_l * (jnp.concatenate([rw_l, ml_l, hy_l], axis=-1) @ w_out[li])

        x = x + gf_l * peer_ffn(rms_norm(x, ffn_norm[li]) * (1.0 + scf_l) + shf_l,
                                peer_wq[li], peer_keys[li], peer_u[li], peer_v[li])

        if need_ctx:
            hy_c = hyena_mixer(p_c[..., s_mla:], *hy_prm)
            ctx = ctx + gm_c * (jnp.concatenate([rw_c, ml_c, hy_c], axis=-1) @ w_out[li])
            ctx = ctx + gf_c * peer_ffn(rms_norm(ctx, ffn_norm[li]) * (1.0 + scf_c) + shf_c,
                                        peer_wq[li], peer_keys[li], peer_u[li], peer_v[li])
    return x
```

```python
import functools
import math

import jax
import jax.numpy as jnp
from jax import lax
from jax.experimental import pallas as pl
from jax.experimental.pallas import tpu as pltpu

D_MODEL = 1024
DEPTH = 2
GRID_W = 64
N_MOD = 6
NORM_EPS = 1e-6

RW_HEADS = 6
RW_HEAD_DIM = 64
RW_WIDTH = RW_HEADS * RW_HEAD_DIM
RW_DECAY_RANK = 64
RW_A_RANK = 64
RW_GATE_RANK = 128
RW_DECAY_SCALE = 0.6065306597
RW_GN_EPS = 64e-5
L2_EPS = 1e-12

MLA_HEADS = 6
MLA_Q_RANK = 256
MLA_KV_RANK = 128
MLA_NOPE_DIM = 64
MLA_ROPE_DIM = 32
MLA_V_DIM = 64
MLA_QK_DIM = MLA_NOPE_DIM + MLA_ROPE_DIM
MLA_WIDTH = MLA_HEADS * MLA_V_DIM
AXIS_ROPE_DIM = MLA_ROPE_DIM // 2
ROPE_THETA = 10000.0
Q_BLOCK = 128

HY_WIDTH = 256
HY_ORDER = 2
HY_POS_BANDS = 16
HY_SHORT_DECAY_PCT = 0.3
HY_LONG_DECAY_PCT = 1.5
HY_DECAY_TARGET = 1e-2

PEER_HEADS = 8
PEER_N_KEYS = 128
PEER_TOPK = 16
PEER_QUERY_DIM = 256
PEER_HALF = PEER_QUERY_DIM // 2
PEER_CHUNK = 128

RW_PROJ = 3 * RW_WIDTH + RW_DECAY_RANK + RW_A_RANK + RW_GATE_RANK
MLA_PROJ = MLA_Q_RANK + MLA_KV_RANK + MLA_ROPE_DIM
HY_PROJ = (HY_ORDER + 1) * HY_WIDTH
IN_PROJ = RW_PROJ + MLA_PROJ + HY_PROJ
MIX_WIDTH = RW_WIDTH + MLA_WIDTH + HY_WIDTH

VMEM_LIMIT_BYTES = 48 * 1024 * 1024


def _norm_mod_proj_kernel(x_ref, gain_ref, scale_ref, shift_ref, w_ref, o_ref):
    x = x_ref[0]
    y = x * lax.rsqrt(jnp.mean(x * x, axis=-1, keepdims=True) + NORM_EPS)
    y = y * gain_ref[...]
    y = y * (1.0 + scale_ref[0]) + shift_ref[0]
    o_ref[0] = jnp.dot(y.astype(jnp.bfloat16), w_ref[...], preferred_element_type=jnp.float32)


def norm_mod_proj(x, gain, scale, shift, w, block_rows):
    B, L, D = x.shape
    N = w.shape[1]
    return pl.pallas_call(
        _norm_mod_proj_kernel,
        grid=(B, L // block_rows),
        in_specs=[
            pl.BlockSpec((1, block_rows, D), lambda b, i: (b, i, 0)),
            pl.BlockSpec((1, D), lambda b, i: (0, 0)),
            pl.BlockSpec((1, 1, D), lambda b, i: (b, 0, 0)),
            pl.BlockSpec((1, 1, D), lambda b, i: (b, 0, 0)),
            pl.BlockSpec((D, N), lambda b, i: (0, 0)),
        ],
        out_specs=pl.BlockSpec((1, block_rows, N), lambda b, i: (b, i, 0)),
        out_shape=jax.ShapeDtypeStruct((B, L, N), jnp.float32),
        compiler_params=pltpu.CompilerParams(
            dimension_semantics=("parallel", "parallel"), vmem_limit_bytes=VMEM_LIMIT_BYTES),
        name="norm_mod_proj",
    )(x, gain.reshape(1, D), scale, shift, w.astype(jnp.bfloat16))


def rms_norm(x, gain):
    y = x * lax.rsqrt(jnp.mean(x * x, axis=-1, keepdims=True) + NORM_EPS)
    return y * gain


def short_conv(x, w):
    xp = jnp.pad(x, ((0, 0), (1, 1), (0, 0)))
    return xp[:, :-2] * w[0] + xp[:, 1:-1] * w[1] + xp[:, 2:] * w[2]


def _heads(t):
    return t.reshape(t.shape[:-1] + (RW_HEADS, RW_HEAD_DIM))


def rwkv7_prepare(p, conv_w, decay_up, decay0, a_up, a0, gate_up, k_k, k_a):
    z = short_conv(p, conv_w)
    o1, o2, o3 = RW_WIDTH, 2 * RW_WIDTH, 3 * RW_WIDTH
    o4 = o3 + RW_DECAY_RANK
    o5 = o4 + RW_A_RANK
    r, k, v = z[..., :o1], z[..., o1:o2], z[..., o2:o3]
    d_lo, a_lo, g_lo = z[..., o3:o4], z[..., o4:o5], z[..., o5:]
    decay = jnp.exp(-RW_DECAY_SCALE * jax.nn.sigmoid(
        decay0[:, None, None, :] + jnp.einsum('blr,nrc->nblc', jnp.tanh(d_lo), decay_up)))
    a = jax.nn.sigmoid(a0[:, None, None, :] + jnp.einsum('blr,nrc->nblc', a_lo, a_up))
    g = jnp.einsum('blr,rc->blc', jax.nn.sigmoid(g_lo), gate_up)
    kk = _heads(k * k_k)
    kk = kk * lax.rsqrt(jnp.sum(kk * kk, axis=-1, keepdims=True) + L2_EPS)
    k_rep = _heads(k)[None] * (1.0 + (_heads(a) - 1.0) * _heads(k_a))
    return _heads(r), _heads(k), _heads(v), _heads(decay), _heads(a), kk, k_rep, g


def rwkv7_scan(state0, r, decay, kk, a, k_rep, v, reverse):
    def step(S, inp):
        r_t, w_t, kk_t, a_t, k_t, v_t = inp
        S = (S * w_t[:, :, None, :]
             - jnp.einsum('bhvk,bhk->bhv', S, kk_t)[..., None] * (kk_t * a_t)[:, :, None, :]
             + v_t[..., None] * k_t[:, :, None, :])
        return S, jnp.einsum('bhvk,bhk->bhv', S, r_t)
    seq = tuple(jnp.moveaxis(t, 1, 0) for t in (r, decay, kk, a, k_rep, v))
    s_fin, y = lax.scan(step, state0, seq, reverse=reverse)
    return s_fin, jnp.moveaxis(y, 0, 1)


def rwkv7_readout(y, r, k, v, g, r_k, gn_g, gn_b):
    B, L = y.shape[:2]
    mu = jnp.mean(y, axis=-1, keepdims=True)
    var = jnp.mean(jnp.square(y - mu), axis=-1, keepdims=True)
    yn = ((y - mu) * lax.rsqrt(var + RW_GN_EPS)).reshape(B, L, RW_WIDTH) * gn_g + gn_b
    bonus = jnp.sum(r * k * r_k.reshape(RW_HEADS, RW_HEAD_DIM), axis=-1, keepdims=True) * v
    return (yn + bonus.reshape(B, L, RW_WIDTH)) * g


def rwkv7_mixer(p_lat, p_ctx, conv_w, decay_up, decay0, a_up, a0, gate_up, k_k, k_a, r_k, gn_g, gn_b, need_ctx):
    prm = (conv_w, decay_up, decay0, a_up, a0, gate_up, k_k, k_a)
    lr, lk, lv, ldec, la, lkk, lkr, lg = rwkv7_prepare(p_lat, *prm)
    cr, ck, cv, cdec, ca, ckk, ckr, cg = rwkv7_prepare(p_ctx, *prm)
    B = p_lat.shape[0]
    ys_l, ys_c = [], []
    for d, rev in enumerate((False, True)):
        s0 = jnp.zeros((B, RW_HEADS, RW_HEAD_DIM, RW_HEAD_DIM), jnp.float32)
        s_ctx, yc = rwkv7_scan(s0, cr, cdec[d], ckk, ca[d], ckr[d], cv, rev)
        _, yl = rwkv7_scan(s_ctx, lr, ldec[d], lkk, la[d], lkr[d], lv, rev)
        ys_l.append(yl)
        ys_c.append(yc)
    out_l = rwkv7_readout(ys_l[0] + ys_l[1], lr, lk, lv, lg, r_k, gn_g, gn_b)
    out_c = rwkv7_readout(ys_c[0] + ys_c[1], cr, ck, cv, cg, r_k, gn_g, gn_b) if need_ctx else None
    return out_l, out_c


def axial_rope_tables(L):
    rows = L // GRID_W
    row, col = jnp.meshgrid(jnp.arange(rows), jnp.arange(GRID_W), indexing='ij')
    inv = ROPE_THETA ** (-jnp.arange(0, AXIS_ROPE_DIM, 2, dtype=jnp.float32) / AXIS_ROPE_DIM)
    pos = jnp.stack([row.reshape(-1), col.reshape(-1)], axis=-1).astype(jnp.float32)
    ang = pos[:, :, None] * inv[None, None, :]
    return jnp.cos(ang), jnp.sin(ang)


def apply_axial_rope(x, rope):
    cos, sin = rope
    B, L, H, _ = x.shape
    xa = x.reshape(B, L, H, 2, AXIS_ROPE_DIM)
    half = AXIS_ROPE_DIM // 2
    x1, x2 = xa[..., :half], xa[..., half:]
    cs, sn = cos[None, :, None], sin[None, :, None]
    out = jnp.concatenate([x1 * cs - x2 * sn, x2 * cs + x1 * sn], axis=-1)
    return out.reshape(B, L, H, MLA_ROPE_DIM)


def mla_qkv(p, rope, q_norm, w_uq, kv_norm, w_ukv, q_gain, k_gain):
    B, L, _ = p.shape
    c_q = p[..., :MLA_Q_RANK]
    c_kv = p[..., MLA_Q_RANK:MLA_Q_RANK + MLA_KV_RANK]
    k_rope = p[..., MLA_Q_RANK + MLA_KV_RANK:]
    q = (rms_norm(c_q, q_norm) @ w_uq).reshape(B, L, MLA_HEADS, MLA_QK_DIM)
    kv = (rms_norm(c_kv, kv_norm) @ w_ukv).reshape(B, L, MLA_HEADS, MLA_NOPE_DIM + MLA_V_DIM)
    k = jnp.concatenate([kv[..., :MLA_NOPE_DIM],
                         jnp.broadcast_to(k_rope[:, :, None, :], (B, L, MLA_HEADS, MLA_ROPE_DIM))], axis=-1)
    v = kv[..., MLA_NOPE_DIM:]
    q = rms_norm(q, q_gain)
    k = rms_norm(k, k_gain)
    if rope is not None:
        q = jnp.concatenate([q[..., :MLA_NOPE_DIM], apply_axial_rope(q[..., MLA_NOPE_DIM:], rope)], axis=-1)
        k = jnp.concatenate([k[..., :MLA_NOPE_DIM], apply_axial_rope(k[..., MLA_NOPE_DIM:], rope)], axis=-1)
    return q, k, v


def softmax_attend(q, k, v):
    s = jnp.einsum('bqhd,bkhd->bhqk', q, k, preferred_element_type=jnp.float32) * (MLA_QK_DIM ** -0.5)
    pr = jax.nn.softmax(s, axis=-1)
    return jnp.einsum('bhqk,bkhd->bqhd', pr, v)


def mla_mixer(p_lat, p_ctx, rope, q_norm, w_uq, kv_norm, w_ukv, q_gain, k_gain, need_ctx):
    prm = (q_norm, w_uq, kv_norm, w_ukv, q_gain, k_gain)
    q_l, k_l, v_l = mla_qkv(p_lat, rope, *prm)
    q_c, k_c, v_c = mla_qkv(p_ctx, None, *prm)
    k_all = jnp.concatenate([k_l, k_c], axis=1)
    v_all = jnp.concatenate([v_l, v_c], axis=1)
    B, L = p_lat.shape[:2]
    q_blocks = jnp.moveaxis(q_l.reshape(B, L // Q_BLOCK, Q_BLOCK, MLA_HEADS, MLA_QK_DIM), 1, 0)
    y_l = lax.map(lambda qb: softmax_attend(qb, k_all, v_all), q_blocks)
    y_l = jnp.moveaxis(y_l, 0, 1).reshape(B, L, MLA_WIDTH)
    y_c = softmax_attend(q_c, k_c, v_c).reshape(B, p_ctx.shape[1], MLA_WIDTH) if need_ctx else None
    return y_l, y_c


def hyena_filters(L, w1, b1, freq1, w2, b2, freq2, w3, b3):
    tn = jnp.arange(L, dtype=jnp.float32) / L
    bands = jnp.arange(1, HY_POS_BANDS + 1, dtype=jnp.float32)
    ang = 2.0 * math.pi * tn[:, None] * bands[None, :]
    z = jnp.concatenate([tn[:, None], jnp.cos(ang), jnp.sin(ang)], axis=-1)
    h = jnp.sin(freq1 * (z @ w1 + b1))
    h = jnp.sin(freq2 * (h @ w2 + b2))
    h = (h @ w3 + b3).reshape(L, HY_ORDER, 2, HY_WIDTH)
    rates = jnp.abs(jnp.linspace(math.log(HY_DECAY_TARGET) / HY_LONG_DECAY_PCT,
                                 math.log(HY_DECAY_TARGET) / HY_SHORT_DECAY_PCT, HY_WIDTH))
    h = h * jnp.exp(-tn[:, None] * rates[None, :])[:, None, None, :]
    zero = jnp.zeros((1, HY_ORDER, HY_WIDTH), h.dtype)
    h_full = jnp.concatenate([h[:, :, 0], zero, h[:0:-1, :, 1]], axis=0)
    return h_full * lax.rsqrt(jnp.sum(jnp.square(h_full), axis=0, keepdims=True))


def fft_long_conv(u, h_full, bias):
    L = u.shape[1]
    uf = jnp.fft.rfft(u, n=2 * L, axis=1)
    hf = jnp.fft.rfft(h_full, n=2 * L, axis=0)
    y = jnp.fft.irfft(uf * hf[None], n=2 * L, axis=1)[:, :L]
    return y + u * bias


def hyena_mixer(p, conv_w, w1, b1, freq1, w2, b2, freq2, w3, b3, bias):
    L = p.shape[1]
    z = short_conv(p, conv_w)
    gates = (z[..., :HY_WIDTH], z[..., HY_WIDTH:2 * HY_WIDTH])
    y = z[..., 2 * HY_WIDTH:]
    h_full = hyena_filters(L, w1, b1, freq1, w2, b2, freq2, w3, b3)
    for o in range(HY_ORDER):
        y = gates[o] * fft_long_conv(y, h_full[:, o], bias[o])
    return y


def peer_ffn(h, w_q, sub_keys, exp_u, exp_v):
    B, L, D = h.shape
    chunks = h.reshape(B * L // PEER_CHUNK, PEER_CHUNK, D)

    def retrieve(xc):
        q = (xc @ w_q).reshape(PEER_CHUNK, PEER_HEADS, 2, PEER_HALF)
        s = jnp.einsum('thpd,hpnd->thpn', q, sub_keys, preferred_element_type=jnp.float32)
        sv, si = lax.top_k(s, PEER_TOPK)
        cand_s = (sv[:, :, 0, :, None] + sv[:, :, 1, None, :]).reshape(PEER_CHUNK, PEER_HEADS, PEER_TOPK * PEER_TOPK)
        cand_i = (si[:, :, 0, :, None] * PEER_N_KEYS + si[:, :, 1, None, :]).reshape(PEER_CHUNK, PEER_HEADS, PEER_TOPK * PEER_TOPK)
        top_s, top_j = lax.top_k(cand_s, PEER_TOPK)
        e_idx = jnp.take_along_axis(cand_i, top_j, axis=-1)
        gate = jax.nn.softmax(top_s, axis=-1)
        act = jax.nn.gelu(jnp.einsum('thkd,td->thk', exp_u[e_idx], xc), approximate=False)
        return jnp.einsum('thk,thkd->td', gate * act, exp_v[e_idx])

    return lax.map(retrieve, chunks).reshape(B, L, D)


def kernel(x, c, ctx, c_ctx, mod_w, mod_b, mix_norm, w_in, w_out, rw_conv, rw_decay_up, rw_decay0, rw_a_up, rw_a0, rw_gate_up, rw_k_k, rw_k_a, rw_r_k, rw_gn_g, rw_gn_b, mla_q_norm, mla_w_uq, mla_kv_norm, mla_w_ukv, mla_q_gain, mla_k_gain, hy_conv, hy_w1, hy_b1, hy_freq1, hy_w2, hy_b2, hy_freq2, hy_w3, hy_b3, hy_bias, ffn_norm, peer_wq, peer_keys, peer_u, peer_v):
    B, L, D = x.shape
    rope = axial_rope_tables(L)
    s_rw, s_mla = RW_PROJ, RW_PROJ + MLA_PROJ
    for li in range(DEPTH):
        need_ctx = li < DEPTH - 1
        mod_l = (jax.nn.silu(c) @ mod_w[li] + mod_b[li])[:, None, :]
        mod_c = (jax.nn.silu(c_ctx) @ mod_w[li] + mod_b[li])[None, None, :]
        shm_l, scm_l, gm_l, shf_l, scf_l, gf_l = jnp.split(mod_l, N_MOD, axis=-1)
        shm_c, scm_c, gm_c, shf_c, scf_c, gf_c = jnp.split(mod_c, N_MOD, axis=-1)

        p_l = norm_mod_proj(x, mix_norm[li], scm_l, shm_l, w_in[li], 512)
        p_c = norm_mod_proj(ctx, mix_norm[li], jnp.broadcast_to(scm_c, (B, 1, D)),
                            jnp.broadcast_to(shm_c, (B, 1, D)), w_in[li], 256)
        rw_l, rw_c = rwkv7_mixer(p_l[..., :s_rw], p_c[..., :s_rw], rw_conv[li], rw_decay_up[li], rw_decay0[li],
                                 rw_a_up[li], rw_a0[li], rw_gate_up[li], rw_k_k[li], rw_k_a[li], rw_r_k[li],
                                 rw_gn_g[li], rw_gn_b[li], need_ctx)
        ml_l, ml_c = mla_mixer(p_l[..., s_rw:s_mla], p_c[..., s_rw:s_mla], rope, mla_q_norm[li], mla_w_uq[li],
                               mla_kv_norm[li], mla_w_ukv[li], mla_q_gain[li], mla_k_gain[li], need_ctx)
        hy_prm = (hy_conv[li], hy_w1[li], hy_b1[li], hy_freq1[li], hy_w2[li], hy_b2[li], hy_freq2[li],
                  hy_w3[li], hy_b3[li], hy_bias[li])
        hy_l = hyena_mixer(p_l[..., s_mla:], *hy_prm)
        x = x + gm_l * (jnp.concatenate([rw_l, ml_l, hy_l], axis=-1) @ w_out[li])
        x = x + gf_l * peer_ffn(rms_norm(x, ffn_norm[li]) * (1.0 + scf_l) + shf_l,
                                peer_wq[li], peer_keys[li], peer_u[li], peer_v[li])
        if need_ctx:
            hy_c = hyena_mixer(p_c[..., s_mla:], *hy_prm)
            ctx = ctx + gm_c * (jnp.concatenate([rw_c, ml_c, hy_c], axis=-1) @ w_out[li])
            ctx = ctx + gf_c * peer_ffn(rms_norm(ctx, ffn_norm[li]) * (1.0 + scf_c) + shf_c,
                                        peer_wq[li], peer_keys[li], peer_u[li], peer_v[li])
    return x
```

```python
import functools
import math

import jax
import jax.numpy as jnp
from jax import lax
from jax.experimental import pallas as pl
from jax.experimental.pallas import tpu as pltpu
from jax.experimental.pallas import tpu_sc as plsc

D_MODEL = 1024
DEPTH = 2
GRID_W = 64
N_MOD = 6
NORM_EPS = 1e-6

RW_HEADS = 6
RW_HEAD_DIM = 64
RW_WIDTH = RW_HEADS * RW_HEAD_DIM
RW_DECAY_RANK = 64
RW_A_RANK = 64
RW_GATE_RANK = 128
RW_DECAY_SCALE = 0.6065306597
RW_GN_EPS = 64e-5
L2_EPS = 1e-12

MLA_HEADS = 6
MLA_Q_RANK = 256
MLA_KV_RANK = 128
MLA_NOPE_DIM = 64
MLA_ROPE_DIM = 32
MLA_V_DIM = 64
MLA_QK_DIM = MLA_NOPE_DIM + MLA_ROPE_DIM
MLA_WIDTH = MLA_HEADS * MLA_V_DIM
AXIS_ROPE_DIM = MLA_ROPE_DIM // 2
ROPE_THETA = 10000.0
Q_BLOCK = 128

HY_WIDTH = 256
HY_ORDER = 2
HY_POS_BANDS = 16
HY_SHORT_DECAY_PCT = 0.3
HY_LONG_DECAY_PCT = 1.5
HY_DECAY_TARGET = 1e-2

PEER_HEADS = 8
PEER_N_KEYS = 128
PEER_TOPK = 16
PEER_QUERY_DIM = 256
PEER_HALF = PEER_QUERY_DIM // 2

RW_PROJ = 3 * RW_WIDTH + RW_DECAY_RANK + RW_A_RANK + RW_GATE_RANK
MLA_PROJ = MLA_Q_RANK + MLA_KV_RANK + MLA_ROPE_DIM
HY_PROJ = (HY_ORDER + 1) * HY_WIDTH
IN_PROJ = RW_PROJ + MLA_PROJ + HY_PROJ
MIX_WIDTH = RW_WIDTH + MLA_WIDTH + HY_WIDTH

VMEM_LIMIT_BYTES = 48 * 1024 * 1024


def _norm_mod_proj_kernel(x_ref, gain_ref, scale_ref, shift_ref, w_ref, o_ref):
    x = x_ref[0]
    y = x * lax.rsqrt(jnp.mean(x * x, axis=-1, keepdims=True) + NORM_EPS)
    y = y * gain_ref[...]
    y = y * (1.0 + scale_ref[0]) + shift_ref[0]
    o_ref[0] = jnp.dot(y.astype(jnp.bfloat16), w_ref[...], preferred_element_type=jnp.float32)


def norm_mod_proj(x, gain, scale, shift, w, block_rows):
    B, L, D = x.shape
    N = w.shape[1]
    return pl.pallas_call(
        _norm_mod_proj_kernel,
        grid=(B, L // block_rows),
        in_specs=[
            pl.BlockSpec((1, block_rows, D), lambda b, i: (b, i, 0)),
            pl.BlockSpec((1, D), lambda b, i: (0, 0)),
            pl.BlockSpec((1, 1, D), lambda b, i: (b, 0, 0)),
            pl.BlockSpec((1, 1, D), lambda b, i: (b, 0, 0)),
            pl.BlockSpec((D, N), lambda b, i: (0, 0)),
        ],
        out_specs=pl.BlockSpec((1, block_rows, N), lambda b, i: (b, i, 0)),
        out_shape=jax.ShapeDtypeStruct((B, L, N), jnp.float32),
        compiler_params=pltpu.CompilerParams(
            dimension_semantics=("parallel", "parallel"), vmem_limit_bytes=VMEM_LIMIT_BYTES),
        name="norm_mod_proj",
    )(x, gain.reshape(1, D), scale, shift, w.astype(jnp.bfloat16))


RW_CHUNK = 64


def _rwkv_chunk_kernel(r_ref, kk_ref, v_ref, lw_ref, akk_ref, kr_ref, y_ref, h_ref):
    d = pl.program_id(0)
    n = pl.program_id(2)

    @pl.when(n == 0)
    def _():
        h_ref[...] = jnp.zeros_like(h_ref)

    C = RW_CHUNK
    row = lax.broadcasted_iota(jnp.int32, (C, C), 0)
    col = lax.broadcasted_iota(jnp.int32, (C, C), 1)
    lag = (row - col) * (1 - 2 * d)
    before = lag > 0
    upto = lag >= 0
    tri = upto.astype(jnp.float32)
    eye = (row == col).astype(jnp.float32)
    bf = jnp.bfloat16
    f32 = jnp.float32

    def mm(a, b):
        return jnp.dot(a.astype(bf), b.astype(bf), preferred_element_type=f32)

    def mm_nt(a, b):
        return lax.dot_general(a.astype(bf), b.astype(bf), (((1,), (1,)), ((), ())), preferred_element_type=f32)

    def mm_tn(a, b):
        return lax.dot_general(a.astype(bf), b.astype(bf), (((0,), (0,)), ((), ())), preferred_element_type=f32)

    for h in range(RW_HEADS):
        r = r_ref[0, h]
        kk = kk_ref[0, h]
        v = v_ref[0, h]
        lw = lw_ref[0, 0, h]
        akk = akk_ref[0, 0, h]
        kr = kr_ref[0, 0, h]
        G = jnp.dot(tri, lw, preferred_element_type=f32, precision=lax.Precision.HIGHEST)
        gtot = jnp.sum(lw, axis=0, keepdims=True)
        E = jnp.exp(G)
        Einv = jnp.exp(-G)
        At = -kk * jnp.exp(G - lw)
        Bt = akk * Einv
        Kt = kr * Einv
        Rt = r * E
        Ehat = jnp.exp(gtot - G)
        Bh = akk * Ehat
        Kh = kr * Ehat
        X = mm_nt(jnp.concatenate([At, Rt], axis=0), jnp.concatenate([Bt, Kt], axis=0))
        M_ab = jnp.where(before, X[:C, :C], 0.0)
        M_ak = jnp.where(before, X[:C, C:], 0.0)
        A_rb = jnp.where(upto, X[C:, :C], 0.0)
        A_rk = jnp.where(upto, X[C:, C:], 0.0)
        Mp = M_ab
        T = eye + Mp
        for _ in range(5):
            Mp = jnp.dot(Mp, Mp, preferred_element_type=f32)
            T = T + jnp.dot(T, Mp, preferred_element_type=f32)
        MV = mm(M_ak, v)
        WU = jnp.dot(T, jnp.concatenate([At, MV], axis=1), preferred_element_type=f32)
        Wt = WU[:, :RW_HEAD_DIM]
        Ut = WU[:, RW_HEAD_DIM:]
        H0 = h_ref[h]
        Om = Rt + mm(A_rb, Wt)
        Y0 = mm(A_rb, Ut) + mm(A_rk, v)
        y_ref[0, 0, h] = jnp.dot(Om, H0, preferred_element_type=f32) + Y0
        BW = mm_tn(Bh, WU)
        P = eye * jnp.exp(gtot) + BW[:, :RW_HEAD_DIM]
        Q = BW[:, RW_HEAD_DIM:] + mm_tn(Kh, v)
        h_ref[h] = jnp.dot(P, H0, preferred_element_type=f32) + Q


def rwkv_chunked(r, kk, v, lw, akk, kr, n_ctx):
    B, H, T, _ = r.shape
    nc = n_ctx // RW_CHUNK
    nt = T // RW_CHUNK

    def chunk_of(d, n):
        bwd = jnp.where(n < nc, nc - 1 - n, nt - 1 - (n - nc))
        return jnp.where(d == 0, n, bwd)

    spec1 = pl.BlockSpec((1, H, RW_CHUNK, RW_HEAD_DIM), lambda d, b, n: (b, 0, chunk_of(d, n), 0))
    spec2 = pl.BlockSpec((1, 1, H, RW_CHUNK, RW_HEAD_DIM), lambda d, b, n: (d, b, 0, chunk_of(d, n), 0))
    return pl.pallas_call(
        _rwkv_chunk_kernel,
        grid=(2, B, nt),
        in_specs=[spec1, spec1, spec1, spec2, spec2, spec2],
        out_specs=spec2,
        out_shape=jax.ShapeDtypeStruct((2, B, H, T, RW_HEAD_DIM), jnp.float32),
        scratch_shapes=[pltpu.VMEM((H, RW_HEAD_DIM, RW_HEAD_DIM), jnp.float32)],
        compiler_params=pltpu.CompilerParams(dimension_semantics=("parallel", "parallel", "arbitrary")),
        name="rwkv_chunked",
    )(r, kk, v, lw, akk, kr)


def rms_norm(x, gain):
    y = x * lax.rsqrt(jnp.mean(x * x, axis=-1, keepdims=True) + NORM_EPS)
    return y * gain


def short_conv(x, w):
    xp = jnp.pad(x, ((0, 0), (1, 1), (0, 0)))
    return xp[:, :-2] * w[0] + xp[:, 1:-1] * w[1] + xp[:, 2:] * w[2]


def _heads(t):
    return t.reshape(t.shape[:-1] + (RW_HEADS, RW_HEAD_DIM))


def rwkv7_prepare(p, conv_w, decay_up, decay0, a_up, a0, gate_up, k_k, k_a):
    z = short_conv(p, conv_w)
    o1, o2, o3 = RW_WIDTH, 2 * RW_WIDTH, 3 * RW_WIDTH
    o4 = o3 + RW_DECAY_RANK
    o5 = o4 + RW_A_RANK
    r, k, v = z[..., :o1], z[..., o1:o2], z[..., o2:o3]
    d_lo, a_lo, g_lo = z[..., o3:o4], z[..., o4:o5], z[..., o5:]
    log_decay = -RW_DECAY_SCALE * jax.nn.sigmoid(
        decay0[:, None, None, :] + jnp.einsum('blr,nrc->nblc', jnp.tanh(d_lo), decay_up))
    a = jax.nn.sigmoid(a0[:, None, None, :] + jnp.einsum('blr,nrc->nblc', a_lo, a_up))
    g = jnp.einsum('blr,rc->blc', jax.nn.sigmoid(g_lo), gate_up)
    kk = _heads(k * k_k)
    kk = kk * lax.rsqrt(jnp.sum(kk * kk, axis=-1, keepdims=True) + L2_EPS)
    k_rep = _heads(k)[None] * (1.0 + (_heads(a) - 1.0) * _heads(k_a))
    return _heads(r), _heads(k), _heads(v), _heads(log_decay), kk[None] * _heads(a), kk, k_rep, g


def rwkv7_readout(y, r, k, v, g, r_k, gn_g, gn_b):
    B, L = y.shape[:2]
    mu = jnp.mean(y, axis=-1, keepdims=True)
    var = jnp.mean(jnp.square(y - mu), axis=-1, keepdims=True)
    yn = ((y - mu) * lax.rsqrt(var + RW_GN_EPS)).reshape(B, L, RW_WIDTH) * gn_g + gn_b
    bonus = jnp.sum(r * k * r_k.reshape(RW_HEADS, RW_HEAD_DIM), axis=-1, keepdims=True) * v
    return (yn + bonus.reshape(B, L, RW_WIDTH)) * g


def rwkv7_mixer(p_lat, p_ctx, conv_w, decay_up, decay0, a_up, a0, gate_up, k_k, k_a, r_k, gn_g, gn_b, need_ctx):
    prm = (conv_w, decay_up, decay0, a_up, a0, gate_up, k_k, k_a)
    lat = rwkv7_prepare(p_lat, *prm)
    ctx = rwkv7_prepare(p_ctx, *prm)
    n_ctx = p_ctx.shape[1]

    def seq(i):
        t = jnp.concatenate([ctx[i], lat[i]], axis=-3)
        return jnp.swapaxes(t, -3, -2)

    y = rwkv_chunked(seq(0), seq(5), seq(2), seq(3), seq(4), seq(6), n_ctx)
    y = jnp.swapaxes(y[0] + y[1], 1, 2)
    lr, lk, lv, _, _, _, _, lg = lat
    cr, ck, cv, _, _, _, _, cg = ctx
    out_l = rwkv7_readout(y[:, n_ctx:], lr, lk, lv, lg, r_k, gn_g, gn_b)
    out_c = rwkv7_readout(y[:, :n_ctx], cr, ck, cv, cg, r_k, gn_g, gn_b) if need_ctx else None
    return out_l, out_c


def axial_rope_tables(L):
    rows = L // GRID_W
    row, col = jnp.meshgrid(jnp.arange(rows), jnp.arange(GRID_W), indexing='ij')
    inv = ROPE_THETA ** (-jnp.arange(0, AXIS_ROPE_DIM, 2, dtype=jnp.float32) / AXIS_ROPE_DIM)
    pos = jnp.stack([row.reshape(-1), col.reshape(-1)], axis=-1).astype(jnp.float32)
    ang = pos[:, :, None] * inv[None, None, :]
    return jnp.cos(ang), jnp.sin(ang)


def apply_axial_rope(x, rope):
    cos, sin = rope
    B, L, H, _ = x.shape
    xa = x.reshape(B, L, H, 2, AXIS_ROPE_DIM)
    half = AXIS_ROPE_DIM // 2
    x1, x2 = xa[..., :half], xa[..., half:]
    cs, sn = cos[None, :, None], sin[None, :, None]
    out = jnp.concatenate([x1 * cs - x2 * sn, x2 * cs + x1 * sn], axis=-1)
    return out.reshape(B, L, H, MLA_ROPE_DIM)


def mla_qkv(p, rope, q_norm, w_uq, kv_norm, w_ukv, q_gain, k_gain):
    B, L, _ = p.shape
    c_q = p[..., :MLA_Q_RANK]
    c_kv = p[..., MLA_Q_RANK:MLA_Q_RANK + MLA_KV_RANK]
    k_rope = p[..., MLA_Q_RANK + MLA_KV_RANK:]
    q = (rms_norm(c_q, q_norm) @ w_uq).reshape(B, L, MLA_HEADS, MLA_QK_DIM)
    kv = (rms_norm(c_kv, kv_norm) @ w_ukv).reshape(B, L, MLA_HEADS, MLA_NOPE_DIM + MLA_V_DIM)
    k = jnp.concatenate([kv[..., :MLA_NOPE_DIM],
                         jnp.broadcast_to(k_rope[:, :, None, :], (B, L, MLA_HEADS, MLA_ROPE_DIM))], axis=-1)
    v = kv[..., MLA_NOPE_DIM:]
    q = rms_norm(q, q_gain)
    k = rms_norm(k, k_gain)
    if rope is not None:
        q = jnp.concatenate([q[..., :MLA_NOPE_DIM], apply_axial_rope(q[..., MLA_NOPE_DIM:], rope)], axis=-1)
        k = jnp.concatenate([k[..., :MLA_NOPE_DIM], apply_axial_rope(k[..., MLA_NOPE_DIM:], rope)], axis=-1)
    return q, k, v


def softmax_attend(q, k, v):
    s = jnp.einsum('bqhd,bkhd->bhqk', q, k, preferred_element_type=jnp.float32) * (MLA_QK_DIM ** -0.5)
    pr = jax.nn.softmax(s, axis=-1)
    return jnp.einsum('bhqk,bkhd->bqhd', pr, v)


def mla_mixer(p_lat, p_ctx, rope, q_norm, w_uq, kv_norm, w_ukv, q_gain, k_gain, need_ctx):
    prm = (q_norm, w_uq, kv_norm, w_ukv, q_gain, k_gain)
    q_l, k_l, v_l = mla_qkv(p_lat, rope, *prm)
    q_c, k_c, v_c = mla_qkv(p_ctx, None, *prm)
    k_all = jnp.concatenate([k_l, k_c], axis=1)
    v_all = jnp.concatenate([v_l, v_c], axis=1)
    B, L = p_lat.shape[:2]
    q_blocks = jnp.moveaxis(q_l.reshape(B, L // Q_BLOCK, Q_BLOCK, MLA_HEADS, MLA_QK_DIM), 1, 0)
    y_l = lax.map(lambda qb: softmax_attend(qb, k_all, v_all), q_blocks)
    y_l = jnp.moveaxis(y_l, 0, 1).reshape(B, L, MLA_WIDTH)
    y_c = softmax_attend(q_c, k_c, v_c).reshape(B, p_ctx.shape[1], MLA_WIDTH) if need_ctx else None
    return y_l, y_c


def hyena_filters(L, w1, b1, freq1, w2, b2, freq2, w3, b3):
    tn = jnp.arange(L, dtype=jnp.float32) / L
    bands = jnp.arange(1, HY_POS_BANDS + 1, dtype=jnp.float32)
    ang = 2.0 * math.pi * tn[:, None] * bands[None, :]
    z = jnp.concatenate([tn[:, None], jnp.cos(ang), jnp.sin(ang)], axis=-1)
    h = jnp.sin(freq1 * (z @ w1 + b1))
    h = jnp.sin(freq2 * (h @ w2 + b2))
    h = (h @ w3 + b3).reshape(L, HY_ORDER, 2, HY_WIDTH)
    rates = jnp.abs(jnp.linspace(math.log(HY_DECAY_TARGET) / HY_LONG_DECAY_PCT,
                                 math.log(HY_DECAY_TARGET) / HY_SHORT_DECAY_PCT, HY_WIDTH))
    h = h * jnp.exp(-tn[:, None] * rates[None, :])[:, None, None, :]
    zero = jnp.zeros((1, HY_ORDER, HY_WIDTH), h.dtype)
    h_full = jnp.concatenate([h[:, :, 0], zero, h[:0:-1, :, 1]], axis=0)
    return h_full * lax.rsqrt(jnp.sum(jnp.square(h_full), axis=0, keepdims=True))


def fft_long_conv(u, h_full, bias):
    L = u.shape[1]
    uf = jnp.fft.rfft(u, n=2 * L, axis=1)
    hf = jnp.fft.rfft(h_full, n=2 * L, axis=0)
    y = jnp.fft.irfft(uf * hf[None], n=2 * L, axis=1)[:, :L]
    return y + u * bias


def hyena_mixer(p, conv_w, w1, b1, freq1, w2, b2, freq2, w3, b3, bias):
    L = p.shape[1]
    z = short_conv(p, conv_w)
    gates = (z[..., :HY_WIDTH], z[..., HY_WIDTH:2 * HY_WIDTH])
    y = z[..., 2 * HY_WIDTH:]
    h_full = hyena_filters(L, w1, b1, freq1, w2, b2, freq2, w3, b3)
    for o in range(HY_ORDER):
        y = gates[o] * fft_long_conv(y, h_full[:, o], bias[o])
    return y


SC_CORES = 2
SC_SUBCORES = 16
SC_LANES = 16
SC_WORKERS = SC_CORES * SC_SUBCORES
PEER_SLOTS = PEER_HEADS * PEER_TOPK
PEER_GATHER_ROWS = 32
PEER_GATHERS = PEER_SLOTS // PEER_GATHER_ROWS
PEER_ACC_VREGS = 8


def _sc_peer(table, idx, aux, phase):
    N = idx.shape[0]
    tpw = N // SC_WORKERS
    assert tpw % 2 == 0 and N % SC_WORKERS == 0
    mesh = plsc.VectorSubcoreMesh(core_axis_name="c", subcore_axis_name="s")
    aux_shape = (D_MODEL,) if phase == "dot" else (PEER_SLOTS, SC_LANES)
    out_tok = (PEER_SLOTS, SC_LANES) if phase == "dot" else (D_MODEL,)

    @functools.partial(
        pl.kernel, mesh=mesh,
        out_type=jax.ShapeDtypeStruct((N,) + out_tok, jnp.float32),
        compiler_params=pltpu.CompilerParams(needs_layout_passes=False),
        scratch_types=[
            pltpu.VMEM((2, PEER_GATHERS, PEER_GATHER_ROWS), jnp.int32),
            pltpu.VMEM((2,) + aux_shape, jnp.float32),
            pltpu.VMEM((2, PEER_GATHER_ROWS, D_MODEL), jnp.float32),
            pltpu.VMEM((2,) + out_tok, jnp.float32),
            pltpu.SemaphoreType.DMA((2,)),
            pltpu.SemaphoreType.DMA((2,)),
            pltpu.SemaphoreType.DMA((2,)),
        ],
    )
    def k(table_hbm, idx_hbm, aux_hbm, out_hbm, idx_v, aux_v, rows_v, out_v, sem_r, sem_i, sem_o):
        wid = lax.axis_index("s") * SC_CORES + lax.axis_index("c")
        base = wid * tpw

        def gather(p, c, b):
            return pltpu.make_async_copy(table_hbm.at[idx_v.at[p, c]], rows_v.at[b], sem_r.at[b])

        def load_meta(t, p):
            return (pltpu.make_async_copy(idx_hbm.at[t], idx_v.at[p], sem_i.at[p]),
                    pltpu.make_async_copy(aux_hbm.at[t], aux_v.at[p], sem_i.at[p]))

        def store_out(t, p):
            return pltpu.make_async_copy(out_v.at[p], out_hbm.at[t], sem_o.at[p])

        def compute(p, c, b):
            if phase == "dot":
                for g in range(PEER_GATHER_ROWS // PEER_ACC_VREGS):
                    def body(cc, accs):
                        xv = aux_v[p, pl.ds(cc * SC_LANES, SC_LANES)]
                        return tuple(accs[r] + rows_v[b, g * PEER_ACC_VREGS + r, pl.ds(cc * SC_LANES, SC_LANES)] * xv
                                     for r in range(PEER_ACC_VREGS))
                    accs = lax.fori_loop(0, D_MODEL // SC_LANES, body,
                                         tuple(jnp.zeros((SC_LANES,), jnp.float32) for _ in range(PEER_ACC_VREGS)))
                    for r in range(PEER_ACC_VREGS):
                        out_v[p, c * PEER_GATHER_ROWS + g * PEER_ACC_VREGS + r, :] = accs[r]
            else:
                for db in range(D_MODEL // (PEER_ACC_VREGS * SC_LANES)):
                    def body(kk, accs):
                        wv = aux_v[p, c * PEER_GATHER_ROWS + kk, :]
                        return tuple(accs[j] + rows_v[b, kk, pl.ds((db * PEER_ACC_VREGS + j) * SC_LANES, SC_LANES)] * wv
                                     for j in range(PEER_ACC_VREGS))
                    if c == 0:
                        init = tuple(jnp.zeros((SC_LANES,), jnp.float32) for _ in range(PEER_ACC_VREGS))
                    else:
                        init = tuple(out_v[p, pl.ds((db * PEER_ACC_VREGS + j) * SC_LANES, SC_LANES)]
                                     for j in range(PEER_ACC_VREGS))
                    accs = lax.fori_loop(0, PEER_GATHER_ROWS, body, init)
                    for j in range(PEER_ACC_VREGS):
                        out_v[p, pl.ds((db * PEER_ACC_VREGS + j) * SC_LANES, SC_LANES)] = accs[j]

        for d in load_meta(base, 0):
            d.start()
        for d in load_meta(base, 0):
            d.wait()
        gather(0, 0, 0).start()

        @pl.loop(0, tpw // 2)
        def _(i2):
            for p in range(2):
                i = i2 * 2 + p
                t = base + i
                nxt = base + jnp.minimum(i + 1, tpw - 1)
                for d in load_meta(nxt, 1 - p):
                    d.start()

                @pl.when(i2 > 0)
                def _():
                    store_out(t, p).wait()

                for c in range(PEER_GATHERS):
                    b = c % 2
                    if c < PEER_GATHERS - 1:
                        gather(p, c + 1, 1 - b).start()
                    else:
                        for d in load_meta(nxt, 1 - p):
                            d.wait()
                        gather(1 - p, 0, 0).start()
                    gather(p, c, b).wait()
                    compute(p, c, b)
                store_out(t, p).start()

        gather(0, 0, 0).wait()
        for p in range(2):
            store_out(base, p).wait()

    return k(table, idx.reshape(N, PEER_GATHERS, PEER_GATHER_ROWS), aux)


def peer_ffn(h, w_q, sub_keys, exp_u, exp_v):
    B, L, D = h.shape
    N = B * L
    hf = h.reshape(N, D)
    q = (hf @ w_q).reshape(N, PEER_HEADS, 2, PEER_HALF)
    s = jnp.einsum('thpd,hpnd->thpn', q, sub_keys, preferred_element_type=jnp.float32)
    sv, si = lax.top_k(s, PEER_TOPK)
    cand_s = (sv[:, :, 0, :, None] + sv[:, :, 1, None, :]).reshape(N, PEER_HEADS, PEER_TOPK * PEER_TOPK)
    cand_i = (si[:, :, 0, :, None] * PEER_N_KEYS + si[:, :, 1, None, :]).reshape(N, PEER_HEADS, PEER_TOPK * PEER_TOPK)
    top_s, top_j = lax.top_k(cand_s, PEER_TOPK)
    e_idx = jnp.take_along_axis(cand_i, top_j, axis=-1).reshape(N, PEER_SLOTS)
    gate = jax.nn.softmax(top_s, axis=-1).reshape(N, PEER_SLOTS)
    act = jax.nn.gelu(jnp.sum(_sc_peer(exp_u, e_idx, hf, "dot"), axis=-1), approximate=False)
    w16 = jnp.broadcast_to((gate * act)[:, :, None], (N, PEER_SLOTS, SC_LANES))
    return _sc_peer(exp_v, e_idx, w16, "wsum").reshape(B, L, D)


def kernel(x, c, ctx, c_ctx, mod_w, mod_b, mix_norm, w_in, w_out, rw_conv, rw_decay_up, rw_decay0, rw_a_up, rw_a0, rw_gate_up, rw_k_k, rw_k_a, rw_r_k, rw_gn_g, rw_gn_b, mla_q_norm, mla_w_uq, mla_kv_norm, mla_w_ukv, mla_q_gain, mla_k_gain, hy_conv, hy_w1, hy_b1, hy_freq1, hy_w2, hy_b2, hy_freq2, hy_w3, hy_b3, hy_bias, ffn_norm, peer_wq, peer_keys, peer_u, peer_v):
    B, L, D = x.shape
    rope = axial_rope_tables(L)
    s_rw, s_mla = RW_PROJ, RW_PROJ + MLA_PROJ
    for li in range(DEPTH):
        need_ctx = li < DEPTH - 1
        mod_l = (jax.nn.silu(c) @ mod_w[li] + mod_b[li])[:, None, :]
        mod_c = (jax.nn.silu(c_ctx) @ mod_w[li] + mod_b[li])[None, None, :]
        shm_l, scm_l, gm_l, shf_l, scf_l, gf_l = jnp.split(mod_l, N_MOD, axis=-1)
        shm_c, scm_c, gm_c, shf_c, scf_c, gf_c = jnp.split(mod_c, N_MOD, axis=-1)

        p_l = norm_mod_proj(x, mix_norm[li], scm_l, shm_l, w_in[li], 512)
        p_c = norm_mod_proj(ctx, mix_norm[li], jnp.broadcast_to(scm_c, (B, 1, D)),
                            jnp.broadcast_to(shm_c, (B, 1, D)), w_in[li], 256)
        rw_l, rw_c = rwkv7_mixer(p_l[..., :s_rw], p_c[..., :s_rw], rw_conv[li], rw_decay_up[li], rw_decay0[li],
                                 rw_a_up[li], rw_a0[li], rw_gate_up[li], rw_k_k[li], rw_k_a[li], rw_r_k[li],
                                 rw_gn_g[li], rw_gn_b[li], need_ctx)
        ml_l, ml_c = mla_mixer(p_l[..., s_rw:s_mla], p_c[..., s_rw:s_mla], rope, mla_q_norm[li], mla_w_uq[li],
                               mla_kv_norm[li], mla_w_ukv[li], mla_q_gain[li], mla_k_gain[li], need_ctx)
        hy_prm = (hy_conv[li], hy_w1[li], hy_b1[li], hy_freq1[li], hy_w2[li], hy_b2[li], hy_freq2[li],
                  hy_w3[li], hy_b3[li], hy_bias[li])
        hy_l = hyena_mixer(p_l[..., s_mla:], *hy_prm)
        x = x + gm_l * (jnp.concatenate([rw_l, ml_l, hy_l], axis=-1) @ w_out[li])
        x = x + gf_l * peer_ffn(rms_norm(x, ffn_norm[li]) * (1.0 + scf_l) + shf_l,
                                peer_wq[li], peer_keys[li], peer_u[li], peer_v[li])
        if need_ctx:
            hy_c = hyena_mixer(p_c[..., s_mla:], *hy_prm)
            ctx = ctx + gm_c * (jnp.concatenate([rw_c, ml_c, hy_c], axis=-1) @ w_out[li])
            ctx = ctx + gf_c * peer_ffn(rms_norm(ctx, ffn_norm[li]) * (1.0 + scf_c) + shf_c,
                                        peer_wq[li], peer_keys[li], peer_u[li], peer_v[li])
    return x
```

```python
import functools
import math

import jax
import jax.numpy as jnp
from jax import lax
from jax.experimental import pallas as pl
from jax.experimental.pallas import tpu as pltpu
from jax.experimental.pallas import tpu_sc as plsc

D_MODEL = 1024
DEPTH = 2
GRID_W = 64
N_MOD = 6
NORM_EPS = 1e-6

RW_HEADS = 6
RW_HEAD_DIM = 64
RW_WIDTH = RW_HEADS * RW_HEAD_DIM
RW_DECAY_RANK = 64
RW_A_RANK = 64
RW_GATE_RANK = 128
RW_DECAY_SCALE = 0.6065306597
RW_GN_EPS = 64e-5
L2_EPS = 1e-12

MLA_HEADS = 6
MLA_Q_RANK = 256
MLA_KV_RANK = 128
MLA_NOPE_DIM = 64
MLA_ROPE_DIM = 32
MLA_V_DIM = 64
MLA_QK_DIM = MLA_NOPE_DIM + MLA_ROPE_DIM
MLA_WIDTH = MLA_HEADS * MLA_V_DIM
AXIS_ROPE_DIM = MLA_ROPE_DIM // 2
ROPE_THETA = 10000.0
Q_BLOCK = 128

HY_WIDTH = 256
HY_ORDER = 2
HY_POS_BANDS = 16
HY_SHORT_DECAY_PCT = 0.3
HY_LONG_DECAY_PCT = 1.5
HY_DECAY_TARGET = 1e-2

PEER_HEADS = 8
PEER_N_KEYS = 128
PEER_TOPK = 16
PEER_QUERY_DIM = 256
PEER_HALF = PEER_QUERY_DIM // 2

RW_PROJ = 3 * RW_WIDTH + RW_DECAY_RANK + RW_A_RANK + RW_GATE_RANK
MLA_PROJ = MLA_Q_RANK + MLA_KV_RANK + MLA_ROPE_DIM
HY_PROJ = (HY_ORDER + 1) * HY_WIDTH
IN_PROJ = RW_PROJ + MLA_PROJ + HY_PROJ
MIX_WIDTH = RW_WIDTH + MLA_WIDTH + HY_WIDTH

VMEM_LIMIT_BYTES = 48 * 1024 * 1024


def _norm_mod_proj_kernel(x_ref, gain_ref, scale_ref, shift_ref, w_ref, o_ref):
    x = x_ref[0]
    y = x * lax.rsqrt(jnp.mean(x * x, axis=-1, keepdims=True) + NORM_EPS)
    y = y * gain_ref[...]
    y = y * (1.0 + scale_ref[0]) + shift_ref[0]
    o_ref[0] = jnp.dot(y.astype(jnp.bfloat16), w_ref[...], preferred_element_type=jnp.float32)


def norm_mod_proj(x, gain, scale, shift, w, block_rows):
    B, L, D = x.shape
    N = w.shape[1]
    return pl.pallas_call(
        _norm_mod_proj_kernel,
        grid=(B, L // block_rows),
        in_specs=[
            pl.BlockSpec((1, block_rows, D), lambda b, i: (b, i, 0)),
            pl.BlockSpec((1, D), lambda b, i: (0, 0)),
            pl.BlockSpec((1, 1, D), lambda b, i: (b, 0, 0)),
            pl.BlockSpec((1, 1, D), lambda b, i: (b, 0, 0)),
            pl.BlockSpec((D, N), lambda b, i: (0, 0)),
        ],
        out_specs=pl.BlockSpec((1, block_rows, N), lambda b, i: (b, i, 0)),
        out_shape=jax.ShapeDtypeStruct((B, L, N), jnp.float32),
        compiler_params=pltpu.CompilerParams(
            dimension_semantics=("parallel", "parallel"), vmem_limit_bytes=VMEM_LIMIT_BYTES),
        name="norm_mod_proj",
    )(x, gain.reshape(1, D), scale, shift, w.astype(jnp.bfloat16))


RW_CHUNK = 64


def _rwkv_chunk_kernel(r_ref, kk_ref, v_ref, lw_ref, akk_ref, kr_ref, y_ref, h_ref):
    d = pl.program_id(0)
    n = pl.program_id(2)

    @pl.when(n == 0)
    def _():
        h_ref[...] = jnp.zeros_like(h_ref)

    C = RW_CHUNK
    row = lax.broadcasted_iota(jnp.int32, (C, C), 0)
    col = lax.broadcasted_iota(jnp.int32, (C, C), 1)
    lag = (row - col) * (1 - 2 * d)
    before = lag > 0
    upto = lag >= 0
    tri = upto.astype(jnp.float32)
    eye = (row == col).astype(jnp.float32)
    bf = jnp.bfloat16
    f32 = jnp.float32

    def mm(a, b):
        return jnp.dot(a.astype(bf), b.astype(bf), preferred_element_type=f32)

    def mm_nt(a, b):
        return lax.dot_general(a.astype(bf), b.astype(bf), (((1,), (1,)), ((), ())), preferred_element_type=f32)

    def mm_tn(a, b):
        return lax.dot_general(a.astype(bf), b.astype(bf), (((0,), (0,)), ((), ())), preferred_element_type=f32)

    for h in range(RW_HEADS):
        r = r_ref[0, h]
        kk = kk_ref[0, h]
        v = v_ref[0, h]
        lw = lw_ref[0, 0, h]
        akk = akk_ref[0, 0, h]
        kr = kr_ref[0, 0, h]
        G = jnp.dot(tri, lw, preferred_element_type=f32, precision=lax.Precision.HIGHEST)
        gtot = jnp.sum(lw, axis=0, keepdims=True)
        E = jnp.exp(G)
        Einv = jnp.exp(-G)
        At = -kk * jnp.exp(G - lw)
        Bt = akk * Einv
        Kt = kr * Einv
        Rt = r * E
        Ehat = jnp.exp(gtot - G)
        Bh = akk * Ehat
        Kh = kr * Ehat
        X = mm_nt(jnp.concatenate([At, Rt], axis=0), jnp.concatenate([Bt, Kt], axis=0))
        M_ab = jnp.where(before, X[:C, :C], 0.0)
        M_ak = jnp.where(before, X[:C, C:], 0.0)
        A_rb = jnp.where(upto, X[C:, :C], 0.0)
        A_rk = jnp.where(upto, X[C:, C:], 0.0)
        Mp = M_ab
        T = eye + Mp
        for _ in range(5):
            Mp = jnp.dot(Mp, Mp, preferred_element_type=f32)
            T = T + jnp.dot(T, Mp, preferred_element_type=f32)
        MV = mm(M_ak, v)
        WU = jnp.dot(T, jnp.concatenate([At, MV], axis=1), preferred_element_type=f32)
        Wt = WU[:, :RW_HEAD_DIM]
        Ut = WU[:, RW_HEAD_DIM:]
        H0 = h_ref[h]
        Om = Rt + mm(A_rb, Wt)
        Y0 = mm(A_rb, Ut) + mm(A_rk, v)
        y_ref[0, 0, h] = jnp.dot(Om, H0, preferred_element_type=f32) + Y0
        BW = mm_tn(Bh, WU)
        P = eye * jnp.exp(gtot) + BW[:, :RW_HEAD_DIM]
        Q = BW[:, RW_HEAD_DIM:] + mm_tn(Kh, v)
        h_ref[h] = jnp.dot(P, H0, preferred_element_type=f32) + Q


def rwkv_chunked(r, kk, v, lw, akk, kr, n_ctx):
    B, H, T, _ = r.shape
    nc = n_ctx // RW_CHUNK
    nt = T // RW_CHUNK

    def chunk_of(d, n):
        bwd = jnp.where(n < nc, nc - 1 - n, nt - 1 - (n - nc))
        return jnp.where(d == 0, n, bwd)

    spec1 = pl.BlockSpec((1, H, RW_CHUNK, RW_HEAD_DIM), lambda d, b, n: (b, 0, chunk_of(d, n), 0))
    spec2 = pl.BlockSpec((1, 1, H, RW_CHUNK, RW_HEAD_DIM), lambda d, b, n: (d, b, 0, chunk_of(d, n), 0))
    return pl.pallas_call(
        _rwkv_chunk_kernel,
        grid=(2, B, nt),
        in_specs=[spec1, spec1, spec1, spec2, spec2, spec2],
        out_specs=spec2,
        out_shape=jax.ShapeDtypeStruct((2, B, H, T, RW_HEAD_DIM), jnp.float32),
        scratch_shapes=[pltpu.VMEM((H, RW_HEAD_DIM, RW_HEAD_DIM), jnp.float32)],
        compiler_params=pltpu.CompilerParams(dimension_semantics=("parallel", "parallel", "arbitrary")),
        name="rwkv_chunked",
    )(r, kk, v, lw, akk, kr)


def rms_norm(x, gain):
    y = x * lax.rsqrt(jnp.mean(x * x, axis=-1, keepdims=True) + NORM_EPS)
    return y * gain


def short_conv(x, w):
    xp = jnp.pad(x, ((0, 0), (1, 1), (0, 0)))
    return xp[:, :-2] * w[0] + xp[:, 1:-1] * w[1] + xp[:, 2:] * w[2]


def _heads(t):
    return t.reshape(t.shape[:-1] + (RW_HEADS, RW_HEAD_DIM))


def rwkv7_prepare(p, conv_w, decay_up, decay0, a_up, a0, gate_up, k_k, k_a):
    z = short_conv(p, conv_w)
    o1, o2, o3 = RW_WIDTH, 2 * RW_WIDTH, 3 * RW_WIDTH
    o4 = o3 + RW_DECAY_RANK
    o5 = o4 + RW_A_RANK
    r, k, v = z[..., :o1], z[..., o1:o2], z[..., o2:o3]
    d_lo, a_lo, g_lo = z[..., o3:o4], z[..., o4:o5], z[..., o5:]
    log_decay = -RW_DECAY_SCALE * jax.nn.sigmoid(
        decay0[:, None, None, :] + jnp.einsum('blr,nrc->nblc', jnp.tanh(d_lo), decay_up))
    a = jax.nn.sigmoid(a0[:, None, None, :] + jnp.einsum('blr,nrc->nblc', a_lo, a_up))
    g = jnp.einsum('blr,rc->blc', jax.nn.sigmoid(g_lo), gate_up)
    kk = _heads(k * k_k)
    kk = kk * lax.rsqrt(jnp.sum(kk * kk, axis=-1, keepdims=True) + L2_EPS)
    k_rep = _heads(k)[None] * (1.0 + (_heads(a) - 1.0) * _heads(k_a))
    return _heads(r), _heads(k), _heads(v), _heads(log_decay), kk[None] * _heads(a), kk, k_rep, g


def rwkv7_readout(y, r, k, v, g, r_k, gn_g, gn_b):
    B, L = y.shape[:2]
    mu = jnp.mean(y, axis=-1, keepdims=True)
    var = jnp.mean(jnp.square(y - mu), axis=-1, keepdims=True)
    yn = ((y - mu) * lax.rsqrt(var + RW_GN_EPS)).reshape(B, L, RW_WIDTH) * gn_g + gn_b
    bonus = jnp.sum(r * k * r_k.reshape(RW_HEADS, RW_HEAD_DIM), axis=-1, keepdims=True) * v
    return (yn + bonus.reshape(B, L, RW_WIDTH)) * g


def rwkv7_mixer(p_lat, p_ctx, conv_w, decay_up, decay0, a_up, a0, gate_up, k_k, k_a, r_k, gn_g, gn_b, need_ctx):
    prm = (conv_w, decay_up, decay0, a_up, a0, gate_up, k_k, k_a)
    lat = rwkv7_prepare(p_lat, *prm)
    ctx = rwkv7_prepare(p_ctx, *prm)
    n_ctx = p_ctx.shape[1]

    def seq(i):
        t = jnp.concatenate([ctx[i], lat[i]], axis=-3)
        return jnp.swapaxes(t, -3, -2)

    y = rwkv_chunked(seq(0), seq(5), seq(2), seq(3), seq(4), seq(6), n_ctx)
    y = jnp.swapaxes(y[0] + y[1], 1, 2)
    lr, lk, lv, _, _, _, _, lg = lat
    cr, ck, cv, _, _, _, _, cg = ctx
    out_l = rwkv7_readout(y[:, n_ctx:], lr, lk, lv, lg, r_k, gn_g, gn_b)
    out_c = rwkv7_readout(y[:, :n_ctx], cr, ck, cv, cg, r_k, gn_g, gn_b) if need_ctx else None
    return out_l, out_c


def axial_rope_tables(L):
    rows = L // GRID_W
    row, col = jnp.meshgrid(jnp.arange(rows), jnp.arange(GRID_W), indexing='ij')
    inv = ROPE_THETA ** (-jnp.arange(0, AXIS_ROPE_DIM, 2, dtype=jnp.float32) / AXIS_ROPE_DIM)
    pos = jnp.stack([row.reshape(-1), col.reshape(-1)], axis=-1).astype(jnp.float32)
    ang = pos[:, :, None] * inv[None, None, :]
    return jnp.cos(ang), jnp.sin(ang)


def apply_axial_rope(x, rope):
    cos, sin = rope
    B, L, H, _ = x.shape
    xa = x.reshape(B, L, H, 2, AXIS_ROPE_DIM)
    half = AXIS_ROPE_DIM // 2
    x1, x2 = xa[..., :half], xa[..., half:]
    cs, sn = cos[None, :, None], sin[None, :, None]
    out = jnp.concatenate([x1 * cs - x2 * sn, x2 * cs + x1 * sn], axis=-1)
    return out.reshape(B, L, H, MLA_ROPE_DIM)


def mla_qkv(p, rope, q_norm, w_uq, kv_norm, w_ukv, q_gain, k_gain):
    B, L, _ = p.shape
    c_q = p[..., :MLA_Q_RANK]
    c_kv = p[..., MLA_Q_RANK:MLA_Q_RANK + MLA_KV_RANK]
    k_rope = p[..., MLA_Q_RANK + MLA_KV_RANK:]
    q = (rms_norm(c_q, q_norm) @ w_uq).reshape(B, L, MLA_HEADS, MLA_QK_DIM)
    kv = (rms_norm(c_kv, kv_norm) @ w_ukv).reshape(B, L, MLA_HEADS, MLA_NOPE_DIM + MLA_V_DIM)
    k = jnp.concatenate([kv[..., :MLA_NOPE_DIM],
                         jnp.broadcast_to(k_rope[:, :, None, :], (B, L, MLA_HEADS, MLA_ROPE_DIM))], axis=-1)
    v = kv[..., MLA_NOPE_DIM:]
    q = rms_norm(q, q_gain)
    k = rms_norm(k, k_gain)
    if rope is not None:
        q = jnp.concatenate([q[..., :MLA_NOPE_DIM], apply_axial_rope(q[..., MLA_NOPE_DIM:], rope)], axis=-1)
        k = jnp.concatenate([k[..., :MLA_NOPE_DIM], apply_axial_rope(k[..., MLA_NOPE_DIM:], rope)], axis=-1)
    return q, k, v


ATTN_Q_ROWS = 512


def _attn_kernel(q_ref, k_ref, v_ref, o_ref):
    q = (q_ref[0, 0] * (MLA_QK_DIM ** -0.5)).astype(jnp.bfloat16)
    k = k_ref[0, 0].astype(jnp.bfloat16)
    s = lax.dot_general(q, k, (((1,), (1,)), ((), ())), preferred_element_type=jnp.float32)
    e = jnp.exp(s - jnp.max(s, axis=-1, keepdims=True))
    o = jnp.dot(e.astype(jnp.bfloat16), v_ref[0, 0].astype(jnp.bfloat16), preferred_element_type=jnp.float32)
    o_ref[0, 0] = o / jnp.sum(e, axis=-1, keepdims=True)


def attention(q, k, v):
    B, H, Lq, dk = q.shape
    Lk, dv = v.shape[2], v.shape[3]
    tq = min(ATTN_Q_ROWS, Lq)
    return pl.pallas_call(
        _attn_kernel,
        grid=(B, H, Lq // tq),
        in_specs=[
            pl.BlockSpec((1, 1, tq, dk), lambda b, h, i: (b, h, i, 0)),
            pl.BlockSpec((1, 1, Lk, dk), lambda b, h, i: (b, h, 0, 0)),
            pl.BlockSpec((1, 1, Lk, dv), lambda b, h, i: (b, h, 0, 0)),
        ],
        out_specs=pl.BlockSpec((1, 1, tq, dv), lambda b, h, i: (b, h, i, 0)),
        out_shape=jax.ShapeDtypeStruct((B, H, Lq, dv), jnp.float32),
        compiler_params=pltpu.CompilerParams(dimension_semantics=("parallel", "parallel", "parallel"),
                                             vmem_limit_bytes=VMEM_LIMIT_BYTES),
        name="mla_attention",
    )(q, k, v)


def mla_mixer(p_lat, p_ctx, rope, q_norm, w_uq, kv_norm, w_ukv, q_gain, k_gain, need_ctx):
    prm = (q_norm, w_uq, kv_norm, w_ukv, q_gain, k_gain)
    q_l, k_l, v_l = (jnp.swapaxes(t, 1, 2) for t in mla_qkv(p_lat, rope, *prm))
    q_c, k_c, v_c = (jnp.swapaxes(t, 1, 2) for t in mla_qkv(p_ctx, None, *prm))
    k_all = jnp.concatenate([k_l, k_c], axis=2)
    v_all = jnp.concatenate([v_l, v_c], axis=2)
    B, L = p_lat.shape[:2]
    y_l = jnp.swapaxes(attention(q_l, k_all, v_all), 1, 2).reshape(B, L, MLA_WIDTH)
    y_c = jnp.swapaxes(attention(q_c, k_c, v_c), 1, 2).reshape(B, p_ctx.shape[1], MLA_WIDTH) if need_ctx else None
    return y_l, y_c


def hyena_filters(L, w1, b1, freq1, w2, b2, freq2, w3, b3):
    tn = jnp.arange(L, dtype=jnp.float32) / L
    bands = jnp.arange(1, HY_POS_BANDS + 1, dtype=jnp.float32)
    ang = 2.0 * math.pi * tn[:, None] * bands[None, :]
    z = jnp.concatenate([tn[:, None], jnp.cos(ang), jnp.sin(ang)], axis=-1)
    h = jnp.sin(freq1 * (z @ w1 + b1))
    h = jnp.sin(freq2 * (h @ w2 + b2))
    h = (h @ w3 + b3).reshape(L, HY_ORDER, 2, HY_WIDTH)
    rates = jnp.abs(jnp.linspace(math.log(HY_DECAY_TARGET) / HY_LONG_DECAY_PCT,
                                 math.log(HY_DECAY_TARGET) / HY_SHORT_DECAY_PCT, HY_WIDTH))
    h = h * jnp.exp(-tn[:, None] * rates[None, :])[:, None, None, :]
    zero = jnp.zeros((1, HY_ORDER, HY_WIDTH), h.dtype)
    h_full = jnp.concatenate([h[:, :, 0], zero, h[:0:-1, :, 1]], axis=0)
    return h_full * lax.rsqrt(jnp.sum(jnp.square(h_full), axis=0, keepdims=True))


def fft_long_conv(u, h_full, bias):
    L = u.shape[1]
    uf = jnp.fft.rfft(u, n=2 * L, axis=1)
    hf = jnp.fft.rfft(h_full, n=2 * L, axis=0)
    y = jnp.fft.irfft(uf * hf[None], n=2 * L, axis=1)[:, :L]
    return y + u * bias


def hyena_mixer(p, conv_w, w1, b1, freq1, w2, b2, freq2, w3, b3, bias):
    L = p.shape[1]
    z = short_conv(p, conv_w)
    gates = (z[..., :HY_WIDTH], z[..., HY_WIDTH:2 * HY_WIDTH])
    y = z[..., 2 * HY_WIDTH:]
    h_full = hyena_filters(L, w1, b1, freq1, w2, b2, freq2, w3, b3)
    for o in range(HY_ORDER):
        y = gates[o] * fft_long_conv(y, h_full[:, o], bias[o])
    return y


SC_CORES = 2
SC_SUBCORES = 16
SC_LANES = 16
SC_WORKERS = SC_CORES * SC_SUBCORES
PEER_SLOTS = PEER_HEADS * PEER_TOPK
PEER_GATHER_ROWS = 32
PEER_GATHERS = PEER_SLOTS // PEER_GATHER_ROWS
PEER_ACC_VREGS = 8


def _sc_peer(table, idx, aux, phase):
    N = idx.shape[0]
    tpw = N // SC_WORKERS
    assert tpw % 2 == 0 and N % SC_WORKERS == 0
    mesh = plsc.VectorSubcoreMesh(core_axis_name="c", subcore_axis_name="s")
    aux_shape = (D_MODEL,) if phase == "dot" else (PEER_SLOTS, SC_LANES)
    out_tok = (PEER_SLOTS, SC_LANES) if phase == "dot" else (D_MODEL,)

    @functools.partial(
        pl.kernel, mesh=mesh,
        out_type=jax.ShapeDtypeStruct((N,) + out_tok, jnp.float32),
        compiler_params=pltpu.CompilerParams(needs_layout_passes=False),
        scratch_types=[
            pltpu.VMEM((2, PEER_GATHERS, PEER_GATHER_ROWS), jnp.int32),
            pltpu.VMEM((2,) + aux_shape, jnp.float32),
            pltpu.VMEM((2, PEER_GATHER_ROWS, D_MODEL), jnp.float32),
            pltpu.VMEM((2,) + out_tok, jnp.float32),
            pltpu.SemaphoreType.DMA((2,)),
            pltpu.SemaphoreType.DMA((2,)),
            pltpu.SemaphoreType.DMA((2,)),
        ],
    )
    def k(table_hbm, idx_hbm, aux_hbm, out_hbm, idx_v, aux_v, rows_v, out_v, sem_r, sem_i, sem_o):
        wid = lax.axis_index("s") * SC_CORES + lax.axis_index("c")
        base = wid * tpw

        def gather(p, c, b):
            return pltpu.make_async_copy(table_hbm.at[idx_v.at[p, c]], rows_v.at[b], sem_r.at[b])

        def load_meta(t, p):
            return (pltpu.make_async_copy(idx_hbm.at[t], idx_v.at[p], sem_i.at[p]),
                    pltpu.make_async_copy(aux_hbm.at[t], aux_v.at[p], sem_i.at[p]))

        def store_out(t, p):
            return pltpu.make_async_copy(out_v.at[p], out_hbm.at[t], sem_o.at[p])

        def compute(p, c, b):
            if phase == "dot":
                for g in range(PEER_GATHER_ROWS // PEER_ACC_VREGS):
                    def body(cc, accs):
                        xv = aux_v[p, pl.ds(cc * SC_LANES, SC_LANES)]
                        return tuple(accs[r] + rows_v[b, g * PEER_ACC_VREGS + r, pl.ds(cc * SC_LANES, SC_LANES)] * xv
                                     for r in range(PEER_ACC_VREGS))
                    accs = lax.fori_loop(0, D_MODEL // SC_LANES, body,
                                         tuple(jnp.zeros((SC_LANES,), jnp.float32) for _ in range(PEER_ACC_VREGS)))
                    for r in range(PEER_ACC_VREGS):
                        out_v[p, c * PEER_GATHER_ROWS + g * PEER_ACC_VREGS + r, :] = accs[r]
            else:
                for db in range(D_MODEL // (PEER_ACC_VREGS * SC_LANES)):
                    def body(kk, accs):
                        wv = aux_v[p, c * PEER_GATHER_ROWS + kk, :]
                        return tuple(accs[j] + rows_v[b, kk, pl.ds((db * PEER_ACC_VREGS + j) * SC_LANES, SC_LANES)] * wv
                                     for j in range(PEER_ACC_VREGS))
                    if c == 0:
                        init = tuple(jnp.zeros((SC_LANES,), jnp.float32) for _ in range(PEER_ACC_VREGS))
                    else:
                        init = tuple(out_v[p, pl.ds((db * PEER_ACC_VREGS + j) * SC_LANES, SC_LANES)]
                                     for j in range(PEER_ACC_VREGS))
                    accs = lax.fori_loop(0, PEER_GATHER_ROWS, body, init)
                    for j in range(PEER_ACC_VREGS):
                        out_v[p, pl.ds((db * PEER_ACC_VREGS + j) * SC_LANES, SC_LANES)] = accs[j]

        for d in load_meta(base, 0):
            d.start()
        for d in load_meta(base, 0):
            d.wait()
        gather(0, 0, 0).start()

        @pl.loop(0, tpw // 2)
        def _(i2):
            for p in range(2):
                i = i2 * 2 + p
                t = base + i
                nxt = base + jnp.minimum(i + 1, tpw - 1)
                for d in load_meta(nxt, 1 - p):
                    d.start()

                @pl.when(i2 > 0)
                def _():
                    store_out(t, p).wait()

                for c in range(PEER_GATHERS):
                    b = c % 2
                    if c < PEER_GATHERS - 1:
                        gather(p, c + 1, 1 - b).start()
                    else:
                        for d in load_meta(nxt, 1 - p):
                            d.wait()
                        gather(1 - p, 0, 0).start()
                    gather(p, c, b).wait()
                    compute(p, c, b)
                store_out(t, p).start()

        gather(0, 0, 0).wait()
        for p in range(2):
            store_out(base, p).wait()

    return k(table, idx.reshape(N, PEER_GATHERS, PEER_GATHER_ROWS), aux)


PEER_TOKENS = 256
INT_BIG = 2 ** 30


def _extract_topk(cand_ref, ids_ref, val_out_ref, id_out_ref, row0):
    def body(r, carry):
        c = cand_ref[...]
        ids = ids_ref[...]
        m = jnp.max(c, axis=0, keepdims=True)
        sel = jnp.min(jnp.where(c == m, ids, INT_BIG), axis=0, keepdims=True)
        cand_ref[...] = jnp.where(ids == sel, -jnp.inf, c)
        val_out_ref[pl.ds(row0 + r, 1), :] = m
        id_out_ref[pl.ds(row0 + r, 1), :] = sel
        return carry
    lax.fori_loop(0, PEER_TOPK, body, 0)


def _peer_retrieve_kernel(x_ref, gain_ref, scale_ref, shift_ref, wq_ref, keys_ref,
                          h_ref, idx_ref, gate_ref,
                          s_ref, ids1_ref, sv_ref, si_ref, cand_ref, cid_ref, ts_ref):
    x = x_ref[0]
    y = x * lax.rsqrt(jnp.mean(x * x, axis=-1, keepdims=True) + NORM_EPS)
    h = (y * gain_ref[...]) * (1.0 + scale_ref[0]) + shift_ref[0]
    h_ref[0] = h
    q = jnp.dot(h.astype(jnp.bfloat16), wq_ref[...], preferred_element_type=jnp.float32)
    T = PEER_TOKENS
    K = PEER_TOPK
    ids1_ref[...] = lax.broadcasted_iota(jnp.int32, (PEER_N_KEYS, T), 0)
    for hd in range(PEER_HEADS):
        for p in range(2):
            hp = hd * 2 + p
            qs = q[:, hp * PEER_HALF:(hp + 1) * PEER_HALF].astype(jnp.bfloat16)
            s_ref[...] = lax.dot_general(keys_ref[hp], qs, (((1,), (1,)), ((), ())),
                                         preferred_element_type=jnp.float32)
            _extract_topk(s_ref, ids1_ref, sv_ref, si_ref, p * K)
        for i in range(K):
            cand_ref[i * K:(i + 1) * K, :] = sv_ref[i:i + 1, :] + sv_ref[K:2 * K, :]
            cid_ref[i * K:(i + 1) * K, :] = si_ref[i:i + 1, :] * PEER_N_KEYS + si_ref[K:2 * K, :]
        _extract_topk(cand_ref, cid_ref, ts_ref, idx_ref, hd * K)
        ts = ts_ref[hd * K:(hd + 1) * K, :]
        e = jnp.exp(ts - jnp.max(ts, axis=0, keepdims=True))
        gate_ref[hd * K:(hd + 1) * K, :] = e / jnp.sum(e, axis=0, keepdims=True)


def peer_retrieve(x, gain, scale, shift, w_q, sub_keys):
    B, L, D = x.shape
    T = PEER_TOKENS
    nt = L // T
    keys = sub_keys.reshape(PEER_HEADS * 2, PEER_N_KEYS, PEER_HALF).astype(jnp.bfloat16)
    return pl.pallas_call(
        _peer_retrieve_kernel,
        grid=(B, nt),
        in_specs=[
            pl.BlockSpec((1, T, D), lambda b, i: (b, i, 0)),
            pl.BlockSpec((1, D), lambda b, i: (0, 0)),
            pl.BlockSpec((1, 1, D), lambda b, i: (b, 0, 0)),
            pl.BlockSpec((1, 1, D), lambda b, i: (b, 0, 0)),
            pl.BlockSpec((D, PEER_HEADS * 2 * PEER_HALF), lambda b, i: (0, 0)),
            pl.BlockSpec((PEER_HEADS * 2, PEER_N_KEYS, PEER_HALF), lambda b, i: (0, 0, 0)),
        ],
        out_specs=[
            pl.BlockSpec((1, T, D), lambda b, i: (b, i, 0)),
            pl.BlockSpec((PEER_SLOTS, T), lambda b, i: (0, b * nt + i)),
            pl.BlockSpec((PEER_SLOTS, T), lambda b, i: (0, b * nt + i)),
        ],
        out_shape=[
            jax.ShapeDtypeStruct((B, L, D), jnp.float32),
            jax.ShapeDtypeStruct((PEER_SLOTS, B * L), jnp.int32),
            jax.ShapeDtypeStruct((PEER_SLOTS, B * L), jnp.float32),
        ],
        scratch_shapes=[
            pltpu.VMEM((PEER_N_KEYS, T), jnp.float32),
            pltpu.VMEM((PEER_N_KEYS, T), jnp.int32),
            pltpu.VMEM((2 * PEER_TOPK, T), jnp.float32),
            pltpu.VMEM((2 * PEER_TOPK, T), jnp.int32),
            pltpu.VMEM((PEER_TOPK * PEER_TOPK, T), jnp.float32),
            pltpu.VMEM((PEER_TOPK * PEER_TOPK, T), jnp.int32),
            pltpu.VMEM((PEER_SLOTS, T), jnp.float32),
        ],
        compiler_params=pltpu.CompilerParams(dimension_semantics=("parallel", "parallel"),
                                             vmem_limit_bytes=VMEM_LIMIT_BYTES),
        name="peer_retrieve",
    )(x, gain.reshape(1, D), scale, shift, w_q.astype(jnp.bfloat16), keys)


def peer_ffn(x, gain, scale, shift, w_q, sub_keys, exp_u, exp_v):
    B, L, D = x.shape
    N = B * L
    h, idx_t, gate_t = peer_retrieve(x, gain, scale, shift, w_q, sub_keys)
    hf = h.reshape(N, D)
    e_idx = idx_t.T
    gate = gate_t.T
    act = jax.nn.gelu(jnp.sum(_sc_peer(exp_u, e_idx, hf, "dot"), axis=-1), approximate=False)
    w16 = jnp.broadcast_to((gate * act)[:, :, None], (N, PEER_SLOTS, SC_LANES))
    return _sc_peer(exp_v, e_idx, w16, "wsum").reshape(B, L, D)


def kernel(x, c, ctx, c_ctx, mod_w, mod_b, mix_norm, w_in, w_out, rw_conv, rw_decay_up, rw_decay0, rw_a_up, rw_a0, rw_gate_up, rw_k_k, rw_k_a, rw_r_k, rw_gn_g, rw_gn_b, mla_q_norm, mla_w_uq, mla_kv_norm, mla_w_ukv, mla_q_gain, mla_k_gain, hy_conv, hy_w1, hy_b1, hy_freq1, hy_w2, hy_b2, hy_freq2, hy_w3, hy_b3, hy_bias, ffn_norm, peer_wq, peer_keys, peer_u, peer_v):
    B, L, D = x.shape
    rope = axial_rope_tables(L)
    s_rw, s_mla = RW_PROJ, RW_PROJ + MLA_PROJ
    for li in range(DEPTH):
        need_ctx = li < DEPTH - 1
        mod_l = (jax.nn.silu(c) @ mod_w[li] + mod_b[li])[:, None, :]
        mod_c = (jax.nn.silu(c_ctx) @ mod_w[li] + mod_b[li])[None, None, :]
        shm_l, scm_l, gm_l, shf_l, scf_l, gf_l = jnp.split(mod_l, N_MOD, axis=-1)
        shm_c, scm_c, gm_c, shf_c, scf_c, gf_c = jnp.split(mod_c, N_MOD, axis=-1)

        p_l = norm_mod_proj(x, mix_norm[li], scm_l, shm_l, w_in[li], 512)
        p_c = norm_mod_proj(ctx, mix_norm[li], jnp.broadcast_to(scm_c, (B, 1, D)),
                            jnp.broadcast_to(shm_c, (B, 1, D)), w_in[li], 256)
        rw_l, rw_c = rwkv7_mixer(p_l[..., :s_rw], p_c[..., :s_rw], rw_conv[li], rw_decay_up[li], rw_decay0[li],
                                 rw_a_up[li], rw_a0[li], rw_gate_up[li], rw_k_k[li], rw_k_a[li], rw_r_k[li],
                                 rw_gn_g[li], rw_gn_b[li], need_ctx)
        ml_l, ml_c = mla_mixer(p_l[..., s_rw:s_mla], p_c[..., s_rw:s_mla], rope, mla_q_norm[li], mla_w_uq[li],
                               mla_kv_norm[li], mla_w_ukv[li], mla_q_gain[li], mla_k_gain[li], need_ctx)
        hy_prm = (hy_conv[li], hy_w1[li], hy_b1[li], hy_freq1[li], hy_w2[li], hy_b2[li], hy_freq2[li],
                  hy_w3[li], hy_b3[li], hy_bias[li])
        hy_l = hyena_mixer(p_l[..., s_mla:], *hy_prm)
        x = x + gm_l * (jnp.concatenate([rw_l, ml_l, hy_l], axis=-1) @ w_out[li])
        x = x + gf_l * peer_ffn(x, ffn_norm[li], scf_l, shf_l,
                                peer_wq[li], peer_keys[li], peer_u[li], peer_v[li])
        if need_ctx:
            hy_c = hyena_mixer(p_c[..., s_mla:], *hy_prm)
            ctx = ctx + gm_c * (jnp.concatenate([rw_c, ml_c, hy_c], axis=-1) @ w_out[li])
            ctx = ctx + gf_c * peer_ffn(ctx, ffn_norm[li], jnp.broadcast_to(scf_c, (B, 1, D)),
                                        jnp.broadcast_to(shf_c, (B, 1, D)),
                                        peer_wq[li], peer_keys[li], peer_u[li], peer_v[li])
    return x
```

```python
import functools
import math

import jax
import jax.numpy as jnp
from jax import lax
from jax.experimental import pallas as pl
from jax.experimental.pallas import tpu as pltpu
from jax.experimental.pallas import tpu_sc as plsc

D_MODEL = 1024
DEPTH = 2
GRID_W = 64
N_MOD = 6
NORM_EPS = 1e-6

RW_HEADS = 6
RW_HEAD_DIM = 64
RW_WIDTH = RW_HEADS * RW_HEAD_DIM
RW_DECAY_RANK = 64
RW_A_RANK = 64
RW_GATE_RANK = 128
RW_DECAY_SCALE = 0.6065306597
RW_GN_EPS = 64e-5
L2_EPS = 1e-12

MLA_HEADS = 6
MLA_Q_RANK = 256
MLA_KV_RANK = 128
MLA_NOPE_DIM = 64
MLA_ROPE_DIM = 32
MLA_V_DIM = 64
MLA_QK_DIM = MLA_NOPE_DIM + MLA_ROPE_DIM
MLA_WIDTH = MLA_HEADS * MLA_V_DIM
AXIS_ROPE_DIM = MLA_ROPE_DIM // 2
ROPE_THETA = 10000.0
Q_BLOCK = 128

HY_WIDTH = 256
HY_ORDER = 2
HY_POS_BANDS = 16
HY_SHORT_DECAY_PCT = 0.3
HY_LONG_DECAY_PCT = 1.5
HY_DECAY_TARGET = 1e-2

PEER_HEADS = 8
PEER_N_KEYS = 128
PEER_TOPK = 16
PEER_QUERY_DIM = 256
PEER_HALF = PEER_QUERY_DIM // 2

RW_PROJ = 3 * RW_WIDTH + RW_DECAY_RANK + RW_A_RANK + RW_GATE_RANK
MLA_PROJ = MLA_Q_RANK + MLA_KV_RANK + MLA_ROPE_DIM
HY_PROJ = (HY_ORDER + 1) * HY_WIDTH
IN_PROJ = RW_PROJ + MLA_PROJ + HY_PROJ
MIX_WIDTH = RW_WIDTH + MLA_WIDTH + HY_WIDTH

VMEM_LIMIT_BYTES = 48 * 1024 * 1024


def _norm_mod_proj_kernel(x_ref, gain_ref, scale_ref, shift_ref, w_ref, o_ref):
    x = x_ref[0]
    y = x * lax.rsqrt(jnp.mean(x * x, axis=-1, keepdims=True) + NORM_EPS)
    y = y * gain_ref[...]
    y = y * (1.0 + scale_ref[0]) + shift_ref[0]
    o_ref[0] = jnp.dot(y.astype(jnp.bfloat16), w_ref[...], preferred_element_type=jnp.float32)


def norm_mod_proj(x, gain, scale, shift, w, block_rows):
    B, L, D = x.shape
    N = w.shape[1]
    return pl.pallas_call(
        _norm_mod_proj_kernel,
        grid=(B, L // block_rows),
        in_specs=[
            pl.BlockSpec((1, block_rows, D), lambda b, i: (b, i, 0)),
            pl.BlockSpec((1, D), lambda b, i: (0, 0)),
            pl.BlockSpec((1, 1, D), lambda b, i: (b, 0, 0)),
            pl.BlockSpec((1, 1, D), lambda b, i: (b, 0, 0)),
            pl.BlockSpec((D, N), lambda b, i: (0, 0)),
        ],
        out_specs=pl.BlockSpec((1, block_rows, N), lambda b, i: (b, i, 0)),
        out_shape=jax.ShapeDtypeStruct((B, L, N), jnp.float32),
        compiler_params=pltpu.CompilerParams(
            dimension_semantics=("parallel", "parallel"), vmem_limit_bytes=VMEM_LIMIT_BYTES),
        name="norm_mod_proj",
    )(x, gain.reshape(1, D), scale, shift, w.astype(jnp.bfloat16))


RW_CHUNK = 64


def _rwkv_chunk_kernel(r_ref, kk_ref, v_ref, lw_ref, akk_ref, kr_ref, y_ref, h_ref):
    d = pl.program_id(0)
    n = pl.program_id(2)

    @pl.when(n == 0)
    def _():
        h_ref[...] = jnp.zeros_like(h_ref)

    C = RW_CHUNK
    row = lax.broadcasted_iota(jnp.int32, (C, C), 0)
    col = lax.broadcasted_iota(jnp.int32, (C, C), 1)
    lag = (row - col) * (1 - 2 * d)
    before = lag > 0
    upto = lag >= 0
    tri = upto.astype(jnp.float32)
    eye = (row == col).astype(jnp.float32)
    bf = jnp.bfloat16
    f32 = jnp.float32

    def mm(a, b):
        return jnp.dot(a.astype(bf), b.astype(bf), preferred_element_type=f32)

    def mm_nt(a, b):
        return lax.dot_general(a.astype(bf), b.astype(bf), (((1,), (1,)), ((), ())), preferred_element_type=f32)

    def mm_tn(a, b):
        return lax.dot_general(a.astype(bf), b.astype(bf), (((0,), (0,)), ((), ())), preferred_element_type=f32)

    for h in range(RW_HEADS):
        r = r_ref[0, h]
        kk = kk_ref[0, h]
        v = v_ref[0, h]
        lw = lw_ref[0, 0, h]
        akk = akk_ref[0, 0, h]
        kr = kr_ref[0, 0, h]
        G = jnp.dot(tri, lw, preferred_element_type=f32, precision=lax.Precision.HIGHEST)
        gtot = jnp.sum(lw, axis=0, keepdims=True)
        E = jnp.exp(G)
        Einv = jnp.exp(-G)
        At = -kk * jnp.exp(G - lw)
        Bt = akk * Einv
        Kt = kr * Einv
        Rt = r * E
        Ehat = jnp.exp(gtot - G)
        Bh = akk * Ehat
        Kh = kr * Ehat
        X = mm_nt(jnp.concatenate([At, Rt], axis=0), jnp.concatenate([Bt, Kt], axis=0))
        M_ab = jnp.where(before, X[:C, :C], 0.0)
        M_ak = jnp.where(before, X[:C, C:], 0.0)
        A_rb = jnp.where(upto, X[C:, :C], 0.0)
        A_rk = jnp.where(upto, X[C:, C:], 0.0)
        Mp = M_ab
        T = eye + Mp
        for _ in range(5):
            Mp = jnp.dot(Mp, Mp, preferred_element_type=f32)
            T = T + jnp.dot(T, Mp, preferred_element_type=f32)
        MV = mm(M_ak, v)
        WU = jnp.dot(T, jnp.concatenate([At, MV], axis=1), preferred_element_type=f32)
        Wt = WU[:, :RW_HEAD_DIM]
        Ut = WU[:, RW_HEAD_DIM:]
        H0 = h_ref[h]
        Om = Rt + mm(A_rb, Wt)
        Y0 = mm(A_rb, Ut) + mm(A_rk, v)
        y_ref[0, 0, h] = jnp.dot(Om, H0, preferred_element_type=f32) + Y0
        BW = mm_tn(Bh, WU)
        P = eye * jnp.exp(gtot) + BW[:, :RW_HEAD_DIM]
        Q = BW[:, RW_HEAD_DIM:] + mm_tn(Kh, v)
        h_ref[h] = jnp.dot(P, H0, preferred_element_type=f32) + Q


def rwkv_chunked(r, kk, v, lw, akk, kr, n_ctx):
    B, H, T, _ = r.shape
    nc = n_ctx // RW_CHUNK
    nt = T // RW_CHUNK

    def chunk_of(d, n):
        bwd = jnp.where(n < nc, nc - 1 - n, nt - 1 - (n - nc))
        return jnp.where(d == 0, n, bwd)

    spec1 = pl.BlockSpec((1, H, RW_CHUNK, RW_HEAD_DIM), lambda d, b, n: (b, 0, chunk_of(d, n), 0))
    spec2 = pl.BlockSpec((1, 1, H, RW_CHUNK, RW_HEAD_DIM), lambda d, b, n: (d, b, 0, chunk_of(d, n), 0))
    return pl.pallas_call(
        _rwkv_chunk_kernel,
        grid=(2, B, nt),
        in_specs=[spec1, spec1, spec1, spec2, spec2, spec2],
        out_specs=spec2,
        out_shape=jax.ShapeDtypeStruct((2, B, H, T, RW_HEAD_DIM), jnp.float32),
        scratch_shapes=[pltpu.VMEM((H, RW_HEAD_DIM, RW_HEAD_DIM), jnp.float32)],
        compiler_params=pltpu.CompilerParams(dimension_semantics=("parallel", "parallel", "arbitrary")),
        name="rwkv_chunked",
    )(r, kk, v, lw, akk, kr)


def rms_norm(x, gain):
    y = x * lax.rsqrt(jnp.mean(x * x, axis=-1, keepdims=True) + NORM_EPS)
    return y * gain


def short_conv(x, w):
    xp = jnp.pad(x, ((0, 0), (1, 1), (0, 0)))
    return xp[:, :-2] * w[0] + xp[:, 1:-1] * w[1] + xp[:, 2:] * w[2]


def _heads(t):
    return t.reshape(t.shape[:-1] + (RW_HEADS, RW_HEAD_DIM))


def rwkv7_prepare(p, conv_w, decay_up, decay0, a_up, a0, gate_up, k_k, k_a):
    z = short_conv(p, conv_w)
    o1, o2, o3 = RW_WIDTH, 2 * RW_WIDTH, 3 * RW_WIDTH
    o4 = o3 + RW_DECAY_RANK
    o5 = o4 + RW_A_RANK
    r, k, v = z[..., :o1], z[..., o1:o2], z[..., o2:o3]
    d_lo, a_lo, g_lo = z[..., o3:o4], z[..., o4:o5], z[..., o5:]
    log_decay = -RW_DECAY_SCALE * jax.nn.sigmoid(
        decay0[:, None, None, :] + jnp.einsum('blr,nrc->nblc', jnp.tanh(d_lo), decay_up))
    a = jax.nn.sigmoid(a0[:, None, None, :] + jnp.einsum('blr,nrc->nblc', a_lo, a_up))
    g = jnp.einsum('blr,rc->blc', jax.nn.sigmoid(g_lo), gate_up)
    kk = _heads(k * k_k)
    kk = kk * lax.rsqrt(jnp.sum(kk * kk, axis=-1, keepdims=True) + L2_EPS)
    k_rep = _heads(k)[None] * (1.0 + (_heads(a) - 1.0) * _heads(k_a))
    return _heads(r), _heads(k), _heads(v), _heads(log_decay), kk[None] * _heads(a), kk, k_rep, g


def rwkv7_readout(y, r, k, v, g, r_k, gn_g, gn_b):
    B, L = y.shape[:2]
    mu = jnp.mean(y, axis=-1, keepdims=True)
    var = jnp.mean(jnp.square(y - mu), axis=-1, keepdims=True)
    yn = ((y - mu) * lax.rsqrt(var + RW_GN_EPS)).reshape(B, L, RW_WIDTH) * gn_g + gn_b
    bonus = jnp.sum(r * k * r_k.reshape(RW_HEADS, RW_HEAD_DIM), axis=-1, keepdims=True) * v
    return (yn + bonus.reshape(B, L, RW_WIDTH)) * g


def rwkv7_mixer(p_lat, p_ctx, conv_w, decay_up, decay0, a_up, a0, gate_up, k_k, k_a, r_k, gn_g, gn_b, need_ctx):
    prm = (conv_w, decay_up, decay0, a_up, a0, gate_up, k_k, k_a)
    lat = rwkv7_prepare(p_lat, *prm)
    ctx = rwkv7_prepare(p_ctx, *prm)
    n_ctx = p_ctx.shape[1]

    def seq(i):
        t = jnp.concatenate([ctx[i], lat[i]], axis=-3)
        return jnp.swapaxes(t, -3, -2)

    y = rwkv_chunked(seq(0), seq(5), seq(2), seq(3), seq(4), seq(6), n_ctx)
    y = jnp.swapaxes(y[0] + y[1], 1, 2)
    lr, lk, lv, _, _, _, _, lg = lat
    cr, ck, cv, _, _, _, _, cg = ctx
    out_l = rwkv7_readout(y[:, n_ctx:], lr, lk, lv, lg, r_k, gn_g, gn_b)
    out_c = rwkv7_readout(y[:, :n_ctx], cr, ck, cv, cg, r_k, gn_g, gn_b) if need_ctx else None
    return out_l, out_c


def axial_rope_tables(L):
    rows = L // GRID_W
    row, col = jnp.meshgrid(jnp.arange(rows), jnp.arange(GRID_W), indexing='ij')
    inv = ROPE_THETA ** (-jnp.arange(0, AXIS_ROPE_DIM, 2, dtype=jnp.float32) / AXIS_ROPE_DIM)
    pos = jnp.stack([row.reshape(-1), col.reshape(-1)], axis=-1).astype(jnp.float32)
    ang = pos[:, :, None] * inv[None, None, :]
    return jnp.cos(ang), jnp.sin(ang)


def apply_axial_rope(x, rope):
    cos, sin = rope
    B, L, H, _ = x.shape
    xa = x.reshape(B, L, H, 2, AXIS_ROPE_DIM)
    half = AXIS_ROPE_DIM // 2
    x1, x2 = xa[..., :half], xa[..., half:]
    cs, sn = cos[None, :, None], sin[None, :, None]
    out = jnp.concatenate([x1 * cs - x2 * sn, x2 * cs + x1 * sn], axis=-1)
    return out.reshape(B, L, H, MLA_ROPE_DIM)


def mla_qkv(p, rope, q_norm, w_uq, kv_norm, w_ukv, q_gain, k_gain):
    B, L, _ = p.shape
    c_q = p[..., :MLA_Q_RANK]
    c_kv = p[..., MLA_Q_RANK:MLA_Q_RANK + MLA_KV_RANK]
    k_rope = p[..., MLA_Q_RANK + MLA_KV_RANK:]
    q = (rms_norm(c_q, q_norm) @ w_uq).reshape(B, L, MLA_HEADS, MLA_QK_DIM)
    kv = (rms_norm(c_kv, kv_norm) @ w_ukv).reshape(B, L, MLA_HEADS, MLA_NOPE_DIM + MLA_V_DIM)
    k = jnp.concatenate([kv[..., :MLA_NOPE_DIM],
                         jnp.broadcast_to(k_rope[:, :, None, :], (B, L, MLA_HEADS, MLA_ROPE_DIM))], axis=-1)
    v = kv[..., MLA_NOPE_DIM:]
    q = rms_norm(q, q_gain)
    k = rms_norm(k, k_gain)
    if rope is not None:
        q = jnp.concatenate([q[..., :MLA_NOPE_DIM], apply_axial_rope(q[..., MLA_NOPE_DIM:], rope)], axis=-1)
        k = jnp.concatenate([k[..., :MLA_NOPE_DIM], apply_axial_rope(k[..., MLA_NOPE_DIM:], rope)], axis=-1)
    return q, k, v


ATTN_Q_ROWS = 512


def _attn_kernel(q_ref, k_ref, v_ref, o_ref):
    q = (q_ref[0, 0] * (MLA_QK_DIM ** -0.5)).astype(jnp.bfloat16)
    k = k_ref[0, 0].astype(jnp.bfloat16)
    s = lax.dot_general(q, k, (((1,), (1,)), ((), ())), preferred_element_type=jnp.float32)
    e = jnp.exp(s - jnp.max(s, axis=-1, keepdims=True))
    o = jnp.dot(e.astype(jnp.bfloat16), v_ref[0, 0].astype(jnp.bfloat16), preferred_element_type=jnp.float32)
    o_ref[0, 0] = o / jnp.sum(e, axis=-1, keepdims=True)


def attention(q, k, v):
    B, H, Lq, dk = q.shape
    Lk, dv = v.shape[2], v.shape[3]
    tq = min(ATTN_Q_ROWS, Lq)
    return pl.pallas_call(
        _attn_kernel,
        grid=(B, H, Lq // tq),
        in_specs=[
            pl.BlockSpec((1, 1, tq, dk), lambda b, h, i: (b, h, i, 0)),
            pl.BlockSpec((1, 1, Lk, dk), lambda b, h, i: (b, h, 0, 0)),
            pl.BlockSpec((1, 1, Lk, dv), lambda b, h, i: (b, h, 0, 0)),
        ],
        out_specs=pl.BlockSpec((1, 1, tq, dv), lambda b, h, i: (b, h, i, 0)),
        out_shape=jax.ShapeDtypeStruct((B, H, Lq, dv), jnp.float32),
        compiler_params=pltpu.CompilerParams(dimension_semantics=("parallel", "parallel", "parallel"),
                                             vmem_limit_bytes=VMEM_LIMIT_BYTES),
        name="mla_attention",
    )(q, k, v)


def mla_mixer(p_lat, p_ctx, rope, q_norm, w_uq, kv_norm, w_ukv, q_gain, k_gain, need_ctx):
    prm = (q_norm, w_uq, kv_norm, w_ukv, q_gain, k_gain)
    q_l, k_l, v_l = (jnp.swapaxes(t, 1, 2) for t in mla_qkv(p_lat, rope, *prm))
    q_c, k_c, v_c = (jnp.swapaxes(t, 1, 2) for t in mla_qkv(p_ctx, None, *prm))
    k_all = jnp.concatenate([k_l, k_c], axis=2)
    v_all = jnp.concatenate([v_l, v_c], axis=2)
    B, L = p_lat.shape[:2]
    y_l = jnp.swapaxes(attention(q_l, k_all, v_all), 1, 2).reshape(B, L, MLA_WIDTH)
    y_c = jnp.swapaxes(attention(q_c, k_c, v_c), 1, 2).reshape(B, p_ctx.shape[1], MLA_WIDTH) if need_ctx else None
    return y_l, y_c


def hyena_filters(L, w1, b1, freq1, w2, b2, freq2, w3, b3):
    tn = jnp.arange(L, dtype=jnp.float32) / L
    bands = jnp.arange(1, HY_POS_BANDS + 1, dtype=jnp.float32)
    ang = 2.0 * math.pi * tn[:, None] * bands[None, :]
    z = jnp.concatenate([tn[:, None], jnp.cos(ang), jnp.sin(ang)], axis=-1)
    h = jnp.sin(freq1 * (z @ w1 + b1))
    h = jnp.sin(freq2 * (h @ w2 + b2))
    h = (h @ w3 + b3).reshape(L, HY_ORDER, 2, HY_WIDTH)
    rates = jnp.abs(jnp.linspace(math.log(HY_DECAY_TARGET) / HY_LONG_DECAY_PCT,
                                 math.log(HY_DECAY_TARGET) / HY_SHORT_DECAY_PCT, HY_WIDTH))
    h = h * jnp.exp(-tn[:, None] * rates[None, :])[:, None, None, :]
    zero = jnp.zeros((1, HY_ORDER, HY_WIDTH), h.dtype)
    h_full = jnp.concatenate([h[:, :, 0], zero, h[:0:-1, :, 1]], axis=0)
    return h_full * lax.rsqrt(jnp.sum(jnp.square(h_full), axis=0, keepdims=True))


def fft_long_conv(u, h_full, bias):
    L = u.shape[1]
    uf = jnp.fft.rfft(u, n=2 * L, axis=1)
    hf = jnp.fft.rfft(h_full, n=2 * L, axis=0)
    y = jnp.fft.irfft(uf * hf[None], n=2 * L, axis=1)[:, :L]
    return y + u * bias


def hyena_mixer(p, conv_w, w1, b1, freq1, w2, b2, freq2, w3, b3, bias):
    L = p.shape[1]
    z = short_conv(p, conv_w)
    gates = (z[..., :HY_WIDTH], z[..., HY_WIDTH:2 * HY_WIDTH])
    y = z[..., 2 * HY_WIDTH:]
    h_full = hyena_filters(L, w1, b1, freq1, w2, b2, freq2, w3, b3)
    for o in range(HY_ORDER):
        y = gates[o] * fft_long_conv(y, h_full[:, o], bias[o])
    return y


SC_CORES = 2
SC_SUBCORES = 16
SC_LANES = 16
SC_WORKERS = SC_CORES * SC_SUBCORES
PEER_SLOTS = PEER_HEADS * PEER_TOPK
PEER_GATHER_ROWS = 32
PEER_GATHERS = PEER_SLOTS // PEER_GATHER_ROWS
PEER_ACC_VREGS = 8


def _sc_peer(table, idx, aux, phase):
    N = idx.shape[0]
    tpw = N // SC_WORKERS
    assert tpw % 2 == 0 and N % SC_WORKERS == 0
    mesh = plsc.VectorSubcoreMesh(core_axis_name="c", subcore_axis_name="s")
    aux_shape = (D_MODEL,) if phase == "dot" else (PEER_SLOTS, SC_LANES)
    out_tok = (PEER_SLOTS, SC_LANES) if phase == "dot" else (D_MODEL,)

    @functools.partial(
        pl.kernel, mesh=mesh,
        out_type=jax.ShapeDtypeStruct((N,) + out_tok, jnp.float32),
        compiler_params=pltpu.CompilerParams(needs_layout_passes=False),
        scratch_types=[
            pltpu.VMEM((2, PEER_GATHERS, PEER_GATHER_ROWS), jnp.int32),
            pltpu.VMEM((2,) + aux_shape, jnp.float32),
            pltpu.VMEM((2, PEER_GATHER_ROWS, D_MODEL), jnp.float32),
            pltpu.VMEM((2,) + out_tok, jnp.float32),
            pltpu.SemaphoreType.DMA((2,)),
            pltpu.SemaphoreType.DMA((2,)),
            pltpu.SemaphoreType.DMA((2,)),
        ],
    )
    def k(table_hbm, idx_hbm, aux_hbm, out_hbm, idx_v, aux_v, rows_v, out_v, sem_r, sem_i, sem_o):
        wid = lax.axis_index("s") * SC_CORES + lax.axis_index("c")
        base = wid * tpw

        def gather(p, c, b):
            return pltpu.make_async_copy(table_hbm.at[idx_v.at[p, c]], rows_v.at[b], sem_r.at[b])

        def load_meta(t, p):
            return (pltpu.make_async_copy(idx_hbm.at[t], idx_v.at[p], sem_i.at[p]),
                    pltpu.make_async_copy(aux_hbm.at[t], aux_v.at[p], sem_i.at[p]))

        def store_out(t, p):
            return pltpu.make_async_copy(out_v.at[p], out_hbm.at[t], sem_o.at[p])

        def compute(p, c, b):
            if phase == "dot":
                for g in range(PEER_GATHER_ROWS // PEER_ACC_VREGS):
                    def body(cc, accs):
                        xv = aux_v[p, pl.ds(cc * SC_LANES, SC_LANES)]
                        return tuple(accs[r] + rows_v[b, g * PEER_ACC_VREGS + r, pl.ds(cc * SC_LANES, SC_LANES)] * xv
                                     for r in range(PEER_ACC_VREGS))
                    accs = lax.fori_loop(0, D_MODEL // SC_LANES, body,
                                         tuple(jnp.zeros((SC_LANES,), jnp.float32) for _ in range(PEER_ACC_VREGS)))
                    for r in range(PEER_ACC_VREGS):
                        out_v[p, c * PEER_GATHER_ROWS + g * PEER_ACC_VREGS + r, :] = accs[r]
            else:
                for db in range(D_MODEL // (PEER_ACC_VREGS * SC_LANES)):
                    def body(kk, accs):
                        wv = aux_v[p, c * PEER_GATHER_ROWS + kk, :]
                        return tuple(accs[j] + rows_v[b, kk, pl.ds((db * PEER_ACC_VREGS + j) * SC_LANES, SC_LANES)] * wv
                                     for j in range(PEER_ACC_VREGS))
                    if c == 0:
                        init = tuple(jnp.zeros((SC_LANES,), jnp.float32) for _ in range(PEER_ACC_VREGS))
                    else:
                        init = tuple(out_v[p, pl.ds((db * PEER_ACC_VREGS + j) * SC_LANES, SC_LANES)]
                                     for j in range(PEER_ACC_VREGS))
                    accs = lax.fori_loop(0, PEER_GATHER_ROWS, body, init)
                    for j in range(PEER_ACC_VREGS):
                        out_v[p, pl.ds((db * PEER_ACC_VREGS + j) * SC_LANES, SC_LANES)] = accs[j]

        for d in load_meta(base, 0):
            d.start()
        for d in load_meta(base, 0):
            d.wait()
        gather(0, 0, 0).start()

        @pl.loop(0, tpw // 2)
        def _(i2):
            for p in range(2):
                i = i2 * 2 + p
                t = base + i
                nxt = base + jnp.minimum(i + 1, tpw - 1)
                for d in load_meta(nxt, 1 - p):
                    d.start()

                @pl.when(i2 > 0)
                def _():
                    store_out(t, p).wait()

                for c in range(PEER_GATHERS):
                    b = c % 2
                    if c < PEER_GATHERS - 1:
                        gather(p, c + 1, 1 - b).start()
                    else:
                        for d in load_meta(nxt, 1 - p):
                            d.wait()
                        gather(1 - p, 0, 0).start()
                    gather(p, c, b).wait()
                    compute(p, c, b)
                store_out(t, p).start()

        gather(0, 0, 0).wait()
        for p in range(2):
            store_out(base, p).wait()

    return k(table, idx.reshape(N, PEER_GATHERS, PEER_GATHER_ROWS), aux)


PEER_TOKENS = 256
INT_BIG = 2 ** 30


def _extract_topk(cand_ref, ids_ref, val_out_ref, id_out_ref, row0):
    def body(r, carry):
        c = cand_ref[...]
        ids = ids_ref[...]
        m = jnp.max(c, axis=0, keepdims=True)
        sel = jnp.min(jnp.where(c == m, ids, INT_BIG), axis=0, keepdims=True)
        cand_ref[...] = jnp.where(ids == sel, -jnp.inf, c)
        val_out_ref[pl.ds(row0 + r, 1), :] = m
        id_out_ref[pl.ds(row0 + r, 1), :] = sel
        return carry
    lax.fori_loop(0, PEER_TOPK, body, 0)


def _peer_retrieve_kernel(x_ref, gain_ref, scale_ref, shift_ref, wq_ref, keys_ref,
                          h_ref, idx_ref, gate_ref,
                          s_ref, ids1_ref, sv_ref, si_ref, cand_ref, cid_ref, ts_ref):
    x = x_ref[0]
    y = x * lax.rsqrt(jnp.mean(x * x, axis=-1, keepdims=True) + NORM_EPS)
    h = (y * gain_ref[...]) * (1.0 + scale_ref[0]) + shift_ref[0]
    h_ref[0] = h
    q = jnp.dot(h.astype(jnp.bfloat16), wq_ref[...], preferred_element_type=jnp.float32)
    T = PEER_TOKENS
    K = PEER_TOPK
    ids1_ref[...] = lax.broadcasted_iota(jnp.int32, (PEER_N_KEYS, T), 0)
    for hd in range(PEER_HEADS):
        for p in range(2):
            hp = hd * 2 + p
            qs = q[:, hp * PEER_HALF:(hp + 1) * PEER_HALF].astype(jnp.bfloat16)
            s_ref[...] = lax.dot_general(keys_ref[hp], qs, (((1,), (1,)), ((), ())),
                                         preferred_element_type=jnp.float32)
            _extract_topk(s_ref, ids1_ref, sv_ref, si_ref, p * K)
        for i in range(K):
            cand_ref[i * K:(i + 1) * K, :] = sv_ref[i:i + 1, :] + sv_ref[K:2 * K, :]
            cid_ref[i * K:(i + 1) * K, :] = si_ref[i:i + 1, :] * PEER_N_KEYS + si_ref[K:2 * K, :]
        _extract_topk(cand_ref, cid_ref, ts_ref, idx_ref, hd * K)
        ts = ts_ref[hd * K:(hd + 1) * K, :]
        e = jnp.exp(ts - jnp.max(ts, axis=0, keepdims=True))
        gate_ref[hd * K:(hd + 1) * K, :] = e / jnp.sum(e, axis=0, keepdims=True)


def peer_retrieve(x, gain, scale, shift, w_q, sub_keys):
    B, L, D = x.shape
    T = PEER_TOKENS
    nt = L // T
    keys = sub_keys.reshape(PEER_HEADS * 2, PEER_N_KEYS, PEER_HALF).astype(jnp.bfloat16)
    return pl.pallas_call(
        _peer_retrieve_kernel,
        grid=(B, nt),
        in_specs=[
            pl.BlockSpec((1, T, D), lambda b, i: (b, i, 0)),
            pl.BlockSpec((1, D), lambda b, i: (0, 0)),
            pl.BlockSpec((1, 1, D), lambda b, i: (b, 0, 0)),
            pl.BlockSpec((1, 1, D), lambda b, i: (b, 0, 0)),
            pl.BlockSpec((D, PEER_HEADS * 2 * PEER_HALF), lambda b, i: (0, 0)),
            pl.BlockSpec((PEER_HEADS * 2, PEER_N_KEYS, PEER_HALF), lambda b, i: (0, 0, 0)),
        ],
        out_specs=[
            pl.BlockSpec((1, T, D), lambda b, i: (b, i, 0)),
            pl.BlockSpec((PEER_SLOTS, T), lambda b, i: (0, b * nt + i)),
            pl.BlockSpec((PEER_SLOTS, T), lambda b, i: (0, b * nt + i)),
        ],
        out_shape=[
            jax.ShapeDtypeStruct((B, L, D), jnp.float32),
            jax.ShapeDtypeStruct((PEER_SLOTS, B * L), jnp.int32),
            jax.ShapeDtypeStruct((PEER_SLOTS, B * L), jnp.float32),
        ],
        scratch_shapes=[
            pltpu.VMEM((PEER_N_KEYS, T), jnp.float32),
            pltpu.VMEM((PEER_N_KEYS, T), jnp.int32),
            pltpu.VMEM((2 * PEER_TOPK, T), jnp.float32),
            pltpu.VMEM((2 * PEER_TOPK, T), jnp.int32),
            pltpu.VMEM((PEER_TOPK * PEER_TOPK, T), jnp.float32),
            pltpu.VMEM((PEER_TOPK * PEER_TOPK, T), jnp.int32),
            pltpu.VMEM((PEER_SLOTS, T), jnp.float32),
        ],
        compiler_params=pltpu.CompilerParams(dimension_semantics=("parallel", "parallel"),
                                             vmem_limit_bytes=VMEM_LIMIT_BYTES),
        name="peer_retrieve",
    )(x, gain.reshape(1, D), scale, shift, w_q.astype(jnp.bfloat16), keys)


def peer_ffn(x, gain, scale, shift, w_q, sub_keys, exp_u, exp_v):
    B, L, D = x.shape
    N = B * L
    h, idx_t, gate_t = peer_retrieve(x, gain, scale, shift, w_q, sub_keys)
    hf = h.reshape(N, D)
    e_idx = idx_t.T
    gate = gate_t.T
    act = jax.nn.gelu(jnp.sum(_sc_peer(exp_u, e_idx, hf, "dot"), axis=-1), approximate=False)
    w16 = jnp.broadcast_to((gate * act)[:, :, None], (N, PEER_SLOTS, SC_LANES))
    return _sc_peer(exp_v, e_idx, w16, "wsum").reshape(B, L, D)


def _forward_group(x, c, ctx, c_ctx, mod_w, mod_b, mix_norm, w_in, w_out, rw_conv, rw_decay_up, rw_decay0, rw_a_up, rw_a0, rw_gate_up, rw_k_k, rw_k_a, rw_r_k, rw_gn_g, rw_gn_b, mla_q_norm, mla_w_uq, mla_kv_norm, mla_w_ukv, mla_q_gain, mla_k_gain, hy_conv, hy_w1, hy_b1, hy_freq1, hy_w2, hy_b2, hy_freq2, hy_w3, hy_b3, hy_bias, ffn_norm, peer_wq, peer_keys, peer_u, peer_v):
    B, L, D = x.shape
    rope = axial_rope_tables(L)
    s_rw, s_mla = RW_PROJ, RW_PROJ + MLA_PROJ
    for li in range(DEPTH):
        need_ctx = li < DEPTH - 1
        mod_l = (jax.nn.silu(c) @ mod_w[li] + mod_b[li])[:, None, :]
        mod_c = (jax.nn.silu(c_ctx) @ mod_w[li] + mod_b[li])[None, None, :]
        shm_l, scm_l, gm_l, shf_l, scf_l, gf_l = jnp.split(mod_l, N_MOD, axis=-1)
        shm_c, scm_c, gm_c, shf_c, scf_c, gf_c = jnp.split(mod_c, N_MOD, axis=-1)

        p_l = norm_mod_proj(x, mix_norm[li], scm_l, shm_l, w_in[li], 512)
        p_c = norm_mod_proj(ctx, mix_norm[li], jnp.broadcast_to(scm_c, (B, 1, D)),
                            jnp.broadcast_to(shm_c, (B, 1, D)), w_in[li], 256)
        rw_l, rw_c = rwkv7_mixer(p_l[..., :s_rw], p_c[..., :s_rw], rw_conv[li], rw_decay_up[li], rw_decay0[li],
                                 rw_a_up[li], rw_a0[li], rw_gate_up[li], rw_k_k[li], rw_k_a[li], rw_r_k[li],
                                 rw_gn_g[li], rw_gn_b[li], need_ctx)
        ml_l, ml_c = mla_mixer(p_l[..., s_rw:s_mla], p_c[..., s_rw:s_mla], rope, mla_q_norm[li], mla_w_uq[li],
                               mla_kv_norm[li], mla_w_ukv[li], mla_q_gain[li], mla_k_gain[li], need_ctx)
        hy_prm = (hy_conv[li], hy_w1[li], hy_b1[li], hy_freq1[li], hy_w2[li], hy_b2[li], hy_freq2[li],
                  hy_w3[li], hy_b3[li], hy_bias[li])
        hy_l = hyena_mixer(p_l[..., s_mla:], *hy_prm)
        x = x + gm_l * (jnp.concatenate([rw_l, ml_l, hy_l], axis=-1) @ w_out[li])
        x = x + gf_l * peer_ffn(x, ffn_norm[li], scf_l, shf_l,
                                peer_wq[li], peer_keys[li], peer_u[li], peer_v[li])
        if need_ctx:
            hy_c = hyena_mixer(p_c[..., s_mla:], *hy_prm)
            ctx = ctx + gm_c * (jnp.concatenate([rw_c, ml_c, hy_c], axis=-1) @ w_out[li])
            ctx = ctx + gf_c * peer_ffn(ctx, ffn_norm[li], jnp.broadcast_to(scf_c, (B, 1, D)),
                                        jnp.broadcast_to(shf_c, (B, 1, D)),
                                        peer_wq[li], peer_keys[li], peer_u[li], peer_v[li])
    return x


BATCH_GROUPS = 2


def kernel(x, c, ctx, c_ctx, mod_w, mod_b, mix_norm, w_in, w_out, rw_conv, rw_decay_up, rw_decay0, rw_a_up, rw_a0, rw_gate_up, rw_k_k, rw_k_a, rw_r_k, rw_gn_g, rw_gn_b, mla_q_norm, mla_w_uq, mla_kv_norm, mla_w_ukv, mla_q_gain, mla_k_gain, hy_conv, hy_w1, hy_b1, hy_freq1, hy_w2, hy_b2, hy_freq2, hy_w3, hy_b3, hy_bias, ffn_norm, peer_wq, peer_keys, peer_u, peer_v):
    params = (mod_w, mod_b, mix_norm, w_in, w_out, rw_conv, rw_decay_up, rw_decay0, rw_a_up, rw_a0, rw_gate_up,
              rw_k_k, rw_k_a, rw_r_k, rw_gn_g, rw_gn_b, mla_q_norm, mla_w_uq, mla_kv_norm, mla_w_ukv, mla_q_gain,
              mla_k_gain, hy_conv, hy_w1, hy_b1, hy_freq1, hy_w2, hy_b2, hy_freq2, hy_w3, hy_b3, hy_bias,
              ffn_norm, peer_wq, peer_keys, peer_u, peer_v)
    bg = x.shape[0] // BATCH_GROUPS
    outs = [_forward_group(x[g * bg:(g + 1) * bg], c[g * bg:(g + 1) * bg], ctx[g * bg:(g + 1) * bg], c_ctx, *params)
            for g in range(BATCH_GROUPS)]
    return jnp.concatenate(outs, axis=0)
```

```python
import functools
import math

import jax
import jax.numpy as jnp
import numpy as np
from jax import lax
from jax.experimental import pallas as pl
from jax.experimental.pallas import tpu as pltpu
from jax.experimental.pallas import tpu_sc as plsc

D_MODEL = 1024
DEPTH = 2
GRID_W = 64
N_MOD = 6
NORM_EPS = 1e-6

RW_HEADS = 6
RW_HEAD_DIM = 64
RW_WIDTH = RW_HEADS * RW_HEAD_DIM
RW_DECAY_RANK = 64
RW_A_RANK = 64
RW_GATE_RANK = 128
RW_DECAY_SCALE = 0.6065306597
RW_GN_EPS = 64e-5
L2_EPS = 1e-12

MLA_HEADS = 6
MLA_Q_RANK = 256
MLA_KV_RANK = 128
MLA_NOPE_DIM = 64
MLA_ROPE_DIM = 32
MLA_V_DIM = 64
MLA_QK_DIM = MLA_NOPE_DIM + MLA_ROPE_DIM
MLA_WIDTH = MLA_HEADS * MLA_V_DIM
AXIS_ROPE_DIM = MLA_ROPE_DIM // 2
ROPE_THETA = 10000.0
Q_BLOCK = 128

HY_WIDTH = 256
HY_ORDER = 2
HY_POS_BANDS = 16
HY_SHORT_DECAY_PCT = 0.3
HY_LONG_DECAY_PCT = 1.5
HY_DECAY_TARGET = 1e-2

PEER_HEADS = 8
PEER_N_KEYS = 128
PEER_TOPK = 16
PEER_QUERY_DIM = 256
PEER_HALF = PEER_QUERY_DIM // 2

RW_PROJ = 3 * RW_WIDTH + RW_DECAY_RANK + RW_A_RANK + RW_GATE_RANK
MLA_PROJ = MLA_Q_RANK + MLA_KV_RANK + MLA_ROPE_DIM
HY_PROJ = (HY_ORDER + 1) * HY_WIDTH
IN_PROJ = RW_PROJ + MLA_PROJ + HY_PROJ
MIX_WIDTH = RW_WIDTH + MLA_WIDTH + HY_WIDTH

VMEM_LIMIT_BYTES = 48 * 1024 * 1024


def _norm_mod_proj_kernel(x_ref, gain_ref, scale_ref, shift_ref, w_ref, o_ref):
    x = x_ref[0]
    y = x * lax.rsqrt(jnp.mean(x * x, axis=-1, keepdims=True) + NORM_EPS)
    y = y * gain_ref[...]
    y = y * (1.0 + scale_ref[0]) + shift_ref[0]
    o_ref[0] = jnp.dot(y.astype(jnp.bfloat16), w_ref[...], preferred_element_type=jnp.float32)


def norm_mod_proj(x, gain, scale, shift, w, block_rows):
    B, L, D = x.shape
    N = w.shape[1]
    return pl.pallas_call(
        _norm_mod_proj_kernel,
        grid=(B, L // block_rows),
        in_specs=[
            pl.BlockSpec((1, block_rows, D), lambda b, i: (b, i, 0)),
            pl.BlockSpec((1, D), lambda b, i: (0, 0)),
            pl.BlockSpec((1, 1, D), lambda b, i: (b, 0, 0)),
            pl.BlockSpec((1, 1, D), lambda b, i: (b, 0, 0)),
            pl.BlockSpec((D, N), lambda b, i: (0, 0)),
        ],
        out_specs=pl.BlockSpec((1, block_rows, N), lambda b, i: (b, i, 0)),
        out_shape=jax.ShapeDtypeStruct((B, L, N), jnp.float32),
        compiler_params=pltpu.CompilerParams(
            dimension_semantics=("parallel", "parallel"), vmem_limit_bytes=VMEM_LIMIT_BYTES),
        name="norm_mod_proj",
    )(x, gain.reshape(1, D), scale, shift, w.astype(jnp.bfloat16))


RW_CHUNK = 64


def _rwkv_chunk_kernel(r_ref, kk_ref, v_ref, lw_ref, akk_ref, kr_ref, y_ref, h_ref):
    d = pl.program_id(0)
    n = pl.program_id(2)

    @pl.when(n == 0)
    def _():
        h_ref[...] = jnp.zeros_like(h_ref)

    C = RW_CHUNK
    row = lax.broadcasted_iota(jnp.int32, (C, C), 0)
    col = lax.broadcasted_iota(jnp.int32, (C, C), 1)
    lag = (row - col) * (1 - 2 * d)
    before = lag > 0
    upto = lag >= 0
    tri = upto.astype(jnp.float32)
    eye = (row == col).astype(jnp.float32)
    bf = jnp.bfloat16
    f32 = jnp.float32

    def mm(a, b):
        return jnp.dot(a.astype(bf), b.astype(bf), preferred_element_type=f32)

    def mm_nt(a, b):
        return lax.dot_general(a.astype(bf), b.astype(bf), (((1,), (1,)), ((), ())), preferred_element_type=f32)

    def mm_tn(a, b):
        return lax.dot_general(a.astype(bf), b.astype(bf), (((0,), (0,)), ((), ())), preferred_element_type=f32)

    hs = range(RW_HEADS)
    HD = RW_HEAD_DIM
    r = [r_ref[0, h] for h in hs]
    kk = [kk_ref[0, h] for h in hs]
    v = [v_ref[0, h] for h in hs]
    lw = [lw_ref[0, 0, h] for h in hs]
    akk = [akk_ref[0, 0, h] for h in hs]
    kr = [kr_ref[0, 0, h] for h in hs]
    G = [jnp.dot(tri, lw[h], preferred_element_type=f32, precision=lax.Precision.HIGHEST) for h in hs]
    gtot = [jnp.sum(lw[h], axis=0, keepdims=True) for h in hs]
    Einv = [jnp.exp(-G[h]) for h in hs]
    At = [-kk[h] * jnp.exp(G[h] - lw[h]) for h in hs]
    Rt = [r[h] * jnp.exp(G[h]) for h in hs]
    Bt = [akk[h] * Einv[h] for h in hs]
    Kt = [kr[h] * Einv[h] for h in hs]
    X = [mm_nt(jnp.concatenate([At[h], Rt[h]], axis=0), jnp.concatenate([Bt[h], Kt[h]], axis=0)) for h in hs]
    M_ab = [jnp.where(before, X[h][:C, :C], 0.0) for h in hs]
    M_ak = [jnp.where(before, X[h][:C, C:], 0.0) for h in hs]
    A_rb = [jnp.where(upto, X[h][C:, :C], 0.0) for h in hs]
    A_rk = [jnp.where(upto, X[h][C:, C:], 0.0) for h in hs]
    MV = [mm(M_ak[h], v[h]) for h in hs]
    Mp = M_ab
    T = [eye + Mp[h] for h in hs]
    for _ in range(5):
        Mp = [jnp.dot(Mp[h], Mp[h], preferred_element_type=f32) for h in hs]
        T = [T[h] + jnp.dot(T[h], Mp[h], preferred_element_type=f32) for h in hs]
    WU = [jnp.dot(T[h], jnp.concatenate([At[h], MV[h]], axis=1), preferred_element_type=f32) for h in hs]
    H0 = [h_ref[h] for h in hs]
    Ehat = [jnp.exp(gtot[h] - G[h]) for h in hs]
    Om = [Rt[h] + mm(A_rb[h], WU[h][:, :HD]) for h in hs]
    Y0 = [mm(A_rb[h], WU[h][:, HD:]) + mm(A_rk[h], v[h]) for h in hs]
    BW = [mm_tn(akk[h] * Ehat[h], WU[h]) for h in hs]
    KV = [mm_tn(kr[h] * Ehat[h], v[h]) for h in hs]
    for h in hs:
        y_ref[0, 0, h] = jnp.dot(Om[h], H0[h], preferred_element_type=f32) + Y0[h]
    for h in hs:
        P = eye * jnp.exp(gtot[h]) + BW[h][:, :HD]
        h_ref[h] = jnp.dot(P, H0[h], preferred_element_type=f32) + BW[h][:, HD:] + KV[h]


def rwkv_chunked(r, kk, v, lw, akk, kr, n_ctx):
    B, H, T, _ = r.shape
    nc = n_ctx // RW_CHUNK
    nt = T // RW_CHUNK

    def chunk_of(d, n):
        bwd = jnp.where(n < nc, nc - 1 - n, nt - 1 - (n - nc))
        return jnp.where(d == 0, n, bwd)

    spec1 = pl.BlockSpec((1, H, RW_CHUNK, RW_HEAD_DIM), lambda d, b, n: (b, 0, chunk_of(d, n), 0))
    spec2 = pl.BlockSpec((1, 1, H, RW_CHUNK, RW_HEAD_DIM), lambda d, b, n: (d, b, 0, chunk_of(d, n), 0))
    return pl.pallas_call(
        _rwkv_chunk_kernel,
        grid=(2, B, nt),
        in_specs=[spec1, spec1, spec1, spec2, spec2, spec2],
        out_specs=spec2,
        out_shape=jax.ShapeDtypeStruct((2, B, H, T, RW_HEAD_DIM), jnp.float32),
        scratch_shapes=[pltpu.VMEM((H, RW_HEAD_DIM, RW_HEAD_DIM), jnp.float32)],
        compiler_params=pltpu.CompilerParams(dimension_semantics=("parallel", "parallel", "arbitrary")),
        name="rwkv_chunked",
    )(r, kk, v, lw, akk, kr)


def rms_norm(x, gain):
    y = x * lax.rsqrt(jnp.mean(x * x, axis=-1, keepdims=True) + NORM_EPS)
    return y * gain


def short_conv(x, w):
    xp = jnp.pad(x, ((0, 0), (1, 1), (0, 0)))
    return xp[:, :-2] * w[0] + xp[:, 1:-1] * w[1] + xp[:, 2:] * w[2]


def _heads(t):
    return t.reshape(t.shape[:-1] + (RW_HEADS, RW_HEAD_DIM))


def rwkv7_prepare(p, conv_w, decay_up, decay0, a_up, a0, gate_up, k_k, k_a):
    z = short_conv(p, conv_w)
    o1, o2, o3 = RW_WIDTH, 2 * RW_WIDTH, 3 * RW_WIDTH
    o4 = o3 + RW_DECAY_RANK
    o5 = o4 + RW_A_RANK
    r, k, v = z[..., :o1], z[..., o1:o2], z[..., o2:o3]
    d_lo, a_lo, g_lo = z[..., o3:o4], z[..., o4:o5], z[..., o5:]
    log_decay = -RW_DECAY_SCALE * jax.nn.sigmoid(
        decay0[:, None, None, :] + jnp.einsum('blr,nrc->nblc', jnp.tanh(d_lo), decay_up))
    a = jax.nn.sigmoid(a0[:, None, None, :] + jnp.einsum('blr,nrc->nblc', a_lo, a_up))
    g = jnp.einsum('blr,rc->blc', jax.nn.sigmoid(g_lo), gate_up)
    kk = _heads(k * k_k)
    kk = kk * lax.rsqrt(jnp.sum(kk * kk, axis=-1, keepdims=True) + L2_EPS)
    k_rep = _heads(k)[None] * (1.0 + (_heads(a) - 1.0) * _heads(k_a))
    return _heads(r), _heads(k), _heads(v), _heads(log_decay), kk[None] * _heads(a), kk, k_rep, g


def rwkv7_readout(y, r, k, v, g, r_k, gn_g, gn_b):
    B, L = y.shape[:2]
    mu = jnp.mean(y, axis=-1, keepdims=True)
    var = jnp.mean(jnp.square(y - mu), axis=-1, keepdims=True)
    yn = ((y - mu) * lax.rsqrt(var + RW_GN_EPS)).reshape(B, L, RW_WIDTH) * gn_g + gn_b
    bonus = jnp.sum(r * k * r_k.reshape(RW_HEADS, RW_HEAD_DIM), axis=-1, keepdims=True) * v
    return (yn + bonus.reshape(B, L, RW_WIDTH)) * g


def rwkv7_mixer(p_lat, p_ctx, conv_w, decay_up, decay0, a_up, a0, gate_up, k_k, k_a, r_k, gn_g, gn_b, need_ctx):
    prm = (conv_w, decay_up, decay0, a_up, a0, gate_up, k_k, k_a)
    lat = rwkv7_prepare(p_lat, *prm)
    ctx = rwkv7_prepare(p_ctx, *prm)
    n_ctx = p_ctx.shape[1]

    def seq(i):
        t = jnp.concatenate([ctx[i], lat[i]], axis=-3)
        return jnp.swapaxes(t, -3, -2)

    y = rwkv_chunked(seq(0), seq(5), seq(2), seq(3), seq(4), seq(6), n_ctx)
    y = jnp.swapaxes(y[0] + y[1], 1, 2)
    lr, lk, lv, _, _, _, _, lg = lat
    cr, ck, cv, _, _, _, _, cg = ctx
    out_l = rwkv7_readout(y[:, n_ctx:], lr, lk, lv, lg, r_k, gn_g, gn_b)
    out_c = rwkv7_readout(y[:, :n_ctx], cr, ck, cv, cg, r_k, gn_g, gn_b) if need_ctx else None
    return out_l, out_c


def axial_rope_tables(L):
    rows = L // GRID_W
    row, col = jnp.meshgrid(jnp.arange(rows), jnp.arange(GRID_W), indexing='ij')
    inv = ROPE_THETA ** (-jnp.arange(0, AXIS_ROPE_DIM, 2, dtype=jnp.float32) / AXIS_ROPE_DIM)
    pos = jnp.stack([row.reshape(-1), col.reshape(-1)], axis=-1).astype(jnp.float32)
    ang = pos[:, :, None] * inv[None, None, :]
    return jnp.cos(ang), jnp.sin(ang)


def apply_axial_rope(x, rope):
    cos, sin = rope
    B, L, H, _ = x.shape
    xa = x.reshape(B, L, H, 2, AXIS_ROPE_DIM)
    half = AXIS_ROPE_DIM // 2
    x1, x2 = xa[..., :half], xa[..., half:]
    cs, sn = cos[None, :, None], sin[None, :, None]
    out = jnp.concatenate([x1 * cs - x2 * sn, x2 * cs + x1 * sn], axis=-1)
    return out.reshape(B, L, H, MLA_ROPE_DIM)


def mla_qkv(p, rope, q_norm, w_uq, kv_norm, w_ukv, q_gain, k_gain):
    B, L, _ = p.shape
    c_q = p[..., :MLA_Q_RANK]
    c_kv = p[..., MLA_Q_RANK:MLA_Q_RANK + MLA_KV_RANK]
    k_rope = p[..., MLA_Q_RANK + MLA_KV_RANK:]
    q = (rms_norm(c_q, q_norm) @ w_uq).reshape(B, L, MLA_HEADS, MLA_QK_DIM)
    kv = (rms_norm(c_kv, kv_norm) @ w_ukv).reshape(B, L, MLA_HEADS, MLA_NOPE_DIM + MLA_V_DIM)
    k = jnp.concatenate([kv[..., :MLA_NOPE_DIM],
                         jnp.broadcast_to(k_rope[:, :, None, :], (B, L, MLA_HEADS, MLA_ROPE_DIM))], axis=-1)
    v = kv[..., MLA_NOPE_DIM:]
    q = rms_norm(q, q_gain)
    k = rms_norm(k, k_gain)
    if rope is not None:
        q = jnp.concatenate([q[..., :MLA_NOPE_DIM], apply_axial_rope(q[..., MLA_NOPE_DIM:], rope)], axis=-1)
        k = jnp.concatenate([k[..., :MLA_NOPE_DIM], apply_axial_rope(k[..., MLA_NOPE_DIM:], rope)], axis=-1)
    return q, k, v


ATTN_Q_ROWS = 512


def _attn_kernel(q_ref, k_ref, v_ref, o_ref):
    q = (q_ref[0, 0] * (MLA_QK_DIM ** -0.5)).astype(jnp.bfloat16)
    k = k_ref[0, 0].astype(jnp.bfloat16)
    s = lax.dot_general(q, k, (((1,), (1,)), ((), ())), preferred_element_type=jnp.float32)
    e = jnp.exp(s - jnp.max(s, axis=-1, keepdims=True))
    o = jnp.dot(e.astype(jnp.bfloat16), v_ref[0, 0].astype(jnp.bfloat16), preferred_element_type=jnp.float32)
    o_ref[0, 0] = o / jnp.sum(e, axis=-1, keepdims=True)


def attention(q, k, v):
    B, H, Lq, dk = q.shape
    Lk, dv = v.shape[2], v.shape[3]
    tq = min(ATTN_Q_ROWS, Lq)
    return pl.pallas_call(
        _attn_kernel,
        grid=(B, H, Lq // tq),
        in_specs=[
            pl.BlockSpec((1, 1, tq, dk), lambda b, h, i: (b, h, i, 0)),
            pl.BlockSpec((1, 1, Lk, dk), lambda b, h, i: (b, h, 0, 0)),
            pl.BlockSpec((1, 1, Lk, dv), lambda b, h, i: (b, h, 0, 0)),
        ],
        out_specs=pl.BlockSpec((1, 1, tq, dv), lambda b, h, i: (b, h, i, 0)),
        out_shape=jax.ShapeDtypeStruct((B, H, Lq, dv), jnp.float32),
        compiler_params=pltpu.CompilerParams(dimension_semantics=("parallel", "parallel", "parallel"),
                                             vmem_limit_bytes=VMEM_LIMIT_BYTES),
        name="mla_attention",
    )(q, k, v)


def mla_mixer(p_lat, p_ctx, rope, q_norm, w_uq, kv_norm, w_ukv, q_gain, k_gain, need_ctx):
    prm = (q_norm, w_uq, kv_norm, w_ukv, q_gain, k_gain)
    q_l, k_l, v_l = (jnp.swapaxes(t, 1, 2) for t in mla_qkv(p_lat, rope, *prm))
    q_c, k_c, v_c = (jnp.swapaxes(t, 1, 2) for t in mla_qkv(p_ctx, None, *prm))
    k_all = jnp.concatenate([k_l, k_c], axis=2)
    v_all = jnp.concatenate([v_l, v_c], axis=2)
    B, L = p_lat.shape[:2]
    y_l = jnp.swapaxes(attention(q_l, k_all, v_all), 1, 2).reshape(B, L, MLA_WIDTH)
    y_c = jnp.swapaxes(attention(q_c, k_c, v_c), 1, 2).reshape(B, p_ctx.shape[1], MLA_WIDTH) if need_ctx else None
    return y_l, y_c


def hyena_filters(L, w1, b1, freq1, w2, b2, freq2, w3, b3):
    tn = jnp.arange(L, dtype=jnp.float32) / L
    bands = jnp.arange(1, HY_POS_BANDS + 1, dtype=jnp.float32)
    ang = 2.0 * math.pi * tn[:, None] * bands[None, :]
    z = jnp.concatenate([tn[:, None], jnp.cos(ang), jnp.sin(ang)], axis=-1)
    h = jnp.sin(freq1 * (z @ w1 + b1))
    h = jnp.sin(freq2 * (h @ w2 + b2))
    h = (h @ w3 + b3).reshape(L, HY_ORDER, 2, HY_WIDTH)
    rates = jnp.abs(jnp.linspace(math.log(HY_DECAY_TARGET) / HY_LONG_DECAY_PCT,
                                 math.log(HY_DECAY_TARGET) / HY_SHORT_DECAY_PCT, HY_WIDTH))
    h = h * jnp.exp(-tn[:, None] * rates[None, :])[:, None, None, :]
    zero = jnp.zeros((1, HY_ORDER, HY_WIDTH), h.dtype)
    h_full = jnp.concatenate([h[:, :, 0], zero, h[:0:-1, :, 1]], axis=0)
    return h_full * lax.rsqrt(jnp.sum(jnp.square(h_full), axis=0, keepdims=True))


def fft_long_conv(u, h_full, bias):
    L = u.shape[1]
    uf = jnp.fft.rfft(u, n=2 * L, axis=1)
    hf = jnp.fft.rfft(h_full, n=2 * L, axis=0)
    y = jnp.fft.irfft(uf * hf[None], n=2 * L, axis=1)[:, :L]
    return y + u * bias


FFT_N1 = 64
FFT_N2 = 128
FFT_N = FFT_N1 * FFT_N2
HY_SEQS = 32


def _dft_tables(seqs):
    n1 = np.arange(FFT_N1)
    n2 = np.arange(FFT_N2)
    f64 = np.exp(-2j * np.pi * np.outer(n1, n1) / FFT_N1)
    f128 = np.exp(-2j * np.pi * np.outer(n2, n2) / FFT_N2)
    tw = np.exp(-2j * np.pi * np.outer(n1, n2) / FFT_N)
    half = FFT_N1 // 2
    fh = f64[:, :half]
    m1 = np.block([[fh.real, -fh.imag], [fh.imag, fh.real]])
    m1f = np.concatenate([f64.real, f64.imag], axis=0)
    m2 = np.block([[f128.real, f128.imag], [-f128.imag, f128.real]])
    m3 = np.block([[f128.real, -f128.imag], [f128.imag, f128.real]]) / FFT_N
    c = np.conj(f64)[:half, :]
    m4 = np.block([[c.real, -c.imag], [c.imag, c.real]])
    bf = lambda a: jnp.asarray(a, jnp.float32).astype(jnp.bfloat16)
    f32 = lambda a: jnp.asarray(a, jnp.float32)
    return dict(m1=bf(m1), m1f=bf(m1f), m2=bf(m2), m3=bf(m3), m4=bf(m4),
                twr_l=f32(np.tile(tw.real, (1, seqs))), twi_l=f32(np.tile(tw.imag, (1, seqs))),
                twr_s=f32(np.tile(tw.real, (seqs, 1))), twi_s=f32(np.tile(tw.imag, (seqs, 1))))


def _spectrum(cols, m1, twr_l, twi_l, m2, R):
    a = jnp.dot(m1, cols.astype(jnp.bfloat16), preferred_element_type=jnp.float32)
    ar, ai = a[:FFT_N1], a[FFT_N1:]
    pr = ar * twr_l - ai * twi_l
    pi = ar * twi_l + ai * twr_l
    lhs = jnp.concatenate(
        [jnp.concatenate([pr[:, r * FFT_N2:(r + 1) * FFT_N2], pi[:, r * FFT_N2:(r + 1) * FFT_N2]], axis=1)
         for r in range(R)], axis=0)
    return jnp.dot(lhs.astype(jnp.bfloat16), m2, preferred_element_type=jnp.float32)


def _filter_fft_kernel(h_ref, m1f_ref, twr_ref, twi_ref, m2_ref, o_ref):
    R = HY_SEQS
    cols = jnp.concatenate([h_ref[r] for r in range(R)], axis=1)
    x = _spectrum(cols, m1f_ref[...], twr_ref[...], twi_ref[...], m2_ref[...], R)
    o_ref[...] = x.reshape(R, FFT_N1, 2 * FFT_N2)


def _hyena_conv_kernel(y_ref, g_ref, hf_ref, bias_ref, m1_ref, twr_l_ref, twi_l_ref, m2_ref, m3_ref,
                       twr_s_ref, twi_s_ref, m4_ref, o_ref):
    R = HY_SEQS
    half = FFT_N1 // 2
    y = [y_ref[0], y_ref[1]]
    for o in range(HY_ORDER):
        top = jnp.concatenate([y[0][r] for r in range(R)], axis=1)
        bot = jnp.concatenate([y[1][r] for r in range(R)], axis=1)
        x = _spectrum(jnp.concatenate([top, bot], axis=0), m1_ref[...], twr_l_ref[...], twi_l_ref[...], m2_ref[...], R)
        hf = hf_ref[o].reshape(R * FFT_N1, 2 * FFT_N2)
        xr, xi = x[:, :FFT_N2], x[:, FFT_N2:]
        hr, hi = hf[:, :FFT_N2], hf[:, FFT_N2:]
        yc = jnp.concatenate([xr * hr - xi * hi, xr * hi + xi * hr], axis=1)
        b = jnp.dot(yc.astype(jnp.bfloat16), m3_ref[...], preferred_element_type=jnp.float32)
        br, bi = b[:, :FFT_N2], b[:, FFT_N2:]
        qr = br * twr_s_ref[...] + bi * twi_s_ref[...]
        qi = bi * twr_s_ref[...] - br * twi_s_ref[...]
        bc = jnp.concatenate(
            [jnp.concatenate([qr[r * FFT_N1:(r + 1) * FFT_N1], qi[r * FFT_N1:(r + 1) * FFT_N1]], axis=0)
             for r in range(R)], axis=1)
        yo = jnp.dot(m4_ref[...], bc.astype(jnp.bfloat16), preferred_element_type=jnp.float32)
        for p in range(2):
            conv = jnp.stack([yo[p * half:(p + 1) * half, r * FFT_N2:(r + 1) * FFT_N2] for r in range(R)], axis=0)
            y[p] = g_ref[o, p] * (conv + y[p] * bias_ref[o])
    o_ref[0] = y[0]
    o_ref[1] = y[1]


def hyena_long_conv(y_t, g_t, h_t, bias):
    B, C, L = y_t.shape
    assert 2 * L == FFT_N and B % 2 == 0 and C % HY_SEQS == 0
    R = HY_SEQS
    half = FFT_N1 // 2
    tb = _dft_tables(R)
    const = lambda a: pl.BlockSpec(a.shape, lambda *_: (0,) * a.ndim)
    hf = pl.pallas_call(
        _filter_fft_kernel,
        grid=(HY_ORDER * C // R,),
        in_specs=[pl.BlockSpec((R, FFT_N1, FFT_N2), lambda i: (i, 0, 0)),
                  const(tb['m1f']), const(tb['twr_l']), const(tb['twi_l']), const(tb['m2'])],
        out_specs=pl.BlockSpec((R, FFT_N1, 2 * FFT_N2), lambda i: (i, 0, 0)),
        out_shape=jax.ShapeDtypeStruct((HY_ORDER * C, FFT_N1, 2 * FFT_N2), jnp.float32),
        compiler_params=pltpu.CompilerParams(dimension_semantics=("parallel",), vmem_limit_bytes=VMEM_LIMIT_BYTES),
        name="hyena_filter_fft",
    )(h_t.reshape(HY_ORDER * C, FFT_N1, FFT_N2), tb['m1f'], tb['twr_l'], tb['twi_l'], tb['m2'])
    hf = hf.reshape(HY_ORDER, C, FFT_N1, 2 * FFT_N2)
    out = pl.pallas_call(
        _hyena_conv_kernel,
        grid=(B // 2, C // R),
        in_specs=[pl.BlockSpec((2, R, half, FFT_N2), lambda b, c: (b, c, 0, 0)),
                  pl.BlockSpec((HY_ORDER, 2, R, half, FFT_N2), lambda b, c: (0, b, c, 0, 0)),
                  pl.BlockSpec((HY_ORDER, R, FFT_N1, 2 * FFT_N2), lambda b, c: (0, c, 0, 0)),
                  pl.BlockSpec((HY_ORDER, R, 1, 1), lambda b, c: (0, c, 0, 0)),
                  const(tb['m1']), const(tb['twr_l']), const(tb['twi_l']), const(tb['m2']), const(tb['m3']),
                  const(tb['twr_s']), const(tb['twi_s']), const(tb['m4'])],
        out_specs=pl.BlockSpec((2, R, half, FFT_N2), lambda b, c: (b, c, 0, 0)),
        out_shape=jax.ShapeDtypeStruct((B, C, half, FFT_N2), jnp.float32),
        compiler_params=pltpu.CompilerParams(dimension_semantics=("parallel", "parallel"),
                                             vmem_limit_bytes=VMEM_LIMIT_BYTES),
        name="hyena_conv",
    )(y_t.reshape(B, C, half, FFT_N2), g_t.reshape(HY_ORDER, B, C, half, FFT_N2), hf,
      bias.reshape(HY_ORDER, C, 1, 1), tb['m1'], tb['twr_l'], tb['twi_l'], tb['m2'], tb['m3'],
      tb['twr_s'], tb['twi_s'], tb['m4'])
    return out.reshape(B, C, L)


def hyena_mixer(p, conv_w, w1, b1, freq1, w2, b2, freq2, w3, b3, bias):
    B, L = p.shape[:2]
    z = short_conv(p, conv_w)
    h_full = hyena_filters(L, w1, b1, freq1, w2, b2, freq2, w3, b3)
    if 2 * L == FFT_N:
        g_t = jnp.transpose(z[..., :HY_ORDER * HY_WIDTH].reshape(B, L, HY_ORDER, HY_WIDTH), (2, 0, 3, 1))
        y_t = jnp.swapaxes(z[..., HY_ORDER * HY_WIDTH:], 1, 2)
        y_t = hyena_long_conv(y_t, g_t, jnp.transpose(h_full, (1, 2, 0)), bias)
        return jnp.swapaxes(y_t, 1, 2)
    gates = (z[..., :HY_WIDTH], z[..., HY_WIDTH:2 * HY_WIDTH])
    y = z[..., 2 * HY_WIDTH:]
    for o in range(HY_ORDER):
        y = gates[o] * fft_long_conv(y, h_full[:, o], bias[o])
    return y


SC_CORES = 2
SC_SUBCORES = 16
SC_LANES = 16
SC_WORKERS = SC_CORES * SC_SUBCORES
PEER_SLOTS = PEER_HEADS * PEER_TOPK
PEER_GATHER_ROWS = 32
PEER_GATHERS = PEER_SLOTS // PEER_GATHER_ROWS
PEER_ACC_VREGS = 8


def _sc_peer(table, idx, aux, phase):
    N = idx.shape[0]
    tpw = N // SC_WORKERS
    assert tpw % 2 == 0 and N % SC_WORKERS == 0
    mesh = plsc.VectorSubcoreMesh(core_axis_name="c", subcore_axis_name="s")
    aux_shape = (D_MODEL,) if phase == "dot" else (PEER_SLOTS,)
    out_tok = (PEER_SLOTS,) if phase == "dot" else (D_MODEL,)

    @functools.partial(
        pl.kernel, mesh=mesh,
        out_type=jax.ShapeDtypeStruct((N,) + out_tok, jnp.float32),
        compiler_params=pltpu.CompilerParams(needs_layout_passes=False),
        scratch_types=[
            pltpu.VMEM((2, PEER_GATHERS, PEER_GATHER_ROWS), jnp.int32),
            pltpu.VMEM((2,) + aux_shape, jnp.float32),
            pltpu.VMEM((2, PEER_GATHER_ROWS, D_MODEL), jnp.float32),
            pltpu.VMEM((2,) + out_tok, jnp.float32),
            pltpu.SemaphoreType.DMA((2,)),
            pltpu.SemaphoreType.DMA((2,)),
            pltpu.SemaphoreType.DMA((2,)),
        ],
    )
    def k(table_hbm, idx_hbm, aux_hbm, out_hbm, idx_v, aux_v, rows_v, out_v, sem_r, sem_i, sem_o):
        wid = lax.axis_index("s") * SC_CORES + lax.axis_index("c")
        base = wid * tpw

        def gather(p, c, b):
            return pltpu.make_async_copy(table_hbm.at[idx_v.at[p, c]], rows_v.at[b], sem_r.at[b])

        def load_meta(t, p):
            return (pltpu.make_async_copy(idx_hbm.at[t], idx_v.at[p], sem_i.at[p]),
                    pltpu.make_async_copy(aux_hbm.at[t], aux_v.at[p], sem_i.at[p]))

        def store_out(t, p):
            return pltpu.make_async_copy(out_v.at[p], out_hbm.at[t], sem_o.at[p])

        def compute(p, c, b):
            if phase == "dot":
                lane = lax.iota(jnp.int32, SC_LANES)
                vec = jnp.zeros((SC_LANES,), jnp.float32)
                groups_per_vec = SC_LANES // PEER_ACC_VREGS
                for g in range(PEER_GATHER_ROWS // PEER_ACC_VREGS):
                    def body(cc, accs):
                        xv = aux_v[p, pl.ds(cc * SC_LANES, SC_LANES)]
                        return tuple(accs[r] + rows_v[b, g * PEER_ACC_VREGS + r, pl.ds(cc * SC_LANES, SC_LANES)] * xv
                                     for r in range(PEER_ACC_VREGS))
                    accs = lax.fori_loop(0, D_MODEL // SC_LANES, body,
                                         tuple(jnp.zeros((SC_LANES,), jnp.float32) for _ in range(PEER_ACC_VREGS)))
                    for r in range(PEER_ACC_VREGS):
                        vec = jnp.where(lane == (g % groups_per_vec) * PEER_ACC_VREGS + r, jnp.sum(accs[r]), vec)
                    if g % groups_per_vec == groups_per_vec - 1:
                        out_v[p, pl.ds(c * PEER_GATHER_ROWS + (g // groups_per_vec) * SC_LANES, SC_LANES)] = vec
            else:
                for db in range(D_MODEL // (PEER_ACC_VREGS * SC_LANES)):
                    def body(kk, accs):
                        wv = plsc.load_gather(aux_v.at[p], [jnp.full((SC_LANES,), c * PEER_GATHER_ROWS + kk, jnp.int32)])
                        return tuple(accs[j] + rows_v[b, kk, pl.ds((db * PEER_ACC_VREGS + j) * SC_LANES, SC_LANES)] * wv
                                     for j in range(PEER_ACC_VREGS))
                    if c == 0:
                        init = tuple(jnp.zeros((SC_LANES,), jnp.float32) for _ in range(PEER_ACC_VREGS))
                    else:
                        init = tuple(out_v[p, pl.ds((db * PEER_ACC_VREGS + j) * SC_LANES, SC_LANES)]
                                     for j in range(PEER_ACC_VREGS))
                    accs = lax.fori_loop(0, PEER_GATHER_ROWS, body, init)
                    for j in range(PEER_ACC_VREGS):
                        out_v[p, pl.ds((db * PEER_ACC_VREGS + j) * SC_LANES, SC_LANES)] = accs[j]

        for d in load_meta(base, 0):
            d.start()
        for d in load_meta(base, 0):
            d.wait()
        gather(0, 0, 0).start()

        @pl.loop(0, tpw // 2)
        def _(i2):
            for p in range(2):
                i = i2 * 2 + p
                t = base + i
                nxt = base + jnp.minimum(i + 1, tpw - 1)
                for d in load_meta(nxt, 1 - p):
                    d.start()

                @pl.when(i2 > 0)
                def _():
                    store_out(t, p).wait()

                for c in range(PEER_GATHERS):
                    b = c % 2
                    if c < PEER_GATHERS - 1:
                        gather(p, c + 1, 1 - b).start()
                    else:
                        for d in load_meta(nxt, 1 - p):
                            d.wait()
                        gather(1 - p, 0, 0).start()
                    gather(p, c, b).wait()
                    compute(p, c, b)
                store_out(t, p).start()

        gather(0, 0, 0).wait()
        for p in range(2):
            store_out(base, p).wait()

    return k(table, idx.reshape(N, PEER_GATHERS, PEER_GATHER_ROWS), aux)


PEER_TOKENS = 256
INT_BIG = 2 ** 30


def _extract_topk(cand_ref, ids_ref, val_out_ref, id_out_ref, row0):
    def body(r, carry):
        c = cand_ref[...]
        ids = ids_ref[...]
        m = jnp.max(c, axis=0, keepdims=True)
        sel = jnp.min(jnp.where(c == m, ids, INT_BIG), axis=0, keepdims=True)
        cand_ref[...] = jnp.where(ids == sel, -jnp.inf, c)
        val_out_ref[pl.ds(row0 + r, 1), :] = m
        id_out_ref[pl.ds(row0 + r, 1), :] = sel
        return carry
    lax.fori_loop(0, PEER_TOPK, body, 0)


def _peer_retrieve_kernel(x_ref, gain_ref, scale_ref, shift_ref, wq_ref, keys_ref,
                          h_ref, idx_out_ref, gate_out_ref,
                          s_ref, ids1_ref, sv_ref, si_ref, cand_ref, cid_ref, ts_ref, idx_ref, gate_ref):
    x = x_ref[0]
    y = x * lax.rsqrt(jnp.mean(x * x, axis=-1, keepdims=True) + NORM_EPS)
    h = (y * gain_ref[...]) * (1.0 + scale_ref[0]) + shift_ref[0]
    h_ref[0] = h
    q = jnp.dot(h.astype(jnp.bfloat16), wq_ref[...], preferred_element_type=jnp.float32)
    T = PEER_TOKENS
    K = PEER_TOPK
    ids1_ref[...] = lax.broadcasted_iota(jnp.int32, (PEER_N_KEYS, T), 0)
    for hd in range(PEER_HEADS):
        for p in range(2):
            hp = hd * 2 + p
            qs = q[:, hp * PEER_HALF:(hp + 1) * PEER_HALF].astype(jnp.bfloat16)
            s_ref[...] = lax.dot_general(keys_ref[hp], qs, (((1,), (1,)), ((), ())),
                                         preferred_element_type=jnp.float32)
            _extract_topk(s_ref, ids1_ref, sv_ref, si_ref, p * K)
        for i in range(K):
            cand_ref[i * K:(i + 1) * K, :] = sv_ref[i:i + 1, :] + sv_ref[K:2 * K, :]
            cid_ref[i * K:(i + 1) * K, :] = si_ref[i:i + 1, :] * PEER_N_KEYS + si_ref[K:2 * K, :]
        _extract_topk(cand_ref, cid_ref, ts_ref, idx_ref, hd * K)
        ts = ts_ref[hd * K:(hd + 1) * K, :]
        e = jnp.exp(ts - jnp.max(ts, axis=0, keepdims=True))
        gate_ref[hd * K:(hd + 1) * K, :] = e / jnp.sum(e, axis=0, keepdims=True)
    idx_out_ref[...] = idx_ref[...].T
    gate_out_ref[...] = gate_ref[...].T


def peer_retrieve(x, gain, scale, shift, w_q, sub_keys):
    B, L, D = x.shape
    T = PEER_TOKENS
    nt = L // T
    keys = sub_keys.reshape(PEER_HEADS * 2, PEER_N_KEYS, PEER_HALF).astype(jnp.bfloat16)
    return pl.pallas_call(
        _peer_retrieve_kernel,
        grid=(B, nt),
        in_specs=[
            pl.BlockSpec((1, T, D), lambda b, i: (b, i, 0)),
            pl.BlockSpec((1, D), lambda b, i: (0, 0)),
            pl.BlockSpec((1, 1, D), lambda b, i: (b, 0, 0)),
            pl.BlockSpec((1, 1, D), lambda b, i: (b, 0, 0)),
            pl.BlockSpec((D, PEER_HEADS * 2 * PEER_HALF), lambda b, i: (0, 0)),
            pl.BlockSpec((PEER_HEADS * 2, PEER_N_KEYS, PEER_HALF), lambda b, i: (0, 0, 0)),
        ],
        out_specs=[
            pl.BlockSpec((1, T, D), lambda b, i: (b, i, 0)),
            pl.BlockSpec((T, PEER_SLOTS), lambda b, i: (b * nt + i, 0)),
            pl.BlockSpec((T, PEER_SLOTS), lambda b, i: (b * nt + i, 0)),
        ],
        out_shape=[
            jax.ShapeDtypeStruct((B, L, D), jnp.float32),
            jax.ShapeDtypeStruct((B * L, PEER_SLOTS), jnp.int32),
            jax.ShapeDtypeStruct((B * L, PEER_SLOTS), jnp.float32),
        ],
        scratch_shapes=[
            pltpu.VMEM((PEER_N_KEYS, T), jnp.float32),
            pltpu.VMEM((PEER_N_KEYS, T), jnp.int32),
            pltpu.VMEM((2 * PEER_TOPK, T), jnp.float32),
            pltpu.VMEM((2 * PEER_TOPK, T), jnp.int32),
            pltpu.VMEM((PEER_TOPK * PEER_TOPK, T), jnp.float32),
            pltpu.VMEM((PEER_TOPK * PEER_TOPK, T), jnp.int32),
            pltpu.VMEM((PEER_SLOTS, T), jnp.float32),
            pltpu.VMEM((PEER_SLOTS, T), jnp.int32),
            pltpu.VMEM((PEER_SLOTS, T), jnp.float32),
        ],
        compiler_params=pltpu.CompilerParams(dimension_semantics=("parallel", "parallel"),
                                             vmem_limit_bytes=VMEM_LIMIT_BYTES),
        name="peer_retrieve",
    )(x, gain.reshape(1, D), scale, shift, w_q.astype(jnp.bfloat16), keys)


PEER_ACT_ROWS = 1024


def _peer_act_kernel(dots_ref, gate_ref, w_ref):
    a = dots_ref[...]
    w_ref[...] = gate_ref[...] * (0.5 * a * (1.0 + lax.erf(a * (2.0 ** -0.5))))


def peer_act(dots, gate):
    N = dots.shape[0]
    T = min(PEER_ACT_ROWS, N)
    spec = pl.BlockSpec((T, PEER_SLOTS), lambda i: (i, 0))
    return pl.pallas_call(
        _peer_act_kernel,
        grid=(N // T,),
        in_specs=[spec, spec],
        out_specs=spec,
        out_shape=jax.ShapeDtypeStruct((N, PEER_SLOTS), jnp.float32),
        compiler_params=pltpu.CompilerParams(dimension_semantics=("parallel",)),
        name="peer_act",
    )(dots, gate)


def peer_ffn(x, gain, scale, shift, w_q, sub_keys, exp_u, exp_v):
    B, L, D = x.shape
    N = B * L
    h, e_idx, gate = peer_retrieve(x, gain, scale, shift, w_q, sub_keys)
    dots = _sc_peer(exp_u, e_idx, h.reshape(N, D), "dot")
    return _sc_peer(exp_v, e_idx, peer_act(dots, gate), "wsum").reshape(B, L, D)


def _forward_group(x, c, ctx, c_ctx, mod_w, mod_b, mix_norm, w_in, w_out, rw_conv, rw_decay_up, rw_decay0, rw_a_up, rw_a0, rw_gate_up, rw_k_k, rw_k_a, rw_r_k, rw_gn_g, rw_gn_b, mla_q_norm, mla_w_uq, mla_kv_norm, mla_w_ukv, mla_q_gain, mla_k_gain, hy_conv, hy_w1, hy_b1, hy_freq1, hy_w2, hy_b2, hy_freq2, hy_w3, hy_b3, hy_bias, ffn_norm, peer_wq, peer_keys, peer_u, peer_v):
    B, L, D = x.shape
    rope = axial_rope_tables(L)
    s_rw, s_mla = RW_PROJ, RW_PROJ + MLA_PROJ
    for li in range(DEPTH):
        need_ctx = li < DEPTH - 1
        mod_l = (jax.nn.silu(c) @ mod_w[li] + mod_b[li])[:, None, :]
        mod_c = (jax.nn.silu(c_ctx) @ mod_w[li] + mod_b[li])[None, None, :]
        shm_l, scm_l, gm_l, shf_l, scf_l, gf_l = jnp.split(mod_l, N_MOD, axis=-1)
        shm_c, scm_c, gm_c, shf_c, scf_c, gf_c = jnp.split(mod_c, N_MOD, axis=-1)

        p_l = norm_mod_proj(x, mix_norm[li], scm_l, shm_l, w_in[li], 512)
        p_c = norm_mod_proj(ctx, mix_norm[li], jnp.broadcast_to(scm_c, (B, 1, D)),
                            jnp.broadcast_to(shm_c, (B, 1, D)), w_in[li], 256)
        rw_l, rw_c = rwkv7_mixer(p_l[..., :s_rw], p_c[..., :s_rw], rw_conv[li], rw_decay_up[li], rw_decay0[li],
                                 rw_a_up[li], rw_a0[li], rw_gate_up[li], rw_k_k[li], rw_k_a[li], rw_r_k[li],
                                 rw_gn_g[li], rw_gn_b[li], need_ctx)
        ml_l, ml_c = mla_mixer(p_l[..., s_rw:s_mla], p_c[..., s_rw:s_mla], rope, mla_q_norm[li], mla_w_uq[li],
                               mla_kv_norm[li], mla_w_ukv[li], mla_q_gain[li], mla_k_gain[li], need_ctx)
        hy_prm = (hy_conv[li], hy_w1[li], hy_b1[li], hy_freq1[li], hy_w2[li], hy_b2[li], hy_freq2[li],
                  hy_w3[li], hy_b3[li], hy_bias[li])
        hy_l = hyena_mixer(p_l[..., s_mla:], *hy_prm)
        x = x + gm_l * (jnp.concatenate([rw_l, ml_l, hy_l], axis=-1) @ w_out[li])
        x = x + gf_l * peer_ffn(x, ffn_norm[li], scf_l, shf_l,
                                peer_wq[li], peer_keys[li], peer_u[li], peer_v[li])
        if need_ctx:
            hy_c = hyena_mixer(p_c[..., s_mla:], *hy_prm)
            ctx = ctx + gm_c * (jnp.concatenate([rw_c, ml_c, hy_c], axis=-1) @ w_out[li])
            ctx = ctx + gf_c * peer_ffn(ctx, ffn_norm[li], jnp.broadcast_to(scf_c, (B, 1, D)),
                                        jnp.broadcast_to(shf_c, (B, 1, D)),
                                        peer_wq[li], peer_keys[li], peer_u[li], peer_v[li])
    return x


def kernel(x, c, ctx, c_ctx, mod_w, mod_b, mix_norm, w_in, w_out, rw_conv, rw_decay_up, rw_decay0, rw_a_up, rw_a0, rw_gate_up, rw_k_k, rw_k_a, rw_r_k, rw_gn_g, rw_gn_b, mla_q_norm, mla_w_uq, mla_kv_norm, mla_w_ukv, mla_q_gain, mla_k_gain, hy_conv, hy_w1, hy_b1, hy_freq1, hy_w2, hy_b2, hy_freq2, hy_w3, hy_b3, hy_bias, ffn_norm, peer_wq, peer_keys, peer_u, peer_v):
    params = (mod_w, mod_b, mix_norm, w_in, w_out, rw_conv, rw_decay_up, rw_decay0, rw_a_up, rw_a0, rw_gate_up,
              rw_k_k, rw_k_a, rw_r_k, rw_gn_g, rw_gn_b, mla_q_norm, mla_w_uq, mla_kv_norm, mla_w_ukv, mla_q_gain,
              mla_k_gain, hy_conv, hy_w1, hy_b1, hy_freq1, hy_w2, hy_b2, hy_freq2, hy_w3, hy_b3, hy_bias,
              ffn_norm, peer_wq, peer_keys, peer_u, peer_v)
    return _forward_group(x, c, ctx, c_ctx, *params)
```

```python
import functools
import math

import jax
import jax.numpy as jnp
import numpy as np
from jax import lax
from jax.experimental import pallas as pl
from jax.experimental.pallas import tpu as pltpu
from jax.experimental.pallas import tpu_sc as plsc

D_MODEL = 1024
DEPTH = 2
GRID_W = 64
N_MOD = 6
NORM_EPS = 1e-6

RW_HEADS = 6
RW_HEAD_DIM = 64
RW_WIDTH = RW_HEADS * RW_HEAD_DIM
RW_DECAY_RANK = 64
RW_A_RANK = 64
RW_GATE_RANK = 128
RW_DECAY_SCALE = 0.6065306597
RW_GN_EPS = 64e-5
L2_EPS = 1e-12

MLA_HEADS = 6
MLA_Q_RANK = 256
MLA_KV_RANK = 128
MLA_NOPE_DIM = 64
MLA_ROPE_DIM = 32
MLA_V_DIM = 64
MLA_QK_DIM = MLA_NOPE_DIM + MLA_ROPE_DIM
MLA_WIDTH = MLA_HEADS * MLA_V_DIM
AXIS_ROPE_DIM = MLA_ROPE_DIM // 2
ROPE_THETA = 10000.0
Q_BLOCK = 128

HY_WIDTH = 256
HY_ORDER = 2
HY_POS_BANDS = 16
HY_SHORT_DECAY_PCT = 0.3
HY_LONG_DECAY_PCT = 1.5
HY_DECAY_TARGET = 1e-2

PEER_HEADS = 8
PEER_N_KEYS = 128
PEER_TOPK = 16
PEER_QUERY_DIM = 256
PEER_HALF = PEER_QUERY_DIM // 2

RW_PROJ = 3 * RW_WIDTH + RW_DECAY_RANK + RW_A_RANK + RW_GATE_RANK
MLA_PROJ = MLA_Q_RANK + MLA_KV_RANK + MLA_ROPE_DIM
HY_PROJ = (HY_ORDER + 1) * HY_WIDTH
IN_PROJ = RW_PROJ + MLA_PROJ + HY_PROJ
MIX_WIDTH = RW_WIDTH + MLA_WIDTH + HY_WIDTH

VMEM_LIMIT_BYTES = 48 * 1024 * 1024


def _norm_mod_proj_kernel(x_ref, gain_ref, scale_ref, shift_ref, w_ref, o_ref):
    x = x_ref[0]
    y = x * lax.rsqrt(jnp.mean(x * x, axis=-1, keepdims=True) + NORM_EPS)
    y = y * gain_ref[...]
    y = y * (1.0 + scale_ref[0]) + shift_ref[0]
    o_ref[0] = jnp.dot(y.astype(jnp.bfloat16), w_ref[...], preferred_element_type=jnp.float32)


def norm_mod_proj(x, gain, scale, shift, w, block_rows):
    B, L, D = x.shape
    N = w.shape[1]
    return pl.pallas_call(
        _norm_mod_proj_kernel,
        grid=(B, L // block_rows),
        in_specs=[
            pl.BlockSpec((1, block_rows, D), lambda b, i: (b, i, 0)),
            pl.BlockSpec((1, D), lambda b, i: (0, 0)),
            pl.BlockSpec((1, 1, D), lambda b, i: (b, 0, 0)),
            pl.BlockSpec((1, 1, D), lambda b, i: (b, 0, 0)),
            pl.BlockSpec((D, N), lambda b, i: (0, 0)),
        ],
        out_specs=pl.BlockSpec((1, block_rows, N), lambda b, i: (b, i, 0)),
        out_shape=jax.ShapeDtypeStruct((B, L, N), jnp.float32),
        compiler_params=pltpu.CompilerParams(
            dimension_semantics=("parallel", "parallel"), vmem_limit_bytes=VMEM_LIMIT_BYTES),
        name="norm_mod_proj",
    )(x, gain.reshape(1, D), scale, shift, w.astype(jnp.bfloat16))


RW_CHUNK = 64


def _rwkv_chunk_kernel(r_ref, kk_ref, v_ref, lw_ref, akk_ref, kr_ref, y_ref, h_ref):
    d = pl.program_id(0)
    n = pl.program_id(2)

    @pl.when(n == 0)
    def _():
        h_ref[...] = jnp.zeros_like(h_ref)

    C = RW_CHUNK
    row = lax.broadcasted_iota(jnp.int32, (C, C), 0)
    col = lax.broadcasted_iota(jnp.int32, (C, C), 1)
    lag = (row - col) * (1 - 2 * d)
    before = lag > 0
    upto = lag >= 0
    tri = upto.astype(jnp.float32)
    eye = (row == col).astype(jnp.float32)
    bf = jnp.bfloat16
    f32 = jnp.float32

    def mm(a, b):
        return jnp.dot(a.astype(bf), b.astype(bf), preferred_element_type=f32)

    def mm_nt(a, b):
        return lax.dot_general(a.astype(bf), b.astype(bf), (((1,), (1,)), ((), ())), preferred_element_type=f32)

    def mm_tn(a, b):
        return lax.dot_general(a.astype(bf), b.astype(bf), (((0,), (0,)), ((), ())), preferred_element_type=f32)

    hs = range(RW_HEADS)
    HD = RW_HEAD_DIM
    r = [r_ref[0, h] for h in hs]
    kk = [kk_ref[0, h] for h in hs]
    v = [v_ref[0, h] for h in hs]
    lw = [lw_ref[0, 0, h] for h in hs]
    akk = [akk_ref[0, 0, h] for h in hs]
    kr = [kr_ref[0, 0, h] for h in hs]
    G = [jnp.dot(tri, lw[h], preferred_element_type=f32, precision=lax.Precision.HIGHEST) for h in hs]
    gtot = [jnp.sum(lw[h], axis=0, keepdims=True) for h in hs]
    Einv = [jnp.exp(-G[h]) for h in hs]
    At = [-kk[h] * jnp.exp(G[h] - lw[h]) for h in hs]
    Rt = [r[h] * jnp.exp(G[h]) for h in hs]
    Bt = [akk[h] * Einv[h] for h in hs]
    Kt = [kr[h] * Einv[h] for h in hs]
    X = [mm_nt(jnp.concatenate([At[h], Rt[h]], axis=0), jnp.concatenate([Bt[h], Kt[h]], axis=0)) for h in hs]
    M_ab = [jnp.where(before, X[h][:C, :C], 0.0) for h in hs]
    M_ak = [jnp.where(before, X[h][:C, C:], 0.0) for h in hs]
    A_rb = [jnp.where(upto, X[h][C:, :C], 0.0) for h in hs]
    A_rk = [jnp.where(upto, X[h][C:, C:], 0.0) for h in hs]
    MV = [mm(M_ak[h], v[h]) for h in hs]
    Mp = M_ab
    T = [eye + Mp[h] for h in hs]
    for _ in range(5):
        Mp = [jnp.dot(Mp[h], Mp[h], preferred_element_type=f32) for h in hs]
        T = [T[h] + jnp.dot(T[h], Mp[h], preferred_element_type=f32) for h in hs]
    WU = [jnp.dot(T[h], jnp.concatenate([At[h], MV[h]], axis=1), preferred_element_type=f32) for h in hs]
    H0 = [h_ref[h] for h in hs]
    Ehat = [jnp.exp(gtot[h] - G[h]) for h in hs]
    Om = [Rt[h] + mm(A_rb[h], WU[h][:, :HD]) for h in hs]
    Y0 = [mm(A_rb[h], WU[h][:, HD:]) + mm(A_rk[h], v[h]) for h in hs]
    BW = [mm_tn(akk[h] * Ehat[h], WU[h]) for h in hs]
    KV = [mm_tn(kr[h] * Ehat[h], v[h]) for h in hs]
    for h in hs:
        y_ref[0, 0, h] = jnp.dot(Om[h], H0[h], preferred_element_type=f32) + Y0[h]
    for h in hs:
        P = eye * jnp.exp(gtot[h]) + BW[h][:, :HD]
        h_ref[h] = jnp.dot(P, H0[h], preferred_element_type=f32) + BW[h][:, HD:] + KV[h]


def rwkv_chunked(r, kk, v, lw, akk, kr, n_ctx):
    B, H, T, _ = r.shape
    nc = n_ctx // RW_CHUNK
    nt = T // RW_CHUNK

    def chunk_of(d, n):
        bwd = jnp.where(n < nc, nc - 1 - n, nt - 1 - (n - nc))
        return jnp.where(d == 0, n, bwd)

    spec1 = pl.BlockSpec((1, H, RW_CHUNK, RW_HEAD_DIM), lambda d, b, n: (b, 0, chunk_of(d, n), 0))
    spec2 = pl.BlockSpec((1, 1, H, RW_CHUNK, RW_HEAD_DIM), lambda d, b, n: (d, b, 0, chunk_of(d, n), 0))
    return pl.pallas_call(
        _rwkv_chunk_kernel,
        grid=(2, B, nt),
        in_specs=[spec1, spec1, spec1, spec2, spec2, spec2],
        out_specs=spec2,
        out_shape=jax.ShapeDtypeStruct((2, B, H, T, RW_HEAD_DIM), jnp.float32),
        scratch_shapes=[pltpu.VMEM((H, RW_HEAD_DIM, RW_HEAD_DIM), jnp.float32)],
        compiler_params=pltpu.CompilerParams(dimension_semantics=("parallel", "parallel", "arbitrary")),
        name="rwkv_chunked",
    )(r, kk, v, lw, akk, kr)


def rms_norm(x, gain):
    y = x * lax.rsqrt(jnp.mean(x * x, axis=-1, keepdims=True) + NORM_EPS)
    return y * gain


def short_conv(x, w):
    xp = jnp.pad(x, ((0, 0), (1, 1), (0, 0)))
    return xp[:, :-2] * w[0] + xp[:, 1:-1] * w[1] + xp[:, 2:] * w[2]


def _heads(t):
    return t.reshape(t.shape[:-1] + (RW_HEADS, RW_HEAD_DIM))


def rwkv7_prepare(p, conv_w, decay_up, decay0, a_up, a0, gate_up, k_k, k_a):
    z = short_conv(p, conv_w)
    o1, o2, o3 = RW_WIDTH, 2 * RW_WIDTH, 3 * RW_WIDTH
    o4 = o3 + RW_DECAY_RANK
    o5 = o4 + RW_A_RANK
    r, k, v = z[..., :o1], z[..., o1:o2], z[..., o2:o3]
    d_lo, a_lo, g_lo = z[..., o3:o4], z[..., o4:o5], z[..., o5:]
    log_decay = -RW_DECAY_SCALE * jax.nn.sigmoid(
        decay0[:, None, None, :] + jnp.einsum('blr,nrc->nblc', jnp.tanh(d_lo), decay_up))
    a = jax.nn.sigmoid(a0[:, None, None, :] + jnp.einsum('blr,nrc->nblc', a_lo, a_up))
    g = jnp.einsum('blr,rc->blc', jax.nn.sigmoid(g_lo), gate_up)
    kk = _heads(k * k_k)
    kk = kk * lax.rsqrt(jnp.sum(kk * kk, axis=-1, keepdims=True) + L2_EPS)
    k_rep = _heads(k)[None] * (1.0 + (_heads(a) - 1.0) * _heads(k_a))
    return _heads(r), _heads(k), _heads(v), _heads(log_decay), kk[None] * _heads(a), kk, k_rep, g


def rwkv7_readout(y, r, k, v, g, r_k, gn_g, gn_b):
    B, L = y.shape[:2]
    mu = jnp.mean(y, axis=-1, keepdims=True)
    var = jnp.mean(jnp.square(y - mu), axis=-1, keepdims=True)
    yn = ((y - mu) * lax.rsqrt(var + RW_GN_EPS)).reshape(B, L, RW_WIDTH) * gn_g + gn_b
    bonus = jnp.sum(r * k * r_k.reshape(RW_HEADS, RW_HEAD_DIM), axis=-1, keepdims=True) * v
    return (yn + bonus.reshape(B, L, RW_WIDTH)) * g


def rwkv7_mixer(p_lat, p_ctx, conv_w, decay_up, decay0, a_up, a0, gate_up, k_k, k_a, r_k, gn_g, gn_b, need_ctx):
    prm = (conv_w, decay_up, decay0, a_up, a0, gate_up, k_k, k_a)
    lat = rwkv7_prepare(p_lat, *prm)
    ctx = rwkv7_prepare(p_ctx, *prm)
    n_ctx = p_ctx.shape[1]

    def seq(i):
        t = jnp.concatenate([ctx[i], lat[i]], axis=-3)
        return jnp.swapaxes(t, -3, -2)

    y = rwkv_chunked(seq(0), seq(5), seq(2), seq(3), seq(4), seq(6), n_ctx)
    y = jnp.swapaxes(y[0] + y[1], 1, 2)
    lr, lk, lv, _, _, _, _, lg = lat
    cr, ck, cv, _, _, _, _, cg = ctx
    out_l = rwkv7_readout(y[:, n_ctx:], lr, lk, lv, lg, r_k, gn_g, gn_b)
    out_c = rwkv7_readout(y[:, :n_ctx], cr, ck, cv, cg, r_k, gn_g, gn_b) if need_ctx else None
    return out_l, out_c


def axial_rope_tables(L):
    rows = L // GRID_W
    row, col = jnp.meshgrid(jnp.arange(rows), jnp.arange(GRID_W), indexing='ij')
    inv = ROPE_THETA ** (-jnp.arange(0, AXIS_ROPE_DIM, 2, dtype=jnp.float32) / AXIS_ROPE_DIM)
    pos = jnp.stack([row.reshape(-1), col.reshape(-1)], axis=-1).astype(jnp.float32)
    ang = pos[:, :, None] * inv[None, None, :]
    return jnp.cos(ang), jnp.sin(ang)


def apply_axial_rope(x, rope):
    cos, sin = rope
    B, L, H, _ = x.shape
    xa = x.reshape(B, L, H, 2, AXIS_ROPE_DIM)
    half = AXIS_ROPE_DIM // 2
    x1, x2 = xa[..., :half], xa[..., half:]
    cs, sn = cos[None, :, None], sin[None, :, None]
    out = jnp.concatenate([x1 * cs - x2 * sn, x2 * cs + x1 * sn], axis=-1)
    return out.reshape(B, L, H, MLA_ROPE_DIM)


def mla_qkv(p, rope, q_norm, w_uq, kv_norm, w_ukv, q_gain, k_gain):
    B, L, _ = p.shape
    c_q = p[..., :MLA_Q_RANK]
    c_kv = p[..., MLA_Q_RANK:MLA_Q_RANK + MLA_KV_RANK]
    k_rope = p[..., MLA_Q_RANK + MLA_KV_RANK:]
    q = (rms_norm(c_q, q_norm) @ w_uq).reshape(B, L, MLA_HEADS, MLA_QK_DIM)
    kv = (rms_norm(c_kv, kv_norm) @ w_ukv).reshape(B, L, MLA_HEADS, MLA_NOPE_DIM + MLA_V_DIM)
    k = jnp.concatenate([kv[..., :MLA_NOPE_DIM],
                         jnp.broadcast_to(k_rope[:, :, None, :], (B, L, MLA_HEADS, MLA_ROPE_DIM))], axis=-1)
    v = kv[..., MLA_NOPE_DIM:]
    q = rms_norm(q, q_gain)
    k = rms_norm(k, k_gain)
    if rope is not None:
        q = jnp.concatenate([q[..., :MLA_NOPE_DIM], apply_axial_rope(q[..., MLA_NOPE_DIM:], rope)], axis=-1)
        k = jnp.concatenate([k[..., :MLA_NOPE_DIM], apply_axial_rope(k[..., MLA_NOPE_DIM:], rope)], axis=-1)
    return q, k, v


ATTN_Q_ROWS = 512


def _attn_kernel(q_ref, k_ref, v_ref, o_ref):
    q = (q_ref[0, 0] * (MLA_QK_DIM ** -0.5)).astype(jnp.bfloat16)
    k = k_ref[0, 0].astype(jnp.bfloat16)
    s = lax.dot_general(q, k, (((1,), (1,)), ((), ())), preferred_element_type=jnp.float32)
    e = jnp.exp(s - jnp.max(s, axis=-1, keepdims=True))
    o = jnp.dot(e.astype(jnp.bfloat16), v_ref[0, 0].astype(jnp.bfloat16), preferred_element_type=jnp.float32)
    o_ref[0, 0] = o / jnp.sum(e, axis=-1, keepdims=True)


def attention(q, k, v):
    B, H, Lq, dk = q.shape
    Lk, dv = v.shape[2], v.shape[3]
    tq = min(ATTN_Q_ROWS, Lq)
    return pl.pallas_call(
        _attn_kernel,
        grid=(B, H, Lq // tq),
        in_specs=[
            pl.BlockSpec((1, 1, tq, dk), lambda b, h, i: (b, h, i, 0)),
            pl.BlockSpec((1, 1, Lk, dk), lambda b, h, i: (b, h, 0, 0)),
            pl.BlockSpec((1, 1, Lk, dv), lambda b, h, i: (b, h, 0, 0)),
        ],
        out_specs=pl.BlockSpec((1, 1, tq, dv), lambda b, h, i: (b, h, i, 0)),
        out_shape=jax.ShapeDtypeStruct((B, H, Lq, dv), jnp.float32),
        compiler_params=pltpu.CompilerParams(dimension_semantics=("parallel", "parallel", "parallel"),
                                             vmem_limit_bytes=VMEM_LIMIT_BYTES),
        name="mla_attention",
    )(q, k, v)


def mla_mixer(p_lat, p_ctx, rope, q_norm, w_uq, kv_norm, w_ukv, q_gain, k_gain, need_ctx):
    prm = (q_norm, w_uq, kv_norm, w_ukv, q_gain, k_gain)
    q_l, k_l, v_l = (jnp.swapaxes(t, 1, 2) for t in mla_qkv(p_lat, rope, *prm))
    q_c, k_c, v_c = (jnp.swapaxes(t, 1, 2) for t in mla_qkv(p_ctx, None, *prm))
    k_all = jnp.concatenate([k_l, k_c], axis=2)
    v_all = jnp.concatenate([v_l, v_c], axis=2)
    B, L = p_lat.shape[:2]
    y_l = jnp.swapaxes(attention(q_l, k_all, v_all), 1, 2).reshape(B, L, MLA_WIDTH)
    y_c = jnp.swapaxes(attention(q_c, k_c, v_c), 1, 2).reshape(B, p_ctx.shape[1], MLA_WIDTH) if need_ctx else None
    return y_l, y_c


def hyena_filters(L, w1, b1, freq1, w2, b2, freq2, w3, b3):
    tn = jnp.arange(L, dtype=jnp.float32) / L
    bands = jnp.arange(1, HY_POS_BANDS + 1, dtype=jnp.float32)
    ang = 2.0 * math.pi * tn[:, None] * bands[None, :]
    z = jnp.concatenate([tn[:, None], jnp.cos(ang), jnp.sin(ang)], axis=-1)
    h = jnp.sin(freq1 * (z @ w1 + b1))
    h = jnp.sin(freq2 * (h @ w2 + b2))
    h = (h @ w3 + b3).reshape(L, HY_ORDER, 2, HY_WIDTH)
    rates = jnp.abs(jnp.linspace(math.log(HY_DECAY_TARGET) / HY_LONG_DECAY_PCT,
                                 math.log(HY_DECAY_TARGET) / HY_SHORT_DECAY_PCT, HY_WIDTH))
    h = h * jnp.exp(-tn[:, None] * rates[None, :])[:, None, None, :]
    zero = jnp.zeros((1, HY_ORDER, HY_WIDTH), h.dtype)
    h_full = jnp.concatenate([h[:, :, 0], zero, h[:0:-1, :, 1]], axis=0)
    return h_full * lax.rsqrt(jnp.sum(jnp.square(h_full), axis=0, keepdims=True))


def fft_long_conv(u, h_full, bias):
    L = u.shape[1]
    uf = jnp.fft.rfft(u, n=2 * L, axis=1)
    hf = jnp.fft.rfft(h_full, n=2 * L, axis=0)
    y = jnp.fft.irfft(uf * hf[None], n=2 * L, axis=1)[:, :L]
    return y + u * bias


FFT_N1 = 64
FFT_N2 = 128
FFT_N = FFT_N1 * FFT_N2
HY_SEQS = 32


def _dft_tables(seqs):
    n1 = np.arange(FFT_N1)
    n2 = np.arange(FFT_N2)
    f64 = np.exp(-2j * np.pi * np.outer(n1, n1) / FFT_N1)
    f128 = np.exp(-2j * np.pi * np.outer(n2, n2) / FFT_N2)
    tw = np.exp(-2j * np.pi * np.outer(n1, n2) / FFT_N)
    half = FFT_N1 // 2
    fh = f64[:, :half]
    m1 = np.block([[fh.real, -fh.imag], [fh.imag, fh.real]])
    m1f = np.concatenate([f64.real, f64.imag], axis=0)
    m2 = np.block([[f128.real, f128.imag], [-f128.imag, f128.real]])
    m3 = np.block([[f128.real, -f128.imag], [f128.imag, f128.real]]) / FFT_N
    c = np.conj(f64)[:half, :]
    m4 = np.block([[c.real, -c.imag], [c.imag, c.real]])
    bf = lambda a: jnp.asarray(a, jnp.float32).astype(jnp.bfloat16)
    f32 = lambda a: jnp.asarray(a, jnp.float32)
    return dict(m1=bf(m1), m1f=bf(m1f), m2=bf(m2), m3=bf(m3), m4=bf(m4),
                twr_l=f32(np.tile(tw.real, (1, seqs))), twi_l=f32(np.tile(tw.imag, (1, seqs))),
                twr_s=f32(np.tile(tw.real, (seqs, 1))), twi_s=f32(np.tile(tw.imag, (seqs, 1))))


def _spectrum(cols, m1, twr_l, twi_l, m2, R):
    a = jnp.dot(m1, cols.astype(jnp.bfloat16), preferred_element_type=jnp.float32)
    ar, ai = a[:FFT_N1], a[FFT_N1:]
    pr = ar * twr_l - ai * twi_l
    pi = ar * twi_l + ai * twr_l
    lhs = jnp.concatenate(
        [jnp.concatenate([pr[:, r * FFT_N2:(r + 1) * FFT_N2], pi[:, r * FFT_N2:(r + 1) * FFT_N2]], axis=1)
         for r in range(R)], axis=0)
    return jnp.dot(lhs.astype(jnp.bfloat16), m2, preferred_element_type=jnp.float32)


def _filter_fft_kernel(h_ref, m1f_ref, twr_ref, twi_ref, m2_ref, o_ref):
    R = HY_SEQS
    cols = jnp.concatenate([h_ref[r] for r in range(R)], axis=1)
    x = _spectrum(cols, m1f_ref[...], twr_ref[...], twi_ref[...], m2_ref[...], R)
    o_ref[...] = x.reshape(R, FFT_N1, 2 * FFT_N2)


def _hyena_conv_kernel(y_ref, g_ref, hf_ref, bias_ref, m1_ref, twr_l_ref, twi_l_ref, m2_ref, m3_ref,
                       twr_s_ref, twi_s_ref, m4_ref, o_ref):
    R = HY_SEQS
    half = FFT_N1 // 2
    y = [y_ref[0], y_ref[1]]
    for o in range(HY_ORDER):
        top = jnp.concatenate([y[0][r] for r in range(R)], axis=1)
        bot = jnp.concatenate([y[1][r] for r in range(R)], axis=1)
        x = _spectrum(jnp.concatenate([top, bot], axis=0), m1_ref[...], twr_l_ref[...], twi_l_ref[...], m2_ref[...], R)
        hf = hf_ref[o].reshape(R * FFT_N1, 2 * FFT_N2)
        xr, xi = x[:, :FFT_N2], x[:, FFT_N2:]
        hr, hi = hf[:, :FFT_N2], hf[:, FFT_N2:]
        yc = jnp.concatenate([xr * hr - xi * hi, xr * hi + xi * hr], axis=1)
        b = jnp.dot(yc.astype(jnp.bfloat16), m3_ref[...], preferred_element_type=jnp.float32)
        br, bi = b[:, :FFT_N2], b[:, FFT_N2:]
        qr = br * twr_s_ref[...] + bi * twi_s_ref[...]
        qi = bi * twr_s_ref[...] - br * twi_s_ref[...]
        bc = jnp.concatenate(
            [jnp.concatenate([qr[r * FFT_N1:(r + 1) * FFT_N1], qi[r * FFT_N1:(r + 1) * FFT_N1]], axis=0)
             for r in range(R)], axis=1)
        yo = jnp.dot(m4_ref[...], bc.astype(jnp.bfloat16), preferred_element_type=jnp.float32)
        for p in range(2):
            conv = jnp.stack([yo[p * half:(p + 1) * half, r * FFT_N2:(r + 1) * FFT_N2] for r in range(R)], axis=0)
            y[p] = g_ref[o, p] * (conv + y[p] * bias_ref[o])
    o_ref[0] = y[0]
    o_ref[1] = y[1]


def hyena_long_conv(y_t, g_t, h_t, bias):
    B, C, L = y_t.shape
    assert 2 * L == FFT_N and B % 2 == 0 and C % HY_SEQS == 0
    R = HY_SEQS
    half = FFT_N1 // 2
    tb = _dft_tables(R)
    const = lambda a: pl.BlockSpec(a.shape, lambda *_: (0,) * a.ndim)
    hf = pl.pallas_call(
        _filter_fft_kernel,
        grid=(HY_ORDER * C // R,),
        in_specs=[pl.BlockSpec((R, FFT_N1, FFT_N2), lambda i: (i, 0, 0)),
                  const(tb['m1f']), const(tb['twr_l']), const(tb['twi_l']), const(tb['m2'])],
        out_specs=pl.BlockSpec((R, FFT_N1, 2 * FFT_N2), lambda i: (i, 0, 0)),
        out_shape=jax.ShapeDtypeStruct((HY_ORDER * C, FFT_N1, 2 * FFT_N2), jnp.float32),
        compiler_params=pltpu.CompilerParams(dimension_semantics=("parallel",), vmem_limit_bytes=VMEM_LIMIT_BYTES),
        name="hyena_filter_fft",
    )(h_t.reshape(HY_ORDER * C, FFT_N1, FFT_N2), tb['m1f'], tb['twr_l'], tb['twi_l'], tb['m2'])
    hf = hf.reshape(HY_ORDER, C, FFT_N1, 2 * FFT_N2)
    out = pl.pallas_call(
        _hyena_conv_kernel,
        grid=(B // 2, C // R),
        in_specs=[pl.BlockSpec((2, R, half, FFT_N2), lambda b, c: (b, c, 0, 0)),
                  pl.BlockSpec((HY_ORDER, 2, R, half, FFT_N2), lambda b, c: (0, b, c, 0, 0)),
                  pl.BlockSpec((HY_ORDER, R, FFT_N1, 2 * FFT_N2), lambda b, c: (0, c, 0, 0)),
                  pl.BlockSpec((HY_ORDER, R, 1, 1), lambda b, c: (0, c, 0, 0)),
                  const(tb['m1']), const(tb['twr_l']), const(tb['twi_l']), const(tb['m2']), const(tb['m3']),
                  const(tb['twr_s']), const(tb['twi_s']), const(tb['m4'])],
        out_specs=pl.BlockSpec((2, R, half, FFT_N2), lambda b, c: (b, c, 0, 0)),
        out_shape=jax.ShapeDtypeStruct((B, C, half, FFT_N2), jnp.float32),
        compiler_params=pltpu.CompilerParams(dimension_semantics=("parallel", "parallel"),
                                             vmem_limit_bytes=VMEM_LIMIT_BYTES),
        name="hyena_conv",
    )(y_t.reshape(B, C, half, FFT_N2), g_t.reshape(HY_ORDER, B, C, half, FFT_N2), hf,
      bias.reshape(HY_ORDER, C, 1, 1), tb['m1'], tb['twr_l'], tb['twi_l'], tb['m2'], tb['m3'],
      tb['twr_s'], tb['twi_s'], tb['m4'])
    return out.reshape(B, C, L)


def hyena_mixer(p, conv_w, w1, b1, freq1, w2, b2, freq2, w3, b3, bias):
    B, L = p.shape[:2]
    z = short_conv(p, conv_w)
    h_full = hyena_filters(L, w1, b1, freq1, w2, b2, freq2, w3, b3)
    if 2 * L == FFT_N:
        g_t = jnp.transpose(z[..., :HY_ORDER * HY_WIDTH].reshape(B, L, HY_ORDER, HY_WIDTH), (2, 0, 3, 1))
        y_t = jnp.swapaxes(z[..., HY_ORDER * HY_WIDTH:], 1, 2)
        y_t = hyena_long_conv(y_t, g_t, jnp.transpose(h_full, (1, 2, 0)), bias)
        return jnp.swapaxes(y_t, 1, 2)
    gates = (z[..., :HY_WIDTH], z[..., HY_WIDTH:2 * HY_WIDTH])
    y = z[..., 2 * HY_WIDTH:]
    for o in range(HY_ORDER):
        y = gates[o] * fft_long_conv(y, h_full[:, o], bias[o])
    return y


SC_CORES = 2
SC_SUBCORES = 16
SC_LANES = 16
SC_WORKERS = SC_CORES * SC_SUBCORES
PEER_SLOTS = PEER_HEADS * PEER_TOPK
PEER_GATHER_ROWS = 32
PEER_GATHERS = PEER_SLOTS // PEER_GATHER_ROWS
PEER_ACC_VREGS = 8


def _sc_peer(table, idx, aux, phase):
    N = idx.shape[0]
    tpw = N // SC_WORKERS
    assert tpw % 2 == 0 and N % SC_WORKERS == 0
    mesh = plsc.VectorSubcoreMesh(core_axis_name="c", subcore_axis_name="s")
    aux_shape = (D_MODEL,) if phase == "dot" else (PEER_SLOTS,)
    out_tok = (PEER_SLOTS,) if phase == "dot" else (D_MODEL,)

    @functools.partial(
        pl.kernel, mesh=mesh,
        out_type=jax.ShapeDtypeStruct((N,) + out_tok, jnp.float32),
        compiler_params=pltpu.CompilerParams(needs_layout_passes=False),
        scratch_types=[
            pltpu.VMEM((2, PEER_GATHERS, PEER_GATHER_ROWS), jnp.int32),
            pltpu.VMEM((2,) + aux_shape, jnp.float32),
            pltpu.VMEM((2, PEER_GATHER_ROWS, D_MODEL), jnp.float32),
            pltpu.VMEM((2,) + out_tok, jnp.float32),
            pltpu.SemaphoreType.DMA((2,)),
            pltpu.SemaphoreType.DMA((2,)),
            pltpu.SemaphoreType.DMA((2,)),
        ],
    )
    def k(table_hbm, idx_hbm, aux_hbm, out_hbm, idx_v, aux_v, rows_v, out_v, sem_r, sem_i, sem_o):
        wid = lax.axis_index("s") * SC_CORES + lax.axis_index("c")
        base = wid * tpw

        def gather(p, c, b):
            return pltpu.make_async_copy(table_hbm.at[idx_v.at[p, c]], rows_v.at[b], sem_r.at[b])

        def load_meta(t, p):
            return (pltpu.make_async_copy(idx_hbm.at[t], idx_v.at[p], sem_i.at[p]),
                    pltpu.make_async_copy(aux_hbm.at[t], aux_v.at[p], sem_i.at[p]))

        def store_out(t, p):
            return pltpu.make_async_copy(out_v.at[p], out_hbm.at[t], sem_o.at[p])

        def compute(p, c, b):
            if phase == "dot":
                lane = lax.iota(jnp.int32, SC_LANES)
                vec = jnp.zeros((SC_LANES,), jnp.float32)
                groups_per_vec = SC_LANES // PEER_ACC_VREGS
                for g in range(PEER_GATHER_ROWS // PEER_ACC_VREGS):
                    def body(cc, accs):
                        xv = aux_v[p, pl.ds(cc * SC_LANES, SC_LANES)]
                        return tuple(accs[r] + rows_v[b, g * PEER_ACC_VREGS + r, pl.ds(cc * SC_LANES, SC_LANES)] * xv
                                     for r in range(PEER_ACC_VREGS))
                    accs = lax.fori_loop(0, D_MODEL // SC_LANES, body,
                                         tuple(jnp.zeros((SC_LANES,), jnp.float32) for _ in range(PEER_ACC_VREGS)))
                    for r in range(PEER_ACC_VREGS):
                        vec = jnp.where(lane == (g % groups_per_vec) * PEER_ACC_VREGS + r, jnp.sum(accs[r]), vec)
                    if g % groups_per_vec == groups_per_vec - 1:
                        out_v[p, pl.ds(c * PEER_GATHER_ROWS + (g // groups_per_vec) * SC_LANES, SC_LANES)] = vec
            else:
                for db in range(D_MODEL // (PEER_ACC_VREGS * SC_LANES)):
                    def body(kk, accs):
                        wv = plsc.load_gather(aux_v.at[p], [jnp.full((SC_LANES,), c * PEER_GATHER_ROWS + kk, jnp.int32)])
                        return tuple(accs[j] + rows_v[b, kk, pl.ds((db * PEER_ACC_VREGS + j) * SC_LANES, SC_LANES)] * wv
                                     for j in range(PEER_ACC_VREGS))
                    if c == 0:
                        init = tuple(jnp.zeros((SC_LANES,), jnp.float32) for _ in range(PEER_ACC_VREGS))
                    else:
                        init = tuple(out_v[p, pl.ds((db * PEER_ACC_VREGS + j) * SC_LANES, SC_LANES)]
                                     for j in range(PEER_ACC_VREGS))
                    accs = lax.fori_loop(0, PEER_GATHER_ROWS, body, init)
                    for j in range(PEER_ACC_VREGS):
                        out_v[p, pl.ds((db * PEER_ACC_VREGS + j) * SC_LANES, SC_LANES)] = accs[j]

        for d in load_meta(base, 0):
            d.start()
        for d in load_meta(base, 0):
            d.wait()
        gather(0, 0, 0).start()

        @pl.loop(0, tpw // 2)
        def _(i2):
            for p in range(2):
                i = i2 * 2 + p
                t = base + i
                nxt = base + jnp.minimum(i + 1, tpw - 1)
                for d in load_meta(nxt, 1 - p):
                    d.start()

                @pl.when(i2 > 0)
                def _():
                    store_out(t, p).wait()

                for c in range(PEER_GATHERS):
                    b = c % 2
                    if c < PEER_GATHERS - 1:
                        gather(p, c + 1, 1 - b).start()
                    else:
                        for d in load_meta(nxt, 1 - p):
                            d.wait()
                        gather(1 - p, 0, 0).start()
                    gather(p, c, b).wait()
                    compute(p, c, b)
                store_out(t, p).start()

        gather(0, 0, 0).wait()
        for p in range(2):
            store_out(base, p).wait()

    return k(table, idx.reshape(N, PEER_GATHERS, PEER_GATHER_ROWS), aux)


PEER_TOKENS = 256
INT_BIG = 2 ** 30


def _extract_topk(cand_ref, ids_ref, val_out_ref, id_out_ref, row0):
    def body(r, carry):
        c = cand_ref[...]
        ids = ids_ref[...]
        m = jnp.max(c, axis=0, keepdims=True)
        sel = jnp.min(jnp.where(c == m, ids, INT_BIG), axis=0, keepdims=True)
        cand_ref[...] = jnp.where(ids == sel, -jnp.inf, c)
        val_out_ref[pl.ds(row0 + r, 1), :] = m
        id_out_ref[pl.ds(row0 + r, 1), :] = sel
        return carry
    lax.fori_loop(0, PEER_TOPK, body, 0)


def _peer_retrieve_kernel(x_ref, gain_ref, scale_ref, shift_ref, wq_ref, keys_ref,
                          h_ref, idx_out_ref, gate_out_ref,
                          s_ref, ids1_ref, sv_ref, si_ref, cand_ref, cid_ref, ts_ref, idx_ref, gate_ref):
    x = x_ref[0]
    y = x * lax.rsqrt(jnp.mean(x * x, axis=-1, keepdims=True) + NORM_EPS)
    h = (y * gain_ref[...]) * (1.0 + scale_ref[0]) + shift_ref[0]
    h_ref[0] = h
    q = jnp.dot(h.astype(jnp.bfloat16), wq_ref[...], preferred_element_type=jnp.float32)
    T = PEER_TOKENS
    K = PEER_TOPK
    ids1_ref[...] = lax.broadcasted_iota(jnp.int32, (PEER_N_KEYS, T), 0)
    for hd in range(PEER_HEADS):
        for p in range(2):
            hp = hd * 2 + p
            qs = q[:, hp * PEER_HALF:(hp + 1) * PEER_HALF].astype(jnp.bfloat16)
            s_ref[...] = lax.dot_general(keys_ref[hp], qs, (((1,), (1,)), ((), ())),
                                         preferred_element_type=jnp.float32)
            _extract_topk(s_ref, ids1_ref, sv_ref, si_ref, p * K)
        for i in range(K):
            cand_ref[i * K:(i + 1) * K, :] = sv_ref[i:i + 1, :] + sv_ref[K:2 * K, :]
            cid_ref[i * K:(i + 1) * K, :] = si_ref[i:i + 1, :] * PEER_N_KEYS + si_ref[K:2 * K, :]
        _extract_topk(cand_ref, cid_ref, ts_ref, idx_ref, hd * K)
        ts = ts_ref[hd * K:(hd + 1) * K, :]
        e = jnp.exp(ts - jnp.max(ts, axis=0, keepdims=True))
        gate_ref[hd * K:(hd + 1) * K, :] = e / jnp.sum(e, axis=0, keepdims=True)
    idx_out_ref[...] = idx_ref[...].T
    gate_out_ref[...] = gate_ref[...].T


def peer_retrieve(x, gain, scale, shift, w_q, sub_keys):
    B, L, D = x.shape
    T = PEER_TOKENS
    nt = L // T
    keys = sub_keys.reshape(PEER_HEADS * 2, PEER_N_KEYS, PEER_HALF).astype(jnp.bfloat16)
    return pl.pallas_call(
        _peer_retrieve_kernel,
        grid=(B, nt),
        in_specs=[
            pl.BlockSpec((1, T, D), lambda b, i: (b, i, 0)),
            pl.BlockSpec((1, D), lambda b, i: (0, 0)),
            pl.BlockSpec((1, 1, D), lambda b, i: (b, 0, 0)),
            pl.BlockSpec((1, 1, D), lambda b, i: (b, 0, 0)),
            pl.BlockSpec((D, PEER_HEADS * 2 * PEER_HALF), lambda b, i: (0, 0)),
            pl.BlockSpec((PEER_HEADS * 2, PEER_N_KEYS, PEER_HALF), lambda b, i: (0, 0, 0)),
        ],
        out_specs=[
            pl.BlockSpec((1, T, D), lambda b, i: (b, i, 0)),
            pl.BlockSpec((T, PEER_SLOTS), lambda b, i: (b * nt + i, 0)),
            pl.BlockSpec((T, PEER_SLOTS), lambda b, i: (b * nt + i, 0)),
        ],
        out_shape=[
            jax.ShapeDtypeStruct((B, L, D), jnp.float32),
            jax.ShapeDtypeStruct((B * L, PEER_SLOTS), jnp.int32),
            jax.ShapeDtypeStruct((B * L, PEER_SLOTS), jnp.float32),
        ],
        scratch_shapes=[
            pltpu.VMEM((PEER_N_KEYS, T), jnp.float32),
            pltpu.VMEM((PEER_N_KEYS, T), jnp.int32),
            pltpu.VMEM((2 * PEER_TOPK, T), jnp.float32),
            pltpu.VMEM((2 * PEER_TOPK, T), jnp.int32),
            pltpu.VMEM((PEER_TOPK * PEER_TOPK, T), jnp.float32),
            pltpu.VMEM((PEER_TOPK * PEER_TOPK, T), jnp.int32),
            pltpu.VMEM((PEER_SLOTS, T), jnp.float32),
            pltpu.VMEM((PEER_SLOTS, T), jnp.int32),
            pltpu.VMEM((PEER_SLOTS, T), jnp.float32),
        ],
        compiler_params=pltpu.CompilerParams(dimension_semantics=("parallel", "parallel"),
                                             vmem_limit_bytes=VMEM_LIMIT_BYTES),
        name="peer_retrieve",
    )(x, gain.reshape(1, D), scale, shift, w_q.astype(jnp.bfloat16), keys)


PEER_ACT_ROWS = 1024


def _peer_act_kernel(dots_ref, gate_ref, w_ref):
    a = dots_ref[...]
    w_ref[...] = gate_ref[...] * (0.5 * a * (1.0 + lax.erf(a * (2.0 ** -0.5))))


def peer_act(dots, gate):
    N = dots.shape[0]
    T = min(PEER_ACT_ROWS, N)
    spec = pl.BlockSpec((T, PEER_SLOTS), lambda i: (i, 0))
    return pl.pallas_call(
        _peer_act_kernel,
        grid=(N // T,),
        in_specs=[spec, spec],
        out_specs=spec,
        out_shape=jax.ShapeDtypeStruct((N, PEER_SLOTS), jnp.float32),
        compiler_params=pltpu.CompilerParams(dimension_semantics=("parallel",)),
        name="peer_act",
    )(dots, gate)


def peer_ffn(x, gain, scale, shift, w_q, sub_keys, exp_u, exp_v):
    B, L, D = x.shape
    N = B * L
    h, e_idx, gate = peer_retrieve(x, gain, scale, shift, w_q, sub_keys)
    dots = _sc_peer(exp_u, e_idx, h.reshape(N, D), "dot")
    return _sc_peer(exp_v, e_idx, peer_act(dots, gate), "wsum").reshape(B, L, D)


def _mix_and_retrieve(li, x, c, ctx, c_ctx, mod_w, mod_b, mix_norm, w_in, w_out, rw_conv, rw_decay_up, rw_decay0, rw_a_up, rw_a0, rw_gate_up, rw_k_k, rw_k_a, rw_r_k, rw_gn_g, rw_gn_b, mla_q_norm, mla_w_uq, mla_kv_norm, mla_w_ukv, mla_q_gain, mla_k_gain, hy_conv, hy_w1, hy_b1, hy_freq1, hy_w2, hy_b2, hy_freq2, hy_w3, hy_b3, hy_bias, ffn_norm, peer_wq, peer_keys, peer_u, peer_v):
    B, L, D = x.shape
    rope = axial_rope_tables(L)
    s_rw, s_mla = RW_PROJ, RW_PROJ + MLA_PROJ
    need_ctx = li < DEPTH - 1
    mod_l = (jax.nn.silu(c) @ mod_w[li] + mod_b[li])[:, None, :]
    mod_c = (jax.nn.silu(c_ctx) @ mod_w[li] + mod_b[li])[None, None, :]
    shm_l, scm_l, gm_l, shf_l, scf_l, gf_l = jnp.split(mod_l, N_MOD, axis=-1)
    shm_c, scm_c, gm_c, shf_c, scf_c, gf_c = jnp.split(mod_c, N_MOD, axis=-1)

    p_l = norm_mod_proj(x, mix_norm[li], scm_l, shm_l, w_in[li], 512)
    p_c = norm_mod_proj(ctx, mix_norm[li], jnp.broadcast_to(scm_c, (B, 1, D)),
                        jnp.broadcast_to(shm_c, (B, 1, D)), w_in[li], 256)
    rw_l, rw_c = rwkv7_mixer(p_l[..., :s_rw], p_c[..., :s_rw], rw_conv[li], rw_decay_up[li], rw_decay0[li],
                             rw_a_up[li], rw_a0[li], rw_gate_up[li], rw_k_k[li], rw_k_a[li], rw_r_k[li],
                             rw_gn_g[li], rw_gn_b[li], need_ctx)
    ml_l, ml_c = mla_mixer(p_l[..., s_rw:s_mla], p_c[..., s_rw:s_mla], rope, mla_q_norm[li], mla_w_uq[li],
                           mla_kv_norm[li], mla_w_ukv[li], mla_q_gain[li], mla_k_gain[li], need_ctx)
    hy_prm = (hy_conv[li], hy_w1[li], hy_b1[li], hy_freq1[li], hy_w2[li], hy_b2[li], hy_freq2[li],
              hy_w3[li], hy_b3[li], hy_bias[li])
    hy_l = hyena_mixer(p_l[..., s_mla:], *hy_prm)
    x = x + gm_l * (jnp.concatenate([rw_l, ml_l, hy_l], axis=-1) @ w_out[li])
    if need_ctx:
        hy_c = hyena_mixer(p_c[..., s_mla:], *hy_prm)
        ctx = ctx + gm_c * (jnp.concatenate([rw_c, ml_c, hy_c], axis=-1) @ w_out[li])
        ctx = ctx + gf_c * peer_ffn(ctx, ffn_norm[li], jnp.broadcast_to(scf_c, (B, 1, D)),
                                    jnp.broadcast_to(shf_c, (B, 1, D)),
                                    peer_wq[li], peer_keys[li], peer_u[li], peer_v[li])
    h, e_idx, gate = peer_retrieve(x, ffn_norm[li], scf_l, shf_l, peer_wq[li], peer_keys[li])
    return x, ctx, gf_l, h.reshape(B * L, D), e_idx, gate


BATCH_GROUPS = 4


def kernel(x, c, ctx, c_ctx, mod_w, mod_b, mix_norm, w_in, w_out, rw_conv, rw_decay_up, rw_decay0, rw_a_up, rw_a0, rw_gate_up, rw_k_k, rw_k_a, rw_r_k, rw_gn_g, rw_gn_b, mla_q_norm, mla_w_uq, mla_kv_norm, mla_w_ukv, mla_q_gain, mla_k_gain, hy_conv, hy_w1, hy_b1, hy_freq1, hy_w2, hy_b2, hy_freq2, hy_w3, hy_b3, hy_bias, ffn_norm, peer_wq, peer_keys, peer_u, peer_v):
    params = (mod_w, mod_b, mix_norm, w_in, w_out, rw_conv, rw_decay_up, rw_decay0, rw_a_up, rw_a0, rw_gate_up,
              rw_k_k, rw_k_a, rw_r_k, rw_gn_g, rw_gn_b, mla_q_norm, mla_w_uq, mla_kv_norm, mla_w_ukv, mla_q_gain,
              mla_k_gain, hy_conv, hy_w1, hy_b1, hy_freq1, hy_w2, hy_b2, hy_freq2, hy_w3, hy_b3, hy_bias,
              ffn_norm, peer_wq, peer_keys, peer_u, peer_v)
    G = BATCH_GROUPS
    bg = x.shape[0] // G
    L, D = x.shape[1:]
    xs = [x[g * bg:(g + 1) * bg] for g in range(G)]
    cs = [c[g * bg:(g + 1) * bg] for g in range(G)]
    ctxs = [ctx[g * bg:(g + 1) * bg] for g in range(G)]
    stages = [(li, g) for li in range(DEPTH) for g in range(G)]
    pending = {}
    token = None
    for k in range(len(stages) + 1):
        if k < len(stages):
            li, g = stages[k]
            ins = (xs[g], ctxs[g])
            if token is not None:
                token, ins = lax.optimization_barrier((token, ins))
            xm, ctxs[g], gf, h, e_idx, gate = _mix_and_retrieve(li, ins[0], cs[g], ins[1], c_ctx, *params)
            dots = _sc_peer(peer_u[li], e_idx, h, "dot")
            pending[k] = (xm, gf, e_idx, gate, dots)
            token = gate
        if k >= 1:
            li, g = stages[k - 1]
            xm, gf, e_idx, gate, dots = pending.pop(k - 1)
            token, (dots, gate) = lax.optimization_barrier((token, (dots, gate)))
            w = peer_act(dots, gate)
            xs[g] = xm + gf * _sc_peer(peer_v[li], e_idx, w, "wsum").reshape(bg, L, D)
            token = w
    return jnp.concatenate(xs, axis=0)
```

```python
import functools
import math

import jax
import jax.numpy as jnp
import numpy as np
from jax import lax
from jax.experimental import pallas as pl
from jax.experimental.pallas import tpu as pltpu
from jax.experimental.pallas import tpu_sc as plsc

D_MODEL = 1024
DEPTH = 2
GRID_W = 64
N_MOD = 6
NORM_EPS = 1e-6

RW_HEADS = 6
RW_HEAD_DIM = 64
RW_WIDTH = RW_HEADS * RW_HEAD_DIM
RW_DECAY_RANK = 64
RW_A_RANK = 64
RW_GATE_RANK = 128
RW_DECAY_SCALE = 0.6065306597
RW_GN_EPS = 64e-5
L2_EPS = 1e-12

MLA_HEADS = 6
MLA_Q_RANK = 256
MLA_KV_RANK = 128
MLA_NOPE_DIM = 64
MLA_ROPE_DIM = 32
MLA_V_DIM = 64
MLA_QK_DIM = MLA_NOPE_DIM + MLA_ROPE_DIM
MLA_WIDTH = MLA_HEADS * MLA_V_DIM
AXIS_ROPE_DIM = MLA_ROPE_DIM // 2
ROPE_THETA = 10000.0

HY_WIDTH = 256
HY_ORDER = 2
HY_POS_BANDS = 16
HY_SHORT_DECAY_PCT = 0.3
HY_LONG_DECAY_PCT = 1.5
HY_DECAY_TARGET = 1e-2

PEER_HEADS = 8
PEER_N_KEYS = 128
PEER_TOPK = 16
PEER_QUERY_DIM = 256
PEER_HALF = PEER_QUERY_DIM // 2

RW_PROJ = 3 * RW_WIDTH + RW_DECAY_RANK + RW_A_RANK + RW_GATE_RANK
MLA_PROJ = MLA_Q_RANK + MLA_KV_RANK + MLA_ROPE_DIM
HY_PROJ = (HY_ORDER + 1) * HY_WIDTH
IN_PROJ = RW_PROJ + MLA_PROJ + HY_PROJ
MIX_WIDTH = RW_WIDTH + MLA_WIDTH + HY_WIDTH

VMEM_LIMIT_BYTES = 48 * 1024 * 1024


def _norm_mod_proj_kernel(x_ref, gain_ref, scale_ref, shift_ref, w_ref, o_ref):
    x = x_ref[0]
    y = x * lax.rsqrt(jnp.mean(x * x, axis=-1, keepdims=True) + NORM_EPS)
    y = y * gain_ref[...]
    y = y * (1.0 + scale_ref[0]) + shift_ref[0]
    o_ref[0] = jnp.dot(y.astype(jnp.bfloat16), w_ref[...], preferred_element_type=jnp.float32)


def norm_mod_proj(x, gain, scale, shift, w, block_rows):
    B, L, D = x.shape
    N = w.shape[1]
    return pl.pallas_call(
        _norm_mod_proj_kernel,
        grid=(B, L // block_rows),
        in_specs=[
            pl.BlockSpec((1, block_rows, D), lambda b, i: (b, i, 0)),
            pl.BlockSpec((1, D), lambda b, i: (0, 0)),
            pl.BlockSpec((1, 1, D), lambda b, i: (b, 0, 0)),
            pl.BlockSpec((1, 1, D), lambda b, i: (b, 0, 0)),
            pl.BlockSpec((D, N), lambda b, i: (0, 0)),
        ],
        out_specs=pl.BlockSpec((1, block_rows, N), lambda b, i: (b, i, 0)),
        out_shape=jax.ShapeDtypeStruct((B, L, N), jnp.float32),
        compiler_params=pltpu.CompilerParams(
            dimension_semantics=("parallel", "parallel"), vmem_limit_bytes=VMEM_LIMIT_BYTES),
        name="norm_mod_proj",
    )(x, gain.reshape(1, D), scale, shift, w.astype(jnp.bfloat16))


RW_CHUNK = 64


def _rwkv_chunk_kernel(r_ref, kk_ref, v_ref, lw_ref, akk_ref, kr_ref, y_ref, h_ref):
    d = pl.program_id(0)
    n = pl.program_id(2)

    @pl.when(n == 0)
    def _():
        h_ref[...] = jnp.zeros_like(h_ref)

    C = RW_CHUNK
    row = lax.broadcasted_iota(jnp.int32, (C, C), 0)
    col = lax.broadcasted_iota(jnp.int32, (C, C), 1)
    lag = (row - col) * (1 - 2 * d)
    before = lag > 0
    upto = lag >= 0
    tri = upto.astype(jnp.float32)
    eye = (row == col).astype(jnp.float32)
    bf = jnp.bfloat16
    f32 = jnp.float32

    def mm(a, b):
        return jnp.dot(a.astype(bf), b.astype(bf), preferred_element_type=f32)

    def mm_nt(a, b):
        return lax.dot_general(a.astype(bf), b.astype(bf), (((1,), (1,)), ((), ())), preferred_element_type=f32)

    def mm_tn(a, b):
        return lax.dot_general(a.astype(bf), b.astype(bf), (((0,), (0,)), ((), ())), preferred_element_type=f32)

    hs = range(RW_HEADS)
    HD = RW_HEAD_DIM
    heads = lambda t: [t[:, h * HD:(h + 1) * HD] for h in hs]
    r = heads(r_ref[0])
    kk = heads(kk_ref[0])
    v = heads(v_ref[0])
    lw = heads(lw_ref[0, 0])
    akk = heads(akk_ref[0, 0])
    kr = heads(kr_ref[0, 0])
    G = [jnp.dot(tri, lw[h], preferred_element_type=f32, precision=lax.Precision.HIGHEST) for h in hs]
    gtot = [jnp.sum(lw[h], axis=0, keepdims=True) for h in hs]
    Einv = [jnp.exp(-G[h]) for h in hs]
    At = [-kk[h] * jnp.exp(G[h] - lw[h]) for h in hs]
    Rt = [r[h] * jnp.exp(G[h]) for h in hs]
    Bt = [akk[h] * Einv[h] for h in hs]
    Kt = [kr[h] * Einv[h] for h in hs]
    X = [mm_nt(jnp.concatenate([At[h], Rt[h]], axis=0), jnp.concatenate([Bt[h], Kt[h]], axis=0)) for h in hs]
    M_ab = [jnp.where(before, X[h][:C, :C], 0.0) for h in hs]
    M_ak = [jnp.where(before, X[h][:C, C:], 0.0) for h in hs]
    A_rb = [jnp.where(upto, X[h][C:, :C], 0.0) for h in hs]
    A_rk = [jnp.where(upto, X[h][C:, C:], 0.0) for h in hs]
    MV = [mm(M_ak[h], v[h]) for h in hs]
    Mp = M_ab
    T = [eye + Mp[h] for h in hs]
    for _ in range(5):
        Mp = [jnp.dot(Mp[h], Mp[h], preferred_element_type=f32) for h in hs]
        T = [T[h] + jnp.dot(T[h], Mp[h], preferred_element_type=f32) for h in hs]
    WU = [jnp.dot(T[h], jnp.concatenate([At[h], MV[h]], axis=1), preferred_element_type=f32) for h in hs]
    H0 = [h_ref[h] for h in hs]
    Ehat = [jnp.exp(gtot[h] - G[h]) for h in hs]
    Om = [Rt[h] + mm(A_rb[h], WU[h][:, :HD]) for h in hs]
    Y0 = [mm(A_rb[h], WU[h][:, HD:]) + mm(A_rk[h], v[h]) for h in hs]
    BW = [mm_tn(akk[h] * Ehat[h], WU[h]) for h in hs]
    KV = [mm_tn(kr[h] * Ehat[h], v[h]) for h in hs]
    y_ref[0, 0] = jnp.concatenate([jnp.dot(Om[h], H0[h], preferred_element_type=f32) + Y0[h] for h in hs], axis=1)
    for h in hs:
        P = eye * jnp.exp(gtot[h]) + BW[h][:, :HD]
        h_ref[h] = jnp.dot(P, H0[h], preferred_element_type=f32) + BW[h][:, HD:] + KV[h]


def rwkv_chunked(r, kk, v, lw, akk, kr, n_ctx):
    B, T, W = r.shape
    H = W // RW_HEAD_DIM
    nc = n_ctx // RW_CHUNK
    nt = T // RW_CHUNK

    def chunk_of(d, n):
        bwd = jnp.where(n < nc, nc - 1 - n, nt - 1 - (n - nc))
        return jnp.where(d == 0, n, bwd)

    spec1 = pl.BlockSpec((1, RW_CHUNK, W), lambda d, b, n: (b, chunk_of(d, n), 0))
    spec2 = pl.BlockSpec((1, 1, RW_CHUNK, W), lambda d, b, n: (d, b, chunk_of(d, n), 0))
    return pl.pallas_call(
        _rwkv_chunk_kernel,
        grid=(2, B, nt),
        in_specs=[spec1, spec1, spec1, spec2, spec2, spec2],
        out_specs=spec2,
        out_shape=jax.ShapeDtypeStruct((2, B, T, W), jnp.float32),
        scratch_shapes=[pltpu.VMEM((H, RW_HEAD_DIM, RW_HEAD_DIM), jnp.float32)],
        compiler_params=pltpu.CompilerParams(dimension_semantics=("parallel", "parallel", "arbitrary")),
        name="rwkv_chunked",
    )(r, kk, v, lw, akk, kr)


def short_conv(x, w):
    xp = jnp.pad(x, ((0, 0), (1, 1), (0, 0)))
    return xp[:, :-2] * w[0] + xp[:, 1:-1] * w[1] + xp[:, 2:] * w[2]


LANE = 128
RW_PREP_ROWS = 256
MLA_PAD_WIDTH = MLA_HEADS * LANE
MLA_PREP_ROWS = 256
ATTN_Q_ROWS = 512


def _split_dot(x, m):
    hi = x.astype(jnp.bfloat16)
    lo = (x - hi.astype(jnp.float32)).astype(jnp.bfloat16)
    return (jnp.dot(hi, m, preferred_element_type=jnp.float32) + jnp.dot(lo, m, preferred_element_type=jnp.float32))


def _rwkv_prep_kernel(z_ref, wda_ref, d0_ref, a0_ref, gup_ref, kk_ref_w, ka_ref, rk_ref, hsum_ref,
                      r_ref, kk_ref, v_ref, lw_ref, akk_ref, kr_ref, g_ref, bonus_ref):
    W = RW_WIDTH
    bf = jnp.bfloat16
    z = z_ref[0]
    r, k, v = z[:, :W], z[:, W:2 * W], z[:, 2 * W:3 * W]
    da = z[:, 3 * W:3 * W + LANE]
    lane = lax.broadcasted_iota(jnp.int32, da.shape, 1)
    da = jnp.where(lane < RW_DECAY_RANK, jnp.tanh(da), da)
    up = jnp.dot(da.astype(bf), wda_ref[...], preferred_element_type=jnp.float32)
    g_lo = z[:, 3 * W + LANE:]
    g_ref[0] = jnp.dot(jax.nn.sigmoid(g_lo).astype(bf), gup_ref[...], preferred_element_type=jnp.float32)
    hsum = hsum_ref[...]
    kk = k * kk_ref_w[...]
    kk = kk * lax.rsqrt(_split_dot(kk * kk, hsum) + L2_EPS)
    r_ref[0] = r
    v_ref[0] = v
    kk_ref[0] = kk
    bonus_ref[0] = _split_dot(r * k * rk_ref[...], hsum) * v
    for d in range(2):
        lw_ref[d, 0] = -RW_DECAY_SCALE * jax.nn.sigmoid(d0_ref[d:d + 1, :] + up[:, d * W:(d + 1) * W])
        a = jax.nn.sigmoid(a0_ref[d:d + 1, :] + up[:, (2 + d) * W:(3 + d) * W])
        akk_ref[d, 0] = kk * a
        kr_ref[d, 0] = k * (1.0 + (a - 1.0) * ka_ref[...])


def rwkv_prep(z, decay_up, decay0, a_up, a0, gate_up, k_k, k_a, r_k):
    B, L, _ = z.shape
    W = RW_WIDTH
    T = min(RW_PREP_ROWS, L)
    zero = jnp.zeros((RW_DECAY_RANK, 2 * W), jnp.float32)
    wda = jnp.concatenate([
        jnp.concatenate([decay_up[0], decay_up[1], zero], axis=1),
        jnp.concatenate([zero, a_up[0], a_up[1]], axis=1)], axis=0).astype(jnp.bfloat16)
    head = jnp.arange(W) // RW_HEAD_DIM
    hsum = (head[:, None] == head[None, :]).astype(jnp.bfloat16)
    row = lambda a: a.reshape(1, W)
    const = lambda a: pl.BlockSpec(a.shape, lambda b, i: (0,) * a.ndim)
    tok = pl.BlockSpec((1, T, W), lambda b, i: (b, i, 0))
    tok2 = pl.BlockSpec((2, 1, T, W), lambda b, i: (0, b, i, 0))
    f1 = jax.ShapeDtypeStruct((B, L, W), jnp.float32)
    f2 = jax.ShapeDtypeStruct((2, B, L, W), jnp.float32)
    args = (z, wda, decay0, a0, gate_up.astype(jnp.bfloat16), row(k_k), row(k_a), row(r_k), hsum)
    return pl.pallas_call(
        _rwkv_prep_kernel,
        grid=(B, L // T),
        in_specs=[pl.BlockSpec((1, T, RW_PROJ), lambda b, i: (b, i, 0))] + [const(a) for a in args[1:]],
        out_specs=[tok, tok, tok, tok2, tok2, tok2, tok, tok],
        out_shape=[f1, f1, f1, f2, f2, f2, f1, f1],
        compiler_params=pltpu.CompilerParams(dimension_semantics=("parallel", "parallel"),
                                             vmem_limit_bytes=VMEM_LIMIT_BYTES),
        name="rwkv_prep",
    )(*args)


def _rwkv_readout_kernel(y_ref, g_ref, bonus_ref, gng_ref, gnb_ref, hsum_ref, o_ref):
    y = y_ref[0, 0] + y_ref[1, 0]
    hsum = hsum_ref[...]
    mu = _split_dot(y, hsum) * (1.0 / RW_HEAD_DIM)
    d = y - mu
    var = _split_dot(d * d, hsum) * (1.0 / RW_HEAD_DIM)
    yn = d * lax.rsqrt(var + RW_GN_EPS) * gng_ref[...] + gnb_ref[...]
    o_ref[0] = (yn + bonus_ref[0]) * g_ref[0]


def rwkv_readout(y, g, bonus, gn_g, gn_b, t0):
    B, L, W = g.shape
    T = min(RW_PREP_ROWS, L)
    off = t0 // T
    head = jnp.arange(W) // RW_HEAD_DIM
    hsum = (head[:, None] == head[None, :]).astype(jnp.bfloat16)
    tok = pl.BlockSpec((1, T, W), lambda b, i: (b, i, 0))
    const = lambda a: pl.BlockSpec(a.shape, lambda b, i: (0,) * a.ndim)
    gg, gb = gn_g.reshape(1, W), gn_b.reshape(1, W)
    return pl.pallas_call(
        _rwkv_readout_kernel,
        grid=(B, L // T),
        in_specs=[pl.BlockSpec((2, 1, T, W), lambda b, i: (0, b, i + off, 0)), tok, tok, const(gg), const(gb), const(hsum)],
        out_specs=tok,
        out_shape=jax.ShapeDtypeStruct((B, L, W), jnp.float32),
        compiler_params=pltpu.CompilerParams(dimension_semantics=("parallel", "parallel")),
        name="rwkv_readout",
    )(y, g, bonus, gg, gb, hsum)


def rwkv7_mixer(p_lat, p_ctx, conv_w, decay_up, decay0, a_up, a0, gate_up, k_k, k_a, r_k, gn_g, gn_b, need_ctx):
    prm = (decay_up, decay0, a_up, a0, gate_up, k_k, k_a, r_k)
    lat = rwkv_prep(short_conv(p_lat, conv_w), *prm)
    ctx = rwkv_prep(short_conv(p_ctx, conv_w), *prm)
    n_ctx = p_ctx.shape[1]
    seq = lambda i: jnp.concatenate([ctx[i], lat[i]], axis=-2)
    y = rwkv_chunked(seq(0), seq(1), seq(2), seq(3), seq(4), seq(5), n_ctx)
    out_l = rwkv_readout(y, lat[6], lat[7], gn_g, gn_b, n_ctx)
    out_c = rwkv_readout(y, ctx[6], ctx[7], gn_g, gn_b, 0) if need_ctx else None
    return out_l, out_c


def _rope_tables(L, use_rope):
    lane = np.arange(LANE)
    in_rope = (lane >= MLA_NOPE_DIM) & (lane < MLA_QK_DIM)
    j = lane - MLA_NOPE_DIM
    axis = j // AXIS_ROPE_DIM
    half = AXIS_ROPE_DIM // 2
    f = j % half
    first = (j % AXIS_ROPE_DIM) < half
    inv = ROPE_THETA ** (-jnp.arange(0, AXIS_ROPE_DIM, 2, dtype=jnp.float32) / AXIS_ROPE_DIM)
    t = jnp.arange(L)
    pos = jnp.stack([t // GRID_W, t % GRID_W], axis=-1).astype(jnp.float32)
    ang = pos[:, np.clip(axis, 0, 1)] * inv[np.clip(f, 0, half - 1)][None, :]
    rope_on = jnp.asarray(in_rope)[None, :] & use_rope
    cos = jnp.where(rope_on, jnp.cos(ang), 1.0)
    sin = jnp.where(rope_on, jnp.sin(ang) * jnp.where(jnp.asarray(first), -1.0, 1.0)[None, :], 0.0)
    return jnp.tile(cos, (1, MLA_HEADS)), jnp.tile(sin, (1, MLA_HEADS))


def _mla_prep_kernel(p_ref, qn_ref, wq_ref, kvn_ref, wk_ref, wv_ref, place_ref, qg_ref, kg_ref, hsum_ref, cos_ref, sin_ref,
                     q_ref, k_ref, v_ref):
    bf = jnp.bfloat16
    p = p_ref[0]
    c_q = p[:, :MLA_Q_RANK]
    c_kv = p[:, MLA_Q_RANK:MLA_Q_RANK + MLA_KV_RANK]
    tail = p[:, MLA_Q_RANK + MLA_KV_RANK:]
    cqn = c_q * lax.rsqrt(jnp.mean(c_q * c_q, axis=-1, keepdims=True) + NORM_EPS) * qn_ref[...]
    ckn = c_kv * lax.rsqrt(jnp.mean(c_kv * c_kv, axis=-1, keepdims=True) + NORM_EPS) * kvn_ref[...]
    q = jnp.dot(cqn.astype(bf), wq_ref[...], preferred_element_type=jnp.float32)
    k = jnp.dot(ckn.astype(bf), wk_ref[...], preferred_element_type=jnp.float32) + _split_dot(tail, place_ref[...])
    v_ref[0] = jnp.dot(ckn.astype(bf), wv_ref[...], preferred_element_type=jnp.float32).astype(bf)
    hsum = hsum_ref[...]
    cos, sin = cos_ref[...], sin_ref[...]
    lane = lax.broadcasted_iota(jnp.int32, q.shape, 1)
    first = ((lane - MLA_NOPE_DIM) % AXIS_ROPE_DIM) < (AXIS_ROPE_DIM // 2)
    half = AXIS_ROPE_DIM // 2

    def finish(x, gain):
        x = x * lax.rsqrt(_split_dot(x * x, hsum) * (1.0 / MLA_QK_DIM) + NORM_EPS) * gain
        partner = jnp.where(first, pltpu.roll(x, MLA_PAD_WIDTH - half, 1), pltpu.roll(x, half, 1))
        return x * cos + partner * sin

    q_ref[0] = (finish(q, qg_ref[...]) * (MLA_QK_DIM ** -0.5)).astype(bf)
    k_ref[0] = finish(k, kg_ref[...]).astype(bf)


def mla_prep(p, use_rope, q_norm, w_uq, kv_norm, w_ukv, q_gain, k_gain):
    B, L, _ = p.shape
    T = min(MLA_PREP_ROWS, L)
    H = MLA_HEADS
    pad_cols = lambda w, d: jnp.pad(w.reshape(w.shape[0], H, d), ((0, 0), (0, 0), (0, LANE - d))).reshape(w.shape[0], H * LANE)
    wq = pad_cols(w_uq, MLA_QK_DIM).astype(jnp.bfloat16)
    ukv = w_ukv.reshape(MLA_KV_RANK, H, MLA_NOPE_DIM + MLA_V_DIM)
    wk = pad_cols(ukv[:, :, :MLA_NOPE_DIM].reshape(MLA_KV_RANK, H * MLA_NOPE_DIM), MLA_NOPE_DIM).astype(jnp.bfloat16)
    wv = ukv[:, :, MLA_NOPE_DIM:].reshape(MLA_KV_RANK, H * MLA_V_DIM).astype(jnp.bfloat16)
    lane = np.arange(H * LANE)
    place = jnp.asarray(((lane[None, :] % LANE) - MLA_NOPE_DIM == np.arange(MLA_ROPE_DIM)[:, None]), jnp.bfloat16)
    hsum = jnp.asarray((lane[:, None] // LANE) == (lane[None, :] // LANE), jnp.bfloat16)
    pad_gain = lambda g: jnp.tile(jnp.pad(g, (0, LANE - MLA_QK_DIM)), H).reshape(1, H * LANE)
    cos, sin = _rope_tables(L, use_rope)
    const = lambda a: pl.BlockSpec(a.shape, lambda b, i: (0,) * a.ndim)
    args = (p, q_norm.reshape(1, -1), wq, kv_norm.reshape(1, -1), wk, wv, place, pad_gain(q_gain), pad_gain(k_gain), hsum)
    pos = pl.BlockSpec((T, H * LANE), lambda b, i: (i, 0))
    return pl.pallas_call(
        _mla_prep_kernel,
        grid=(B, L // T),
        in_specs=[pl.BlockSpec((1, T, MLA_PROJ), lambda b, i: (b, i, 0))] + [const(a) for a in args[1:]] + [pos, pos],
        out_specs=[pl.BlockSpec((1, T, H * LANE), lambda b, i: (b, i, 0)), pl.BlockSpec((1, T, H * LANE), lambda b, i: (b, i, 0)),
                   pl.BlockSpec((1, T, MLA_WIDTH), lambda b, i: (b, i, 0))],
        out_shape=[jax.ShapeDtypeStruct((B, L, H * LANE), jnp.bfloat16), jax.ShapeDtypeStruct((B, L, H * LANE), jnp.bfloat16),
                   jax.ShapeDtypeStruct((B, L, MLA_WIDTH), jnp.bfloat16)],
        compiler_params=pltpu.CompilerParams(dimension_semantics=("parallel", "parallel"),
                                             vmem_limit_bytes=VMEM_LIMIT_BYTES),
        name="mla_prep",
    )(*args, cos, sin)


def _attn_kernel(q_ref, k_ref, v_ref, o_ref):
    lane = lax.broadcasted_iota(jnp.int32, (q_ref.shape[1], LANE), 1)
    for pair in range(MLA_HEADS // 2):
        v_pair = v_ref[0, :, pair * LANE:(pair + 1) * LANE]
        outs = []
        for h in (2 * pair, 2 * pair + 1):
            q = q_ref[0, :, h * LANE:(h + 1) * LANE]
            k = k_ref[0, :, h * LANE:(h + 1) * LANE]
            s = lax.dot_general(q, k, (((1,), (1,)), ((), ())), preferred_element_type=jnp.float32)
            e = jnp.exp(s - jnp.max(s, axis=-1, keepdims=True))
            o = jnp.dot(e.astype(jnp.bfloat16), v_pair, preferred_element_type=jnp.float32)
            outs.append(o / jnp.sum(e, axis=-1, keepdims=True))
        o_ref[0, :, pair * LANE:(pair + 1) * LANE] = jnp.where(lane < MLA_V_DIM, outs[0], outs[1])


def attention(q, k, v):
    B, Lq, P = q.shape
    Lk = k.shape[1]
    tq = min(ATTN_Q_ROWS, Lq)
    return pl.pallas_call(
        _attn_kernel,
        grid=(B, Lq // tq),
        in_specs=[pl.BlockSpec((1, tq, P), lambda b, i: (b, i, 0)),
                  pl.BlockSpec((1, Lk, P), lambda b, i: (b, 0, 0)),
                  pl.BlockSpec((1, Lk, MLA_WIDTH), lambda b, i: (b, 0, 0))],
        out_specs=pl.BlockSpec((1, tq, MLA_WIDTH), lambda b, i: (b, i, 0)),
        out_shape=jax.ShapeDtypeStruct((B, Lq, MLA_WIDTH), jnp.float32),
        compiler_params=pltpu.CompilerParams(dimension_semantics=("parallel", "parallel"),
                                             vmem_limit_bytes=VMEM_LIMIT_BYTES),
        name="mla_attention",
    )(q, k, v)


def mla_mixer(p_lat, p_ctx, q_norm, w_uq, kv_norm, w_ukv, q_gain, k_gain, need_ctx):
    prm = (q_norm, w_uq, kv_norm, w_ukv, q_gain, k_gain)
    q_l, k_l, v_l = mla_prep(p_lat, True, *prm)
    q_c, k_c, v_c = mla_prep(p_ctx, False, *prm)
    y_l = attention(q_l, jnp.concatenate([k_l, k_c], axis=1), jnp.concatenate([v_l, v_c], axis=1))
    y_c = attention(q_c, k_c, v_c) if need_ctx else None
    return y_l, y_c


def hyena_filters(L, w1, b1, freq1, w2, b2, freq2, w3, b3):
    tn = jnp.arange(L, dtype=jnp.float32) / L
    bands = jnp.arange(1, HY_POS_BANDS + 1, dtype=jnp.float32)
    ang = 2.0 * math.pi * tn[:, None] * bands[None, :]
    z = jnp.concatenate([tn[:, None], jnp.cos(ang), jnp.sin(ang)], axis=-1)
    h = jnp.sin(freq1 * (z @ w1 + b1))
    h = jnp.sin(freq2 * (h @ w2 + b2))
    h = (h @ w3 + b3).reshape(L, HY_ORDER, 2, HY_WIDTH)
    rates = jnp.abs(jnp.linspace(math.log(HY_DECAY_TARGET) / HY_LONG_DECAY_PCT,
                                 math.log(HY_DECAY_TARGET) / HY_SHORT_DECAY_PCT, HY_WIDTH))
    h = h * jnp.exp(-tn[:, None] * rates[None, :])[:, None, None, :]
    zero = jnp.zeros((1, HY_ORDER, HY_WIDTH), h.dtype)
    h_full = jnp.concatenate([h[:, :, 0], zero, h[:0:-1, :, 1]], axis=0)
    return h_full * lax.rsqrt(jnp.sum(jnp.square(h_full), axis=0, keepdims=True))


def fft_long_conv(u, h_full, bias):
    L = u.shape[1]
    uf = jnp.fft.rfft(u, n=2 * L, axis=1)
    hf = jnp.fft.rfft(h_full, n=2 * L, axis=0)
    y = jnp.fft.irfft(uf * hf[None], n=2 * L, axis=1)[:, :L]
    return y + u * bias


FFT_N1 = 64
FFT_N2 = 128
FFT_N = FFT_N1 * FFT_N2
HY_SEQS = 32


def _dft_tables(seqs):
    n1 = np.arange(FFT_N1)
    n2 = np.arange(FFT_N2)
    f64 = np.exp(-2j * np.pi * np.outer(n1, n1) / FFT_N1)
    f128 = np.exp(-2j * np.pi * np.outer(n2, n2) / FFT_N2)
    tw = np.exp(-2j * np.pi * np.outer(n1, n2) / FFT_N)
    half = FFT_N1 // 2
    fh = f64[:, :half]
    m1 = np.block([[fh.real, -fh.imag], [fh.imag, fh.real]])
    m1f = np.concatenate([f64.real, f64.imag], axis=0)
    m2 = np.block([[f128.real, f128.imag], [-f128.imag, f128.real]])
    m3 = np.block([[f128.real, -f128.imag], [f128.imag, f128.real]]) / FFT_N
    c = np.conj(f64)[:half, :]
    m4 = np.block([[c.real, -c.imag], [c.imag, c.real]])
    bf = lambda a: jnp.asarray(a, jnp.float32).astype(jnp.bfloat16)
    f32 = lambda a: jnp.asarray(a, jnp.float32)
    return dict(m1=bf(m1), m1f=bf(m1f), m2=bf(m2), m3=bf(m3), m4=bf(m4),
                twr_l=f32(np.tile(tw.real, (1, seqs))), twi_l=f32(np.tile(tw.imag, (1, seqs))),
                twr_s=f32(np.tile(tw.real, (seqs, 1))), twi_s=f32(np.tile(tw.imag, (seqs, 1))))


def _spectrum(cols, m1, twr_l, twi_l, m2, R):
    a = jnp.dot(m1, cols.astype(jnp.bfloat16), preferred_element_type=jnp.float32)
    ar, ai = a[:FFT_N1], a[FFT_N1:]
    pr = ar * twr_l - ai * twi_l
    pi = ar * twi_l + ai * twr_l
    lhs = jnp.concatenate(
        [jnp.concatenate([pr[:, r * FFT_N2:(r + 1) * FFT_N2], pi[:, r * FFT_N2:(r + 1) * FFT_N2]], axis=1)
         for r in range(R)], axis=0)
    return jnp.dot(lhs.astype(jnp.bfloat16), m2, preferred_element_type=jnp.float32)


def _filter_fft_kernel(h_ref, m1f_ref, twr_ref, twi_ref, m2_ref, o_ref):
    R = HY_SEQS
    cols = jnp.concatenate([h_ref[r] for r in range(R)], axis=1)
    x = _spectrum(cols, m1f_ref[...], twr_ref[...], twi_ref[...], m2_ref[...], R)
    o_ref[...] = x.reshape(R, FFT_N1, 2 * FFT_N2)


def _hyena_conv_kernel(y_ref, g_ref, hf_ref, bias_ref, m1_ref, twr_l_ref, twi_l_ref, m2_ref, m3_ref,
                       twr_s_ref, twi_s_ref, m4_ref, o_ref):
    R = HY_SEQS
    half = FFT_N1 // 2
    y = [y_ref[0], y_ref[1]]
    for o in range(HY_ORDER):
        top = jnp.concatenate([y[0][r] for r in range(R)], axis=1)
        bot = jnp.concatenate([y[1][r] for r in range(R)], axis=1)
        x = _spectrum(jnp.concatenate([top, bot], axis=0), m1_ref[...], twr_l_ref[...], twi_l_ref[...], m2_ref[...], R)
        hf = hf_ref[o].reshape(R * FFT_N1, 2 * FFT_N2)
        xr, xi = x[:, :FFT_N2], x[:, FFT_N2:]
        hr, hi = hf[:, :FFT_N2], hf[:, FFT_N2:]
        yc = jnp.concatenate([xr * hr - xi * hi, xr * hi + xi * hr], axis=1)
        b = jnp.dot(yc.astype(jnp.bfloat16), m3_ref[...], preferred_element_type=jnp.float32)
        br, bi = b[:, :FFT_N2], b[:, FFT_N2:]
        qr = br * twr_s_ref[...] + bi * twi_s_ref[...]
        qi = bi * twr_s_ref[...] - br * twi_s_ref[...]
        bc = jnp.concatenate(
            [jnp.concatenate([qr[r * FFT_N1:(r + 1) * FFT_N1], qi[r * FFT_N1:(r + 1) * FFT_N1]], axis=0)
             for r in range(R)], axis=1)
        yo = jnp.dot(m4_ref[...], bc.astype(jnp.bfloat16), preferred_element_type=jnp.float32)
        for p in range(2):
            conv = jnp.stack([yo[p * half:(p + 1) * half, r * FFT_N2:(r + 1) * FFT_N2] for r in range(R)], axis=0)
            y[p] = g_ref[o, p] * (conv + y[p] * bias_ref[o])
    o_ref[0] = y[0]
    o_ref[1] = y[1]


def hyena_long_conv(y_t, g_t, h_t, bias):
    B, C, L = y_t.shape
    assert 2 * L == FFT_N and B % 2 == 0 and C % HY_SEQS == 0
    R = HY_SEQS
    half = FFT_N1 // 2
    tb = _dft_tables(R)
    const = lambda a: pl.BlockSpec(a.shape, lambda *_: (0,) * a.ndim)
    hf = pl.pallas_call(
        _filter_fft_kernel,
        grid=(HY_ORDER * C // R,),
        in_specs=[pl.BlockSpec((R, FFT_N1, FFT_N2), lambda i: (i, 0, 0)),
                  const(tb['m1f']), const(tb['twr_l']), const(tb['twi_l']), const(tb['m2'])],
        out_specs=pl.BlockSpec((R, FFT_N1, 2 * FFT_N2), lambda i: (i, 0, 0)),
        out_shape=jax.ShapeDtypeStruct((HY_ORDER * C, FFT_N1, 2 * FFT_N2), jnp.float32),
        compiler_params=pltpu.CompilerParams(dimension_semantics=("parallel",), vmem_limit_bytes=VMEM_LIMIT_BYTES),
        name="hyena_filter_fft",
    )(h_t.reshape(HY_ORDER * C, FFT_N1, FFT_N2), tb['m1f'], tb['twr_l'], tb['twi_l'], tb['m2'])
    hf = hf.reshape(HY_ORDER, C, FFT_N1, 2 * FFT_N2)
    out = pl.pallas_call(
        _hyena_conv_kernel,
        grid=(B // 2, C // R),
        in_specs=[pl.BlockSpec((2, R, half, FFT_N2), lambda b, c: (b, c, 0, 0)),
                  pl.BlockSpec((HY_ORDER, 2, R, half, FFT_N2), lambda b, c: (0, b, c, 0, 0)),
                  pl.BlockSpec((HY_ORDER, R, FFT_N1, 2 * FFT_N2), lambda b, c: (0, c, 0, 0)),
                  pl.BlockSpec((HY_ORDER, R, 1, 1), lambda b, c: (0, c, 0, 0)),
                  const(tb['m1']), const(tb['twr_l']), const(tb['twi_l']), const(tb['m2']), const(tb['m3']),
                  const(tb['twr_s']), const(tb['twi_s']), const(tb['m4'])],
        out_specs=pl.BlockSpec((2, R, half, FFT_N2), lambda b, c: (b, c, 0, 0)),
        out_shape=jax.ShapeDtypeStruct((B, C, half, FFT_N2), jnp.float32),
        compiler_params=pltpu.CompilerParams(dimension_semantics=("parallel", "parallel"),
                                             vmem_limit_bytes=VMEM_LIMIT_BYTES),
        name="hyena_conv",
    )(y_t.reshape(B, C, half, FFT_N2), g_t.reshape(HY_ORDER, B, C, half, FFT_N2), hf,
      bias.reshape(HY_ORDER, C, 1, 1), tb['m1'], tb['twr_l'], tb['twi_l'], tb['m2'], tb['m3'],
      tb['twr_s'], tb['twi_s'], tb['m4'])
    return out.reshape(B, C, L)


def hyena_mixer(p, conv_w, w1, b1, freq1, w2, b2, freq2, w3, b3, bias):
    B, L = p.shape[:2]
    z = short_conv(p, conv_w)
    h_full = hyena_filters(L, w1, b1, freq1, w2, b2, freq2, w3, b3)
    if 2 * L == FFT_N:
        g_t = jnp.transpose(z[..., :HY_ORDER * HY_WIDTH].reshape(B, L, HY_ORDER, HY_WIDTH), (2, 0, 3, 1))
        y_t = jnp.swapaxes(z[..., HY_ORDER * HY_WIDTH:], 1, 2)
        y_t = hyena_long_conv(y_t, g_t, jnp.transpose(h_full, (1, 2, 0)), bias)
        return jnp.swapaxes(y_t, 1, 2)
    gates = (z[..., :HY_WIDTH], z[..., HY_WIDTH:2 * HY_WIDTH])
    y = z[..., 2 * HY_WIDTH:]
    for o in range(HY_ORDER):
        y = gates[o] * fft_long_conv(y, h_full[:, o], bias[o])
    return y


SC_CORES = 2
SC_SUBCORES = 16
SC_LANES = 16
SC_WORKERS = SC_CORES * SC_SUBCORES
PEER_SLOTS = PEER_HEADS * PEER_TOPK
PEER_GATHER_ROWS = 32
PEER_GATHERS = PEER_SLOTS // PEER_GATHER_ROWS
PEER_ACC_VREGS = 8


def _sc_peer(table, idx, aux, phase):
    N = idx.shape[0]
    tpw = N // SC_WORKERS
    assert tpw % 2 == 0 and N % SC_WORKERS == 0
    mesh = plsc.VectorSubcoreMesh(core_axis_name="c", subcore_axis_name="s")
    aux_shape = (D_MODEL,) if phase == "dot" else (PEER_SLOTS,)
    out_tok = (PEER_SLOTS,) if phase == "dot" else (D_MODEL,)

    @functools.partial(
        pl.kernel, mesh=mesh,
        out_type=jax.ShapeDtypeStruct((N,) + out_tok, jnp.float32),
        compiler_params=pltpu.CompilerParams(needs_layout_passes=False),
        scratch_types=[
            pltpu.VMEM((2, PEER_GATHERS, PEER_GATHER_ROWS), jnp.int32),
            pltpu.VMEM((2,) + aux_shape, jnp.float32),
            pltpu.VMEM((2, PEER_GATHER_ROWS, D_MODEL), jnp.float32),
            pltpu.VMEM((2,) + out_tok, jnp.float32),
            pltpu.SemaphoreType.DMA((2,)),
            pltpu.SemaphoreType.DMA((2,)),
            pltpu.SemaphoreType.DMA((2,)),
        ],
    )
    def k(table_hbm, idx_hbm, aux_hbm, out_hbm, idx_v, aux_v, rows_v, out_v, sem_r, sem_i, sem_o):
        wid = lax.axis_index("s") * SC_CORES + lax.axis_index("c")
        base = wid * tpw

        def gather(p, c, b):
            return pltpu.make_async_copy(table_hbm.at[idx_v.at[p, c]], rows_v.at[b], sem_r.at[b])

        def load_meta(t, p):
            return (pltpu.make_async_copy(idx_hbm.at[t], idx_v.at[p], sem_i.at[p]),
                    pltpu.make_async_copy(aux_hbm.at[t], aux_v.at[p], sem_i.at[p]))

        def store_out(t, p):
            return pltpu.make_async_copy(out_v.at[p], out_hbm.at[t], sem_o.at[p])

        def compute(p, c, b):
            if phase == "dot":
                lane = lax.iota(jnp.int32, SC_LANES)
                vec = jnp.zeros((SC_LANES,), jnp.float32)
                groups_per_vec = SC_LANES // PEER_ACC_VREGS
                for g in range(PEER_GATHER_ROWS // PEER_ACC_VREGS):
                    def body(cc, accs):
                        xv = aux_v[p, pl.ds(cc * SC_LANES, SC_LANES)]
                        return tuple(accs[r] + rows_v[b, g * PEER_ACC_VREGS + r, pl.ds(cc * SC_LANES, SC_LANES)] * xv
                                     for r in range(PEER_ACC_VREGS))
                    accs = lax.fori_loop(0, D_MODEL // SC_LANES, body,
                                         tuple(jnp.zeros((SC_LANES,), jnp.float32) for _ in range(PEER_ACC_VREGS)))
                    for r in range(PEER_ACC_VREGS):
                        vec = jnp.where(lane == (g % groups_per_vec) * PEER_ACC_VREGS + r, jnp.sum(accs[r]), vec)
                    if g % groups_per_vec == groups_per_vec - 1:
                        out_v[p, pl.ds(c * PEER_GATHER_ROWS + (g // groups_per_vec) * SC_LANES, SC_LANES)] = vec
            else:
                for db in range(D_MODEL // (PEER_ACC_VREGS * SC_LANES)):
                    def body(kk, accs):
                        wv = plsc.load_gather(aux_v.at[p], [jnp.full((SC_LANES,), c * PEER_GATHER_ROWS + kk, jnp.int32)])
                        return tuple(accs[j] + rows_v[b, kk, pl.ds((db * PEER_ACC_VREGS + j) * SC_LANES, SC_LANES)] * wv
                                     for j in range(PEER_ACC_VREGS))
                    if c == 0:
                        init = tuple(jnp.zeros((SC_LANES,), jnp.float32) for _ in range(PEER_ACC_VREGS))
                    else:
                        init = tuple(out_v[p, pl.ds((db * PEER_ACC_VREGS + j) * SC_LANES, SC_LANES)]
                                     for j in range(PEER_ACC_VREGS))
                    accs = lax.fori_loop(0, PEER_GATHER_ROWS, body, init)
                    for j in range(PEER_ACC_VREGS):
                        out_v[p, pl.ds((db * PEER_ACC_VREGS + j) * SC_LANES, SC_LANES)] = accs[j]

        for d in load_meta(base, 0):
            d.start()
        for d in load_meta(base, 0):
            d.wait()
        gather(0, 0, 0).start()

        @pl.loop(0, tpw // 2)
        def _(i2):
            for p in range(2):
                i = i2 * 2 + p
                t = base + i
                nxt = base + jnp.minimum(i + 1, tpw - 1)
                for d in load_meta(nxt, 1 - p):
                    d.start()

                @pl.when(i2 > 0)
                def _():
                    store_out(t, p).wait()

                for c in range(PEER_GATHERS):
                    b = c % 2
                    if c < PEER_GATHERS - 1:
                        gather(p, c + 1, 1 - b).start()
                    else:
                        for d in load_meta(nxt, 1 - p):
                            d.wait()
                        gather(1 - p, 0, 0).start()
                    gather(p, c, b).wait()
                    compute(p, c, b)
                store_out(t, p).start()

        gather(0, 0, 0).wait()
        for p in range(2):
            store_out(base, p).wait()

    return k(table, idx.reshape(N, PEER_GATHERS, PEER_GATHER_ROWS), aux)


PEER_TOKENS = 256
INT_BIG = 2 ** 30


def _extract_topk(cand_ref, ids_ref, val_out_ref, id_out_ref, row0):
    def body(r, carry):
        c = cand_ref[...]
        ids = ids_ref[...]
        m = jnp.max(c, axis=0, keepdims=True)
        sel = jnp.min(jnp.where(c == m, ids, INT_BIG), axis=0, keepdims=True)
        cand_ref[...] = jnp.where(ids == sel, -jnp.inf, c)
        val_out_ref[pl.ds(row0 + r, 1), :] = m
        id_out_ref[pl.ds(row0 + r, 1), :] = sel
        return carry
    lax.fori_loop(0, PEER_TOPK, body, 0)


def _peer_retrieve_kernel(x_ref, gain_ref, scale_ref, shift_ref, wq_ref, keys_ref,
                          h_ref, idx_out_ref, gate_out_ref,
                          s_ref, ids1_ref, sv_ref, si_ref, cand_ref, cid_ref, ts_ref, idx_ref, gate_ref):
    x = x_ref[0]
    y = x * lax.rsqrt(jnp.mean(x * x, axis=-1, keepdims=True) + NORM_EPS)
    h = (y * gain_ref[...]) * (1.0 + scale_ref[0]) + shift_ref[0]
    h_ref[0] = h
    q = jnp.dot(h.astype(jnp.bfloat16), wq_ref[...], preferred_element_type=jnp.float32)
    T = PEER_TOKENS
    K = PEER_TOPK
    ids1_ref[...] = lax.broadcasted_iota(jnp.int32, (PEER_N_KEYS, T), 0)
    for hd in range(PEER_HEADS):
        for p in range(2):
            hp = hd * 2 + p
            qs = q[:, hp * PEER_HALF:(hp + 1) * PEER_HALF].astype(jnp.bfloat16)
            s_ref[...] = lax.dot_general(keys_ref[hp], qs, (((1,), (1,)), ((), ())),
                                         preferred_element_type=jnp.float32)
            _extract_topk(s_ref, ids1_ref, sv_ref, si_ref, p * K)
        for i in range(K):
            cand_ref[i * K:(i + 1) * K, :] = sv_ref[i:i + 1, :] + sv_ref[K:2 * K, :]
            cid_ref[i * K:(i + 1) * K, :] = si_ref[i:i + 1, :] * PEER_N_KEYS + si_ref[K:2 * K, :]
        _extract_topk(cand_ref, cid_ref, ts_ref, idx_ref, hd * K)
        ts = ts_ref[hd * K:(hd + 1) * K, :]
        e = jnp.exp(ts - jnp.max(ts, axis=0, keepdims=True))
        gate_ref[hd * K:(hd + 1) * K, :] = e / jnp.sum(e, axis=0, keepdims=True)
    idx_out_ref[...] = idx_ref[...].T
    gate_out_ref[...] = gate_ref[...].T


def peer_retrieve(x, gain, scale, shift, w_q, sub_keys):
    B, L, D = x.shape
    T = PEER_TOKENS
    nt = L // T
    keys = sub_keys.reshape(PEER_HEADS * 2, PEER_N_KEYS, PEER_HALF).astype(jnp.bfloat16)
    return pl.pallas_call(
        _peer_retrieve_kernel,
        grid=(B, nt),
        in_specs=[
            pl.BlockSpec((1, T, D), lambda b, i: (b, i, 0)),
            pl.BlockSpec((1, D), lambda b, i: (0, 0)),
            pl.BlockSpec((1, 1, D), lambda b, i: (b, 0, 0)),
            pl.BlockSpec((1, 1, D), lambda b, i: (b, 0, 0)),
            pl.BlockSpec((D, PEER_HEADS * 2 * PEER_HALF), lambda b, i: (0, 0)),
            pl.BlockSpec((PEER_HEADS * 2, PEER_N_KEYS, PEER_HALF), lambda b, i: (0, 0, 0)),
        ],
        out_specs=[
            pl.BlockSpec((1, T, D), lambda b, i: (b, i, 0)),
            pl.BlockSpec((T, PEER_SLOTS), lambda b, i: (b * nt + i, 0)),
            pl.BlockSpec((T, PEER_SLOTS), lambda b, i: (b * nt + i, 0)),
        ],
        out_shape=[
            jax.ShapeDtypeStruct((B, L, D), jnp.float32),
            jax.ShapeDtypeStruct((B * L, PEER_SLOTS), jnp.int32),
            jax.ShapeDtypeStruct((B * L, PEER_SLOTS), jnp.float32),
        ],
        scratch_shapes=[
            pltpu.VMEM((PEER_N_KEYS, T), jnp.float32),
            pltpu.VMEM((PEER_N_KEYS, T), jnp.int32),
            pltpu.VMEM((2 * PEER_TOPK, T), jnp.float32),
            pltpu.VMEM((2 * PEER_TOPK, T), jnp.int32),
            pltpu.VMEM((PEER_TOPK * PEER_TOPK, T), jnp.float32),
            pltpu.VMEM((PEER_TOPK * PEER_TOPK, T), jnp.int32),
            pltpu.VMEM((PEER_SLOTS, T), jnp.float32),
            pltpu.VMEM((PEER_SLOTS, T), jnp.int32),
            pltpu.VMEM((PEER_SLOTS, T), jnp.float32),
        ],
        compiler_params=pltpu.CompilerParams(dimension_semantics=("parallel", "parallel"),
                                             vmem_limit_bytes=VMEM_LIMIT_BYTES),
        name="peer_retrieve",
    )(x, gain.reshape(1, D), scale, shift, w_q.astype(jnp.bfloat16), keys)


PEER_ACT_ROWS = 1024


def _peer_act_kernel(dots_ref, gate_ref, w_ref):
    a = dots_ref[...]
    w_ref[...] = gate_ref[...] * (0.5 * a * (1.0 + lax.erf(a * (2.0 ** -0.5))))


def peer_act(dots, gate):
    N = dots.shape[0]
    T = min(PEER_ACT_ROWS, N)
    spec = pl.BlockSpec((T, PEER_SLOTS), lambda i: (i, 0))
    return pl.pallas_call(
        _peer_act_kernel,
        grid=(N // T,),
        in_specs=[spec, spec],
        out_specs=spec,
        out_shape=jax.ShapeDtypeStruct((N, PEER_SLOTS), jnp.float32),
        compiler_params=pltpu.CompilerParams(dimension_semantics=("parallel",)),
        name="peer_act",
    )(dots, gate)


def peer_ffn(x, gain, scale, shift, w_q, sub_keys, exp_u, exp_v):
    B, L, D = x.shape
    N = B * L
    h, e_idx, gate = peer_retrieve(x, gain, scale, shift, w_q, sub_keys)
    dots = _sc_peer(exp_u, e_idx, h.reshape(N, D), "dot")
    return _sc_peer(exp_v, e_idx, peer_act(dots, gate), "wsum").reshape(B, L, D)


def _mix_and_retrieve(li, x, c, ctx, c_ctx, mod_w, mod_b, mix_norm, w_in, w_out, rw_conv, rw_decay_up, rw_decay0, rw_a_up, rw_a0, rw_gate_up, rw_k_k, rw_k_a, rw_r_k, rw_gn_g, rw_gn_b, mla_q_norm, mla_w_uq, mla_kv_norm, mla_w_ukv, mla_q_gain, mla_k_gain, hy_conv, hy_w1, hy_b1, hy_freq1, hy_w2, hy_b2, hy_freq2, hy_w3, hy_b3, hy_bias, ffn_norm, peer_wq, peer_keys, peer_u, peer_v):
    B, L, D = x.shape
    s_rw, s_mla = RW_PROJ, RW_PROJ + MLA_PROJ
    need_ctx = li < DEPTH - 1
    mod_l = (jax.nn.silu(c) @ mod_w[li] + mod_b[li])[:, None, :]
    mod_c = (jax.nn.silu(c_ctx) @ mod_w[li] + mod_b[li])[None, None, :]
    shm_l, scm_l, gm_l, shf_l, scf_l, gf_l = jnp.split(mod_l, N_MOD, axis=-1)
    shm_c, scm_c, gm_c, shf_c, scf_c, gf_c = jnp.split(mod_c, N_MOD, axis=-1)

    p_l = norm_mod_proj(x, mix_norm[li], scm_l, shm_l, w_in[li], 512)
    p_c = norm_mod_proj(ctx, mix_norm[li], jnp.broadcast_to(scm_c, (B, 1, D)),
                        jnp.broadcast_to(shm_c, (B, 1, D)), w_in[li], 256)
    rw_l, rw_c = rwkv7_mixer(p_l[..., :s_rw], p_c[..., :s_rw], rw_conv[li], rw_decay_up[li], rw_decay0[li],
                             rw_a_up[li], rw_a0[li], rw_gate_up[li], rw_k_k[li], rw_k_a[li], rw_r_k[li],
                             rw_gn_g[li], rw_gn_b[li], need_ctx)
    ml_l, ml_c = mla_mixer(p_l[..., s_rw:s_mla], p_c[..., s_rw:s_mla], mla_q_norm[li], mla_w_uq[li],
                           mla_kv_norm[li], mla_w_ukv[li], mla_q_gain[li], mla_k_gain[li], need_ctx)
    hy_prm = (hy_conv[li], hy_w1[li], hy_b1[li], hy_freq1[li], hy_w2[li], hy_b2[li], hy_freq2[li],
              hy_w3[li], hy_b3[li], hy_bias[li])
    hy_l = hyena_mixer(p_l[..., s_mla:], *hy_prm)
    x = x + gm_l * (jnp.concatenate([rw_l, ml_l, hy_l], axis=-1) @ w_out[li])
    if need_ctx:
        hy_c = hyena_mixer(p_c[..., s_mla:], *hy_prm)
        ctx = ctx + gm_c * (jnp.concatenate([rw_c, ml_c, hy_c], axis=-1) @ w_out[li])
        ctx = ctx + gf_c * peer_ffn(ctx, ffn_norm[li], jnp.broadcast_to(scf_c, (B, 1, D)),
                                    jnp.broadcast_to(shf_c, (B, 1, D)),
                                    peer_wq[li], peer_keys[li], peer_u[li], peer_v[li])
    h, e_idx, gate = peer_retrieve(x, ffn_norm[li], scf_l, shf_l, peer_wq[li], peer_keys[li])
    return x, ctx, gf_l, h.reshape(B * L, D), e_idx, gate


BATCH_GROUPS = 4


def kernel(x, c, ctx, c_ctx, mod_w, mod_b, mix_norm, w_in, w_out, rw_conv, rw_decay_up, rw_decay0, rw_a_up, rw_a0, rw_gate_up, rw_k_k, rw_k_a, rw_r_k, rw_gn_g, rw_gn_b, mla_q_norm, mla_w_uq, mla_kv_norm, mla_w_ukv, mla_q_gain, mla_k_gain, hy_conv, hy_w1, hy_b1, hy_freq1, hy_w2, hy_b2, hy_freq2, hy_w3, hy_b3, hy_bias, ffn_norm, peer_wq, peer_keys, peer_u, peer_v):
    params = (mod_w, mod_b, mix_norm, w_in, w_out, rw_conv, rw_decay_up, rw_decay0, rw_a_up, rw_a0, rw_gate_up,
              rw_k_k, rw_k_a, rw_r_k, rw_gn_g, rw_gn_b, mla_q_norm, mla_w_uq, mla_kv_norm, mla_w_ukv, mla_q_gain,
              mla_k_gain, hy_conv, hy_w1, hy_b1, hy_freq1, hy_w2, hy_b2, hy_freq2, hy_w3, hy_b3, hy_bias,
              ffn_norm, peer_wq, peer_keys, peer_u, peer_v)
    G = BATCH_GROUPS
    bg = x.shape[0] // G
    L, D = x.shape[1:]
    xs = [x[g * bg:(g + 1) * bg] for g in range(G)]
    cs = [c[g * bg:(g + 1) * bg] for g in range(G)]
    ctxs = [ctx[g * bg:(g + 1) * bg] for g in range(G)]
    stages = [(li, g) for li in range(DEPTH) for g in range(G)]
    pending = {}
    token = None
    for k in range(len(stages) + 1):
        if k < len(stages):
            li, g = stages[k]
            ins = (xs[g], ctxs[g])
            if token is not None:
                token, ins = lax.optimization_barrier((token, ins))
            xm, ctxs[g], gf, h, e_idx, gate = _mix_and_retrieve(li, ins[0], cs[g], ins[1], c_ctx, *params)
            dots = _sc_peer(peer_u[li], e_idx, h, "dot")
            pending[k] = (xm, gf, e_idx, gate, dots)
            token = gate
        if k >= 1:
            li, g = stages[k - 1]
            xm, gf, e_idx, gate, dots = pending.pop(k - 1)
            token, (dots, gate) = lax.optimization_barrier((token, (dots, gate)))
            w = peer_act(dots, gate)
            xs[g] = xm + gf * _sc_peer(peer_v[li], e_idx, w, "wsum").reshape(bg, L, D)
            token = w
    return jnp.concatenate(xs, axis=0)
```

```python
import functools
import math

import jax
import jax.numpy as jnp
import numpy as np
from jax import lax
from jax.experimental import pallas as pl
from jax.experimental.pallas import tpu as pltpu
from jax.experimental.pallas import tpu_sc as plsc

D_MODEL = 1024
DEPTH = 2
GRID_W = 64
N_MOD = 6
NORM_EPS = 1e-6

RW_HEADS = 6
RW_HEAD_DIM = 64
RW_WIDTH = RW_HEADS * RW_HEAD_DIM
RW_DECAY_RANK = 64
RW_A_RANK = 64
RW_GATE_RANK = 128
RW_DECAY_SCALE = 0.6065306597
RW_GN_EPS = 64e-5
L2_EPS = 1e-12

MLA_HEADS = 6
MLA_Q_RANK = 256
MLA_KV_RANK = 128
MLA_NOPE_DIM = 64
MLA_ROPE_DIM = 32
MLA_V_DIM = 64
MLA_QK_DIM = MLA_NOPE_DIM + MLA_ROPE_DIM
MLA_WIDTH = MLA_HEADS * MLA_V_DIM
AXIS_ROPE_DIM = MLA_ROPE_DIM // 2
ROPE_THETA = 10000.0

HY_WIDTH = 256
HY_ORDER = 2
HY_POS_BANDS = 16
HY_SHORT_DECAY_PCT = 0.3
HY_LONG_DECAY_PCT = 1.5
HY_DECAY_TARGET = 1e-2

PEER_HEADS = 8
PEER_N_KEYS = 128
PEER_TOPK = 16
PEER_QUERY_DIM = 256
PEER_HALF = PEER_QUERY_DIM // 2

RW_PROJ = 3 * RW_WIDTH + RW_DECAY_RANK + RW_A_RANK + RW_GATE_RANK
MLA_PROJ = MLA_Q_RANK + MLA_KV_RANK + MLA_ROPE_DIM
HY_PROJ = (HY_ORDER + 1) * HY_WIDTH
IN_PROJ = RW_PROJ + MLA_PROJ + HY_PROJ
MIX_WIDTH = RW_WIDTH + MLA_WIDTH + HY_WIDTH

VMEM_LIMIT_BYTES = 48 * 1024 * 1024


def _norm_mod_proj_kernel(x_ref, gain_ref, scale_ref, shift_ref, w_ref, o_ref):
    x = x_ref[0]
    y = x * lax.rsqrt(jnp.mean(x * x, axis=-1, keepdims=True) + NORM_EPS)
    y = y * gain_ref[...]
    y = y * (1.0 + scale_ref[0]) + shift_ref[0]
    o_ref[0] = jnp.dot(y.astype(jnp.bfloat16), w_ref[...], preferred_element_type=jnp.float32)


def norm_mod_proj(x, gain, scale, shift, w, block_rows):
    B, L, D = x.shape
    N = w.shape[1]
    return pl.pallas_call(
        _norm_mod_proj_kernel,
        grid=(B, L // block_rows),
        in_specs=[
            pl.BlockSpec((1, block_rows, D), lambda b, i: (b, i, 0)),
            pl.BlockSpec((1, D), lambda b, i: (0, 0)),
            pl.BlockSpec((1, 1, D), lambda b, i: (b, 0, 0)),
            pl.BlockSpec((1, 1, D), lambda b, i: (b, 0, 0)),
            pl.BlockSpec((D, N), lambda b, i: (0, 0)),
        ],
        out_specs=pl.BlockSpec((1, block_rows, N), lambda b, i: (b, i, 0)),
        out_shape=jax.ShapeDtypeStruct((B, L, N), jnp.float32),
        compiler_params=pltpu.CompilerParams(
            dimension_semantics=("parallel", "parallel"), vmem_limit_bytes=VMEM_LIMIT_BYTES),
        name="norm_mod_proj",
    )(x, gain.reshape(1, D), scale, shift, w.astype(jnp.bfloat16))


RW_CHUNK = 64


def _rwkv_chunk_kernel(r_ref, kk_ref, v_ref, lw_ref, akk_ref, kr_ref, y_ref, h_ref):
    d = pl.program_id(0)
    n = pl.program_id(2)

    @pl.when(n == 0)
    def _():
        h_ref[...] = jnp.zeros_like(h_ref)

    C = RW_CHUNK
    row = lax.broadcasted_iota(jnp.int32, (C, C), 0)
    col = lax.broadcasted_iota(jnp.int32, (C, C), 1)
    lag = (row - col) * (1 - 2 * d)
    before = lag > 0
    upto = lag >= 0
    tri = upto.astype(jnp.float32)
    eye = (row == col).astype(jnp.float32)
    bf = jnp.bfloat16
    f32 = jnp.float32

    def mm(a, b):
        return jnp.dot(a.astype(bf), b.astype(bf), preferred_element_type=f32)

    def mm_nt(a, b):
        return lax.dot_general(a.astype(bf), b.astype(bf), (((1,), (1,)), ((), ())), preferred_element_type=f32)

    def mm_tn(a, b):
        return lax.dot_general(a.astype(bf), b.astype(bf), (((0,), (0,)), ((), ())), preferred_element_type=f32)

    hs = range(RW_HEADS)
    HD = RW_HEAD_DIM
    heads = lambda t: [t[:, h * HD:(h + 1) * HD] for h in hs]
    r = heads(r_ref[0])
    kk = heads(kk_ref[0])
    v = heads(v_ref[0])
    lw = heads(lw_ref[0, 0])
    akk = heads(akk_ref[0, 0])
    kr = heads(kr_ref[0, 0])
    G = [jnp.dot(tri, lw[h], preferred_element_type=f32, precision=lax.Precision.HIGHEST) for h in hs]
    gtot = [jnp.sum(lw[h], axis=0, keepdims=True) for h in hs]
    Einv = [jnp.exp(-G[h]) for h in hs]
    At = [-kk[h] * jnp.exp(G[h] - lw[h]) for h in hs]
    Rt = [r[h] * jnp.exp(G[h]) for h in hs]
    Bt = [akk[h] * Einv[h] for h in hs]
    Kt = [kr[h] * Einv[h] for h in hs]
    X = [mm_nt(jnp.concatenate([At[h], Rt[h]], axis=0), jnp.concatenate([Bt[h], Kt[h]], axis=0)) for h in hs]
    M_ab = [jnp.where(before, X[h][:C, :C], 0.0) for h in hs]
    M_ak = [jnp.where(before, X[h][:C, C:], 0.0) for h in hs]
    A_rb = [jnp.where(upto, X[h][C:, :C], 0.0) for h in hs]
    A_rk = [jnp.where(upto, X[h][C:, C:], 0.0) for h in hs]
    MV = [mm(M_ak[h], v[h]) for h in hs]
    Mp = M_ab
    T = [eye + Mp[h] for h in hs]
    for _ in range(5):
        Mp = [jnp.dot(Mp[h], Mp[h], preferred_element_type=f32) for h in hs]
        T = [T[h] + jnp.dot(T[h], Mp[h], preferred_element_type=f32) for h in hs]
    WU = [jnp.dot(T[h], jnp.concatenate([At[h], MV[h]], axis=1), preferred_element_type=f32) for h in hs]
    H0 = [h_ref[h] for h in hs]
    Ehat = [jnp.exp(gtot[h] - G[h]) for h in hs]
    Om = [Rt[h] + mm(A_rb[h], WU[h][:, :HD]) for h in hs]
    Y0 = [mm(A_rb[h], WU[h][:, HD:]) + mm(A_rk[h], v[h]) for h in hs]
    BW = [mm_tn(akk[h] * Ehat[h], WU[h]) for h in hs]
    KV = [mm_tn(kr[h] * Ehat[h], v[h]) for h in hs]
    y_ref[0, 0] = jnp.concatenate([jnp.dot(Om[h], H0[h], preferred_element_type=f32) + Y0[h] for h in hs], axis=1)
    for h in hs:
        P = eye * jnp.exp(gtot[h]) + BW[h][:, :HD]
        h_ref[h] = jnp.dot(P, H0[h], preferred_element_type=f32) + BW[h][:, HD:] + KV[h]


def rwkv_chunked(r, kk, v, lw, akk, kr, n_ctx):
    B, T, W = r.shape
    H = W // RW_HEAD_DIM
    nc = n_ctx // RW_CHUNK
    nt = T // RW_CHUNK

    def chunk_of(d, n):
        bwd = jnp.where(n < nc, nc - 1 - n, nt - 1 - (n - nc))
        return jnp.where(d == 0, n, bwd)

    spec1 = pl.BlockSpec((1, RW_CHUNK, W), lambda d, b, n: (b, chunk_of(d, n), 0))
    spec2 = pl.BlockSpec((1, 1, RW_CHUNK, W), lambda d, b, n: (d, b, chunk_of(d, n), 0))
    return pl.pallas_call(
        _rwkv_chunk_kernel,
        grid=(2, B, nt),
        in_specs=[spec1, spec1, spec1, spec2, spec2, spec2],
        out_specs=spec2,
        out_shape=jax.ShapeDtypeStruct((2, B, T, W), jnp.float32),
        scratch_shapes=[pltpu.VMEM((H, RW_HEAD_DIM, RW_HEAD_DIM), jnp.float32)],
        compiler_params=pltpu.CompilerParams(dimension_semantics=("parallel", "parallel", "arbitrary")),
        name="rwkv_chunked",
    )(r, kk, v, lw, akk, kr)


def short_conv(x, w):
    xp = jnp.pad(x, ((0, 0), (1, 1), (0, 0)))
    return xp[:, :-2] * w[0] + xp[:, 1:-1] * w[1] + xp[:, 2:] * w[2]


LANE = 128
RW_PREP_ROWS = 256
MLA_PAD_WIDTH = MLA_HEADS * LANE
MLA_PREP_ROWS = 256
ATTN_Q_ROWS = 512


def _split_dot(x, m):
    hi = x.astype(jnp.bfloat16)
    lo = (x - hi.astype(jnp.float32)).astype(jnp.bfloat16)
    return (jnp.dot(hi, m, preferred_element_type=jnp.float32) + jnp.dot(lo, m, preferred_element_type=jnp.float32))


def _rwkv_prep_kernel(z_ref, wda_ref, d0_ref, a0_ref, gup_ref, kk_ref_w, ka_ref, rk_ref, hsum_ref,
                      r_ref, kk_ref, v_ref, lw_ref, akk_ref, kr_ref, g_ref, bonus_ref):
    W = RW_WIDTH
    bf = jnp.bfloat16
    z = z_ref[0]
    r, k, v = z[:, :W], z[:, W:2 * W], z[:, 2 * W:3 * W]
    da = z[:, 3 * W:3 * W + LANE]
    lane = lax.broadcasted_iota(jnp.int32, da.shape, 1)
    da = jnp.where(lane < RW_DECAY_RANK, jnp.tanh(da), da)
    up = jnp.dot(da.astype(bf), wda_ref[...], preferred_element_type=jnp.float32)
    g_lo = z[:, 3 * W + LANE:]
    g_ref[0] = jnp.dot(jax.nn.sigmoid(g_lo).astype(bf), gup_ref[...], preferred_element_type=jnp.float32)
    hsum = hsum_ref[...]
    kk = k * kk_ref_w[...]
    kk = kk * lax.rsqrt(_split_dot(kk * kk, hsum) + L2_EPS)
    r_ref[0] = r
    v_ref[0] = v
    kk_ref[0] = kk
    bonus_ref[0] = _split_dot(r * k * rk_ref[...], hsum) * v
    for d in range(2):
        lw_ref[d, 0] = -RW_DECAY_SCALE * jax.nn.sigmoid(d0_ref[d:d + 1, :] + up[:, d * W:(d + 1) * W])
        a = jax.nn.sigmoid(a0_ref[d:d + 1, :] + up[:, (2 + d) * W:(3 + d) * W])
        akk_ref[d, 0] = kk * a
        kr_ref[d, 0] = k * (1.0 + (a - 1.0) * ka_ref[...])


def rwkv_prep(z, decay_up, decay0, a_up, a0, gate_up, k_k, k_a, r_k):
    B, L, _ = z.shape
    W = RW_WIDTH
    T = min(RW_PREP_ROWS, L)
    zero = jnp.zeros((RW_DECAY_RANK, 2 * W), jnp.float32)
    wda = jnp.concatenate([
        jnp.concatenate([decay_up[0], decay_up[1], zero], axis=1),
        jnp.concatenate([zero, a_up[0], a_up[1]], axis=1)], axis=0).astype(jnp.bfloat16)
    head = jnp.arange(W) // RW_HEAD_DIM
    hsum = (head[:, None] == head[None, :]).astype(jnp.bfloat16)
    row = lambda a: a.reshape(1, W)
    const = lambda a: pl.BlockSpec(a.shape, lambda b, i: (0,) * a.ndim)
    tok = pl.BlockSpec((1, T, W), lambda b, i: (b, i, 0))
    tok2 = pl.BlockSpec((2, 1, T, W), lambda b, i: (0, b, i, 0))
    f1 = jax.ShapeDtypeStruct((B, L, W), jnp.float32)
    f2 = jax.ShapeDtypeStruct((2, B, L, W), jnp.float32)
    args = (z, wda, decay0, a0, gate_up.astype(jnp.bfloat16), row(k_k), row(k_a), row(r_k), hsum)
    return pl.pallas_call(
        _rwkv_prep_kernel,
        grid=(B, L // T),
        in_specs=[pl.BlockSpec((1, T, RW_PROJ), lambda b, i: (b, i, 0))] + [const(a) for a in args[1:]],
        out_specs=[tok, tok, tok, tok2, tok2, tok2, tok, tok],
        out_shape=[f1, f1, f1, f2, f2, f2, f1, f1],
        compiler_params=pltpu.CompilerParams(dimension_semantics=("parallel", "parallel"),
                                             vmem_limit_bytes=VMEM_LIMIT_BYTES),
        name="rwkv_prep",
    )(*args)


def _rwkv_readout_kernel(y_ref, g_ref, bonus_ref, gng_ref, gnb_ref, hsum_ref, o_ref):
    y = y_ref[0, 0] + y_ref[1, 0]
    hsum = hsum_ref[...]
    mu = _split_dot(y, hsum) * (1.0 / RW_HEAD_DIM)
    d = y - mu
    var = _split_dot(d * d, hsum) * (1.0 / RW_HEAD_DIM)
    yn = d * lax.rsqrt(var + RW_GN_EPS) * gng_ref[...] + gnb_ref[...]
    o_ref[0] = (yn + bonus_ref[0]) * g_ref[0]


def rwkv_readout(y, g, bonus, gn_g, gn_b, t0):
    B, L, W = g.shape
    T = min(RW_PREP_ROWS, L)
    off = t0 // T
    head = jnp.arange(W) // RW_HEAD_DIM
    hsum = (head[:, None] == head[None, :]).astype(jnp.bfloat16)
    tok = pl.BlockSpec((1, T, W), lambda b, i: (b, i, 0))
    const = lambda a: pl.BlockSpec(a.shape, lambda b, i: (0,) * a.ndim)
    gg, gb = gn_g.reshape(1, W), gn_b.reshape(1, W)
    return pl.pallas_call(
        _rwkv_readout_kernel,
        grid=(B, L // T),
        in_specs=[pl.BlockSpec((2, 1, T, W), lambda b, i: (0, b, i + off, 0)), tok, tok, const(gg), const(gb), const(hsum)],
        out_specs=tok,
        out_shape=jax.ShapeDtypeStruct((B, L, W), jnp.float32),
        compiler_params=pltpu.CompilerParams(dimension_semantics=("parallel", "parallel")),
        name="rwkv_readout",
    )(y, g, bonus, gg, gb, hsum)


def rwkv7_mixer(p_lat, p_ctx, conv_w, decay_up, decay0, a_up, a0, gate_up, k_k, k_a, r_k, gn_g, gn_b, need_ctx):
    prm = (decay_up, decay0, a_up, a0, gate_up, k_k, k_a, r_k)
    lat = rwkv_prep(short_conv(p_lat, conv_w), *prm)
    ctx = rwkv_prep(short_conv(p_ctx, conv_w), *prm)
    n_ctx = p_ctx.shape[1]
    seq = lambda i: jnp.concatenate([ctx[i], lat[i]], axis=-2)
    y = rwkv_chunked(seq(0), seq(1), seq(2), seq(3), seq(4), seq(5), n_ctx)
    out_l = rwkv_readout(y, lat[6], lat[7], gn_g, gn_b, n_ctx)
    out_c = rwkv_readout(y, ctx[6], ctx[7], gn_g, gn_b, 0) if need_ctx else None
    return out_l, out_c


def _rope_tables(L, use_rope):
    lane = np.arange(LANE)
    in_rope = (lane >= MLA_NOPE_DIM) & (lane < MLA_QK_DIM)
    j = lane - MLA_NOPE_DIM
    axis = j // AXIS_ROPE_DIM
    half = AXIS_ROPE_DIM // 2
    f = j % half
    first = (j % AXIS_ROPE_DIM) < half
    inv = ROPE_THETA ** (-jnp.arange(0, AXIS_ROPE_DIM, 2, dtype=jnp.float32) / AXIS_ROPE_DIM)
    t = jnp.arange(L)
    pos = jnp.stack([t // GRID_W, t % GRID_W], axis=-1).astype(jnp.float32)
    ang = pos[:, np.clip(axis, 0, 1)] * inv[np.clip(f, 0, half - 1)][None, :]
    rope_on = jnp.asarray(in_rope)[None, :] & use_rope
    cos = jnp.where(rope_on, jnp.cos(ang), 1.0)
    sin = jnp.where(rope_on, jnp.sin(ang) * jnp.where(jnp.asarray(first), -1.0, 1.0)[None, :], 0.0)
    return jnp.tile(cos, (1, MLA_HEADS)), jnp.tile(sin, (1, MLA_HEADS))


def _mla_prep_kernel(p_ref, qn_ref, wq_ref, kvn_ref, wk_ref, wv_ref, place_ref, qg_ref, kg_ref, hsum_ref, cos_ref, sin_ref,
                     q_ref, k_ref, v_ref):
    bf = jnp.bfloat16
    p = p_ref[0]
    c_q = p[:, :MLA_Q_RANK]
    c_kv = p[:, MLA_Q_RANK:MLA_Q_RANK + MLA_KV_RANK]
    tail = p[:, MLA_Q_RANK + MLA_KV_RANK:]
    cqn = c_q * lax.rsqrt(jnp.mean(c_q * c_q, axis=-1, keepdims=True) + NORM_EPS) * qn_ref[...]
    ckn = c_kv * lax.rsqrt(jnp.mean(c_kv * c_kv, axis=-1, keepdims=True) + NORM_EPS) * kvn_ref[...]
    q = jnp.dot(cqn.astype(bf), wq_ref[...], preferred_element_type=jnp.float32)
    k = jnp.dot(ckn.astype(bf), wk_ref[...], preferred_element_type=jnp.float32) + _split_dot(tail, place_ref[...])
    v_ref[0] = jnp.dot(ckn.astype(bf), wv_ref[...], preferred_element_type=jnp.float32).astype(bf)
    hsum = hsum_ref[...]
    cos, sin = cos_ref[...], sin_ref[...]
    lane = lax.broadcasted_iota(jnp.int32, q.shape, 1)
    first = ((lane - MLA_NOPE_DIM) % AXIS_ROPE_DIM) < (AXIS_ROPE_DIM // 2)
    half = AXIS_ROPE_DIM // 2

    def finish(x, gain):
        x = x * lax.rsqrt(_split_dot(x * x, hsum) * (1.0 / MLA_QK_DIM) + NORM_EPS) * gain
        partner = jnp.where(first, pltpu.roll(x, MLA_PAD_WIDTH - half, 1), pltpu.roll(x, half, 1))
        return x * cos + partner * sin

    q_ref[0] = (finish(q, qg_ref[...]) * (MLA_QK_DIM ** -0.5)).astype(bf)
    k_ref[0] = finish(k, kg_ref[...]).astype(bf)


def mla_prep(p, use_rope, q_norm, w_uq, kv_norm, w_ukv, q_gain, k_gain):
    B, L, _ = p.shape
    T = min(MLA_PREP_ROWS, L)
    H = MLA_HEADS
    pad_cols = lambda w, d: jnp.pad(w.reshape(w.shape[0], H, d), ((0, 0), (0, 0), (0, LANE - d))).reshape(w.shape[0], H * LANE)
    wq = pad_cols(w_uq, MLA_QK_DIM).astype(jnp.bfloat16)
    ukv = w_ukv.reshape(MLA_KV_RANK, H, MLA_NOPE_DIM + MLA_V_DIM)
    wk = pad_cols(ukv[:, :, :MLA_NOPE_DIM].reshape(MLA_KV_RANK, H * MLA_NOPE_DIM), MLA_NOPE_DIM).astype(jnp.bfloat16)
    wv = ukv[:, :, MLA_NOPE_DIM:].reshape(MLA_KV_RANK, H * MLA_V_DIM).astype(jnp.bfloat16)
    lane = np.arange(H * LANE)
    place = jnp.asarray(((lane[None, :] % LANE) - MLA_NOPE_DIM == np.arange(MLA_ROPE_DIM)[:, None]), jnp.bfloat16)
    hsum = jnp.asarray((lane[:, None] // LANE) == (lane[None, :] // LANE), jnp.bfloat16)
    pad_gain = lambda g: jnp.tile(jnp.pad(g, (0, LANE - MLA_QK_DIM)), H).reshape(1, H * LANE)
    cos, sin = _rope_tables(L, use_rope)
    const = lambda a: pl.BlockSpec(a.shape, lambda b, i: (0,) * a.ndim)
    args = (p, q_norm.reshape(1, -1), wq, kv_norm.reshape(1, -1), wk, wv, place, pad_gain(q_gain), pad_gain(k_gain), hsum)
    pos = pl.BlockSpec((T, H * LANE), lambda b, i: (i, 0))
    return pl.pallas_call(
        _mla_prep_kernel,
        grid=(B, L // T),
        in_specs=[pl.BlockSpec((1, T, MLA_PROJ), lambda b, i: (b, i, 0))] + [const(a) for a in args[1:]] + [pos, pos],
        out_specs=[pl.BlockSpec((1, T, H * LANE), lambda b, i: (b, i, 0)), pl.BlockSpec((1, T, H * LANE), lambda b, i: (b, i, 0)),
                   pl.BlockSpec((1, T, MLA_WIDTH), lambda b, i: (b, i, 0))],
        out_shape=[jax.ShapeDtypeStruct((B, L, H * LANE), jnp.bfloat16), jax.ShapeDtypeStruct((B, L, H * LANE), jnp.bfloat16),
                   jax.ShapeDtypeStruct((B, L, MLA_WIDTH), jnp.bfloat16)],
        compiler_params=pltpu.CompilerParams(dimension_semantics=("parallel", "parallel"),
                                             vmem_limit_bytes=VMEM_LIMIT_BYTES),
        name="mla_prep",
    )(*args, cos, sin)


def _attn_kernel(q_ref, k_ref, v_ref, o_ref):
    lane = lax.broadcasted_iota(jnp.int32, (q_ref.shape[1], LANE), 1)
    for pair in range(MLA_HEADS // 2):
        v_pair = v_ref[0, :, pair * LANE:(pair + 1) * LANE]
        outs = []
        for h in (2 * pair, 2 * pair + 1):
            q = q_ref[0, :, h * LANE:(h + 1) * LANE]
            k = k_ref[0, :, h * LANE:(h + 1) * LANE]
            s = lax.dot_general(q, k, (((1,), (1,)), ((), ())), preferred_element_type=jnp.float32)
            e = jnp.exp(s - jnp.max(s, axis=-1, keepdims=True))
            o = jnp.dot(e.astype(jnp.bfloat16), v_pair, preferred_element_type=jnp.float32)
            outs.append(o / jnp.sum(e, axis=-1, keepdims=True))
        o_ref[0, :, pair * LANE:(pair + 1) * LANE] = jnp.where(lane < MLA_V_DIM, outs[0], outs[1])


def attention(q, k, v):
    B, Lq, P = q.shape
    Lk = k.shape[1]
    tq = min(ATTN_Q_ROWS, Lq)
    return pl.pallas_call(
        _attn_kernel,
        grid=(B, Lq // tq),
        in_specs=[pl.BlockSpec((1, tq, P), lambda b, i: (b, i, 0)),
                  pl.BlockSpec((1, Lk, P), lambda b, i: (b, 0, 0)),
                  pl.BlockSpec((1, Lk, MLA_WIDTH), lambda b, i: (b, 0, 0))],
        out_specs=pl.BlockSpec((1, tq, MLA_WIDTH), lambda b, i: (b, i, 0)),
        out_shape=jax.ShapeDtypeStruct((B, Lq, MLA_WIDTH), jnp.float32),
        compiler_params=pltpu.CompilerParams(dimension_semantics=("parallel", "parallel"),
                                             vmem_limit_bytes=VMEM_LIMIT_BYTES),
        name="mla_attention",
    )(q, k, v)


def mla_mixer(p_lat, p_ctx, q_norm, w_uq, kv_norm, w_ukv, q_gain, k_gain, need_ctx):
    prm = (q_norm, w_uq, kv_norm, w_ukv, q_gain, k_gain)
    q_l, k_l, v_l = mla_prep(p_lat, True, *prm)
    q_c, k_c, v_c = mla_prep(p_ctx, False, *prm)
    y_l = attention(q_l, jnp.concatenate([k_l, k_c], axis=1), jnp.concatenate([v_l, v_c], axis=1))
    y_c = attention(q_c, k_c, v_c) if need_ctx else None
    return y_l, y_c


def hyena_filters(L, w1, b1, freq1, w2, b2, freq2, w3, b3):
    tn = jnp.arange(L, dtype=jnp.float32) / L
    bands = jnp.arange(1, HY_POS_BANDS + 1, dtype=jnp.float32)
    ang = 2.0 * math.pi * tn[:, None] * bands[None, :]
    z = jnp.concatenate([tn[:, None], jnp.cos(ang), jnp.sin(ang)], axis=-1)
    h = jnp.sin(freq1 * (z @ w1 + b1))
    h = jnp.sin(freq2 * (h @ w2 + b2))
    h = (h @ w3 + b3).reshape(L, HY_ORDER, 2, HY_WIDTH)
    rates = jnp.abs(jnp.linspace(math.log(HY_DECAY_TARGET) / HY_LONG_DECAY_PCT,
                                 math.log(HY_DECAY_TARGET) / HY_SHORT_DECAY_PCT, HY_WIDTH))
    h = h * jnp.exp(-tn[:, None] * rates[None, :])[:, None, None, :]
    zero = jnp.zeros((1, HY_ORDER, HY_WIDTH), h.dtype)
    h_full = jnp.concatenate([h[:, :, 0], zero, h[:0:-1, :, 1]], axis=0)
    return h_full * lax.rsqrt(jnp.sum(jnp.square(h_full), axis=0, keepdims=True))


def fft_long_conv(u, h_full, bias):
    L = u.shape[1]
    uf = jnp.fft.rfft(u, n=2 * L, axis=1)
    hf = jnp.fft.rfft(h_full, n=2 * L, axis=0)
    y = jnp.fft.irfft(uf * hf[None], n=2 * L, axis=1)[:, :L]
    return y + u * bias


FFT_N1 = 64
FFT_N2 = 128
FFT_N = FFT_N1 * FFT_N2
HY_SEQS = 32


def _dft_tables(seqs):
    n1 = np.arange(FFT_N1)
    n2 = np.arange(FFT_N2)
    f64 = np.exp(-2j * np.pi * np.outer(n1, n1) / FFT_N1)
    f128 = np.exp(-2j * np.pi * np.outer(n2, n2) / FFT_N2)
    tw = np.exp(-2j * np.pi * np.outer(n1, n2) / FFT_N)
    half = FFT_N1 // 2
    fh = f64[:, :half]
    m1 = np.block([[fh.real, -fh.imag], [fh.imag, fh.real]])
    m1f = np.concatenate([f64.real, f64.imag], axis=0)
    m2 = np.block([[f128.real, f128.imag], [-f128.imag, f128.real]])
    m3 = np.block([[f128.real, -f128.imag], [f128.imag, f128.real]]) / FFT_N
    c = np.conj(f64)[:half, :]
    m4 = np.block([[c.real, -c.imag], [c.imag, c.real]])
    bf = lambda a: jnp.asarray(a, jnp.float32).astype(jnp.bfloat16)
    f32 = lambda a: jnp.asarray(a, jnp.float32)
    return dict(m1=bf(m1), m1f=bf(m1f), m2=bf(m2), m3=bf(m3), m4=bf(m4),
                twr_l=f32(np.tile(tw.real, (1, seqs))), twi_l=f32(np.tile(tw.imag, (1, seqs))),
                twr_s=f32(np.tile(tw.real, (seqs, 1))), twi_s=f32(np.tile(tw.imag, (seqs, 1))))


def _spectrum(cols, m1, twr_l, twi_l, m2, R):
    a = jnp.dot(m1, cols.astype(jnp.bfloat16), preferred_element_type=jnp.float32)
    ar, ai = a[:FFT_N1], a[FFT_N1:]
    pr = ar * twr_l - ai * twi_l
    pi = ar * twi_l + ai * twr_l
    lhs = jnp.concatenate(
        [jnp.concatenate([pr[:, r * FFT_N2:(r + 1) * FFT_N2], pi[:, r * FFT_N2:(r + 1) * FFT_N2]], axis=1)
         for r in range(R)], axis=0)
    return jnp.dot(lhs.astype(jnp.bfloat16), m2, preferred_element_type=jnp.float32)


def _filter_fft_kernel(h_ref, m1f_ref, twr_ref, twi_ref, m2_ref, o_ref):
    R = HY_SEQS
    cols = jnp.concatenate([h_ref[r] for r in range(R)], axis=1)
    x = _spectrum(cols, m1f_ref[...], twr_ref[...], twi_ref[...], m2_ref[...], R)
    o_ref[...] = x.reshape(R, FFT_N1, 2 * FFT_N2)


def _hyena_conv_kernel(y_ref, g_ref, hf_ref, bias_ref, m1_ref, twr_l_ref, twi_l_ref, m2_ref, m3_ref,
                       twr_s_ref, twi_s_ref, m4_ref, o_ref):
    R = HY_SEQS
    half = FFT_N1 // 2
    y = [y_ref[0], y_ref[1]]
    for o in range(HY_ORDER):
        top = jnp.concatenate([y[0][r] for r in range(R)], axis=1)
        bot = jnp.concatenate([y[1][r] for r in range(R)], axis=1)
        x = _spectrum(jnp.concatenate([top, bot], axis=0), m1_ref[...], twr_l_ref[...], twi_l_ref[...], m2_ref[...], R)
        hf = hf_ref[o].reshape(R * FFT_N1, 2 * FFT_N2)
        xr, xi = x[:, :FFT_N2], x[:, FFT_N2:]
        hr, hi = hf[:, :FFT_N2], hf[:, FFT_N2:]
        yc = jnp.concatenate([xr * hr - xi * hi, xr * hi + xi * hr], axis=1)
        b = jnp.dot(yc.astype(jnp.bfloat16), m3_ref[...], preferred_element_type=jnp.float32)
        br, bi = b[:, :FFT_N2], b[:, FFT_N2:]
        qr = br * twr_s_ref[...] + bi * twi_s_ref[...]
        qi = bi * twr_s_ref[...] - br * twi_s_ref[...]
        bc = jnp.concatenate(
            [jnp.concatenate([qr[r * FFT_N1:(r + 1) * FFT_N1], qi[r * FFT_N1:(r + 1) * FFT_N1]], axis=0)
             for r in range(R)], axis=1)
        yo = jnp.dot(m4_ref[...], bc.astype(jnp.bfloat16), preferred_element_type=jnp.float32)
        for p in range(2):
            conv = jnp.stack([yo[p * half:(p + 1) * half, r * FFT_N2:(r + 1) * FFT_N2] for r in range(R)], axis=0)
            y[p] = g_ref[o, p] * (conv + y[p] * bias_ref[o])
    o_ref[0] = y[0]
    o_ref[1] = y[1]


def hyena_long_conv(y_t, g_t, h_t, bias):
    B, C, L = y_t.shape
    assert 2 * L == FFT_N and B % 2 == 0 and C % HY_SEQS == 0
    R = HY_SEQS
    half = FFT_N1 // 2
    tb = _dft_tables(R)
    const = lambda a: pl.BlockSpec(a.shape, lambda *_: (0,) * a.ndim)
    hf = pl.pallas_call(
        _filter_fft_kernel,
        grid=(HY_ORDER * C // R,),
        in_specs=[pl.BlockSpec((R, FFT_N1, FFT_N2), lambda i: (i, 0, 0)),
                  const(tb['m1f']), const(tb['twr_l']), const(tb['twi_l']), const(tb['m2'])],
        out_specs=pl.BlockSpec((R, FFT_N1, 2 * FFT_N2), lambda i: (i, 0, 0)),
        out_shape=jax.ShapeDtypeStruct((HY_ORDER * C, FFT_N1, 2 * FFT_N2), jnp.float32),
        compiler_params=pltpu.CompilerParams(dimension_semantics=("parallel",), vmem_limit_bytes=VMEM_LIMIT_BYTES),
        name="hyena_filter_fft",
    )(h_t.reshape(HY_ORDER * C, FFT_N1, FFT_N2), tb['m1f'], tb['twr_l'], tb['twi_l'], tb['m2'])
    hf = hf.reshape(HY_ORDER, C, FFT_N1, 2 * FFT_N2)
    out = pl.pallas_call(
        _hyena_conv_kernel,
        grid=(B // 2, C // R),
        in_specs=[pl.BlockSpec((2, R, half, FFT_N2), lambda b, c: (b, c, 0, 0)),
                  pl.BlockSpec((HY_ORDER, 2, R, half, FFT_N2), lambda b, c: (0, b, c, 0, 0)),
                  pl.BlockSpec((HY_ORDER, R, FFT_N1, 2 * FFT_N2), lambda b, c: (0, c, 0, 0)),
                  pl.BlockSpec((HY_ORDER, R, 1, 1), lambda b, c: (0, c, 0, 0)),
                  const(tb['m1']), const(tb['twr_l']), const(tb['twi_l']), const(tb['m2']), const(tb['m3']),
                  const(tb['twr_s']), const(tb['twi_s']), const(tb['m4'])],
        out_specs=pl.BlockSpec((2, R, half, FFT_N2), lambda b, c: (b, c, 0, 0)),
        out_shape=jax.ShapeDtypeStruct((B, C, half, FFT_N2), jnp.float32),
        compiler_params=pltpu.CompilerParams(dimension_semantics=("parallel", "parallel"),
                                             vmem_limit_bytes=VMEM_LIMIT_BYTES),
        name="hyena_conv",
    )(y_t.reshape(B, C, half, FFT_N2), g_t.reshape(HY_ORDER, B, C, half, FFT_N2), hf,
      bias.reshape(HY_ORDER, C, 1, 1), tb['m1'], tb['twr_l'], tb['twi_l'], tb['m2'], tb['m3'],
      tb['twr_s'], tb['twi_s'], tb['m4'])
    return out.reshape(B, C, L)


def hyena_mixer(p, conv_w, w1, b1, freq1, w2, b2, freq2, w3, b3, bias):
    B, L = p.shape[:2]
    z = short_conv(p, conv_w)
    h_full = hyena_filters(L, w1, b1, freq1, w2, b2, freq2, w3, b3)
    if 2 * L == FFT_N:
        g_t = jnp.transpose(z[..., :HY_ORDER * HY_WIDTH].reshape(B, L, HY_ORDER, HY_WIDTH), (2, 0, 3, 1))
        y_t = jnp.swapaxes(z[..., HY_ORDER * HY_WIDTH:], 1, 2)
        y_t = hyena_long_conv(y_t, g_t, jnp.transpose(h_full, (1, 2, 0)), bias)
        return jnp.swapaxes(y_t, 1, 2)
    gates = (z[..., :HY_WIDTH], z[..., HY_WIDTH:2 * HY_WIDTH])
    y = z[..., 2 * HY_WIDTH:]
    for o in range(HY_ORDER):
        y = gates[o] * fft_long_conv(y, h_full[:, o], bias[o])
    return y


SC_CORES = 2
SC_SUBCORES = 16
SC_LANES = 16
SC_WORKERS = SC_CORES * SC_SUBCORES
PEER_SLOTS = PEER_HEADS * PEER_TOPK
PEER_GATHER_ROWS = 32
PEER_GATHERS = PEER_SLOTS // PEER_GATHER_ROWS
PEER_ACC_VREGS = 8
PEER_ROW_BUFFERS = 4
PEER_ROW_WORDS = D_MODEL // 2
HI_MASK = -65536


def pack_expert_table(t):
    b = lax.bitcast_convert_type(t.astype(jnp.bfloat16), jnp.uint16).astype(jnp.uint32)
    return lax.bitcast_convert_type(b[:, :PEER_ROW_WORDS] | (b[:, PEER_ROW_WORDS:] << 16), jnp.int32)


def _sc_peer(table, idx, aux, phase):
    N = idx.shape[0]
    tpw = N // SC_WORKERS
    assert tpw % 2 == 0 and N % SC_WORKERS == 0
    assert PEER_GATHERS % PEER_ROW_BUFFERS == 0
    mesh = plsc.VectorSubcoreMesh(core_axis_name="c", subcore_axis_name="s")
    aux_shape = (D_MODEL,) if phase == "dot" else (PEER_SLOTS,)
    out_tok = (PEER_SLOTS,) if phase == "dot" else (D_MODEL,)
    NBUF = PEER_ROW_BUFFERS
    AHEAD = NBUF - 1
    HW = PEER_ROW_WORDS

    @functools.partial(
        pl.kernel, mesh=mesh,
        out_type=jax.ShapeDtypeStruct((N,) + out_tok, jnp.float32),
        compiler_params=pltpu.CompilerParams(needs_layout_passes=False),
        scratch_types=[
            pltpu.VMEM((2, PEER_GATHERS, PEER_GATHER_ROWS), jnp.int32),
            pltpu.VMEM((2,) + aux_shape, jnp.float32),
            pltpu.VMEM((NBUF, PEER_GATHER_ROWS, HW), jnp.int32),
            pltpu.VMEM((2,) + out_tok, jnp.float32),
            pltpu.SemaphoreType.DMA((NBUF,)),
            pltpu.SemaphoreType.DMA((2,)),
            pltpu.SemaphoreType.DMA((2,)),
        ],
    )
    def k(table_hbm, idx_hbm, aux_hbm, out_hbm, idx_v, aux_v, rows_v, out_v, sem_r, sem_i, sem_o):
        wid = lax.axis_index("s") * SC_CORES + lax.axis_index("c")
        base = wid * tpw

        def gather(p, c, b):
            return pltpu.make_async_copy(table_hbm.at[idx_v.at[p, c]], rows_v.at[b], sem_r.at[b])

        def load_meta(t, p):
            return (pltpu.make_async_copy(idx_hbm.at[t], idx_v.at[p], sem_i.at[p]),
                    pltpu.make_async_copy(aux_hbm.at[t], aux_v.at[p], sem_i.at[p]))

        def store_out(t, p):
            return pltpu.make_async_copy(out_v.at[p], out_hbm.at[t], sem_o.at[p])

        def halves(word):
            return (plsc.bitcast(lax.shift_left(word, 16), jnp.float32), plsc.bitcast(word & HI_MASK, jnp.float32))

        def compute(p, c, b):
            if phase == "dot":
                lane = lax.iota(jnp.int32, SC_LANES)
                vec = jnp.zeros((SC_LANES,), jnp.float32)
                groups_per_vec = SC_LANES // PEER_ACC_VREGS
                for g in range(PEER_GATHER_ROWS // PEER_ACC_VREGS):
                    def body(cc, accs):
                        x_lo = aux_v[p, pl.ds(cc * SC_LANES, SC_LANES)]
                        x_hi = aux_v[p, pl.ds(HW + cc * SC_LANES, SC_LANES)]
                        out = []
                        for r in range(PEER_ACC_VREGS):
                            lo, hi = halves(rows_v[b, g * PEER_ACC_VREGS + r, pl.ds(cc * SC_LANES, SC_LANES)])
                            out.append(accs[r] + lo * x_lo + hi * x_hi)
                        return tuple(out)
                    accs = lax.fori_loop(0, HW // SC_LANES, body,
                                         tuple(jnp.zeros((SC_LANES,), jnp.float32) for _ in range(PEER_ACC_VREGS)))
                    for r in range(PEER_ACC_VREGS):
                        vec = jnp.where(lane == (g % groups_per_vec) * PEER_ACC_VREGS + r, jnp.sum(accs[r]), vec)
                    if g % groups_per_vec == groups_per_vec - 1:
                        out_v[p, pl.ds(c * PEER_GATHER_ROWS + (g // groups_per_vec) * SC_LANES, SC_LANES)] = vec
            else:
                words = PEER_ACC_VREGS // 2
                for db in range(HW // (words * SC_LANES)):
                    def body(kk, accs):
                        wv = plsc.load_gather(aux_v.at[p], [jnp.full((SC_LANES,), c * PEER_GATHER_ROWS + kk, jnp.int32)])
                        out = []
                        for j in range(words):
                            lo, hi = halves(rows_v[b, kk, pl.ds((db * words + j) * SC_LANES, SC_LANES)])
                            out += [accs[2 * j] + lo * wv, accs[2 * j + 1] + hi * wv]
                        return tuple(out)
                    if c == 0:
                        init = tuple(jnp.zeros((SC_LANES,), jnp.float32) for _ in range(2 * words))
                    else:
                        init = tuple(out_v[p, pl.ds(half * HW + (db * words + j) * SC_LANES, SC_LANES)]
                                     for j in range(words) for half in range(2))
                    accs = lax.fori_loop(0, PEER_GATHER_ROWS, body, init)
                    for j in range(words):
                        out_v[p, pl.ds((db * words + j) * SC_LANES, SC_LANES)] = accs[2 * j]
                        out_v[p, pl.ds(HW + (db * words + j) * SC_LANES, SC_LANES)] = accs[2 * j + 1]

        for d in load_meta(base, 0):
            d.start()
        for d in load_meta(base, 0):
            d.wait()
        for c in range(AHEAD):
            gather(0, c, c % NBUF).start()

        @pl.loop(0, tpw // 2)
        def _(i2):
            for p in range(2):
                i = i2 * 2 + p
                t = base + i
                nxt = base + jnp.minimum(i + 1, tpw - 1)
                for d in load_meta(nxt, 1 - p):
                    d.start()

                @pl.when(i2 > 0)
                def _():
                    store_out(t, p).wait()

                for c in range(PEER_GATHERS):
                    ahead = c + AHEAD
                    if ahead < PEER_GATHERS:
                        gather(p, ahead, ahead % NBUF).start()
                    else:
                        if ahead == PEER_GATHERS:
                            for d in load_meta(nxt, 1 - p):
                                d.wait()
                        gather(1 - p, ahead - PEER_GATHERS, ahead % NBUF).start()
                    gather(p, c, c % NBUF).wait()
                    compute(p, c, c % NBUF)
                store_out(t, p).start()

        for c in range(AHEAD):
            gather(0, c, c % NBUF).wait()
        for p in range(2):
            store_out(base, p).wait()

    return k(table, idx.reshape(N, PEER_GATHERS, PEER_GATHER_ROWS), aux)


PEER_TOKENS = 256
INT_BIG = 2 ** 30


def _extract_topk(cand_ref, ids_ref, val_out_ref, id_out_ref, row0):
    def body(r, carry):
        c = cand_ref[...]
        ids = ids_ref[...]
        m = jnp.max(c, axis=0, keepdims=True)
        sel = jnp.min(jnp.where(c == m, ids, INT_BIG), axis=0, keepdims=True)
        cand_ref[...] = jnp.where(ids == sel, -jnp.inf, c)
        val_out_ref[pl.ds(row0 + r, 1), :] = m
        id_out_ref[pl.ds(row0 + r, 1), :] = sel
        return carry
    lax.fori_loop(0, PEER_TOPK, body, 0)


def _peer_retrieve_kernel(x_ref, gain_ref, scale_ref, shift_ref, wq_ref, keys_ref,
                          h_ref, idx_out_ref, gate_out_ref,
                          s_ref, ids1_ref, sv_ref, si_ref, cand_ref, cid_ref, ts_ref, idx_ref, gate_ref):
    x = x_ref[0]
    y = x * lax.rsqrt(jnp.mean(x * x, axis=-1, keepdims=True) + NORM_EPS)
    h = (y * gain_ref[...]) * (1.0 + scale_ref[0]) + shift_ref[0]
    h_ref[0] = h
    q = jnp.dot(h.astype(jnp.bfloat16), wq_ref[...], preferred_element_type=jnp.float32)
    T = PEER_TOKENS
    K = PEER_TOPK
    ids1_ref[...] = lax.broadcasted_iota(jnp.int32, (PEER_N_KEYS, T), 0)
    for hd in range(PEER_HEADS):
        for p in range(2):
            hp = hd * 2 + p
            qs = q[:, hp * PEER_HALF:(hp + 1) * PEER_HALF].astype(jnp.bfloat16)
            s_ref[...] = lax.dot_general(keys_ref[hp], qs, (((1,), (1,)), ((), ())),
                                         preferred_element_type=jnp.float32)
            _extract_topk(s_ref, ids1_ref, sv_ref, si_ref, p * K)
        for i in range(K):
            cand_ref[i * K:(i + 1) * K, :] = sv_ref[i:i + 1, :] + sv_ref[K:2 * K, :]
            cid_ref[i * K:(i + 1) * K, :] = si_ref[i:i + 1, :] * PEER_N_KEYS + si_ref[K:2 * K, :]
        _extract_topk(cand_ref, cid_ref, ts_ref, idx_ref, hd * K)
        ts = ts_ref[hd * K:(hd + 1) * K, :]
        e = jnp.exp(ts - jnp.max(ts, axis=0, keepdims=True))
        gate_ref[hd * K:(hd + 1) * K, :] = e / jnp.sum(e, axis=0, keepdims=True)
    idx_out_ref[...] = idx_ref[...].T
    gate_out_ref[...] = gate_ref[...].T


def peer_retrieve(x, gain, scale, shift, w_q, sub_keys):
    B, L, D = x.shape
    T = PEER_TOKENS
    nt = L // T
    keys = sub_keys.reshape(PEER_HEADS * 2, PEER_N_KEYS, PEER_HALF).astype(jnp.bfloat16)
    return pl.pallas_call(
        _peer_retrieve_kernel,
        grid=(B, nt),
        in_specs=[
            pl.BlockSpec((1, T, D), lambda b, i: (b, i, 0)),
            pl.BlockSpec((1, D), lambda b, i: (0, 0)),
            pl.BlockSpec((1, 1, D), lambda b, i: (b, 0, 0)),
            pl.BlockSpec((1, 1, D), lambda b, i: (b, 0, 0)),
            pl.BlockSpec((D, PEER_HEADS * 2 * PEER_HALF), lambda b, i: (0, 0)),
            pl.BlockSpec((PEER_HEADS * 2, PEER_N_KEYS, PEER_HALF), lambda b, i: (0, 0, 0)),
        ],
        out_specs=[
            pl.BlockSpec((1, T, D), lambda b, i: (b, i, 0)),
            pl.BlockSpec((T, PEER_SLOTS), lambda b, i: (b * nt + i, 0)),
            pl.BlockSpec((T, PEER_SLOTS), lambda b, i: (b * nt + i, 0)),
        ],
        out_shape=[
            jax.ShapeDtypeStruct((B, L, D), jnp.float32),
            jax.ShapeDtypeStruct((B * L, PEER_SLOTS), jnp.int32),
            jax.ShapeDtypeStruct((B * L, PEER_SLOTS), jnp.float32),
        ],
        scratch_shapes=[
            pltpu.VMEM((PEER_N_KEYS, T), jnp.float32),
            pltpu.VMEM((PEER_N_KEYS, T), jnp.int32),
            pltpu.VMEM((2 * PEER_TOPK, T), jnp.float32),
            pltpu.VMEM((2 * PEER_TOPK, T), jnp.int32),
            pltpu.VMEM((PEER_TOPK * PEER_TOPK, T), jnp.float32),
            pltpu.VMEM((PEER_TOPK * PEER_TOPK, T), jnp.int32),
            pltpu.VMEM((PEER_SLOTS, T), jnp.float32),
            pltpu.VMEM((PEER_SLOTS, T), jnp.int32),
            pltpu.VMEM((PEER_SLOTS, T), jnp.float32),
        ],
        compiler_params=pltpu.CompilerParams(dimension_semantics=("parallel", "parallel"),
                                             vmem_limit_bytes=VMEM_LIMIT_BYTES),
        name="peer_retrieve",
    )(x, gain.reshape(1, D), scale, shift, w_q.astype(jnp.bfloat16), keys)


PEER_ACT_ROWS = 1024


def _peer_act_kernel(dots_ref, gate_ref, w_ref):
    a = dots_ref[...]
    w_ref[...] = gate_ref[...] * (0.5 * a * (1.0 + lax.erf(a * (2.0 ** -0.5))))


def peer_act(dots, gate):
    N = dots.shape[0]
    T = min(PEER_ACT_ROWS, N)
    spec = pl.BlockSpec((T, PEER_SLOTS), lambda i: (i, 0))
    return pl.pallas_call(
        _peer_act_kernel,
        grid=(N // T,),
        in_specs=[spec, spec],
        out_specs=spec,
        out_shape=jax.ShapeDtypeStruct((N, PEER_SLOTS), jnp.float32),
        compiler_params=pltpu.CompilerParams(dimension_semantics=("parallel",)),
        name="peer_act",
    )(dots, gate)


def peer_ffn(x, gain, scale, shift, w_q, sub_keys, exp_u, exp_v):
    B, L, D = x.shape
    N = B * L
    h, e_idx, gate = peer_retrieve(x, gain, scale, shift, w_q, sub_keys)
    dots = _sc_peer(exp_u, e_idx, h.reshape(N, D), "dot")
    return _sc_peer(exp_v, e_idx, peer_act(dots, gate), "wsum").reshape(B, L, D)


def _mix_and_retrieve(li, x, c, ctx, c_ctx, mod_w, mod_b, mix_norm, w_in, w_out, rw_conv, rw_decay_up, rw_decay0, rw_a_up, rw_a0, rw_gate_up, rw_k_k, rw_k_a, rw_r_k, rw_gn_g, rw_gn_b, mla_q_norm, mla_w_uq, mla_kv_norm, mla_w_ukv, mla_q_gain, mla_k_gain, hy_conv, hy_w1, hy_b1, hy_freq1, hy_w2, hy_b2, hy_freq2, hy_w3, hy_b3, hy_bias, ffn_norm, peer_wq, peer_keys, peer_u, peer_v):
    B, L, D = x.shape
    s_rw, s_mla = RW_PROJ, RW_PROJ + MLA_PROJ
    need_ctx = li < DEPTH - 1
    mod_l = (jax.nn.silu(c) @ mod_w[li] + mod_b[li])[:, None, :]
    mod_c = (jax.nn.silu(c_ctx) @ mod_w[li] + mod_b[li])[None, None, :]
    shm_l, scm_l, gm_l, shf_l, scf_l, gf_l = jnp.split(mod_l, N_MOD, axis=-1)
    shm_c, scm_c, gm_c, shf_c, scf_c, gf_c = jnp.split(mod_c, N_MOD, axis=-1)

    p_l = norm_mod_proj(x, mix_norm[li], scm_l, shm_l, w_in[li], 512)
    p_c = norm_mod_proj(ctx, mix_norm[li], jnp.broadcast_to(scm_c, (B, 1, D)),
                        jnp.broadcast_to(shm_c, (B, 1, D)), w_in[li], 256)
    rw_l, rw_c = rwkv7_mixer(p_l[..., :s_rw], p_c[..., :s_rw], rw_conv[li], rw_decay_up[li], rw_decay0[li],
                             rw_a_up[li], rw_a0[li], rw_gate_up[li], rw_k_k[li], rw_k_a[li], rw_r_k[li],
                             rw_gn_g[li], rw_gn_b[li], need_ctx)
    ml_l, ml_c = mla_mixer(p_l[..., s_rw:s_mla], p_c[..., s_rw:s_mla], mla_q_norm[li], mla_w_uq[li],
                           mla_kv_norm[li], mla_w_ukv[li], mla_q_gain[li], mla_k_gain[li], need_ctx)
    hy_prm = (hy_conv[li], hy_w1[li], hy_b1[li], hy_freq1[li], hy_w2[li], hy_b2[li], hy_freq2[li],
              hy_w3[li], hy_b3[li], hy_bias[li])
    hy_l = hyena_mixer(p_l[..., s_mla:], *hy_prm)
    x = x + gm_l * (jnp.concatenate([rw_l, ml_l, hy_l], axis=-1) @ w_out[li])
    if need_ctx:
        hy_c = hyena_mixer(p_c[..., s_mla:], *hy_prm)
        ctx = ctx + gm_c * (jnp.concatenate([rw_c, ml_c, hy_c], axis=-1) @ w_out[li])
        ctx = ctx + gf_c * peer_ffn(ctx, ffn_norm[li], jnp.broadcast_to(scf_c, (B, 1, D)),
                                    jnp.broadcast_to(shf_c, (B, 1, D)),
                                    peer_wq[li], peer_keys[li], peer_u[li], peer_v[li])
    h, e_idx, gate = peer_retrieve(x, ffn_norm[li], scf_l, shf_l, peer_wq[li], peer_keys[li])
    return x, ctx, gf_l, h.reshape(B * L, D), e_idx, gate


BATCH_GROUPS = 4


def kernel(x, c, ctx, c_ctx, mod_w, mod_b, mix_norm, w_in, w_out, rw_conv, rw_decay_up, rw_decay0, rw_a_up, rw_a0, rw_gate_up, rw_k_k, rw_k_a, rw_r_k, rw_gn_g, rw_gn_b, mla_q_norm, mla_w_uq, mla_kv_norm, mla_w_ukv, mla_q_gain, mla_k_gain, hy_conv, hy_w1, hy_b1, hy_freq1, hy_w2, hy_b2, hy_freq2, hy_w3, hy_b3, hy_bias, ffn_norm, peer_wq, peer_keys, peer_u, peer_v):
    params = (mod_w, mod_b, mix_norm, w_in, w_out, rw_conv, rw_decay_up, rw_decay0, rw_a_up, rw_a0, rw_gate_up,
              rw_k_k, rw_k_a, rw_r_k, rw_gn_g, rw_gn_b, mla_q_norm, mla_w_uq, mla_kv_norm, mla_w_ukv, mla_q_gain,
              mla_k_gain, hy_conv, hy_w1, hy_b1, hy_freq1, hy_w2, hy_b2, hy_freq2, hy_w3, hy_b3, hy_bias,
              ffn_norm, peer_wq, peer_keys)
    peer_u = [pack_expert_table(peer_u[li]) for li in range(DEPTH)]
    peer_v = [pack_expert_table(peer_v[li]) for li in range(DEPTH)]
    params = params + (peer_u, peer_v)
    G = BATCH_GROUPS
    bg = x.shape[0] // G
    L, D = x.shape[1:]
    xs = [x[g * bg:(g + 1) * bg] for g in range(G)]
    cs = [c[g * bg:(g + 1) * bg] for g in range(G)]
    ctxs = [ctx[g * bg:(g + 1) * bg] for g in range(G)]
    stages = [(li, g) for li in range(DEPTH) for g in range(G)]
    pending = {}
    token = None
    for k in range(len(stages) + 1):
        if k < len(stages):
            li, g = stages[k]
            ins = (xs[g], ctxs[g])
            if token is not None:
                token, ins = lax.optimization_barrier((token, ins))
            xm, ctxs[g], gf, h, e_idx, gate = _mix_and_retrieve(li, ins[0], cs[g], ins[1], c_ctx, *params)
            dots = _sc_peer(peer_u[li], e_idx, h, "dot")
            pending[k] = (xm, gf, e_idx, gate, dots)
            token = gate
        if k >= 1:
            li, g = stages[k - 1]
            xm, gf, e_idx, gate, dots = pending.pop(k - 1)
            token, (dots, gate) = lax.optimization_barrier((token, (dots, gate)))
            w = peer_act(dots, gate)
            xs[g] = xm + gf * _sc_peer(peer_v[li], e_idx, w, "wsum").reshape(bg, L, D)
            token = w
    return jnp.concatenate(xs, axis=0)
```

```python
import functools
import math

import jax
import jax.numpy as jnp
import numpy as np
from jax import lax
from jax.experimental import pallas as pl
from jax.experimental.pallas import tpu as pltpu
from jax.experimental.pallas import tpu_sc as plsc

D_MODEL = 1024
DEPTH = 2
GRID_W = 64
N_MOD = 6
NORM_EPS = 1e-6

RW_HEADS = 6
RW_HEAD_DIM = 64
RW_WIDTH = RW_HEADS * RW_HEAD_DIM
RW_DECAY_RANK = 64
RW_A_RANK = 64
RW_GATE_RANK = 128
RW_DECAY_SCALE = 0.6065306597
RW_GN_EPS = 64e-5
L2_EPS = 1e-12

MLA_HEADS = 6
MLA_Q_RANK = 256
MLA_KV_RANK = 128
MLA_NOPE_DIM = 64
MLA_ROPE_DIM = 32
MLA_V_DIM = 64
MLA_QK_DIM = MLA_NOPE_DIM + MLA_ROPE_DIM
MLA_WIDTH = MLA_HEADS * MLA_V_DIM
AXIS_ROPE_DIM = MLA_ROPE_DIM // 2
ROPE_THETA = 10000.0

HY_WIDTH = 256
HY_ORDER = 2
HY_POS_BANDS = 16
HY_SHORT_DECAY_PCT = 0.3
HY_LONG_DECAY_PCT = 1.5
HY_DECAY_TARGET = 1e-2

PEER_HEADS = 8
PEER_N_KEYS = 128
PEER_TOPK = 16
PEER_QUERY_DIM = 256
PEER_HALF = PEER_QUERY_DIM // 2

RW_PROJ = 3 * RW_WIDTH + RW_DECAY_RANK + RW_A_RANK + RW_GATE_RANK
MLA_PROJ = MLA_Q_RANK + MLA_KV_RANK + MLA_ROPE_DIM
HY_PROJ = (HY_ORDER + 1) * HY_WIDTH
IN_PROJ = RW_PROJ + MLA_PROJ + HY_PROJ
MIX_WIDTH = RW_WIDTH + MLA_WIDTH + HY_WIDTH

VMEM_LIMIT_BYTES = 48 * 1024 * 1024


def _norm_mod_proj_kernel(x_ref, gain_ref, scale_ref, shift_ref, w_ref, o_ref):
    x = x_ref[0]
    y = x * lax.rsqrt(jnp.mean(x * x, axis=-1, keepdims=True) + NORM_EPS)
    y = y * gain_ref[...]
    y = y * (1.0 + scale_ref[0]) + shift_ref[0]
    o_ref[0] = jnp.dot(y.astype(jnp.bfloat16), w_ref[...], preferred_element_type=jnp.float32)


def norm_mod_proj(x, gain, scale, shift, w, block_rows):
    B, L, D = x.shape
    N = w.shape[1]
    return pl.pallas_call(
        _norm_mod_proj_kernel,
        grid=(B, L // block_rows),
        in_specs=[
            pl.BlockSpec((1, block_rows, D), lambda b, i: (b, i, 0)),
            pl.BlockSpec((1, D), lambda b, i: (0, 0)),
            pl.BlockSpec((1, 1, D), lambda b, i: (b, 0, 0)),
            pl.BlockSpec((1, 1, D), lambda b, i: (b, 0, 0)),
            pl.BlockSpec((D, N), lambda b, i: (0, 0)),
        ],
        out_specs=pl.BlockSpec((1, block_rows, N), lambda b, i: (b, i, 0)),
        out_shape=jax.ShapeDtypeStruct((B, L, N), jnp.float32),
        compiler_params=pltpu.CompilerParams(
            dimension_semantics=("parallel", "parallel"), vmem_limit_bytes=VMEM_LIMIT_BYTES),
        name="norm_mod_proj",
    )(x, gain.reshape(1, D), scale, shift, w.astype(jnp.bfloat16))


RW_CHUNK = 64


def _rwkv_chunk_kernel(r_ref, kk_ref, v_ref, lw_ref, akk_ref, kr_ref, y_ref, h_ref):
    d = pl.program_id(0)
    n = pl.program_id(2)

    @pl.when(n == 0)
    def _():
        h_ref[...] = jnp.zeros_like(h_ref)

    C = RW_CHUNK
    row = lax.broadcasted_iota(jnp.int32, (C, C), 0)
    col = lax.broadcasted_iota(jnp.int32, (C, C), 1)
    lag = (row - col) * (1 - 2 * d)
    before = lag > 0
    upto = lag >= 0
    tri = upto.astype(jnp.float32)
    eye = (row == col).astype(jnp.float32)
    bf = jnp.bfloat16
    f32 = jnp.float32

    def mm(a, b):
        return jnp.dot(a.astype(bf), b.astype(bf), preferred_element_type=f32)

    def mm_nt(a, b):
        return lax.dot_general(a.astype(bf), b.astype(bf), (((1,), (1,)), ((), ())), preferred_element_type=f32)

    def mm_tn(a, b):
        return lax.dot_general(a.astype(bf), b.astype(bf), (((0,), (0,)), ((), ())), preferred_element_type=f32)

    hs = range(RW_HEADS)
    HD = RW_HEAD_DIM
    heads = lambda t: [t[:, h * HD:(h + 1) * HD] for h in hs]
    r = heads(r_ref[0])
    kk = heads(kk_ref[0])
    v = heads(v_ref[0])
    lw = heads(lw_ref[0, 0])
    akk = heads(akk_ref[0, 0])
    kr = heads(kr_ref[0, 0])
    G = [jnp.dot(tri, lw[h], preferred_element_type=f32, precision=lax.Precision.HIGHEST) for h in hs]
    gtot = [jnp.sum(lw[h], axis=0, keepdims=True) for h in hs]
    Einv = [jnp.exp(-G[h]) for h in hs]
    At = [-kk[h] * jnp.exp(G[h] - lw[h]) for h in hs]
    Rt = [r[h] * jnp.exp(G[h]) for h in hs]
    Bt = [akk[h] * Einv[h] for h in hs]
    Kt = [kr[h] * Einv[h] for h in hs]
    X = [mm_nt(jnp.concatenate([At[h], Rt[h]], axis=0), jnp.concatenate([Bt[h], Kt[h]], axis=0)) for h in hs]
    M_ab = [jnp.where(before, X[h][:C, :C], 0.0) for h in hs]
    M_ak = [jnp.where(before, X[h][:C, C:], 0.0) for h in hs]
    A_rb = [jnp.where(upto, X[h][C:, :C], 0.0) for h in hs]
    A_rk = [jnp.where(upto, X[h][C:, C:], 0.0) for h in hs]
    MV = [mm(M_ak[h], v[h]) for h in hs]
    Mp = M_ab
    T = [eye + Mp[h] for h in hs]
    for _ in range(5):
        Mp = [jnp.dot(Mp[h], Mp[h], preferred_element_type=f32) for h in hs]
        T = [T[h] + jnp.dot(T[h], Mp[h], preferred_element_type=f32) for h in hs]
    WU = [jnp.dot(T[h], jnp.concatenate([At[h], MV[h]], axis=1), preferred_element_type=f32) for h in hs]
    H0 = [h_ref[h] for h in hs]
    Ehat = [jnp.exp(gtot[h] - G[h]) for h in hs]
    Om = [Rt[h] + mm(A_rb[h], WU[h][:, :HD]) for h in hs]
    Y0 = [mm(A_rb[h], WU[h][:, HD:]) + mm(A_rk[h], v[h]) for h in hs]
    BW = [mm_tn(akk[h] * Ehat[h], WU[h]) for h in hs]
    KV = [mm_tn(kr[h] * Ehat[h], v[h]) for h in hs]
    y_ref[0, 0] = jnp.concatenate([jnp.dot(Om[h], H0[h], preferred_element_type=f32) + Y0[h] for h in hs], axis=1)
    for h in hs:
        P = eye * jnp.exp(gtot[h]) + BW[h][:, :HD]
        h_ref[h] = jnp.dot(P, H0[h], preferred_element_type=f32) + BW[h][:, HD:] + KV[h]


def rwkv_chunked(r, kk, v, lw, akk, kr, n_ctx):
    B, T, W = r.shape
    H = W // RW_HEAD_DIM
    nc = n_ctx // RW_CHUNK
    nt = T // RW_CHUNK

    def chunk_of(d, n):
        bwd = jnp.where(n < nc, nc - 1 - n, nt - 1 - (n - nc))
        return jnp.where(d == 0, n, bwd)

    spec1 = pl.BlockSpec((1, RW_CHUNK, W), lambda d, b, n: (b, chunk_of(d, n), 0))
    spec2 = pl.BlockSpec((1, 1, RW_CHUNK, W), lambda d, b, n: (d, b, chunk_of(d, n), 0))
    return pl.pallas_call(
        _rwkv_chunk_kernel,
        grid=(2, B, nt),
        in_specs=[spec1, spec1, spec1, spec2, spec2, spec2],
        out_specs=spec2,
        out_shape=jax.ShapeDtypeStruct((2, B, T, W), jnp.float32),
        scratch_shapes=[pltpu.VMEM((H, RW_HEAD_DIM, RW_HEAD_DIM), jnp.float32)],
        compiler_params=pltpu.CompilerParams(dimension_semantics=("parallel", "parallel", "arbitrary")),
        name="rwkv_chunked",
    )(r, kk, v, lw, akk, kr)


def short_conv(x, w):
    xp = jnp.pad(x, ((0, 0), (1, 1), (0, 0)))
    return xp[:, :-2] * w[0] + xp[:, 1:-1] * w[1] + xp[:, 2:] * w[2]


LANE = 128
RW_PREP_ROWS = 256
MLA_PAD_WIDTH = MLA_HEADS * LANE
MLA_PREP_ROWS = 256
ATTN_Q_ROWS = 512


def _split_dot(x, m):
    hi = x.astype(jnp.bfloat16)
    lo = (x - hi.astype(jnp.float32)).astype(jnp.bfloat16)
    return (jnp.dot(hi, m, preferred_element_type=jnp.float32) + jnp.dot(lo, m, preferred_element_type=jnp.float32))


def _rwkv_prep_kernel(z_ref, wda_ref, d0_ref, a0_ref, gup_ref, kk_ref_w, ka_ref, rk_ref, hsum_ref,
                      r_ref, kk_ref, v_ref, lw_ref, akk_ref, kr_ref, g_ref, bonus_ref):
    W = RW_WIDTH
    bf = jnp.bfloat16
    z = z_ref[0]
    r, k, v = z[:, :W], z[:, W:2 * W], z[:, 2 * W:3 * W]
    da = z[:, 3 * W:3 * W + LANE]
    lane = lax.broadcasted_iota(jnp.int32, da.shape, 1)
    da = jnp.where(lane < RW_DECAY_RANK, jnp.tanh(da), da)
    up = jnp.dot(da.astype(bf), wda_ref[...], preferred_element_type=jnp.float32)
    g_lo = z[:, 3 * W + LANE:]
    g_ref[0] = jnp.dot(jax.nn.sigmoid(g_lo).astype(bf), gup_ref[...], preferred_element_type=jnp.float32)
    hsum = hsum_ref[...]
    kk = k * kk_ref_w[...]
    kk = kk * lax.rsqrt(_split_dot(kk * kk, hsum) + L2_EPS)
    r_ref[0] = r
    v_ref[0] = v
    kk_ref[0] = kk
    bonus_ref[0] = _split_dot(r * k * rk_ref[...], hsum) * v
    for d in range(2):
        lw_ref[d, 0] = -RW_DECAY_SCALE * jax.nn.sigmoid(d0_ref[d:d + 1, :] + up[:, d * W:(d + 1) * W])
        a = jax.nn.sigmoid(a0_ref[d:d + 1, :] + up[:, (2 + d) * W:(3 + d) * W])
        akk_ref[d, 0] = kk * a
        kr_ref[d, 0] = k * (1.0 + (a - 1.0) * ka_ref[...])


def rwkv_prep(z, decay_up, decay0, a_up, a0, gate_up, k_k, k_a, r_k):
    B, L, _ = z.shape
    W = RW_WIDTH
    T = min(RW_PREP_ROWS, L)
    zero = jnp.zeros((RW_DECAY_RANK, 2 * W), jnp.float32)
    wda = jnp.concatenate([
        jnp.concatenate([decay_up[0], decay_up[1], zero], axis=1),
        jnp.concatenate([zero, a_up[0], a_up[1]], axis=1)], axis=0).astype(jnp.bfloat16)
    head = jnp.arange(W) // RW_HEAD_DIM
    hsum = (head[:, None] == head[None, :]).astype(jnp.bfloat16)
    row = lambda a: a.reshape(1, W)
    const = lambda a: pl.BlockSpec(a.shape, lambda b, i: (0,) * a.ndim)
    tok = pl.BlockSpec((1, T, W), lambda b, i: (b, i, 0))
    tok2 = pl.BlockSpec((2, 1, T, W), lambda b, i: (0, b, i, 0))
    f1 = jax.ShapeDtypeStruct((B, L, W), jnp.float32)
    f2 = jax.ShapeDtypeStruct((2, B, L, W), jnp.float32)
    args = (z, wda, decay0, a0, gate_up.astype(jnp.bfloat16), row(k_k), row(k_a), row(r_k), hsum)
    return pl.pallas_call(
        _rwkv_prep_kernel,
        grid=(B, L // T),
        in_specs=[pl.BlockSpec((1, T, RW_PROJ), lambda b, i: (b, i, 0))] + [const(a) for a in args[1:]],
        out_specs=[tok, tok, tok, tok2, tok2, tok2, tok, tok],
        out_shape=[f1, f1, f1, f2, f2, f2, f1, f1],
        compiler_params=pltpu.CompilerParams(dimension_semantics=("parallel", "parallel"),
                                             vmem_limit_bytes=VMEM_LIMIT_BYTES),
        name="rwkv_prep",
    )(*args)


def _rwkv_readout_kernel(y_ref, g_ref, bonus_ref, gng_ref, gnb_ref, hsum_ref, o_ref):
    y = y_ref[0, 0] + y_ref[1, 0]
    hsum = hsum_ref[...]
    mu = _split_dot(y, hsum) * (1.0 / RW_HEAD_DIM)
    d = y - mu
    var = _split_dot(d * d, hsum) * (1.0 / RW_HEAD_DIM)
    yn = d * lax.rsqrt(var + RW_GN_EPS) * gng_ref[...] + gnb_ref[...]
    o_ref[0] = (yn + bonus_ref[0]) * g_ref[0]


def rwkv_readout(y, g, bonus, gn_g, gn_b, t0):
    B, L, W = g.shape
    T = min(RW_PREP_ROWS, L)
    off = t0 // T
    head = jnp.arange(W) // RW_HEAD_DIM
    hsum = (head[:, None] == head[None, :]).astype(jnp.bfloat16)
    tok = pl.BlockSpec((1, T, W), lambda b, i: (b, i, 0))
    const = lambda a: pl.BlockSpec(a.shape, lambda b, i: (0,) * a.ndim)
    gg, gb = gn_g.reshape(1, W), gn_b.reshape(1, W)
    return pl.pallas_call(
        _rwkv_readout_kernel,
        grid=(B, L // T),
        in_specs=[pl.BlockSpec((2, 1, T, W), lambda b, i: (0, b, i + off, 0)), tok, tok, const(gg), const(gb), const(hsum)],
        out_specs=tok,
        out_shape=jax.ShapeDtypeStruct((B, L, W), jnp.float32),
        compiler_params=pltpu.CompilerParams(dimension_semantics=("parallel", "parallel")),
        name="rwkv_readout",
    )(y, g, bonus, gg, gb, hsum)


def rwkv7_mixer(p_lat, p_ctx, conv_w, decay_up, decay0, a_up, a0, gate_up, k_k, k_a, r_k, gn_g, gn_b, need_ctx):
    prm = (decay_up, decay0, a_up, a0, gate_up, k_k, k_a, r_k)
    lat = rwkv_prep(short_conv(p_lat, conv_w), *prm)
    ctx = rwkv_prep(short_conv(p_ctx, conv_w), *prm)
    n_ctx = p_ctx.shape[1]
    seq = lambda i: jnp.concatenate([ctx[i], lat[i]], axis=-2)
    y = rwkv_chunked(seq(0), seq(1), seq(2), seq(3), seq(4), seq(5), n_ctx)
    out_l = rwkv_readout(y, lat[6], lat[7], gn_g, gn_b, n_ctx)
    out_c = rwkv_readout(y, ctx[6], ctx[7], gn_g, gn_b, 0) if need_ctx else None
    return out_l, out_c


def _rope_tables(L, use_rope):
    lane = np.arange(LANE)
    in_rope = (lane >= MLA_NOPE_DIM) & (lane < MLA_QK_DIM)
    j = lane - MLA_NOPE_DIM
    axis = j // AXIS_ROPE_DIM
    half = AXIS_ROPE_DIM // 2
    f = j % half
    first = (j % AXIS_ROPE_DIM) < half
    inv = ROPE_THETA ** (-jnp.arange(0, AXIS_ROPE_DIM, 2, dtype=jnp.float32) / AXIS_ROPE_DIM)
    t = jnp.arange(L)
    pos = jnp.stack([t // GRID_W, t % GRID_W], axis=-1).astype(jnp.float32)
    ang = pos[:, np.clip(axis, 0, 1)] * inv[np.clip(f, 0, half - 1)][None, :]
    rope_on = jnp.asarray(in_rope)[None, :] & use_rope
    cos = jnp.where(rope_on, jnp.cos(ang), 1.0)
    sin = jnp.where(rope_on, jnp.sin(ang) * jnp.where(jnp.asarray(first), -1.0, 1.0)[None, :], 0.0)
    return jnp.tile(cos, (1, MLA_HEADS)), jnp.tile(sin, (1, MLA_HEADS))


def _mla_prep_kernel(p_ref, qn_ref, wq_ref, kvn_ref, wk_ref, wv_ref, place_ref, qg_ref, kg_ref, hsum_ref, cos_ref, sin_ref,
                     q_ref, k_ref, v_ref):
    bf = jnp.bfloat16
    p = p_ref[0]
    c_q = p[:, :MLA_Q_RANK]
    c_kv = p[:, MLA_Q_RANK:MLA_Q_RANK + MLA_KV_RANK]
    tail = p[:, MLA_Q_RANK + MLA_KV_RANK:]
    cqn = c_q * lax.rsqrt(jnp.mean(c_q * c_q, axis=-1, keepdims=True) + NORM_EPS) * qn_ref[...]
    ckn = c_kv * lax.rsqrt(jnp.mean(c_kv * c_kv, axis=-1, keepdims=True) + NORM_EPS) * kvn_ref[...]
    q = jnp.dot(cqn.astype(bf), wq_ref[...], preferred_element_type=jnp.float32)
    k = jnp.dot(ckn.astype(bf), wk_ref[...], preferred_element_type=jnp.float32) + _split_dot(tail, place_ref[...])
    v_ref[0] = jnp.dot(ckn.astype(bf), wv_ref[...], preferred_element_type=jnp.float32).astype(bf)
    hsum = hsum_ref[...]
    cos, sin = cos_ref[...], sin_ref[...]
    lane = lax.broadcasted_iota(jnp.int32, q.shape, 1)
    first = ((lane - MLA_NOPE_DIM) % AXIS_ROPE_DIM) < (AXIS_ROPE_DIM // 2)
    half = AXIS_ROPE_DIM // 2

    def finish(x, gain):
        x = x * lax.rsqrt(_split_dot(x * x, hsum) * (1.0 / MLA_QK_DIM) + NORM_EPS) * gain
        partner = jnp.where(first, pltpu.roll(x, MLA_PAD_WIDTH - half, 1), pltpu.roll(x, half, 1))
        return x * cos + partner * sin

    q_ref[0] = (finish(q, qg_ref[...]) * (MLA_QK_DIM ** -0.5)).astype(bf)
    k_ref[0] = finish(k, kg_ref[...]).astype(bf)


def mla_prep(p, use_rope, q_norm, w_uq, kv_norm, w_ukv, q_gain, k_gain):
    B, L, _ = p.shape
    T = min(MLA_PREP_ROWS, L)
    H = MLA_HEADS
    pad_cols = lambda w, d: jnp.pad(w.reshape(w.shape[0], H, d), ((0, 0), (0, 0), (0, LANE - d))).reshape(w.shape[0], H * LANE)
    wq = pad_cols(w_uq, MLA_QK_DIM).astype(jnp.bfloat16)
    ukv = w_ukv.reshape(MLA_KV_RANK, H, MLA_NOPE_DIM + MLA_V_DIM)
    wk = pad_cols(ukv[:, :, :MLA_NOPE_DIM].reshape(MLA_KV_RANK, H * MLA_NOPE_DIM), MLA_NOPE_DIM).astype(jnp.bfloat16)
    wv = ukv[:, :, MLA_NOPE_DIM:].reshape(MLA_KV_RANK, H * MLA_V_DIM).astype(jnp.bfloat16)
    lane = np.arange(H * LANE)
    place = jnp.asarray(((lane[None, :] % LANE) - MLA_NOPE_DIM == np.arange(MLA_ROPE_DIM)[:, None]), jnp.bfloat16)
    hsum = jnp.asarray((lane[:, None] // LANE) == (lane[None, :] // LANE), jnp.bfloat16)
    pad_gain = lambda g: jnp.tile(jnp.pad(g, (0, LANE - MLA_QK_DIM)), H).reshape(1, H * LANE)
    cos, sin = _rope_tables(L, use_rope)
    const = lambda a: pl.BlockSpec(a.shape, lambda b, i: (0,) * a.ndim)
    args = (p, q_norm.reshape(1, -1), wq, kv_norm.reshape(1, -1), wk, wv, place, pad_gain(q_gain), pad_gain(k_gain), hsum)
    pos = pl.BlockSpec((T, H * LANE), lambda b, i: (i, 0))
    return pl.pallas_call(
        _mla_prep_kernel,
        grid=(B, L // T),
        in_specs=[pl.BlockSpec((1, T, MLA_PROJ), lambda b, i: (b, i, 0))] + [const(a) for a in args[1:]] + [pos, pos],
        out_specs=[pl.BlockSpec((1, T, H * LANE), lambda b, i: (b, i, 0)), pl.BlockSpec((1, T, H * LANE), lambda b, i: (b, i, 0)),
                   pl.BlockSpec((1, T, MLA_WIDTH), lambda b, i: (b, i, 0))],
        out_shape=[jax.ShapeDtypeStruct((B, L, H * LANE), jnp.bfloat16), jax.ShapeDtypeStruct((B, L, H * LANE), jnp.bfloat16),
                   jax.ShapeDtypeStruct((B, L, MLA_WIDTH), jnp.bfloat16)],
        compiler_params=pltpu.CompilerParams(dimension_semantics=("parallel", "parallel"),
                                             vmem_limit_bytes=VMEM_LIMIT_BYTES),
        name="mla_prep",
    )(*args, cos, sin)


def _attn_kernel(q_ref, k_ref, v_ref, o_ref):
    lane = lax.broadcasted_iota(jnp.int32, (q_ref.shape[1], LANE), 1)
    for pair in range(MLA_HEADS // 2):
        v_pair = v_ref[0, :, pair * LANE:(pair + 1) * LANE]
        outs = []
        for h in (2 * pair, 2 * pair + 1):
            q = q_ref[0, :, h * LANE:(h + 1) * LANE]
            k = k_ref[0, :, h * LANE:(h + 1) * LANE]
            s = lax.dot_general(q, k, (((1,), (1,)), ((), ())), preferred_element_type=jnp.float32)
            e = jnp.exp(s - jnp.max(s, axis=-1, keepdims=True))
            o = jnp.dot(e.astype(jnp.bfloat16), v_pair, preferred_element_type=jnp.float32)
            outs.append(o / jnp.sum(e, axis=-1, keepdims=True))
        o_ref[0, :, pair * LANE:(pair + 1) * LANE] = jnp.where(lane < MLA_V_DIM, outs[0], outs[1])


def attention(q, k, v):
    B, Lq, P = q.shape
    Lk = k.shape[1]
    tq = min(ATTN_Q_ROWS, Lq)
    return pl.pallas_call(
        _attn_kernel,
        grid=(B, Lq // tq),
        in_specs=[pl.BlockSpec((1, tq, P), lambda b, i: (b, i, 0)),
                  pl.BlockSpec((1, Lk, P), lambda b, i: (b, 0, 0)),
                  pl.BlockSpec((1, Lk, MLA_WIDTH), lambda b, i: (b, 0, 0))],
        out_specs=pl.BlockSpec((1, tq, MLA_WIDTH), lambda b, i: (b, i, 0)),
        out_shape=jax.ShapeDtypeStruct((B, Lq, MLA_WIDTH), jnp.float32),
        compiler_params=pltpu.CompilerParams(dimension_semantics=("parallel", "parallel"),
                                             vmem_limit_bytes=VMEM_LIMIT_BYTES),
        name="mla_attention",
    )(q, k, v)


def mla_mixer(p_lat, p_ctx, q_norm, w_uq, kv_norm, w_ukv, q_gain, k_gain, need_ctx):
    prm = (q_norm, w_uq, kv_norm, w_ukv, q_gain, k_gain)
    q_l, k_l, v_l = mla_prep(p_lat, True, *prm)
    q_c, k_c, v_c = mla_prep(p_ctx, False, *prm)
    y_l = attention(q_l, jnp.concatenate([k_l, k_c], axis=1), jnp.concatenate([v_l, v_c], axis=1))
    y_c = attention(q_c, k_c, v_c) if need_ctx else None
    return y_l, y_c


def hyena_filters(L, w1, b1, freq1, w2, b2, freq2, w3, b3):
    tn = jnp.arange(L, dtype=jnp.float32) / L
    bands = jnp.arange(1, HY_POS_BANDS + 1, dtype=jnp.float32)
    ang = 2.0 * math.pi * tn[:, None] * bands[None, :]
    z = jnp.concatenate([tn[:, None], jnp.cos(ang), jnp.sin(ang)], axis=-1)
    h = jnp.sin(freq1 * (z @ w1 + b1))
    h = jnp.sin(freq2 * (h @ w2 + b2))
    h = (h @ w3 + b3).reshape(L, HY_ORDER, 2, HY_WIDTH)
    rates = jnp.abs(jnp.linspace(math.log(HY_DECAY_TARGET) / HY_LONG_DECAY_PCT,
                                 math.log(HY_DECAY_TARGET) / HY_SHORT_DECAY_PCT, HY_WIDTH))
    h = h * jnp.exp(-tn[:, None] * rates[None, :])[:, None, None, :]
    zero = jnp.zeros((1, HY_ORDER, HY_WIDTH), h.dtype)
    h_full = jnp.concatenate([h[:, :, 0], zero, h[:0:-1, :, 1]], axis=0)
    return h_full * lax.rsqrt(jnp.sum(jnp.square(h_full), axis=0, keepdims=True))


def fft_long_conv(u, h_full, bias):
    L = u.shape[1]
    uf = jnp.fft.rfft(u, n=2 * L, axis=1)
    hf = jnp.fft.rfft(h_full, n=2 * L, axis=0)
    y = jnp.fft.irfft(uf * hf[None], n=2 * L, axis=1)[:, :L]
    return y + u * bias


FFT_N1 = 64
FFT_N2 = 128
FFT_N = FFT_N1 * FFT_N2
HY_SEQS = 32


def _dft_tables(seqs):
    n1 = np.arange(FFT_N1)
    n2 = np.arange(FFT_N2)
    f64 = np.exp(-2j * np.pi * np.outer(n1, n1) / FFT_N1)
    f128 = np.exp(-2j * np.pi * np.outer(n2, n2) / FFT_N2)
    tw = np.exp(-2j * np.pi * np.outer(n1, n2) / FFT_N)
    half = FFT_N1 // 2
    fh = f64[:, :half]
    m1 = np.block([[fh.real, -fh.imag], [fh.imag, fh.real]])
    m1f = np.concatenate([f64.real, f64.imag], axis=0)
    m2 = np.block([[f128.real, f128.imag], [-f128.imag, f128.real]])
    m3 = np.block([[f128.real, -f128.imag], [f128.imag, f128.real]]) / FFT_N
    c = np.conj(f64)[:half, :]
    m4 = np.block([[c.real, -c.imag], [c.imag, c.real]])
    bf = lambda a: jnp.asarray(a, jnp.float32).astype(jnp.bfloat16)
    f32 = lambda a: jnp.asarray(a, jnp.float32)
    return dict(m1=bf(m1), m1f=bf(m1f), m2=bf(m2), m3=bf(m3), m4=bf(m4),
                twr_l=f32(np.tile(tw.real, (1, seqs))), twi_l=f32(np.tile(tw.imag, (1, seqs))),
                twr_s=f32(np.tile(tw.real, (seqs, 1))), twi_s=f32(np.tile(tw.imag, (seqs, 1))))


def _spectrum(cols, m1, twr_l, twi_l, m2, R):
    a = jnp.dot(m1, cols.astype(jnp.bfloat16), preferred_element_type=jnp.float32)
    ar, ai = a[:FFT_N1], a[FFT_N1:]
    pr = ar * twr_l - ai * twi_l
    pi = ar * twi_l + ai * twr_l
    lhs = jnp.concatenate(
        [jnp.concatenate([pr[:, r * FFT_N2:(r + 1) * FFT_N2], pi[:, r * FFT_N2:(r + 1) * FFT_N2]], axis=1)
         for r in range(R)], axis=0)
    return jnp.dot(lhs.astype(jnp.bfloat16), m2, preferred_element_type=jnp.float32)


def _filter_fft_kernel(h_ref, m1f_ref, twr_ref, twi_ref, m2_ref, o_ref):
    R = HY_SEQS
    cols = jnp.concatenate([h_ref[r] for r in range(R)], axis=1)
    x = _spectrum(cols, m1f_ref[...], twr_ref[...], twi_ref[...], m2_ref[...], R)
    o_ref[...] = x.reshape(R, FFT_N1, 2 * FFT_N2)


def _hyena_conv_kernel(y_ref, g_ref, hf_ref, bias_ref, m1_ref, twr_l_ref, twi_l_ref, m2_ref, m3_ref,
                       twr_s_ref, twi_s_ref, m4_ref, o_ref):
    R = HY_SEQS
    half = FFT_N1 // 2
    y = [y_ref[0], y_ref[1]]
    for o in range(HY_ORDER):
        top = jnp.concatenate([y[0][r] for r in range(R)], axis=1)
        bot = jnp.concatenate([y[1][r] for r in range(R)], axis=1)
        x = _spectrum(jnp.concatenate([top, bot], axis=0), m1_ref[...], twr_l_ref[...], twi_l_ref[...], m2_ref[...], R)
        hf = hf_ref[o].reshape(R * FFT_N1, 2 * FFT_N2)
        xr, xi = x[:, :FFT_N2], x[:, FFT_N2:]
        hr, hi = hf[:, :FFT_N2], hf[:, FFT_N2:]
        yc = jnp.concatenate([xr * hr - xi * hi, xr * hi + xi * hr], axis=1)
        b = jnp.dot(yc.astype(jnp.bfloat16), m3_ref[...], preferred_element_type=jnp.float32)
        br, bi = b[:, :FFT_N2], b[:, FFT_N2:]
        qr = br * twr_s_ref[...] + bi * twi_s_ref[...]
        qi = bi * twr_s_ref[...] - br * twi_s_ref[...]
        bc = jnp.concatenate(
            [jnp.concatenate([qr[r * FFT_N1:(r + 1) * FFT_N1], qi[r * FFT_N1:(r + 1) * FFT_N1]], axis=0)
             for r in range(R)], axis=1)
        yo = jnp.dot(m4_ref[...], bc.astype(jnp.bfloat16), preferred_element_type=jnp.float32)
        for p in range(2):
            conv = jnp.stack([yo[p * half:(p + 1) * half, r * FFT_N2:(r + 1) * FFT_N2] for r in range(R)], axis=0)
            y[p] = g_ref[o, p] * (conv + y[p] * bias_ref[o])
    o_ref[0] = y[0]
    o_ref[1] = y[1]


def hyena_long_conv(y_t, g_t, h_t, bias):
    B, C, L = y_t.shape
    assert 2 * L == FFT_N and B % 2 == 0 and C % HY_SEQS == 0
    R = HY_SEQS
    half = FFT_N1 // 2
    tb = _dft_tables(R)
    const = lambda a: pl.BlockSpec(a.shape, lambda *_: (0,) * a.ndim)
    hf = pl.pallas_call(
        _filter_fft_kernel,
        grid=(HY_ORDER * C // R,),
        in_specs=[pl.BlockSpec((R, FFT_N1, FFT_N2), lambda i: (i, 0, 0)),
                  const(tb['m1f']), const(tb['twr_l']), const(tb['twi_l']), const(tb['m2'])],
        out_specs=pl.BlockSpec((R, FFT_N1, 2 * FFT_N2), lambda i: (i, 0, 0)),
        out_shape=jax.ShapeDtypeStruct((HY_ORDER * C, FFT_N1, 2 * FFT_N2), jnp.float32),
        compiler_params=pltpu.CompilerParams(dimension_semantics=("parallel",), vmem_limit_bytes=VMEM_LIMIT_BYTES),
        name="hyena_filter_fft",
    )(h_t.reshape(HY_ORDER * C, FFT_N1, FFT_N2), tb['m1f'], tb['twr_l'], tb['twi_l'], tb['m2'])
    hf = hf.reshape(HY_ORDER, C, FFT_N1, 2 * FFT_N2)
    out = pl.pallas_call(
        _hyena_conv_kernel,
        grid=(B // 2, C // R),
        in_specs=[pl.BlockSpec((2, R, half, FFT_N2), lambda b, c: (b, c, 0, 0)),
                  pl.BlockSpec((HY_ORDER, 2, R, half, FFT_N2), lambda b, c: (0, b, c, 0, 0)),
                  pl.BlockSpec((HY_ORDER, R, FFT_N1, 2 * FFT_N2), lambda b, c: (0, c, 0, 0)),
                  pl.BlockSpec((HY_ORDER, R, 1, 1), lambda b, c: (0, c, 0, 0)),
                  const(tb['m1']), const(tb['twr_l']), const(tb['twi_l']), const(tb['m2']), const(tb['m3']),
                  const(tb['twr_s']), const(tb['twi_s']), const(tb['m4'])],
        out_specs=pl.BlockSpec((2, R, half, FFT_N2), lambda b, c: (b, c, 0, 0)),
        out_shape=jax.ShapeDtypeStruct((B, C, half, FFT_N2), jnp.float32),
        compiler_params=pltpu.CompilerParams(dimension_semantics=("parallel", "parallel"),
                                             vmem_limit_bytes=VMEM_LIMIT_BYTES),
        name="hyena_conv",
    )(y_t.reshape(B, C, half, FFT_N2), g_t.reshape(HY_ORDER, B, C, half, FFT_N2), hf,
      bias.reshape(HY_ORDER, C, 1, 1), tb['m1'], tb['twr_l'], tb['twi_l'], tb['m2'], tb['m3'],
      tb['twr_s'], tb['twi_s'], tb['m4'])
    return out.reshape(B, C, L)


def hyena_mixer(p, conv_w, w1, b1, freq1, w2, b2, freq2, w3, b3, bias):
    B, L = p.shape[:2]
    z = short_conv(p, conv_w)
    h_full = hyena_filters(L, w1, b1, freq1, w2, b2, freq2, w3, b3)
    if 2 * L == FFT_N:
        g_t = jnp.transpose(z[..., :HY_ORDER * HY_WIDTH].reshape(B, L, HY_ORDER, HY_WIDTH), (2, 0, 3, 1))
        y_t = jnp.swapaxes(z[..., HY_ORDER * HY_WIDTH:], 1, 2)
        if B % 2:
            y_t = jnp.pad(y_t, ((0, 1), (0, 0), (0, 0)))
            g_t = jnp.pad(g_t, ((0, 0), (0, 1), (0, 0), (0, 0)))
        y_t = hyena_long_conv(y_t, g_t, jnp.transpose(h_full, (1, 2, 0)), bias)[:B]
        return jnp.swapaxes(y_t, 1, 2)
    gates = (z[..., :HY_WIDTH], z[..., HY_WIDTH:2 * HY_WIDTH])
    y = z[..., 2 * HY_WIDTH:]
    for o in range(HY_ORDER):
        y = gates[o] * fft_long_conv(y, h_full[:, o], bias[o])
    return y


SC_CORES = 2
SC_SUBCORES = 16
SC_LANES = 16
SC_WORKERS = SC_CORES * SC_SUBCORES
PEER_SLOTS = PEER_HEADS * PEER_TOPK
PEER_GATHER_ROWS = 32
PEER_GATHERS = PEER_SLOTS // PEER_GATHER_ROWS
PEER_ACC_VREGS = 8
PEER_ROW_BUFFERS = 4
PEER_ROW_WORDS = D_MODEL // 2
HI_MASK = -65536


def pack_expert_table(t):
    b = lax.bitcast_convert_type(t.astype(jnp.bfloat16), jnp.uint16).astype(jnp.uint32)
    return lax.bitcast_convert_type(b[:, :PEER_ROW_WORDS] | (b[:, PEER_ROW_WORDS:] << 16), jnp.int32)


def _sc_peer(table, idx, aux, phase):
    N = idx.shape[0]
    tpw = N // SC_WORKERS
    assert tpw % 2 == 0 and N % SC_WORKERS == 0
    assert PEER_GATHERS % PEER_ROW_BUFFERS == 0
    mesh = plsc.VectorSubcoreMesh(core_axis_name="c", subcore_axis_name="s")
    aux_shape = (D_MODEL,) if phase == "dot" else (PEER_SLOTS,)
    out_tok = (PEER_SLOTS,) if phase == "dot" else (D_MODEL,)
    NBUF = PEER_ROW_BUFFERS
    AHEAD = NBUF - 1
    HW = PEER_ROW_WORDS

    @functools.partial(
        pl.kernel, mesh=mesh,
        out_type=jax.ShapeDtypeStruct((N,) + out_tok, jnp.float32),
        compiler_params=pltpu.CompilerParams(needs_layout_passes=False),
        scratch_types=[
            pltpu.VMEM((2, PEER_GATHERS, PEER_GATHER_ROWS), jnp.int32),
            pltpu.VMEM((2,) + aux_shape, jnp.float32),
            pltpu.VMEM((NBUF, PEER_GATHER_ROWS, HW), jnp.int32),
            pltpu.VMEM((2,) + out_tok, jnp.float32),
            pltpu.SemaphoreType.DMA((NBUF,)),
            pltpu.SemaphoreType.DMA((2,)),
            pltpu.SemaphoreType.DMA((2,)),
        ],
    )
    def k(table_hbm, idx_hbm, aux_hbm, out_hbm, idx_v, aux_v, rows_v, out_v, sem_r, sem_i, sem_o):
        wid = lax.axis_index("s") * SC_CORES + lax.axis_index("c")
        base = wid * tpw

        def gather(p, c, b):
            return pltpu.make_async_copy(table_hbm.at[idx_v.at[p, c]], rows_v.at[b], sem_r.at[b])

        def load_meta(t, p):
            return (pltpu.make_async_copy(idx_hbm.at[t], idx_v.at[p], sem_i.at[p]),
                    pltpu.make_async_copy(aux_hbm.at[t], aux_v.at[p], sem_i.at[p]))

        def store_out(t, p):
            return pltpu.make_async_copy(out_v.at[p], out_hbm.at[t], sem_o.at[p])

        def halves(word):
            return (plsc.bitcast(lax.shift_left(word, 16), jnp.float32), plsc.bitcast(word & HI_MASK, jnp.float32))

        def compute(p, c, b):
            if phase == "dot":
                lane = lax.iota(jnp.int32, SC_LANES)
                vec = jnp.zeros((SC_LANES,), jnp.float32)
                groups_per_vec = SC_LANES // PEER_ACC_VREGS
                for g in range(PEER_GATHER_ROWS // PEER_ACC_VREGS):
                    def body(cc, accs):
                        x_lo = aux_v[p, pl.ds(cc * SC_LANES, SC_LANES)]
                        x_hi = aux_v[p, pl.ds(HW + cc * SC_LANES, SC_LANES)]
                        out = []
                        for r in range(PEER_ACC_VREGS):
                            lo, hi = halves(rows_v[b, g * PEER_ACC_VREGS + r, pl.ds(cc * SC_LANES, SC_LANES)])
                            out.append(accs[r] + lo * x_lo + hi * x_hi)
                        return tuple(out)
                    accs = lax.fori_loop(0, HW // SC_LANES, body,
                                         tuple(jnp.zeros((SC_LANES,), jnp.float32) for _ in range(PEER_ACC_VREGS)))
                    for r in range(PEER_ACC_VREGS):
                        vec = jnp.where(lane == (g % groups_per_vec) * PEER_ACC_VREGS + r, jnp.sum(accs[r]), vec)
                    if g % groups_per_vec == groups_per_vec - 1:
                        out_v[p, pl.ds(c * PEER_GATHER_ROWS + (g // groups_per_vec) * SC_LANES, SC_LANES)] = vec
            else:
                words = PEER_ACC_VREGS // 2
                for db in range(HW // (words * SC_LANES)):
                    def body(kk, accs):
                        wv = plsc.load_gather(aux_v.at[p], [jnp.full((SC_LANES,), c * PEER_GATHER_ROWS + kk, jnp.int32)])
                        out = []
                        for j in range(words):
                            lo, hi = halves(rows_v[b, kk, pl.ds((db * words + j) * SC_LANES, SC_LANES)])
                            out += [accs[2 * j] + lo * wv, accs[2 * j + 1] + hi * wv]
                        return tuple(out)
                    if c == 0:
                        init = tuple(jnp.zeros((SC_LANES,), jnp.float32) for _ in range(2 * words))
                    else:
                        init = tuple(out_v[p, pl.ds(half * HW + (db * words + j) * SC_LANES, SC_LANES)]
                                     for j in range(words) for half in range(2))
                    accs = lax.fori_loop(0, PEER_GATHER_ROWS, body, init)
                    for j in range(words):
                        out_v[p, pl.ds((db * words + j) * SC_LANES, SC_LANES)] = accs[2 * j]
                        out_v[p, pl.ds(HW + (db * words + j) * SC_LANES, SC_LANES)] = accs[2 * j + 1]

        for d in load_meta(base, 0):
            d.start()
        for d in load_meta(base, 0):
            d.wait()
        for c in range(AHEAD):
            gather(0, c, c % NBUF).start()

        @pl.loop(0, tpw // 2)
        def _(i2):
            for p in range(2):
                i = i2 * 2 + p
                t = base + i
                nxt = base + jnp.minimum(i + 1, tpw - 1)
                for d in load_meta(nxt, 1 - p):
                    d.start()

                @pl.when(i2 > 0)
                def _():
                    store_out(t, p).wait()

                for c in range(PEER_GATHERS):
                    ahead = c + AHEAD
                    if ahead < PEER_GATHERS:
                        gather(p, ahead, ahead % NBUF).start()
                    else:
                        if ahead == PEER_GATHERS:
                            for d in load_meta(nxt, 1 - p):
                                d.wait()
                        gather(1 - p, ahead - PEER_GATHERS, ahead % NBUF).start()
                    gather(p, c, c % NBUF).wait()
                    compute(p, c, c % NBUF)
                store_out(t, p).start()

        for c in range(AHEAD):
            gather(0, c, c % NBUF).wait()
        for p in range(2):
            store_out(base, p).wait()

    return k(table, idx.reshape(N, PEER_GATHERS, PEER_GATHER_ROWS), aux)


PEER_TOKENS = 256
INT_BIG = 2 ** 30


def _extract_topk(cand_ref, ids_ref, val_out_ref, id_out_ref, row0):
    def body(r, carry):
        c = cand_ref[...]
        ids = ids_ref[...]
        m = jnp.max(c, axis=0, keepdims=True)
        sel = jnp.min(jnp.where(c == m, ids, INT_BIG), axis=0, keepdims=True)
        cand_ref[...] = jnp.where(ids == sel, -jnp.inf, c)
        val_out_ref[pl.ds(row0 + r, 1), :] = m
        id_out_ref[pl.ds(row0 + r, 1), :] = sel
        return carry
    lax.fori_loop(0, PEER_TOPK, body, 0)


def _peer_retrieve_kernel(x_ref, gain_ref, scale_ref, shift_ref, wq_ref, keys_ref,
                          h_ref, idx_out_ref, gate_out_ref,
                          s_ref, ids1_ref, sv_ref, si_ref, cand_ref, cid_ref, ts_ref, idx_ref, gate_ref):
    x = x_ref[0]
    y = x * lax.rsqrt(jnp.mean(x * x, axis=-1, keepdims=True) + NORM_EPS)
    h = (y * gain_ref[...]) * (1.0 + scale_ref[0]) + shift_ref[0]
    h_ref[0] = h
    q = jnp.dot(h.astype(jnp.bfloat16), wq_ref[...], preferred_element_type=jnp.float32)
    T = PEER_TOKENS
    K = PEER_TOPK
    ids1_ref[...] = lax.broadcasted_iota(jnp.int32, (PEER_N_KEYS, T), 0)
    for hd in range(PEER_HEADS):
        for p in range(2):
            hp = hd * 2 + p
            qs = q[:, hp * PEER_HALF:(hp + 1) * PEER_HALF].astype(jnp.bfloat16)
            s_ref[...] = lax.dot_general(keys_ref[hp], qs, (((1,), (1,)), ((), ())),
                                         preferred_element_type=jnp.float32)
            _extract_topk(s_ref, ids1_ref, sv_ref, si_ref, p * K)
        for i in range(K):
            cand_ref[i * K:(i + 1) * K, :] = sv_ref[i:i + 1, :] + sv_ref[K:2 * K, :]
            cid_ref[i * K:(i + 1) * K, :] = si_ref[i:i + 1, :] * PEER_N_KEYS + si_ref[K:2 * K, :]
        _extract_topk(cand_ref, cid_ref, ts_ref, idx_ref, hd * K)
        ts = ts_ref[hd * K:(hd + 1) * K, :]
        e = jnp.exp(ts - jnp.max(ts, axis=0, keepdims=True))
        gate_ref[hd * K:(hd + 1) * K, :] = e / jnp.sum(e, axis=0, keepdims=True)
    idx_out_ref[...] = idx_ref[...].T
    gate_out_ref[...] = gate_ref[...].T


def peer_retrieve(x, gain, scale, shift, w_q, sub_keys):
    B, L, D = x.shape
    T = PEER_TOKENS
    nt = L // T
    keys = sub_keys.reshape(PEER_HEADS * 2, PEER_N_KEYS, PEER_HALF).astype(jnp.bfloat16)
    return pl.pallas_call(
        _peer_retrieve_kernel,
        grid=(B, nt),
        in_specs=[
            pl.BlockSpec((1, T, D), lambda b, i: (b, i, 0)),
            pl.BlockSpec((1, D), lambda b, i: (0, 0)),
            pl.BlockSpec((1, 1, D), lambda b, i: (b, 0, 0)),
            pl.BlockSpec((1, 1, D), lambda b, i: (b, 0, 0)),
            pl.BlockSpec((D, PEER_HEADS * 2 * PEER_HALF), lambda b, i: (0, 0)),
            pl.BlockSpec((PEER_HEADS * 2, PEER_N_KEYS, PEER_HALF), lambda b, i: (0, 0, 0)),
        ],
        out_specs=[
            pl.BlockSpec((1, T, D), lambda b, i: (b, i, 0)),
            pl.BlockSpec((T, PEER_SLOTS), lambda b, i: (b * nt + i, 0)),
            pl.BlockSpec((T, PEER_SLOTS), lambda b, i: (b * nt + i, 0)),
        ],
        out_shape=[
            jax.ShapeDtypeStruct((B, L, D), jnp.float32),
            jax.ShapeDtypeStruct((B * L, PEER_SLOTS), jnp.int32),
            jax.ShapeDtypeStruct((B * L, PEER_SLOTS), jnp.float32),
        ],
        scratch_shapes=[
            pltpu.VMEM((PEER_N_KEYS, T), jnp.float32),
            pltpu.VMEM((PEER_N_KEYS, T), jnp.int32),
            pltpu.VMEM((2 * PEER_TOPK, T), jnp.float32),
            pltpu.VMEM((2 * PEER_TOPK, T), jnp.int32),
            pltpu.VMEM((PEER_TOPK * PEER_TOPK, T), jnp.float32),
            pltpu.VMEM((PEER_TOPK * PEER_TOPK, T), jnp.int32),
            pltpu.VMEM((PEER_SLOTS, T), jnp.float32),
            pltpu.VMEM((PEER_SLOTS, T), jnp.int32),
            pltpu.VMEM((PEER_SLOTS, T), jnp.float32),
        ],
        compiler_params=pltpu.CompilerParams(dimension_semantics=("parallel", "parallel"),
                                             vmem_limit_bytes=VMEM_LIMIT_BYTES),
        name="peer_retrieve",
    )(x, gain.reshape(1, D), scale, shift, w_q.astype(jnp.bfloat16), keys)


PEER_ACT_ROWS = 1024


def _peer_act_kernel(dots_ref, gate_ref, w_ref):
    a = dots_ref[...]
    w_ref[...] = gate_ref[...] * (0.5 * a * (1.0 + lax.erf(a * (2.0 ** -0.5))))


def peer_act(dots, gate):
    N = dots.shape[0]
    T = min(PEER_ACT_ROWS, N)
    spec = pl.BlockSpec((T, PEER_SLOTS), lambda i: (i, 0))
    return pl.pallas_call(
        _peer_act_kernel,
        grid=(N // T,),
        in_specs=[spec, spec],
        out_specs=spec,
        out_shape=jax.ShapeDtypeStruct((N, PEER_SLOTS), jnp.float32),
        compiler_params=pltpu.CompilerParams(dimension_semantics=("parallel",)),
        name="peer_act",
    )(dots, gate)


def peer_ffn(x, gain, scale, shift, w_q, sub_keys, exp_u, exp_v):
    B, L, D = x.shape
    N = B * L
    h, e_idx, gate = peer_retrieve(x, gain, scale, shift, w_q, sub_keys)
    dots = _sc_peer(exp_u, e_idx, h.reshape(N, D), "dot")
    return _sc_peer(exp_v, e_idx, peer_act(dots, gate), "wsum").reshape(B, L, D)


def _mix_and_retrieve(li, x, c, ctx, c_ctx, mod_w, mod_b, mix_norm, w_in, w_out, rw_conv, rw_decay_up, rw_decay0, rw_a_up, rw_a0, rw_gate_up, rw_k_k, rw_k_a, rw_r_k, rw_gn_g, rw_gn_b, mla_q_norm, mla_w_uq, mla_kv_norm, mla_w_ukv, mla_q_gain, mla_k_gain, hy_conv, hy_w1, hy_b1, hy_freq1, hy_w2, hy_b2, hy_freq2, hy_w3, hy_b3, hy_bias, ffn_norm, peer_wq, peer_keys, peer_u, peer_v):
    B, L, D = x.shape
    s_rw, s_mla = RW_PROJ, RW_PROJ + MLA_PROJ
    need_ctx = li < DEPTH - 1
    mod_l = (jax.nn.silu(c) @ mod_w[li] + mod_b[li])[:, None, :]
    mod_c = (jax.nn.silu(c_ctx) @ mod_w[li] + mod_b[li])[None, None, :]
    shm_l, scm_l, gm_l, shf_l, scf_l, gf_l = jnp.split(mod_l, N_MOD, axis=-1)
    shm_c, scm_c, gm_c, shf_c, scf_c, gf_c = jnp.split(mod_c, N_MOD, axis=-1)

    p_l = norm_mod_proj(x, mix_norm[li], scm_l, shm_l, w_in[li], 512)
    p_c = norm_mod_proj(ctx, mix_norm[li], jnp.broadcast_to(scm_c, (B, 1, D)),
                        jnp.broadcast_to(shm_c, (B, 1, D)), w_in[li], 256)
    rw_l, rw_c = rwkv7_mixer(p_l[..., :s_rw], p_c[..., :s_rw], rw_conv[li], rw_decay_up[li], rw_decay0[li],
                             rw_a_up[li], rw_a0[li], rw_gate_up[li], rw_k_k[li], rw_k_a[li], rw_r_k[li],
                             rw_gn_g[li], rw_gn_b[li], need_ctx)
    ml_l, ml_c = mla_mixer(p_l[..., s_rw:s_mla], p_c[..., s_rw:s_mla], mla_q_norm[li], mla_w_uq[li],
                           mla_kv_norm[li], mla_w_ukv[li], mla_q_gain[li], mla_k_gain[li], need_ctx)
    hy_prm = (hy_conv[li], hy_w1[li], hy_b1[li], hy_freq1[li], hy_w2[li], hy_b2[li], hy_freq2[li],
              hy_w3[li], hy_b3[li], hy_bias[li])
    hy_l = hyena_mixer(p_l[..., s_mla:], *hy_prm)
    x = x + gm_l * (jnp.concatenate([rw_l, ml_l, hy_l], axis=-1) @ w_out[li])
    if need_ctx:
        hy_c = hyena_mixer(p_c[..., s_mla:], *hy_prm)
        ctx = ctx + gm_c * (jnp.concatenate([rw_c, ml_c, hy_c], axis=-1) @ w_out[li])
        ctx = ctx + gf_c * peer_ffn(ctx, ffn_norm[li], jnp.broadcast_to(scf_c, (B, 1, D)),
                                    jnp.broadcast_to(shf_c, (B, 1, D)),
                                    peer_wq[li], peer_keys[li], peer_u[li], peer_v[li])
    h, e_idx, gate = peer_retrieve(x, ffn_norm[li], scf_l, shf_l, peer_wq[li], peer_keys[li])
    return x, ctx, gf_l, h.reshape(B * L, D), e_idx, gate


BATCH_GROUPS = 8


def kernel(x, c, ctx, c_ctx, mod_w, mod_b, mix_norm, w_in, w_out, rw_conv, rw_decay_up, rw_decay0, rw_a_up, rw_a0, rw_gate_up, rw_k_k, rw_k_a, rw_r_k, rw_gn_g, rw_gn_b, mla_q_norm, mla_w_uq, mla_kv_norm, mla_w_ukv, mla_q_gain, mla_k_gain, hy_conv, hy_w1, hy_b1, hy_freq1, hy_w2, hy_b2, hy_freq2, hy_w3, hy_b3, hy_bias, ffn_norm, peer_wq, peer_keys, peer_u, peer_v):
    params = (mod_w, mod_b, mix_norm, w_in, w_out, rw_conv, rw_decay_up, rw_decay0, rw_a_up, rw_a0, rw_gate_up,
              rw_k_k, rw_k_a, rw_r_k, rw_gn_g, rw_gn_b, mla_q_norm, mla_w_uq, mla_kv_norm, mla_w_ukv, mla_q_gain,
              mla_k_gain, hy_conv, hy_w1, hy_b1, hy_freq1, hy_w2, hy_b2, hy_freq2, hy_w3, hy_b3, hy_bias,
              ffn_norm, peer_wq, peer_keys)
    peer_u = [pack_expert_table(peer_u[li]) for li in range(DEPTH)]
    peer_v = [pack_expert_table(peer_v[li]) for li in range(DEPTH)]
    params = params + (peer_u, peer_v)
    G = BATCH_GROUPS
    bg = x.shape[0] // G
    L, D = x.shape[1:]
    xs = [x[g * bg:(g + 1) * bg] for g in range(G)]
    cs = [c[g * bg:(g + 1) * bg] for g in range(G)]
    ctxs = [ctx[g * bg:(g + 1) * bg] for g in range(G)]
    stages = [(li, g) for li in range(DEPTH) for g in range(G)]
    pending = {}
    token = None
    for k in range(len(stages) + 1):
        if k < len(stages):
            li, g = stages[k]
            ins = (xs[g], ctxs[g])
            if token is not None:
                token, ins = lax.optimization_barrier((token, ins))
            xm, ctxs[g], gf, h, e_idx, gate = _mix_and_retrieve(li, ins[0], cs[g], ins[1], c_ctx, *params)
            dots = _sc_peer(peer_u[li], e_idx, h, "dot")
            pending[k] = (xm, gf, e_idx, gate, dots)
            token = gate
        if k >= 1:
            li, g = stages[k - 1]
            xm, gf, e_idx, gate, dots = pending.pop(k - 1)
            token, (dots, gate) = lax.optimization_barrier((token, (dots, gate)))
            w = peer_act(dots, gate)
            xs[g] = xm + gf * _sc_peer(peer_v[li], e_idx, w, "wsum").reshape(bg, L, D)
            token = w
    return jnp.concatenate(xs, axis=0)
```

```python
import functools
import math

import jax
import jax.numpy as jnp
import numpy as np
from jax import lax
from jax.experimental import pallas as pl
from jax.experimental.pallas import tpu as pltpu
from jax.experimental.pallas import tpu_sc as plsc

D_MODEL = 1024
DEPTH = 2
GRID_W = 64
N_MOD = 6
NORM_EPS = 1e-6

RW_HEADS = 6
RW_HEAD_DIM = 64
RW_WIDTH = RW_HEADS * RW_HEAD_DIM
RW_DECAY_RANK = 64
RW_A_RANK = 64
RW_GATE_RANK = 128
RW_DECAY_SCALE = 0.6065306597
RW_GN_EPS = 64e-5
L2_EPS = 1e-12

MLA_HEADS = 6
MLA_Q_RANK = 256
MLA_KV_RANK = 128
MLA_NOPE_DIM = 64
MLA_ROPE_DIM = 32
MLA_V_DIM = 64
MLA_QK_DIM = MLA_NOPE_DIM + MLA_ROPE_DIM
MLA_WIDTH = MLA_HEADS * MLA_V_DIM
AXIS_ROPE_DIM = MLA_ROPE_DIM // 2
ROPE_THETA = 10000.0

HY_WIDTH = 256
HY_ORDER = 2
HY_POS_BANDS = 16
HY_SHORT_DECAY_PCT = 0.3
HY_LONG_DECAY_PCT = 1.5
HY_DECAY_TARGET = 1e-2

PEER_HEADS = 8
PEER_N_KEYS = 128
PEER_TOPK = 16
PEER_QUERY_DIM = 256
PEER_HALF = PEER_QUERY_DIM // 2

RW_PROJ = 3 * RW_WIDTH + RW_DECAY_RANK + RW_A_RANK + RW_GATE_RANK
MLA_PROJ = MLA_Q_RANK + MLA_KV_RANK + MLA_ROPE_DIM
HY_PROJ = (HY_ORDER + 1) * HY_WIDTH
IN_PROJ = RW_PROJ + MLA_PROJ + HY_PROJ
MIX_WIDTH = RW_WIDTH + MLA_WIDTH + HY_WIDTH

VMEM_LIMIT_BYTES = 48 * 1024 * 1024


def _norm_mod_proj_kernel(x_ref, gain_ref, scale_ref, shift_ref, w_ref, o_ref):
    x = x_ref[0]
    y = x * lax.rsqrt(jnp.mean(x * x, axis=-1, keepdims=True) + NORM_EPS)
    y = y * gain_ref[...]
    y = y * (1.0 + scale_ref[0]) + shift_ref[0]
    o_ref[0] = jnp.dot(y.astype(jnp.bfloat16), w_ref[...], preferred_element_type=jnp.float32)


def norm_mod_proj(x, gain, scale, shift, w, block_rows):
    B, L, D = x.shape
    N = w.shape[1]
    return pl.pallas_call(
        _norm_mod_proj_kernel,
        grid=(B, L // block_rows),
        in_specs=[
            pl.BlockSpec((1, block_rows, D), lambda b, i: (b, i, 0)),
            pl.BlockSpec((1, D), lambda b, i: (0, 0)),
            pl.BlockSpec((1, 1, D), lambda b, i: (b, 0, 0)),
            pl.BlockSpec((1, 1, D), lambda b, i: (b, 0, 0)),
            pl.BlockSpec((D, N), lambda b, i: (0, 0)),
        ],
        out_specs=pl.BlockSpec((1, block_rows, N), lambda b, i: (b, i, 0)),
        out_shape=jax.ShapeDtypeStruct((B, L, N), jnp.float32),
        compiler_params=pltpu.CompilerParams(
            dimension_semantics=("parallel", "parallel"), vmem_limit_bytes=VMEM_LIMIT_BYTES),
        name="norm_mod_proj",
    )(x, gain.reshape(1, D), scale, shift, w.astype(jnp.bfloat16))


RW_CHUNK = 64


def _rwkv_chunk_kernel(r_ref, kk_ref, v_ref, lw_ref, akk_ref, kr_ref, y_ref, h_ref):
    d = pl.program_id(0)
    n = pl.program_id(2)

    @pl.when(n == 0)
    def _():
        h_ref[...] = jnp.zeros_like(h_ref)

    C = RW_CHUNK
    row = lax.broadcasted_iota(jnp.int32, (C, C), 0)
    col = lax.broadcasted_iota(jnp.int32, (C, C), 1)
    lag = (row - col) * (1 - 2 * d)
    before = lag > 0
    upto = lag >= 0
    tri = upto.astype(jnp.float32)
    eye = (row == col).astype(jnp.float32)
    bf = jnp.bfloat16
    f32 = jnp.float32

    def mm(a, b):
        return jnp.dot(a.astype(bf), b.astype(bf), preferred_element_type=f32)

    def mm_nt(a, b):
        return lax.dot_general(a.astype(bf), b.astype(bf), (((1,), (1,)), ((), ())), preferred_element_type=f32)

    def mm_tn(a, b):
        return lax.dot_general(a.astype(bf), b.astype(bf), (((0,), (0,)), ((), ())), preferred_element_type=f32)

    hs = range(RW_HEADS)
    HD = RW_HEAD_DIM
    heads = lambda t: [t[:, h * HD:(h + 1) * HD] for h in hs]
    r = heads(r_ref[0])
    kk = heads(kk_ref[0])
    v = heads(v_ref[0])
    lw = heads(lw_ref[0, 0])
    akk = heads(akk_ref[0, 0])
    kr = heads(kr_ref[0, 0])
    G = [jnp.dot(tri, lw[h], preferred_element_type=f32, precision=lax.Precision.HIGHEST) for h in hs]
    gtot = [jnp.sum(lw[h], axis=0, keepdims=True) for h in hs]
    Einv = [jnp.exp(-G[h]) for h in hs]
    At = [-kk[h] * jnp.exp(G[h] - lw[h]) for h in hs]
    Rt = [r[h] * jnp.exp(G[h]) for h in hs]
    Bt = [akk[h] * Einv[h] for h in hs]
    Kt = [kr[h] * Einv[h] for h in hs]
    X = [mm_nt(jnp.concatenate([At[h], Rt[h]], axis=0), jnp.concatenate([Bt[h], Kt[h]], axis=0)) for h in hs]
    M_ab = [jnp.where(before, X[h][:C, :C], 0.0) for h in hs]
    M_ak = [jnp.where(before, X[h][:C, C:], 0.0) for h in hs]
    A_rb = [jnp.where(upto, X[h][C:, :C], 0.0) for h in hs]
    A_rk = [jnp.where(upto, X[h][C:, C:], 0.0) for h in hs]
    MV = [mm(M_ak[h], v[h]) for h in hs]
    Mp = M_ab
    T = [eye + Mp[h] for h in hs]
    for _ in range(5):
        Mp = [jnp.dot(Mp[h], Mp[h], preferred_element_type=f32) for h in hs]
        T = [T[h] + jnp.dot(T[h], Mp[h], preferred_element_type=f32) for h in hs]
    WU = [jnp.dot(T[h], jnp.concatenate([At[h], MV[h]], axis=1), preferred_element_type=f32) for h in hs]
    H0 = [h_ref[h] for h in hs]
    Ehat = [jnp.exp(gtot[h] - G[h]) for h in hs]
    Om = [Rt[h] + mm(A_rb[h], WU[h][:, :HD]) for h in hs]
    Y0 = [mm(A_rb[h], WU[h][:, HD:]) + mm(A_rk[h], v[h]) for h in hs]
    BW = [mm_tn(akk[h] * Ehat[h], WU[h]) for h in hs]
    KV = [mm_tn(kr[h] * Ehat[h], v[h]) for h in hs]
    y_ref[0, 0] = jnp.concatenate([jnp.dot(Om[h], H0[h], preferred_element_type=f32) + Y0[h] for h in hs], axis=1)
    for h in hs:
        P = eye * jnp.exp(gtot[h]) + BW[h][:, :HD]
        h_ref[h] = jnp.dot(P, H0[h], preferred_element_type=f32) + BW[h][:, HD:] + KV[h]


def rwkv_chunked(r, kk, v, lw, akk, kr, n_ctx):
    B, T, W = r.shape
    H = W // RW_HEAD_DIM
    nc = n_ctx // RW_CHUNK
    nt = T // RW_CHUNK

    def chunk_of(d, n):
        bwd = jnp.where(n < nc, nc - 1 - n, nt - 1 - (n - nc))
        return jnp.where(d == 0, n, bwd)

    spec1 = pl.BlockSpec((1, RW_CHUNK, W), lambda d, b, n: (b, chunk_of(d, n), 0))
    spec2 = pl.BlockSpec((1, 1, RW_CHUNK, W), lambda d, b, n: (d, b, chunk_of(d, n), 0))
    return pl.pallas_call(
        _rwkv_chunk_kernel,
        grid=(2, B, nt),
        in_specs=[spec1, spec1, spec1, spec2, spec2, spec2],
        out_specs=spec2,
        out_shape=jax.ShapeDtypeStruct((2, B, T, W), jnp.float32),
        scratch_shapes=[pltpu.VMEM((H, RW_HEAD_DIM, RW_HEAD_DIM), jnp.float32)],
        compiler_params=pltpu.CompilerParams(dimension_semantics=("parallel", "parallel", "arbitrary")),
        name="rwkv_chunked",
    )(r, kk, v, lw, akk, kr)


def short_conv(x, w):
    xp = jnp.pad(x, ((0, 0), (1, 1), (0, 0)))
    return xp[:, :-2] * w[0] + xp[:, 1:-1] * w[1] + xp[:, 2:] * w[2]


LANE = 128
RW_PREP_ROWS = 256
MLA_PAD_WIDTH = MLA_HEADS * LANE
MLA_PREP_ROWS = 256
ATTN_Q_ROWS = 512


def _split_dot(x, m):
    hi = x.astype(jnp.bfloat16)
    lo = (x - hi.astype(jnp.float32)).astype(jnp.bfloat16)
    return (jnp.dot(hi, m, preferred_element_type=jnp.float32) + jnp.dot(lo, m, preferred_element_type=jnp.float32))


def _rwkv_prep_kernel(z_ref, wda_ref, d0_ref, a0_ref, gup_ref, kk_ref_w, ka_ref, rk_ref, hsum_ref,
                      r_ref, kk_ref, v_ref, lw_ref, akk_ref, kr_ref, g_ref, bonus_ref):
    W = RW_WIDTH
    bf = jnp.bfloat16
    z = z_ref[0]
    r, k, v = z[:, :W], z[:, W:2 * W], z[:, 2 * W:3 * W]
    da = z[:, 3 * W:3 * W + LANE]
    lane = lax.broadcasted_iota(jnp.int32, da.shape, 1)
    da = jnp.where(lane < RW_DECAY_RANK, jnp.tanh(da), da)
    up = jnp.dot(da.astype(bf), wda_ref[...], preferred_element_type=jnp.float32)
    g_lo = z[:, 3 * W + LANE:]
    g_ref[0] = jnp.dot(jax.nn.sigmoid(g_lo).astype(bf), gup_ref[...], preferred_element_type=jnp.float32)
    hsum = hsum_ref[...]
    kk = k * kk_ref_w[...]
    kk = kk * lax.rsqrt(_split_dot(kk * kk, hsum) + L2_EPS)
    r_ref[0] = r
    v_ref[0] = v
    kk_ref[0] = kk
    bonus_ref[0] = _split_dot(r * k * rk_ref[...], hsum) * v
    for d in range(2):
        lw_ref[d, 0] = -RW_DECAY_SCALE * jax.nn.sigmoid(d0_ref[d:d + 1, :] + up[:, d * W:(d + 1) * W])
        a = jax.nn.sigmoid(a0_ref[d:d + 1, :] + up[:, (2 + d) * W:(3 + d) * W])
        akk_ref[d, 0] = kk * a
        kr_ref[d, 0] = k * (1.0 + (a - 1.0) * ka_ref[...])


def rwkv_prep(z, decay_up, decay0, a_up, a0, gate_up, k_k, k_a, r_k):
    B, L, _ = z.shape
    W = RW_WIDTH
    T = min(RW_PREP_ROWS, L)
    zero = jnp.zeros((RW_DECAY_RANK, 2 * W), jnp.float32)
    wda = jnp.concatenate([
        jnp.concatenate([decay_up[0], decay_up[1], zero], axis=1),
        jnp.concatenate([zero, a_up[0], a_up[1]], axis=1)], axis=0).astype(jnp.bfloat16)
    head = jnp.arange(W) // RW_HEAD_DIM
    hsum = (head[:, None] == head[None, :]).astype(jnp.bfloat16)
    row = lambda a: a.reshape(1, W)
    const = lambda a: pl.BlockSpec(a.shape, lambda b, i: (0,) * a.ndim)
    tok = pl.BlockSpec((1, T, W), lambda b, i: (b, i, 0))
    tok2 = pl.BlockSpec((2, 1, T, W), lambda b, i: (0, b, i, 0))
    f1 = jax.ShapeDtypeStruct((B, L, W), jnp.float32)
    f2 = jax.ShapeDtypeStruct((2, B, L, W), jnp.float32)
    args = (z, wda, decay0, a0, gate_up.astype(jnp.bfloat16), row(k_k), row(k_a), row(r_k), hsum)
    return pl.pallas_call(
        _rwkv_prep_kernel,
        grid=(B, L // T),
        in_specs=[pl.BlockSpec((1, T, RW_PROJ), lambda b, i: (b, i, 0))] + [const(a) for a in args[1:]],
        out_specs=[tok, tok, tok, tok2, tok2, tok2, tok, tok],
        out_shape=[f1, f1, f1, f2, f2, f2, f1, f1],
        compiler_params=pltpu.CompilerParams(dimension_semantics=("parallel", "parallel"),
                                             vmem_limit_bytes=VMEM_LIMIT_BYTES),
        name="rwkv_prep",
    )(*args)


def _rwkv_readout_kernel(y_ref, g_ref, bonus_ref, gng_ref, gnb_ref, hsum_ref, o_ref):
    y = y_ref[0, 0] + y_ref[1, 0]
    hsum = hsum_ref[...]
    mu = _split_dot(y, hsum) * (1.0 / RW_HEAD_DIM)
    d = y - mu
    var = _split_dot(d * d, hsum) * (1.0 / RW_HEAD_DIM)
    yn = d * lax.rsqrt(var + RW_GN_EPS) * gng_ref[...] + gnb_ref[...]
    o_ref[0] = (yn + bonus_ref[0]) * g_ref[0]


def rwkv_readout(y, g, bonus, gn_g, gn_b, t0):
    B, L, W = g.shape
    T = min(RW_PREP_ROWS, L)
    off = t0 // T
    head = jnp.arange(W) // RW_HEAD_DIM
    hsum = (head[:, None] == head[None, :]).astype(jnp.bfloat16)
    tok = pl.BlockSpec((1, T, W), lambda b, i: (b, i, 0))
    const = lambda a: pl.BlockSpec(a.shape, lambda b, i: (0,) * a.ndim)
    gg, gb = gn_g.reshape(1, W), gn_b.reshape(1, W)
    return pl.pallas_call(
        _rwkv_readout_kernel,
        grid=(B, L // T),
        in_specs=[pl.BlockSpec((2, 1, T, W), lambda b, i: (0, b, i + off, 0)), tok, tok, const(gg), const(gb), const(hsum)],
        out_specs=tok,
        out_shape=jax.ShapeDtypeStruct((B, L, W), jnp.float32),
        compiler_params=pltpu.CompilerParams(dimension_semantics=("parallel", "parallel")),
        name="rwkv_readout",
    )(y, g, bonus, gg, gb, hsum)


def rwkv7_mixer(p_lat, p_ctx, conv_w, decay_up, decay0, a_up, a0, gate_up, k_k, k_a, r_k, gn_g, gn_b, need_ctx):
    prm = (decay_up, decay0, a_up, a0, gate_up, k_k, k_a, r_k)
    lat = rwkv_prep(short_conv(p_lat, conv_w), *prm)
    ctx = rwkv_prep(short_conv(p_ctx, conv_w), *prm)
    n_ctx = p_ctx.shape[1]
    seq = lambda i: jnp.concatenate([ctx[i], lat[i]], axis=-2)
    y = rwkv_chunked(seq(0), seq(1), seq(2), seq(3), seq(4), seq(5), n_ctx)
    out_l = rwkv_readout(y, lat[6], lat[7], gn_g, gn_b, n_ctx)
    out_c = rwkv_readout(y, ctx[6], ctx[7], gn_g, gn_b, 0) if need_ctx else None
    return out_l, out_c


def _rope_tables(L, use_rope):
    lane = np.arange(LANE)
    in_rope = (lane >= MLA_NOPE_DIM) & (lane < MLA_QK_DIM)
    j = lane - MLA_NOPE_DIM
    axis = j // AXIS_ROPE_DIM
    half = AXIS_ROPE_DIM // 2
    f = j % half
    first = (j % AXIS_ROPE_DIM) < half
    inv = ROPE_THETA ** (-jnp.arange(0, AXIS_ROPE_DIM, 2, dtype=jnp.float32) / AXIS_ROPE_DIM)
    t = jnp.arange(L)
    pos = jnp.stack([t // GRID_W, t % GRID_W], axis=-1).astype(jnp.float32)
    ang = pos[:, np.clip(axis, 0, 1)] * inv[np.clip(f, 0, half - 1)][None, :]
    rope_on = jnp.asarray(in_rope)[None, :] & use_rope
    cos = jnp.where(rope_on, jnp.cos(ang), 1.0)
    sin = jnp.where(rope_on, jnp.sin(ang) * jnp.where(jnp.asarray(first), -1.0, 1.0)[None, :], 0.0)
    return jnp.tile(cos, (1, MLA_HEADS)), jnp.tile(sin, (1, MLA_HEADS))


def _mla_prep_kernel(p_ref, qn_ref, wq_ref, kvn_ref, wk_ref, wv_ref, place_ref, qg_ref, kg_ref, hsum_ref, cos_ref, sin_ref,
                     q_ref, k_ref, v_ref):
    bf = jnp.bfloat16
    p = p_ref[0]
    c_q = p[:, :MLA_Q_RANK]
    c_kv = p[:, MLA_Q_RANK:MLA_Q_RANK + MLA_KV_RANK]
    tail = p[:, MLA_Q_RANK + MLA_KV_RANK:]
    cqn = c_q * lax.rsqrt(jnp.mean(c_q * c_q, axis=-1, keepdims=True) + NORM_EPS) * qn_ref[...]
    ckn = c_kv * lax.rsqrt(jnp.mean(c_kv * c_kv, axis=-1, keepdims=True) + NORM_EPS) * kvn_ref[...]
    q = jnp.dot(cqn.astype(bf), wq_ref[...], preferred_element_type=jnp.float32)
    k = jnp.dot(ckn.astype(bf), wk_ref[...], preferred_element_type=jnp.float32) + _split_dot(tail, place_ref[...])
    v_ref[0] = jnp.dot(ckn.astype(bf), wv_ref[...], preferred_element_type=jnp.float32).astype(bf)
    hsum = hsum_ref[...]
    cos, sin = cos_ref[...], sin_ref[...]
    lane = lax.broadcasted_iota(jnp.int32, q.shape, 1)
    first = ((lane - MLA_NOPE_DIM) % AXIS_ROPE_DIM) < (AXIS_ROPE_DIM // 2)
    half = AXIS_ROPE_DIM // 2

    def finish(x, gain):
        x = x * lax.rsqrt(_split_dot(x * x, hsum) * (1.0 / MLA_QK_DIM) + NORM_EPS) * gain
        partner = jnp.where(first, pltpu.roll(x, MLA_PAD_WIDTH - half, 1), pltpu.roll(x, half, 1))
        return x * cos + partner * sin

    q_ref[0] = (finish(q, qg_ref[...]) * (MLA_QK_DIM ** -0.5)).astype(bf)
    k_ref[0] = finish(k, kg_ref[...]).astype(bf)


def mla_prep(p, use_rope, q_norm, w_uq, kv_norm, w_ukv, q_gain, k_gain):
    B, L, _ = p.shape
    T = min(MLA_PREP_ROWS, L)
    H = MLA_HEADS
    pad_cols = lambda w, d: jnp.pad(w.reshape(w.shape[0], H, d), ((0, 0), (0, 0), (0, LANE - d))).reshape(w.shape[0], H * LANE)
    wq = pad_cols(w_uq, MLA_QK_DIM).astype(jnp.bfloat16)
    ukv = w_ukv.reshape(MLA_KV_RANK, H, MLA_NOPE_DIM + MLA_V_DIM)
    wk = pad_cols(ukv[:, :, :MLA_NOPE_DIM].reshape(MLA_KV_RANK, H * MLA_NOPE_DIM), MLA_NOPE_DIM).astype(jnp.bfloat16)
    wv = ukv[:, :, MLA_NOPE_DIM:].reshape(MLA_KV_RANK, H * MLA_V_DIM).astype(jnp.bfloat16)
    lane = np.arange(H * LANE)
    place = jnp.asarray(((lane[None, :] % LANE) - MLA_NOPE_DIM == np.arange(MLA_ROPE_DIM)[:, None]), jnp.bfloat16)
    hsum = jnp.asarray((lane[:, None] // LANE) == (lane[None, :] // LANE), jnp.bfloat16)
    pad_gain = lambda g: jnp.tile(jnp.pad(g, (0, LANE - MLA_QK_DIM)), H).reshape(1, H * LANE)
    cos, sin = _rope_tables(L, use_rope)
    const = lambda a: pl.BlockSpec(a.shape, lambda b, i: (0,) * a.ndim)
    args = (p, q_norm.reshape(1, -1), wq, kv_norm.reshape(1, -1), wk, wv, place, pad_gain(q_gain), pad_gain(k_gain), hsum)
    pos = pl.BlockSpec((T, H * LANE), lambda b, i: (i, 0))
    return pl.pallas_call(
        _mla_prep_kernel,
        grid=(B, L // T),
        in_specs=[pl.BlockSpec((1, T, MLA_PROJ), lambda b, i: (b, i, 0))] + [const(a) for a in args[1:]] + [pos, pos],
        out_specs=[pl.BlockSpec((1, T, H * LANE), lambda b, i: (b, i, 0)), pl.BlockSpec((1, T, H * LANE), lambda b, i: (b, i, 0)),
                   pl.BlockSpec((1, T, MLA_WIDTH), lambda b, i: (b, i, 0))],
        out_shape=[jax.ShapeDtypeStruct((B, L, H * LANE), jnp.bfloat16), jax.ShapeDtypeStruct((B, L, H * LANE), jnp.bfloat16),
                   jax.ShapeDtypeStruct((B, L, MLA_WIDTH), jnp.bfloat16)],
        compiler_params=pltpu.CompilerParams(dimension_semantics=("parallel", "parallel"),
                                             vmem_limit_bytes=VMEM_LIMIT_BYTES),
        name="mla_prep",
    )(*args, cos, sin)


def _attn_kernel(q_ref, k_ref, v_ref, o_ref):
    lane = lax.broadcasted_iota(jnp.int32, (q_ref.shape[1], LANE), 1)
    for pair in range(MLA_HEADS // 2):
        v_pair = v_ref[0, :, pair * LANE:(pair + 1) * LANE]
        outs = []
        for h in (2 * pair, 2 * pair + 1):
            q = q_ref[0, :, h * LANE:(h + 1) * LANE]
            k = k_ref[0, :, h * LANE:(h + 1) * LANE]
            s = lax.dot_general(q, k, (((1,), (1,)), ((), ())), preferred_element_type=jnp.float32)
            e = jnp.exp(s - jnp.max(s, axis=-1, keepdims=True))
            o = jnp.dot(e.astype(jnp.bfloat16), v_pair, preferred_element_type=jnp.float32)
            outs.append(o / jnp.sum(e, axis=-1, keepdims=True))
        o_ref[0, :, pair * LANE:(pair + 1) * LANE] = jnp.where(lane < MLA_V_DIM, outs[0], outs[1])


def attention(q, k, v):
    B, Lq, P = q.shape
    Lk = k.shape[1]
    tq = min(ATTN_Q_ROWS, Lq)
    return pl.pallas_call(
        _attn_kernel,
        grid=(B, Lq // tq),
        in_specs=[pl.BlockSpec((1, tq, P), lambda b, i: (b, i, 0)),
                  pl.BlockSpec((1, Lk, P), lambda b, i: (b, 0, 0)),
                  pl.BlockSpec((1, Lk, MLA_WIDTH), lambda b, i: (b, 0, 0))],
        out_specs=pl.BlockSpec((1, tq, MLA_WIDTH), lambda b, i: (b, i, 0)),
        out_shape=jax.ShapeDtypeStruct((B, Lq, MLA_WIDTH), jnp.float32),
        compiler_params=pltpu.CompilerParams(dimension_semantics=("parallel", "parallel"),
                                             vmem_limit_bytes=VMEM_LIMIT_BYTES),
        name="mla_attention",
    )(q, k, v)


def mla_mixer(p_lat, p_ctx, q_norm, w_uq, kv_norm, w_ukv, q_gain, k_gain, need_ctx):
    prm = (q_norm, w_uq, kv_norm, w_ukv, q_gain, k_gain)
    q_l, k_l, v_l = mla_prep(p_lat, True, *prm)
    q_c, k_c, v_c = mla_prep(p_ctx, False, *prm)
    y_l = attention(q_l, jnp.concatenate([k_l, k_c], axis=1), jnp.concatenate([v_l, v_c], axis=1))
    y_c = attention(q_c, k_c, v_c) if need_ctx else None
    return y_l, y_c


def hyena_filters(L, w1, b1, freq1, w2, b2, freq2, w3, b3):
    tn = jnp.arange(L, dtype=jnp.float32) / L
    bands = jnp.arange(1, HY_POS_BANDS + 1, dtype=jnp.float32)
    ang = 2.0 * math.pi * tn[:, None] * bands[None, :]
    z = jnp.concatenate([tn[:, None], jnp.cos(ang), jnp.sin(ang)], axis=-1)
    h = jnp.sin(freq1 * (z @ w1 + b1))
    h = jnp.sin(freq2 * (h @ w2 + b2))
    h = (h @ w3 + b3).reshape(L, HY_ORDER, 2, HY_WIDTH)
    rates = jnp.abs(jnp.linspace(math.log(HY_DECAY_TARGET) / HY_LONG_DECAY_PCT,
                                 math.log(HY_DECAY_TARGET) / HY_SHORT_DECAY_PCT, HY_WIDTH))
    h = h * jnp.exp(-tn[:, None] * rates[None, :])[:, None, None, :]
    zero = jnp.zeros((1, HY_ORDER, HY_WIDTH), h.dtype)
    h_full = jnp.concatenate([h[:, :, 0], zero, h[:0:-1, :, 1]], axis=0)
    return h_full * lax.rsqrt(jnp.sum(jnp.square(h_full), axis=0, keepdims=True))


def fft_long_conv(u, h_full, bias):
    L = u.shape[1]
    uf = jnp.fft.rfft(u, n=2 * L, axis=1)
    hf = jnp.fft.rfft(h_full, n=2 * L, axis=0)
    y = jnp.fft.irfft(uf * hf[None], n=2 * L, axis=1)[:, :L]
    return y + u * bias


FFT_N1 = 64
FFT_N2 = 128
FFT_N = FFT_N1 * FFT_N2
HY_SEQS = 32


def _dft_tables(seqs):
    n1 = np.arange(FFT_N1)
    n2 = np.arange(FFT_N2)
    f64 = np.exp(-2j * np.pi * np.outer(n1, n1) / FFT_N1)
    f128 = np.exp(-2j * np.pi * np.outer(n2, n2) / FFT_N2)
    tw = np.exp(-2j * np.pi * np.outer(n1, n2) / FFT_N)
    half = FFT_N1 // 2
    fh = f64[:, :half]
    m1 = np.block([[fh.real, -fh.imag], [fh.imag, fh.real]])
    m1f = np.concatenate([f64.real, f64.imag], axis=0)
    m2 = np.block([[f128.real, f128.imag], [-f128.imag, f128.real]])
    m3 = np.block([[f128.real, -f128.imag], [f128.imag, f128.real]]) / FFT_N
    c = np.conj(f64)[:half, :]
    m4 = np.block([[c.real, -c.imag], [c.imag, c.real]])
    bf = lambda a: jnp.asarray(a, jnp.float32).astype(jnp.bfloat16)
    f32 = lambda a: jnp.asarray(a, jnp.float32)
    return dict(m1=bf(m1), m1f=bf(m1f), m2=bf(m2), m3=bf(m3), m4=bf(m4),
                twr_l=f32(np.tile(tw.real, (1, seqs))), twi_l=f32(np.tile(tw.imag, (1, seqs))),
                twr_s=f32(np.tile(tw.real, (seqs, 1))), twi_s=f32(np.tile(tw.imag, (seqs, 1))))


def _spectrum(cols, m1, twr_l, twi_l, m2, R):
    a = jnp.dot(m1, cols.astype(jnp.bfloat16), preferred_element_type=jnp.float32)
    ar, ai = a[:FFT_N1], a[FFT_N1:]
    pr = ar * twr_l - ai * twi_l
    pi = ar * twi_l + ai * twr_l
    lhs = jnp.concatenate(
        [jnp.concatenate([pr[:, r * FFT_N2:(r + 1) * FFT_N2], pi[:, r * FFT_N2:(r + 1) * FFT_N2]], axis=1)
         for r in range(R)], axis=0)
    return jnp.dot(lhs.astype(jnp.bfloat16), m2, preferred_element_type=jnp.float32)


def _filter_fft_kernel(h_ref, m1f_ref, twr_ref, twi_ref, m2_ref, o_ref):
    R = HY_SEQS
    cols = jnp.concatenate([h_ref[r] for r in range(R)], axis=1)
    x = _spectrum(cols, m1f_ref[...], twr_ref[...], twi_ref[...], m2_ref[...], R)
    o_ref[...] = x.reshape(R, FFT_N1, 2 * FFT_N2)


def _hyena_conv_kernel(y_ref, g_ref, hf_ref, bias_ref, m1_ref, twr_l_ref, twi_l_ref, m2_ref, m3_ref,
                       twr_s_ref, twi_s_ref, m4_ref, o_ref):
    R = HY_SEQS
    half = FFT_N1 // 2
    y = [y_ref[0], y_ref[1]]
    for o in range(HY_ORDER):
        top = jnp.concatenate([y[0][r] for r in range(R)], axis=1)
        bot = jnp.concatenate([y[1][r] for r in range(R)], axis=1)
        x = _spectrum(jnp.concatenate([top, bot], axis=0), m1_ref[...], twr_l_ref[...], twi_l_ref[...], m2_ref[...], R)
        hf = hf_ref[o].reshape(R * FFT_N1, 2 * FFT_N2)
        xr, xi = x[:, :FFT_N2], x[:, FFT_N2:]
        hr, hi = hf[:, :FFT_N2], hf[:, FFT_N2:]
        yc = jnp.concatenate([xr * hr - xi * hi, xr * hi + xi * hr], axis=1)
        b = jnp.dot(yc.astype(jnp.bfloat16), m3_ref[...], preferred_element_type=jnp.float32)
        br, bi = b[:, :FFT_N2], b[:, FFT_N2:]
        qr = br * twr_s_ref[...] + bi * twi_s_ref[...]
        qi = bi * twr_s_ref[...] - br * twi_s_ref[...]
        bc = jnp.concatenate(
            [jnp.concatenate([qr[r * FFT_N1:(r + 1) * FFT_N1], qi[r * FFT_N1:(r + 1) * FFT_N1]], axis=0)
             for r in range(R)], axis=1)
        yo = jnp.dot(m4_ref[...], bc.astype(jnp.bfloat16), preferred_element_type=jnp.float32)
        for p in range(2):
            conv = jnp.stack([yo[p * half:(p + 1) * half, r * FFT_N2:(r + 1) * FFT_N2] for r in range(R)], axis=0)
            y[p] = g_ref[o, p] * (conv + y[p] * bias_ref[o])
    o_ref[0] = y[0]
    o_ref[1] = y[1]


def hyena_long_conv(y_t, g_t, h_t, bias):
    B, C, L = y_t.shape
    assert 2 * L == FFT_N and B % 2 == 0 and C % HY_SEQS == 0
    R = HY_SEQS
    half = FFT_N1 // 2
    tb = _dft_tables(R)
    const = lambda a: pl.BlockSpec(a.shape, lambda *_: (0,) * a.ndim)
    hf = pl.pallas_call(
        _filter_fft_kernel,
        grid=(HY_ORDER * C // R,),
        in_specs=[pl.BlockSpec((R, FFT_N1, FFT_N2), lambda i: (i, 0, 0)),
                  const(tb['m1f']), const(tb['twr_l']), const(tb['twi_l']), const(tb['m2'])],
        out_specs=pl.BlockSpec((R, FFT_N1, 2 * FFT_N2), lambda i: (i, 0, 0)),
        out_shape=jax.ShapeDtypeStruct((HY_ORDER * C, FFT_N1, 2 * FFT_N2), jnp.float32),
        compiler_params=pltpu.CompilerParams(dimension_semantics=("parallel",), vmem_limit_bytes=VMEM_LIMIT_BYTES),
        name="hyena_filter_fft",
    )(h_t.reshape(HY_ORDER * C, FFT_N1, FFT_N2), tb['m1f'], tb['twr_l'], tb['twi_l'], tb['m2'])
    hf = hf.reshape(HY_ORDER, C, FFT_N1, 2 * FFT_N2)
    out = pl.pallas_call(
        _hyena_conv_kernel,
        grid=(B // 2, C // R),
        in_specs=[pl.BlockSpec((2, R, half, FFT_N2), lambda b, c: (b, c, 0, 0)),
                  pl.BlockSpec((HY_ORDER, 2, R, half, FFT_N2), lambda b, c: (0, b, c, 0, 0)),
                  pl.BlockSpec((HY_ORDER, R, FFT_N1, 2 * FFT_N2), lambda b, c: (0, c, 0, 0)),
                  pl.BlockSpec((HY_ORDER, R, 1, 1), lambda b, c: (0, c, 0, 0)),
                  const(tb['m1']), const(tb['twr_l']), const(tb['twi_l']), const(tb['m2']), const(tb['m3']),
                  const(tb['twr_s']), const(tb['twi_s']), const(tb['m4'])],
        out_specs=pl.BlockSpec((2, R, half, FFT_N2), lambda b, c: (b, c, 0, 0)),
        out_shape=jax.ShapeDtypeStruct((B, C, half, FFT_N2), jnp.float32),
        compiler_params=pltpu.CompilerParams(dimension_semantics=("parallel", "parallel"),
                                             vmem_limit_bytes=VMEM_LIMIT_BYTES),
        name="hyena_conv",
    )(y_t.reshape(B, C, half, FFT_N2), g_t.reshape(HY_ORDER, B, C, half, FFT_N2), hf,
      bias.reshape(HY_ORDER, C, 1, 1), tb['m1'], tb['twr_l'], tb['twi_l'], tb['m2'], tb['m3'],
      tb['twr_s'], tb['twi_s'], tb['m4'])
    return out.reshape(B, C, L)


def hyena_mixer(p, conv_w, w1, b1, freq1, w2, b2, freq2, w3, b3, bias):
    B, L = p.shape[:2]
    z = short_conv(p, conv_w)
    h_full = hyena_filters(L, w1, b1, freq1, w2, b2, freq2, w3, b3)
    if 2 * L == FFT_N:
        g_t = jnp.transpose(z[..., :HY_ORDER * HY_WIDTH].reshape(B, L, HY_ORDER, HY_WIDTH), (2, 0, 3, 1))
        y_t = jnp.swapaxes(z[..., HY_ORDER * HY_WIDTH:], 1, 2)
        y_t = hyena_long_conv(y_t, g_t, jnp.transpose(h_full, (1, 2, 0)), bias)
        return jnp.swapaxes(y_t, 1, 2)
    gates = (z[..., :HY_WIDTH], z[..., HY_WIDTH:2 * HY_WIDTH])
    y = z[..., 2 * HY_WIDTH:]
    for o in range(HY_ORDER):
        y = gates[o] * fft_long_conv(y, h_full[:, o], bias[o])
    return y


SC_CORES = 2
SC_SUBCORES = 16
SC_LANES = 16
SC_WORKERS = SC_CORES * SC_SUBCORES
PEER_SLOTS = PEER_HEADS * PEER_TOPK
PEER_GATHER_ROWS = 32
PEER_GATHERS = PEER_SLOTS // PEER_GATHER_ROWS
PEER_ACC_VREGS = 8
PEER_ROW_BUFFERS = 4
PEER_ROW_WORDS = D_MODEL // 2
HI_MASK = -65536


def pack_expert_table(t):
    b = lax.bitcast_convert_type(t.astype(jnp.bfloat16), jnp.uint16).astype(jnp.uint32)
    return lax.bitcast_convert_type(b[:, :PEER_ROW_WORDS] | (b[:, PEER_ROW_WORDS:] << 16), jnp.int32)


def _sc_peer(table, idx, aux, phase):
    N = idx.shape[0]
    tpw = N // SC_WORKERS
    assert tpw % 2 == 0 and N % SC_WORKERS == 0
    assert PEER_GATHERS % PEER_ROW_BUFFERS == 0
    mesh = plsc.VectorSubcoreMesh(core_axis_name="c", subcore_axis_name="s")
    aux_shape = (D_MODEL,) if phase == "dot" else (PEER_SLOTS,)
    out_tok = (PEER_SLOTS,) if phase == "dot" else (D_MODEL,)
    NBUF = PEER_ROW_BUFFERS
    AHEAD = NBUF - 1
    HW = PEER_ROW_WORDS

    @functools.partial(
        pl.kernel, mesh=mesh,
        out_type=jax.ShapeDtypeStruct((N,) + out_tok, jnp.float32),
        compiler_params=pltpu.CompilerParams(needs_layout_passes=False),
        cost_estimate=pl.CostEstimate(flops=2 * N * PEER_SLOTS * D_MODEL, transcendentals=0,
                                      bytes_accessed=N * PEER_SLOTS * HW * 4 + N * (D_MODEL + 2 * PEER_SLOTS) * 4),
        scratch_types=[
            pltpu.VMEM((2, PEER_GATHERS, PEER_GATHER_ROWS), jnp.int32),
            pltpu.VMEM((2,) + aux_shape, jnp.float32),
            pltpu.VMEM((NBUF, PEER_GATHER_ROWS, HW), jnp.int32),
            pltpu.VMEM((2,) + out_tok, jnp.float32),
            pltpu.SemaphoreType.DMA((NBUF,)),
            pltpu.SemaphoreType.DMA((2,)),
            pltpu.SemaphoreType.DMA((2,)),
        ],
    )
    def k(table_hbm, idx_hbm, aux_hbm, out_hbm, idx_v, aux_v, rows_v, out_v, sem_r, sem_i, sem_o):
        wid = lax.axis_index("s") * SC_CORES + lax.axis_index("c")
        base = wid * tpw

        def gather(p, c, b):
            return pltpu.make_async_copy(table_hbm.at[idx_v.at[p, c]], rows_v.at[b], sem_r.at[b])

        def load_meta(t, p):
            return (pltpu.make_async_copy(idx_hbm.at[t], idx_v.at[p], sem_i.at[p]),
                    pltpu.make_async_copy(aux_hbm.at[t], aux_v.at[p], sem_i.at[p]))

        def store_out(t, p):
            return pltpu.make_async_copy(out_v.at[p], out_hbm.at[t], sem_o.at[p])

        def halves(word):
            return (plsc.bitcast(lax.shift_left(word, 16), jnp.float32), plsc.bitcast(word & HI_MASK, jnp.float32))

        def compute(p, c, b):
            if phase == "dot":
                lane = lax.iota(jnp.int32, SC_LANES)
                vec = jnp.zeros((SC_LANES,), jnp.float32)
                groups_per_vec = SC_LANES // PEER_ACC_VREGS
                for g in range(PEER_GATHER_ROWS // PEER_ACC_VREGS):
                    def body(cc, accs):
                        x_lo = aux_v[p, pl.ds(cc * SC_LANES, SC_LANES)]
                        x_hi = aux_v[p, pl.ds(HW + cc * SC_LANES, SC_LANES)]
                        out = []
                        for r in range(PEER_ACC_VREGS):
                            lo, hi = halves(rows_v[b, g * PEER_ACC_VREGS + r, pl.ds(cc * SC_LANES, SC_LANES)])
                            out.append(accs[r] + lo * x_lo + hi * x_hi)
                        return tuple(out)
                    accs = lax.fori_loop(0, HW // SC_LANES, body,
                                         tuple(jnp.zeros((SC_LANES,), jnp.float32) for _ in range(PEER_ACC_VREGS)))
                    for r in range(PEER_ACC_VREGS):
                        vec = jnp.where(lane == (g % groups_per_vec) * PEER_ACC_VREGS + r, jnp.sum(accs[r]), vec)
                    if g % groups_per_vec == groups_per_vec - 1:
                        out_v[p, pl.ds(c * PEER_GATHER_ROWS + (g // groups_per_vec) * SC_LANES, SC_LANES)] = vec
            else:
                words = PEER_ACC_VREGS // 2
                for db in range(HW // (words * SC_LANES)):
                    def body(kk, accs):
                        wv = plsc.load_gather(aux_v.at[p], [jnp.full((SC_LANES,), c * PEER_GATHER_ROWS + kk, jnp.int32)])
                        out = []
                        for j in range(words):
                            lo, hi = halves(rows_v[b, kk, pl.ds((db * words + j) * SC_LANES, SC_LANES)])
                            out += [accs[2 * j] + lo * wv, accs[2 * j + 1] + hi * wv]
                        return tuple(out)
                    if c == 0:
                        init = tuple(jnp.zeros((SC_LANES,), jnp.float32) for _ in range(2 * words))
                    else:
                        init = tuple(out_v[p, pl.ds(half * HW + (db * words + j) * SC_LANES, SC_LANES)]
                                     for j in range(words) for half in range(2))
                    accs = lax.fori_loop(0, PEER_GATHER_ROWS, body, init)
                    for j in range(words):
                        out_v[p, pl.ds((db * words + j) * SC_LANES, SC_LANES)] = accs[2 * j]
                        out_v[p, pl.ds(HW + (db * words + j) * SC_LANES, SC_LANES)] = accs[2 * j + 1]

        for d in load_meta(base, 0):
            d.start()
        for d in load_meta(base, 0):
            d.wait()
        for c in range(AHEAD):
            gather(0, c, c % NBUF).start()

        @pl.loop(0, tpw // 2)
        def _(i2):
            for p in range(2):
                i = i2 * 2 + p
                t = base + i
                nxt = base + jnp.minimum(i + 1, tpw - 1)
                for d in load_meta(nxt, 1 - p):
                    d.start()

                @pl.when(i2 > 0)
                def _():
                    store_out(t, p).wait()

                for c in range(PEER_GATHERS):
                    ahead = c + AHEAD
                    if ahead < PEER_GATHERS:
                        gather(p, ahead, ahead % NBUF).start()
                    else:
                        if ahead == PEER_GATHERS:
                            for d in load_meta(nxt, 1 - p):
                                d.wait()
                        gather(1 - p, ahead - PEER_GATHERS, ahead % NBUF).start()
                    gather(p, c, c % NBUF).wait()
                    compute(p, c, c % NBUF)
                store_out(t, p).start()

        for c in range(AHEAD):
            gather(0, c, c % NBUF).wait()
        for p in range(2):
            store_out(base, p).wait()

    return k(table, idx.reshape(N, PEER_GATHERS, PEER_GATHER_ROWS), aux)


PEER_TOKENS = 256
INT_BIG = 2 ** 30


def _extract_topk(cand_ref, ids_ref, val_out_ref, id_out_ref, row0):
    def body(r, carry):
        c = cand_ref[...]
        ids = ids_ref[...]
        m = jnp.max(c, axis=0, keepdims=True)
        sel = jnp.min(jnp.where(c == m, ids, INT_BIG), axis=0, keepdims=True)
        cand_ref[...] = jnp.where(ids == sel, -jnp.inf, c)
        val_out_ref[pl.ds(row0 + r, 1), :] = m
        id_out_ref[pl.ds(row0 + r, 1), :] = sel
        return carry
    lax.fori_loop(0, PEER_TOPK, body, 0)


def _peer_retrieve_kernel(x_ref, gain_ref, scale_ref, shift_ref, wq_ref, keys_ref,
                          h_ref, idx_out_ref, gate_out_ref,
                          s_ref, ids1_ref, sv_ref, si_ref, cand_ref, cid_ref, ts_ref, idx_ref, gate_ref):
    x = x_ref[0]
    y = x * lax.rsqrt(jnp.mean(x * x, axis=-1, keepdims=True) + NORM_EPS)
    h = (y * gain_ref[...]) * (1.0 + scale_ref[0]) + shift_ref[0]
    h_ref[0] = h
    q = jnp.dot(h.astype(jnp.bfloat16), wq_ref[...], preferred_element_type=jnp.float32)
    T = PEER_TOKENS
    K = PEER_TOPK
    ids1_ref[...] = lax.broadcasted_iota(jnp.int32, (PEER_N_KEYS, T), 0)
    for hd in range(PEER_HEADS):
        for p in range(2):
            hp = hd * 2 + p
            qs = q[:, hp * PEER_HALF:(hp + 1) * PEER_HALF].astype(jnp.bfloat16)
            s_ref[...] = lax.dot_general(keys_ref[hp], qs, (((1,), (1,)), ((), ())),
                                         preferred_element_type=jnp.float32)
            _extract_topk(s_ref, ids1_ref, sv_ref, si_ref, p * K)
        for i in range(K):
            cand_ref[i * K:(i + 1) * K, :] = sv_ref[i:i + 1, :] + sv_ref[K:2 * K, :]
            cid_ref[i * K:(i + 1) * K, :] = si_ref[i:i + 1, :] * PEER_N_KEYS + si_ref[K:2 * K, :]
        _extract_topk(cand_ref, cid_ref, ts_ref, idx_ref, hd * K)
        ts = ts_ref[hd * K:(hd + 1) * K, :]
        e = jnp.exp(ts - jnp.max(ts, axis=0, keepdims=True))
        gate_ref[hd * K:(hd + 1) * K, :] = e / jnp.sum(e, axis=0, keepdims=True)
    idx_out_ref[...] = idx_ref[...].T
    gate_out_ref[...] = gate_ref[...].T


def peer_retrieve(x, gain, scale, shift, w_q, sub_keys):
    B, L, D = x.shape
    T = PEER_TOKENS
    nt = L // T
    keys = sub_keys.reshape(PEER_HEADS * 2, PEER_N_KEYS, PEER_HALF).astype(jnp.bfloat16)
    return pl.pallas_call(
        _peer_retrieve_kernel,
        grid=(B, nt),
        in_specs=[
            pl.BlockSpec((1, T, D), lambda b, i: (b, i, 0)),
            pl.BlockSpec((1, D), lambda b, i: (0, 0)),
            pl.BlockSpec((1, 1, D), lambda b, i: (b, 0, 0)),
            pl.BlockSpec((1, 1, D), lambda b, i: (b, 0, 0)),
            pl.BlockSpec((D, PEER_HEADS * 2 * PEER_HALF), lambda b, i: (0, 0)),
            pl.BlockSpec((PEER_HEADS * 2, PEER_N_KEYS, PEER_HALF), lambda b, i: (0, 0, 0)),
        ],
        out_specs=[
            pl.BlockSpec((1, T, D), lambda b, i: (b, i, 0)),
            pl.BlockSpec((T, PEER_SLOTS), lambda b, i: (b * nt + i, 0)),
            pl.BlockSpec((T, PEER_SLOTS), lambda b, i: (b * nt + i, 0)),
        ],
        out_shape=[
            jax.ShapeDtypeStruct((B, L, D), jnp.float32),
            jax.ShapeDtypeStruct((B * L, PEER_SLOTS), jnp.int32),
            jax.ShapeDtypeStruct((B * L, PEER_SLOTS), jnp.float32),
        ],
        scratch_shapes=[
            pltpu.VMEM((PEER_N_KEYS, T), jnp.float32),
            pltpu.VMEM((PEER_N_KEYS, T), jnp.int32),
            pltpu.VMEM((2 * PEER_TOPK, T), jnp.float32),
            pltpu.VMEM((2 * PEER_TOPK, T), jnp.int32),
            pltpu.VMEM((PEER_TOPK * PEER_TOPK, T), jnp.float32),
            pltpu.VMEM((PEER_TOPK * PEER_TOPK, T), jnp.int32),
            pltpu.VMEM((PEER_SLOTS, T), jnp.float32),
            pltpu.VMEM((PEER_SLOTS, T), jnp.int32),
            pltpu.VMEM((PEER_SLOTS, T), jnp.float32),
        ],
        compiler_params=pltpu.CompilerParams(dimension_semantics=("parallel", "parallel"),
                                             vmem_limit_bytes=VMEM_LIMIT_BYTES),
        name="peer_retrieve",
    )(x, gain.reshape(1, D), scale, shift, w_q.astype(jnp.bfloat16), keys)


PEER_ACT_ROWS = 1024


def _peer_act_kernel(dots_ref, gate_ref, w_ref):
    a = dots_ref[...]
    w_ref[...] = gate_ref[...] * (0.5 * a * (1.0 + lax.erf(a * (2.0 ** -0.5))))


def peer_act(dots, gate):
    N = dots.shape[0]
    T = min(PEER_ACT_ROWS, N)
    spec = pl.BlockSpec((T, PEER_SLOTS), lambda i: (i, 0))
    return pl.pallas_call(
        _peer_act_kernel,
        grid=(N // T,),
        in_specs=[spec, spec],
        out_specs=spec,
        out_shape=jax.ShapeDtypeStruct((N, PEER_SLOTS), jnp.float32),
        compiler_params=pltpu.CompilerParams(dimension_semantics=("parallel",)),
        name="peer_act",
    )(dots, gate)


def peer_ffn(x, gain, scale, shift, w_q, sub_keys, exp_u, exp_v):
    B, L, D = x.shape
    N = B * L
    h, e_idx, gate = peer_retrieve(x, gain, scale, shift, w_q, sub_keys)
    dots = _sc_peer(exp_u, e_idx, h.reshape(N, D), "dot")
    return _sc_peer(exp_v, e_idx, peer_act(dots, gate), "wsum").reshape(B, L, D)


def _mix_and_retrieve(li, x, c, ctx, c_ctx, mod_w, mod_b, mix_norm, w_in, w_out, rw_conv, rw_decay_up, rw_decay0, rw_a_up, rw_a0, rw_gate_up, rw_k_k, rw_k_a, rw_r_k, rw_gn_g, rw_gn_b, mla_q_norm, mla_w_uq, mla_kv_norm, mla_w_ukv, mla_q_gain, mla_k_gain, hy_conv, hy_w1, hy_b1, hy_freq1, hy_w2, hy_b2, hy_freq2, hy_w3, hy_b3, hy_bias, ffn_norm, peer_wq, peer_keys, peer_u, peer_v):
    B, L, D = x.shape
    s_rw, s_mla = RW_PROJ, RW_PROJ + MLA_PROJ
    need_ctx = li < DEPTH - 1
    mod_l = (jax.nn.silu(c) @ mod_w[li] + mod_b[li])[:, None, :]
    mod_c = (jax.nn.silu(c_ctx) @ mod_w[li] + mod_b[li])[None, None, :]
    shm_l, scm_l, gm_l, shf_l, scf_l, gf_l = jnp.split(mod_l, N_MOD, axis=-1)
    shm_c, scm_c, gm_c, shf_c, scf_c, gf_c = jnp.split(mod_c, N_MOD, axis=-1)

    p_l = norm_mod_proj(x, mix_norm[li], scm_l, shm_l, w_in[li], 512)
    p_c = norm_mod_proj(ctx, mix_norm[li], jnp.broadcast_to(scm_c, (B, 1, D)),
                        jnp.broadcast_to(shm_c, (B, 1, D)), w_in[li], 256)
    rw_l, rw_c = rwkv7_mixer(p_l[..., :s_rw], p_c[..., :s_rw], rw_conv[li], rw_decay_up[li], rw_decay0[li],
                             rw_a_up[li], rw_a0[li], rw_gate_up[li], rw_k_k[li], rw_k_a[li], rw_r_k[li],
                             rw_gn_g[li], rw_gn_b[li], need_ctx)
    ml_l, ml_c = mla_mixer(p_l[..., s_rw:s_mla], p_c[..., s_rw:s_mla], mla_q_norm[li], mla_w_uq[li],
                           mla_kv_norm[li], mla_w_ukv[li], mla_q_gain[li], mla_k_gain[li], need_ctx)
    hy_prm = (hy_conv[li], hy_w1[li], hy_b1[li], hy_freq1[li], hy_w2[li], hy_b2[li], hy_freq2[li],
              hy_w3[li], hy_b3[li], hy_bias[li])
    hy_l = hyena_mixer(p_l[..., s_mla:], *hy_prm)
    x = x + gm_l * (jnp.concatenate([rw_l, ml_l, hy_l], axis=-1) @ w_out[li])
    if need_ctx:
        hy_c = hyena_mixer(p_c[..., s_mla:], *hy_prm)
        ctx = ctx + gm_c * (jnp.concatenate([rw_c, ml_c, hy_c], axis=-1) @ w_out[li])
        ctx = ctx + gf_c * peer_ffn(ctx, ffn_norm[li], jnp.broadcast_to(scf_c, (B, 1, D)),
                                    jnp.broadcast_to(shf_c, (B, 1, D)),
                                    peer_wq[li], peer_keys[li], peer_u[li], peer_v[li])
    h, e_idx, gate = peer_retrieve(x, ffn_norm[li], scf_l, shf_l, peer_wq[li], peer_keys[li])
    return x, ctx, gf_l, h.reshape(B * L, D), e_idx, gate


BATCH_GROUPS = 4


def kernel(x, c, ctx, c_ctx, mod_w, mod_b, mix_norm, w_in, w_out, rw_conv, rw_decay_up, rw_decay0, rw_a_up, rw_a0, rw_gate_up, rw_k_k, rw_k_a, rw_r_k, rw_gn_g, rw_gn_b, mla_q_norm, mla_w_uq, mla_kv_norm, mla_w_ukv, mla_q_gain, mla_k_gain, hy_conv, hy_w1, hy_b1, hy_freq1, hy_w2, hy_b2, hy_freq2, hy_w3, hy_b3, hy_bias, ffn_norm, peer_wq, peer_keys, peer_u, peer_v):
    params = (mod_w, mod_b, mix_norm, w_in, w_out, rw_conv, rw_decay_up, rw_decay0, rw_a_up, rw_a0, rw_gate_up,
              rw_k_k, rw_k_a, rw_r_k, rw_gn_g, rw_gn_b, mla_q_norm, mla_w_uq, mla_kv_norm, mla_w_ukv, mla_q_gain,
              mla_k_gain, hy_conv, hy_w1, hy_b1, hy_freq1, hy_w2, hy_b2, hy_freq2, hy_w3, hy_b3, hy_bias,
              ffn_norm, peer_wq, peer_keys)
    peer_u = [pack_expert_table(peer_u[li]) for li in range(DEPTH)]
    peer_v = [pack_expert_table(peer_v[li]) for li in range(DEPTH)]
    params = params + (peer_u, peer_v)
    G = BATCH_GROUPS
    bg = x.shape[0] // G
    L, D = x.shape[1:]
    xs = [x[g * bg:(g + 1) * bg] for g in range(G)]
    cs = [c[g * bg:(g + 1) * bg] for g in range(G)]
    ctxs = [ctx[g * bg:(g + 1) * bg] for g in range(G)]
    stages = [(li, g) for li in range(DEPTH) for g in range(G)]
    pending = {}
    token = None
    for k in range(len(stages) + 1):
        if k < len(stages):
            li, g = stages[k]
            ins = (xs[g], ctxs[g])
            if token is not None:
                token, ins = lax.optimization_barrier((token, ins))
            xm, ctxs[g], gf, h, e_idx, gate = _mix_and_retrieve(li, ins[0], cs[g], ins[1], c_ctx, *params)
            dots = _sc_peer(peer_u[li], e_idx, h, "dot")
            pending[k] = (xm, gf, e_idx, gate, dots)
            token = gate
        if k >= 1:
            li, g = stages[k - 1]
            xm, gf, e_idx, gate, dots = pending.pop(k - 1)
            token, (dots, gate) = lax.optimization_barrier((token, (dots, gate)))
            w = peer_act(dots, gate)
            xs[g] = xm + gf * _sc_peer(peer_v[li], e_idx, w, "wsum").reshape(bg, L, D)
            token = w
    return jnp.concatenate(xs, axis=0)
```

```python
import functools
import math

import jax
import jax.numpy as jnp
import numpy as np
from jax import lax
from jax.experimental import pallas as pl
from jax.experimental.pallas import tpu as pltpu
from jax.experimental.pallas import tpu_sc as plsc

D_MODEL = 1024
DEPTH = 2
GRID_W = 64
N_MOD = 6
NORM_EPS = 1e-6

RW_HEADS = 6
RW_HEAD_DIM = 64
RW_WIDTH = RW_HEADS * RW_HEAD_DIM
RW_DECAY_RANK = 64
RW_A_RANK = 64
RW_GATE_RANK = 128
RW_DECAY_SCALE = 0.6065306597
RW_GN_EPS = 64e-5
L2_EPS = 1e-12

MLA_HEADS = 6
MLA_Q_RANK = 256
MLA_KV_RANK = 128
MLA_NOPE_DIM = 64
MLA_ROPE_DIM = 32
MLA_V_DIM = 64
MLA_QK_DIM = MLA_NOPE_DIM + MLA_ROPE_DIM
MLA_WIDTH = MLA_HEADS * MLA_V_DIM
AXIS_ROPE_DIM = MLA_ROPE_DIM // 2
ROPE_THETA = 10000.0

HY_WIDTH = 256
HY_ORDER = 2
HY_POS_BANDS = 16
HY_SHORT_DECAY_PCT = 0.3
HY_LONG_DECAY_PCT = 1.5
HY_DECAY_TARGET = 1e-2

PEER_HEADS = 8
PEER_N_KEYS = 128
PEER_TOPK = 16
PEER_QUERY_DIM = 256
PEER_HALF = PEER_QUERY_DIM // 2

RW_PROJ = 3 * RW_WIDTH + RW_DECAY_RANK + RW_A_RANK + RW_GATE_RANK
MLA_PROJ = MLA_Q_RANK + MLA_KV_RANK + MLA_ROPE_DIM
HY_PROJ = (HY_ORDER + 1) * HY_WIDTH
IN_PROJ = RW_PROJ + MLA_PROJ + HY_PROJ
MIX_WIDTH = RW_WIDTH + MLA_WIDTH + HY_WIDTH

VMEM_LIMIT_BYTES = 48 * 1024 * 1024


def _norm_mod_proj_kernel(x_ref, gain_ref, scale_ref, shift_ref, w_ref, o_ref):
    x = x_ref[0]
    y = x * lax.rsqrt(jnp.mean(x * x, axis=-1, keepdims=True) + NORM_EPS)
    y = y * gain_ref[...]
    y = y * (1.0 + scale_ref[0]) + shift_ref[0]
    o_ref[0] = jnp.dot(y.astype(jnp.bfloat16), w_ref[...], preferred_element_type=jnp.float32)


def norm_mod_proj(x, gain, scale, shift, w, block_rows):
    B, L, D = x.shape
    N = w.shape[1]
    return pl.pallas_call(
        _norm_mod_proj_kernel,
        grid=(B, L // block_rows),
        in_specs=[
            pl.BlockSpec((1, block_rows, D), lambda b, i: (b, i, 0)),
            pl.BlockSpec((1, D), lambda b, i: (0, 0)),
            pl.BlockSpec((1, 1, D), lambda b, i: (b, 0, 0)),
            pl.BlockSpec((1, 1, D), lambda b, i: (b, 0, 0)),
            pl.BlockSpec((D, N), lambda b, i: (0, 0)),
        ],
        out_specs=pl.BlockSpec((1, block_rows, N), lambda b, i: (b, i, 0)),
        out_shape=jax.ShapeDtypeStruct((B, L, N), jnp.float32),
        compiler_params=pltpu.CompilerParams(
            dimension_semantics=("parallel", "parallel"), vmem_limit_bytes=VMEM_LIMIT_BYTES),
        name="norm_mod_proj",
    )(x, gain.reshape(1, D), scale, shift, w.astype(jnp.bfloat16))


RW_CHUNK = 64


def _rwkv_chunk_kernel(r_ref, kk_ref, v_ref, lw_ref, akk_ref, kr_ref, y_ref, h_ref):
    d = pl.program_id(0)
    n = pl.program_id(2)

    @pl.when(n == 0)
    def _():
        h_ref[...] = jnp.zeros_like(h_ref)

    C = RW_CHUNK
    row = lax.broadcasted_iota(jnp.int32, (C, C), 0)
    col = lax.broadcasted_iota(jnp.int32, (C, C), 1)
    lag = (row - col) * (1 - 2 * d)
    before = lag > 0
    upto = lag >= 0
    tri = upto.astype(jnp.float32)
    eye = (row == col).astype(jnp.float32)
    bf = jnp.bfloat16
    f32 = jnp.float32

    def mm(a, b):
        return jnp.dot(a.astype(bf), b.astype(bf), preferred_element_type=f32)

    def mm_nt(a, b):
        return lax.dot_general(a.astype(bf), b.astype(bf), (((1,), (1,)), ((), ())), preferred_element_type=f32)

    def mm_tn(a, b):
        return lax.dot_general(a.astype(bf), b.astype(bf), (((0,), (0,)), ((), ())), preferred_element_type=f32)

    hs = range(RW_HEADS)
    HD = RW_HEAD_DIM
    heads = lambda t: [t[:, h * HD:(h + 1) * HD] for h in hs]
    r = heads(r_ref[0])
    kk = heads(kk_ref[0])
    v = heads(v_ref[0])
    lw = heads(lw_ref[0, 0])
    akk = heads(akk_ref[0, 0])
    kr = heads(kr_ref[0, 0])
    G = [jnp.dot(tri, lw[h], preferred_element_type=f32, precision=lax.Precision.HIGHEST) for h in hs]
    gtot = [jnp.sum(lw[h], axis=0, keepdims=True) for h in hs]
    Einv = [jnp.exp(-G[h]) for h in hs]
    At = [-kk[h] * jnp.exp(G[h] - lw[h]) for h in hs]
    Rt = [r[h] * jnp.exp(G[h]) for h in hs]
    Bt = [akk[h] * Einv[h] for h in hs]
    Kt = [kr[h] * Einv[h] for h in hs]
    X = [mm_nt(jnp.concatenate([At[h], Rt[h]], axis=0), jnp.concatenate([Bt[h], Kt[h]], axis=0)) for h in hs]
    M_ab = [jnp.where(before, X[h][:C, :C], 0.0) for h in hs]
    M_ak = [jnp.where(before, X[h][:C, C:], 0.0) for h in hs]
    A_rb = [jnp.where(upto, X[h][C:, :C], 0.0) for h in hs]
    A_rk = [jnp.where(upto, X[h][C:, C:], 0.0) for h in hs]
    MV = [mm(M_ak[h], v[h]) for h in hs]
    Mp = M_ab
    T = [eye + Mp[h] for h in hs]
    for _ in range(5):
        Mp = [jnp.dot(Mp[h], Mp[h], preferred_element_type=f32) for h in hs]
        T = [T[h] + jnp.dot(T[h], Mp[h], preferred_element_type=f32) for h in hs]
    WU = [jnp.dot(T[h], jnp.concatenate([At[h], MV[h]], axis=1), preferred_element_type=f32) for h in hs]
    H0 = [h_ref[h] for h in hs]
    Ehat = [jnp.exp(gtot[h] - G[h]) for h in hs]
    Om = [Rt[h] + mm(A_rb[h], WU[h][:, :HD]) for h in hs]
    Y0 = [mm(A_rb[h], WU[h][:, HD:]) + mm(A_rk[h], v[h]) for h in hs]
    BW = [mm_tn(akk[h] * Ehat[h], WU[h]) for h in hs]
    KV = [mm_tn(kr[h] * Ehat[h], v[h]) for h in hs]
    y_ref[0, 0] = jnp.concatenate([jnp.dot(Om[h], H0[h], preferred_element_type=f32) + Y0[h] for h in hs], axis=1)
    for h in hs:
        P = eye * jnp.exp(gtot[h]) + BW[h][:, :HD]
        h_ref[h] = jnp.dot(P, H0[h], preferred_element_type=f32) + BW[h][:, HD:] + KV[h]


def rwkv_chunked(r, kk, v, lw, akk, kr, n_ctx):
    B, T, W = r.shape
    H = W // RW_HEAD_DIM
    nc = n_ctx // RW_CHUNK
    nt = T // RW_CHUNK

    def chunk_of(d, n):
        bwd = jnp.where(n < nc, nc - 1 - n, nt - 1 - (n - nc))
        return jnp.where(d == 0, n, bwd)

    spec1 = pl.BlockSpec((1, RW_CHUNK, W), lambda d, b, n: (b, chunk_of(d, n), 0))
    spec2 = pl.BlockSpec((1, 1, RW_CHUNK, W), lambda d, b, n: (d, b, chunk_of(d, n), 0))
    return pl.pallas_call(
        _rwkv_chunk_kernel,
        grid=(2, B, nt),
        in_specs=[spec1, spec1, spec1, spec2, spec2, spec2],
        out_specs=spec2,
        out_shape=jax.ShapeDtypeStruct((2, B, T, W), jnp.float32),
        scratch_shapes=[pltpu.VMEM((H, RW_HEAD_DIM, RW_HEAD_DIM), jnp.float32)],
        compiler_params=pltpu.CompilerParams(dimension_semantics=("parallel", "parallel", "arbitrary")),
        name="rwkv_chunked",
    )(r, kk, v, lw, akk, kr)


def short_conv(x, w):
    xp = jnp.pad(x, ((0, 0), (1, 1), (0, 0)))
    return xp[:, :-2] * w[0] + xp[:, 1:-1] * w[1] + xp[:, 2:] * w[2]


LANE = 128
RW_PREP_ROWS = 256
MLA_PAD_WIDTH = MLA_HEADS * LANE
MLA_PREP_ROWS = 256
ATTN_Q_ROWS = 512


def _split_dot(x, m):
    hi = x.astype(jnp.bfloat16)
    lo = (x - hi.astype(jnp.float32)).astype(jnp.bfloat16)
    return (jnp.dot(hi, m, preferred_element_type=jnp.float32) + jnp.dot(lo, m, preferred_element_type=jnp.float32))


def _rwkv_prep_kernel(z_ref, wda_ref, d0_ref, a0_ref, gup_ref, kk_ref_w, ka_ref, rk_ref, hsum_ref,
                      r_ref, kk_ref, v_ref, lw_ref, akk_ref, kr_ref, g_ref, bonus_ref):
    W = RW_WIDTH
    bf = jnp.bfloat16
    z = z_ref[0]
    r, k, v = z[:, :W], z[:, W:2 * W], z[:, 2 * W:3 * W]
    da = z[:, 3 * W:3 * W + LANE]
    lane = lax.broadcasted_iota(jnp.int32, da.shape, 1)
    da = jnp.where(lane < RW_DECAY_RANK, jnp.tanh(da), da)
    up = jnp.dot(da.astype(bf), wda_ref[...], preferred_element_type=jnp.float32)
    g_lo = z[:, 3 * W + LANE:]
    g_ref[0] = jnp.dot(jax.nn.sigmoid(g_lo).astype(bf), gup_ref[...], preferred_element_type=jnp.float32)
    hsum = hsum_ref[...]
    kk = k * kk_ref_w[...]
    kk = kk * lax.rsqrt(_split_dot(kk * kk, hsum) + L2_EPS)
    r_ref[0] = r
    v_ref[0] = v
    kk_ref[0] = kk
    bonus_ref[0] = _split_dot(r * k * rk_ref[...], hsum) * v
    for d in range(2):
        lw_ref[d, 0] = -RW_DECAY_SCALE * jax.nn.sigmoid(d0_ref[d:d + 1, :] + up[:, d * W:(d + 1) * W])
        a = jax.nn.sigmoid(a0_ref[d:d + 1, :] + up[:, (2 + d) * W:(3 + d) * W])
        akk_ref[d, 0] = kk * a
        kr_ref[d, 0] = k * (1.0 + (a - 1.0) * ka_ref[...])


def rwkv_prep(z, decay_up, decay0, a_up, a0, gate_up, k_k, k_a, r_k):
    B, L, _ = z.shape
    W = RW_WIDTH
    T = min(RW_PREP_ROWS, L)
    zero = jnp.zeros((RW_DECAY_RANK, 2 * W), jnp.float32)
    wda = jnp.concatenate([
        jnp.concatenate([decay_up[0], decay_up[1], zero], axis=1),
        jnp.concatenate([zero, a_up[0], a_up[1]], axis=1)], axis=0).astype(jnp.bfloat16)
    head = jnp.arange(W) // RW_HEAD_DIM
    hsum = (head[:, None] == head[None, :]).astype(jnp.bfloat16)
    row = lambda a: a.reshape(1, W)
    const = lambda a: pl.BlockSpec(a.shape, lambda b, i: (0,) * a.ndim)
    tok = pl.BlockSpec((1, T, W), lambda b, i: (b, i, 0))
    tok2 = pl.BlockSpec((2, 1, T, W), lambda b, i: (0, b, i, 0))
    f1 = jax.ShapeDtypeStruct((B, L, W), jnp.float32)
    f2 = jax.ShapeDtypeStruct((2, B, L, W), jnp.float32)
    args = (z, wda, decay0, a0, gate_up.astype(jnp.bfloat16), row(k_k), row(k_a), row(r_k), hsum)
    return pl.pallas_call(
        _rwkv_prep_kernel,
        grid=(B, L // T),
        in_specs=[pl.BlockSpec((1, T, RW_PROJ), lambda b, i: (b, i, 0))] + [const(a) for a in args[1:]],
        out_specs=[tok, tok, tok, tok2, tok2, tok2, tok, tok],
        out_shape=[f1, f1, f1, f2, f2, f2, f1, f1],
        compiler_params=pltpu.CompilerParams(dimension_semantics=("parallel", "parallel"),
                                             vmem_limit_bytes=VMEM_LIMIT_BYTES),
        name="rwkv_prep",
    )(*args)


def _rwkv_readout_kernel(y_ref, g_ref, bonus_ref, gng_ref, gnb_ref, hsum_ref, o_ref):
    y = y_ref[0, 0] + y_ref[1, 0]
    hsum = hsum_ref[...]
    mu = _split_dot(y, hsum) * (1.0 / RW_HEAD_DIM)
    d = y - mu
    var = _split_dot(d * d, hsum) * (1.0 / RW_HEAD_DIM)
    yn = d * lax.rsqrt(var + RW_GN_EPS) * gng_ref[...] + gnb_ref[...]
    o_ref[0] = (yn + bonus_ref[0]) * g_ref[0]


def rwkv_readout(y, g, bonus, gn_g, gn_b, t0):
    B, L, W = g.shape
    T = min(RW_PREP_ROWS, L)
    off = t0 // T
    head = jnp.arange(W) // RW_HEAD_DIM
    hsum = (head[:, None] == head[None, :]).astype(jnp.bfloat16)
    tok = pl.BlockSpec((1, T, W), lambda b, i: (b, i, 0))
    const = lambda a: pl.BlockSpec(a.shape, lambda b, i: (0,) * a.ndim)
    gg, gb = gn_g.reshape(1, W), gn_b.reshape(1, W)
    return pl.pallas_call(
        _rwkv_readout_kernel,
        grid=(B, L // T),
        in_specs=[pl.BlockSpec((2, 1, T, W), lambda b, i: (0, b, i + off, 0)), tok, tok, const(gg), const(gb), const(hsum)],
        out_specs=tok,
        out_shape=jax.ShapeDtypeStruct((B, L, W), jnp.float32),
        compiler_params=pltpu.CompilerParams(dimension_semantics=("parallel", "parallel")),
        name="rwkv_readout",
    )(y, g, bonus, gg, gb, hsum)


def rwkv7_mixer(p_lat, p_ctx, conv_w, decay_up, decay0, a_up, a0, gate_up, k_k, k_a, r_k, gn_g, gn_b, need_ctx):
    prm = (decay_up, decay0, a_up, a0, gate_up, k_k, k_a, r_k)
    lat = rwkv_prep(short_conv(p_lat, conv_w), *prm)
    ctx = rwkv_prep(short_conv(p_ctx, conv_w), *prm)
    n_ctx = p_ctx.shape[1]
    seq = lambda i: jnp.concatenate([ctx[i], lat[i]], axis=-2)
    y = rwkv_chunked(seq(0), seq(1), seq(2), seq(3), seq(4), seq(5), n_ctx)
    out_l = rwkv_readout(y, lat[6], lat[7], gn_g, gn_b, n_ctx)
    out_c = rwkv_readout(y, ctx[6], ctx[7], gn_g, gn_b, 0) if need_ctx else None
    return out_l, out_c


def _rope_tables(L, use_rope):
    lane = np.arange(LANE)
    in_rope = (lane >= MLA_NOPE_DIM) & (lane < MLA_QK_DIM)
    j = lane - MLA_NOPE_DIM
    axis = j // AXIS_ROPE_DIM
    half = AXIS_ROPE_DIM // 2
    f = j % half
    first = (j % AXIS_ROPE_DIM) < half
    inv = ROPE_THETA ** (-jnp.arange(0, AXIS_ROPE_DIM, 2, dtype=jnp.float32) / AXIS_ROPE_DIM)
    t = jnp.arange(L)
    pos = jnp.stack([t // GRID_W, t % GRID_W], axis=-1).astype(jnp.float32)
    ang = pos[:, np.clip(axis, 0, 1)] * inv[np.clip(f, 0, half - 1)][None, :]
    rope_on = jnp.asarray(in_rope)[None, :] & use_rope
    cos = jnp.where(rope_on, jnp.cos(ang), 1.0)
    sin = jnp.where(rope_on, jnp.sin(ang) * jnp.where(jnp.asarray(first), -1.0, 1.0)[None, :], 0.0)
    return jnp.tile(cos, (1, MLA_HEADS)), jnp.tile(sin, (1, MLA_HEADS))


def _mla_prep_kernel(p_ref, qn_ref, wq_ref, kvn_ref, wk_ref, wv_ref, place_ref, qg_ref, kg_ref, hsum_ref, cos_ref, sin_ref,
                     q_ref, k_ref, v_ref):
    bf = jnp.bfloat16
    p = p_ref[0]
    c_q = p[:, :MLA_Q_RANK]
    c_kv = p[:, MLA_Q_RANK:MLA_Q_RANK + MLA_KV_RANK]
    tail = p[:, MLA_Q_RANK + MLA_KV_RANK:]
    cqn = c_q * lax.rsqrt(jnp.mean(c_q * c_q, axis=-1, keepdims=True) + NORM_EPS) * qn_ref[...]
    ckn = c_kv * lax.rsqrt(jnp.mean(c_kv * c_kv, axis=-1, keepdims=True) + NORM_EPS) * kvn_ref[...]
    q = jnp.dot(cqn.astype(bf), wq_ref[...], preferred_element_type=jnp.float32)
    k = jnp.dot(ckn.astype(bf), wk_ref[...], preferred_element_type=jnp.float32) + _split_dot(tail, place_ref[...])
    v_ref[0] = jnp.dot(ckn.astype(bf), wv_ref[...], preferred_element_type=jnp.float32).astype(bf)
    hsum = hsum_ref[...]
    cos, sin = cos_ref[...], sin_ref[...]
    lane = lax.broadcasted_iota(jnp.int32, q.shape, 1)
    first = ((lane - MLA_NOPE_DIM) % AXIS_ROPE_DIM) < (AXIS_ROPE_DIM // 2)
    half = AXIS_ROPE_DIM // 2

    def finish(x, gain):
        x = x * lax.rsqrt(_split_dot(x * x, hsum) * (1.0 / MLA_QK_DIM) + NORM_EPS) * gain
        partner = jnp.where(first, pltpu.roll(x, MLA_PAD_WIDTH - half, 1), pltpu.roll(x, half, 1))
        return x * cos + partner * sin

    q_ref[0] = (finish(q, qg_ref[...]) * (MLA_QK_DIM ** -0.5)).astype(bf)
    k_ref[0] = finish(k, kg_ref[...]).astype(bf)


def mla_prep(p, use_rope, q_norm, w_uq, kv_norm, w_ukv, q_gain, k_gain):
    B, L, _ = p.shape
    T = min(MLA_PREP_ROWS, L)
    H = MLA_HEADS
    pad_cols = lambda w, d: jnp.pad(w.reshape(w.shape[0], H, d), ((0, 0), (0, 0), (0, LANE - d))).reshape(w.shape[0], H * LANE)
    wq = pad_cols(w_uq, MLA_QK_DIM).astype(jnp.bfloat16)
    ukv = w_ukv.reshape(MLA_KV_RANK, H, MLA_NOPE_DIM + MLA_V_DIM)
    wk = pad_cols(ukv[:, :, :MLA_NOPE_DIM].reshape(MLA_KV_RANK, H * MLA_NOPE_DIM), MLA_NOPE_DIM).astype(jnp.bfloat16)
    wv = ukv[:, :, MLA_NOPE_DIM:].reshape(MLA_KV_RANK, H * MLA_V_DIM).astype(jnp.bfloat16)
    lane = np.arange(H * LANE)
    place = jnp.asarray(((lane[None, :] % LANE) - MLA_NOPE_DIM == np.arange(MLA_ROPE_DIM)[:, None]), jnp.bfloat16)
    hsum = jnp.asarray((lane[:, None] // LANE) == (lane[None, :] // LANE), jnp.bfloat16)
    pad_gain = lambda g: jnp.tile(jnp.pad(g, (0, LANE - MLA_QK_DIM)), H).reshape(1, H * LANE)
    cos, sin = _rope_tables(L, use_rope)
    const = lambda a: pl.BlockSpec(a.shape, lambda b, i: (0,) * a.ndim)
    args = (p, q_norm.reshape(1, -1), wq, kv_norm.reshape(1, -1), wk, wv, place, pad_gain(q_gain), pad_gain(k_gain), hsum)
    pos = pl.BlockSpec((T, H * LANE), lambda b, i: (i, 0))
    return pl.pallas_call(
        _mla_prep_kernel,
        grid=(B, L // T),
        in_specs=[pl.BlockSpec((1, T, MLA_PROJ), lambda b, i: (b, i, 0))] + [const(a) for a in args[1:]] + [pos, pos],
        out_specs=[pl.BlockSpec((1, T, H * LANE), lambda b, i: (b, i, 0)), pl.BlockSpec((1, T, H * LANE), lambda b, i: (b, i, 0)),
                   pl.BlockSpec((1, T, MLA_WIDTH), lambda b, i: (b, i, 0))],
        out_shape=[jax.ShapeDtypeStruct((B, L, H * LANE), jnp.bfloat16), jax.ShapeDtypeStruct((B, L, H * LANE), jnp.bfloat16),
                   jax.ShapeDtypeStruct((B, L, MLA_WIDTH), jnp.bfloat16)],
        compiler_params=pltpu.CompilerParams(dimension_semantics=("parallel", "parallel"),
                                             vmem_limit_bytes=VMEM_LIMIT_BYTES),
        name="mla_prep",
    )(*args, cos, sin)


def _attn_kernel(q_ref, k_ref, v_ref, o_ref):
    lane = lax.broadcasted_iota(jnp.int32, (q_ref.shape[1], LANE), 1)
    for pair in range(MLA_HEADS // 2):
        v_pair = v_ref[0, :, pair * LANE:(pair + 1) * LANE]
        outs = []
        for h in (2 * pair, 2 * pair + 1):
            q = q_ref[0, :, h * LANE:(h + 1) * LANE]
            k = k_ref[0, :, h * LANE:(h + 1) * LANE]
            s = lax.dot_general(q, k, (((1,), (1,)), ((), ())), preferred_element_type=jnp.float32)
            e = jnp.exp(s - jnp.max(s, axis=-1, keepdims=True))
            o = jnp.dot(e.astype(jnp.bfloat16), v_pair, preferred_element_type=jnp.float32)
            outs.append(o / jnp.sum(e, axis=-1, keepdims=True))
        o_ref[0, :, pair * LANE:(pair + 1) * LANE] = jnp.where(lane < MLA_V_DIM, outs[0], outs[1])


def attention(q, k, v):
    B, Lq, P = q.shape
    Lk = k.shape[1]
    tq = min(ATTN_Q_ROWS, Lq)
    return pl.pallas_call(
        _attn_kernel,
        grid=(B, Lq // tq),
        in_specs=[pl.BlockSpec((1, tq, P), lambda b, i: (b, i, 0)),
                  pl.BlockSpec((1, Lk, P), lambda b, i: (b, 0, 0)),
                  pl.BlockSpec((1, Lk, MLA_WIDTH), lambda b, i: (b, 0, 0))],
        out_specs=pl.BlockSpec((1, tq, MLA_WIDTH), lambda b, i: (b, i, 0)),
        out_shape=jax.ShapeDtypeStruct((B, Lq, MLA_WIDTH), jnp.float32),
        compiler_params=pltpu.CompilerParams(dimension_semantics=("parallel", "parallel"),
                                             vmem_limit_bytes=VMEM_LIMIT_BYTES),
        name="mla_attention",
    )(q, k, v)


def mla_mixer(p_lat, p_ctx, q_norm, w_uq, kv_norm, w_ukv, q_gain, k_gain, need_ctx):
    prm = (q_norm, w_uq, kv_norm, w_ukv, q_gain, k_gain)
    q_l, k_l, v_l = mla_prep(p_lat, True, *prm)
    q_c, k_c, v_c = mla_prep(p_ctx, False, *prm)
    y_l = attention(q_l, jnp.concatenate([k_l, k_c], axis=1), jnp.concatenate([v_l, v_c], axis=1))
    y_c = attention(q_c, k_c, v_c) if need_ctx else None
    return y_l, y_c


def hyena_filters(L, w1, b1, freq1, w2, b2, freq2, w3, b3):
    tn = jnp.arange(L, dtype=jnp.float32) / L
    bands = jnp.arange(1, HY_POS_BANDS + 1, dtype=jnp.float32)
    ang = 2.0 * math.pi * tn[:, None] * bands[None, :]
    z = jnp.concatenate([tn[:, None], jnp.cos(ang), jnp.sin(ang)], axis=-1)
    h = jnp.sin(freq1 * (z @ w1 + b1))
    h = jnp.sin(freq2 * (h @ w2 + b2))
    h = (h @ w3 + b3).reshape(L, HY_ORDER, 2, HY_WIDTH)
    rates = jnp.abs(jnp.linspace(math.log(HY_DECAY_TARGET) / HY_LONG_DECAY_PCT,
                                 math.log(HY_DECAY_TARGET) / HY_SHORT_DECAY_PCT, HY_WIDTH))
    h = h * jnp.exp(-tn[:, None] * rates[None, :])[:, None, None, :]
    zero = jnp.zeros((1, HY_ORDER, HY_WIDTH), h.dtype)
    h_full = jnp.concatenate([h[:, :, 0], zero, h[:0:-1, :, 1]], axis=0)
    return h_full * lax.rsqrt(jnp.sum(jnp.square(h_full), axis=0, keepdims=True))


def fft_long_conv(u, h_full, bias):
    L = u.shape[1]
    uf = jnp.fft.rfft(u, n=2 * L, axis=1)
    hf = jnp.fft.rfft(h_full, n=2 * L, axis=0)
    y = jnp.fft.irfft(uf * hf[None], n=2 * L, axis=1)[:, :L]
    return y + u * bias


FFT_N1 = 64
FFT_N2 = 128
FFT_N = FFT_N1 * FFT_N2
HY_SEQS = 32


def _dft_tables(seqs):
    n1 = np.arange(FFT_N1)
    n2 = np.arange(FFT_N2)
    f64 = np.exp(-2j * np.pi * np.outer(n1, n1) / FFT_N1)
    f128 = np.exp(-2j * np.pi * np.outer(n2, n2) / FFT_N2)
    tw = np.exp(-2j * np.pi * np.outer(n1, n2) / FFT_N)
    half = FFT_N1 // 2
    fh = f64[:, :half]
    m1 = np.block([[fh.real, -fh.imag], [fh.imag, fh.real]])
    m1f = np.concatenate([f64.real, f64.imag], axis=0)
    m2 = np.block([[f128.real, f128.imag], [-f128.imag, f128.real]])
    m3 = np.block([[f128.real, -f128.imag], [f128.imag, f128.real]]) / FFT_N
    c = np.conj(f64)[:half, :]
    m4 = np.block([[c.real, -c.imag], [c.imag, c.real]])
    bf = lambda a: jnp.asarray(a, jnp.float32).astype(jnp.bfloat16)
    f32 = lambda a: jnp.asarray(a, jnp.float32)
    return dict(m1=bf(m1), m1f=bf(m1f), m2=bf(m2), m3=bf(m3), m4=bf(m4),
                twr_l=f32(np.tile(tw.real, (1, seqs))), twi_l=f32(np.tile(tw.imag, (1, seqs))),
                twr_s=f32(np.tile(tw.real, (seqs, 1))), twi_s=f32(np.tile(tw.imag, (seqs, 1))))


def _spectrum(cols, m1, twr_l, twi_l, m2, R):
    a = jnp.dot(m1, cols.astype(jnp.bfloat16), preferred_element_type=jnp.float32)
    ar, ai = a[:FFT_N1], a[FFT_N1:]
    pr = ar * twr_l - ai * twi_l
    pi = ar * twi_l + ai * twr_l
    lhs = jnp.concatenate(
        [jnp.concatenate([pr[:, r * FFT_N2:(r + 1) * FFT_N2], pi[:, r * FFT_N2:(r + 1) * FFT_N2]], axis=1)
         for r in range(R)], axis=0)
    return jnp.dot(lhs.astype(jnp.bfloat16), m2, preferred_element_type=jnp.float32)


def _filter_fft_kernel(h_ref, m1f_ref, twr_ref, twi_ref, m2_ref, o_ref):
    R = HY_SEQS
    cols = jnp.concatenate([h_ref[r] for r in range(R)], axis=1)
    x = _spectrum(cols, m1f_ref[...], twr_ref[...], twi_ref[...], m2_ref[...], R)
    o_ref[...] = x.reshape(R, FFT_N1, 2 * FFT_N2)


def _hyena_conv_kernel(y_ref, g_ref, hf_ref, bias_ref, m1_ref, twr_l_ref, twi_l_ref, m2_ref, m3_ref,
                       twr_s_ref, twi_s_ref, m4_ref, o_ref):
    R = HY_SEQS
    half = FFT_N1 // 2
    y = [y_ref[0], y_ref[1]]
    for o in range(HY_ORDER):
        top = jnp.concatenate([y[0][r] for r in range(R)], axis=1)
        bot = jnp.concatenate([y[1][r] for r in range(R)], axis=1)
        x = _spectrum(jnp.concatenate([top, bot], axis=0), m1_ref[...], twr_l_ref[...], twi_l_ref[...], m2_ref[...], R)
        hf = hf_ref[o].reshape(R * FFT_N1, 2 * FFT_N2)
        xr, xi = x[:, :FFT_N2], x[:, FFT_N2:]
        hr, hi = hf[:, :FFT_N2], hf[:, FFT_N2:]
        yc = jnp.concatenate([xr * hr - xi * hi, xr * hi + xi * hr], axis=1)
        b = jnp.dot(yc.astype(jnp.bfloat16), m3_ref[...], preferred_element_type=jnp.float32)
        br, bi = b[:, :FFT_N2], b[:, FFT_N2:]
        qr = br * twr_s_ref[...] + bi * twi_s_ref[...]
        qi = bi * twr_s_ref[...] - br * twi_s_ref[...]
        bc = jnp.concatenate(
            [jnp.concatenate([qr[r * FFT_N1:(r + 1) * FFT_N1], qi[r * FFT_N1:(r + 1) * FFT_N1]], axis=0)
             for r in range(R)], axis=1)
        yo = jnp.dot(m4_ref[...], bc.astype(jnp.bfloat16), preferred_element_type=jnp.float32)
        for p in range(2):
            conv = jnp.stack([yo[p * half:(p + 1) * half, r * FFT_N2:(r + 1) * FFT_N2] for r in range(R)], axis=0)
            y[p] = g_ref[o, p] * (conv + y[p] * bias_ref[o])
    o_ref[0] = y[0]
    o_ref[1] = y[1]


def hyena_long_conv(y_t, g_t, h_t, bias):
    B, C, L = y_t.shape
    assert 2 * L == FFT_N and B % 2 == 0 and C % HY_SEQS == 0
    R = HY_SEQS
    half = FFT_N1 // 2
    tb = _dft_tables(R)
    const = lambda a: pl.BlockSpec(a.shape, lambda *_: (0,) * a.ndim)
    hf = pl.pallas_call(
        _filter_fft_kernel,
        grid=(HY_ORDER * C // R,),
        in_specs=[pl.BlockSpec((R, FFT_N1, FFT_N2), lambda i: (i, 0, 0)),
                  const(tb['m1f']), const(tb['twr_l']), const(tb['twi_l']), const(tb['m2'])],
        out_specs=pl.BlockSpec((R, FFT_N1, 2 * FFT_N2), lambda i: (i, 0, 0)),
        out_shape=jax.ShapeDtypeStruct((HY_ORDER * C, FFT_N1, 2 * FFT_N2), jnp.float32),
        compiler_params=pltpu.CompilerParams(dimension_semantics=("parallel",), vmem_limit_bytes=VMEM_LIMIT_BYTES),
        name="hyena_filter_fft",
    )(h_t.reshape(HY_ORDER * C, FFT_N1, FFT_N2), tb['m1f'], tb['twr_l'], tb['twi_l'], tb['m2'])
    hf = hf.reshape(HY_ORDER, C, FFT_N1, 2 * FFT_N2)
    out = pl.pallas_call(
        _hyena_conv_kernel,
        grid=(B // 2, C // R),
        in_specs=[pl.BlockSpec((2, R, half, FFT_N2), lambda b, c: (b, c, 0, 0)),
                  pl.BlockSpec((HY_ORDER, 2, R, half, FFT_N2), lambda b, c: (0, b, c, 0, 0)),
                  pl.BlockSpec((HY_ORDER, R, FFT_N1, 2 * FFT_N2), lambda b, c: (0, c, 0, 0)),
                  pl.BlockSpec((HY_ORDER, R, 1, 1), lambda b, c: (0, c, 0, 0)),
                  const(tb['m1']), const(tb['twr_l']), const(tb['twi_l']), const(tb['m2']), const(tb['m3']),
                  const(tb['twr_s']), const(tb['twi_s']), const(tb['m4'])],
        out_specs=pl.BlockSpec((2, R, half, FFT_N2), lambda b, c: (b, c, 0, 0)),
        out_shape=jax.ShapeDtypeStruct((B, C, half, FFT_N2), jnp.float32),
        compiler_params=pltpu.CompilerParams(dimension_semantics=("parallel", "parallel"),
                                             vmem_limit_bytes=VMEM_LIMIT_BYTES),
        name="hyena_conv",
    )(y_t.reshape(B, C, half, FFT_N2), g_t.reshape(HY_ORDER, B, C, half, FFT_N2), hf,
      bias.reshape(HY_ORDER, C, 1, 1), tb['m1'], tb['twr_l'], tb['twi_l'], tb['m2'], tb['m3'],
      tb['twr_s'], tb['twi_s'], tb['m4'])
    return out.reshape(B, C, L)


def hyena_mixer(p, conv_w, w1, b1, freq1, w2, b2, freq2, w3, b3, bias):
    B, L = p.shape[:2]
    z = short_conv(p, conv_w)
    h_full = hyena_filters(L, w1, b1, freq1, w2, b2, freq2, w3, b3)
    if 2 * L == FFT_N:
        g_t = jnp.transpose(z[..., :HY_ORDER * HY_WIDTH].reshape(B, L, HY_ORDER, HY_WIDTH), (2, 0, 3, 1))
        y_t = jnp.swapaxes(z[..., HY_ORDER * HY_WIDTH:], 1, 2)
        y_t = hyena_long_conv(y_t, g_t, jnp.transpose(h_full, (1, 2, 0)), bias)
        return jnp.swapaxes(y_t, 1, 2)
    gates = (z[..., :HY_WIDTH], z[..., HY_WIDTH:2 * HY_WIDTH])
    y = z[..., 2 * HY_WIDTH:]
    for o in range(HY_ORDER):
        y = gates[o] * fft_long_conv(y, h_full[:, o], bias[o])
    return y


SC_CORES = 2
SC_SUBCORES = 16
SC_LANES = 16
SC_WORKERS = SC_CORES * SC_SUBCORES
PEER_SLOTS = PEER_HEADS * PEER_TOPK
PEER_GATHER_ROWS = 32
PEER_GATHERS = PEER_SLOTS // PEER_GATHER_ROWS
PEER_ACC_VREGS = 8
PEER_ROW_BUFFERS = 4
PEER_ROW_WORDS = D_MODEL // 2
HI_MASK = -65536


def pack_expert_table(t):
    b = lax.bitcast_convert_type(t.astype(jnp.bfloat16), jnp.uint16).astype(jnp.uint32)
    return lax.bitcast_convert_type(b[:, :PEER_ROW_WORDS] | (b[:, PEER_ROW_WORDS:] << 16), jnp.int32)


def _sc_peer(table, idx, aux, phase):
    N = idx.shape[0]
    tpw = N // SC_WORKERS
    assert tpw % 2 == 0 and N % SC_WORKERS == 0
    assert PEER_GATHERS % PEER_ROW_BUFFERS == 0
    mesh = plsc.VectorSubcoreMesh(core_axis_name="c", subcore_axis_name="s")
    aux_shape = (D_MODEL,) if phase == "dot" else (PEER_SLOTS,)
    out_tok = (PEER_SLOTS,) if phase == "dot" else (D_MODEL,)
    NBUF = PEER_ROW_BUFFERS
    AHEAD = NBUF - 1
    HW = PEER_ROW_WORDS

    @functools.partial(
        pl.kernel, mesh=mesh,
        out_type=jax.ShapeDtypeStruct((N,) + out_tok, jnp.float32),
        compiler_params=pltpu.CompilerParams(needs_layout_passes=False),
        cost_estimate=pl.CostEstimate(flops=2 * N * PEER_SLOTS * D_MODEL, transcendentals=0,
                                      bytes_accessed=16 * (N * PEER_SLOTS * HW * 4 + N * (D_MODEL + 2 * PEER_SLOTS) * 4)),
        scratch_types=[
            pltpu.VMEM((2, PEER_GATHERS, PEER_GATHER_ROWS), jnp.int32),
            pltpu.VMEM((2,) + aux_shape, jnp.float32),
            pltpu.VMEM((NBUF, PEER_GATHER_ROWS, HW), jnp.int32),
            pltpu.VMEM((2,) + out_tok, jnp.float32),
            pltpu.SemaphoreType.DMA((NBUF,)),
            pltpu.SemaphoreType.DMA((2,)),
            pltpu.SemaphoreType.DMA((2,)),
        ],
    )
    def k(table_hbm, idx_hbm, aux_hbm, out_hbm, idx_v, aux_v, rows_v, out_v, sem_r, sem_i, sem_o):
        wid = lax.axis_index("s") * SC_CORES + lax.axis_index("c")
        base = wid * tpw

        def gather(p, c, b):
            return pltpu.make_async_copy(table_hbm.at[idx_v.at[p, c]], rows_v.at[b], sem_r.at[b])

        def load_meta(t, p):
            return (pltpu.make_async_copy(idx_hbm.at[t], idx_v.at[p], sem_i.at[p]),
                    pltpu.make_async_copy(aux_hbm.at[t], aux_v.at[p], sem_i.at[p]))

        def store_out(t, p):
            return pltpu.make_async_copy(out_v.at[p], out_hbm.at[t], sem_o.at[p])

        def halves(word):
            return (plsc.bitcast(lax.shift_left(word, 16), jnp.float32), plsc.bitcast(word & HI_MASK, jnp.float32))

        def compute(p, c, b):
            if phase == "dot":
                lane = lax.iota(jnp.int32, SC_LANES)
                vec = jnp.zeros((SC_LANES,), jnp.float32)
                groups_per_vec = SC_LANES // PEER_ACC_VREGS
                for g in range(PEER_GATHER_ROWS // PEER_ACC_VREGS):
                    def body(cc, accs):
                        x_lo = aux_v[p, pl.ds(cc * SC_LANES, SC_LANES)]
                        x_hi = aux_v[p, pl.ds(HW + cc * SC_LANES, SC_LANES)]
                        out = []
                        for r in range(PEER_ACC_VREGS):
                            lo, hi = halves(rows_v[b, g * PEER_ACC_VREGS + r, pl.ds(cc * SC_LANES, SC_LANES)])
                            out.append(accs[r] + lo * x_lo + hi * x_hi)
                        return tuple(out)
                    accs = lax.fori_loop(0, HW // SC_LANES, body,
                                         tuple(jnp.zeros((SC_LANES,), jnp.float32) for _ in range(PEER_ACC_VREGS)))
                    for r in range(PEER_ACC_VREGS):
                        vec = jnp.where(lane == (g % groups_per_vec) * PEER_ACC_VREGS + r, jnp.sum(accs[r]), vec)
                    if g % groups_per_vec == groups_per_vec - 1:
                        out_v[p, pl.ds(c * PEER_GATHER_ROWS + (g // groups_per_vec) * SC_LANES, SC_LANES)] = vec
            else:
                words = PEER_ACC_VREGS // 2
                for db in range(HW // (words * SC_LANES)):
                    def body(kk, accs):
                        wv = plsc.load_gather(aux_v.at[p], [jnp.full((SC_LANES,), c * PEER_GATHER_ROWS + kk, jnp.int32)])
                        out = []
                        for j in range(words):
                            lo, hi = halves(rows_v[b, kk, pl.ds((db * words + j) * SC_LANES, SC_LANES)])
                            out += [accs[2 * j] + lo * wv, accs[2 * j + 1] + hi * wv]
                        return tuple(out)
                    if c == 0:
                        init = tuple(jnp.zeros((SC_LANES,), jnp.float32) for _ in range(2 * words))
                    else:
                        init = tuple(out_v[p, pl.ds(half * HW + (db * words + j) * SC_LANES, SC_LANES)]
                                     for j in range(words) for half in range(2))
                    accs = lax.fori_loop(0, PEER_GATHER_ROWS, body, init)
                    for j in range(words):
                        out_v[p, pl.ds((db * words + j) * SC_LANES, SC_LANES)] = accs[2 * j]
                        out_v[p, pl.ds(HW + (db * words + j) * SC_LANES, SC_LANES)] = accs[2 * j + 1]

        for d in load_meta(base, 0):
            d.start()
        for d in load_meta(base, 0):
            d.wait()
        for c in range(AHEAD):
            gather(0, c, c % NBUF).start()

        @pl.loop(0, tpw // 2)
        def _(i2):
            for p in range(2):
                i = i2 * 2 + p
                t = base + i
                nxt = base + jnp.minimum(i + 1, tpw - 1)
                for d in load_meta(nxt, 1 - p):
                    d.start()

                @pl.when(i2 > 0)
                def _():
                    store_out(t, p).wait()

                for c in range(PEER_GATHERS):
                    ahead = c + AHEAD
                    if ahead < PEER_GATHERS:
                        gather(p, ahead, ahead % NBUF).start()
                    else:
                        if ahead == PEER_GATHERS:
                            for d in load_meta(nxt, 1 - p):
                                d.wait()
                        gather(1 - p, ahead - PEER_GATHERS, ahead % NBUF).start()
                    gather(p, c, c % NBUF).wait()
                    compute(p, c, c % NBUF)
                store_out(t, p).start()

        for c in range(AHEAD):
            gather(0, c, c % NBUF).wait()
        for p in range(2):
            store_out(base, p).wait()

    return k(table, idx.reshape(N, PEER_GATHERS, PEER_GATHER_ROWS), aux)


PEER_TOKENS = 256
INT_BIG = 2 ** 30


def _extract_topk(cand_ref, ids_ref, val_out_ref, id_out_ref, row0):
    def body(r, carry):
        c = cand_ref[...]
        ids = ids_ref[...]
        m = jnp.max(c, axis=0, keepdims=True)
        sel = jnp.min(jnp.where(c == m, ids, INT_BIG), axis=0, keepdims=True)
        cand_ref[...] = jnp.where(ids == sel, -jnp.inf, c)
        val_out_ref[pl.ds(row0 + r, 1), :] = m
        id_out_ref[pl.ds(row0 + r, 1), :] = sel
        return carry
    lax.fori_loop(0, PEER_TOPK, body, 0)


def _peer_retrieve_kernel(x_ref, gain_ref, scale_ref, shift_ref, wq_ref, keys_ref,
                          h_ref, idx_out_ref, gate_out_ref,
                          s_ref, ids1_ref, sv_ref, si_ref, cand_ref, cid_ref, ts_ref, idx_ref, gate_ref):
    x = x_ref[0]
    y = x * lax.rsqrt(jnp.mean(x * x, axis=-1, keepdims=True) + NORM_EPS)
    h = (y * gain_ref[...]) * (1.0 + scale_ref[0]) + shift_ref[0]
    h_ref[0] = h
    q = jnp.dot(h.astype(jnp.bfloat16), wq_ref[...], preferred_element_type=jnp.float32)
    T = PEER_TOKENS
    K = PEER_TOPK
    ids1_ref[...] = lax.broadcasted_iota(jnp.int32, (PEER_N_KEYS, T), 0)
    for hd in range(PEER_HEADS):
        for p in range(2):
            hp = hd * 2 + p
            qs = q[:, hp * PEER_HALF:(hp + 1) * PEER_HALF].astype(jnp.bfloat16)
            s_ref[...] = lax.dot_general(keys_ref[hp], qs, (((1,), (1,)), ((), ())),
                                         preferred_element_type=jnp.float32)
            _extract_topk(s_ref, ids1_ref, sv_ref, si_ref, p * K)
        for i in range(K):
            cand_ref[i * K:(i + 1) * K, :] = sv_ref[i:i + 1, :] + sv_ref[K:2 * K, :]
            cid_ref[i * K:(i + 1) * K, :] = si_ref[i:i + 1, :] * PEER_N_KEYS + si_ref[K:2 * K, :]
        _extract_topk(cand_ref, cid_ref, ts_ref, idx_ref, hd * K)
        ts = ts_ref[hd * K:(hd + 1) * K, :]
        e = jnp.exp(ts - jnp.max(ts, axis=0, keepdims=True))
        gate_ref[hd * K:(hd + 1) * K, :] = e / jnp.sum(e, axis=0, keepdims=True)
    idx_out_ref[...] = idx_ref[...].T
    gate_out_ref[...] = gate_ref[...].T


def peer_retrieve(x, gain, scale, shift, w_q, sub_keys):
    B, L, D = x.shape
    T = PEER_TOKENS
    nt = L // T
    keys = sub_keys.reshape(PEER_HEADS * 2, PEER_N_KEYS, PEER_HALF).astype(jnp.bfloat16)
    return pl.pallas_call(
        _peer_retrieve_kernel,
        grid=(B, nt),
        in_specs=[
            pl.BlockSpec((1, T, D), lambda b, i: (b, i, 0)),
            pl.BlockSpec((1, D), lambda b, i: (0, 0)),
            pl.BlockSpec((1, 1, D), lambda b, i: (b, 0, 0)),
            pl.BlockSpec((1, 1, D), lambda b, i: (b, 0, 0)),
            pl.BlockSpec((D, PEER_HEADS * 2 * PEER_HALF), lambda b, i: (0, 0)),
            pl.BlockSpec((PEER_HEADS * 2, PEER_N_KEYS, PEER_HALF), lambda b, i: (0, 0, 0)),
        ],
        out_specs=[
            pl.BlockSpec((1, T, D), lambda b, i: (b, i, 0)),
            pl.BlockSpec((T, PEER_SLOTS), lambda b, i: (b * nt + i, 0)),
            pl.BlockSpec((T, PEER_SLOTS), lambda b, i: (b * nt + i, 0)),
        ],
        out_shape=[
            jax.ShapeDtypeStruct((B, L, D), jnp.float32),
            jax.ShapeDtypeStruct((B * L, PEER_SLOTS), jnp.int32),
            jax.ShapeDtypeStruct((B * L, PEER_SLOTS), jnp.float32),
        ],
        scratch_shapes=[
            pltpu.VMEM((PEER_N_KEYS, T), jnp.float32),
            pltpu.VMEM((PEER_N_KEYS, T), jnp.int32),
            pltpu.VMEM((2 * PEER_TOPK, T), jnp.float32),
            pltpu.VMEM((2 * PEER_TOPK, T), jnp.int32),
            pltpu.VMEM((PEER_TOPK * PEER_TOPK, T), jnp.float32),
            pltpu.VMEM((PEER_TOPK * PEER_TOPK, T), jnp.int32),
            pltpu.VMEM((PEER_SLOTS, T), jnp.float32),
            pltpu.VMEM((PEER_SLOTS, T), jnp.int32),
            pltpu.VMEM((PEER_SLOTS, T), jnp.float32),
        ],
        compiler_params=pltpu.CompilerParams(dimension_semantics=("parallel", "parallel"),
                                             vmem_limit_bytes=VMEM_LIMIT_BYTES),
        name="peer_retrieve",
    )(x, gain.reshape(1, D), scale, shift, w_q.astype(jnp.bfloat16), keys)


PEER_ACT_ROWS = 1024


def _peer_act_kernel(dots_ref, gate_ref, w_ref):
    a = dots_ref[...]
    w_ref[...] = gate_ref[...] * (0.5 * a * (1.0 + lax.erf(a * (2.0 ** -0.5))))


def peer_act(dots, gate):
    N = dots.shape[0]
    T = min(PEER_ACT_ROWS, N)
    spec = pl.BlockSpec((T, PEER_SLOTS), lambda i: (i, 0))
    return pl.pallas_call(
        _peer_act_kernel,
        grid=(N // T,),
        in_specs=[spec, spec],
        out_specs=spec,
        out_shape=jax.ShapeDtypeStruct((N, PEER_SLOTS), jnp.float32),
        compiler_params=pltpu.CompilerParams(dimension_semantics=("parallel",)),
        name="peer_act",
    )(dots, gate)


def peer_ffn(x, gain, scale, shift, w_q, sub_keys, exp_u, exp_v):
    B, L, D = x.shape
    N = B * L
    h, e_idx, gate = peer_retrieve(x, gain, scale, shift, w_q, sub_keys)
    dots = _sc_peer(exp_u, e_idx, h.reshape(N, D), "dot")
    return _sc_peer(exp_v, e_idx, peer_act(dots, gate), "wsum").reshape(B, L, D)


def _mix_and_retrieve(li, x, c, ctx, c_ctx, mod_w, mod_b, mix_norm, w_in, w_out, rw_conv, rw_decay_up, rw_decay0, rw_a_up, rw_a0, rw_gate_up, rw_k_k, rw_k_a, rw_r_k, rw_gn_g, rw_gn_b, mla_q_norm, mla_w_uq, mla_kv_norm, mla_w_ukv, mla_q_gain, mla_k_gain, hy_conv, hy_w1, hy_b1, hy_freq1, hy_w2, hy_b2, hy_freq2, hy_w3, hy_b3, hy_bias, ffn_norm, peer_wq, peer_keys, peer_u, peer_v):
    B, L, D = x.shape
    s_rw, s_mla = RW_PROJ, RW_PROJ + MLA_PROJ
    need_ctx = li < DEPTH - 1
    mod_l = (jax.nn.silu(c) @ mod_w[li] + mod_b[li])[:, None, :]
    mod_c = (jax.nn.silu(c_ctx) @ mod_w[li] + mod_b[li])[None, None, :]
    shm_l, scm_l, gm_l, shf_l, scf_l, gf_l = jnp.split(mod_l, N_MOD, axis=-1)
    shm_c, scm_c, gm_c, shf_c, scf_c, gf_c = jnp.split(mod_c, N_MOD, axis=-1)

    p_l = norm_mod_proj(x, mix_norm[li], scm_l, shm_l, w_in[li], 512)
    p_c = norm_mod_proj(ctx, mix_norm[li], jnp.broadcast_to(scm_c, (B, 1, D)),
                        jnp.broadcast_to(shm_c, (B, 1, D)), w_in[li], 256)
    rw_l, rw_c = rwkv7_mixer(p_l[..., :s_rw], p_c[..., :s_rw], rw_conv[li], rw_decay_up[li], rw_decay0[li],
                             rw_a_up[li], rw_a0[li], rw_gate_up[li], rw_k_k[li], rw_k_a[li], rw_r_k[li],
                             rw_gn_g[li], rw_gn_b[li], need_ctx)
    ml_l, ml_c = mla_mixer(p_l[..., s_rw:s_mla], p_c[..., s_rw:s_mla], mla_q_norm[li], mla_w_uq[li],
                           mla_kv_norm[li], mla_w_ukv[li], mla_q_gain[li], mla_k_gain[li], need_ctx)
    hy_prm = (hy_conv[li], hy_w1[li], hy_b1[li], hy_freq1[li], hy_w2[li], hy_b2[li], hy_freq2[li],
              hy_w3[li], hy_b3[li], hy_bias[li])
    hy_l = hyena_mixer(p_l[..., s_mla:], *hy_prm)
    x = x + gm_l * (jnp.concatenate([rw_l, ml_l, hy_l], axis=-1) @ w_out[li])
    if need_ctx:
        hy_c = hyena_mixer(p_c[..., s_mla:], *hy_prm)
        ctx = ctx + gm_c * (jnp.concatenate([rw_c, ml_c, hy_c], axis=-1) @ w_out[li])
        ctx = ctx + gf_c * peer_ffn(ctx, ffn_norm[li], jnp.broadcast_to(scf_c, (B, 1, D)),
                                    jnp.broadcast_to(shf_c, (B, 1, D)),
                                    peer_wq[li], peer_keys[li], peer_u[li], peer_v[li])
    h, e_idx, gate = peer_retrieve(x, ffn_norm[li], scf_l, shf_l, peer_wq[li], peer_keys[li])
    return x, ctx, gf_l, h.reshape(B * L, D), e_idx, gate


BATCH_GROUPS = 4


def kernel(x, c, ctx, c_ctx, mod_w, mod_b, mix_norm, w_in, w_out, rw_conv, rw_decay_up, rw_decay0, rw_a_up, rw_a0, rw_gate_up, rw_k_k, rw_k_a, rw_r_k, rw_gn_g, rw_gn_b, mla_q_norm, mla_w_uq, mla_kv_norm, mla_w_ukv, mla_q_gain, mla_k_gain, hy_conv, hy_w1, hy_b1, hy_freq1, hy_w2, hy_b2, hy_freq2, hy_w3, hy_b3, hy_bias, ffn_norm, peer_wq, peer_keys, peer_u, peer_v):
    params = (mod_w, mod_b, mix_norm, w_in, w_out, rw_conv, rw_decay_up, rw_decay0, rw_a_up, rw_a0, rw_gate_up,
              rw_k_k, rw_k_a, rw_r_k, rw_gn_g, rw_gn_b, mla_q_norm, mla_w_uq, mla_kv_norm, mla_w_ukv, mla_q_gain,
              mla_k_gain, hy_conv, hy_w1, hy_b1, hy_freq1, hy_w2, hy_b2, hy_freq2, hy_w3, hy_b3, hy_bias,
              ffn_norm, peer_wq, peer_keys)
    peer_u = [pack_expert_table(peer_u[li]) for li in range(DEPTH)]
    peer_v = [pack_expert_table(peer_v[li]) for li in range(DEPTH)]
    params = params + (peer_u, peer_v)
    G = BATCH_GROUPS
    bg = x.shape[0] // G
    L, D = x.shape[1:]
    xs = [x[g * bg:(g + 1) * bg] for g in range(G)]
    cs = [c[g * bg:(g + 1) * bg] for g in range(G)]
    ctxs = [ctx[g * bg:(g + 1) * bg] for g in range(G)]
    stages = [(li, g) for li in range(DEPTH) for g in range(G)]
    pending = {}
    token = None
    for k in range(len(stages) + 1):
        if k < len(stages):
            li, g = stages[k]
            ins = (xs[g], ctxs[g])
            if token is not None:
                token, ins = lax.optimization_barrier((token, ins))
            xm, ctxs[g], gf, h, e_idx, gate = _mix_and_retrieve(li, ins[0], cs[g], ins[1], c_ctx, *params)
            dots = _sc_peer(peer_u[li], e_idx, h, "dot")
            pending[k] = (xm, gf, e_idx, gate, dots)
            token = gate
        if k >= 1:
            li, g = stages[k - 1]
            xm, gf, e_idx, gate, dots = pending.pop(k - 1)
            token, (dots, gate) = lax.optimization_barrier((token, (dots, gate)))
            w = peer_act(dots, gate)
            xs[g] = xm + gf * _sc_peer(peer_v[li], e_idx, w, "wsum").reshape(bg, L, D)
            token = w
    return jnp.concatenate(xs, axis=0)
```

```python
import functools
import math

import jax
import jax.numpy as jnp
import numpy as np
from jax import lax
from jax.experimental import pallas as pl
from jax.experimental.pallas import tpu as pltpu
from jax.experimental.pallas import tpu_sc as plsc

D_MODEL = 1024
DEPTH = 2
GRID_W = 64
N_MOD = 6
NORM_EPS = 1e-6

RW_HEADS = 6
RW_HEAD_DIM = 64
RW_WIDTH = RW_HEADS * RW_HEAD_DIM
RW_DECAY_RANK = 64
RW_A_RANK = 64
RW_GATE_RANK = 128
RW_DECAY_SCALE = 0.6065306597
RW_GN_EPS = 64e-5
L2_EPS = 1e-12

MLA_HEADS = 6
MLA_Q_RANK = 256
MLA_KV_RANK = 128
MLA_NOPE_DIM = 64
MLA_ROPE_DIM = 32
MLA_V_DIM = 64
MLA_QK_DIM = MLA_NOPE_DIM + MLA_ROPE_DIM
MLA_WIDTH = MLA_HEADS * MLA_V_DIM
AXIS_ROPE_DIM = MLA_ROPE_DIM // 2
ROPE_THETA = 10000.0

HY_WIDTH = 256
HY_ORDER = 2
HY_POS_BANDS = 16
HY_SHORT_DECAY_PCT = 0.3
HY_LONG_DECAY_PCT = 1.5
HY_DECAY_TARGET = 1e-2

PEER_HEADS = 8
PEER_N_KEYS = 128
PEER_TOPK = 16
PEER_QUERY_DIM = 256
PEER_HALF = PEER_QUERY_DIM // 2

RW_PROJ = 3 * RW_WIDTH + RW_DECAY_RANK + RW_A_RANK + RW_GATE_RANK
MLA_PROJ = MLA_Q_RANK + MLA_KV_RANK + MLA_ROPE_DIM
HY_PROJ = (HY_ORDER + 1) * HY_WIDTH
IN_PROJ = RW_PROJ + MLA_PROJ + HY_PROJ
MIX_WIDTH = RW_WIDTH + MLA_WIDTH + HY_WIDTH

VMEM_LIMIT_BYTES = 48 * 1024 * 1024


def _norm_mod_proj_kernel(x_ref, gain_ref, scale_ref, shift_ref, w_ref, o_ref):
    x = x_ref[0]
    y = x * lax.rsqrt(jnp.mean(x * x, axis=-1, keepdims=True) + NORM_EPS)
    y = y * gain_ref[...]
    y = y * (1.0 + scale_ref[0]) + shift_ref[0]
    o_ref[0] = jnp.dot(y.astype(jnp.bfloat16), w_ref[...], preferred_element_type=jnp.float32)


def norm_mod_proj(x, gain, scale, shift, w, block_rows):
    B, L, D = x.shape
    N = w.shape[1]
    return pl.pallas_call(
        _norm_mod_proj_kernel,
        grid=(B, L // block_rows),
        in_specs=[
            pl.BlockSpec((1, block_rows, D), lambda b, i: (b, i, 0)),
            pl.BlockSpec((1, D), lambda b, i: (0, 0)),
            pl.BlockSpec((1, 1, D), lambda b, i: (b, 0, 0)),
            pl.BlockSpec((1, 1, D), lambda b, i: (b, 0, 0)),
            pl.BlockSpec((D, N), lambda b, i: (0, 0)),
        ],
        out_specs=pl.BlockSpec((1, block_rows, N), lambda b, i: (b, i, 0)),
        out_shape=jax.ShapeDtypeStruct((B, L, N), jnp.float32),
        compiler_params=pltpu.CompilerParams(
            dimension_semantics=("parallel", "parallel"), vmem_limit_bytes=VMEM_LIMIT_BYTES),
        name="norm_mod_proj",
    )(x, gain.reshape(1, D), scale, shift, w.astype(jnp.bfloat16))


RW_CHUNK = 64


def _rwkv_chunk_kernel(r_ref, kk_ref, v_ref, lw_ref, akk_ref, kr_ref, y_ref, h_ref):
    d = pl.program_id(0)
    n = pl.program_id(2)

    @pl.when(n == 0)
    def _():
        h_ref[...] = jnp.zeros_like(h_ref)

    C = RW_CHUNK
    row = lax.broadcasted_iota(jnp.int32, (C, C), 0)
    col = lax.broadcasted_iota(jnp.int32, (C, C), 1)
    lag = (row - col) * (1 - 2 * d)
    before = lag > 0
    upto = lag >= 0
    tri = upto.astype(jnp.float32)
    eye = (row == col).astype(jnp.float32)
    bf = jnp.bfloat16
    f32 = jnp.float32

    def mm(a, b):
        return jnp.dot(a.astype(bf), b.astype(bf), preferred_element_type=f32)

    def mm_nt(a, b):
        return lax.dot_general(a.astype(bf), b.astype(bf), (((1,), (1,)), ((), ())), preferred_element_type=f32)

    def mm_tn(a, b):
        return lax.dot_general(a.astype(bf), b.astype(bf), (((0,), (0,)), ((), ())), preferred_element_type=f32)

    hs = range(RW_HEADS)
    HD = RW_HEAD_DIM
    heads = lambda t: [t[:, h * HD:(h + 1) * HD] for h in hs]
    r = heads(r_ref[0])
    kk = heads(kk_ref[0])
    v = heads(v_ref[0])
    lw = heads(lw_ref[0, 0])
    akk = heads(akk_ref[0, 0])
    kr = heads(kr_ref[0, 0])
    G = [jnp.dot(tri, lw[h], preferred_element_type=f32, precision=lax.Precision.HIGHEST) for h in hs]
    gtot = [jnp.sum(lw[h], axis=0, keepdims=True) for h in hs]
    Einv = [jnp.exp(-G[h]) for h in hs]
    At = [-kk[h] * jnp.exp(G[h] - lw[h]) for h in hs]
    Rt = [r[h] * jnp.exp(G[h]) for h in hs]
    Bt = [akk[h] * Einv[h] for h in hs]
    Kt = [kr[h] * Einv[h] for h in hs]
    X = [mm_nt(jnp.concatenate([At[h], Rt[h]], axis=0), jnp.concatenate([Bt[h], Kt[h]], axis=0)) for h in hs]
    M_ab = [jnp.where(before, X[h][:C, :C], 0.0) for h in hs]
    M_ak = [jnp.where(before, X[h][:C, C:], 0.0) for h in hs]
    A_rb = [jnp.where(upto, X[h][C:, :C], 0.0) for h in hs]
    A_rk = [jnp.where(upto, X[h][C:, C:], 0.0) for h in hs]
    MV = [mm(M_ak[h], v[h]) for h in hs]
    Mp = M_ab
    T = [eye + Mp[h] for h in hs]
    for _ in range(5):
        Mp = [jnp.dot(Mp[h], Mp[h], preferred_element_type=f32) for h in hs]
        T = [T[h] + jnp.dot(T[h], Mp[h], preferred_element_type=f32) for h in hs]
    WU = [jnp.dot(T[h], jnp.concatenate([At[h], MV[h]], axis=1), preferred_element_type=f32) for h in hs]
    H0 = [h_ref[h] for h in hs]
    Ehat = [jnp.exp(gtot[h] - G[h]) for h in hs]
    Om = [Rt[h] + mm(A_rb[h], WU[h][:, :HD]) for h in hs]
    Y0 = [mm(A_rb[h], WU[h][:, HD:]) + mm(A_rk[h], v[h]) for h in hs]
    BW = [mm_tn(akk[h] * Ehat[h], WU[h]) for h in hs]
    KV = [mm_tn(kr[h] * Ehat[h], v[h]) for h in hs]
    y_ref[0, 0] = jnp.concatenate([jnp.dot(Om[h], H0[h], preferred_element_type=f32) + Y0[h] for h in hs], axis=1)
    for h in hs:
        P = eye * jnp.exp(gtot[h]) + BW[h][:, :HD]
        h_ref[h] = jnp.dot(P, H0[h], preferred_element_type=f32) + BW[h][:, HD:] + KV[h]


def rwkv_chunked(r, kk, v, lw, akk, kr, n_ctx):
    B, T, W = r.shape
    H = W // RW_HEAD_DIM
    nc = n_ctx // RW_CHUNK
    nt = T // RW_CHUNK

    def chunk_of(d, n):
        bwd = jnp.where(n < nc, nc - 1 - n, nt - 1 - (n - nc))
        return jnp.where(d == 0, n, bwd)

    spec1 = pl.BlockSpec((1, RW_CHUNK, W), lambda d, b, n: (b, chunk_of(d, n), 0))
    spec2 = pl.BlockSpec((1, 1, RW_CHUNK, W), lambda d, b, n: (d, b, chunk_of(d, n), 0))
    return pl.pallas_call(
        _rwkv_chunk_kernel,
        grid=(2, B, nt),
        in_specs=[spec1, spec1, spec1, spec2, spec2, spec2],
        out_specs=spec2,
        out_shape=jax.ShapeDtypeStruct((2, B, T, W), jnp.float32),
        scratch_shapes=[pltpu.VMEM((H, RW_HEAD_DIM, RW_HEAD_DIM), jnp.float32)],
        compiler_params=pltpu.CompilerParams(dimension_semantics=("parallel", "parallel", "arbitrary")),
        name="rwkv_chunked",
    )(r, kk, v, lw, akk, kr)


def short_conv(x, w):
    xp = jnp.pad(x, ((0, 0), (1, 1), (0, 0)))
    return xp[:, :-2] * w[0] + xp[:, 1:-1] * w[1] + xp[:, 2:] * w[2]


LANE = 128
RW_PREP_ROWS = 256
MLA_PAD_WIDTH = MLA_HEADS * LANE
MLA_PREP_ROWS = 256
ATTN_Q_ROWS = 512


def _split_dot(x, m):
    hi = x.astype(jnp.bfloat16)
    lo = (x - hi.astype(jnp.float32)).astype(jnp.bfloat16)
    return (jnp.dot(hi, m, preferred_element_type=jnp.float32) + jnp.dot(lo, m, preferred_element_type=jnp.float32))


def _rwkv_prep_kernel(z_ref, wda_ref, d0_ref, a0_ref, gup_ref, kk_ref_w, ka_ref, rk_ref, hsum_ref,
                      r_ref, kk_ref, v_ref, lw_ref, akk_ref, kr_ref, g_ref, bonus_ref):
    W = RW_WIDTH
    bf = jnp.bfloat16
    z = z_ref[0]
    r, k, v = z[:, :W], z[:, W:2 * W], z[:, 2 * W:3 * W]
    da = z[:, 3 * W:3 * W + LANE]
    lane = lax.broadcasted_iota(jnp.int32, da.shape, 1)
    da = jnp.where(lane < RW_DECAY_RANK, jnp.tanh(da), da)
    up = jnp.dot(da.astype(bf), wda_ref[...], preferred_element_type=jnp.float32)
    g_lo = z[:, 3 * W + LANE:]
    g_ref[0] = jnp.dot(jax.nn.sigmoid(g_lo).astype(bf), gup_ref[...], preferred_element_type=jnp.float32)
    hsum = hsum_ref[...]
    kk = k * kk_ref_w[...]
    kk = kk * lax.rsqrt(_split_dot(kk * kk, hsum) + L2_EPS)
    r_ref[0] = r
    v_ref[0] = v
    kk_ref[0] = kk
    bonus_ref[0] = _split_dot(r * k * rk_ref[...], hsum) * v
    for d in range(2):
        lw_ref[d, 0] = -RW_DECAY_SCALE * jax.nn.sigmoid(d0_ref[d:d + 1, :] + up[:, d * W:(d + 1) * W])
        a = jax.nn.sigmoid(a0_ref[d:d + 1, :] + up[:, (2 + d) * W:(3 + d) * W])
        akk_ref[d, 0] = kk * a
        kr_ref[d, 0] = k * (1.0 + (a - 1.0) * ka_ref[...])


def rwkv_prep(z, decay_up, decay0, a_up, a0, gate_up, k_k, k_a, r_k):
    B, L, _ = z.shape
    W = RW_WIDTH
    T = min(RW_PREP_ROWS, L)
    zero = jnp.zeros((RW_DECAY_RANK, 2 * W), jnp.float32)
    wda = jnp.concatenate([
        jnp.concatenate([decay_up[0], decay_up[1], zero], axis=1),
        jnp.concatenate([zero, a_up[0], a_up[1]], axis=1)], axis=0).astype(jnp.bfloat16)
    head = jnp.arange(W) // RW_HEAD_DIM
    hsum = (head[:, None] == head[None, :]).astype(jnp.bfloat16)
    row = lambda a: a.reshape(1, W)
    const = lambda a: pl.BlockSpec(a.shape, lambda b, i: (0,) * a.ndim)
    tok = pl.BlockSpec((1, T, W), lambda b, i: (b, i, 0))
    tok2 = pl.BlockSpec((2, 1, T, W), lambda b, i: (0, b, i, 0))
    f1 = jax.ShapeDtypeStruct((B, L, W), jnp.float32)
    f2 = jax.ShapeDtypeStruct((2, B, L, W), jnp.float32)
    args = (z, wda, decay0, a0, gate_up.astype(jnp.bfloat16), row(k_k), row(k_a), row(r_k), hsum)
    return pl.pallas_call(
        _rwkv_prep_kernel,
        grid=(B, L // T),
        in_specs=[pl.BlockSpec((1, T, RW_PROJ), lambda b, i: (b, i, 0))] + [const(a) for a in args[1:]],
        out_specs=[tok, tok, tok, tok2, tok2, tok2, tok, tok],
        out_shape=[f1, f1, f1, f2, f2, f2, f1, f1],
        compiler_params=pltpu.CompilerParams(dimension_semantics=("parallel", "parallel"),
                                             vmem_limit_bytes=VMEM_LIMIT_BYTES),
        name="rwkv_prep",
    )(*args)


def _rwkv_readout_kernel(y_ref, g_ref, bonus_ref, gng_ref, gnb_ref, hsum_ref, o_ref):
    y = y_ref[0, 0] + y_ref[1, 0]
    hsum = hsum_ref[...]
    mu = _split_dot(y, hsum) * (1.0 / RW_HEAD_DIM)
    d = y - mu
    var = _split_dot(d * d, hsum) * (1.0 / RW_HEAD_DIM)
    yn = d * lax.rsqrt(var + RW_GN_EPS) * gng_ref[...] + gnb_ref[...]
    o_ref[0] = (yn + bonus_ref[0]) * g_ref[0]


def rwkv_readout(y, g, bonus, gn_g, gn_b, t0):
    B, L, W = g.shape
    T = min(RW_PREP_ROWS, L)
    off = t0 // T
    head = jnp.arange(W) // RW_HEAD_DIM
    hsum = (head[:, None] == head[None, :]).astype(jnp.bfloat16)
    tok = pl.BlockSpec((1, T, W), lambda b, i: (b, i, 0))
    const = lambda a: pl.BlockSpec(a.shape, lambda b, i: (0,) * a.ndim)
    gg, gb = gn_g.reshape(1, W), gn_b.reshape(1, W)
    return pl.pallas_call(
        _rwkv_readout_kernel,
        grid=(B, L // T),
        in_specs=[pl.BlockSpec((2, 1, T, W), lambda b, i: (0, b, i + off, 0)), tok, tok, const(gg), const(gb), const(hsum)],
        out_specs=tok,
        out_shape=jax.ShapeDtypeStruct((B, L, W), jnp.float32),
        compiler_params=pltpu.CompilerParams(dimension_semantics=("parallel", "parallel")),
        name="rwkv_readout",
    )(y, g, bonus, gg, gb, hsum)


def rwkv7_mixer(p_lat, p_ctx, conv_w, decay_up, decay0, a_up, a0, gate_up, k_k, k_a, r_k, gn_g, gn_b, need_ctx):
    prm = (decay_up, decay0, a_up, a0, gate_up, k_k, k_a, r_k)
    lat = rwkv_prep(short_conv(p_lat, conv_w), *prm)
    ctx = rwkv_prep(short_conv(p_ctx, conv_w), *prm)
    n_ctx = p_ctx.shape[1]
    seq = lambda i: jnp.concatenate([ctx[i], lat[i]], axis=-2)
    y = rwkv_chunked(seq(0), seq(1), seq(2), seq(3), seq(4), seq(5), n_ctx)
    out_l = rwkv_readout(y, lat[6], lat[7], gn_g, gn_b, n_ctx)
    out_c = rwkv_readout(y, ctx[6], ctx[7], gn_g, gn_b, 0) if need_ctx else None
    return out_l, out_c


def _rope_tables(L, use_rope):
    lane = np.arange(LANE)
    in_rope = (lane >= MLA_NOPE_DIM) & (lane < MLA_QK_DIM)
    j = lane - MLA_NOPE_DIM
    axis = j // AXIS_ROPE_DIM
    half = AXIS_ROPE_DIM // 2
    f = j % half
    first = (j % AXIS_ROPE_DIM) < half
    inv = ROPE_THETA ** (-jnp.arange(0, AXIS_ROPE_DIM, 2, dtype=jnp.float32) / AXIS_ROPE_DIM)
    t = jnp.arange(L)
    pos = jnp.stack([t // GRID_W, t % GRID_W], axis=-1).astype(jnp.float32)
    ang = pos[:, np.clip(axis, 0, 1)] * inv[np.clip(f, 0, half - 1)][None, :]
    rope_on = jnp.asarray(in_rope)[None, :] & use_rope
    cos = jnp.where(rope_on, jnp.cos(ang), 1.0)
    sin = jnp.where(rope_on, jnp.sin(ang) * jnp.where(jnp.asarray(first), -1.0, 1.0)[None, :], 0.0)
    return jnp.tile(cos, (1, MLA_HEADS)), jnp.tile(sin, (1, MLA_HEADS))


def _mla_prep_kernel(p_ref, qn_ref, wq_ref, kvn_ref, wk_ref, wv_ref, place_ref, qg_ref, kg_ref, hsum_ref, cos_ref, sin_ref,
                     q_ref, k_ref, v_ref):
    bf = jnp.bfloat16
    p = p_ref[0]
    c_q = p[:, :MLA_Q_RANK]
    c_kv = p[:, MLA_Q_RANK:MLA_Q_RANK + MLA_KV_RANK]
    tail = p[:, MLA_Q_RANK + MLA_KV_RANK:]
    cqn = c_q * lax.rsqrt(jnp.mean(c_q * c_q, axis=-1, keepdims=True) + NORM_EPS) * qn_ref[...]
    ckn = c_kv * lax.rsqrt(jnp.mean(c_kv * c_kv, axis=-1, keepdims=True) + NORM_EPS) * kvn_ref[...]
    q = jnp.dot(cqn.astype(bf), wq_ref[...], preferred_element_type=jnp.float32)
    k = jnp.dot(ckn.astype(bf), wk_ref[...], preferred_element_type=jnp.float32) + _split_dot(tail, place_ref[...])
    v_ref[0] = jnp.dot(ckn.astype(bf), wv_ref[...], preferred_element_type=jnp.float32).astype(bf)
    hsum = hsum_ref[...]
    cos, sin = cos_ref[...], sin_ref[...]
    lane = lax.broadcasted_iota(jnp.int32, q.shape, 1)
    first = ((lane - MLA_NOPE_DIM) % AXIS_ROPE_DIM) < (AXIS_ROPE_DIM // 2)
    half = AXIS_ROPE_DIM // 2

    def finish(x, gain):
        x = x * lax.rsqrt(_split_dot(x * x, hsum) * (1.0 / MLA_QK_DIM) + NORM_EPS) * gain
        partner = jnp.where(first, pltpu.roll(x, MLA_PAD_WIDTH - half, 1), pltpu.roll(x, half, 1))
        return x * cos + partner * sin

    q_ref[0] = (finish(q, qg_ref[...]) * (MLA_QK_DIM ** -0.5)).astype(bf)
    k_ref[0] = finish(k, kg_ref[...]).astype(bf)


def mla_prep(p, use_rope, q_norm, w_uq, kv_norm, w_ukv, q_gain, k_gain):
    B, L, _ = p.shape
    T = min(MLA_PREP_ROWS, L)
    H = MLA_HEADS
    pad_cols = lambda w, d: jnp.pad(w.reshape(w.shape[0], H, d), ((0, 0), (0, 0), (0, LANE - d))).reshape(w.shape[0], H * LANE)
    wq = pad_cols(w_uq, MLA_QK_DIM).astype(jnp.bfloat16)
    ukv = w_ukv.reshape(MLA_KV_RANK, H, MLA_NOPE_DIM + MLA_V_DIM)
    wk = pad_cols(ukv[:, :, :MLA_NOPE_DIM].reshape(MLA_KV_RANK, H * MLA_NOPE_DIM), MLA_NOPE_DIM).astype(jnp.bfloat16)
    wv = ukv[:, :, MLA_NOPE_DIM:].reshape(MLA_KV_RANK, H * MLA_V_DIM).astype(jnp.bfloat16)
    lane = np.arange(H * LANE)
    place = jnp.asarray(((lane[None, :] % LANE) - MLA_NOPE_DIM == np.arange(MLA_ROPE_DIM)[:, None]), jnp.bfloat16)
    hsum = jnp.asarray((lane[:, None] // LANE) == (lane[None, :] // LANE), jnp.bfloat16)
    pad_gain = lambda g: jnp.tile(jnp.pad(g, (0, LANE - MLA_QK_DIM)), H).reshape(1, H * LANE)
    cos, sin = _rope_tables(L, use_rope)
    const = lambda a: pl.BlockSpec(a.shape, lambda b, i: (0,) * a.ndim)
    args = (p, q_norm.reshape(1, -1), wq, kv_norm.reshape(1, -1), wk, wv, place, pad_gain(q_gain), pad_gain(k_gain), hsum)
    pos = pl.BlockSpec((T, H * LANE), lambda b, i: (i, 0))
    return pl.pallas_call(
        _mla_prep_kernel,
        grid=(B, L // T),
        in_specs=[pl.BlockSpec((1, T, MLA_PROJ), lambda b, i: (b, i, 0))] + [const(a) for a in args[1:]] + [pos, pos],
        out_specs=[pl.BlockSpec((1, T, H * LANE), lambda b, i: (b, i, 0)), pl.BlockSpec((1, T, H * LANE), lambda b, i: (b, i, 0)),
                   pl.BlockSpec((1, T, MLA_WIDTH), lambda b, i: (b, i, 0))],
        out_shape=[jax.ShapeDtypeStruct((B, L, H * LANE), jnp.bfloat16), jax.ShapeDtypeStruct((B, L, H * LANE), jnp.bfloat16),
                   jax.ShapeDtypeStruct((B, L, MLA_WIDTH), jnp.bfloat16)],
        compiler_params=pltpu.CompilerParams(dimension_semantics=("parallel", "parallel"),
                                             vmem_limit_bytes=VMEM_LIMIT_BYTES),
        name="mla_prep",
    )(*args, cos, sin)


def _attn_kernel(q_ref, k_ref, v_ref, o_ref):
    lane = lax.broadcasted_iota(jnp.int32, (q_ref.shape[1], LANE), 1)
    for pair in range(MLA_HEADS // 2):
        v_pair = v_ref[0, :, pair * LANE:(pair + 1) * LANE]
        outs = []
        for h in (2 * pair, 2 * pair + 1):
            q = q_ref[0, :, h * LANE:(h + 1) * LANE]
            k = k_ref[0, :, h * LANE:(h + 1) * LANE]
            s = lax.dot_general(q, k, (((1,), (1,)), ((), ())), preferred_element_type=jnp.float32)
            e = jnp.exp(s - jnp.max(s, axis=-1, keepdims=True))
            o = jnp.dot(e.astype(jnp.bfloat16), v_pair, preferred_element_type=jnp.float32)
            outs.append(o / jnp.sum(e, axis=-1, keepdims=True))
        o_ref[0, :, pair * LANE:(pair + 1) * LANE] = jnp.where(lane < MLA_V_DIM, outs[0], outs[1])


def attention(q, k, v):
    B, Lq, P = q.shape
    Lk = k.shape[1]
    tq = min(ATTN_Q_ROWS, Lq)
    return pl.pallas_call(
        _attn_kernel,
        grid=(B, Lq // tq),
        in_specs=[pl.BlockSpec((1, tq, P), lambda b, i: (b, i, 0)),
                  pl.BlockSpec((1, Lk, P), lambda b, i: (b, 0, 0)),
                  pl.BlockSpec((1, Lk, MLA_WIDTH), lambda b, i: (b, 0, 0))],
        out_specs=pl.BlockSpec((1, tq, MLA_WIDTH), lambda b, i: (b, i, 0)),
        out_shape=jax.ShapeDtypeStruct((B, Lq, MLA_WIDTH), jnp.float32),
        compiler_params=pltpu.CompilerParams(dimension_semantics=("parallel", "parallel"),
                                             vmem_limit_bytes=VMEM_LIMIT_BYTES),
        name="mla_attention",
    )(q, k, v)


def mla_mixer(p_lat, p_ctx, q_norm, w_uq, kv_norm, w_ukv, q_gain, k_gain, need_ctx):
    prm = (q_norm, w_uq, kv_norm, w_ukv, q_gain, k_gain)
    q_l, k_l, v_l = mla_prep(p_lat, True, *prm)
    q_c, k_c, v_c = mla_prep(p_ctx, False, *prm)
    y_l = attention(q_l, jnp.concatenate([k_l, k_c], axis=1), jnp.concatenate([v_l, v_c], axis=1))
    y_c = attention(q_c, k_c, v_c) if need_ctx else None
    return y_l, y_c


def hyena_filters(L, w1, b1, freq1, w2, b2, freq2, w3, b3):
    tn = jnp.arange(L, dtype=jnp.float32) / L
    bands = jnp.arange(1, HY_POS_BANDS + 1, dtype=jnp.float32)
    ang = 2.0 * math.pi * tn[:, None] * bands[None, :]
    z = jnp.concatenate([tn[:, None], jnp.cos(ang), jnp.sin(ang)], axis=-1)
    h = jnp.sin(freq1 * (z @ w1 + b1))
    h = jnp.sin(freq2 * (h @ w2 + b2))
    h = (h @ w3 + b3).reshape(L, HY_ORDER, 2, HY_WIDTH)
    rates = jnp.abs(jnp.linspace(math.log(HY_DECAY_TARGET) / HY_LONG_DECAY_PCT,
                                 math.log(HY_DECAY_TARGET) / HY_SHORT_DECAY_PCT, HY_WIDTH))
    h = h * jnp.exp(-tn[:, None] * rates[None, :])[:, None, None, :]
    zero = jnp.zeros((1, HY_ORDER, HY_WIDTH), h.dtype)
    h_full = jnp.concatenate([h[:, :, 0], zero, h[:0:-1, :, 1]], axis=0)
    return h_full * lax.rsqrt(jnp.sum(jnp.square(h_full), axis=0, keepdims=True))


def fft_long_conv(u, h_full, bias):
    L = u.shape[1]
    uf = jnp.fft.rfft(u, n=2 * L, axis=1)
    hf = jnp.fft.rfft(h_full, n=2 * L, axis=0)
    y = jnp.fft.irfft(uf * hf[None], n=2 * L, axis=1)[:, :L]
    return y + u * bias


FFT_N1 = 64
FFT_N2 = 128
FFT_N = FFT_N1 * FFT_N2
HY_SEQS = 32


def _dft_tables(seqs):
    n1 = np.arange(FFT_N1)
    n2 = np.arange(FFT_N2)
    f64 = np.exp(-2j * np.pi * np.outer(n1, n1) / FFT_N1)
    f128 = np.exp(-2j * np.pi * np.outer(n2, n2) / FFT_N2)
    tw = np.exp(-2j * np.pi * np.outer(n1, n2) / FFT_N)
    half = FFT_N1 // 2
    fh = f64[:, :half]
    m1 = np.block([[fh.real, -fh.imag], [fh.imag, fh.real]])
    m1f = np.concatenate([f64.real, f64.imag], axis=0)
    m2 = np.block([[f128.real, f128.imag], [-f128.imag, f128.real]])
    m3 = np.block([[f128.real, -f128.imag], [f128.imag, f128.real]]) / FFT_N
    c = np.conj(f64)[:half, :]
    m4 = np.block([[c.real, -c.imag], [c.imag, c.real]])
    bf = lambda a: jnp.asarray(a, jnp.float32).astype(jnp.bfloat16)
    f32 = lambda a: jnp.asarray(a, jnp.float32)
    return dict(m1=bf(m1), m1f=bf(m1f), m2=bf(m2), m3=bf(m3), m4=bf(m4),
                twr_l=f32(np.tile(tw.real, (1, seqs))), twi_l=f32(np.tile(tw.imag, (1, seqs))),
                twr_s=f32(np.tile(tw.real, (seqs, 1))), twi_s=f32(np.tile(tw.imag, (seqs, 1))))


def _spectrum(cols, m1, twr_l, twi_l, m2, R):
    a = jnp.dot(m1, cols.astype(jnp.bfloat16), preferred_element_type=jnp.float32)
    ar, ai = a[:FFT_N1], a[FFT_N1:]
    pr = ar * twr_l - ai * twi_l
    pi = ar * twi_l + ai * twr_l
    lhs = jnp.concatenate(
        [jnp.concatenate([pr[:, r * FFT_N2:(r + 1) * FFT_N2], pi[:, r * FFT_N2:(r + 1) * FFT_N2]], axis=1)
         for r in range(R)], axis=0)
    return jnp.dot(lhs.astype(jnp.bfloat16), m2, preferred_element_type=jnp.float32)


def _filter_fft_kernel(h_ref, m1f_ref, twr_ref, twi_ref, m2_ref, o_ref):
    R = HY_SEQS
    cols = jnp.concatenate([h_ref[r] for r in range(R)], axis=1)
    x = _spectrum(cols, m1f_ref[...], twr_ref[...], twi_ref[...], m2_ref[...], R)
    o_ref[...] = x.reshape(R, FFT_N1, 2 * FFT_N2)


def _hyena_conv_kernel(y_ref, g_ref, hf_ref, bias_ref, m1_ref, twr_l_ref, twi_l_ref, m2_ref, m3_ref,
                       twr_s_ref, twi_s_ref, m4_ref, o_ref):
    R = HY_SEQS
    half = FFT_N1 // 2
    y = [y_ref[0], y_ref[1]]
    for o in range(HY_ORDER):
        top = jnp.concatenate([y[0][r] for r in range(R)], axis=1)
        bot = jnp.concatenate([y[1][r] for r in range(R)], axis=1)
        x = _spectrum(jnp.concatenate([top, bot], axis=0), m1_ref[...], twr_l_ref[...], twi_l_ref[...], m2_ref[...], R)
        hf = hf_ref[o].reshape(R * FFT_N1, 2 * FFT_N2)
        xr, xi = x[:, :FFT_N2], x[:, FFT_N2:]
        hr, hi = hf[:, :FFT_N2], hf[:, FFT_N2:]
        yc = jnp.concatenate([xr * hr - xi * hi, xr * hi + xi * hr], axis=1)
        b = jnp.dot(yc.astype(jnp.bfloat16), m3_ref[...], preferred_element_type=jnp.float32)
        br, bi = b[:, :FFT_N2], b[:, FFT_N2:]
        qr = br * twr_s_ref[...] + bi * twi_s_ref[...]
        qi = bi * twr_s_ref[...] - br * twi_s_ref[...]
        bc = jnp.concatenate(
            [jnp.concatenate([qr[r * FFT_N1:(r + 1) * FFT_N1], qi[r * FFT_N1:(r + 1) * FFT_N1]], axis=0)
             for r in range(R)], axis=1)
        yo = jnp.dot(m4_ref[...], bc.astype(jnp.bfloat16), preferred_element_type=jnp.float32)
        for p in range(2):
            conv = jnp.stack([yo[p * half:(p + 1) * half, r * FFT_N2:(r + 1) * FFT_N2] for r in range(R)], axis=0)
            y[p] = g_ref[o, p] * (conv + y[p] * bias_ref[o])
    o_ref[0] = y[0]
    o_ref[1] = y[1]


def hyena_long_conv(y_t, g_t, h_t, bias):
    B, C, L = y_t.shape
    assert 2 * L == FFT_N and B % 2 == 0 and C % HY_SEQS == 0
    R = HY_SEQS
    half = FFT_N1 // 2
    tb = _dft_tables(R)
    const = lambda a: pl.BlockSpec(a.shape, lambda *_: (0,) * a.ndim)
    hf = pl.pallas_call(
        _filter_fft_kernel,
        grid=(HY_ORDER * C // R,),
        in_specs=[pl.BlockSpec((R, FFT_N1, FFT_N2), lambda i: (i, 0, 0)),
                  const(tb['m1f']), const(tb['twr_l']), const(tb['twi_l']), const(tb['m2'])],
        out_specs=pl.BlockSpec((R, FFT_N1, 2 * FFT_N2), lambda i: (i, 0, 0)),
        out_shape=jax.ShapeDtypeStruct((HY_ORDER * C, FFT_N1, 2 * FFT_N2), jnp.float32),
        compiler_params=pltpu.CompilerParams(dimension_semantics=("parallel",), vmem_limit_bytes=VMEM_LIMIT_BYTES),
        name="hyena_filter_fft",
    )(h_t.reshape(HY_ORDER * C, FFT_N1, FFT_N2), tb['m1f'], tb['twr_l'], tb['twi_l'], tb['m2'])
    hf = hf.reshape(HY_ORDER, C, FFT_N1, 2 * FFT_N2)
    out = pl.pallas_call(
        _hyena_conv_kernel,
        grid=(B // 2, C // R),
        in_specs=[pl.BlockSpec((2, R, half, FFT_N2), lambda b, c: (b, c, 0, 0)),
                  pl.BlockSpec((HY_ORDER, 2, R, half, FFT_N2), lambda b, c: (0, b, c, 0, 0)),
                  pl.BlockSpec((HY_ORDER, R, FFT_N1, 2 * FFT_N2), lambda b, c: (0, c, 0, 0)),
                  pl.BlockSpec((HY_ORDER, R, 1, 1), lambda b, c: (0, c, 0, 0)),
                  const(tb['m1']), const(tb['twr_l']), const(tb['twi_l']), const(tb['m2']), const(tb['m3']),
                  const(tb['twr_s']), const(tb['twi_s']), const(tb['m4'])],
        out_specs=pl.BlockSpec((2, R, half, FFT_N2), lambda b, c: (b, c, 0, 0)),
        out_shape=jax.ShapeDtypeStruct((B, C, half, FFT_N2), jnp.float32),
        compiler_params=pltpu.CompilerParams(dimension_semantics=("parallel", "parallel"),
                                             vmem_limit_bytes=VMEM_LIMIT_BYTES),
        name="hyena_conv",
    )(y_t.reshape(B, C, half, FFT_N2), g_t.reshape(HY_ORDER, B, C, half, FFT_N2), hf,
      bias.reshape(HY_ORDER, C, 1, 1), tb['m1'], tb['twr_l'], tb['twi_l'], tb['m2'], tb['m3'],
      tb['twr_s'], tb['twi_s'], tb['m4'])
    return out.reshape(B, C, L)


def hyena_mixer(p, conv_w, w1, b1, freq1, w2, b2, freq2, w3, b3, bias):
    B, L = p.shape[:2]
    z = short_conv(p, conv_w)
    h_full = hyena_filters(L, w1, b1, freq1, w2, b2, freq2, w3, b3)
    if 2 * L == FFT_N:
        g_t = jnp.transpose(z[..., :HY_ORDER * HY_WIDTH].reshape(B, L, HY_ORDER, HY_WIDTH), (2, 0, 3, 1))
        y_t = jnp.swapaxes(z[..., HY_ORDER * HY_WIDTH:], 1, 2)
        y_t = hyena_long_conv(y_t, g_t, jnp.transpose(h_full, (1, 2, 0)), bias)
        return jnp.swapaxes(y_t, 1, 2)
    gates = (z[..., :HY_WIDTH], z[..., HY_WIDTH:2 * HY_WIDTH])
    y = z[..., 2 * HY_WIDTH:]
    for o in range(HY_ORDER):
        y = gates[o] * fft_long_conv(y, h_full[:, o], bias[o])
    return y


SC_CORES = 2
SC_SUBCORES = 16
SC_LANES = 16
SC_WORKERS = SC_CORES * SC_SUBCORES
PEER_SLOTS = PEER_HEADS * PEER_TOPK
PEER_GATHER_ROWS = 32
PEER_GATHERS = PEER_SLOTS // PEER_GATHER_ROWS
PEER_ACC_VREGS = 8
PEER_ROW_BUFFERS = 4
PEER_ROW_WORDS = D_MODEL // 2
HI_MASK = -65536


def pack_expert_table(t):
    b = lax.bitcast_convert_type(t.astype(jnp.bfloat16), jnp.uint16).astype(jnp.uint32)
    return lax.bitcast_convert_type(b[:, :PEER_ROW_WORDS] | (b[:, PEER_ROW_WORDS:] << 16), jnp.int32)


def _sc_peer_phase(phase, tpw):
    NBUF = PEER_ROW_BUFFERS
    AHEAD = NBUF - 1
    HW = PEER_ROW_WORDS

    def run(base, table_hbm, idx_hbm, aux_hbm, out_hbm, idx_v, aux_v, rows_v, out_v, sem_r, sem_i, sem_o):

        def gather(p, c, b):
            return pltpu.make_async_copy(table_hbm.at[idx_v.at[p, c]], rows_v.at[b], sem_r.at[b])

        def load_meta(t, p):
            return (pltpu.make_async_copy(idx_hbm.at[t], idx_v.at[p], sem_i.at[p]),
                    pltpu.make_async_copy(aux_hbm.at[t], aux_v.at[p], sem_i.at[p]))

        def store_out(t, p):
            return pltpu.make_async_copy(out_v.at[p], out_hbm.at[t], sem_o.at[p])

        def halves(word):
            return (plsc.bitcast(lax.shift_left(word, 16), jnp.float32), plsc.bitcast(word & HI_MASK, jnp.float32))

        def compute(p, c, b):
            if phase == "dot":
                lane = lax.iota(jnp.int32, SC_LANES)
                vec = jnp.zeros((SC_LANES,), jnp.float32)
                groups_per_vec = SC_LANES // PEER_ACC_VREGS
                for g in range(PEER_GATHER_ROWS // PEER_ACC_VREGS):
                    def body(cc, accs):
                        x_lo = aux_v[p, pl.ds(cc * SC_LANES, SC_LANES)]
                        x_hi = aux_v[p, pl.ds(HW + cc * SC_LANES, SC_LANES)]
                        out = []
                        for r in range(PEER_ACC_VREGS):
                            lo, hi = halves(rows_v[b, g * PEER_ACC_VREGS + r, pl.ds(cc * SC_LANES, SC_LANES)])
                            out.append(accs[r] + lo * x_lo + hi * x_hi)
                        return tuple(out)
                    accs = lax.fori_loop(0, HW // SC_LANES, body,
                                         tuple(jnp.zeros((SC_LANES,), jnp.float32) for _ in range(PEER_ACC_VREGS)))
                    for r in range(PEER_ACC_VREGS):
                        vec = jnp.where(lane == (g % groups_per_vec) * PEER_ACC_VREGS + r, jnp.sum(accs[r]), vec)
                    if g % groups_per_vec == groups_per_vec - 1:
                        out_v[p, pl.ds(c * PEER_GATHER_ROWS + (g // groups_per_vec) * SC_LANES, SC_LANES)] = vec
            else:
                words = PEER_ACC_VREGS // 2
                for db in range(HW // (words * SC_LANES)):
                    def body(kk, accs):
                        wv = plsc.load_gather(aux_v.at[p], [jnp.full((SC_LANES,), c * PEER_GATHER_ROWS + kk, jnp.int32)])
                        out = []
                        for j in range(words):
                            lo, hi = halves(rows_v[b, kk, pl.ds((db * words + j) * SC_LANES, SC_LANES)])
                            out += [accs[2 * j] + lo * wv, accs[2 * j + 1] + hi * wv]
                        return tuple(out)
                    if c == 0:
                        init = tuple(jnp.zeros((SC_LANES,), jnp.float32) for _ in range(2 * words))
                    else:
                        init = tuple(out_v[p, pl.ds(half * HW + (db * words + j) * SC_LANES, SC_LANES)]
                                     for j in range(words) for half in range(2))
                    accs = lax.fori_loop(0, PEER_GATHER_ROWS, body, init)
                    for j in range(words):
                        out_v[p, pl.ds((db * words + j) * SC_LANES, SC_LANES)] = accs[2 * j]
                        out_v[p, pl.ds(HW + (db * words + j) * SC_LANES, SC_LANES)] = accs[2 * j + 1]

        for d in load_meta(base, 0):
            d.start()
        for d in load_meta(base, 0):
            d.wait()
        for c in range(AHEAD):
            gather(0, c, c % NBUF).start()

        @pl.loop(0, tpw // 2)
        def _(i2):
            for p in range(2):
                i = i2 * 2 + p
                t = base + i
                nxt = base + jnp.minimum(i + 1, tpw - 1)
                for d in load_meta(nxt, 1 - p):
                    d.start()

                @pl.when(i2 > 0)
                def _():
                    store_out(t, p).wait()

                for c in range(PEER_GATHERS):
                    ahead = c + AHEAD
                    if ahead < PEER_GATHERS:
                        gather(p, ahead, ahead % NBUF).start()
                    else:
                        if ahead == PEER_GATHERS:
                            for d in load_meta(nxt, 1 - p):
                                d.wait()
                        gather(1 - p, ahead - PEER_GATHERS, ahead % NBUF).start()
                    gather(p, c, c % NBUF).wait()
                    compute(p, c, c % NBUF)
                store_out(t, p).start()

        for c in range(AHEAD):
            gather(0, c, c % NBUF).wait()
        for p in range(2):
            store_out(base, p).wait()

    return run


def _sc_tokens_per_worker(N):
    assert N % (2 * SC_WORKERS) == 0 and PEER_GATHERS % PEER_ROW_BUFFERS == 0
    return N // SC_WORKERS


_SC_AUX = {"dot": (D_MODEL,), "wsum": (PEER_SLOTS,)}
_SC_OUT = {"dot": (PEER_SLOTS,), "wsum": (D_MODEL,)}


def _sc_scratch(phases):
    s = [pltpu.VMEM((2, PEER_GATHERS, PEER_GATHER_ROWS), jnp.int32),
         pltpu.VMEM((PEER_ROW_BUFFERS, PEER_GATHER_ROWS, PEER_ROW_WORDS), jnp.int32),
         pltpu.SemaphoreType.DMA((PEER_ROW_BUFFERS,)), pltpu.SemaphoreType.DMA((2,)), pltpu.SemaphoreType.DMA((2,))]
    for ph in phases:
        s += [pltpu.VMEM((2,) + _SC_AUX[ph], jnp.float32), pltpu.VMEM((2,) + _SC_OUT[ph], jnp.float32)]
    return s


def _sc_peer(table, idx, aux, phase):
    N = idx.shape[0]
    tpw = _sc_tokens_per_worker(N)
    run = _sc_peer_phase(phase, tpw)

    @functools.partial(
        pl.kernel, mesh=plsc.VectorSubcoreMesh(core_axis_name="c", subcore_axis_name="s"),
        out_type=jax.ShapeDtypeStruct((N,) + _SC_OUT[phase], jnp.float32),
        compiler_params=pltpu.CompilerParams(needs_layout_passes=False),
        scratch_types=_sc_scratch([phase]),
    )
    def k(table_hbm, idx_hbm, aux_hbm, out_hbm, idx_v, rows_v, sem_r, sem_i, sem_o, aux_v, out_v):
        base = (lax.axis_index("s") * SC_CORES + lax.axis_index("c")) * tpw
        run(base, table_hbm, idx_hbm, aux_hbm, out_hbm, idx_v, aux_v, rows_v, out_v, sem_r, sem_i, sem_o)

    return k(table, idx.reshape(N, PEER_GATHERS, PEER_GATHER_ROWS), aux)


def _sc_peer_pair(table_v, idx_a, w_a, table_u, idx_b, h_b):
    N = idx_a.shape[0]
    assert idx_b.shape[0] == N
    tpw = _sc_tokens_per_worker(N)
    run_wsum = _sc_peer_phase("wsum", tpw)
    run_dot = _sc_peer_phase("dot", tpw)

    @functools.partial(
        pl.kernel, mesh=plsc.VectorSubcoreMesh(core_axis_name="c", subcore_axis_name="s"),
        out_type=(jax.ShapeDtypeStruct((N,) + _SC_OUT["wsum"], jnp.float32),
                  jax.ShapeDtypeStruct((N,) + _SC_OUT["dot"], jnp.float32)),
        compiler_params=pltpu.CompilerParams(needs_layout_passes=False),
        scratch_types=_sc_scratch(["wsum", "dot"]),
    )
    def k(tv_hbm, ia_hbm, wa_hbm, tu_hbm, ib_hbm, hb_hbm, outa_hbm, outb_hbm,
          idx_v, rows_v, sem_r, sem_i, sem_o, w_v, outa_v, h_v, outb_v):
        base = (lax.axis_index("s") * SC_CORES + lax.axis_index("c")) * tpw
        run_wsum(base, tv_hbm, ia_hbm, wa_hbm, outa_hbm, idx_v, w_v, rows_v, outa_v, sem_r, sem_i, sem_o)
        run_dot(base, tu_hbm, ib_hbm, hb_hbm, outb_hbm, idx_v, h_v, rows_v, outb_v, sem_r, sem_i, sem_o)

    shp = (N, PEER_GATHERS, PEER_GATHER_ROWS)
    return k(table_v, idx_a.reshape(shp), w_a, table_u, idx_b.reshape(shp), h_b)


PEER_TOKENS = 256
INT_BIG = 2 ** 30


def _extract_topk(cand_ref, ids_ref, val_out_ref, id_out_ref, row0):
    def body(r, carry):
        c = cand_ref[...]
        ids = ids_ref[...]
        m = jnp.max(c, axis=0, keepdims=True)
        sel = jnp.min(jnp.where(c == m, ids, INT_BIG), axis=0, keepdims=True)
        cand_ref[...] = jnp.where(ids == sel, -jnp.inf, c)
        val_out_ref[pl.ds(row0 + r, 1), :] = m
        id_out_ref[pl.ds(row0 + r, 1), :] = sel
        return carry
    lax.fori_loop(0, PEER_TOPK, body, 0)


def _peer_retrieve_kernel(x_ref, gain_ref, scale_ref, shift_ref, wq_ref, keys_ref,
                          h_ref, idx_out_ref, gate_out_ref,
                          s_ref, ids1_ref, sv_ref, si_ref, cand_ref, cid_ref, ts_ref, idx_ref, gate_ref):
    x = x_ref[0]
    y = x * lax.rsqrt(jnp.mean(x * x, axis=-1, keepdims=True) + NORM_EPS)
    h = (y * gain_ref[...]) * (1.0 + scale_ref[0]) + shift_ref[0]
    h_ref[0] = h
    q = jnp.dot(h.astype(jnp.bfloat16), wq_ref[...], preferred_element_type=jnp.float32)
    T = PEER_TOKENS
    K = PEER_TOPK
    ids1_ref[...] = lax.broadcasted_iota(jnp.int32, (PEER_N_KEYS, T), 0)
    for hd in range(PEER_HEADS):
        for p in range(2):
            hp = hd * 2 + p
            qs = q[:, hp * PEER_HALF:(hp + 1) * PEER_HALF].astype(jnp.bfloat16)
            s_ref[...] = lax.dot_general(keys_ref[hp], qs, (((1,), (1,)), ((), ())),
                                         preferred_element_type=jnp.float32)
            _extract_topk(s_ref, ids1_ref, sv_ref, si_ref, p * K)
        for i in range(K):
            cand_ref[i * K:(i + 1) * K, :] = sv_ref[i:i + 1, :] + sv_ref[K:2 * K, :]
            cid_ref[i * K:(i + 1) * K, :] = si_ref[i:i + 1, :] * PEER_N_KEYS + si_ref[K:2 * K, :]
        _extract_topk(cand_ref, cid_ref, ts_ref, idx_ref, hd * K)
        ts = ts_ref[hd * K:(hd + 1) * K, :]
        e = jnp.exp(ts - jnp.max(ts, axis=0, keepdims=True))
        gate_ref[hd * K:(hd + 1) * K, :] = e / jnp.sum(e, axis=0, keepdims=True)
    idx_out_ref[...] = idx_ref[...].T
    gate_out_ref[...] = gate_ref[...].T


def peer_retrieve(x, gain, scale, shift, w_q, sub_keys):
    B, L, D = x.shape
    T = PEER_TOKENS
    nt = L // T
    keys = sub_keys.reshape(PEER_HEADS * 2, PEER_N_KEYS, PEER_HALF).astype(jnp.bfloat16)
    return pl.pallas_call(
        _peer_retrieve_kernel,
        grid=(B, nt),
        in_specs=[
            pl.BlockSpec((1, T, D), lambda b, i: (b, i, 0)),
            pl.BlockSpec((1, D), lambda b, i: (0, 0)),
            pl.BlockSpec((1, 1, D), lambda b, i: (b, 0, 0)),
            pl.BlockSpec((1, 1, D), lambda b, i: (b, 0, 0)),
            pl.BlockSpec((D, PEER_HEADS * 2 * PEER_HALF), lambda b, i: (0, 0)),
            pl.BlockSpec((PEER_HEADS * 2, PEER_N_KEYS, PEER_HALF), lambda b, i: (0, 0, 0)),
        ],
        out_specs=[
            pl.BlockSpec((1, T, D), lambda b, i: (b, i, 0)),
            pl.BlockSpec((T, PEER_SLOTS), lambda b, i: (b * nt + i, 0)),
            pl.BlockSpec((T, PEER_SLOTS), lambda b, i: (b * nt + i, 0)),
        ],
        out_shape=[
            jax.ShapeDtypeStruct((B, L, D), jnp.float32),
            jax.ShapeDtypeStruct((B * L, PEER_SLOTS), jnp.int32),
            jax.ShapeDtypeStruct((B * L, PEER_SLOTS), jnp.float32),
        ],
        scratch_shapes=[
            pltpu.VMEM((PEER_N_KEYS, T), jnp.float32),
            pltpu.VMEM((PEER_N_KEYS, T), jnp.int32),
            pltpu.VMEM((2 * PEER_TOPK, T), jnp.float32),
            pltpu.VMEM((2 * PEER_TOPK, T), jnp.int32),
            pltpu.VMEM((PEER_TOPK * PEER_TOPK, T), jnp.float32),
            pltpu.VMEM((PEER_TOPK * PEER_TOPK, T), jnp.int32),
            pltpu.VMEM((PEER_SLOTS, T), jnp.float32),
            pltpu.VMEM((PEER_SLOTS, T), jnp.int32),
            pltpu.VMEM((PEER_SLOTS, T), jnp.float32),
        ],
        compiler_params=pltpu.CompilerParams(dimension_semantics=("parallel", "parallel"),
                                             vmem_limit_bytes=VMEM_LIMIT_BYTES),
        name="peer_retrieve",
    )(x, gain.reshape(1, D), scale, shift, w_q.astype(jnp.bfloat16), keys)


PEER_ACT_ROWS = 1024


def _peer_act_kernel(dots_ref, gate_ref, w_ref):
    a = dots_ref[...]
    w_ref[...] = gate_ref[...] * (0.5 * a * (1.0 + lax.erf(a * (2.0 ** -0.5))))


def peer_act(dots, gate):
    N = dots.shape[0]
    T = min(PEER_ACT_ROWS, N)
    spec = pl.BlockSpec((T, PEER_SLOTS), lambda i: (i, 0))
    return pl.pallas_call(
        _peer_act_kernel,
        grid=(N // T,),
        in_specs=[spec, spec],
        out_specs=spec,
        out_shape=jax.ShapeDtypeStruct((N, PEER_SLOTS), jnp.float32),
        compiler_params=pltpu.CompilerParams(dimension_semantics=("parallel",)),
        name="peer_act",
    )(dots, gate)


def peer_ffn(x, gain, scale, shift, w_q, sub_keys, exp_u, exp_v):
    B, L, D = x.shape
    N = B * L
    h, e_idx, gate = peer_retrieve(x, gain, scale, shift, w_q, sub_keys)
    dots = _sc_peer(exp_u, e_idx, h.reshape(N, D), "dot")
    return _sc_peer(exp_v, e_idx, peer_act(dots, gate), "wsum").reshape(B, L, D)


def _mix_and_retrieve(li, x, c, ctx, c_ctx, mod_w, mod_b, mix_norm, w_in, w_out, rw_conv, rw_decay_up, rw_decay0, rw_a_up, rw_a0, rw_gate_up, rw_k_k, rw_k_a, rw_r_k, rw_gn_g, rw_gn_b, mla_q_norm, mla_w_uq, mla_kv_norm, mla_w_ukv, mla_q_gain, mla_k_gain, hy_conv, hy_w1, hy_b1, hy_freq1, hy_w2, hy_b2, hy_freq2, hy_w3, hy_b3, hy_bias, ffn_norm, peer_wq, peer_keys, peer_u, peer_v):
    B, L, D = x.shape
    s_rw, s_mla = RW_PROJ, RW_PROJ + MLA_PROJ
    need_ctx = li < DEPTH - 1
    mod_l = (jax.nn.silu(c) @ mod_w[li] + mod_b[li])[:, None, :]
    mod_c = (jax.nn.silu(c_ctx) @ mod_w[li] + mod_b[li])[None, None, :]
    shm_l, scm_l, gm_l, shf_l, scf_l, gf_l = jnp.split(mod_l, N_MOD, axis=-1)
    shm_c, scm_c, gm_c, shf_c, scf_c, gf_c = jnp.split(mod_c, N_MOD, axis=-1)

    p_l = norm_mod_proj(x, mix_norm[li], scm_l, shm_l, w_in[li], 512)
    p_c = norm_mod_proj(ctx, mix_norm[li], jnp.broadcast_to(scm_c, (B, 1, D)),
                        jnp.broadcast_to(shm_c, (B, 1, D)), w_in[li], 256)
    rw_l, rw_c = rwkv7_mixer(p_l[..., :s_rw], p_c[..., :s_rw], rw_conv[li], rw_decay_up[li], rw_decay0[li],
                             rw_a_up[li], rw_a0[li], rw_gate_up[li], rw_k_k[li], rw_k_a[li], rw_r_k[li],
                             rw_gn_g[li], rw_gn_b[li], need_ctx)
    ml_l, ml_c = mla_mixer(p_l[..., s_rw:s_mla], p_c[..., s_rw:s_mla], mla_q_norm[li], mla_w_uq[li],
                           mla_kv_norm[li], mla_w_ukv[li], mla_q_gain[li], mla_k_gain[li], need_ctx)
    hy_prm = (hy_conv[li], hy_w1[li], hy_b1[li], hy_freq1[li], hy_w2[li], hy_b2[li], hy_freq2[li],
              hy_w3[li], hy_b3[li], hy_bias[li])
    hy_l = hyena_mixer(p_l[..., s_mla:], *hy_prm)
    x = x + gm_l * (jnp.concatenate([rw_l, ml_l, hy_l], axis=-1) @ w_out[li])
    if need_ctx:
        hy_c = hyena_mixer(p_c[..., s_mla:], *hy_prm)
        ctx = ctx + gm_c * (jnp.concatenate([rw_c, ml_c, hy_c], axis=-1) @ w_out[li])
        ctx = ctx + gf_c * peer_ffn(ctx, ffn_norm[li], jnp.broadcast_to(scf_c, (B, 1, D)),
                                    jnp.broadcast_to(shf_c, (B, 1, D)),
                                    peer_wq[li], peer_keys[li], peer_u[li], peer_v[li])
    h, e_idx, gate = peer_retrieve(x, ffn_norm[li], scf_l, shf_l, peer_wq[li], peer_keys[li])
    return x, ctx, gf_l, h.reshape(B * L, D), e_idx, gate


BATCH_GROUPS = 4


def kernel(x, c, ctx, c_ctx, mod_w, mod_b, mix_norm, w_in, w_out, rw_conv, rw_decay_up, rw_decay0, rw_a_up, rw_a0, rw_gate_up, rw_k_k, rw_k_a, rw_r_k, rw_gn_g, rw_gn_b, mla_q_norm, mla_w_uq, mla_kv_norm, mla_w_ukv, mla_q_gain, mla_k_gain, hy_conv, hy_w1, hy_b1, hy_freq1, hy_w2, hy_b2, hy_freq2, hy_w3, hy_b3, hy_bias, ffn_norm, peer_wq, peer_keys, peer_u, peer_v):
    params = (mod_w, mod_b, mix_norm, w_in, w_out, rw_conv, rw_decay_up, rw_decay0, rw_a_up, rw_a0, rw_gate_up,
              rw_k_k, rw_k_a, rw_r_k, rw_gn_g, rw_gn_b, mla_q_norm, mla_w_uq, mla_kv_norm, mla_w_ukv, mla_q_gain,
              mla_k_gain, hy_conv, hy_w1, hy_b1, hy_freq1, hy_w2, hy_b2, hy_freq2, hy_w3, hy_b3, hy_bias,
              ffn_norm, peer_wq, peer_keys)
    peer_u = [pack_expert_table(peer_u[li]) for li in range(DEPTH)]
    peer_v = [pack_expert_table(peer_v[li]) for li in range(DEPTH)]
    params = params + (peer_u, peer_v)
    G = BATCH_GROUPS
    bg = x.shape[0] // G
    L, D = x.shape[1:]
    xs = [x[g * bg:(g + 1) * bg] for g in range(G)]
    cs = [c[g * bg:(g + 1) * bg] for g in range(G)]
    ctxs = [ctx[g * bg:(g + 1) * bg] for g in range(G)]
    stages = [(li, g) for li in range(DEPTH) for g in range(G)]
    prev = None
    token = None
    for li, g in stages:
        ins = (xs[g], ctxs[g])
        if token is not None:
            token, ins = lax.optimization_barrier((token, ins))
        xm, ctxs[g], gf, h, e_idx, gate = _mix_and_retrieve(li, ins[0], cs[g], ins[1], c_ctx, *params)
        token = gate
        if prev is None:
            dots = _sc_peer(peer_u[li], e_idx, h, "dot")
        else:
            pl_, pg, pxm, pgf, pidx, pgate, pdots = prev
            token, (pdots, pgate) = lax.optimization_barrier((token, (pdots, pgate)))
            w = peer_act(pdots, pgate)
            token = w
            out, dots = _sc_peer_pair(peer_v[pl_], pidx, w, peer_u[li], e_idx, h)
            xs[pg] = pxm + pgf * out.reshape(bg, L, D)
        prev = (li, g, xm, gf, e_idx, gate, dots)
    pl_, pg, pxm, pgf, pidx, pgate, pdots = prev
    xs[pg] = pxm + pgf * _sc_peer(peer_v[pl_], pidx, peer_act(pdots, pgate), "wsum").reshape(bg, L, D)
    return jnp.concatenate(xs, axis=0)
```

```python
import functools
import math

import jax
import jax.numpy as jnp
import numpy as np
from jax import lax
from jax.experimental import pallas as pl
from jax.experimental.pallas import tpu as pltpu
from jax.experimental.pallas import tpu_sc as plsc

D_MODEL = 1024
DEPTH = 2
GRID_W = 64
N_MOD = 6
NORM_EPS = 1e-6

RW_HEADS = 6
RW_HEAD_DIM = 64
RW_WIDTH = RW_HEADS * RW_HEAD_DIM
RW_DECAY_RANK = 64
RW_A_RANK = 64
RW_GATE_RANK = 128
RW_DECAY_SCALE = 0.6065306597
RW_GN_EPS = 64e-5
L2_EPS = 1e-12

MLA_HEADS = 6
MLA_Q_RANK = 256
MLA_KV_RANK = 128
MLA_NOPE_DIM = 64
MLA_ROPE_DIM = 32
MLA_V_DIM = 64
MLA_QK_DIM = MLA_NOPE_DIM + MLA_ROPE_DIM
MLA_WIDTH = MLA_HEADS * MLA_V_DIM
AXIS_ROPE_DIM = MLA_ROPE_DIM // 2
ROPE_THETA = 10000.0

HY_WIDTH = 256
HY_ORDER = 2
HY_POS_BANDS = 16
HY_SHORT_DECAY_PCT = 0.3
HY_LONG_DECAY_PCT = 1.5
HY_DECAY_TARGET = 1e-2

PEER_HEADS = 8
PEER_N_KEYS = 128
PEER_TOPK = 16
PEER_QUERY_DIM = 256
PEER_HALF = PEER_QUERY_DIM // 2

RW_PROJ = 3 * RW_WIDTH + RW_DECAY_RANK + RW_A_RANK + RW_GATE_RANK
MLA_PROJ = MLA_Q_RANK + MLA_KV_RANK + MLA_ROPE_DIM
HY_PROJ = (HY_ORDER + 1) * HY_WIDTH
IN_PROJ = RW_PROJ + MLA_PROJ + HY_PROJ
MIX_WIDTH = RW_WIDTH + MLA_WIDTH + HY_WIDTH

VMEM_LIMIT_BYTES = 48 * 1024 * 1024


def _norm_mod_proj_kernel(x_ref, gain_ref, scale_ref, shift_ref, w_ref, o_ref):
    x = x_ref[0]
    y = x * lax.rsqrt(jnp.mean(x * x, axis=-1, keepdims=True) + NORM_EPS)
    y = y * gain_ref[...]
    y = y * (1.0 + scale_ref[0]) + shift_ref[0]
    o_ref[0] = jnp.dot(y.astype(jnp.bfloat16), w_ref[...], preferred_element_type=jnp.float32)


def norm_mod_proj(x, gain, scale, shift, w, block_rows):
    B, L, D = x.shape
    N = w.shape[1]
    return pl.pallas_call(
        _norm_mod_proj_kernel,
        grid=(B, L // block_rows),
        in_specs=[
            pl.BlockSpec((1, block_rows, D), lambda b, i: (b, i, 0)),
            pl.BlockSpec((1, D), lambda b, i: (0, 0)),
            pl.BlockSpec((1, 1, D), lambda b, i: (b, 0, 0)),
            pl.BlockSpec((1, 1, D), lambda b, i: (b, 0, 0)),
            pl.BlockSpec((D, N), lambda b, i: (0, 0)),
        ],
        out_specs=pl.BlockSpec((1, block_rows, N), lambda b, i: (b, i, 0)),
        out_shape=jax.ShapeDtypeStruct((B, L, N), jnp.float32),
        compiler_params=pltpu.CompilerParams(
            dimension_semantics=("parallel", "parallel"), vmem_limit_bytes=VMEM_LIMIT_BYTES),
        name="norm_mod_proj",
    )(x, gain.reshape(1, D), scale, shift, w.astype(jnp.bfloat16))


RW_CHUNK = 64


def _rwkv_chunk_kernel(r_ref, kk_ref, v_ref, lw_ref, akk_ref, kr_ref, y_ref, h_ref):
    d = pl.program_id(0)
    n = pl.program_id(2)

    @pl.when(n == 0)
    def _():
        h_ref[...] = jnp.zeros_like(h_ref)

    C = RW_CHUNK
    row = lax.broadcasted_iota(jnp.int32, (C, C), 0)
    col = lax.broadcasted_iota(jnp.int32, (C, C), 1)
    lag = (row - col) * (1 - 2 * d)
    before = lag > 0
    upto = lag >= 0
    tri = upto.astype(jnp.float32)
    eye = (row == col).astype(jnp.float32)
    bf = jnp.bfloat16
    f32 = jnp.float32

    def mm(a, b):
        return jnp.dot(a.astype(bf), b.astype(bf), preferred_element_type=f32)

    def mm_nt(a, b):
        return lax.dot_general(a.astype(bf), b.astype(bf), (((1,), (1,)), ((), ())), preferred_element_type=f32)

    def mm_tn(a, b):
        return lax.dot_general(a.astype(bf), b.astype(bf), (((0,), (0,)), ((), ())), preferred_element_type=f32)

    hs = range(RW_HEADS)
    HD = RW_HEAD_DIM
    heads = lambda t: [t[:, h * HD:(h + 1) * HD] for h in hs]
    r = heads(r_ref[0])
    kk = heads(kk_ref[0])
    v = heads(v_ref[0])
    lw = heads(lw_ref[0, 0])
    akk = heads(akk_ref[0, 0])
    kr = heads(kr_ref[0, 0])
    G = [jnp.dot(tri, lw[h], preferred_element_type=f32, precision=lax.Precision.HIGHEST) for h in hs]
    gtot = [jnp.sum(lw[h], axis=0, keepdims=True) for h in hs]
    Einv = [jnp.exp(-G[h]) for h in hs]
    At = [-kk[h] * jnp.exp(G[h] - lw[h]) for h in hs]
    Rt = [r[h] * jnp.exp(G[h]) for h in hs]
    Bt = [akk[h] * Einv[h] for h in hs]
    Kt = [kr[h] * Einv[h] for h in hs]
    X = [mm_nt(jnp.concatenate([At[h], Rt[h]], axis=0), jnp.concatenate([Bt[h], Kt[h]], axis=0)) for h in hs]
    M_ab = [jnp.where(before, X[h][:C, :C], 0.0) for h in hs]
    M_ak = [jnp.where(before, X[h][:C, C:], 0.0) for h in hs]
    A_rb = [jnp.where(upto, X[h][C:, :C], 0.0) for h in hs]
    A_rk = [jnp.where(upto, X[h][C:, C:], 0.0) for h in hs]
    MV = [mm(M_ak[h], v[h]) for h in hs]
    Mp = M_ab
    T = [eye + Mp[h] for h in hs]
    for _ in range(5):
        Mp = [jnp.dot(Mp[h], Mp[h], preferred_element_type=f32) for h in hs]
        T = [T[h] + jnp.dot(T[h], Mp[h], preferred_element_type=f32) for h in hs]
    WU = [jnp.dot(T[h], jnp.concatenate([At[h], MV[h]], axis=1), preferred_element_type=f32) for h in hs]
    H0 = [h_ref[h] for h in hs]
    Ehat = [jnp.exp(gtot[h] - G[h]) for h in hs]
    Om = [Rt[h] + mm(A_rb[h], WU[h][:, :HD]) for h in hs]
    Y0 = [mm(A_rb[h], WU[h][:, HD:]) + mm(A_rk[h], v[h]) for h in hs]
    BW = [mm_tn(akk[h] * Ehat[h], WU[h]) for h in hs]
    KV = [mm_tn(kr[h] * Ehat[h], v[h]) for h in hs]
    y_ref[0, 0] = jnp.concatenate([jnp.dot(Om[h], H0[h], preferred_element_type=f32) + Y0[h] for h in hs], axis=1)
    for h in hs:
        P = eye * jnp.exp(gtot[h]) + BW[h][:, :HD]
        h_ref[h] = jnp.dot(P, H0[h], preferred_element_type=f32) + BW[h][:, HD:] + KV[h]


def rwkv_chunked(r, kk, v, lw, akk, kr, n_ctx):
    B, T, W = r.shape
    H = W // RW_HEAD_DIM
    nc = n_ctx // RW_CHUNK
    nt = T // RW_CHUNK

    def chunk_of(d, n):
        bwd = jnp.where(n < nc, nc - 1 - n, nt - 1 - (n - nc))
        return jnp.where(d == 0, n, bwd)

    spec1 = pl.BlockSpec((1, RW_CHUNK, W), lambda d, b, n: (b, chunk_of(d, n), 0))
    spec2 = pl.BlockSpec((1, 1, RW_CHUNK, W), lambda d, b, n: (d, b, chunk_of(d, n), 0))
    return pl.pallas_call(
        _rwkv_chunk_kernel,
        grid=(2, B, nt),
        in_specs=[spec1, spec1, spec1, spec2, spec2, spec2],
        out_specs=spec2,
        out_shape=jax.ShapeDtypeStruct((2, B, T, W), jnp.float32),
        scratch_shapes=[pltpu.VMEM((H, RW_HEAD_DIM, RW_HEAD_DIM), jnp.float32)],
        compiler_params=pltpu.CompilerParams(dimension_semantics=("parallel", "parallel", "arbitrary")),
        name="rwkv_chunked",
    )(r, kk, v, lw, akk, kr)


def short_conv(x, w):
    xp = jnp.pad(x, ((0, 0), (1, 1), (0, 0)))
    return xp[:, :-2] * w[0] + xp[:, 1:-1] * w[1] + xp[:, 2:] * w[2]


LANE = 128
RW_PREP_ROWS = 256
MLA_PAD_WIDTH = MLA_HEADS * LANE
MLA_PREP_ROWS = 256
ATTN_Q_ROWS = 512


def _split_dot(x, m):
    hi = x.astype(jnp.bfloat16)
    lo = (x - hi.astype(jnp.float32)).astype(jnp.bfloat16)
    return (jnp.dot(hi, m, preferred_element_type=jnp.float32) + jnp.dot(lo, m, preferred_element_type=jnp.float32))


def _rwkv_prep_kernel(z_ref, wda_ref, d0_ref, a0_ref, gup_ref, kk_ref_w, ka_ref, rk_ref, hsum_ref,
                      r_ref, kk_ref, v_ref, lw_ref, akk_ref, kr_ref, g_ref, bonus_ref):
    W = RW_WIDTH
    bf = jnp.bfloat16
    z = z_ref[0]
    r, k, v = z[:, :W], z[:, W:2 * W], z[:, 2 * W:3 * W]
    da = z[:, 3 * W:3 * W + LANE]
    lane = lax.broadcasted_iota(jnp.int32, da.shape, 1)
    da = jnp.where(lane < RW_DECAY_RANK, jnp.tanh(da), da)
    up = jnp.dot(da.astype(bf), wda_ref[...], preferred_element_type=jnp.float32)
    g_lo = z[:, 3 * W + LANE:]
    g_ref[0] = jnp.dot(jax.nn.sigmoid(g_lo).astype(bf), gup_ref[...], preferred_element_type=jnp.float32)
    hsum = hsum_ref[...]
    kk = k * kk_ref_w[...]
    kk = kk * lax.rsqrt(_split_dot(kk * kk, hsum) + L2_EPS)
    r_ref[0] = r
    v_ref[0] = v
    kk_ref[0] = kk
    bonus_ref[0] = _split_dot(r * k * rk_ref[...], hsum) * v
    for d in range(2):
        lw_ref[d, 0] = -RW_DECAY_SCALE * jax.nn.sigmoid(d0_ref[d:d + 1, :] + up[:, d * W:(d + 1) * W])
        a = jax.nn.sigmoid(a0_ref[d:d + 1, :] + up[:, (2 + d) * W:(3 + d) * W])
        akk_ref[d, 0] = kk * a
        kr_ref[d, 0] = k * (1.0 + (a - 1.0) * ka_ref[...])


def rwkv_prep(z, decay_up, decay0, a_up, a0, gate_up, k_k, k_a, r_k):
    B, L, _ = z.shape
    W = RW_WIDTH
    T = min(RW_PREP_ROWS, L)
    zero = jnp.zeros((RW_DECAY_RANK, 2 * W), jnp.float32)
    wda = jnp.concatenate([
        jnp.concatenate([decay_up[0], decay_up[1], zero], axis=1),
        jnp.concatenate([zero, a_up[0], a_up[1]], axis=1)], axis=0).astype(jnp.bfloat16)
    head = jnp.arange(W) // RW_HEAD_DIM
    hsum = (head[:, None] == head[None, :]).astype(jnp.bfloat16)
    row = lambda a: a.reshape(1, W)
    const = lambda a: pl.BlockSpec(a.shape, lambda b, i: (0,) * a.ndim)
    tok = pl.BlockSpec((1, T, W), lambda b, i: (b, i, 0))
    tok2 = pl.BlockSpec((2, 1, T, W), lambda b, i: (0, b, i, 0))
    f1 = jax.ShapeDtypeStruct((B, L, W), jnp.float32)
    f2 = jax.ShapeDtypeStruct((2, B, L, W), jnp.float32)
    args = (z, wda, decay0, a0, gate_up.astype(jnp.bfloat16), row(k_k), row(k_a), row(r_k), hsum)
    return pl.pallas_call(
        _rwkv_prep_kernel,
        grid=(B, L // T),
        in_specs=[pl.BlockSpec((1, T, RW_PROJ), lambda b, i: (b, i, 0))] + [const(a) for a in args[1:]],
        out_specs=[tok, tok, tok, tok2, tok2, tok2, tok, tok],
        out_shape=[f1, f1, f1, f2, f2, f2, f1, f1],
        compiler_params=pltpu.CompilerParams(dimension_semantics=("parallel", "parallel"),
                                             vmem_limit_bytes=VMEM_LIMIT_BYTES),
        name="rwkv_prep",
    )(*args)


def _rwkv_readout_kernel(y_ref, g_ref, bonus_ref, gng_ref, gnb_ref, hsum_ref, o_ref):
    y = y_ref[0, 0] + y_ref[1, 0]
    hsum = hsum_ref[...]
    mu = _split_dot(y, hsum) * (1.0 / RW_HEAD_DIM)
    d = y - mu
    var = _split_dot(d * d, hsum) * (1.0 / RW_HEAD_DIM)
    yn = d * lax.rsqrt(var + RW_GN_EPS) * gng_ref[...] + gnb_ref[...]
    o_ref[0] = (yn + bonus_ref[0]) * g_ref[0]


def rwkv_readout(y, g, bonus, gn_g, gn_b, t0):
    B, L, W = g.shape
    T = min(RW_PREP_ROWS, L)
    off = t0 // T
    head = jnp.arange(W) // RW_HEAD_DIM
    hsum = (head[:, None] == head[None, :]).astype(jnp.bfloat16)
    tok = pl.BlockSpec((1, T, W), lambda b, i: (b, i, 0))
    const = lambda a: pl.BlockSpec(a.shape, lambda b, i: (0,) * a.ndim)
    gg, gb = gn_g.reshape(1, W), gn_b.reshape(1, W)
    return pl.pallas_call(
        _rwkv_readout_kernel,
        grid=(B, L // T),
        in_specs=[pl.BlockSpec((2, 1, T, W), lambda b, i: (0, b, i + off, 0)), tok, tok, const(gg), const(gb), const(hsum)],
        out_specs=tok,
        out_shape=jax.ShapeDtypeStruct((B, L, W), jnp.float32),
        compiler_params=pltpu.CompilerParams(dimension_semantics=("parallel", "parallel")),
        name="rwkv_readout",
    )(y, g, bonus, gg, gb, hsum)


def rwkv7_mixer(p_lat, p_ctx, conv_w, decay_up, decay0, a_up, a0, gate_up, k_k, k_a, r_k, gn_g, gn_b, need_ctx):
    prm = (decay_up, decay0, a_up, a0, gate_up, k_k, k_a, r_k)
    lat = rwkv_prep(short_conv(p_lat, conv_w), *prm)
    ctx = rwkv_prep(short_conv(p_ctx, conv_w), *prm)
    n_ctx = p_ctx.shape[1]
    seq = lambda i: jnp.concatenate([ctx[i], lat[i]], axis=-2)
    y = rwkv_chunked(seq(0), seq(1), seq(2), seq(3), seq(4), seq(5), n_ctx)
    out_l = rwkv_readout(y, lat[6], lat[7], gn_g, gn_b, n_ctx)
    out_c = rwkv_readout(y, ctx[6], ctx[7], gn_g, gn_b, 0) if need_ctx else None
    return out_l, out_c


def _rope_tables(L, use_rope):
    lane = np.arange(LANE)
    in_rope = (lane >= MLA_NOPE_DIM) & (lane < MLA_QK_DIM)
    j = lane - MLA_NOPE_DIM
    axis = j // AXIS_ROPE_DIM
    half = AXIS_ROPE_DIM // 2
    f = j % half
    first = (j % AXIS_ROPE_DIM) < half
    inv = ROPE_THETA ** (-jnp.arange(0, AXIS_ROPE_DIM, 2, dtype=jnp.float32) / AXIS_ROPE_DIM)
    t = jnp.arange(L)
    pos = jnp.stack([t // GRID_W, t % GRID_W], axis=-1).astype(jnp.float32)
    ang = pos[:, np.clip(axis, 0, 1)] * inv[np.clip(f, 0, half - 1)][None, :]
    rope_on = jnp.asarray(in_rope)[None, :] & use_rope
    cos = jnp.where(rope_on, jnp.cos(ang), 1.0)
    sin = jnp.where(rope_on, jnp.sin(ang) * jnp.where(jnp.asarray(first), -1.0, 1.0)[None, :], 0.0)
    return jnp.tile(cos, (1, MLA_HEADS)), jnp.tile(sin, (1, MLA_HEADS))


def _mla_prep_kernel(p_ref, qn_ref, wq_ref, kvn_ref, wk_ref, wv_ref, place_ref, qg_ref, kg_ref, hsum_ref, cos_ref, sin_ref,
                     q_ref, k_ref, v_ref):
    bf = jnp.bfloat16
    p = p_ref[0]
    c_q = p[:, :MLA_Q_RANK]
    c_kv = p[:, MLA_Q_RANK:MLA_Q_RANK + MLA_KV_RANK]
    tail = p[:, MLA_Q_RANK + MLA_KV_RANK:]
    cqn = c_q * lax.rsqrt(jnp.mean(c_q * c_q, axis=-1, keepdims=True) + NORM_EPS) * qn_ref[...]
    ckn = c_kv * lax.rsqrt(jnp.mean(c_kv * c_kv, axis=-1, keepdims=True) + NORM_EPS) * kvn_ref[...]
    q = jnp.dot(cqn.astype(bf), wq_ref[...], preferred_element_type=jnp.float32)
    k = jnp.dot(ckn.astype(bf), wk_ref[...], preferred_element_type=jnp.float32) + _split_dot(tail, place_ref[...])
    v_ref[0] = jnp.dot(ckn.astype(bf), wv_ref[...], preferred_element_type=jnp.float32).astype(bf)
    hsum = hsum_ref[...]
    cos, sin = cos_ref[...], sin_ref[...]
    lane = lax.broadcasted_iota(jnp.int32, q.shape, 1)
    first = ((lane - MLA_NOPE_DIM) % AXIS_ROPE_DIM) < (AXIS_ROPE_DIM // 2)
    half = AXIS_ROPE_DIM // 2

    def finish(x, gain):
        x = x * lax.rsqrt(_split_dot(x * x, hsum) * (1.0 / MLA_QK_DIM) + NORM_EPS) * gain
        partner = jnp.where(first, pltpu.roll(x, MLA_PAD_WIDTH - half, 1), pltpu.roll(x, half, 1))
        return x * cos + partner * sin

    q_ref[0] = (finish(q, qg_ref[...]) * (MLA_QK_DIM ** -0.5)).astype(bf)
    k_ref[0] = finish(k, kg_ref[...]).astype(bf)


def mla_prep(p, use_rope, q_norm, w_uq, kv_norm, w_ukv, q_gain, k_gain):
    B, L, _ = p.shape
    T = min(MLA_PREP_ROWS, L)
    H = MLA_HEADS
    pad_cols = lambda w, d: jnp.pad(w.reshape(w.shape[0], H, d), ((0, 0), (0, 0), (0, LANE - d))).reshape(w.shape[0], H * LANE)
    wq = pad_cols(w_uq, MLA_QK_DIM).astype(jnp.bfloat16)
    ukv = w_ukv.reshape(MLA_KV_RANK, H, MLA_NOPE_DIM + MLA_V_DIM)
    wk = pad_cols(ukv[:, :, :MLA_NOPE_DIM].reshape(MLA_KV_RANK, H * MLA_NOPE_DIM), MLA_NOPE_DIM).astype(jnp.bfloat16)
    wv = ukv[:, :, MLA_NOPE_DIM:].reshape(MLA_KV_RANK, H * MLA_V_DIM).astype(jnp.bfloat16)
    lane = np.arange(H * LANE)
    place = jnp.asarray(((lane[None, :] % LANE) - MLA_NOPE_DIM == np.arange(MLA_ROPE_DIM)[:, None]), jnp.bfloat16)
    hsum = jnp.asarray((lane[:, None] // LANE) == (lane[None, :] // LANE), jnp.bfloat16)
    pad_gain = lambda g: jnp.tile(jnp.pad(g, (0, LANE - MLA_QK_DIM)), H).reshape(1, H * LANE)
    cos, sin = _rope_tables(L, use_rope)
    const = lambda a: pl.BlockSpec(a.shape, lambda b, i: (0,) * a.ndim)
    args = (p, q_norm.reshape(1, -1), wq, kv_norm.reshape(1, -1), wk, wv, place, pad_gain(q_gain), pad_gain(k_gain), hsum)
    pos = pl.BlockSpec((T, H * LANE), lambda b, i: (i, 0))
    return pl.pallas_call(
        _mla_prep_kernel,
        grid=(B, L // T),
        in_specs=[pl.BlockSpec((1, T, MLA_PROJ), lambda b, i: (b, i, 0))] + [const(a) for a in args[1:]] + [pos, pos],
        out_specs=[pl.BlockSpec((1, T, H * LANE), lambda b, i: (b, i, 0)), pl.BlockSpec((1, T, H * LANE), lambda b, i: (b, i, 0)),
                   pl.BlockSpec((1, T, MLA_WIDTH), lambda b, i: (b, i, 0))],
        out_shape=[jax.ShapeDtypeStruct((B, L, H * LANE), jnp.bfloat16), jax.ShapeDtypeStruct((B, L, H * LANE), jnp.bfloat16),
                   jax.ShapeDtypeStruct((B, L, MLA_WIDTH), jnp.bfloat16)],
        compiler_params=pltpu.CompilerParams(dimension_semantics=("parallel", "parallel"),
                                             vmem_limit_bytes=VMEM_LIMIT_BYTES),
        name="mla_prep",
    )(*args, cos, sin)


def _attn_kernel(q_ref, k_ref, v_ref, o_ref):
    lane = lax.broadcasted_iota(jnp.int32, (q_ref.shape[1], LANE), 1)
    for pair in range(MLA_HEADS // 2):
        v_pair = v_ref[0, :, pair * LANE:(pair + 1) * LANE]
        outs = []
        for h in (2 * pair, 2 * pair + 1):
            q = q_ref[0, :, h * LANE:(h + 1) * LANE]
            k = k_ref[0, :, h * LANE:(h + 1) * LANE]
            s = lax.dot_general(q, k, (((1,), (1,)), ((), ())), preferred_element_type=jnp.float32)
            e = jnp.exp(s - jnp.max(s, axis=-1, keepdims=True))
            o = jnp.dot(e.astype(jnp.bfloat16), v_pair, preferred_element_type=jnp.float32)
            outs.append(o / jnp.sum(e, axis=-1, keepdims=True))
        o_ref[0, :, pair * LANE:(pair + 1) * LANE] = jnp.where(lane < MLA_V_DIM, outs[0], outs[1])


def attention(q, k, v):
    B, Lq, P = q.shape
    Lk = k.shape[1]
    tq = min(ATTN_Q_ROWS, Lq)
    return pl.pallas_call(
        _attn_kernel,
        grid=(B, Lq // tq),
        in_specs=[pl.BlockSpec((1, tq, P), lambda b, i: (b, i, 0)),
                  pl.BlockSpec((1, Lk, P), lambda b, i: (b, 0, 0)),
                  pl.BlockSpec((1, Lk, MLA_WIDTH), lambda b, i: (b, 0, 0))],
        out_specs=pl.BlockSpec((1, tq, MLA_WIDTH), lambda b, i: (b, i, 0)),
        out_shape=jax.ShapeDtypeStruct((B, Lq, MLA_WIDTH), jnp.float32),
        compiler_params=pltpu.CompilerParams(dimension_semantics=("parallel", "parallel"),
                                             vmem_limit_bytes=VMEM_LIMIT_BYTES),
        name="mla_attention",
    )(q, k, v)


def mla_mixer(p_lat, p_ctx, q_norm, w_uq, kv_norm, w_ukv, q_gain, k_gain, need_ctx):
    prm = (q_norm, w_uq, kv_norm, w_ukv, q_gain, k_gain)
    q_l, k_l, v_l = mla_prep(p_lat, True, *prm)
    q_c, k_c, v_c = mla_prep(p_ctx, False, *prm)
    y_l = attention(q_l, jnp.concatenate([k_l, k_c], axis=1), jnp.concatenate([v_l, v_c], axis=1))
    y_c = attention(q_c, k_c, v_c) if need_ctx else None
    return y_l, y_c


def hyena_filters(L, w1, b1, freq1, w2, b2, freq2, w3, b3):
    tn = jnp.arange(L, dtype=jnp.float32) / L
    bands = jnp.arange(1, HY_POS_BANDS + 1, dtype=jnp.float32)
    ang = 2.0 * math.pi * tn[:, None] * bands[None, :]
    z = jnp.concatenate([tn[:, None], jnp.cos(ang), jnp.sin(ang)], axis=-1)
    h = jnp.sin(freq1 * (z @ w1 + b1))
    h = jnp.sin(freq2 * (h @ w2 + b2))
    h = (h @ w3 + b3).reshape(L, HY_ORDER, 2, HY_WIDTH)
    rates = jnp.abs(jnp.linspace(math.log(HY_DECAY_TARGET) / HY_LONG_DECAY_PCT,
                                 math.log(HY_DECAY_TARGET) / HY_SHORT_DECAY_PCT, HY_WIDTH))
    h = h * jnp.exp(-tn[:, None] * rates[None, :])[:, None, None, :]
    zero = jnp.zeros((1, HY_ORDER, HY_WIDTH), h.dtype)
    h_full = jnp.concatenate([h[:, :, 0], zero, h[:0:-1, :, 1]], axis=0)
    return h_full * lax.rsqrt(jnp.sum(jnp.square(h_full), axis=0, keepdims=True))


def fft_long_conv(u, h_full, bias):
    L = u.shape[1]
    uf = jnp.fft.rfft(u, n=2 * L, axis=1)
    hf = jnp.fft.rfft(h_full, n=2 * L, axis=0)
    y = jnp.fft.irfft(uf * hf[None], n=2 * L, axis=1)[:, :L]
    return y + u * bias


FFT_N1 = 64
FFT_N2 = 128
FFT_N = FFT_N1 * FFT_N2
HY_SEQS = 32


def _dft_tables(seqs):
    n1 = np.arange(FFT_N1)
    n2 = np.arange(FFT_N2)
    f64 = np.exp(-2j * np.pi * np.outer(n1, n1) / FFT_N1)
    f128 = np.exp(-2j * np.pi * np.outer(n2, n2) / FFT_N2)
    tw = np.exp(-2j * np.pi * np.outer(n1, n2) / FFT_N)
    half = FFT_N1 // 2
    fh = f64[:, :half]
    m1 = np.block([[fh.real, -fh.imag], [fh.imag, fh.real]])
    m1f = np.concatenate([f64.real, f64.imag], axis=0)
    m2 = np.block([[f128.real, f128.imag], [-f128.imag, f128.real]])
    m3 = np.block([[f128.real, -f128.imag], [f128.imag, f128.real]]) / FFT_N
    c = np.conj(f64)[:half, :]
    m4 = np.block([[c.real, -c.imag], [c.imag, c.real]])
    bf = lambda a: jnp.asarray(a, jnp.float32).astype(jnp.bfloat16)
    f32 = lambda a: jnp.asarray(a, jnp.float32)
    return dict(m1=bf(m1), m1f=bf(m1f), m2=bf(m2), m3=bf(m3), m4=bf(m4),
                twr_l=f32(np.tile(tw.real, (1, seqs))), twi_l=f32(np.tile(tw.imag, (1, seqs))),
                twr_s=f32(np.tile(tw.real, (seqs, 1))), twi_s=f32(np.tile(tw.imag, (seqs, 1))))


def _spectrum(cols, m1, twr_l, twi_l, m2, R):
    a = jnp.dot(m1, cols.astype(jnp.bfloat16), preferred_element_type=jnp.float32)
    ar, ai = a[:FFT_N1], a[FFT_N1:]
    pr = ar * twr_l - ai * twi_l
    pi = ar * twi_l + ai * twr_l
    lhs = jnp.concatenate(
        [jnp.concatenate([pr[:, r * FFT_N2:(r + 1) * FFT_N2], pi[:, r * FFT_N2:(r + 1) * FFT_N2]], axis=1)
         for r in range(R)], axis=0)
    return jnp.dot(lhs.astype(jnp.bfloat16), m2, preferred_element_type=jnp.float32)


def _filter_fft_kernel(h_ref, m1f_ref, twr_ref, twi_ref, m2_ref, o_ref):
    R = HY_SEQS
    cols = jnp.concatenate([h_ref[r] for r in range(R)], axis=1)
    x = _spectrum(cols, m1f_ref[...], twr_ref[...], twi_ref[...], m2_ref[...], R)
    o_ref[...] = x.reshape(R, FFT_N1, 2 * FFT_N2)


def _hyena_conv_kernel(y_ref, g_ref, hf_ref, bias_ref, m1_ref, twr_l_ref, twi_l_ref, m2_ref, m3_ref,
                       twr_s_ref, twi_s_ref, m4_ref, o_ref):
    R = HY_SEQS
    half = FFT_N1 // 2
    y = [y_ref[0], y_ref[1]]
    for o in range(HY_ORDER):
        top = jnp.concatenate([y[0][r] for r in range(R)], axis=1)
        bot = jnp.concatenate([y[1][r] for r in range(R)], axis=1)
        x = _spectrum(jnp.concatenate([top, bot], axis=0), m1_ref[...], twr_l_ref[...], twi_l_ref[...], m2_ref[...], R)
        hf = hf_ref[o].reshape(R * FFT_N1, 2 * FFT_N2)
        xr, xi = x[:, :FFT_N2], x[:, FFT_N2:]
        hr, hi = hf[:, :FFT_N2], hf[:, FFT_N2:]
        yc = jnp.concatenate([xr * hr - xi * hi, xr * hi + xi * hr], axis=1)
        b = jnp.dot(yc.astype(jnp.bfloat16), m3_ref[...], preferred_element_type=jnp.float32)
        br, bi = b[:, :FFT_N2], b[:, FFT_N2:]
        qr = br * twr_s_ref[...] + bi * twi_s_ref[...]
        qi = bi * twr_s_ref[...] - br * twi_s_ref[...]
        bc = jnp.concatenate(
            [jnp.concatenate([qr[r * FFT_N1:(r + 1) * FFT_N1], qi[r * FFT_N1:(r + 1) * FFT_N1]], axis=0)
             for r in range(R)], axis=1)
        yo = jnp.dot(m4_ref[...], bc.astype(jnp.bfloat16), preferred_element_type=jnp.float32)
        for p in range(2):
            conv = jnp.stack([yo[p * half:(p + 1) * half, r * FFT_N2:(r + 1) * FFT_N2] for r in range(R)], axis=0)
            y[p] = g_ref[o, p] * (conv + y[p] * bias_ref[o])
    o_ref[0] = y[0]
    o_ref[1] = y[1]


def hyena_long_conv(y_t, g_t, h_t, bias):
    B, C, L = y_t.shape
    assert 2 * L == FFT_N and B % 2 == 0 and C % HY_SEQS == 0
    R = HY_SEQS
    half = FFT_N1 // 2
    tb = _dft_tables(R)
    const = lambda a: pl.BlockSpec(a.shape, lambda *_: (0,) * a.ndim)
    hf = pl.pallas_call(
        _filter_fft_kernel,
        grid=(HY_ORDER * C // R,),
        in_specs=[pl.BlockSpec((R, FFT_N1, FFT_N2), lambda i: (i, 0, 0)),
                  const(tb['m1f']), const(tb['twr_l']), const(tb['twi_l']), const(tb['m2'])],
        out_specs=pl.BlockSpec((R, FFT_N1, 2 * FFT_N2), lambda i: (i, 0, 0)),
        out_shape=jax.ShapeDtypeStruct((HY_ORDER * C, FFT_N1, 2 * FFT_N2), jnp.float32),
        compiler_params=pltpu.CompilerParams(dimension_semantics=("parallel",), vmem_limit_bytes=VMEM_LIMIT_BYTES),
        name="hyena_filter_fft",
    )(h_t.reshape(HY_ORDER * C, FFT_N1, FFT_N2), tb['m1f'], tb['twr_l'], tb['twi_l'], tb['m2'])
    hf = hf.reshape(HY_ORDER, C, FFT_N1, 2 * FFT_N2)
    out = pl.pallas_call(
        _hyena_conv_kernel,
        grid=(B // 2, C // R),
        in_specs=[pl.BlockSpec((2, R, half, FFT_N2), lambda b, c: (b, c, 0, 0)),
                  pl.BlockSpec((HY_ORDER, 2, R, half, FFT_N2), lambda b, c: (0, b, c, 0, 0)),
                  pl.BlockSpec((HY_ORDER, R, FFT_N1, 2 * FFT_N2), lambda b, c: (0, c, 0, 0)),
                  pl.BlockSpec((HY_ORDER, R, 1, 1), lambda b, c: (0, c, 0, 0)),
                  const(tb['m1']), const(tb['twr_l']), const(tb['twi_l']), const(tb['m2']), const(tb['m3']),
                  const(tb['twr_s']), const(tb['twi_s']), const(tb['m4'])],
        out_specs=pl.BlockSpec((2, R, half, FFT_N2), lambda b, c: (b, c, 0, 0)),
        out_shape=jax.ShapeDtypeStruct((B, C, half, FFT_N2), jnp.float32),
        compiler_params=pltpu.CompilerParams(dimension_semantics=("parallel", "parallel"),
                                             vmem_limit_bytes=VMEM_LIMIT_BYTES),
        name="hyena_conv",
    )(y_t.reshape(B, C, half, FFT_N2), g_t.reshape(HY_ORDER, B, C, half, FFT_N2), hf,
      bias.reshape(HY_ORDER, C, 1, 1), tb['m1'], tb['twr_l'], tb['twi_l'], tb['m2'], tb['m3'],
      tb['twr_s'], tb['twi_s'], tb['m4'])
    return out.reshape(B, C, L)


def hyena_mixer(p, conv_w, w1, b1, freq1, w2, b2, freq2, w3, b3, bias):
    B, L = p.shape[:2]
    z = short_conv(p, conv_w)
    h_full = hyena_filters(L, w1, b1, freq1, w2, b2, freq2, w3, b3)
    if 2 * L == FFT_N:
        g_t = jnp.transpose(z[..., :HY_ORDER * HY_WIDTH].reshape(B, L, HY_ORDER, HY_WIDTH), (2, 0, 3, 1))
        y_t = jnp.swapaxes(z[..., HY_ORDER * HY_WIDTH:], 1, 2)
        if B % 2:
            y_t = jnp.pad(y_t, ((0, 1), (0, 0), (0, 0)))
            g_t = jnp.pad(g_t, ((0, 0), (0, 1), (0, 0), (0, 0)))
        y_t = hyena_long_conv(y_t, g_t, jnp.transpose(h_full, (1, 2, 0)), bias)[:B]
        return jnp.swapaxes(y_t, 1, 2)
    gates = (z[..., :HY_WIDTH], z[..., HY_WIDTH:2 * HY_WIDTH])
    y = z[..., 2 * HY_WIDTH:]
    for o in range(HY_ORDER):
        y = gates[o] * fft_long_conv(y, h_full[:, o], bias[o])
    return y


SC_CORES = 2
SC_SUBCORES = 16
SC_LANES = 16
SC_WORKERS = SC_CORES * SC_SUBCORES
PEER_SLOTS = PEER_HEADS * PEER_TOPK
PEER_GATHER_ROWS = 32
PEER_GATHERS = PEER_SLOTS // PEER_GATHER_ROWS
PEER_ACC_VREGS = 8
PEER_ROW_BUFFERS = 4
PEER_ROW_WORDS = D_MODEL // 2
HI_MASK = -65536


def pack_expert_table(t):
    b = lax.bitcast_convert_type(t.astype(jnp.bfloat16), jnp.uint16).astype(jnp.uint32)
    return lax.bitcast_convert_type(b[:, :PEER_ROW_WORDS] | (b[:, PEER_ROW_WORDS:] << 16), jnp.int32)


def _sc_peer_phase(phase, tpw):
    NBUF = PEER_ROW_BUFFERS
    AHEAD = NBUF - 1
    HW = PEER_ROW_WORDS

    def run(base, table_hbm, idx_hbm, aux_hbm, out_hbm, idx_v, aux_v, rows_v, out_v, sem_r, sem_i, sem_o):

        def gather(p, c, b):
            return pltpu.make_async_copy(table_hbm.at[idx_v.at[p, c]], rows_v.at[b], sem_r.at[b])

        def load_meta(t, p):
            return (pltpu.make_async_copy(idx_hbm.at[t], idx_v.at[p], sem_i.at[p]),
                    pltpu.make_async_copy(aux_hbm.at[t], aux_v.at[p], sem_i.at[p]))

        def store_out(t, p):
            return pltpu.make_async_copy(out_v.at[p], out_hbm.at[t], sem_o.at[p])

        def halves(word):
            return (plsc.bitcast(lax.shift_left(word, 16), jnp.float32), plsc.bitcast(word & HI_MASK, jnp.float32))

        def compute(p, c, b):
            if phase == "dot":
                lane = lax.iota(jnp.int32, SC_LANES)
                vec = jnp.zeros((SC_LANES,), jnp.float32)
                groups_per_vec = SC_LANES // PEER_ACC_VREGS
                for g in range(PEER_GATHER_ROWS // PEER_ACC_VREGS):
                    def body(cc, accs):
                        x_lo = aux_v[p, pl.ds(cc * SC_LANES, SC_LANES)]
                        x_hi = aux_v[p, pl.ds(HW + cc * SC_LANES, SC_LANES)]
                        out = []
                        for r in range(PEER_ACC_VREGS):
                            lo, hi = halves(rows_v[b, g * PEER_ACC_VREGS + r, pl.ds(cc * SC_LANES, SC_LANES)])
                            out.append(accs[r] + lo * x_lo + hi * x_hi)
                        return tuple(out)
                    accs = lax.fori_loop(0, HW // SC_LANES, body,
                                         tuple(jnp.zeros((SC_LANES,), jnp.float32) for _ in range(PEER_ACC_VREGS)))
                    for r in range(PEER_ACC_VREGS):
                        vec = jnp.where(lane == (g % groups_per_vec) * PEER_ACC_VREGS + r, jnp.sum(accs[r]), vec)
                    if g % groups_per_vec == groups_per_vec - 1:
                        out_v[p, pl.ds(c * PEER_GATHER_ROWS + (g // groups_per_vec) * SC_LANES, SC_LANES)] = vec
            else:
                words = PEER_ACC_VREGS // 2
                for db in range(HW // (words * SC_LANES)):
                    def body(kk, accs):
                        wv = plsc.load_gather(aux_v.at[p], [jnp.full((SC_LANES,), c * PEER_GATHER_ROWS + kk, jnp.int32)])
                        out = []
                        for j in range(words):
                            lo, hi = halves(rows_v[b, kk, pl.ds((db * words + j) * SC_LANES, SC_LANES)])
                            out += [accs[2 * j] + lo * wv, accs[2 * j + 1] + hi * wv]
                        return tuple(out)
                    if c == 0:
                        init = tuple(jnp.zeros((SC_LANES,), jnp.float32) for _ in range(2 * words))
                    else:
                        init = tuple(out_v[p, pl.ds(half * HW + (db * words + j) * SC_LANES, SC_LANES)]
                                     for j in range(words) for half in range(2))
                    accs = lax.fori_loop(0, PEER_GATHER_ROWS, body, init)
                    for j in range(words):
                        out_v[p, pl.ds((db * words + j) * SC_LANES, SC_LANES)] = accs[2 * j]
                        out_v[p, pl.ds(HW + (db * words + j) * SC_LANES, SC_LANES)] = accs[2 * j + 1]

        for d in load_meta(base, 0):
            d.start()
        for d in load_meta(base, 0):
            d.wait()
        for c in range(AHEAD):
            gather(0, c, c % NBUF).start()

        @pl.loop(0, tpw // 2)
        def _(i2):
            for p in range(2):
                i = i2 * 2 + p
                t = base + i
                nxt = base + jnp.minimum(i + 1, tpw - 1)
                for d in load_meta(nxt, 1 - p):
                    d.start()

                @pl.when(i2 > 0)
                def _():
                    store_out(t, p).wait()

                for c in range(PEER_GATHERS):
                    ahead = c + AHEAD
                    if ahead < PEER_GATHERS:
                        gather(p, ahead, ahead % NBUF).start()
                    else:
                        if ahead == PEER_GATHERS:
                            for d in load_meta(nxt, 1 - p):
                                d.wait()
                        gather(1 - p, ahead - PEER_GATHERS, ahead % NBUF).start()
                    gather(p, c, c % NBUF).wait()
                    compute(p, c, c % NBUF)
                store_out(t, p).start()

        for c in range(AHEAD):
            gather(0, c, c % NBUF).wait()
        for p in range(2):
            store_out(base, p).wait()

    return run


def _sc_tokens_per_worker(N):
    assert N % (2 * SC_WORKERS) == 0 and PEER_GATHERS % PEER_ROW_BUFFERS == 0
    return N // SC_WORKERS


_SC_AUX = {"dot": (D_MODEL,), "wsum": (PEER_SLOTS,)}
_SC_OUT = {"dot": (PEER_SLOTS,), "wsum": (D_MODEL,)}


def _sc_scratch(phases):
    s = [pltpu.VMEM((2, PEER_GATHERS, PEER_GATHER_ROWS), jnp.int32),
         pltpu.VMEM((PEER_ROW_BUFFERS, PEER_GATHER_ROWS, PEER_ROW_WORDS), jnp.int32),
         pltpu.SemaphoreType.DMA((PEER_ROW_BUFFERS,)), pltpu.SemaphoreType.DMA((2,)), pltpu.SemaphoreType.DMA((2,))]
    for ph in phases:
        s += [pltpu.VMEM((2,) + _SC_AUX[ph], jnp.float32), pltpu.VMEM((2,) + _SC_OUT[ph], jnp.float32)]
    return s


def _sc_peer(table, idx, aux, phase):
    N = idx.shape[0]
    tpw = _sc_tokens_per_worker(N)
    run = _sc_peer_phase(phase, tpw)

    @functools.partial(
        pl.kernel, mesh=plsc.VectorSubcoreMesh(core_axis_name="c", subcore_axis_name="s"),
        out_type=jax.ShapeDtypeStruct((N,) + _SC_OUT[phase], jnp.float32),
        compiler_params=pltpu.CompilerParams(needs_layout_passes=False),
        scratch_types=_sc_scratch([phase]),
    )
    def k(table_hbm, idx_hbm, aux_hbm, out_hbm, idx_v, rows_v, sem_r, sem_i, sem_o, aux_v, out_v):
        base = (lax.axis_index("s") * SC_CORES + lax.axis_index("c")) * tpw
        run(base, table_hbm, idx_hbm, aux_hbm, out_hbm, idx_v, aux_v, rows_v, out_v, sem_r, sem_i, sem_o)

    return k(table, idx.reshape(N, PEER_GATHERS, PEER_GATHER_ROWS), aux)


def _sc_peer_pair(table_v, idx_a, w_a, table_u, idx_b, h_b):
    Na, Nb = idx_a.shape[0], idx_b.shape[0]
    tpw_a, tpw_b = _sc_tokens_per_worker(Na), _sc_tokens_per_worker(Nb)
    run_wsum = _sc_peer_phase("wsum", tpw_a)
    run_dot = _sc_peer_phase("dot", tpw_b)

    @functools.partial(
        pl.kernel, mesh=plsc.VectorSubcoreMesh(core_axis_name="c", subcore_axis_name="s"),
        out_type=(jax.ShapeDtypeStruct((Na,) + _SC_OUT["wsum"], jnp.float32),
                  jax.ShapeDtypeStruct((Nb,) + _SC_OUT["dot"], jnp.float32)),
        compiler_params=pltpu.CompilerParams(needs_layout_passes=False),
        scratch_types=_sc_scratch(["wsum", "dot"]),
    )
    def k(tv_hbm, ia_hbm, wa_hbm, tu_hbm, ib_hbm, hb_hbm, outa_hbm, outb_hbm,
          idx_v, rows_v, sem_r, sem_i, sem_o, w_v, outa_v, h_v, outb_v):
        worker = lax.axis_index("s") * SC_CORES + lax.axis_index("c")
        run_wsum(worker * tpw_a, tv_hbm, ia_hbm, wa_hbm, outa_hbm, idx_v, w_v, rows_v, outa_v, sem_r, sem_i, sem_o)
        run_dot(worker * tpw_b, tu_hbm, ib_hbm, hb_hbm, outb_hbm, idx_v, h_v, rows_v, outb_v, sem_r, sem_i, sem_o)

    shp = (PEER_GATHERS, PEER_GATHER_ROWS)
    return k(table_v, idx_a.reshape((Na,) + shp), w_a, table_u, idx_b.reshape((Nb,) + shp), h_b)


PEER_TOKENS = 256
INT_BIG = 2 ** 30


def _extract_topk(cand_ref, ids_ref, val_out_ref, id_out_ref, row0):
    def body(r, carry):
        c = cand_ref[...]
        ids = ids_ref[...]
        m = jnp.max(c, axis=0, keepdims=True)
        sel = jnp.min(jnp.where(c == m, ids, INT_BIG), axis=0, keepdims=True)
        cand_ref[...] = jnp.where(ids == sel, -jnp.inf, c)
        val_out_ref[pl.ds(row0 + r, 1), :] = m
        id_out_ref[pl.ds(row0 + r, 1), :] = sel
        return carry
    lax.fori_loop(0, PEER_TOPK, body, 0)


def _peer_retrieve_kernel(x_ref, gain_ref, scale_ref, shift_ref, wq_ref, keys_ref,
                          h_ref, idx_out_ref, gate_out_ref,
                          s_ref, ids1_ref, sv_ref, si_ref, cand_ref, cid_ref, ts_ref, idx_ref, gate_ref):
    x = x_ref[0]
    y = x * lax.rsqrt(jnp.mean(x * x, axis=-1, keepdims=True) + NORM_EPS)
    h = (y * gain_ref[...]) * (1.0 + scale_ref[0]) + shift_ref[0]
    h_ref[0] = h
    q = jnp.dot(h.astype(jnp.bfloat16), wq_ref[...], preferred_element_type=jnp.float32)
    T = PEER_TOKENS
    K = PEER_TOPK
    ids1_ref[...] = lax.broadcasted_iota(jnp.int32, (PEER_N_KEYS, T), 0)
    for hd in range(PEER_HEADS):
        for p in range(2):
            hp = hd * 2 + p
            qs = q[:, hp * PEER_HALF:(hp + 1) * PEER_HALF].astype(jnp.bfloat16)
            s_ref[...] = lax.dot_general(keys_ref[hp], qs, (((1,), (1,)), ((), ())),
                                         preferred_element_type=jnp.float32)
            _extract_topk(s_ref, ids1_ref, sv_ref, si_ref, p * K)
        for i in range(K):
            cand_ref[i * K:(i + 1) * K, :] = sv_ref[i:i + 1, :] + sv_ref[K:2 * K, :]
            cid_ref[i * K:(i + 1) * K, :] = si_ref[i:i + 1, :] * PEER_N_KEYS + si_ref[K:2 * K, :]
        _extract_topk(cand_ref, cid_ref, ts_ref, idx_ref, hd * K)
        ts = ts_ref[hd * K:(hd + 1) * K, :]
        e = jnp.exp(ts - jnp.max(ts, axis=0, keepdims=True))
        gate_ref[hd * K:(hd + 1) * K, :] = e / jnp.sum(e, axis=0, keepdims=True)
    idx_out_ref[...] = idx_ref[...].T
    gate_out_ref[...] = gate_ref[...].T


def peer_retrieve(x, gain, scale, shift, w_q, sub_keys):
    B, L, D = x.shape
    T = PEER_TOKENS
    nt = L // T
    keys = sub_keys.reshape(PEER_HEADS * 2, PEER_N_KEYS, PEER_HALF).astype(jnp.bfloat16)
    return pl.pallas_call(
        _peer_retrieve_kernel,
        grid=(B, nt),
        in_specs=[
            pl.BlockSpec((1, T, D), lambda b, i: (b, i, 0)),
            pl.BlockSpec((1, D), lambda b, i: (0, 0)),
            pl.BlockSpec((1, 1, D), lambda b, i: (b, 0, 0)),
            pl.BlockSpec((1, 1, D), lambda b, i: (b, 0, 0)),
            pl.BlockSpec((D, PEER_HEADS * 2 * PEER_HALF), lambda b, i: (0, 0)),
            pl.BlockSpec((PEER_HEADS * 2, PEER_N_KEYS, PEER_HALF), lambda b, i: (0, 0, 0)),
        ],
        out_specs=[
            pl.BlockSpec((1, T, D), lambda b, i: (b, i, 0)),
            pl.BlockSpec((T, PEER_SLOTS), lambda b, i: (b * nt + i, 0)),
            pl.BlockSpec((T, PEER_SLOTS), lambda b, i: (b * nt + i, 0)),
        ],
        out_shape=[
            jax.ShapeDtypeStruct((B, L, D), jnp.float32),
            jax.ShapeDtypeStruct((B * L, PEER_SLOTS), jnp.int32),
            jax.ShapeDtypeStruct((B * L, PEER_SLOTS), jnp.float32),
        ],
        scratch_shapes=[
            pltpu.VMEM((PEER_N_KEYS, T), jnp.float32),
            pltpu.VMEM((PEER_N_KEYS, T), jnp.int32),
            pltpu.VMEM((2 * PEER_TOPK, T), jnp.float32),
            pltpu.VMEM((2 * PEER_TOPK, T), jnp.int32),
            pltpu.VMEM((PEER_TOPK * PEER_TOPK, T), jnp.float32),
            pltpu.VMEM((PEER_TOPK * PEER_TOPK, T), jnp.int32),
            pltpu.VMEM((PEER_SLOTS, T), jnp.float32),
            pltpu.VMEM((PEER_SLOTS, T), jnp.int32),
            pltpu.VMEM((PEER_SLOTS, T), jnp.float32),
        ],
        compiler_params=pltpu.CompilerParams(dimension_semantics=("parallel", "parallel"),
                                             vmem_limit_bytes=VMEM_LIMIT_BYTES),
        name="peer_retrieve",
    )(x, gain.reshape(1, D), scale, shift, w_q.astype(jnp.bfloat16), keys)


PEER_ACT_ROWS = 1024


def _peer_act_kernel(dots_ref, gate_ref, w_ref):
    a = dots_ref[...]
    w_ref[...] = gate_ref[...] * (0.5 * a * (1.0 + lax.erf(a * (2.0 ** -0.5))))


def peer_act(dots, gate):
    N = dots.shape[0]
    T = min(PEER_ACT_ROWS, N)
    spec = pl.BlockSpec((T, PEER_SLOTS), lambda i: (i, 0))
    return pl.pallas_call(
        _peer_act_kernel,
        grid=(N // T,),
        in_specs=[spec, spec],
        out_specs=spec,
        out_shape=jax.ShapeDtypeStruct((N, PEER_SLOTS), jnp.float32),
        compiler_params=pltpu.CompilerParams(dimension_semantics=("parallel",)),
        name="peer_act",
    )(dots, gate)


def peer_ffn(x, gain, scale, shift, w_q, sub_keys, exp_u, exp_v):
    B, L, D = x.shape
    N = B * L
    h, e_idx, gate = peer_retrieve(x, gain, scale, shift, w_q, sub_keys)
    dots = _sc_peer(exp_u, e_idx, h.reshape(N, D), "dot")
    return _sc_peer(exp_v, e_idx, peer_act(dots, gate), "wsum").reshape(B, L, D)


def _mix_and_retrieve(li, x, c, ctx, c_ctx, mod_w, mod_b, mix_norm, w_in, w_out, rw_conv, rw_decay_up, rw_decay0, rw_a_up, rw_a0, rw_gate_up, rw_k_k, rw_k_a, rw_r_k, rw_gn_g, rw_gn_b, mla_q_norm, mla_w_uq, mla_kv_norm, mla_w_ukv, mla_q_gain, mla_k_gain, hy_conv, hy_w1, hy_b1, hy_freq1, hy_w2, hy_b2, hy_freq2, hy_w3, hy_b3, hy_bias, ffn_norm, peer_wq, peer_keys, peer_u, peer_v):
    B, L, D = x.shape
    s_rw, s_mla = RW_PROJ, RW_PROJ + MLA_PROJ
    need_ctx = li < DEPTH - 1
    mod_l = (jax.nn.silu(c) @ mod_w[li] + mod_b[li])[:, None, :]
    mod_c = (jax.nn.silu(c_ctx) @ mod_w[li] + mod_b[li])[None, None, :]
    shm_l, scm_l, gm_l, shf_l, scf_l, gf_l = jnp.split(mod_l, N_MOD, axis=-1)
    shm_c, scm_c, gm_c, shf_c, scf_c, gf_c = jnp.split(mod_c, N_MOD, axis=-1)

    p_l = norm_mod_proj(x, mix_norm[li], scm_l, shm_l, w_in[li], 512)
    p_c = norm_mod_proj(ctx, mix_norm[li], jnp.broadcast_to(scm_c, (B, 1, D)),
                        jnp.broadcast_to(shm_c, (B, 1, D)), w_in[li], 256)
    rw_l, rw_c = rwkv7_mixer(p_l[..., :s_rw], p_c[..., :s_rw], rw_conv[li], rw_decay_up[li], rw_decay0[li],
                             rw_a_up[li], rw_a0[li], rw_gate_up[li], rw_k_k[li], rw_k_a[li], rw_r_k[li],
                             rw_gn_g[li], rw_gn_b[li], need_ctx)
    ml_l, ml_c = mla_mixer(p_l[..., s_rw:s_mla], p_c[..., s_rw:s_mla], mla_q_norm[li], mla_w_uq[li],
                           mla_kv_norm[li], mla_w_ukv[li], mla_q_gain[li], mla_k_gain[li], need_ctx)
    hy_prm = (hy_conv[li], hy_w1[li], hy_b1[li], hy_freq1[li], hy_w2[li], hy_b2[li], hy_freq2[li],
              hy_w3[li], hy_b3[li], hy_bias[li])
    hy_l = hyena_mixer(p_l[..., s_mla:], *hy_prm)
    x = x + gm_l * (jnp.concatenate([rw_l, ml_l, hy_l], axis=-1) @ w_out[li])
    if need_ctx:
        hy_c = hyena_mixer(p_c[..., s_mla:], *hy_prm)
        ctx = ctx + gm_c * (jnp.concatenate([rw_c, ml_c, hy_c], axis=-1) @ w_out[li])
        ctx = ctx + gf_c * peer_ffn(ctx, ffn_norm[li], jnp.broadcast_to(scf_c, (B, 1, D)),
                                    jnp.broadcast_to(shf_c, (B, 1, D)),
                                    peer_wq[li], peer_keys[li], peer_u[li], peer_v[li])
    h, e_idx, gate = peer_retrieve(x, ffn_norm[li], scf_l, shf_l, peer_wq[li], peer_keys[li])
    return x, ctx, gf_l, h.reshape(B * L, D), e_idx, gate


BATCH_GROUP_ROWS = (1, 2, 2, 2, 1)


def kernel(x, c, ctx, c_ctx, mod_w, mod_b, mix_norm, w_in, w_out, rw_conv, rw_decay_up, rw_decay0, rw_a_up, rw_a0, rw_gate_up, rw_k_k, rw_k_a, rw_r_k, rw_gn_g, rw_gn_b, mla_q_norm, mla_w_uq, mla_kv_norm, mla_w_ukv, mla_q_gain, mla_k_gain, hy_conv, hy_w1, hy_b1, hy_freq1, hy_w2, hy_b2, hy_freq2, hy_w3, hy_b3, hy_bias, ffn_norm, peer_wq, peer_keys, peer_u, peer_v):
    params = (mod_w, mod_b, mix_norm, w_in, w_out, rw_conv, rw_decay_up, rw_decay0, rw_a_up, rw_a0, rw_gate_up,
              rw_k_k, rw_k_a, rw_r_k, rw_gn_g, rw_gn_b, mla_q_norm, mla_w_uq, mla_kv_norm, mla_w_ukv, mla_q_gain,
              mla_k_gain, hy_conv, hy_w1, hy_b1, hy_freq1, hy_w2, hy_b2, hy_freq2, hy_w3, hy_b3, hy_bias,
              ffn_norm, peer_wq, peer_keys)
    peer_u = [pack_expert_table(peer_u[li]) for li in range(DEPTH)]
    peer_v = [pack_expert_table(peer_v[li]) for li in range(DEPTH)]
    params = params + (peer_u, peer_v)
    assert sum(BATCH_GROUP_ROWS) == x.shape[0]
    G = len(BATCH_GROUP_ROWS)
    lo = [sum(BATCH_GROUP_ROWS[:g]) for g in range(G + 1)]
    L, D = x.shape[1:]
    xs = [x[lo[g]:lo[g + 1]] for g in range(G)]
    cs = [c[lo[g]:lo[g + 1]] for g in range(G)]
    ctxs = [ctx[lo[g]:lo[g + 1]] for g in range(G)]
    stages = [(li, g) for li in range(DEPTH) for g in range(G)]
    prev = None
    token = None
    for li, g in stages:
        ins = (xs[g], ctxs[g])
        if token is not None:
            token, ins = lax.optimization_barrier((token, ins))
        xm, ctxs[g], gf, h, e_idx, gate = _mix_and_retrieve(li, ins[0], cs[g], ins[1], c_ctx, *params)
        token = gate
        if prev is None:
            dots = _sc_peer(peer_u[li], e_idx, h, "dot")
        else:
            pl_, pg, pxm, pgf, pidx, pgate, pdots = prev
            token, (pdots, pgate) = lax.optimization_barrier((token, (pdots, pgate)))
            w = peer_act(pdots, pgate)
            token = w
            out, dots = _sc_peer_pair(peer_v[pl_], pidx, w, peer_u[li], e_idx, h)
            xs[pg] = pxm + pgf * out.reshape(pxm.shape)
        prev = (li, g, xm, gf, e_idx, gate, dots)
    pl_, pg, pxm, pgf, pidx, pgate, pdots = prev
    xs[pg] = pxm + pgf * _sc_peer(peer_v[pl_], pidx, peer_act(pdots, pgate), "wsum").reshape(pxm.shape)
    return jnp.concatenate(xs, axis=0)
```

```python
import functools
import math

import jax
import jax.numpy as jnp
import numpy as np
from jax import lax
from jax.experimental import pallas as pl
from jax.experimental.pallas import tpu as pltpu
from jax.experimental.pallas import tpu_sc as plsc

D_MODEL = 1024
DEPTH = 2
GRID_W = 64
N_MOD = 6
NORM_EPS = 1e-6

RW_HEADS = 6
RW_HEAD_DIM = 64
RW_WIDTH = RW_HEADS * RW_HEAD_DIM
RW_DECAY_RANK = 64
RW_A_RANK = 64
RW_GATE_RANK = 128
RW_DECAY_SCALE = 0.6065306597
RW_GN_EPS = 64e-5
L2_EPS = 1e-12

MLA_HEADS = 6
MLA_Q_RANK = 256
MLA_KV_RANK = 128
MLA_NOPE_DIM = 64
MLA_ROPE_DIM = 32
MLA_V_DIM = 64
MLA_QK_DIM = MLA_NOPE_DIM + MLA_ROPE_DIM
MLA_WIDTH = MLA_HEADS * MLA_V_DIM
AXIS_ROPE_DIM = MLA_ROPE_DIM // 2
ROPE_THETA = 10000.0

HY_WIDTH = 256
HY_ORDER = 2
HY_POS_BANDS = 16
HY_SHORT_DECAY_PCT = 0.3
HY_LONG_DECAY_PCT = 1.5
HY_DECAY_TARGET = 1e-2

PEER_HEADS = 8
PEER_N_KEYS = 128
PEER_TOPK = 16
PEER_QUERY_DIM = 256
PEER_HALF = PEER_QUERY_DIM // 2

RW_PROJ = 3 * RW_WIDTH + RW_DECAY_RANK + RW_A_RANK + RW_GATE_RANK
MLA_PROJ = MLA_Q_RANK + MLA_KV_RANK + MLA_ROPE_DIM
HY_PROJ = (HY_ORDER + 1) * HY_WIDTH
IN_PROJ = RW_PROJ + MLA_PROJ + HY_PROJ
MIX_WIDTH = RW_WIDTH + MLA_WIDTH + HY_WIDTH

VMEM_LIMIT_BYTES = 48 * 1024 * 1024


def _norm_mod_proj_kernel(x_ref, gain_ref, scale_ref, shift_ref, w_ref, o_ref):
    x = x_ref[0]
    y = x * lax.rsqrt(jnp.mean(x * x, axis=-1, keepdims=True) + NORM_EPS)
    y = y * gain_ref[...]
    y = y * (1.0 + scale_ref[0]) + shift_ref[0]
    o_ref[0] = jnp.dot(y.astype(jnp.bfloat16), w_ref[...], preferred_element_type=jnp.float32)


def norm_mod_proj(x, gain, scale, shift, w, block_rows):
    B, L, D = x.shape
    N = w.shape[1]
    return pl.pallas_call(
        _norm_mod_proj_kernel,
        grid=(B, L // block_rows),
        in_specs=[
            pl.BlockSpec((1, block_rows, D), lambda b, i: (b, i, 0)),
            pl.BlockSpec((1, D), lambda b, i: (0, 0)),
            pl.BlockSpec((1, 1, D), lambda b, i: (b, 0, 0)),
            pl.BlockSpec((1, 1, D), lambda b, i: (b, 0, 0)),
            pl.BlockSpec((D, N), lambda b, i: (0, 0)),
        ],
        out_specs=pl.BlockSpec((1, block_rows, N), lambda b, i: (b, i, 0)),
        out_shape=jax.ShapeDtypeStruct((B, L, N), jnp.float32),
        compiler_params=pltpu.CompilerParams(
            dimension_semantics=("parallel", "parallel"), vmem_limit_bytes=VMEM_LIMIT_BYTES),
        name="norm_mod_proj",
    )(x, gain.reshape(1, D), scale, shift, w.astype(jnp.bfloat16))


RW_CHUNK = 64


def _rwkv_chunk_kernel(r_ref, kk_ref, v_ref, lw_ref, akk_ref, kr_ref, y_ref, h_ref):
    d = pl.program_id(0)
    n = pl.program_id(2)

    @pl.when(n == 0)
    def _():
        h_ref[...] = jnp.zeros_like(h_ref)

    C = RW_CHUNK
    row = lax.broadcasted_iota(jnp.int32, (C, C), 0)
    col = lax.broadcasted_iota(jnp.int32, (C, C), 1)
    lag = (row - col) * (1 - 2 * d)
    before = lag > 0
    upto = lag >= 0
    tri = upto.astype(jnp.float32)
    eye = (row == col).astype(jnp.float32)
    bf = jnp.bfloat16
    f32 = jnp.float32

    def mm(a, b):
        return jnp.dot(a.astype(bf), b.astype(bf), preferred_element_type=f32)

    def mm_nt(a, b):
        return lax.dot_general(a.astype(bf), b.astype(bf), (((1,), (1,)), ((), ())), preferred_element_type=f32)

    def mm_tn(a, b):
        return lax.dot_general(a.astype(bf), b.astype(bf), (((0,), (0,)), ((), ())), preferred_element_type=f32)

    hs = range(RW_HEADS)
    HD = RW_HEAD_DIM
    heads = lambda t: [t[:, h * HD:(h + 1) * HD] for h in hs]
    r = heads(r_ref[0])
    kk = heads(kk_ref[0])
    v = heads(v_ref[0])
    lw = heads(lw_ref[0, 0])
    akk = heads(akk_ref[0, 0])
    kr = heads(kr_ref[0, 0])
    G = [jnp.dot(tri, lw[h], preferred_element_type=f32, precision=lax.Precision.HIGHEST) for h in hs]
    gtot = [jnp.sum(lw[h], axis=0, keepdims=True) for h in hs]
    Einv = [jnp.exp(-G[h]) for h in hs]
    At = [-kk[h] * jnp.exp(G[h] - lw[h]) for h in hs]
    Rt = [r[h] * jnp.exp(G[h]) for h in hs]
    Bt = [akk[h] * Einv[h] for h in hs]
    Kt = [kr[h] * Einv[h] for h in hs]
    X = [mm_nt(jnp.concatenate([At[h], Rt[h]], axis=0), jnp.concatenate([Bt[h], Kt[h]], axis=0)) for h in hs]
    M_ab = [jnp.where(before, X[h][:C, :C], 0.0) for h in hs]
    M_ak = [jnp.where(before, X[h][:C, C:], 0.0) for h in hs]
    A_rb = [jnp.where(upto, X[h][C:, :C], 0.0) for h in hs]
    A_rk = [jnp.where(upto, X[h][C:, C:], 0.0) for h in hs]
    MV = [mm(M_ak[h], v[h]) for h in hs]
    Mp = M_ab
    T = [eye + Mp[h] for h in hs]
    for _ in range(5):
        Mp = [jnp.dot(Mp[h], Mp[h], preferred_element_type=f32) for h in hs]
        T = [T[h] + jnp.dot(T[h], Mp[h], preferred_element_type=f32) for h in hs]
    WU = [jnp.dot(T[h], jnp.concatenate([At[h], MV[h]], axis=1), preferred_element_type=f32) for h in hs]
    H0 = [h_ref[h] for h in hs]
    Ehat = [jnp.exp(gtot[h] - G[h]) for h in hs]
    Om = [Rt[h] + mm(A_rb[h], WU[h][:, :HD]) for h in hs]
    Y0 = [mm(A_rb[h], WU[h][:, HD:]) + mm(A_rk[h], v[h]) for h in hs]
    BW = [mm_tn(akk[h] * Ehat[h], WU[h]) for h in hs]
    KV = [mm_tn(kr[h] * Ehat[h], v[h]) for h in hs]
    y_ref[0, 0] = jnp.concatenate([jnp.dot(Om[h], H0[h], preferred_element_type=f32) + Y0[h] for h in hs], axis=1)
    for h in hs:
        P = eye * jnp.exp(gtot[h]) + BW[h][:, :HD]
        h_ref[h] = jnp.dot(P, H0[h], preferred_element_type=f32) + BW[h][:, HD:] + KV[h]


def rwkv_chunked(r, kk, v, lw, akk, kr, n_ctx):
    B, T, W = r.shape
    H = W // RW_HEAD_DIM
    nc = n_ctx // RW_CHUNK
    nt = T // RW_CHUNK

    def chunk_of(d, n):
        bwd = jnp.where(n < nc, nc - 1 - n, nt - 1 - (n - nc))
        return jnp.where(d == 0, n, bwd)

    spec1 = pl.BlockSpec((1, RW_CHUNK, W), lambda d, b, n: (b, chunk_of(d, n), 0))
    spec2 = pl.BlockSpec((1, 1, RW_CHUNK, W), lambda d, b, n: (d, b, chunk_of(d, n), 0))
    return pl.pallas_call(
        _rwkv_chunk_kernel,
        grid=(2, B, nt),
        in_specs=[spec1, spec1, spec1, spec2, spec2, spec2],
        out_specs=spec2,
        out_shape=jax.ShapeDtypeStruct((2, B, T, W), jnp.float32),
        scratch_shapes=[pltpu.VMEM((H, RW_HEAD_DIM, RW_HEAD_DIM), jnp.float32)],
        compiler_params=pltpu.CompilerParams(dimension_semantics=("parallel", "parallel", "arbitrary")),
        name="rwkv_chunked",
    )(r, kk, v, lw, akk, kr)


def short_conv(x, w):
    xp = jnp.pad(x, ((0, 0), (1, 1), (0, 0)))
    return xp[:, :-2] * w[0] + xp[:, 1:-1] * w[1] + xp[:, 2:] * w[2]


LANE = 128
RW_PREP_ROWS = 256
MLA_PAD_WIDTH = MLA_HEADS * LANE
MLA_PREP_ROWS = 256
ATTN_Q_ROWS = 512


def _split_dot(x, m):
    hi = x.astype(jnp.bfloat16)
    lo = (x - hi.astype(jnp.float32)).astype(jnp.bfloat16)
    return (jnp.dot(hi, m, preferred_element_type=jnp.float32) + jnp.dot(lo, m, preferred_element_type=jnp.float32))


def _rwkv_prep_kernel(z_ref, wda_ref, d0_ref, a0_ref, gup_ref, kk_ref_w, ka_ref, rk_ref, hsum_ref,
                      r_ref, kk_ref, v_ref, lw_ref, akk_ref, kr_ref, g_ref, bonus_ref):
    W = RW_WIDTH
    bf = jnp.bfloat16
    z = z_ref[0]
    r, k, v = z[:, :W], z[:, W:2 * W], z[:, 2 * W:3 * W]
    da = z[:, 3 * W:3 * W + LANE]
    lane = lax.broadcasted_iota(jnp.int32, da.shape, 1)
    da = jnp.where(lane < RW_DECAY_RANK, jnp.tanh(da), da)
    up = jnp.dot(da.astype(bf), wda_ref[...], preferred_element_type=jnp.float32)
    g_lo = z[:, 3 * W + LANE:]
    g_ref[0] = jnp.dot(jax.nn.sigmoid(g_lo).astype(bf), gup_ref[...], preferred_element_type=jnp.float32)
    hsum = hsum_ref[...]
    kk = k * kk_ref_w[...]
    kk = kk * lax.rsqrt(_split_dot(kk * kk, hsum) + L2_EPS)
    r_ref[0] = r
    v_ref[0] = v
    kk_ref[0] = kk
    bonus_ref[0] = _split_dot(r * k * rk_ref[...], hsum) * v
    for d in range(2):
        lw_ref[d, 0] = -RW_DECAY_SCALE * jax.nn.sigmoid(d0_ref[d:d + 1, :] + up[:, d * W:(d + 1) * W])
        a = jax.nn.sigmoid(a0_ref[d:d + 1, :] + up[:, (2 + d) * W:(3 + d) * W])
        akk_ref[d, 0] = kk * a
        kr_ref[d, 0] = k * (1.0 + (a - 1.0) * ka_ref[...])


def rwkv_prep(z, decay_up, decay0, a_up, a0, gate_up, k_k, k_a, r_k):
    B, L, _ = z.shape
    W = RW_WIDTH
    T = min(RW_PREP_ROWS, L)
    zero = jnp.zeros((RW_DECAY_RANK, 2 * W), jnp.float32)
    wda = jnp.concatenate([
        jnp.concatenate([decay_up[0], decay_up[1], zero], axis=1),
        jnp.concatenate([zero, a_up[0], a_up[1]], axis=1)], axis=0).astype(jnp.bfloat16)
    head = jnp.arange(W) // RW_HEAD_DIM
    hsum = (head[:, None] == head[None, :]).astype(jnp.bfloat16)
    row = lambda a: a.reshape(1, W)
    const = lambda a: pl.BlockSpec(a.shape, lambda b, i: (0,) * a.ndim)
    tok = pl.BlockSpec((1, T, W), lambda b, i: (b, i, 0))
    tok2 = pl.BlockSpec((2, 1, T, W), lambda b, i: (0, b, i, 0))
    f1 = jax.ShapeDtypeStruct((B, L, W), jnp.float32)
    f2 = jax.ShapeDtypeStruct((2, B, L, W), jnp.float32)
    args = (z, wda, decay0, a0, gate_up.astype(jnp.bfloat16), row(k_k), row(k_a), row(r_k), hsum)
    return pl.pallas_call(
        _rwkv_prep_kernel,
        grid=(B, L // T),
        in_specs=[pl.BlockSpec((1, T, RW_PROJ), lambda b, i: (b, i, 0))] + [const(a) for a in args[1:]],
        out_specs=[tok, tok, tok, tok2, tok2, tok2, tok, tok],
        out_shape=[f1, f1, f1, f2, f2, f2, f1, f1],
        compiler_params=pltpu.CompilerParams(dimension_semantics=("parallel", "parallel"),
                                             vmem_limit_bytes=VMEM_LIMIT_BYTES),
        name="rwkv_prep",
    )(*args)


def _rwkv_readout_kernel(y_ref, g_ref, bonus_ref, gng_ref, gnb_ref, hsum_ref, o_ref):
    y = y_ref[0, 0] + y_ref[1, 0]
    hsum = hsum_ref[...]
    mu = _split_dot(y, hsum) * (1.0 / RW_HEAD_DIM)
    d = y - mu
    var = _split_dot(d * d, hsum) * (1.0 / RW_HEAD_DIM)
    yn = d * lax.rsqrt(var + RW_GN_EPS) * gng_ref[...] + gnb_ref[...]
    o_ref[0] = (yn + bonus_ref[0]) * g_ref[0]


def rwkv_readout(y, g, bonus, gn_g, gn_b, t0):
    B, L, W = g.shape
    T = min(RW_PREP_ROWS, L)
    off = t0 // T
    head = jnp.arange(W) // RW_HEAD_DIM
    hsum = (head[:, None] == head[None, :]).astype(jnp.bfloat16)
    tok = pl.BlockSpec((1, T, W), lambda b, i: (b, i, 0))
    const = lambda a: pl.BlockSpec(a.shape, lambda b, i: (0,) * a.ndim)
    gg, gb = gn_g.reshape(1, W), gn_b.reshape(1, W)
    return pl.pallas_call(
        _rwkv_readout_kernel,
        grid=(B, L // T),
        in_specs=[pl.BlockSpec((2, 1, T, W), lambda b, i: (0, b, i + off, 0)), tok, tok, const(gg), const(gb), const(hsum)],
        out_specs=tok,
        out_shape=jax.ShapeDtypeStruct((B, L, W), jnp.float32),
        compiler_params=pltpu.CompilerParams(dimension_semantics=("parallel", "parallel")),
        name="rwkv_readout",
    )(y, g, bonus, gg, gb, hsum)


def rwkv7_mixer(p_lat, p_ctx, conv_w, decay_up, decay0, a_up, a0, gate_up, k_k, k_a, r_k, gn_g, gn_b, need_ctx):
    prm = (decay_up, decay0, a_up, a0, gate_up, k_k, k_a, r_k)
    lat = rwkv_prep(short_conv(p_lat, conv_w), *prm)
    ctx = rwkv_prep(short_conv(p_ctx, conv_w), *prm)
    n_ctx = p_ctx.shape[1]
    seq = lambda i: jnp.concatenate([ctx[i], lat[i]], axis=-2)
    y = rwkv_chunked(seq(0), seq(1), seq(2), seq(3), seq(4), seq(5), n_ctx)
    out_l = rwkv_readout(y, lat[6], lat[7], gn_g, gn_b, n_ctx)
    out_c = rwkv_readout(y, ctx[6], ctx[7], gn_g, gn_b, 0) if need_ctx else None
    return out_l, out_c


def _rope_tables(L, use_rope):
    lane = np.arange(LANE)
    in_rope = (lane >= MLA_NOPE_DIM) & (lane < MLA_QK_DIM)
    j = lane - MLA_NOPE_DIM
    axis = j // AXIS_ROPE_DIM
    half = AXIS_ROPE_DIM // 2
    f = j % half
    first = (j % AXIS_ROPE_DIM) < half
    inv = ROPE_THETA ** (-jnp.arange(0, AXIS_ROPE_DIM, 2, dtype=jnp.float32) / AXIS_ROPE_DIM)
    t = jnp.arange(L)
    pos = jnp.stack([t // GRID_W, t % GRID_W], axis=-1).astype(jnp.float32)
    ang = pos[:, np.clip(axis, 0, 1)] * inv[np.clip(f, 0, half - 1)][None, :]
    rope_on = jnp.asarray(in_rope)[None, :] & use_rope
    cos = jnp.where(rope_on, jnp.cos(ang), 1.0)
    sin = jnp.where(rope_on, jnp.sin(ang) * jnp.where(jnp.asarray(first), -1.0, 1.0)[None, :], 0.0)
    return jnp.tile(cos, (1, MLA_HEADS)), jnp.tile(sin, (1, MLA_HEADS))


def _mla_prep_kernel(p_ref, qn_ref, wq_ref, kvn_ref, wk_ref, wv_ref, place_ref, qg_ref, kg_ref, hsum_ref, cos_ref, sin_ref,
                     q_ref, k_ref, v_ref):
    bf = jnp.bfloat16
    p = p_ref[0]
    c_q = p[:, :MLA_Q_RANK]
    c_kv = p[:, MLA_Q_RANK:MLA_Q_RANK + MLA_KV_RANK]
    tail = p[:, MLA_Q_RANK + MLA_KV_RANK:]
    cqn = c_q * lax.rsqrt(jnp.mean(c_q * c_q, axis=-1, keepdims=True) + NORM_EPS) * qn_ref[...]
    ckn = c_kv * lax.rsqrt(jnp.mean(c_kv * c_kv, axis=-1, keepdims=True) + NORM_EPS) * kvn_ref[...]
    q = jnp.dot(cqn.astype(bf), wq_ref[...], preferred_element_type=jnp.float32)
    k = jnp.dot(ckn.astype(bf), wk_ref[...], preferred_element_type=jnp.float32) + _split_dot(tail, place_ref[...])
    v_ref[0] = jnp.dot(ckn.astype(bf), wv_ref[...], preferred_element_type=jnp.float32).astype(bf)
    hsum = hsum_ref[...]
    cos, sin = cos_ref[...], sin_ref[...]
    lane = lax.broadcasted_iota(jnp.int32, q.shape, 1)
    first = ((lane - MLA_NOPE_DIM) % AXIS_ROPE_DIM) < (AXIS_ROPE_DIM // 2)
    half = AXIS_ROPE_DIM // 2

    def finish(x, gain):
        x = x * lax.rsqrt(_split_dot(x * x, hsum) * (1.0 / MLA_QK_DIM) + NORM_EPS) * gain
        partner = jnp.where(first, pltpu.roll(x, MLA_PAD_WIDTH - half, 1), pltpu.roll(x, half, 1))
        return x * cos + partner * sin

    q_ref[0] = (finish(q, qg_ref[...]) * (MLA_QK_DIM ** -0.5)).astype(bf)
    k_ref[0] = finish(k, kg_ref[...]).astype(bf)


def mla_prep(p, use_rope, q_norm, w_uq, kv_norm, w_ukv, q_gain, k_gain):
    B, L, _ = p.shape
    T = min(MLA_PREP_ROWS, L)
    H = MLA_HEADS
    pad_cols = lambda w, d: jnp.pad(w.reshape(w.shape[0], H, d), ((0, 0), (0, 0), (0, LANE - d))).reshape(w.shape[0], H * LANE)
    wq = pad_cols(w_uq, MLA_QK_DIM).astype(jnp.bfloat16)
    ukv = w_ukv.reshape(MLA_KV_RANK, H, MLA_NOPE_DIM + MLA_V_DIM)
    wk = pad_cols(ukv[:, :, :MLA_NOPE_DIM].reshape(MLA_KV_RANK, H * MLA_NOPE_DIM), MLA_NOPE_DIM).astype(jnp.bfloat16)
    wv = ukv[:, :, MLA_NOPE_DIM:].reshape(MLA_KV_RANK, H * MLA_V_DIM).astype(jnp.bfloat16)
    lane = np.arange(H * LANE)
    place = jnp.asarray(((lane[None, :] % LANE) - MLA_NOPE_DIM == np.arange(MLA_ROPE_DIM)[:, None]), jnp.bfloat16)
    hsum = jnp.asarray((lane[:, None] // LANE) == (lane[None, :] // LANE), jnp.bfloat16)
    pad_gain = lambda g: jnp.tile(jnp.pad(g, (0, LANE - MLA_QK_DIM)), H).reshape(1, H * LANE)
    cos, sin = _rope_tables(L, use_rope)
    const = lambda a: pl.BlockSpec(a.shape, lambda b, i: (0,) * a.ndim)
    args = (p, q_norm.reshape(1, -1), wq, kv_norm.reshape(1, -1), wk, wv, place, pad_gain(q_gain), pad_gain(k_gain), hsum)
    pos = pl.BlockSpec((T, H * LANE), lambda b, i: (i, 0))
    return pl.pallas_call(
        _mla_prep_kernel,
        grid=(B, L // T),
        in_specs=[pl.BlockSpec((1, T, MLA_PROJ), lambda b, i: (b, i, 0))] + [const(a) for a in args[1:]] + [pos, pos],
        out_specs=[pl.BlockSpec((1, T, H * LANE), lambda b, i: (b, i, 0)), pl.BlockSpec((1, T, H * LANE), lambda b, i: (b, i, 0)),
                   pl.BlockSpec((1, T, MLA_WIDTH), lambda b, i: (b, i, 0))],
        out_shape=[jax.ShapeDtypeStruct((B, L, H * LANE), jnp.bfloat16), jax.ShapeDtypeStruct((B, L, H * LANE), jnp.bfloat16),
                   jax.ShapeDtypeStruct((B, L, MLA_WIDTH), jnp.bfloat16)],
        compiler_params=pltpu.CompilerParams(dimension_semantics=("parallel", "parallel"),
                                             vmem_limit_bytes=VMEM_LIMIT_BYTES),
        name="mla_prep",
    )(*args, cos, sin)


def _attn_kernel(q_ref, k_ref, v_ref, o_ref):
    lane = lax.broadcasted_iota(jnp.int32, (q_ref.shape[1], LANE), 1)
    for pair in range(MLA_HEADS // 2):
        v_pair = v_ref[0, :, pair * LANE:(pair + 1) * LANE]
        outs = []
        for h in (2 * pair, 2 * pair + 1):
            q = q_ref[0, :, h * LANE:(h + 1) * LANE]
            k = k_ref[0, :, h * LANE:(h + 1) * LANE]
            s = lax.dot_general(q, k, (((1,), (1,)), ((), ())), preferred_element_type=jnp.float32)
            e = jnp.exp(s - jnp.max(s, axis=-1, keepdims=True))
            o = jnp.dot(e.astype(jnp.bfloat16), v_pair, preferred_element_type=jnp.float32)
            outs.append(o / jnp.sum(e, axis=-1, keepdims=True))
        o_ref[0, :, pair * LANE:(pair + 1) * LANE] = jnp.where(lane < MLA_V_DIM, outs[0], outs[1])


def attention(q, k, v):
    B, Lq, P = q.shape
    Lk = k.shape[1]
    tq = min(ATTN_Q_ROWS, Lq)
    return pl.pallas_call(
        _attn_kernel,
        grid=(B, Lq // tq),
        in_specs=[pl.BlockSpec((1, tq, P), lambda b, i: (b, i, 0)),
                  pl.BlockSpec((1, Lk, P), lambda b, i: (b, 0, 0)),
                  pl.BlockSpec((1, Lk, MLA_WIDTH), lambda b, i: (b, 0, 0))],
        out_specs=pl.BlockSpec((1, tq, MLA_WIDTH), lambda b, i: (b, i, 0)),
        out_shape=jax.ShapeDtypeStruct((B, Lq, MLA_WIDTH), jnp.float32),
        compiler_params=pltpu.CompilerParams(dimension_semantics=("parallel", "parallel"),
                                             vmem_limit_bytes=VMEM_LIMIT_BYTES),
        name="mla_attention",
    )(q, k, v)


def mla_mixer(p_lat, p_ctx, q_norm, w_uq, kv_norm, w_ukv, q_gain, k_gain, need_ctx):
    prm = (q_norm, w_uq, kv_norm, w_ukv, q_gain, k_gain)
    q_l, k_l, v_l = mla_prep(p_lat, True, *prm)
    q_c, k_c, v_c = mla_prep(p_ctx, False, *prm)
    y_l = attention(q_l, jnp.concatenate([k_l, k_c], axis=1), jnp.concatenate([v_l, v_c], axis=1))
    y_c = attention(q_c, k_c, v_c) if need_ctx else None
    return y_l, y_c


def hyena_filters(L, w1, b1, freq1, w2, b2, freq2, w3, b3):
    tn = jnp.arange(L, dtype=jnp.float32) / L
    bands = jnp.arange(1, HY_POS_BANDS + 1, dtype=jnp.float32)
    ang = 2.0 * math.pi * tn[:, None] * bands[None, :]
    z = jnp.concatenate([tn[:, None], jnp.cos(ang), jnp.sin(ang)], axis=-1)
    h = jnp.sin(freq1 * (z @ w1 + b1))
    h = jnp.sin(freq2 * (h @ w2 + b2))
    h = (h @ w3 + b3).reshape(L, HY_ORDER, 2, HY_WIDTH)
    rates = jnp.abs(jnp.linspace(math.log(HY_DECAY_TARGET) / HY_LONG_DECAY_PCT,
                                 math.log(HY_DECAY_TARGET) / HY_SHORT_DECAY_PCT, HY_WIDTH))
    h = h * jnp.exp(-tn[:, None] * rates[None, :])[:, None, None, :]
    zero = jnp.zeros((1, HY_ORDER, HY_WIDTH), h.dtype)
    h_full = jnp.concatenate([h[:, :, 0], zero, h[:0:-1, :, 1]], axis=0)
    return h_full * lax.rsqrt(jnp.sum(jnp.square(h_full), axis=0, keepdims=True))


def fft_long_conv(u, h_full, bias):
    L = u.shape[1]
    uf = jnp.fft.rfft(u, n=2 * L, axis=1)
    hf = jnp.fft.rfft(h_full, n=2 * L, axis=0)
    y = jnp.fft.irfft(uf * hf[None], n=2 * L, axis=1)[:, :L]
    return y + u * bias


FFT_N1 = 64
FFT_N2 = 128
FFT_N = FFT_N1 * FFT_N2
HY_SEQS = 32


def _dft_tables(seqs):
    n1 = np.arange(FFT_N1)
    n2 = np.arange(FFT_N2)
    f64 = np.exp(-2j * np.pi * np.outer(n1, n1) / FFT_N1)
    f128 = np.exp(-2j * np.pi * np.outer(n2, n2) / FFT_N2)
    tw = np.exp(-2j * np.pi * np.outer(n1, n2) / FFT_N)
    half = FFT_N1 // 2
    fh = f64[:, :half]
    m1 = np.block([[fh.real, -fh.imag], [fh.imag, fh.real]])
    m1f = np.concatenate([f64.real, f64.imag], axis=0)
    m2 = np.block([[f128.real, f128.imag], [-f128.imag, f128.real]])
    m3 = np.block([[f128.real, -f128.imag], [f128.imag, f128.real]]) / FFT_N
    c = np.conj(f64)[:half, :]
    m4 = np.block([[c.real, -c.imag], [c.imag, c.real]])
    bf = lambda a: jnp.asarray(a, jnp.float32).astype(jnp.bfloat16)
    f32 = lambda a: jnp.asarray(a, jnp.float32)
    return dict(m1=bf(m1), m1f=bf(m1f), m2=bf(m2), m3=bf(m3), m4=bf(m4),
                twr_l=f32(np.tile(tw.real, (1, seqs))), twi_l=f32(np.tile(tw.imag, (1, seqs))),
                twr_s=f32(np.tile(tw.real, (seqs, 1))), twi_s=f32(np.tile(tw.imag, (seqs, 1))))


def _spectrum(cols, m1, twr_l, twi_l, m2, R):
    a = jnp.dot(m1, cols.astype(jnp.bfloat16), preferred_element_type=jnp.float32)
    ar, ai = a[:FFT_N1], a[FFT_N1:]
    pr = ar * twr_l - ai * twi_l
    pi = ar * twi_l + ai * twr_l
    lhs = jnp.concatenate(
        [jnp.concatenate([pr[:, r * FFT_N2:(r + 1) * FFT_N2], pi[:, r * FFT_N2:(r + 1) * FFT_N2]], axis=1)
         for r in range(R)], axis=0)
    return jnp.dot(lhs.astype(jnp.bfloat16), m2, preferred_element_type=jnp.float32)


def _filter_fft_kernel(h_ref, m1f_ref, twr_ref, twi_ref, m2_ref, o_ref):
    R = HY_SEQS
    cols = jnp.concatenate([h_ref[r] for r in range(R)], axis=1)
    x = _spectrum(cols, m1f_ref[...], twr_ref[...], twi_ref[...], m2_ref[...], R)
    o_ref[...] = x.reshape(R, FFT_N1, 2 * FFT_N2)


def _hyena_conv_kernel(y_ref, g_ref, hf_ref, bias_ref, m1_ref, twr_l_ref, twi_l_ref, m2_ref, m3_ref,
                       twr_s_ref, twi_s_ref, m4_ref, o_ref):
    R = HY_SEQS
    half = FFT_N1 // 2
    y = [y_ref[0], y_ref[1]]
    for o in range(HY_ORDER):
        top = jnp.concatenate([y[0][r] for r in range(R)], axis=1)
        bot = jnp.concatenate([y[1][r] for r in range(R)], axis=1)
        x = _spectrum(jnp.concatenate([top, bot], axis=0), m1_ref[...], twr_l_ref[...], twi_l_ref[...], m2_ref[...], R)
        hf = hf_ref[o].reshape(R * FFT_N1, 2 * FFT_N2)
        xr, xi = x[:, :FFT_N2], x[:, FFT_N2:]
        hr, hi = hf[:, :FFT_N2], hf[:, FFT_N2:]
        yc = jnp.concatenate([xr * hr - xi * hi, xr * hi + xi * hr], axis=1)
        b = jnp.dot(yc.astype(jnp.bfloat16), m3_ref[...], preferred_element_type=jnp.float32)
        br, bi = b[:, :FFT_N2], b[:, FFT_N2:]
        qr = br * twr_s_ref[...] + bi * twi_s_ref[...]
        qi = bi * twr_s_ref[...] - br * twi_s_ref[...]
        bc = jnp.concatenate(
            [jnp.concatenate([qr[r * FFT_N1:(r + 1) * FFT_N1], qi[r * FFT_N1:(r + 1) * FFT_N1]], axis=0)
             for r in range(R)], axis=1)
        yo = jnp.dot(m4_ref[...], bc.astype(jnp.bfloat16), preferred_element_type=jnp.float32)
        for p in range(2):
            conv = jnp.stack([yo[p * half:(p + 1) * half, r * FFT_N2:(r + 1) * FFT_N2] for r in range(R)], axis=0)
            y[p] = g_ref[o, p] * (conv + y[p] * bias_ref[o])
    o_ref[0] = y[0]
    o_ref[1] = y[1]


def hyena_long_conv(y_t, g_t, h_t, bias):
    B, C, L = y_t.shape
    assert 2 * L == FFT_N and B % 2 == 0 and C % HY_SEQS == 0
    R = HY_SEQS
    half = FFT_N1 // 2
    tb = _dft_tables(R)
    const = lambda a: pl.BlockSpec(a.shape, lambda *_: (0,) * a.ndim)
    hf = pl.pallas_call(
        _filter_fft_kernel,
        grid=(HY_ORDER * C // R,),
        in_specs=[pl.BlockSpec((R, FFT_N1, FFT_N2), lambda i: (i, 0, 0)),
                  const(tb['m1f']), const(tb['twr_l']), const(tb['twi_l']), const(tb['m2'])],
        out_specs=pl.BlockSpec((R, FFT_N1, 2 * FFT_N2), lambda i: (i, 0, 0)),
        out_shape=jax.ShapeDtypeStruct((HY_ORDER * C, FFT_N1, 2 * FFT_N2), jnp.float32),
        compiler_params=pltpu.CompilerParams(dimension_semantics=("parallel",), vmem_limit_bytes=VMEM_LIMIT_BYTES),
        name="hyena_filter_fft",
    )(h_t.reshape(HY_ORDER * C, FFT_N1, FFT_N2), tb['m1f'], tb['twr_l'], tb['twi_l'], tb['m2'])
    hf = hf.reshape(HY_ORDER, C, FFT_N1, 2 * FFT_N2)
    out = pl.pallas_call(
        _hyena_conv_kernel,
        grid=(B // 2, C // R),
        in_specs=[pl.BlockSpec((2, R, half, FFT_N2), lambda b, c: (b, c, 0, 0)),
                  pl.BlockSpec((HY_ORDER, 2, R, half, FFT_N2), lambda b, c: (0, b, c, 0, 0)),
                  pl.BlockSpec((HY_ORDER, R, FFT_N1, 2 * FFT_N2), lambda b, c: (0, c, 0, 0)),
                  pl.BlockSpec((HY_ORDER, R, 1, 1), lambda b, c: (0, c, 0, 0)),
                  const(tb['m1']), const(tb['twr_l']), const(tb['twi_l']), const(tb['m2']), const(tb['m3']),
                  const(tb['twr_s']), const(tb['twi_s']), const(tb['m4'])],
        out_specs=pl.BlockSpec((2, R, half, FFT_N2), lambda b, c: (b, c, 0, 0)),
        out_shape=jax.ShapeDtypeStruct((B, C, half, FFT_N2), jnp.float32),
        compiler_params=pltpu.CompilerParams(dimension_semantics=("parallel", "parallel"),
                                             vmem_limit_bytes=VMEM_LIMIT_BYTES),
        name="hyena_conv",
    )(y_t.reshape(B, C, half, FFT_N2), g_t.reshape(HY_ORDER, B, C, half, FFT_N2), hf,
      bias.reshape(HY_ORDER, C, 1, 1), tb['m1'], tb['twr_l'], tb['twi_l'], tb['m2'], tb['m3'],
      tb['twr_s'], tb['twi_s'], tb['m4'])
    return out.reshape(B, C, L)


def hyena_mixer(p, conv_w, w1, b1, freq1, w2, b2, freq2, w3, b3, bias):
    B, L = p.shape[:2]
    z = short_conv(p, conv_w)
    h_full = hyena_filters(L, w1, b1, freq1, w2, b2, freq2, w3, b3)
    if 2 * L == FFT_N:
        g_t = jnp.transpose(z[..., :HY_ORDER * HY_WIDTH].reshape(B, L, HY_ORDER, HY_WIDTH), (2, 0, 3, 1))
        y_t = jnp.swapaxes(z[..., HY_ORDER * HY_WIDTH:], 1, 2)
        y_t = hyena_long_conv(y_t, g_t, jnp.transpose(h_full, (1, 2, 0)), bias)
        return jnp.swapaxes(y_t, 1, 2)
    gates = (z[..., :HY_WIDTH], z[..., HY_WIDTH:2 * HY_WIDTH])
    y = z[..., 2 * HY_WIDTH:]
    for o in range(HY_ORDER):
        y = gates[o] * fft_long_conv(y, h_full[:, o], bias[o])
    return y


SC_CORES = 2
SC_SUBCORES = 16
SC_LANES = 16
SC_WORKERS = SC_CORES * SC_SUBCORES
PEER_SLOTS = PEER_HEADS * PEER_TOPK
PEER_GATHER_ROWS = 32
PEER_GATHERS = PEER_SLOTS // PEER_GATHER_ROWS
PEER_ACC_VREGS = 8
PEER_ROW_BUFFERS = 4
PEER_ROW_WORDS = D_MODEL // 2
HI_MASK = -65536


def pack_expert_table(t):
    b = lax.bitcast_convert_type(t.astype(jnp.bfloat16), jnp.uint16).astype(jnp.uint32)
    return lax.bitcast_convert_type(b[:, :PEER_ROW_WORDS] | (b[:, PEER_ROW_WORDS:] << 16), jnp.int32)


def _sc_peer_phase(phase, tpw):
    NBUF = PEER_ROW_BUFFERS
    AHEAD = NBUF - 1
    HW = PEER_ROW_WORDS

    def run(base, table_hbm, idx_hbm, aux_hbm, out_hbm, idx_v, aux_v, rows_v, out_v, sem_r, sem_i, sem_o):

        def gather(p, c, b):
            return pltpu.make_async_copy(table_hbm.at[idx_v.at[p, c]], rows_v.at[b], sem_r.at[b])

        def load_meta(t, p):
            return (pltpu.make_async_copy(idx_hbm.at[t], idx_v.at[p], sem_i.at[p]),
                    pltpu.make_async_copy(aux_hbm.at[t], aux_v.at[p], sem_i.at[p]))

        def store_out(t, p):
            return pltpu.make_async_copy(out_v.at[p], out_hbm.at[t], sem_o.at[p])

        def halves(word):
            return (plsc.bitcast(lax.shift_left(word, 16), jnp.float32), plsc.bitcast(word & HI_MASK, jnp.float32))

        def compute(p, c, b):
            if phase == "dot":
                lane = lax.iota(jnp.int32, SC_LANES)
                vec = jnp.zeros((SC_LANES,), jnp.float32)
                groups_per_vec = SC_LANES // PEER_ACC_VREGS
                for g in range(PEER_GATHER_ROWS // PEER_ACC_VREGS):
                    def body(cc, accs):
                        x_lo = aux_v[p, pl.ds(cc * SC_LANES, SC_LANES)]
                        x_hi = aux_v[p, pl.ds(HW + cc * SC_LANES, SC_LANES)]
                        out = []
                        for r in range(PEER_ACC_VREGS):
                            lo, hi = halves(rows_v[b, g * PEER_ACC_VREGS + r, pl.ds(cc * SC_LANES, SC_LANES)])
                            out.append(accs[r] + lo * x_lo + hi * x_hi)
                        return tuple(out)
                    accs = lax.fori_loop(0, HW // SC_LANES, body,
                                         tuple(jnp.zeros((SC_LANES,), jnp.float32) for _ in range(PEER_ACC_VREGS)))
                    for r in range(PEER_ACC_VREGS):
                        vec = jnp.where(lane == (g % groups_per_vec) * PEER_ACC_VREGS + r, jnp.sum(accs[r]), vec)
                    if g % groups_per_vec == groups_per_vec - 1:
                        out_v[p, pl.ds(c * PEER_GATHER_ROWS + (g // groups_per_vec) * SC_LANES, SC_LANES)] = vec
            else:
                words = PEER_ACC_VREGS // 2
                for db in range(HW // (words * SC_LANES)):
                    def body(kk, accs):
                        wv = plsc.load_gather(aux_v.at[p], [jnp.full((SC_LANES,), c * PEER_GATHER_ROWS + kk, jnp.int32)])
                        out = []
                        for j in range(words):
                            lo, hi = halves(rows_v[b, kk, pl.ds((db * words + j) * SC_LANES, SC_LANES)])
                            out += [accs[2 * j] + lo * wv, accs[2 * j + 1] + hi * wv]
                        return tuple(out)
                    if c == 0:
                        init = tuple(jnp.zeros((SC_LANES,), jnp.float32) for _ in range(2 * words))
                    else:
                        init = tuple(out_v[p, pl.ds(half * HW + (db * words + j) * SC_LANES, SC_LANES)]
                                     for j in range(words) for half in range(2))
                    accs = lax.fori_loop(0, PEER_GATHER_ROWS, body, init)
                    for j in range(words):
                        out_v[p, pl.ds((db * words + j) * SC_LANES, SC_LANES)] = accs[2 * j]
                        out_v[p, pl.ds(HW + (db * words + j) * SC_LANES, SC_LANES)] = accs[2 * j + 1]

        for d in load_meta(base, 0):
            d.start()
        for d in load_meta(base, 0):
            d.wait()
        for c in range(AHEAD):
            gather(0, c, c % NBUF).start()

        @pl.loop(0, tpw // 2)
        def _(i2):
            for p in range(2):
                i = i2 * 2 + p
                t = base + i
                nxt = base + jnp.minimum(i + 1, tpw - 1)
                for d in load_meta(nxt, 1 - p):
                    d.start()

                @pl.when(i2 > 0)
                def _():
                    store_out(t, p).wait()

                for c in range(PEER_GATHERS):
                    ahead = c + AHEAD
                    if ahead < PEER_GATHERS:
                        gather(p, ahead, ahead % NBUF).start()
                    else:
                        if ahead == PEER_GATHERS:
                            for d in load_meta(nxt, 1 - p):
                                d.wait()
                        gather(1 - p, ahead - PEER_GATHERS, ahead % NBUF).start()
                    gather(p, c, c % NBUF).wait()
                    compute(p, c, c % NBUF)
                store_out(t, p).start()

        for c in range(AHEAD):
            gather(0, c, c % NBUF).wait()
        for p in range(2):
            store_out(base, p).wait()

    return run


def _sc_tokens_per_worker(N):
    assert N % (2 * SC_WORKERS) == 0 and PEER_GATHERS % PEER_ROW_BUFFERS == 0
    return N // SC_WORKERS


_SC_AUX = {"dot": (D_MODEL,), "wsum": (PEER_SLOTS,)}
_SC_OUT = {"dot": (PEER_SLOTS,), "wsum": (D_MODEL,)}


def _sc_scratch(phases):
    s = [pltpu.VMEM((2, PEER_GATHERS, PEER_GATHER_ROWS), jnp.int32),
         pltpu.VMEM((PEER_ROW_BUFFERS, PEER_GATHER_ROWS, PEER_ROW_WORDS), jnp.int32),
         pltpu.SemaphoreType.DMA((PEER_ROW_BUFFERS,)), pltpu.SemaphoreType.DMA((2,)), pltpu.SemaphoreType.DMA((2,))]
    for ph in phases:
        s += [pltpu.VMEM((2,) + _SC_AUX[ph], jnp.float32), pltpu.VMEM((2,) + _SC_OUT[ph], jnp.float32)]
    return s


def _sc_peer(table, idx, aux, phase):
    N = idx.shape[0]
    tpw = _sc_tokens_per_worker(N)
    run = _sc_peer_phase(phase, tpw)

    @functools.partial(
        pl.kernel, mesh=plsc.VectorSubcoreMesh(core_axis_name="c", subcore_axis_name="s"),
        out_type=jax.ShapeDtypeStruct((N,) + _SC_OUT[phase], jnp.float32),
        compiler_params=pltpu.CompilerParams(needs_layout_passes=False),
        scratch_types=_sc_scratch([phase]),
    )
    def k(table_hbm, idx_hbm, aux_hbm, out_hbm, idx_v, rows_v, sem_r, sem_i, sem_o, aux_v, out_v):
        base = (lax.axis_index("s") * SC_CORES + lax.axis_index("c")) * tpw
        run(base, table_hbm, idx_hbm, aux_hbm, out_hbm, idx_v, aux_v, rows_v, out_v, sem_r, sem_i, sem_o)

    return k(table, idx.reshape(N, PEER_GATHERS, PEER_GATHER_ROWS), aux)


def _sc_peer_pair(table_v, idx_a, w_a, table_u, idx_b, h_b):
    Na, Nb = idx_a.shape[0], idx_b.shape[0]
    tpw_a, tpw_b = _sc_tokens_per_worker(Na), _sc_tokens_per_worker(Nb)
    run_wsum = _sc_peer_phase("wsum", tpw_a)
    run_dot = _sc_peer_phase("dot", tpw_b)

    @functools.partial(
        pl.kernel, mesh=plsc.VectorSubcoreMesh(core_axis_name="c", subcore_axis_name="s"),
        out_type=(jax.ShapeDtypeStruct((Na,) + _SC_OUT["wsum"], jnp.float32),
                  jax.ShapeDtypeStruct((Nb,) + _SC_OUT["dot"], jnp.float32)),
        compiler_params=pltpu.CompilerParams(needs_layout_passes=False),
        scratch_types=_sc_scratch(["wsum", "dot"]),
    )
    def k(tv_hbm, ia_hbm, wa_hbm, tu_hbm, ib_hbm, hb_hbm, outa_hbm, outb_hbm,
          idx_v, rows_v, sem_r, sem_i, sem_o, w_v, outa_v, h_v, outb_v):
        worker = lax.axis_index("s") * SC_CORES + lax.axis_index("c")
        run_wsum(worker * tpw_a, tv_hbm, ia_hbm, wa_hbm, outa_hbm, idx_v, w_v, rows_v, outa_v, sem_r, sem_i, sem_o)
        run_dot(worker * tpw_b, tu_hbm, ib_hbm, hb_hbm, outb_hbm, idx_v, h_v, rows_v, outb_v, sem_r, sem_i, sem_o)

    shp = (PEER_GATHERS, PEER_GATHER_ROWS)
    return k(table_v, idx_a.reshape((Na,) + shp), w_a, table_u, idx_b.reshape((Nb,) + shp), h_b)


PEER_TOKENS = 256
INT_BIG = 2 ** 30


def _extract_topk(cand_ref, ids_ref, val_out_ref, id_out_ref, row0):
    def body(r, carry):
        c = cand_ref[...]
        ids = ids_ref[...]
        m = jnp.max(c, axis=0, keepdims=True)
        sel = jnp.min(jnp.where(c == m, ids, INT_BIG), axis=0, keepdims=True)
        cand_ref[...] = jnp.where(ids == sel, -jnp.inf, c)
        val_out_ref[pl.ds(row0 + r, 1), :] = m
        id_out_ref[pl.ds(row0 + r, 1), :] = sel
        return carry
    lax.fori_loop(0, PEER_TOPK, body, 0)


def _peer_retrieve_kernel(x_ref, gain_ref, scale_ref, shift_ref, wq_ref, keys_ref,
                          h_ref, idx_out_ref, gate_out_ref,
                          s_ref, ids1_ref, sv_ref, si_ref, cand_ref, cid_ref, ts_ref, idx_ref, gate_ref):
    x = x_ref[0]
    y = x * lax.rsqrt(jnp.mean(x * x, axis=-1, keepdims=True) + NORM_EPS)
    h = (y * gain_ref[...]) * (1.0 + scale_ref[0]) + shift_ref[0]
    h_ref[0] = h
    q = jnp.dot(h.astype(jnp.bfloat16), wq_ref[...], preferred_element_type=jnp.float32)
    T = PEER_TOKENS
    K = PEER_TOPK
    ids1_ref[...] = lax.broadcasted_iota(jnp.int32, (PEER_N_KEYS, T), 0)
    for hd in range(PEER_HEADS):
        for p in range(2):
            hp = hd * 2 + p
            qs = q[:, hp * PEER_HALF:(hp + 1) * PEER_HALF].astype(jnp.bfloat16)
            s_ref[...] = lax.dot_general(keys_ref[hp], qs, (((1,), (1,)), ((), ())),
                                         preferred_element_type=jnp.float32)
            _extract_topk(s_ref, ids1_ref, sv_ref, si_ref, p * K)
        for i in range(K):
            cand_ref[i * K:(i + 1) * K, :] = sv_ref[i:i + 1, :] + sv_ref[K:2 * K, :]
            cid_ref[i * K:(i + 1) * K, :] = si_ref[i:i + 1, :] * PEER_N_KEYS + si_ref[K:2 * K, :]
        _extract_topk(cand_ref, cid_ref, ts_ref, idx_ref, hd * K)
        ts = ts_ref[hd * K:(hd + 1) * K, :]
        e = jnp.exp(ts - jnp.max(ts, axis=0, keepdims=True))
        gate_ref[hd * K:(hd + 1) * K, :] = e / jnp.sum(e, axis=0, keepdims=True)
    idx_out_ref[...] = idx_ref[...].T
    gate_out_ref[...] = gate_ref[...].T


def peer_retrieve(x, gain, scale, shift, w_q, sub_keys):
    B, L, D = x.shape
    T = PEER_TOKENS
    nt = L // T
    keys = sub_keys.reshape(PEER_HEADS * 2, PEER_N_KEYS, PEER_HALF).astype(jnp.bfloat16)
    return pl.pallas_call(
        _peer_retrieve_kernel,
        grid=(B, nt),
        in_specs=[
            pl.BlockSpec((1, T, D), lambda b, i: (b, i, 0)),
            pl.BlockSpec((1, D), lambda b, i: (0, 0)),
            pl.BlockSpec((1, 1, D), lambda b, i: (b, 0, 0)),
            pl.BlockSpec((1, 1, D), lambda b, i: (b, 0, 0)),
            pl.BlockSpec((D, PEER_HEADS * 2 * PEER_HALF), lambda b, i: (0, 0)),
            pl.BlockSpec((PEER_HEADS * 2, PEER_N_KEYS, PEER_HALF), lambda b, i: (0, 0, 0)),
        ],
        out_specs=[
            pl.BlockSpec((1, T, D), lambda b, i: (b, i, 0)),
            pl.BlockSpec((T, PEER_SLOTS), lambda b, i: (b * nt + i, 0)),
            pl.BlockSpec((T, PEER_SLOTS), lambda b, i: (b * nt + i, 0)),
        ],
        out_shape=[
            jax.ShapeDtypeStruct((B, L, D), jnp.float32),
            jax.ShapeDtypeStruct((B * L, PEER_SLOTS), jnp.int32),
            jax.ShapeDtypeStruct((B * L, PEER_SLOTS), jnp.float32),
        ],
        scratch_shapes=[
            pltpu.VMEM((PEER_N_KEYS, T), jnp.float32),
            pltpu.VMEM((PEER_N_KEYS, T), jnp.int32),
            pltpu.VMEM((2 * PEER_TOPK, T), jnp.float32),
            pltpu.VMEM((2 * PEER_TOPK, T), jnp.int32),
            pltpu.VMEM((PEER_TOPK * PEER_TOPK, T), jnp.float32),
            pltpu.VMEM((PEER_TOPK * PEER_TOPK, T), jnp.int32),
            pltpu.VMEM((PEER_SLOTS, T), jnp.float32),
            pltpu.VMEM((PEER_SLOTS, T), jnp.int32),
            pltpu.VMEM((PEER_SLOTS, T), jnp.float32),
        ],
        compiler_params=pltpu.CompilerParams(dimension_semantics=("parallel", "parallel"),
                                             vmem_limit_bytes=VMEM_LIMIT_BYTES),
        name="peer_retrieve",
    )(x, gain.reshape(1, D), scale, shift, w_q.astype(jnp.bfloat16), keys)


PEER_ACT_ROWS = 256


def _peer_act_kernel(dots_ref, gate_ref, w_ref):
    a = dots_ref[...]
    w_ref[...] = gate_ref[...] * (0.5 * a * (1.0 + lax.erf(a * (2.0 ** -0.5))))


def peer_act(dots, gate):
    N = dots.shape[0]
    T = min(PEER_ACT_ROWS, N)
    spec = pl.BlockSpec((T, PEER_SLOTS), lambda i: (i, 0))
    return pl.pallas_call(
        _peer_act_kernel,
        grid=(N // T,),
        in_specs=[spec, spec],
        out_specs=spec,
        out_shape=jax.ShapeDtypeStruct((N, PEER_SLOTS), jnp.float32),
        compiler_params=pltpu.CompilerParams(dimension_semantics=("parallel",)),
        name="peer_act",
    )(dots, gate)


def _mix_and_retrieve(li, x, c, ctx, c_ctx, mod_w, mod_b, mix_norm, w_in, w_out, rw_conv, rw_decay_up, rw_decay0, rw_a_up, rw_a0, rw_gate_up, rw_k_k, rw_k_a, rw_r_k, rw_gn_g, rw_gn_b, mla_q_norm, mla_w_uq, mla_kv_norm, mla_w_ukv, mla_q_gain, mla_k_gain, hy_conv, hy_w1, hy_b1, hy_freq1, hy_w2, hy_b2, hy_freq2, hy_w3, hy_b3, hy_bias, ffn_norm, peer_wq, peer_keys, peer_u, peer_v):
    B, L, D = x.shape
    s_rw, s_mla = RW_PROJ, RW_PROJ + MLA_PROJ
    need_ctx = li < DEPTH - 1
    mod_l = (jax.nn.silu(c) @ mod_w[li] + mod_b[li])[:, None, :]
    mod_c = (jax.nn.silu(c_ctx) @ mod_w[li] + mod_b[li])[None, None, :]
    shm_l, scm_l, gm_l, shf_l, scf_l, gf_l = jnp.split(mod_l, N_MOD, axis=-1)
    shm_c, scm_c, gm_c, shf_c, scf_c, gf_c = jnp.split(mod_c, N_MOD, axis=-1)

    p_l = norm_mod_proj(x, mix_norm[li], scm_l, shm_l, w_in[li], 512)
    p_c = norm_mod_proj(ctx, mix_norm[li], jnp.broadcast_to(scm_c, (B, 1, D)),
                        jnp.broadcast_to(shm_c, (B, 1, D)), w_in[li], 256)
    rw_l, rw_c = rwkv7_mixer(p_l[..., :s_rw], p_c[..., :s_rw], rw_conv[li], rw_decay_up[li], rw_decay0[li],
                             rw_a_up[li], rw_a0[li], rw_gate_up[li], rw_k_k[li], rw_k_a[li], rw_r_k[li],
                             rw_gn_g[li], rw_gn_b[li], need_ctx)
    ml_l, ml_c = mla_mixer(p_l[..., s_rw:s_mla], p_c[..., s_rw:s_mla], mla_q_norm[li], mla_w_uq[li],
                           mla_kv_norm[li], mla_w_ukv[li], mla_q_gain[li], mla_k_gain[li], need_ctx)
    hy_prm = (hy_conv[li], hy_w1[li], hy_b1[li], hy_freq1[li], hy_w2[li], hy_b2[li], hy_freq2[li],
              hy_w3[li], hy_b3[li], hy_bias[li])
    hy_l = hyena_mixer(p_l[..., s_mla:], *hy_prm)
    x = x + gm_l * (jnp.concatenate([rw_l, ml_l, hy_l], axis=-1) @ w_out[li])
    h, e_idx, gate = peer_retrieve(x, ffn_norm[li], scf_l, shf_l, peer_wq[li], peer_keys[li])
    h = h.reshape(B * L, D)
    if need_ctx:
        hy_c = hyena_mixer(p_c[..., s_mla:], *hy_prm)
        ctx = ctx + gm_c * (jnp.concatenate([rw_c, ml_c, hy_c], axis=-1) @ w_out[li])
        h_c, idx_c, gate_c = peer_retrieve(ctx, ffn_norm[li], jnp.broadcast_to(scf_c, (B, 1, D)),
                                           jnp.broadcast_to(shf_c, (B, 1, D)), peer_wq[li], peer_keys[li])
        h = jnp.concatenate([h, h_c.reshape(-1, D)], axis=0)
        e_idx = jnp.concatenate([e_idx, idx_c], axis=0)
        gate = jnp.concatenate([gate, gate_c], axis=0)
    return x, ctx, gf_l, (gf_c if need_ctx else None), h, e_idx, gate


def _peer_residual(xm, gf, ctx, gf_c, out):
    n = xm.shape[0] * xm.shape[1]
    x = xm + gf * out[:n].reshape(xm.shape)
    if gf_c is not None:
        ctx = ctx + gf_c * out[n:].reshape(ctx.shape)
    return x, ctx


BATCH_GROUP_ROWS = (2, 2, 2, 2)


def kernel(x, c, ctx, c_ctx, mod_w, mod_b, mix_norm, w_in, w_out, rw_conv, rw_decay_up, rw_decay0, rw_a_up, rw_a0, rw_gate_up, rw_k_k, rw_k_a, rw_r_k, rw_gn_g, rw_gn_b, mla_q_norm, mla_w_uq, mla_kv_norm, mla_w_ukv, mla_q_gain, mla_k_gain, hy_conv, hy_w1, hy_b1, hy_freq1, hy_w2, hy_b2, hy_freq2, hy_w3, hy_b3, hy_bias, ffn_norm, peer_wq, peer_keys, peer_u, peer_v):
    params = (mod_w, mod_b, mix_norm, w_in, w_out, rw_conv, rw_decay_up, rw_decay0, rw_a_up, rw_a0, rw_gate_up,
              rw_k_k, rw_k_a, rw_r_k, rw_gn_g, rw_gn_b, mla_q_norm, mla_w_uq, mla_kv_norm, mla_w_ukv, mla_q_gain,
              mla_k_gain, hy_conv, hy_w1, hy_b1, hy_freq1, hy_w2, hy_b2, hy_freq2, hy_w3, hy_b3, hy_bias,
              ffn_norm, peer_wq, peer_keys)
    peer_u = [pack_expert_table(peer_u[li]) for li in range(DEPTH)]
    peer_v = [pack_expert_table(peer_v[li]) for li in range(DEPTH)]
    params = params + (peer_u, peer_v)
    assert sum(BATCH_GROUP_ROWS) == x.shape[0]
    G = len(BATCH_GROUP_ROWS)
    lo = [sum(BATCH_GROUP_ROWS[:g]) for g in range(G + 1)]
    L, D = x.shape[1:]
    xs = [x[lo[g]:lo[g + 1]] for g in range(G)]
    cs = [c[lo[g]:lo[g + 1]] for g in range(G)]
    ctxs = [ctx[lo[g]:lo[g + 1]] for g in range(G)]
    stages = [(li, g) for li in range(DEPTH) for g in range(G)]
    prev = None
    token = None
    for li, g in stages:
        ins = (xs[g], ctxs[g])
        if token is not None:
            token, ins = lax.optimization_barrier((token, ins))
        xm, cm, gf, gf_c, h, e_idx, gate = _mix_and_retrieve(li, ins[0], cs[g], ins[1], c_ctx, *params)
        token = gate
        if prev is None:
            dots = _sc_peer(peer_u[li], e_idx, h, "dot")
        else:
            pl_, pg, pxm, pgf, pcm, pgf_c, pidx, pgate, pdots = prev
            token, (pdots, pgate) = lax.optimization_barrier((token, (pdots, pgate)))
            w = peer_act(pdots, pgate)
            token = w
            out, dots = _sc_peer_pair(peer_v[pl_], pidx, w, peer_u[li], e_idx, h)
            xs[pg], ctxs[pg] = _peer_residual(pxm, pgf, pcm, pgf_c, out)
        prev = (li, g, xm, gf, cm, gf_c, e_idx, gate, dots)
    pl_, pg, pxm, pgf, pcm, pgf_c, pidx, pgate, pdots = prev
    out = _sc_peer(peer_v[pl_], pidx, peer_act(pdots, pgate), "wsum")
    xs[pg], ctxs[pg] = _peer_residual(pxm, pgf, pcm, pgf_c, out)
    return jnp.concatenate(xs, axis=0)
```

```python
import functools
import math

import jax
import jax.numpy as jnp
import numpy as np
from jax import lax
from jax.experimental import pallas as pl
from jax.experimental.pallas import tpu as pltpu
from jax.experimental.pallas import tpu_sc as plsc

D_MODEL = 1024
DEPTH = 2
GRID_W = 64
N_MOD = 6
NORM_EPS = 1e-6

RW_HEADS = 6
RW_HEAD_DIM = 64
RW_WIDTH = RW_HEADS * RW_HEAD_DIM
RW_DECAY_RANK = 64
RW_A_RANK = 64
RW_GATE_RANK = 128
RW_DECAY_SCALE = 0.6065306597
RW_GN_EPS = 64e-5
L2_EPS = 1e-12

MLA_HEADS = 6
MLA_Q_RANK = 256
MLA_KV_RANK = 128
MLA_NOPE_DIM = 64
MLA_ROPE_DIM = 32
MLA_V_DIM = 64
MLA_QK_DIM = MLA_NOPE_DIM + MLA_ROPE_DIM
MLA_WIDTH = MLA_HEADS * MLA_V_DIM
AXIS_ROPE_DIM = MLA_ROPE_DIM // 2
ROPE_THETA = 10000.0

HY_WIDTH = 256
HY_ORDER = 2
HY_POS_BANDS = 16
HY_SHORT_DECAY_PCT = 0.3
HY_LONG_DECAY_PCT = 1.5
HY_DECAY_TARGET = 1e-2

PEER_HEADS = 8
PEER_N_KEYS = 128
PEER_TOPK = 16
PEER_QUERY_DIM = 256
PEER_HALF = PEER_QUERY_DIM // 2

RW_PROJ = 3 * RW_WIDTH + RW_DECAY_RANK + RW_A_RANK + RW_GATE_RANK
MLA_PROJ = MLA_Q_RANK + MLA_KV_RANK + MLA_ROPE_DIM
HY_PROJ = (HY_ORDER + 1) * HY_WIDTH
IN_PROJ = RW_PROJ + MLA_PROJ + HY_PROJ
MIX_WIDTH = RW_WIDTH + MLA_WIDTH + HY_WIDTH

VMEM_LIMIT_BYTES = 48 * 1024 * 1024


def _norm_mod_proj_kernel(x_ref, gain_ref, scale_ref, shift_ref, w_ref, o_ref):
    x = x_ref[0]
    y = x * lax.rsqrt(jnp.mean(x * x, axis=-1, keepdims=True) + NORM_EPS)
    y = y * gain_ref[...]
    y = y * (1.0 + scale_ref[0]) + shift_ref[0]
    o_ref[0] = jnp.dot(y.astype(jnp.bfloat16), w_ref[...], preferred_element_type=jnp.float32)


def norm_mod_proj(x, gain, scale, shift, w, block_rows):
    B, L, D = x.shape
    N = w.shape[1]
    return pl.pallas_call(
        _norm_mod_proj_kernel,
        grid=(B, L // block_rows),
        in_specs=[
            pl.BlockSpec((1, block_rows, D), lambda b, i: (b, i, 0)),
            pl.BlockSpec((1, D), lambda b, i: (0, 0)),
            pl.BlockSpec((1, 1, D), lambda b, i: (b, 0, 0)),
            pl.BlockSpec((1, 1, D), lambda b, i: (b, 0, 0)),
            pl.BlockSpec((D, N), lambda b, i: (0, 0)),
        ],
        out_specs=pl.BlockSpec((1, block_rows, N), lambda b, i: (b, i, 0)),
        out_shape=jax.ShapeDtypeStruct((B, L, N), jnp.float32),
        compiler_params=pltpu.CompilerParams(
            dimension_semantics=("parallel", "parallel"), vmem_limit_bytes=VMEM_LIMIT_BYTES),
        name="norm_mod_proj",
    )(x, gain.reshape(1, D), scale, shift, w.astype(jnp.bfloat16))


RW_CHUNK = 64


def _rwkv_chunk_kernel(r_ref, kk_ref, v_ref, lw_ref, akk_ref, kr_ref, y_ref, h_ref):
    d = pl.program_id(0)
    n = pl.program_id(2)

    @pl.when(n == 0)
    def _():
        h_ref[...] = jnp.zeros_like(h_ref)

    C = RW_CHUNK
    row = lax.broadcasted_iota(jnp.int32, (C, C), 0)
    col = lax.broadcasted_iota(jnp.int32, (C, C), 1)
    lag = (row - col) * (1 - 2 * d)
    before = lag > 0
    upto = lag >= 0
    tri = upto.astype(jnp.float32)
    eye = (row == col).astype(jnp.float32)
    bf = jnp.bfloat16
    f32 = jnp.float32

    def mm(a, b):
        return jnp.dot(a.astype(bf), b.astype(bf), preferred_element_type=f32)

    def mm_nt(a, b):
        return lax.dot_general(a.astype(bf), b.astype(bf), (((1,), (1,)), ((), ())), preferred_element_type=f32)

    def mm_tn(a, b):
        return lax.dot_general(a.astype(bf), b.astype(bf), (((0,), (0,)), ((), ())), preferred_element_type=f32)

    hs = range(RW_HEADS)
    HD = RW_HEAD_DIM
    heads = lambda t: [t[:, h * HD:(h + 1) * HD] for h in hs]
    r = heads(r_ref[0])
    kk = heads(kk_ref[0])
    v = heads(v_ref[0])
    lw = heads(lw_ref[0, 0])
    akk = heads(akk_ref[0, 0])
    kr = heads(kr_ref[0, 0])
    G = [jnp.dot(tri, lw[h], preferred_element_type=f32, precision=lax.Precision.HIGHEST) for h in hs]
    gtot = [jnp.sum(lw[h], axis=0, keepdims=True) for h in hs]
    Einv = [jnp.exp(-G[h]) for h in hs]
    At = [-kk[h] * jnp.exp(G[h] - lw[h]) for h in hs]
    Rt = [r[h] * jnp.exp(G[h]) for h in hs]
    Bt = [akk[h] * Einv[h] for h in hs]
    Kt = [kr[h] * Einv[h] for h in hs]
    X = [mm_nt(jnp.concatenate([At[h], Rt[h]], axis=0), jnp.concatenate([Bt[h], Kt[h]], axis=0)) for h in hs]
    M_ab = [jnp.where(before, X[h][:C, :C], 0.0) for h in hs]
    M_ak = [jnp.where(before, X[h][:C, C:], 0.0) for h in hs]
    A_rb = [jnp.where(upto, X[h][C:, :C], 0.0) for h in hs]
    A_rk = [jnp.where(upto, X[h][C:, C:], 0.0) for h in hs]
    MV = [mm(M_ak[h], v[h]) for h in hs]
    Mp = M_ab
    T = [eye + Mp[h] for h in hs]
    for _ in range(5):
        Mp = [jnp.dot(Mp[h], Mp[h], preferred_element_type=f32) for h in hs]
        T = [T[h] + jnp.dot(T[h], Mp[h], preferred_element_type=f32) for h in hs]
    WU = [jnp.dot(T[h], jnp.concatenate([At[h], MV[h]], axis=1), preferred_element_type=f32) for h in hs]
    H0 = [h_ref[h] for h in hs]
    Ehat = [jnp.exp(gtot[h] - G[h]) for h in hs]
    Om = [Rt[h] + mm(A_rb[h], WU[h][:, :HD]) for h in hs]
    Y0 = [mm(A_rb[h], WU[h][:, HD:]) + mm(A_rk[h], v[h]) for h in hs]
    BW = [mm_tn(akk[h] * Ehat[h], WU[h]) for h in hs]
    KV = [mm_tn(kr[h] * Ehat[h], v[h]) for h in hs]
    y_ref[0, 0] = jnp.concatenate([jnp.dot(Om[h], H0[h], preferred_element_type=f32) + Y0[h] for h in hs], axis=1)
    for h in hs:
        P = eye * jnp.exp(gtot[h]) + BW[h][:, :HD]
        h_ref[h] = jnp.dot(P, H0[h], preferred_element_type=f32) + BW[h][:, HD:] + KV[h]


def rwkv_chunked(r, kk, v, lw, akk, kr, n_ctx):
    B, T, W = r.shape
    H = W // RW_HEAD_DIM
    nc = n_ctx // RW_CHUNK
    nt = T // RW_CHUNK

    def chunk_of(d, n):
        bwd = jnp.where(n < nc, nc - 1 - n, nt - 1 - (n - nc))
        return jnp.where(d == 0, n, bwd)

    spec1 = pl.BlockSpec((1, RW_CHUNK, W), lambda d, b, n: (b, chunk_of(d, n), 0))
    spec2 = pl.BlockSpec((1, 1, RW_CHUNK, W), lambda d, b, n: (d, b, chunk_of(d, n), 0))
    return pl.pallas_call(
        _rwkv_chunk_kernel,
        grid=(2, B, nt),
        in_specs=[spec1, spec1, spec1, spec2, spec2, spec2],
        out_specs=spec2,
        out_shape=jax.ShapeDtypeStruct((2, B, T, W), jnp.float32),
        scratch_shapes=[pltpu.VMEM((H, RW_HEAD_DIM, RW_HEAD_DIM), jnp.float32)],
        compiler_params=pltpu.CompilerParams(dimension_semantics=("parallel", "parallel", "arbitrary")),
        name="rwkv_chunked",
    )(r, kk, v, lw, akk, kr)


def short_conv(x, w):
    xp = jnp.pad(x, ((0, 0), (1, 1), (0, 0)))
    return xp[:, :-2] * w[0] + xp[:, 1:-1] * w[1] + xp[:, 2:] * w[2]


LANE = 128
RW_PREP_ROWS = 256
MLA_PAD_WIDTH = MLA_HEADS * LANE
MLA_PREP_ROWS = 256
ATTN_Q_ROWS = 512


def _split_dot(x, m):
    hi = x.astype(jnp.bfloat16)
    lo = (x - hi.astype(jnp.float32)).astype(jnp.bfloat16)
    return (jnp.dot(hi, m, preferred_element_type=jnp.float32) + jnp.dot(lo, m, preferred_element_type=jnp.float32))


def _rwkv_prep_kernel(z_ref, wda_ref, d0_ref, a0_ref, gup_ref, kk_ref_w, ka_ref, rk_ref, hsum_ref,
                      r_ref, kk_ref, v_ref, lw_ref, akk_ref, kr_ref, g_ref, bonus_ref):
    W = RW_WIDTH
    bf = jnp.bfloat16
    z = z_ref[0]
    r, k, v = z[:, :W], z[:, W:2 * W], z[:, 2 * W:3 * W]
    da = z[:, 3 * W:3 * W + LANE]
    lane = lax.broadcasted_iota(jnp.int32, da.shape, 1)
    da = jnp.where(lane < RW_DECAY_RANK, jnp.tanh(da), da)
    up = jnp.dot(da.astype(bf), wda_ref[...], preferred_element_type=jnp.float32)
    g_lo = z[:, 3 * W + LANE:]
    g_ref[0] = jnp.dot(jax.nn.sigmoid(g_lo).astype(bf), gup_ref[...], preferred_element_type=jnp.float32)
    hsum = hsum_ref[...]
    kk = k * kk_ref_w[...]
    kk = kk * lax.rsqrt(_split_dot(kk * kk, hsum) + L2_EPS)
    r_ref[0] = r
    v_ref[0] = v
    kk_ref[0] = kk
    bonus_ref[0] = _split_dot(r * k * rk_ref[...], hsum) * v
    for d in range(2):
        lw_ref[d, 0] = -RW_DECAY_SCALE * jax.nn.sigmoid(d0_ref[d:d + 1, :] + up[:, d * W:(d + 1) * W])
        a = jax.nn.sigmoid(a0_ref[d:d + 1, :] + up[:, (2 + d) * W:(3 + d) * W])
        akk_ref[d, 0] = kk * a
        kr_ref[d, 0] = k * (1.0 + (a - 1.0) * ka_ref[...])


def rwkv_prep(z, decay_up, decay0, a_up, a0, gate_up, k_k, k_a, r_k):
    B, L, _ = z.shape
    W = RW_WIDTH
    T = min(RW_PREP_ROWS, L)
    zero = jnp.zeros((RW_DECAY_RANK, 2 * W), jnp.float32)
    wda = jnp.concatenate([
        jnp.concatenate([decay_up[0], decay_up[1], zero], axis=1),
        jnp.concatenate([zero, a_up[0], a_up[1]], axis=1)], axis=0).astype(jnp.bfloat16)
    head = jnp.arange(W) // RW_HEAD_DIM
    hsum = (head[:, None] == head[None, :]).astype(jnp.bfloat16)
    row = lambda a: a.reshape(1, W)
    const = lambda a: pl.BlockSpec(a.shape, lambda b, i: (0,) * a.ndim)
    tok = pl.BlockSpec((1, T, W), lambda b, i: (b, i, 0))
    tok2 = pl.BlockSpec((2, 1, T, W), lambda b, i: (0, b, i, 0))
    f1 = jax.ShapeDtypeStruct((B, L, W), jnp.float32)
    f2 = jax.ShapeDtypeStruct((2, B, L, W), jnp.float32)
    args = (z, wda, decay0, a0, gate_up.astype(jnp.bfloat16), row(k_k), row(k_a), row(r_k), hsum)
    return pl.pallas_call(
        _rwkv_prep_kernel,
        grid=(B, L // T),
        in_specs=[pl.BlockSpec((1, T, RW_PROJ), lambda b, i: (b, i, 0))] + [const(a) for a in args[1:]],
        out_specs=[tok, tok, tok, tok2, tok2, tok2, tok, tok],
        out_shape=[f1, f1, f1, f2, f2, f2, f1, f1],
        compiler_params=pltpu.CompilerParams(dimension_semantics=("parallel", "parallel"),
                                             vmem_limit_bytes=VMEM_LIMIT_BYTES),
        name="rwkv_prep",
    )(*args)


def _rwkv_readout_kernel(y_ref, g_ref, bonus_ref, gng_ref, gnb_ref, hsum_ref, o_ref):
    y = y_ref[0, 0] + y_ref[1, 0]
    hsum = hsum_ref[...]
    mu = _split_dot(y, hsum) * (1.0 / RW_HEAD_DIM)
    d = y - mu
    var = _split_dot(d * d, hsum) * (1.0 / RW_HEAD_DIM)
    yn = d * lax.rsqrt(var + RW_GN_EPS) * gng_ref[...] + gnb_ref[...]
    o_ref[0] = (yn + bonus_ref[0]) * g_ref[0]


def rwkv_readout(y, g, bonus, gn_g, gn_b, t0):
    B, L, W = g.shape
    T = min(RW_PREP_ROWS, L)
    off = t0 // T
    head = jnp.arange(W) // RW_HEAD_DIM
    hsum = (head[:, None] == head[None, :]).astype(jnp.bfloat16)
    tok = pl.BlockSpec((1, T, W), lambda b, i: (b, i, 0))
    const = lambda a: pl.BlockSpec(a.shape, lambda b, i: (0,) * a.ndim)
    gg, gb = gn_g.reshape(1, W), gn_b.reshape(1, W)
    return pl.pallas_call(
        _rwkv_readout_kernel,
        grid=(B, L // T),
        in_specs=[pl.BlockSpec((2, 1, T, W), lambda b, i: (0, b, i + off, 0)), tok, tok, const(gg), const(gb), const(hsum)],
        out_specs=tok,
        out_shape=jax.ShapeDtypeStruct((B, L, W), jnp.float32),
        compiler_params=pltpu.CompilerParams(dimension_semantics=("parallel", "parallel")),
        name="rwkv_readout",
    )(y, g, bonus, gg, gb, hsum)


def rwkv7_mixer(p_lat, p_ctx, conv_w, decay_up, decay0, a_up, a0, gate_up, k_k, k_a, r_k, gn_g, gn_b, need_ctx):
    prm = (decay_up, decay0, a_up, a0, gate_up, k_k, k_a, r_k)
    lat = rwkv_prep(short_conv(p_lat, conv_w), *prm)
    ctx = rwkv_prep(short_conv(p_ctx, conv_w), *prm)
    n_ctx = p_ctx.shape[1]
    seq = lambda i: jnp.concatenate([ctx[i], lat[i]], axis=-2)
    y = rwkv_chunked(seq(0), seq(1), seq(2), seq(3), seq(4), seq(5), n_ctx)
    out_l = rwkv_readout(y, lat[6], lat[7], gn_g, gn_b, n_ctx)
    out_c = rwkv_readout(y, ctx[6], ctx[7], gn_g, gn_b, 0) if need_ctx else None
    return out_l, out_c


def _rope_tables(L, use_rope):
    lane = np.arange(LANE)
    in_rope = (lane >= MLA_NOPE_DIM) & (lane < MLA_QK_DIM)
    j = lane - MLA_NOPE_DIM
    axis = j // AXIS_ROPE_DIM
    half = AXIS_ROPE_DIM // 2
    f = j % half
    first = (j % AXIS_ROPE_DIM) < half
    inv = ROPE_THETA ** (-jnp.arange(0, AXIS_ROPE_DIM, 2, dtype=jnp.float32) / AXIS_ROPE_DIM)
    t = jnp.arange(L)
    pos = jnp.stack([t // GRID_W, t % GRID_W], axis=-1).astype(jnp.float32)
    ang = pos[:, np.clip(axis, 0, 1)] * inv[np.clip(f, 0, half - 1)][None, :]
    rope_on = jnp.asarray(in_rope)[None, :] & use_rope
    cos = jnp.where(rope_on, jnp.cos(ang), 1.0)
    sin = jnp.where(rope_on, jnp.sin(ang) * jnp.where(jnp.asarray(first), -1.0, 1.0)[None, :], 0.0)
    return jnp.tile(cos, (1, MLA_HEADS)), jnp.tile(sin, (1, MLA_HEADS))


def _mla_prep_kernel(p_ref, qn_ref, wq_ref, kvn_ref, wk_ref, wv_ref, place_ref, qg_ref, kg_ref, hsum_ref, cos_ref, sin_ref,
                     q_ref, k_ref, v_ref):
    bf = jnp.bfloat16
    p = p_ref[0]
    c_q = p[:, :MLA_Q_RANK]
    c_kv = p[:, MLA_Q_RANK:MLA_Q_RANK + MLA_KV_RANK]
    tail = p[:, MLA_Q_RANK + MLA_KV_RANK:]
    cqn = c_q * lax.rsqrt(jnp.mean(c_q * c_q, axis=-1, keepdims=True) + NORM_EPS) * qn_ref[...]
    ckn = c_kv * lax.rsqrt(jnp.mean(c_kv * c_kv, axis=-1, keepdims=True) + NORM_EPS) * kvn_ref[...]
    q = jnp.dot(cqn.astype(bf), wq_ref[...], preferred_element_type=jnp.float32)
    k = jnp.dot(ckn.astype(bf), wk_ref[...], preferred_element_type=jnp.float32) + _split_dot(tail, place_ref[...])
    v_ref[0] = jnp.dot(ckn.astype(bf), wv_ref[...], preferred_element_type=jnp.float32).astype(bf)
    hsum = hsum_ref[...]
    cos, sin = cos_ref[...], sin_ref[...]
    lane = lax.broadcasted_iota(jnp.int32, q.shape, 1)
    first = ((lane - MLA_NOPE_DIM) % AXIS_ROPE_DIM) < (AXIS_ROPE_DIM // 2)
    half = AXIS_ROPE_DIM // 2

    def finish(x, gain):
        x = x * lax.rsqrt(_split_dot(x * x, hsum) * (1.0 / MLA_QK_DIM) + NORM_EPS) * gain
        partner = jnp.where(first, pltpu.roll(x, MLA_PAD_WIDTH - half, 1), pltpu.roll(x, half, 1))
        return x * cos + partner * sin

    q_ref[0] = (finish(q, qg_ref[...]) * (MLA_QK_DIM ** -0.5)).astype(bf)
    k_ref[0] = finish(k, kg_ref[...]).astype(bf)


def mla_prep(p, use_rope, q_norm, w_uq, kv_norm, w_ukv, q_gain, k_gain):
    B, L, _ = p.shape
    T = min(MLA_PREP_ROWS, L)
    H = MLA_HEADS
    pad_cols = lambda w, d: jnp.pad(w.reshape(w.shape[0], H, d), ((0, 0), (0, 0), (0, LANE - d))).reshape(w.shape[0], H * LANE)
    wq = pad_cols(w_uq, MLA_QK_DIM).astype(jnp.bfloat16)
    ukv = w_ukv.reshape(MLA_KV_RANK, H, MLA_NOPE_DIM + MLA_V_DIM)
    wk = pad_cols(ukv[:, :, :MLA_NOPE_DIM].reshape(MLA_KV_RANK, H * MLA_NOPE_DIM), MLA_NOPE_DIM).astype(jnp.bfloat16)
    wv = ukv[:, :, MLA_NOPE_DIM:].reshape(MLA_KV_RANK, H * MLA_V_DIM).astype(jnp.bfloat16)
    lane = np.arange(H * LANE)
    place = jnp.asarray(((lane[None, :] % LANE) - MLA_NOPE_DIM == np.arange(MLA_ROPE_DIM)[:, None]), jnp.bfloat16)
    hsum = jnp.asarray((lane[:, None] // LANE) == (lane[None, :] // LANE), jnp.bfloat16)
    pad_gain = lambda g: jnp.tile(jnp.pad(g, (0, LANE - MLA_QK_DIM)), H).reshape(1, H * LANE)
    cos, sin = _rope_tables(L, use_rope)
    const = lambda a: pl.BlockSpec(a.shape, lambda b, i: (0,) * a.ndim)
    args = (p, q_norm.reshape(1, -1), wq, kv_norm.reshape(1, -1), wk, wv, place, pad_gain(q_gain), pad_gain(k_gain), hsum)
    pos = pl.BlockSpec((T, H * LANE), lambda b, i: (i, 0))
    return pl.pallas_call(
        _mla_prep_kernel,
        grid=(B, L // T),
        in_specs=[pl.BlockSpec((1, T, MLA_PROJ), lambda b, i: (b, i, 0))] + [const(a) for a in args[1:]] + [pos, pos],
        out_specs=[pl.BlockSpec((1, T, H * LANE), lambda b, i: (b, i, 0)), pl.BlockSpec((1, T, H * LANE), lambda b, i: (b, i, 0)),
                   pl.BlockSpec((1, T, MLA_WIDTH), lambda b, i: (b, i, 0))],
        out_shape=[jax.ShapeDtypeStruct((B, L, H * LANE), jnp.bfloat16), jax.ShapeDtypeStruct((B, L, H * LANE), jnp.bfloat16),
                   jax.ShapeDtypeStruct((B, L, MLA_WIDTH), jnp.bfloat16)],
        compiler_params=pltpu.CompilerParams(dimension_semantics=("parallel", "parallel"),
                                             vmem_limit_bytes=VMEM_LIMIT_BYTES),
        name="mla_prep",
    )(*args, cos, sin)


def _attn_kernel(q_ref, k_ref, v_ref, o_ref):
    lane = lax.broadcasted_iota(jnp.int32, (q_ref.shape[1], LANE), 1)
    for pair in range(MLA_HEADS // 2):
        v_pair = v_ref[0, :, pair * LANE:(pair + 1) * LANE]
        outs = []
        for h in (2 * pair, 2 * pair + 1):
            q = q_ref[0, :, h * LANE:(h + 1) * LANE]
            k = k_ref[0, :, h * LANE:(h + 1) * LANE]
            s = lax.dot_general(q, k, (((1,), (1,)), ((), ())), preferred_element_type=jnp.float32)
            e = jnp.exp(s - jnp.max(s, axis=-1, keepdims=True))
            o = jnp.dot(e.astype(jnp.bfloat16), v_pair, preferred_element_type=jnp.float32)
            outs.append(o / jnp.sum(e, axis=-1, keepdims=True))
        o_ref[0, :, pair * LANE:(pair + 1) * LANE] = jnp.where(lane < MLA_V_DIM, outs[0], outs[1])


def attention(q, k, v):
    B, Lq, P = q.shape
    Lk = k.shape[1]
    tq = min(ATTN_Q_ROWS, Lq)
    return pl.pallas_call(
        _attn_kernel,
        grid=(B, Lq // tq),
        in_specs=[pl.BlockSpec((1, tq, P), lambda b, i: (b, i, 0)),
                  pl.BlockSpec((1, Lk, P), lambda b, i: (b, 0, 0)),
                  pl.BlockSpec((1, Lk, MLA_WIDTH), lambda b, i: (b, 0, 0))],
        out_specs=pl.BlockSpec((1, tq, MLA_WIDTH), lambda b, i: (b, i, 0)),
        out_shape=jax.ShapeDtypeStruct((B, Lq, MLA_WIDTH), jnp.float32),
        compiler_params=pltpu.CompilerParams(dimension_semantics=("parallel", "parallel"),
                                             vmem_limit_bytes=VMEM_LIMIT_BYTES),
        name="mla_attention",
    )(q, k, v)


def mla_mixer(p_lat, p_ctx, q_norm, w_uq, kv_norm, w_ukv, q_gain, k_gain, need_ctx):
    prm = (q_norm, w_uq, kv_norm, w_ukv, q_gain, k_gain)
    q_l, k_l, v_l = mla_prep(p_lat, True, *prm)
    q_c, k_c, v_c = mla_prep(p_ctx, False, *prm)
    y_l = attention(q_l, jnp.concatenate([k_l, k_c], axis=1), jnp.concatenate([v_l, v_c], axis=1))
    y_c = attention(q_c, k_c, v_c) if need_ctx else None
    return y_l, y_c


def hyena_filters(L, w1, b1, freq1, w2, b2, freq2, w3, b3):
    tn = jnp.arange(L, dtype=jnp.float32) / L
    bands = jnp.arange(1, HY_POS_BANDS + 1, dtype=jnp.float32)
    ang = 2.0 * math.pi * tn[:, None] * bands[None, :]
    z = jnp.concatenate([tn[:, None], jnp.cos(ang), jnp.sin(ang)], axis=-1)
    h = jnp.sin(freq1 * (z @ w1 + b1))
    h = jnp.sin(freq2 * (h @ w2 + b2))
    h = (h @ w3 + b3).reshape(L, HY_ORDER, 2, HY_WIDTH)
    rates = jnp.abs(jnp.linspace(math.log(HY_DECAY_TARGET) / HY_LONG_DECAY_PCT,
                                 math.log(HY_DECAY_TARGET) / HY_SHORT_DECAY_PCT, HY_WIDTH))
    h = h * jnp.exp(-tn[:, None] * rates[None, :])[:, None, None, :]
    zero = jnp.zeros((1, HY_ORDER, HY_WIDTH), h.dtype)
    h_full = jnp.concatenate([h[:, :, 0], zero, h[:0:-1, :, 1]], axis=0)
    return h_full * lax.rsqrt(jnp.sum(jnp.square(h_full), axis=0, keepdims=True))


def fft_long_conv(u, h_full, bias):
    L = u.shape[1]
    uf = jnp.fft.rfft(u, n=2 * L, axis=1)
    hf = jnp.fft.rfft(h_full, n=2 * L, axis=0)
    y = jnp.fft.irfft(uf * hf[None], n=2 * L, axis=1)[:, :L]
    return y + u * bias


FFT_N1 = 64
FFT_N2 = 128
FFT_N = FFT_N1 * FFT_N2
HY_SEQS = 32


def _dft_tables(seqs):
    n1 = np.arange(FFT_N1)
    n2 = np.arange(FFT_N2)
    f64 = np.exp(-2j * np.pi * np.outer(n1, n1) / FFT_N1)
    f128 = np.exp(-2j * np.pi * np.outer(n2, n2) / FFT_N2)
    tw = np.exp(-2j * np.pi * np.outer(n1, n2) / FFT_N)
    half = FFT_N1 // 2
    fh = f64[:, :half]
    m1 = np.block([[fh.real, -fh.imag], [fh.imag, fh.real]])
    m1f = np.concatenate([f64.real, f64.imag], axis=0)
    m2 = np.block([[f128.real, f128.imag], [-f128.imag, f128.real]])
    m3 = np.block([[f128.real, -f128.imag], [f128.imag, f128.real]]) / FFT_N
    c = np.conj(f64)[:half, :]
    m4 = np.block([[c.real, -c.imag], [c.imag, c.real]])
    bf = lambda a: jnp.asarray(a, jnp.float32).astype(jnp.bfloat16)
    f32 = lambda a: jnp.asarray(a, jnp.float32)
    return dict(m1=bf(m1), m1f=bf(m1f), m2=bf(m2), m3=bf(m3), m4=bf(m4),
                twr_l=f32(np.tile(tw.real, (1, seqs))), twi_l=f32(np.tile(tw.imag, (1, seqs))),
                twr_s=f32(np.tile(tw.real, (seqs, 1))), twi_s=f32(np.tile(tw.imag, (seqs, 1))))


def _spectrum(cols, m1, twr_l, twi_l, m2, R):
    a = jnp.dot(m1, cols.astype(jnp.bfloat16), preferred_element_type=jnp.float32)
    ar, ai = a[:FFT_N1], a[FFT_N1:]
    pr = ar * twr_l - ai * twi_l
    pi = ar * twi_l + ai * twr_l
    lhs = jnp.concatenate(
        [jnp.concatenate([pr[:, r * FFT_N2:(r + 1) * FFT_N2], pi[:, r * FFT_N2:(r + 1) * FFT_N2]], axis=1)
         for r in range(R)], axis=0)
    return jnp.dot(lhs.astype(jnp.bfloat16), m2, preferred_element_type=jnp.float32)


def _filter_fft_kernel(h_ref, m1f_ref, twr_ref, twi_ref, m2_ref, o_ref):
    R = HY_SEQS
    cols = jnp.concatenate([h_ref[r] for r in range(R)], axis=1)
    x = _spectrum(cols, m1f_ref[...], twr_ref[...], twi_ref[...], m2_ref[...], R)
    o_ref[...] = x.reshape(R, FFT_N1, 2 * FFT_N2)


def _hyena_conv_kernel(y_ref, g_ref, hf_ref, bias_ref, m1_ref, twr_l_ref, twi_l_ref, m2_ref, m3_ref,
                       twr_s_ref, twi_s_ref, m4_ref, o_ref):
    R = HY_SEQS
    half = FFT_N1 // 2
    y = [y_ref[0], y_ref[1]]
    for o in range(HY_ORDER):
        top = jnp.concatenate([y[0][r] for r in range(R)], axis=1)
        bot = jnp.concatenate([y[1][r] for r in range(R)], axis=1)
        x = _spectrum(jnp.concatenate([top, bot], axis=0), m1_ref[...], twr_l_ref[...], twi_l_ref[...], m2_ref[...], R)
        hf = hf_ref[o].reshape(R * FFT_N1, 2 * FFT_N2)
        xr, xi = x[:, :FFT_N2], x[:, FFT_N2:]
        hr, hi = hf[:, :FFT_N2], hf[:, FFT_N2:]
        yc = jnp.concatenate([xr * hr - xi * hi, xr * hi + xi * hr], axis=1)
        b = jnp.dot(yc.astype(jnp.bfloat16), m3_ref[...], preferred_element_type=jnp.float32)
        br, bi = b[:, :FFT_N2], b[:, FFT_N2:]
        qr = br * twr_s_ref[...] + bi * twi_s_ref[...]
        qi = bi * twr_s_ref[...] - br * twi_s_ref[...]
        bc = jnp.concatenate(
            [jnp.concatenate([qr[r * FFT_N1:(r + 1) * FFT_N1], qi[r * FFT_N1:(r + 1) * FFT_N1]], axis=0)
             for r in range(R)], axis=1)
        yo = jnp.dot(m4_ref[...], bc.astype(jnp.bfloat16), preferred_element_type=jnp.float32)
        for p in range(2):
            conv = jnp.stack([yo[p * half:(p + 1) * half, r * FFT_N2:(r + 1) * FFT_N2] for r in range(R)], axis=0)
            y[p] = g_ref[o, p] * (conv + y[p] * bias_ref[o])
    o_ref[0] = y[0]
    o_ref[1] = y[1]


def hyena_long_conv(y_t, g_t, h_t, bias):
    B, C, L = y_t.shape
    assert 2 * L == FFT_N and B % 2 == 0 and C % HY_SEQS == 0
    R = HY_SEQS
    half = FFT_N1 // 2
    tb = _dft_tables(R)
    const = lambda a: pl.BlockSpec(a.shape, lambda *_: (0,) * a.ndim)
    hf = pl.pallas_call(
        _filter_fft_kernel,
        grid=(HY_ORDER * C // R,),
        in_specs=[pl.BlockSpec((R, FFT_N1, FFT_N2), lambda i: (i, 0, 0)),
                  const(tb['m1f']), const(tb['twr_l']), const(tb['twi_l']), const(tb['m2'])],
        out_specs=pl.BlockSpec((R, FFT_N1, 2 * FFT_N2), lambda i: (i, 0, 0)),
        out_shape=jax.ShapeDtypeStruct((HY_ORDER * C, FFT_N1, 2 * FFT_N2), jnp.float32),
        compiler_params=pltpu.CompilerParams(dimension_semantics=("parallel",), vmem_limit_bytes=VMEM_LIMIT_BYTES),
        name="hyena_filter_fft",
    )(h_t.reshape(HY_ORDER * C, FFT_N1, FFT_N2), tb['m1f'], tb['twr_l'], tb['twi_l'], tb['m2'])
    hf = hf.reshape(HY_ORDER, C, FFT_N1, 2 * FFT_N2)
    out = pl.pallas_call(
        _hyena_conv_kernel,
        grid=(B // 2, C // R),
        in_specs=[pl.BlockSpec((2, R, half, FFT_N2), lambda b, c: (b, c, 0, 0)),
                  pl.BlockSpec((HY_ORDER, 2, R, half, FFT_N2), lambda b, c: (0, b, c, 0, 0)),
                  pl.BlockSpec((HY_ORDER, R, FFT_N1, 2 * FFT_N2), lambda b, c: (0, c, 0, 0)),
                  pl.BlockSpec((HY_ORDER, R, 1, 1), lambda b, c: (0, c, 0, 0)),
                  const(tb['m1']), const(tb['twr_l']), const(tb['twi_l']), const(tb['m2']), const(tb['m3']),
                  const(tb['twr_s']), const(tb['twi_s']), const(tb['m4'])],
        out_specs=pl.BlockSpec((2, R, half, FFT_N2), lambda b, c: (b, c, 0, 0)),
        out_shape=jax.ShapeDtypeStruct((B, C, half, FFT_N2), jnp.float32),
        compiler_params=pltpu.CompilerParams(dimension_semantics=("parallel", "parallel"),
                                             vmem_limit_bytes=VMEM_LIMIT_BYTES),
        name="hyena_conv",
    )(y_t.reshape(B, C, half, FFT_N2), g_t.reshape(HY_ORDER, B, C, half, FFT_N2), hf,
      bias.reshape(HY_ORDER, C, 1, 1), tb['m1'], tb['twr_l'], tb['twi_l'], tb['m2'], tb['m3'],
      tb['twr_s'], tb['twi_s'], tb['m4'])
    return out.reshape(B, C, L)


def hyena_mixer(p, conv_w, w1, b1, freq1, w2, b2, freq2, w3, b3, bias):
    B, L = p.shape[:2]
    z = short_conv(p, conv_w)
    h_full = hyena_filters(L, w1, b1, freq1, w2, b2, freq2, w3, b3)
    if 2 * L == FFT_N:
        g_t = jnp.transpose(z[..., :HY_ORDER * HY_WIDTH].reshape(B, L, HY_ORDER, HY_WIDTH), (2, 0, 3, 1))
        y_t = jnp.swapaxes(z[..., HY_ORDER * HY_WIDTH:], 1, 2)
        y_t = hyena_long_conv(y_t, g_t, jnp.transpose(h_full, (1, 2, 0)), bias)
        return jnp.swapaxes(y_t, 1, 2)
    gates = (z[..., :HY_WIDTH], z[..., HY_WIDTH:2 * HY_WIDTH])
    y = z[..., 2 * HY_WIDTH:]
    for o in range(HY_ORDER):
        y = gates[o] * fft_long_conv(y, h_full[:, o], bias[o])
    return y


SC_CORES = 2
SC_SUBCORES = 16
SC_LANES = 16
SC_WORKERS = SC_CORES * SC_SUBCORES
PEER_SLOTS = PEER_HEADS * PEER_TOPK
PEER_GATHER_ROWS = 32
PEER_GATHERS = PEER_SLOTS // PEER_GATHER_ROWS
PEER_ACC_VREGS = 8
PEER_ROW_BUFFERS = 4
PEER_ROW_WORDS = D_MODEL // 2
HI_MASK = -65536


def pack_expert_table(t):
    b = lax.bitcast_convert_type(t.astype(jnp.bfloat16), jnp.uint16).astype(jnp.uint32)
    return lax.bitcast_convert_type(b[:, :PEER_ROW_WORDS] | (b[:, PEER_ROW_WORDS:] << 16), jnp.int32)


def _sc_peer_phase(phase, tpw):
    NBUF = PEER_ROW_BUFFERS
    AHEAD = NBUF - 1
    HW = PEER_ROW_WORDS

    def run(base, table_hbm, idx_hbm, aux_hbm, out_hbm, idx_v, aux_v, rows_v, out_v, sem_r, sem_i, sem_o):

        def gather(p, c, b):
            return pltpu.make_async_copy(table_hbm.at[idx_v.at[p, c]], rows_v.at[b], sem_r.at[b])

        def load_meta(t, p):
            return (pltpu.make_async_copy(idx_hbm.at[t], idx_v.at[p], sem_i.at[p]),
                    pltpu.make_async_copy(aux_hbm.at[t], aux_v.at[p], sem_i.at[p]))

        def store_out(t, p):
            return pltpu.make_async_copy(out_v.at[p], out_hbm.at[t], sem_o.at[p])

        def halves(word):
            return (plsc.bitcast(lax.shift_left(word, 16), jnp.float32), plsc.bitcast(word & HI_MASK, jnp.float32))

        def compute(p, c, b):
            if phase == "dot":
                lane = lax.iota(jnp.int32, SC_LANES)
                vec = jnp.zeros((SC_LANES,), jnp.float32)
                groups_per_vec = SC_LANES // PEER_ACC_VREGS
                for g in range(PEER_GATHER_ROWS // PEER_ACC_VREGS):
                    def body(cc, accs):
                        x_lo = aux_v[p, pl.ds(cc * SC_LANES, SC_LANES)]
                        x_hi = aux_v[p, pl.ds(HW + cc * SC_LANES, SC_LANES)]
                        out = []
                        for r in range(PEER_ACC_VREGS):
                            lo, hi = halves(rows_v[b, g * PEER_ACC_VREGS + r, pl.ds(cc * SC_LANES, SC_LANES)])
                            out.append(accs[r] + lo * x_lo + hi * x_hi)
                        return tuple(out)
                    accs = lax.fori_loop(0, HW // SC_LANES, body,
                                         tuple(jnp.zeros((SC_LANES,), jnp.float32) for _ in range(PEER_ACC_VREGS)))
                    for r in range(PEER_ACC_VREGS):
                        vec = jnp.where(lane == (g % groups_per_vec) * PEER_ACC_VREGS + r, jnp.sum(accs[r]), vec)
                    if g % groups_per_vec == groups_per_vec - 1:
                        out_v[p, pl.ds(c * PEER_GATHER_ROWS + (g // groups_per_vec) * SC_LANES, SC_LANES)] = vec
            else:
                words = PEER_ACC_VREGS // 2
                for db in range(HW // (words * SC_LANES)):
                    def body(kk, accs):
                        wv = plsc.load_gather(aux_v.at[p], [jnp.full((SC_LANES,), c * PEER_GATHER_ROWS + kk, jnp.int32)])
                        out = []
                        for j in range(words):
                            lo, hi = halves(rows_v[b, kk, pl.ds((db * words + j) * SC_LANES, SC_LANES)])
                            out += [accs[2 * j] + lo * wv, accs[2 * j + 1] + hi * wv]
                        return tuple(out)
                    if c == 0:
                        init = tuple(jnp.zeros((SC_LANES,), jnp.float32) for _ in range(2 * words))
                    else:
                        init = tuple(out_v[p, pl.ds(half * HW + (db * words + j) * SC_LANES, SC_LANES)]
                                     for j in range(words) for half in range(2))
                    accs = lax.fori_loop(0, PEER_GATHER_ROWS, body, init)
                    for j in range(words):
                        out_v[p, pl.ds((db * words + j) * SC_LANES, SC_LANES)] = accs[2 * j]
                        out_v[p, pl.ds(HW + (db * words + j) * SC_LANES, SC_LANES)] = accs[2 * j + 1]

        for d in load_meta(base, 0):
            d.start()
        for d in load_meta(base, 0):
            d.wait()
        for c in range(AHEAD):
            gather(0, c, c % NBUF).start()

        @pl.loop(0, tpw // 2)
        def _(i2):
            for p in range(2):
                i = i2 * 2 + p
                t = base + i
                nxt = base + jnp.minimum(i + 1, tpw - 1)
                for d in load_meta(nxt, 1 - p):
                    d.start()

                @pl.when(i2 > 0)
                def _():
                    store_out(t, p).wait()

                for c in range(PEER_GATHERS):
                    ahead = c + AHEAD
                    if ahead < PEER_GATHERS:
                        gather(p, ahead, ahead % NBUF).start()
                    else:
                        if ahead == PEER_GATHERS:
                            for d in load_meta(nxt, 1 - p):
                                d.wait()
                        gather(1 - p, ahead - PEER_GATHERS, ahead % NBUF).start()
                    gather(p, c, c % NBUF).wait()
                    compute(p, c, c % NBUF)
                store_out(t, p).start()

        for c in range(AHEAD):
            gather(0, c, c % NBUF).wait()
        for p in range(2):
            store_out(base, p).wait()

    return run


def _sc_tokens_per_worker(N):
    assert N % (2 * SC_WORKERS) == 0 and PEER_GATHERS % PEER_ROW_BUFFERS == 0
    return N // SC_WORKERS


_SC_AUX = {"dot": (D_MODEL,), "wsum": (PEER_SLOTS,)}
_SC_OUT = {"dot": (PEER_SLOTS,), "wsum": (D_MODEL,)}


def _sc_scratch(phases):
    s = [pltpu.VMEM((2, PEER_GATHERS, PEER_GATHER_ROWS), jnp.int32),
         pltpu.VMEM((PEER_ROW_BUFFERS, PEER_GATHER_ROWS, PEER_ROW_WORDS), jnp.int32),
         pltpu.SemaphoreType.DMA((PEER_ROW_BUFFERS,)), pltpu.SemaphoreType.DMA((2,)), pltpu.SemaphoreType.DMA((2,))]
    for ph in phases:
        s += [pltpu.VMEM((2,) + _SC_AUX[ph], jnp.float32), pltpu.VMEM((2,) + _SC_OUT[ph], jnp.float32)]
    return s


def _sc_peer(table, idx, aux, phase):
    N = idx.shape[0]
    tpw = _sc_tokens_per_worker(N)
    run = _sc_peer_phase(phase, tpw)

    @functools.partial(
        pl.kernel, mesh=plsc.VectorSubcoreMesh(core_axis_name="c", subcore_axis_name="s"),
        out_type=jax.ShapeDtypeStruct((N,) + _SC_OUT[phase], jnp.float32),
        compiler_params=pltpu.CompilerParams(needs_layout_passes=False),
        scratch_types=_sc_scratch([phase]),
    )
    def k(table_hbm, idx_hbm, aux_hbm, out_hbm, idx_v, rows_v, sem_r, sem_i, sem_o, aux_v, out_v):
        base = (lax.axis_index("s") * SC_CORES + lax.axis_index("c")) * tpw
        run(base, table_hbm, idx_hbm, aux_hbm, out_hbm, idx_v, aux_v, rows_v, out_v, sem_r, sem_i, sem_o)

    return k(table, idx.reshape(N, PEER_GATHERS, PEER_GATHER_ROWS), aux)


def _sc_peer_pair(table_v, idx_a, w_a, table_u, idx_b, h_b):
    Na, Nb = idx_a.shape[0], idx_b.shape[0]
    tpw_a, tpw_b = _sc_tokens_per_worker(Na), _sc_tokens_per_worker(Nb)
    run_wsum = _sc_peer_phase("wsum", tpw_a)
    run_dot = _sc_peer_phase("dot", tpw_b)

    @functools.partial(
        pl.kernel, mesh=plsc.VectorSubcoreMesh(core_axis_name="c", subcore_axis_name="s"),
        out_type=(jax.ShapeDtypeStruct((Na,) + _SC_OUT["wsum"], jnp.float32),
                  jax.ShapeDtypeStruct((Nb,) + _SC_OUT["dot"], jnp.float32)),
        compiler_params=pltpu.CompilerParams(needs_layout_passes=False),
        scratch_types=_sc_scratch(["wsum", "dot"]),
    )
    def k(tv_hbm, ia_hbm, wa_hbm, tu_hbm, ib_hbm, hb_hbm, outa_hbm, outb_hbm,
          idx_v, rows_v, sem_r, sem_i, sem_o, w_v, outa_v, h_v, outb_v):
        worker = lax.axis_index("s") * SC_CORES + lax.axis_index("c")
        run_wsum(worker * tpw_a, tv_hbm, ia_hbm, wa_hbm, outa_hbm, idx_v, w_v, rows_v, outa_v, sem_r, sem_i, sem_o)
        run_dot(worker * tpw_b, tu_hbm, ib_hbm, hb_hbm, outb_hbm, idx_v, h_v, rows_v, outb_v, sem_r, sem_i, sem_o)

    shp = (PEER_GATHERS, PEER_GATHER_ROWS)
    return k(table_v, idx_a.reshape((Na,) + shp), w_a, table_u, idx_b.reshape((Nb,) + shp), h_b)


PEER_TOKENS = 256
INT_BIG = 2 ** 30
PEER_CANDIDATES = -(-sum(PEER_TOPK // (i + 1) for i in range(PEER_TOPK)) // 8) * 8


def _extract_topk(cand_ref, ids_ref, val_out_ref, id_out_ref, row0):
    def body(r, carry):
        c = cand_ref[...]
        ids = ids_ref[...]
        m = jnp.max(c, axis=0, keepdims=True)
        sel = jnp.min(jnp.where(c == m, ids, INT_BIG), axis=0, keepdims=True)
        cand_ref[...] = jnp.where(ids == sel, -jnp.inf, c)
        val_out_ref[pl.ds(row0 + r, 1), :] = m
        id_out_ref[pl.ds(row0 + r, 1), :] = sel
        return carry
    lax.fori_loop(0, PEER_TOPK, body, 0)


def _peer_retrieve_kernel(x_ref, gain_ref, scale_ref, shift_ref, wq_ref, keys_ref,
                          h_ref, idx_out_ref, gate_out_ref,
                          s_ref, ids1_ref, sv_ref, si_ref, cand_ref, cid_ref, ts_ref, idx_ref, gate_ref):
    x = x_ref[0]
    y = x * lax.rsqrt(jnp.mean(x * x, axis=-1, keepdims=True) + NORM_EPS)
    h = (y * gain_ref[...]) * (1.0 + scale_ref[0]) + shift_ref[0]
    h_ref[0] = h
    q = jnp.dot(h.astype(jnp.bfloat16), wq_ref[...], preferred_element_type=jnp.float32)
    T = PEER_TOKENS
    K = PEER_TOPK
    ids1_ref[...] = lax.broadcasted_iota(jnp.int32, (PEER_N_KEYS, T), 0)
    for hd in range(PEER_HEADS):
        for p in range(2):
            hp = hd * 2 + p
            qs = q[:, hp * PEER_HALF:(hp + 1) * PEER_HALF].astype(jnp.bfloat16)
            s_ref[...] = lax.dot_general(keys_ref[hp], qs, (((1,), (1,)), ((), ())),
                                         preferred_element_type=jnp.float32)
            _extract_topk(s_ref, ids1_ref, sv_ref, si_ref, p * K)
        cand_ref[...] = jnp.full(cand_ref.shape, -jnp.inf, jnp.float32)
        cid_ref[...] = INT_BIG - 1 - lax.broadcasted_iota(jnp.int32, cid_ref.shape, 0)
        off = 0
        for i in range(K):
            n = K // (i + 1)
            cand_ref[off:off + n, :] = sv_ref[i:i + 1, :] + sv_ref[K:K + n, :]
            cid_ref[off:off + n, :] = si_ref[i:i + 1, :] * PEER_N_KEYS + si_ref[K:K + n, :]
            off += n
        _extract_topk(cand_ref, cid_ref, ts_ref, idx_ref, hd * K)
        ts = ts_ref[hd * K:(hd + 1) * K, :]
        e = jnp.exp(ts - jnp.max(ts, axis=0, keepdims=True))
        gate_ref[hd * K:(hd + 1) * K, :] = e / jnp.sum(e, axis=0, keepdims=True)
    idx_out_ref[...] = idx_ref[...].T
    gate_out_ref[...] = gate_ref[...].T


def peer_retrieve(x, gain, scale, shift, w_q, sub_keys):
    B, L, D = x.shape
    T = PEER_TOKENS
    nt = L // T
    keys = sub_keys.reshape(PEER_HEADS * 2, PEER_N_KEYS, PEER_HALF).astype(jnp.bfloat16)
    return pl.pallas_call(
        _peer_retrieve_kernel,
        grid=(B, nt),
        in_specs=[
            pl.BlockSpec((1, T, D), lambda b, i: (b, i, 0)),
            pl.BlockSpec((1, D), lambda b, i: (0, 0)),
            pl.BlockSpec((1, 1, D), lambda b, i: (b, 0, 0)),
            pl.BlockSpec((1, 1, D), lambda b, i: (b, 0, 0)),
            pl.BlockSpec((D, PEER_HEADS * 2 * PEER_HALF), lambda b, i: (0, 0)),
            pl.BlockSpec((PEER_HEADS * 2, PEER_N_KEYS, PEER_HALF), lambda b, i: (0, 0, 0)),
        ],
        out_specs=[
            pl.BlockSpec((1, T, D), lambda b, i: (b, i, 0)),
            pl.BlockSpec((T, PEER_SLOTS), lambda b, i: (b * nt + i, 0)),
            pl.BlockSpec((T, PEER_SLOTS), lambda b, i: (b * nt + i, 0)),
        ],
        out_shape=[
            jax.ShapeDtypeStruct((B, L, D), jnp.float32),
            jax.ShapeDtypeStruct((B * L, PEER_SLOTS), jnp.int32),
            jax.ShapeDtypeStruct((B * L, PEER_SLOTS), jnp.float32),
        ],
        scratch_shapes=[
            pltpu.VMEM((PEER_N_KEYS, T), jnp.float32),
            pltpu.VMEM((PEER_N_KEYS, T), jnp.int32),
            pltpu.VMEM((2 * PEER_TOPK, T), jnp.float32),
            pltpu.VMEM((2 * PEER_TOPK, T), jnp.int32),
            pltpu.VMEM((PEER_CANDIDATES, T), jnp.float32),
            pltpu.VMEM((PEER_CANDIDATES, T), jnp.int32),
            pltpu.VMEM((PEER_SLOTS, T), jnp.float32),
            pltpu.VMEM((PEER_SLOTS, T), jnp.int32),
            pltpu.VMEM((PEER_SLOTS, T), jnp.float32),
        ],
        compiler_params=pltpu.CompilerParams(dimension_semantics=("parallel", "parallel"),
                                             vmem_limit_bytes=VMEM_LIMIT_BYTES),
        name="peer_retrieve",
    )(x, gain.reshape(1, D), scale, shift, w_q.astype(jnp.bfloat16), keys)


PEER_ACT_ROWS = 256


def _peer_act_kernel(dots_ref, gate_ref, w_ref):
    a = dots_ref[...]
    w_ref[...] = gate_ref[...] * (0.5 * a * (1.0 + lax.erf(a * (2.0 ** -0.5))))


def peer_act(dots, gate):
    N = dots.shape[0]
    T = min(PEER_ACT_ROWS, N)
    spec = pl.BlockSpec((T, PEER_SLOTS), lambda i: (i, 0))
    return pl.pallas_call(
        _peer_act_kernel,
        grid=(N // T,),
        in_specs=[spec, spec],
        out_specs=spec,
        out_shape=jax.ShapeDtypeStruct((N, PEER_SLOTS), jnp.float32),
        compiler_params=pltpu.CompilerParams(dimension_semantics=("parallel",)),
        name="peer_act",
    )(dots, gate)


def peer_ffn(x, gain, scale, shift, w_q, sub_keys, exp_u, exp_v):
    B, L, D = x.shape
    N = B * L
    h, e_idx, gate = peer_retrieve(x, gain, scale, shift, w_q, sub_keys)
    dots = _sc_peer(exp_u, e_idx, h.reshape(N, D), "dot")
    return _sc_peer(exp_v, e_idx, peer_act(dots, gate), "wsum").reshape(B, L, D)


def _mix_and_retrieve(li, x, c, ctx, c_ctx, mod_w, mod_b, mix_norm, w_in, w_out, rw_conv, rw_decay_up, rw_decay0, rw_a_up, rw_a0, rw_gate_up, rw_k_k, rw_k_a, rw_r_k, rw_gn_g, rw_gn_b, mla_q_norm, mla_w_uq, mla_kv_norm, mla_w_ukv, mla_q_gain, mla_k_gain, hy_conv, hy_w1, hy_b1, hy_freq1, hy_w2, hy_b2, hy_freq2, hy_w3, hy_b3, hy_bias, ffn_norm, peer_wq, peer_keys, peer_u, peer_v):
    B, L, D = x.shape
    s_rw, s_mla = RW_PROJ, RW_PROJ + MLA_PROJ
    need_ctx = li < DEPTH - 1
    mod_l = (jax.nn.silu(c) @ mod_w[li] + mod_b[li])[:, None, :]
    mod_c = (jax.nn.silu(c_ctx) @ mod_w[li] + mod_b[li])[None, None, :]
    shm_l, scm_l, gm_l, shf_l, scf_l, gf_l = jnp.split(mod_l, N_MOD, axis=-1)
    shm_c, scm_c, gm_c, shf_c, scf_c, gf_c = jnp.split(mod_c, N_MOD, axis=-1)

    p_l = norm_mod_proj(x, mix_norm[li], scm_l, shm_l, w_in[li], 512)
    p_c = norm_mod_proj(ctx, mix_norm[li], jnp.broadcast_to(scm_c, (B, 1, D)),
                        jnp.broadcast_to(shm_c, (B, 1, D)), w_in[li], 256)
    rw_l, rw_c = rwkv7_mixer(p_l[..., :s_rw], p_c[..., :s_rw], rw_conv[li], rw_decay_up[li], rw_decay0[li],
                             rw_a_up[li], rw_a0[li], rw_gate_up[li], rw_k_k[li], rw_k_a[li], rw_r_k[li],
                             rw_gn_g[li], rw_gn_b[li], need_ctx)
    ml_l, ml_c = mla_mixer(p_l[..., s_rw:s_mla], p_c[..., s_rw:s_mla], mla_q_norm[li], mla_w_uq[li],
                           mla_kv_norm[li], mla_w_ukv[li], mla_q_gain[li], mla_k_gain[li], need_ctx)
    hy_prm = (hy_conv[li], hy_w1[li], hy_b1[li], hy_freq1[li], hy_w2[li], hy_b2[li], hy_freq2[li],
              hy_w3[li], hy_b3[li], hy_bias[li])
    hy_l = hyena_mixer(p_l[..., s_mla:], *hy_prm)
    x = x + gm_l * (jnp.concatenate([rw_l, ml_l, hy_l], axis=-1) @ w_out[li])
    if need_ctx:
        hy_c = hyena_mixer(p_c[..., s_mla:], *hy_prm)
        ctx = ctx + gm_c * (jnp.concatenate([rw_c, ml_c, hy_c], axis=-1) @ w_out[li])
        ctx = ctx + gf_c * peer_ffn(ctx, ffn_norm[li], jnp.broadcast_to(scf_c, (B, 1, D)),
                                    jnp.broadcast_to(shf_c, (B, 1, D)),
                                    peer_wq[li], peer_keys[li], peer_u[li], peer_v[li])
    h, e_idx, gate = peer_retrieve(x, ffn_norm[li], scf_l, shf_l, peer_wq[li], peer_keys[li])
    return x, ctx, gf_l, h.reshape(B * L, D), e_idx, gate


BATCH_GROUP_ROWS = (2, 2, 2, 2)


def kernel(x, c, ctx, c_ctx, mod_w, mod_b, mix_norm, w_in, w_out, rw_conv, rw_decay_up, rw_decay0, rw_a_up, rw_a0, rw_gate_up, rw_k_k, rw_k_a, rw_r_k, rw_gn_g, rw_gn_b, mla_q_norm, mla_w_uq, mla_kv_norm, mla_w_ukv, mla_q_gain, mla_k_gain, hy_conv, hy_w1, hy_b1, hy_freq1, hy_w2, hy_b2, hy_freq2, hy_w3, hy_b3, hy_bias, ffn_norm, peer_wq, peer_keys, peer_u, peer_v):
    params = (mod_w, mod_b, mix_norm, w_in, w_out, rw_conv, rw_decay_up, rw_decay0, rw_a_up, rw_a0, rw_gate_up,
              rw_k_k, rw_k_a, rw_r_k, rw_gn_g, rw_gn_b, mla_q_norm, mla_w_uq, mla_kv_norm, mla_w_ukv, mla_q_gain,
              mla_k_gain, hy_conv, hy_w1, hy_b1, hy_freq1, hy_w2, hy_b2, hy_freq2, hy_w3, hy_b3, hy_bias,
              ffn_norm, peer_wq, peer_keys)
    peer_u = [pack_expert_table(peer_u[li]) for li in range(DEPTH)]
    peer_v = [pack_expert_table(peer_v[li]) for li in range(DEPTH)]
    params = params + (peer_u, peer_v)
    assert sum(BATCH_GROUP_ROWS) == x.shape[0]
    G = len(BATCH_GROUP_ROWS)
    lo = [sum(BATCH_GROUP_ROWS[:g]) for g in range(G + 1)]
    L, D = x.shape[1:]
    xs = [x[lo[g]:lo[g + 1]] for g in range(G)]
    cs = [c[lo[g]:lo[g + 1]] for g in range(G)]
    ctxs = [ctx[lo[g]:lo[g + 1]] for g in range(G)]
    stages = [(li, g) for li in range(DEPTH) for g in range(G)]
    prev = None
    token = None
    for li, g in stages:
        ins = (xs[g], ctxs[g])
        if token is not None:
            token, ins = lax.optimization_barrier((token, ins))
        xm, ctxs[g], gf, h, e_idx, gate = _mix_and_retrieve(li, ins[0], cs[g], ins[1], c_ctx, *params)
        token = gate
        if prev is None:
            dots = _sc_peer(peer_u[li], e_idx, h, "dot")
        else:
            pl_, pg, pxm, pgf, pidx, pgate, pdots = prev
            token, (pdots, pgate) = lax.optimization_barrier((token, (pdots, pgate)))
            w = peer_act(pdots, pgate)
            token = w
            out, dots = _sc_peer_pair(peer_v[pl_], pidx, w, peer_u[li], e_idx, h)
            xs[pg] = pxm + pgf * out.reshape(pxm.shape)
        prev = (li, g, xm, gf, e_idx, gate, dots)
    pl_, pg, pxm, pgf, pidx, pgate, pdots = prev
    xs[pg] = pxm + pgf * _sc_peer(peer_v[pl_], pidx, peer_act(pdots, pgate), "wsum").reshape(pxm.shape)
    return jnp.concatenate(xs, axis=0)
```

```python
import functools
import math

import jax
import jax.numpy as jnp
import numpy as np
from jax import lax
from jax.experimental import pallas as pl
from jax.experimental.pallas import tpu as pltpu
from jax.experimental.pallas import tpu_sc as plsc

D_MODEL = 1024
DEPTH = 2
GRID_W = 64
N_MOD = 6
NORM_EPS = 1e-6
SHORT_CONV = 3

RW_HEADS = 6
RW_HEAD_DIM = 64
RW_WIDTH = RW_HEADS * RW_HEAD_DIM
RW_DECAY_RANK = 64
RW_A_RANK = 64
RW_GATE_RANK = 128
RW_DECAY_SCALE = 0.6065306597
RW_GN_EPS = 64e-5
L2_EPS = 1e-12

MLA_HEADS = 6
MLA_Q_RANK = 256
MLA_KV_RANK = 128
MLA_NOPE_DIM = 64
MLA_ROPE_DIM = 32
MLA_V_DIM = 64
MLA_QK_DIM = MLA_NOPE_DIM + MLA_ROPE_DIM
MLA_WIDTH = MLA_HEADS * MLA_V_DIM
AXIS_ROPE_DIM = MLA_ROPE_DIM // 2
ROPE_THETA = 10000.0

HY_WIDTH = 256
HY_ORDER = 2
HY_POS_BANDS = 16
HY_SHORT_DECAY_PCT = 0.3
HY_LONG_DECAY_PCT = 1.5
HY_DECAY_TARGET = 1e-2

PEER_HEADS = 8
PEER_N_KEYS = 128
PEER_TOPK = 16
PEER_QUERY_DIM = 256
PEER_HALF = PEER_QUERY_DIM // 2

RW_PROJ = 3 * RW_WIDTH + RW_DECAY_RANK + RW_A_RANK + RW_GATE_RANK
MLA_PROJ = MLA_Q_RANK + MLA_KV_RANK + MLA_ROPE_DIM
HY_PROJ = (HY_ORDER + 1) * HY_WIDTH
IN_PROJ = RW_PROJ + MLA_PROJ + HY_PROJ
MIX_WIDTH = RW_WIDTH + MLA_WIDTH + HY_WIDTH

VMEM_LIMIT_BYTES = 48 * 1024 * 1024


def _norm_mod_proj_kernel(x_ref, gain_ref, scale_ref, shift_ref, *refs):
    w_refs, o_refs = refs[:len(refs) // 2], refs[len(refs) // 2:]
    x = x_ref[0]
    y = x * lax.rsqrt(jnp.mean(x * x, axis=-1, keepdims=True) + NORM_EPS)
    y = y * gain_ref[...]
    y = (y * (1.0 + scale_ref[0]) + shift_ref[0]).astype(jnp.bfloat16)
    for w_ref, o_ref in zip(w_refs, o_refs):
        o_ref[0] = jnp.dot(y, w_ref[...], preferred_element_type=jnp.float32)


def norm_mod_proj(x, gain, scale, shift, w, widths, block_rows):
    B, L, D = x.shape
    assert sum(widths) == w.shape[1]
    offs = [sum(widths[:i]) for i in range(len(widths))]
    ws = [w[:, o:o + n].astype(jnp.bfloat16) for o, n in zip(offs, widths)]
    return pl.pallas_call(
        _norm_mod_proj_kernel,
        grid=(B, L // block_rows),
        in_specs=[
            pl.BlockSpec((1, block_rows, D), lambda b, i: (b, i, 0)),
            pl.BlockSpec((1, D), lambda b, i: (0, 0)),
            pl.BlockSpec((1, 1, D), lambda b, i: (b, 0, 0)),
            pl.BlockSpec((1, 1, D), lambda b, i: (b, 0, 0)),
        ] + [pl.BlockSpec((D, n), lambda b, i: (0, 0)) for n in widths],
        out_specs=[pl.BlockSpec((1, block_rows, n), lambda b, i: (b, i, 0)) for n in widths],
        out_shape=[jax.ShapeDtypeStruct((B, L, n), jnp.float32) for n in widths],
        compiler_params=pltpu.CompilerParams(
            dimension_semantics=("parallel", "parallel"), vmem_limit_bytes=VMEM_LIMIT_BYTES),
        name="norm_mod_proj",
    )(x, gain.reshape(1, D), scale, shift, *ws)


CONV_ROWS = 512
SUBLANES = 8


def _short_conv_kernel(x_ref, prev_ref, next_ref, w_ref, o_ref):
    i = pl.program_id(1)
    x = x_ref[0]
    T = x.shape[0]
    row = lax.broadcasted_iota(jnp.int32, x.shape, 0)
    before = jnp.where(i == 0, 0.0, prev_ref[0, SUBLANES - 1:SUBLANES, :])
    after = jnp.where(i == pl.num_programs(1) - 1, 0.0, next_ref[0, 0:1, :])
    up = jnp.where(row == 0, before, pltpu.roll(x, 1, 0))
    down = jnp.where(row == T - 1, after, pltpu.roll(x, T - 1, 0))
    o_ref[0] = up * w_ref[0:1, :] + x * w_ref[1:2, :] + down * w_ref[2:3, :]


def short_conv(x, w):
    B, L, C = x.shape
    T = min(CONV_ROWS, L)
    per = T // SUBLANES
    last = L // SUBLANES - 1
    return pl.pallas_call(
        _short_conv_kernel,
        grid=(B, L // T),
        in_specs=[pl.BlockSpec((1, T, C), lambda b, i: (b, i, 0)),
                  pl.BlockSpec((1, SUBLANES, C), lambda b, i: (b, jnp.maximum(i * per - 1, 0), 0)),
                  pl.BlockSpec((1, SUBLANES, C), lambda b, i: (b, jnp.minimum((i + 1) * per, last), 0)),
                  pl.BlockSpec((SHORT_CONV, C), lambda b, i: (0, 0))],
        out_specs=pl.BlockSpec((1, T, C), lambda b, i: (b, i, 0)),
        out_shape=jax.ShapeDtypeStruct((B, L, C), jnp.float32),
        compiler_params=pltpu.CompilerParams(dimension_semantics=("parallel", "parallel"),
                                             vmem_limit_bytes=VMEM_LIMIT_BYTES),
        name="short_conv",
    )(x, x, x, w)


RW_CHUNK = 64


def _rwkv_chunk_kernel(r_ref, kk_ref, v_ref, lw_ref, akk_ref, kr_ref, y_ref, h_ref):
    d = pl.program_id(0)
    n = pl.program_id(2)

    @pl.when(n == 0)
    def _():
        h_ref[...] = jnp.zeros_like(h_ref)

    C = RW_CHUNK
    row = lax.broadcasted_iota(jnp.int32, (C, C), 0)
    col = lax.broadcasted_iota(jnp.int32, (C, C), 1)
    lag = (row - col) * (1 - 2 * d)
    before = lag > 0
    upto = lag >= 0
    tri = upto.astype(jnp.float32)
    eye = (row == col).astype(jnp.float32)
    bf = jnp.bfloat16
    f32 = jnp.float32

    def mm(a, b):
        return jnp.dot(a.astype(bf), b.astype(bf), preferred_element_type=f32)

    def mm_nt(a, b):
        return lax.dot_general(a.astype(bf), b.astype(bf), (((1,), (1,)), ((), ())), preferred_element_type=f32)

    def mm_tn(a, b):
        return lax.dot_general(a.astype(bf), b.astype(bf), (((0,), (0,)), ((), ())), preferred_element_type=f32)

    hs = range(RW_HEADS)
    HD = RW_HEAD_DIM
    heads = lambda t: [t[:, h * HD:(h + 1) * HD] for h in hs]
    r = heads(r_ref[0])
    kk = heads(kk_ref[0])
    v = heads(v_ref[0])
    lw = heads(lw_ref[0, 0])
    akk = heads(akk_ref[0, 0])
    kr = heads(kr_ref[0, 0])
    G = [jnp.dot(tri, lw[h], preferred_element_type=f32, precision=lax.Precision.HIGHEST) for h in hs]
    gtot = [jnp.sum(lw[h], axis=0, keepdims=True) for h in hs]
    Einv = [jnp.exp(-G[h]) for h in hs]
    At = [-kk[h] * jnp.exp(G[h] - lw[h]) for h in hs]
    Rt = [r[h] * jnp.exp(G[h]) for h in hs]
    Bt = [akk[h] * Einv[h] for h in hs]
    Kt = [kr[h] * Einv[h] for h in hs]
    X = [mm_nt(jnp.concatenate([At[h], Rt[h]], axis=0), jnp.concatenate([Bt[h], Kt[h]], axis=0)) for h in hs]
    M_ab = [jnp.where(before, X[h][:C, :C], 0.0) for h in hs]
    M_ak = [jnp.where(before, X[h][:C, C:], 0.0) for h in hs]
    A_rb = [jnp.where(upto, X[h][C:, :C], 0.0) for h in hs]
    A_rk = [jnp.where(upto, X[h][C:, C:], 0.0) for h in hs]
    MV = [mm(M_ak[h], v[h]) for h in hs]
    Mp = M_ab
    T = [eye + Mp[h] for h in hs]
    for _ in range(5):
        Mp = [jnp.dot(Mp[h], Mp[h], preferred_element_type=f32) for h in hs]
        T = [T[h] + jnp.dot(T[h], Mp[h], preferred_element_type=f32) for h in hs]
    WU = [jnp.dot(T[h], jnp.concatenate([At[h], MV[h]], axis=1), preferred_element_type=f32) for h in hs]
    H0 = [h_ref[h] for h in hs]
    Ehat = [jnp.exp(gtot[h] - G[h]) for h in hs]
    Om = [Rt[h] + mm(A_rb[h], WU[h][:, :HD]) for h in hs]
    Y0 = [mm(A_rb[h], WU[h][:, HD:]) + mm(A_rk[h], v[h]) for h in hs]
    BW = [mm_tn(akk[h] * Ehat[h], WU[h]) for h in hs]
    KV = [mm_tn(kr[h] * Ehat[h], v[h]) for h in hs]
    y_ref[0, 0] = jnp.concatenate([jnp.dot(Om[h], H0[h], preferred_element_type=f32) + Y0[h] for h in hs], axis=1)
    for h in hs:
        P = eye * jnp.exp(gtot[h]) + BW[h][:, :HD]
        h_ref[h] = jnp.dot(P, H0[h], preferred_element_type=f32) + BW[h][:, HD:] + KV[h]


def rwkv_chunked(r, kk, v, lw, akk, kr, n_ctx):
    B, T, W = r.shape
    H = W // RW_HEAD_DIM
    nc = n_ctx // RW_CHUNK
    nt = T // RW_CHUNK

    def chunk_of(d, n):
        bwd = jnp.where(n < nc, nc - 1 - n, nt - 1 - (n - nc))
        return jnp.where(d == 0, n, bwd)

    spec1 = pl.BlockSpec((1, RW_CHUNK, W), lambda d, b, n: (b, chunk_of(d, n), 0))
    spec2 = pl.BlockSpec((1, 1, RW_CHUNK, W), lambda d, b, n: (d, b, chunk_of(d, n), 0))
    return pl.pallas_call(
        _rwkv_chunk_kernel,
        grid=(2, B, nt),
        in_specs=[spec1, spec1, spec1, spec2, spec2, spec2],
        out_specs=spec2,
        out_shape=jax.ShapeDtypeStruct((2, B, T, W), jnp.float32),
        scratch_shapes=[pltpu.VMEM((H, RW_HEAD_DIM, RW_HEAD_DIM), jnp.float32)],
        compiler_params=pltpu.CompilerParams(dimension_semantics=("parallel", "parallel", "arbitrary")),
        name="rwkv_chunked",
    )(r, kk, v, lw, akk, kr)


LANE = 128
RW_PREP_ROWS = 256
MLA_PAD_WIDTH = MLA_HEADS * LANE
MLA_PREP_ROWS = 256
ATTN_Q_ROWS = 512


def _split_dot(x, m):
    hi = x.astype(jnp.bfloat16)
    lo = (x - hi.astype(jnp.float32)).astype(jnp.bfloat16)
    return (jnp.dot(hi, m, preferred_element_type=jnp.float32) + jnp.dot(lo, m, preferred_element_type=jnp.float32))


def _rwkv_prep_kernel(z_ref, wda_ref, d0_ref, a0_ref, gup_ref, kk_ref_w, ka_ref, rk_ref, hsum_ref,
                      r_ref, kk_ref, v_ref, lw_ref, akk_ref, kr_ref, g_ref, bonus_ref):
    W = RW_WIDTH
    bf = jnp.bfloat16
    z = z_ref[0]
    r, k, v = z[:, :W], z[:, W:2 * W], z[:, 2 * W:3 * W]
    da = z[:, 3 * W:3 * W + LANE]
    lane = lax.broadcasted_iota(jnp.int32, da.shape, 1)
    da = jnp.where(lane < RW_DECAY_RANK, jnp.tanh(da), da)
    up = jnp.dot(da.astype(bf), wda_ref[...], preferred_element_type=jnp.float32)
    g_lo = z[:, 3 * W + LANE:]
    g_ref[0] = jnp.dot(jax.nn.sigmoid(g_lo).astype(bf), gup_ref[...], preferred_element_type=jnp.float32)
    hsum = hsum_ref[...]
    kk = k * kk_ref_w[...]
    kk = kk * lax.rsqrt(_split_dot(kk * kk, hsum) + L2_EPS)
    r_ref[0] = r
    v_ref[0] = v
    kk_ref[0] = kk
    bonus_ref[0] = _split_dot(r * k * rk_ref[...], hsum) * v
    for d in range(2):
        lw_ref[d, 0] = -RW_DECAY_SCALE * jax.nn.sigmoid(d0_ref[d:d + 1, :] + up[:, d * W:(d + 1) * W])
        a = jax.nn.sigmoid(a0_ref[d:d + 1, :] + up[:, (2 + d) * W:(3 + d) * W])
        akk_ref[d, 0] = kk * a
        kr_ref[d, 0] = k * (1.0 + (a - 1.0) * ka_ref[...])


def rwkv_prep(z, decay_up, decay0, a_up, a0, gate_up, k_k, k_a, r_k):
    B, L, _ = z.shape
    W = RW_WIDTH
    T = min(RW_PREP_ROWS, L)
    zero = jnp.zeros((RW_DECAY_RANK, 2 * W), jnp.float32)
    wda = jnp.concatenate([
        jnp.concatenate([decay_up[0], decay_up[1], zero], axis=1),
        jnp.concatenate([zero, a_up[0], a_up[1]], axis=1)], axis=0).astype(jnp.bfloat16)
    head = jnp.arange(W) // RW_HEAD_DIM
    hsum = (head[:, None] == head[None, :]).astype(jnp.bfloat16)
    row = lambda a: a.reshape(1, W)
    const = lambda a: pl.BlockSpec(a.shape, lambda b, i: (0,) * a.ndim)
    tok = pl.BlockSpec((1, T, W), lambda b, i: (b, i, 0))
    tok2 = pl.BlockSpec((2, 1, T, W), lambda b, i: (0, b, i, 0))
    f1 = jax.ShapeDtypeStruct((B, L, W), jnp.float32)
    f2 = jax.ShapeDtypeStruct((2, B, L, W), jnp.float32)
    args = (z, wda, decay0, a0, gate_up.astype(jnp.bfloat16), row(k_k), row(k_a), row(r_k), hsum)
    return pl.pallas_call(
        _rwkv_prep_kernel,
        grid=(B, L // T),
        in_specs=[pl.BlockSpec((1, T, RW_PROJ), lambda b, i: (b, i, 0))] + [const(a) for a in args[1:]],
        out_specs=[tok, tok, tok, tok2, tok2, tok2, tok, tok],
        out_shape=[f1, f1, f1, f2, f2, f2, f1, f1],
        compiler_params=pltpu.CompilerParams(dimension_semantics=("parallel", "parallel"),
                                             vmem_limit_bytes=VMEM_LIMIT_BYTES),
        name="rwkv_prep",
    )(*args)


def _rwkv_readout_kernel(y_ref, g_ref, bonus_ref, gng_ref, gnb_ref, hsum_ref, o_ref):
    y = y_ref[0, 0] + y_ref[1, 0]
    hsum = hsum_ref[...]
    mu = _split_dot(y, hsum) * (1.0 / RW_HEAD_DIM)
    d = y - mu
    var = _split_dot(d * d, hsum) * (1.0 / RW_HEAD_DIM)
    yn = d * lax.rsqrt(var + RW_GN_EPS) * gng_ref[...] + gnb_ref[...]
    o_ref[0] = (yn + bonus_ref[0]) * g_ref[0]


def rwkv_readout(y, g, bonus, gn_g, gn_b, t0):
    B, L, W = g.shape
    T = min(RW_PREP_ROWS, L)
    off = t0 // T
    head = jnp.arange(W) // RW_HEAD_DIM
    hsum = (head[:, None] == head[None, :]).astype(jnp.bfloat16)
    tok = pl.BlockSpec((1, T, W), lambda b, i: (b, i, 0))
    const = lambda a: pl.BlockSpec(a.shape, lambda b, i: (0,) * a.ndim)
    gg, gb = gn_g.reshape(1, W), gn_b.reshape(1, W)
    return pl.pallas_call(
        _rwkv_readout_kernel,
        grid=(B, L // T),
        in_specs=[pl.BlockSpec((2, 1, T, W), lambda b, i: (0, b, i + off, 0)), tok, tok, const(gg), const(gb), const(hsum)],
        out_specs=tok,
        out_shape=jax.ShapeDtypeStruct((B, L, W), jnp.float32),
        compiler_params=pltpu.CompilerParams(dimension_semantics=("parallel", "parallel")),
        name="rwkv_readout",
    )(y, g, bonus, gg, gb, hsum)


def rwkv7_mixer(p_lat, p_ctx, conv_w, decay_up, decay0, a_up, a0, gate_up, k_k, k_a, r_k, gn_g, gn_b, need_ctx):
    prm = (decay_up, decay0, a_up, a0, gate_up, k_k, k_a, r_k)
    lat = rwkv_prep(short_conv(p_lat, conv_w), *prm)
    ctx = rwkv_prep(short_conv(p_ctx, conv_w), *prm)
    n_ctx = p_ctx.shape[1]
    seq = lambda i: jnp.concatenate([ctx[i], lat[i]], axis=-2)
    y = rwkv_chunked(seq(0), seq(1), seq(2), seq(3), seq(4), seq(5), n_ctx)
    out_l = rwkv_readout(y, lat[6], lat[7], gn_g, gn_b, n_ctx)
    out_c = rwkv_readout(y, ctx[6], ctx[7], gn_g, gn_b, 0) if need_ctx else None
    return out_l, out_c


def _rope_tables(L, use_rope):
    lane = np.arange(LANE)
    in_rope = (lane >= MLA_NOPE_DIM) & (lane < MLA_QK_DIM)
    j = lane - MLA_NOPE_DIM
    axis = j // AXIS_ROPE_DIM
    half = AXIS_ROPE_DIM // 2
    f = j % half
    first = (j % AXIS_ROPE_DIM) < half
    inv = ROPE_THETA ** (-jnp.arange(0, AXIS_ROPE_DIM, 2, dtype=jnp.float32) / AXIS_ROPE_DIM)
    t = jnp.arange(L)
    pos = jnp.stack([t // GRID_W, t % GRID_W], axis=-1).astype(jnp.float32)
    ang = pos[:, np.clip(axis, 0, 1)] * inv[np.clip(f, 0, half - 1)][None, :]
    rope_on = jnp.asarray(in_rope)[None, :] & use_rope
    cos = jnp.where(rope_on, jnp.cos(ang), 1.0)
    sin = jnp.where(rope_on, jnp.sin(ang) * jnp.where(jnp.asarray(first), -1.0, 1.0)[None, :], 0.0)
    return jnp.tile(cos, (1, MLA_HEADS)), jnp.tile(sin, (1, MLA_HEADS))


def _mla_prep_kernel(p_ref, qn_ref, wq_ref, kvn_ref, wk_ref, wv_ref, place_ref, qg_ref, kg_ref, hsum_ref, cos_ref, sin_ref,
                     q_ref, k_ref, v_ref):
    bf = jnp.bfloat16
    p = p_ref[0]
    c_q = p[:, :MLA_Q_RANK]
    c_kv = p[:, MLA_Q_RANK:MLA_Q_RANK + MLA_KV_RANK]
    tail = p[:, MLA_Q_RANK + MLA_KV_RANK:]
    cqn = c_q * lax.rsqrt(jnp.mean(c_q * c_q, axis=-1, keepdims=True) + NORM_EPS) * qn_ref[...]
    ckn = c_kv * lax.rsqrt(jnp.mean(c_kv * c_kv, axis=-1, keepdims=True) + NORM_EPS) * kvn_ref[...]
    q = jnp.dot(cqn.astype(bf), wq_ref[...], preferred_element_type=jnp.float32)
    k = jnp.dot(ckn.astype(bf), wk_ref[...], preferred_element_type=jnp.float32) + _split_dot(tail, place_ref[...])
    v_ref[0] = jnp.dot(ckn.astype(bf), wv_ref[...], preferred_element_type=jnp.float32).astype(bf)
    hsum = hsum_ref[...]
    cos, sin = cos_ref[...], sin_ref[...]
    lane = lax.broadcasted_iota(jnp.int32, q.shape, 1)
    first = ((lane - MLA_NOPE_DIM) % AXIS_ROPE_DIM) < (AXIS_ROPE_DIM // 2)
    half = AXIS_ROPE_DIM // 2

    def finish(x, gain):
        x = x * lax.rsqrt(_split_dot(x * x, hsum) * (1.0 / MLA_QK_DIM) + NORM_EPS) * gain
        partner = jnp.where(first, pltpu.roll(x, MLA_PAD_WIDTH - half, 1), pltpu.roll(x, half, 1))
        return x * cos + partner * sin

    q_ref[0] = (finish(q, qg_ref[...]) * (MLA_QK_DIM ** -0.5)).astype(bf)
    k_ref[0] = finish(k, kg_ref[...]).astype(bf)


def mla_prep(p, use_rope, q_norm, w_uq, kv_norm, w_ukv, q_gain, k_gain):
    B, L, _ = p.shape
    T = min(MLA_PREP_ROWS, L)
    H = MLA_HEADS
    pad_cols = lambda w, d: jnp.pad(w.reshape(w.shape[0], H, d), ((0, 0), (0, 0), (0, LANE - d))).reshape(w.shape[0], H * LANE)
    wq = pad_cols(w_uq, MLA_QK_DIM).astype(jnp.bfloat16)
    ukv = w_ukv.reshape(MLA_KV_RANK, H, MLA_NOPE_DIM + MLA_V_DIM)
    wk = pad_cols(ukv[:, :, :MLA_NOPE_DIM].reshape(MLA_KV_RANK, H * MLA_NOPE_DIM), MLA_NOPE_DIM).astype(jnp.bfloat16)
    wv = ukv[:, :, MLA_NOPE_DIM:].reshape(MLA_KV_RANK, H * MLA_V_DIM).astype(jnp.bfloat16)
    lane = np.arange(H * LANE)
    place = jnp.asarray(((lane[None, :] % LANE) - MLA_NOPE_DIM == np.arange(MLA_ROPE_DIM)[:, None]), jnp.bfloat16)
    hsum = jnp.asarray((lane[:, None] // LANE) == (lane[None, :] // LANE), jnp.bfloat16)
    pad_gain = lambda g: jnp.tile(jnp.pad(g, (0, LANE - MLA_QK_DIM)), H).reshape(1, H * LANE)
    cos, sin = _rope_tables(L, use_rope)
    const = lambda a: pl.BlockSpec(a.shape, lambda b, i: (0,) * a.ndim)
    args = (p, q_norm.reshape(1, -1), wq, kv_norm.reshape(1, -1), wk, wv, place, pad_gain(q_gain), pad_gain(k_gain), hsum)
    pos = pl.BlockSpec((T, H * LANE), lambda b, i: (i, 0))
    return pl.pallas_call(
        _mla_prep_kernel,
        grid=(B, L // T),
        in_specs=[pl.BlockSpec((1, T, MLA_PROJ), lambda b, i: (b, i, 0))] + [const(a) for a in args[1:]] + [pos, pos],
        out_specs=[pl.BlockSpec((1, T, H * LANE), lambda b, i: (b, i, 0)), pl.BlockSpec((1, T, H * LANE), lambda b, i: (b, i, 0)),
                   pl.BlockSpec((1, T, MLA_WIDTH), lambda b, i: (b, i, 0))],
        out_shape=[jax.ShapeDtypeStruct((B, L, H * LANE), jnp.bfloat16), jax.ShapeDtypeStruct((B, L, H * LANE), jnp.bfloat16),
                   jax.ShapeDtypeStruct((B, L, MLA_WIDTH), jnp.bfloat16)],
        compiler_params=pltpu.CompilerParams(dimension_semantics=("parallel", "parallel"),
                                             vmem_limit_bytes=VMEM_LIMIT_BYTES),
        name="mla_prep",
    )(*args, cos, sin)


def _attn_kernel(q_ref, k_ref, v_ref, o_ref):
    lane = lax.broadcasted_iota(jnp.int32, (q_ref.shape[1], LANE), 1)
    for pair in range(MLA_HEADS // 2):
        v_pair = v_ref[0, :, pair * LANE:(pair + 1) * LANE]
        outs = []
        for h in (2 * pair, 2 * pair + 1):
            q = q_ref[0, :, h * LANE:(h + 1) * LANE]
            k = k_ref[0, :, h * LANE:(h + 1) * LANE]
            s = lax.dot_general(q, k, (((1,), (1,)), ((), ())), preferred_element_type=jnp.float32)
            e = jnp.exp(s - jnp.max(s, axis=-1, keepdims=True))
            o = jnp.dot(e.astype(jnp.bfloat16), v_pair, preferred_element_type=jnp.float32)
            outs.append(o / jnp.sum(e, axis=-1, keepdims=True))
        o_ref[0, :, pair * LANE:(pair + 1) * LANE] = jnp.where(lane < MLA_V_DIM, outs[0], outs[1])


def attention(q, k, v):
    B, Lq, P = q.shape
    Lk = k.shape[1]
    tq = min(ATTN_Q_ROWS, Lq)
    return pl.pallas_call(
        _attn_kernel,
        grid=(B, Lq // tq),
        in_specs=[pl.BlockSpec((1, tq, P), lambda b, i: (b, i, 0)),
                  pl.BlockSpec((1, Lk, P), lambda b, i: (b, 0, 0)),
                  pl.BlockSpec((1, Lk, MLA_WIDTH), lambda b, i: (b, 0, 0))],
        out_specs=pl.BlockSpec((1, tq, MLA_WIDTH), lambda b, i: (b, i, 0)),
        out_shape=jax.ShapeDtypeStruct((B, Lq, MLA_WIDTH), jnp.float32),
        compiler_params=pltpu.CompilerParams(dimension_semantics=("parallel", "parallel"),
                                             vmem_limit_bytes=VMEM_LIMIT_BYTES),
        name="mla_attention",
    )(q, k, v)


def mla_mixer(p_lat, p_ctx, q_norm, w_uq, kv_norm, w_ukv, q_gain, k_gain, need_ctx):
    prm = (q_norm, w_uq, kv_norm, w_ukv, q_gain, k_gain)
    q_l, k_l, v_l = mla_prep(p_lat, True, *prm)
    q_c, k_c, v_c = mla_prep(p_ctx, False, *prm)
    y_l = attention(q_l, jnp.concatenate([k_l, k_c], axis=1), jnp.concatenate([v_l, v_c], axis=1))
    y_c = attention(q_c, k_c, v_c) if need_ctx else None
    return y_l, y_c


HY_FILTER_ROWS = 256


def _hyena_filter_kernel(z_ref, tn_ref, w1_ref, b1_ref, f1_ref, w2_ref, b2_ref, f2_ref, w3_ref, b3_ref, rates_ref, o_ref):
    hp = lax.Precision.HIGHEST
    h = jnp.sin(f1_ref[...] * (jnp.dot(z_ref[...], w1_ref[...], precision=hp, preferred_element_type=jnp.float32) + b1_ref[...]))
    h = jnp.sin(f2_ref[...] * (jnp.dot(h, w2_ref[...], precision=hp, preferred_element_type=jnp.float32) + b2_ref[...]))
    h = jnp.dot(h, w3_ref[...], precision=hp, preferred_element_type=jnp.float32) + b3_ref[...]
    o_ref[...] = h * jnp.exp(-tn_ref[...] * rates_ref[...])


def hyena_filters(L, w1, b1, freq1, w2, b2, freq2, w3, b3):
    tn = jnp.arange(L, dtype=jnp.float32) / L
    bands = jnp.arange(1, HY_POS_BANDS + 1, dtype=jnp.float32)
    ang = 2.0 * math.pi * tn[:, None] * bands[None, :]
    z = jnp.concatenate([tn[:, None], jnp.cos(ang), jnp.sin(ang)], axis=-1)
    rates = jnp.abs(jnp.linspace(math.log(HY_DECAY_TARGET) / HY_LONG_DECAY_PCT,
                                 math.log(HY_DECAY_TARGET) / HY_SHORT_DECAY_PCT, HY_WIDTH))
    pad = -z.shape[1] % 8
    T = min(HY_FILTER_ROWS, L)
    n_out = w3.shape[1]
    row = lambda a: a.reshape(1, -1)
    const = lambda a: pl.BlockSpec(a.shape, lambda i: (0,) * a.ndim)
    args = (jnp.pad(w1, ((0, pad), (0, 0))), row(b1), row(freq1), w2, row(b2), row(freq2), w3, row(b3),
            row(jnp.tile(rates, n_out // HY_WIDTH)))
    h = pl.pallas_call(
        _hyena_filter_kernel,
        grid=(L // T,),
        in_specs=[pl.BlockSpec((T, z.shape[1] + pad), lambda i: (i, 0)), pl.BlockSpec((T, 1), lambda i: (i, 0))]
                 + [const(a) for a in args],
        out_specs=pl.BlockSpec((T, n_out), lambda i: (i, 0)),
        out_shape=jax.ShapeDtypeStruct((L, n_out), jnp.float32),
        compiler_params=pltpu.CompilerParams(dimension_semantics=("parallel",)),
        name="hyena_filter_mlp",
    )(jnp.pad(z, ((0, 0), (0, pad))), tn[:, None], *args)
    h = h.reshape(L, HY_ORDER, 2, HY_WIDTH)
    zero = jnp.zeros((1, HY_ORDER, HY_WIDTH), h.dtype)
    h_full = jnp.concatenate([h[:, :, 0], zero, h[:0:-1, :, 1]], axis=0)
    return h_full * lax.rsqrt(jnp.sum(jnp.square(h_full), axis=0, keepdims=True))


def fft_long_conv(u, h_full, bias):
    L = u.shape[1]
    uf = jnp.fft.rfft(u, n=2 * L, axis=1)
    hf = jnp.fft.rfft(h_full, n=2 * L, axis=0)
    y = jnp.fft.irfft(uf * hf[None], n=2 * L, axis=1)[:, :L]
    return y + u * bias


FFT_N1 = 64
FFT_N2 = 128
FFT_N = FFT_N1 * FFT_N2
HY_SEQS = 32


def _dft_tables(seqs):
    n1 = np.arange(FFT_N1)
    n2 = np.arange(FFT_N2)
    f64 = np.exp(-2j * np.pi * np.outer(n1, n1) / FFT_N1)
    f128 = np.exp(-2j * np.pi * np.outer(n2, n2) / FFT_N2)
    tw = np.exp(-2j * np.pi * np.outer(n1, n2) / FFT_N)
    half = FFT_N1 // 2
    fh = f64[:, :half]
    m1 = np.block([[fh.real, -fh.imag], [fh.imag, fh.real]])
    m1f = np.concatenate([f64.real, f64.imag], axis=0)
    m2 = np.block([[f128.real, f128.imag], [-f128.imag, f128.real]])
    m3 = np.block([[f128.real, -f128.imag], [f128.imag, f128.real]]) / FFT_N
    c = np.conj(f64)[:half, :]
    m4 = np.block([[c.real, -c.imag], [c.imag, c.real]])
    bf = lambda a: jnp.asarray(a, jnp.float32).astype(jnp.bfloat16)
    f32 = lambda a: jnp.asarray(a, jnp.float32)
    return dict(m1=bf(m1), m1f=bf(m1f), m2=bf(m2), m3=bf(m3), m4=bf(m4),
                twr_l=f32(np.tile(tw.real, (1, seqs))), twi_l=f32(np.tile(tw.imag, (1, seqs))),
                twr_s=f32(np.tile(tw.real, (seqs, 1))), twi_s=f32(np.tile(tw.imag, (seqs, 1))))


def _spectrum(cols, m1, twr_l, twi_l, m2, R):
    a = jnp.dot(m1, cols.astype(jnp.bfloat16), preferred_element_type=jnp.float32)
    ar, ai = a[:FFT_N1], a[FFT_N1:]
    pr = ar * twr_l - ai * twi_l
    pi = ar * twi_l + ai * twr_l
    lhs = jnp.concatenate(
        [jnp.concatenate([pr[:, r * FFT_N2:(r + 1) * FFT_N2], pi[:, r * FFT_N2:(r + 1) * FFT_N2]], axis=1)
         for r in range(R)], axis=0)
    return jnp.dot(lhs.astype(jnp.bfloat16), m2, preferred_element_type=jnp.float32)


def _filter_fft_kernel(h_ref, m1f_ref, twr_ref, twi_ref, m2_ref, o_ref):
    R = HY_SEQS
    cols = jnp.concatenate([h_ref[r] for r in range(R)], axis=1)
    x = _spectrum(cols, m1f_ref[...], twr_ref[...], twi_ref[...], m2_ref[...], R)
    o_ref[...] = x.reshape(R, FFT_N1, 2 * FFT_N2)


def _hyena_conv_kernel(y_ref, g_ref, hf_ref, bias_ref, m1_ref, twr_l_ref, twi_l_ref, m2_ref, m3_ref,
                       twr_s_ref, twi_s_ref, m4_ref, o_ref):
    R = HY_SEQS
    half = FFT_N1 // 2
    y = [y_ref[0], y_ref[1]]
    for o in range(HY_ORDER):
        top = jnp.concatenate([y[0][r] for r in range(R)], axis=1)
        bot = jnp.concatenate([y[1][r] for r in range(R)], axis=1)
        x = _spectrum(jnp.concatenate([top, bot], axis=0), m1_ref[...], twr_l_ref[...], twi_l_ref[...], m2_ref[...], R)
        hf = hf_ref[o].reshape(R * FFT_N1, 2 * FFT_N2)
        xr, xi = x[:, :FFT_N2], x[:, FFT_N2:]
        hr, hi = hf[:, :FFT_N2], hf[:, FFT_N2:]
        yc = jnp.concatenate([xr * hr - xi * hi, xr * hi + xi * hr], axis=1)
        b = jnp.dot(yc.astype(jnp.bfloat16), m3_ref[...], preferred_element_type=jnp.float32)
        br, bi = b[:, :FFT_N2], b[:, FFT_N2:]
        qr = br * twr_s_ref[...] + bi * twi_s_ref[...]
        qi = bi * twr_s_ref[...] - br * twi_s_ref[...]
        bc = jnp.concatenate(
            [jnp.concatenate([qr[r * FFT_N1:(r + 1) * FFT_N1], qi[r * FFT_N1:(r + 1) * FFT_N1]], axis=0)
             for r in range(R)], axis=1)
        yo = jnp.dot(m4_ref[...], bc.astype(jnp.bfloat16), preferred_element_type=jnp.float32)
        for p in range(2):
            conv = jnp.stack([yo[p * half:(p + 1) * half, r * FFT_N2:(r + 1) * FFT_N2] for r in range(R)], axis=0)
            y[p] = g_ref[o, p] * (conv + y[p] * bias_ref[o])
    o_ref[0] = y[0]
    o_ref[1] = y[1]


def hyena_long_conv(y_t, g_t, h_t, bias):
    B, C, L = y_t.shape
    assert 2 * L == FFT_N and B % 2 == 0 and C % HY_SEQS == 0
    R = HY_SEQS
    half = FFT_N1 // 2
    tb = _dft_tables(R)
    const = lambda a: pl.BlockSpec(a.shape, lambda *_: (0,) * a.ndim)
    hf = pl.pallas_call(
        _filter_fft_kernel,
        grid=(HY_ORDER * C // R,),
        in_specs=[pl.BlockSpec((R, FFT_N1, FFT_N2), lambda i: (i, 0, 0)),
                  const(tb['m1f']), const(tb['twr_l']), const(tb['twi_l']), const(tb['m2'])],
        out_specs=pl.BlockSpec((R, FFT_N1, 2 * FFT_N2), lambda i: (i, 0, 0)),
        out_shape=jax.ShapeDtypeStruct((HY_ORDER * C, FFT_N1, 2 * FFT_N2), jnp.float32),
        compiler_params=pltpu.CompilerParams(dimension_semantics=("parallel",), vmem_limit_bytes=VMEM_LIMIT_BYTES),
        name="hyena_filter_fft",
    )(h_t.reshape(HY_ORDER * C, FFT_N1, FFT_N2), tb['m1f'], tb['twr_l'], tb['twi_l'], tb['m2'])
    hf = hf.reshape(HY_ORDER, C, FFT_N1, 2 * FFT_N2)
    out = pl.pallas_call(
        _hyena_conv_kernel,
        grid=(B // 2, C // R),
        in_specs=[pl.BlockSpec((2, R, half, FFT_N2), lambda b, c: (b, c, 0, 0)),
                  pl.BlockSpec((HY_ORDER, 2, R, half, FFT_N2), lambda b, c: (0, b, c, 0, 0)),
                  pl.BlockSpec((HY_ORDER, R, FFT_N1, 2 * FFT_N2), lambda b, c: (0, c, 0, 0)),
                  pl.BlockSpec((HY_ORDER, R, 1, 1), lambda b, c: (0, c, 0, 0)),
                  const(tb['m1']), const(tb['twr_l']), const(tb['twi_l']), const(tb['m2']), const(tb['m3']),
                  const(tb['twr_s']), const(tb['twi_s']), const(tb['m4'])],
        out_specs=pl.BlockSpec((2, R, half, FFT_N2), lambda b, c: (b, c, 0, 0)),
        out_shape=jax.ShapeDtypeStruct((B, C, half, FFT_N2), jnp.float32),
        compiler_params=pltpu.CompilerParams(dimension_semantics=("parallel", "parallel"),
                                             vmem_limit_bytes=VMEM_LIMIT_BYTES),
        name="hyena_conv",
    )(y_t.reshape(B, C, half, FFT_N2), g_t.reshape(HY_ORDER, B, C, half, FFT_N2), hf,
      bias.reshape(HY_ORDER, C, 1, 1), tb['m1'], tb['twr_l'], tb['twi_l'], tb['m2'], tb['m3'],
      tb['twr_s'], tb['twi_s'], tb['m4'])
    return out.reshape(B, C, L)


def hyena_mixer(p, conv_w, w1, b1, freq1, w2, b2, freq2, w3, b3, bias):
    B, L = p.shape[:2]
    z = short_conv(p, conv_w)
    h_full = hyena_filters(L, w1, b1, freq1, w2, b2, freq2, w3, b3)
    if 2 * L == FFT_N:
        g_t = jnp.transpose(z[..., :HY_ORDER * HY_WIDTH].reshape(B, L, HY_ORDER, HY_WIDTH), (2, 0, 3, 1))
        y_t = jnp.swapaxes(z[..., HY_ORDER * HY_WIDTH:], 1, 2)
        y_t = hyena_long_conv(y_t, g_t, jnp.transpose(h_full, (1, 2, 0)), bias)
        return jnp.swapaxes(y_t, 1, 2)
    gates = (z[..., :HY_WIDTH], z[..., HY_WIDTH:2 * HY_WIDTH])
    y = z[..., 2 * HY_WIDTH:]
    for o in range(HY_ORDER):
        y = gates[o] * fft_long_conv(y, h_full[:, o], bias[o])
    return y


SC_CORES = 2
SC_SUBCORES = 16
SC_LANES = 16
SC_WORKERS = SC_CORES * SC_SUBCORES
PEER_SLOTS = PEER_HEADS * PEER_TOPK
PEER_GATHER_ROWS = 32
PEER_GATHERS = PEER_SLOTS // PEER_GATHER_ROWS
PEER_ACC_VREGS = 8
PEER_ROW_BUFFERS = 4
PEER_ROW_WORDS = D_MODEL // 2
HI_MASK = -65536


def pack_expert_table(t):
    b = lax.bitcast_convert_type(t.astype(jnp.bfloat16), jnp.uint16).astype(jnp.uint32)
    return lax.bitcast_convert_type(b[:, :PEER_ROW_WORDS] | (b[:, PEER_ROW_WORDS:] << 16), jnp.int32)


def _sc_peer_phase(phase, tpw):
    NBUF = PEER_ROW_BUFFERS
    AHEAD = NBUF - 1
    HW = PEER_ROW_WORDS

    def run(base, table_hbm, idx_hbm, aux_hbm, out_hbm, idx_v, aux_v, rows_v, out_v, sem_r, sem_i, sem_o):

        def gather(p, c, b):
            return pltpu.make_async_copy(table_hbm.at[idx_v.at[p, c]], rows_v.at[b], sem_r.at[b])

        def load_meta(t, p):
            return (pltpu.make_async_copy(idx_hbm.at[t], idx_v.at[p], sem_i.at[p]),
                    pltpu.make_async_copy(aux_hbm.at[t], aux_v.at[p], sem_i.at[p]))

        def store_out(t, p):
            return pltpu.make_async_copy(out_v.at[p], out_hbm.at[t], sem_o.at[p])

        def halves(word):
            return (plsc.bitcast(lax.shift_left(word, 16), jnp.float32), plsc.bitcast(word & HI_MASK, jnp.float32))

        def compute(p, c, b):
            if phase == "dot":
                lane = lax.iota(jnp.int32, SC_LANES)
                vec = jnp.zeros((SC_LANES,), jnp.float32)
                groups_per_vec = SC_LANES // PEER_ACC_VREGS
                for g in range(PEER_GATHER_ROWS // PEER_ACC_VREGS):
                    def body(cc, accs):
                        x_lo = aux_v[p, pl.ds(cc * SC_LANES, SC_LANES)]
                        x_hi = aux_v[p, pl.ds(HW + cc * SC_LANES, SC_LANES)]
                        out = []
                        for r in range(PEER_ACC_VREGS):
                            lo, hi = halves(rows_v[b, g * PEER_ACC_VREGS + r, pl.ds(cc * SC_LANES, SC_LANES)])
                            out.append(accs[r] + lo * x_lo + hi * x_hi)
                        return tuple(out)
                    accs = lax.fori_loop(0, HW // SC_LANES, body,
                                         tuple(jnp.zeros((SC_LANES,), jnp.float32) for _ in range(PEER_ACC_VREGS)))
                    for r in range(PEER_ACC_VREGS):
                        vec = jnp.where(lane == (g % groups_per_vec) * PEER_ACC_VREGS + r, jnp.sum(accs[r]), vec)
                    if g % groups_per_vec == groups_per_vec - 1:
                        out_v[p, pl.ds(c * PEER_GATHER_ROWS + (g // groups_per_vec) * SC_LANES, SC_LANES)] = vec
            else:
                words = PEER_ACC_VREGS // 2
                for db in range(HW // (words * SC_LANES)):
                    def body(kk, accs):
                        wv = plsc.load_gather(aux_v.at[p], [jnp.full((SC_LANES,), c * PEER_GATHER_ROWS + kk, jnp.int32)])
                        out = []
                        for j in range(words):
                            lo, hi = halves(rows_v[b, kk, pl.ds((db * words + j) * SC_LANES, SC_LANES)])
                            out += [accs[2 * j] + lo * wv, accs[2 * j + 1] + hi * wv]
                        return tuple(out)
                    if c == 0:
                        init = tuple(jnp.zeros((SC_LANES,), jnp.float32) for _ in range(2 * words))
                    else:
                        init = tuple(out_v[p, pl.ds(half * HW + (db * words + j) * SC_LANES, SC_LANES)]
                                     for j in range(words) for half in range(2))
                    accs = lax.fori_loop(0, PEER_GATHER_ROWS, body, init)
                    for j in range(words):
                        out_v[p, pl.ds((db * words + j) * SC_LANES, SC_LANES)] = accs[2 * j]
                        out_v[p, pl.ds(HW + (db * words + j) * SC_LANES, SC_LANES)] = accs[2 * j + 1]

        for d in load_meta(base, 0):
            d.start()
        for d in load_meta(base, 0):
            d.wait()
        for c in range(AHEAD):
            gather(0, c, c % NBUF).start()

        @pl.loop(0, tpw // 2)
        def _(i2):
            for p in range(2):
                i = i2 * 2 + p
                t = base + i
                nxt = base + jnp.minimum(i + 1, tpw - 1)
                for d in load_meta(nxt, 1 - p):
                    d.start()

                @pl.when(i2 > 0)
                def _():
                    store_out(t, p).wait()

                for c in range(PEER_GATHERS):
                    ahead = c + AHEAD
                    if ahead < PEER_GATHERS:
                        gather(p, ahead, ahead % NBUF).start()
                    else:
                        if ahead == PEER_GATHERS:
                            for d in load_meta(nxt, 1 - p):
                                d.wait()
                        gather(1 - p, ahead - PEER_GATHERS, ahead % NBUF).start()
                    gather(p, c, c % NBUF).wait()
                    compute(p, c, c % NBUF)
                store_out(t, p).start()

        for c in range(AHEAD):
            gather(0, c, c % NBUF).wait()
        for p in range(2):
            store_out(base, p).wait()

    return run


def _sc_tokens_per_worker(N):
    assert N % (2 * SC_WORKERS) == 0 and PEER_GATHERS % PEER_ROW_BUFFERS == 0
    return N // SC_WORKERS


_SC_AUX = {"dot": (D_MODEL,), "wsum": (PEER_SLOTS,)}
_SC_OUT = {"dot": (PEER_SLOTS,), "wsum": (D_MODEL,)}


def _sc_scratch(phases):
    s = [pltpu.VMEM((2, PEER_GATHERS, PEER_GATHER_ROWS), jnp.int32),
         pltpu.VMEM((PEER_ROW_BUFFERS, PEER_GATHER_ROWS, PEER_ROW_WORDS), jnp.int32),
         pltpu.SemaphoreType.DMA((PEER_ROW_BUFFERS,)), pltpu.SemaphoreType.DMA((2,)), pltpu.SemaphoreType.DMA((2,))]
    for ph in phases:
        s += [pltpu.VMEM((2,) + _SC_AUX[ph], jnp.float32), pltpu.VMEM((2,) + _SC_OUT[ph], jnp.float32)]
    return s


def _sc_peer(table, idx, aux, phase):
    N = idx.shape[0]
    tpw = _sc_tokens_per_worker(N)
    run = _sc_peer_phase(phase, tpw)

    @functools.partial(
        pl.kernel, mesh=plsc.VectorSubcoreMesh(core_axis_name="c", subcore_axis_name="s"),
        out_type=jax.ShapeDtypeStruct((N,) + _SC_OUT[phase], jnp.float32),
        compiler_params=pltpu.CompilerParams(needs_layout_passes=False),
        scratch_types=_sc_scratch([phase]),
    )
    def k(table_hbm, idx_hbm, aux_hbm, out_hbm, idx_v, rows_v, sem_r, sem_i, sem_o, aux_v, out_v):
        base = (lax.axis_index("s") * SC_CORES + lax.axis_index("c")) * tpw
        run(base, table_hbm, idx_hbm, aux_hbm, out_hbm, idx_v, aux_v, rows_v, out_v, sem_r, sem_i, sem_o)

    return k(table, idx.reshape(N, PEER_GATHERS, PEER_GATHER_ROWS), aux)


def _sc_peer_pair(table_v, idx_a, w_a, table_u, idx_b, h_b):
    Na, Nb = idx_a.shape[0], idx_b.shape[0]
    tpw_a, tpw_b = _sc_tokens_per_worker(Na), _sc_tokens_per_worker(Nb)
    run_wsum = _sc_peer_phase("wsum", tpw_a)
    run_dot = _sc_peer_phase("dot", tpw_b)

    @functools.partial(
        pl.kernel, mesh=plsc.VectorSubcoreMesh(core_axis_name="c", subcore_axis_name="s"),
        out_type=(jax.ShapeDtypeStruct((Na,) + _SC_OUT["wsum"], jnp.float32),
                  jax.ShapeDtypeStruct((Nb,) + _SC_OUT["dot"], jnp.float32)),
        compiler_params=pltpu.CompilerParams(needs_layout_passes=False),
        scratch_types=_sc_scratch(["wsum", "dot"]),
    )
    def k(tv_hbm, ia_hbm, wa_hbm, tu_hbm, ib_hbm, hb_hbm, outa_hbm, outb_hbm,
          idx_v, rows_v, sem_r, sem_i, sem_o, w_v, outa_v, h_v, outb_v):
        worker = lax.axis_index("s") * SC_CORES + lax.axis_index("c")
        run_wsum(worker * tpw_a, tv_hbm, ia_hbm, wa_hbm, outa_hbm, idx_v, w_v, rows_v, outa_v, sem_r, sem_i, sem_o)
        run_dot(worker * tpw_b, tu_hbm, ib_hbm, hb_hbm, outb_hbm, idx_v, h_v, rows_v, outb_v, sem_r, sem_i, sem_o)

    shp = (PEER_GATHERS, PEER_GATHER_ROWS)
    return k(table_v, idx_a.reshape((Na,) + shp), w_a, table_u, idx_b.reshape((Nb,) + shp), h_b)


PEER_TOKENS = 256
INT_BIG = 2 ** 30
PEER_CANDIDATES = -(-sum(PEER_TOPK // (i + 1) for i in range(PEER_TOPK)) // 8) * 8


def _extract_topk(cand_ref, ids_ref, val_out_ref, id_out_ref, row0):
    def body(r, carry):
        c = cand_ref[...]
        ids = ids_ref[...]
        m = jnp.max(c, axis=0, keepdims=True)
        sel = jnp.min(jnp.where(c == m, ids, INT_BIG), axis=0, keepdims=True)
        cand_ref[...] = jnp.where(ids == sel, -jnp.inf, c)
        val_out_ref[pl.ds(row0 + r, 1), :] = m
        id_out_ref[pl.ds(row0 + r, 1), :] = sel
        return carry
    lax.fori_loop(0, PEER_TOPK, body, 0)


def _peer_retrieve_kernel(x_ref, gain_ref, scale_ref, shift_ref, wq_ref, keys_ref,
                          h_ref, idx_out_ref, gate_out_ref,
                          s_ref, ids1_ref, sv_ref, si_ref, cand_ref, cid_ref, ts_ref, idx_ref, gate_ref):
    x = x_ref[0]
    y = x * lax.rsqrt(jnp.mean(x * x, axis=-1, keepdims=True) + NORM_EPS)
    h = (y * gain_ref[...]) * (1.0 + scale_ref[0]) + shift_ref[0]
    h_ref[0] = h
    q = jnp.dot(h.astype(jnp.bfloat16), wq_ref[...], preferred_element_type=jnp.float32)
    T = PEER_TOKENS
    K = PEER_TOPK
    ids1_ref[...] = lax.broadcasted_iota(jnp.int32, (PEER_N_KEYS, T), 0)
    for hd in range(PEER_HEADS):
        for p in range(2):
            hp = hd * 2 + p
            qs = q[:, hp * PEER_HALF:(hp + 1) * PEER_HALF].astype(jnp.bfloat16)
            s_ref[...] = lax.dot_general(keys_ref[hp], qs, (((1,), (1,)), ((), ())),
                                         preferred_element_type=jnp.float32)
            _extract_topk(s_ref, ids1_ref, sv_ref, si_ref, p * K)
        cand_ref[...] = jnp.full(cand_ref.shape, -jnp.inf, jnp.float32)
        cid_ref[...] = INT_BIG - 1 - lax.broadcasted_iota(jnp.int32, cid_ref.shape, 0)
        off = 0
        for i in range(K):
            n = K // (i + 1)
            cand_ref[off:off + n, :] = sv_ref[i:i + 1, :] + sv_ref[K:K + n, :]
            cid_ref[off:off + n, :] = si_ref[i:i + 1, :] * PEER_N_KEYS + si_ref[K:K + n, :]
            off += n
        _extract_topk(cand_ref, cid_ref, ts_ref, idx_ref, hd * K)
        ts = ts_ref[hd * K:(hd + 1) * K, :]
        e = jnp.exp(ts - jnp.max(ts, axis=0, keepdims=True))
        gate_ref[hd * K:(hd + 1) * K, :] = e / jnp.sum(e, axis=0, keepdims=True)
    idx_out_ref[...] = idx_ref[...].T
    gate_out_ref[...] = gate_ref[...].T


def peer_retrieve(x, gain, scale, shift, w_q, sub_keys):
    B, L, D = x.shape
    T = PEER_TOKENS
    nt = L // T
    keys = sub_keys.reshape(PEER_HEADS * 2, PEER_N_KEYS, PEER_HALF).astype(jnp.bfloat16)
    return pl.pallas_call(
        _peer_retrieve_kernel,
        grid=(B, nt),
        in_specs=[
            pl.BlockSpec((1, T, D), lambda b, i: (b, i, 0)),
            pl.BlockSpec((1, D), lambda b, i: (0, 0)),
            pl.BlockSpec((1, 1, D), lambda b, i: (b, 0, 0)),
            pl.BlockSpec((1, 1, D), lambda b, i: (b, 0, 0)),
            pl.BlockSpec((D, PEER_HEADS * 2 * PEER_HALF), lambda b, i: (0, 0)),
            pl.BlockSpec((PEER_HEADS * 2, PEER_N_KEYS, PEER_HALF), lambda b, i: (0, 0, 0)),
        ],
        out_specs=[
            pl.BlockSpec((1, T, D), lambda b, i: (b, i, 0)),
            pl.BlockSpec((T, PEER_SLOTS), lambda b, i: (b * nt + i, 0)),
            pl.BlockSpec((T, PEER_SLOTS), lambda b, i: (b * nt + i, 0)),
        ],
        out_shape=[
            jax.ShapeDtypeStruct((B, L, D), jnp.float32),
            jax.ShapeDtypeStruct((B * L, PEER_SLOTS), jnp.int32),
            jax.ShapeDtypeStruct((B * L, PEER_SLOTS), jnp.float32),
        ],
        scratch_shapes=[
            pltpu.VMEM((PEER_N_KEYS, T), jnp.float32),
            pltpu.VMEM((PEER_N_KEYS, T), jnp.int32),
            pltpu.VMEM((2 * PEER_TOPK, T), jnp.float32),
            pltpu.VMEM((2 * PEER_TOPK, T), jnp.int32),
            pltpu.VMEM((PEER_CANDIDATES, T), jnp.float32),
            pltpu.VMEM((PEER_CANDIDATES, T), jnp.int32),
            pltpu.VMEM((PEER_SLOTS, T), jnp.float32),
            pltpu.VMEM((PEER_SLOTS, T), jnp.int32),
            pltpu.VMEM((PEER_SLOTS, T), jnp.float32),
        ],
        compiler_params=pltpu.CompilerParams(dimension_semantics=("parallel", "parallel"),
                                             vmem_limit_bytes=VMEM_LIMIT_BYTES),
        name="peer_retrieve",
    )(x, gain.reshape(1, D), scale, shift, w_q.astype(jnp.bfloat16), keys)


PEER_ACT_ROWS = 256


def _peer_act_kernel(dots_ref, gate_ref, w_ref):
    a = dots_ref[...]
    w_ref[...] = gate_ref[...] * (0.5 * a * (1.0 + lax.erf(a * (2.0 ** -0.5))))


def peer_act(dots, gate):
    N = dots.shape[0]
    T = min(PEER_ACT_ROWS, N)
    spec = pl.BlockSpec((T, PEER_SLOTS), lambda i: (i, 0))
    return pl.pallas_call(
        _peer_act_kernel,
        grid=(N // T,),
        in_specs=[spec, spec],
        out_specs=spec,
        out_shape=jax.ShapeDtypeStruct((N, PEER_SLOTS), jnp.float32),
        compiler_params=pltpu.CompilerParams(dimension_semantics=("parallel",)),
        name="peer_act",
    )(dots, gate)


OUT_PROJ_ROWS = 512


def _out_proj_kernel(x_ref, g_ref, rw_ref, ml_ref, hy_ref, w_ref, o_ref):
    bf = jnp.bfloat16
    y = jnp.dot(rw_ref[0].astype(bf), w_ref[:RW_WIDTH, :], preferred_element_type=jnp.float32)
    y += jnp.dot(ml_ref[0].astype(bf), w_ref[RW_WIDTH:RW_WIDTH + MLA_WIDTH, :], preferred_element_type=jnp.float32)
    y += jnp.dot(hy_ref[0].astype(bf), w_ref[RW_WIDTH + MLA_WIDTH:, :], preferred_element_type=jnp.float32)
    o_ref[0] = x_ref[0] + g_ref[0] * y


def mix_out_proj(x, gate, rw, ml, hy, w_out):
    B, L, D = x.shape
    T = min(OUT_PROJ_ROWS, L)
    tok = lambda w: pl.BlockSpec((1, T, w), lambda b, i: (b, i, 0))
    return pl.pallas_call(
        _out_proj_kernel,
        grid=(B, L // T),
        in_specs=[tok(D), pl.BlockSpec((1, 1, D), lambda b, i: (b, 0, 0)), tok(RW_WIDTH), tok(MLA_WIDTH), tok(HY_WIDTH),
                  pl.BlockSpec((MIX_WIDTH, D), lambda b, i: (0, 0))],
        out_specs=tok(D),
        out_shape=jax.ShapeDtypeStruct((B, L, D), jnp.float32),
        compiler_params=pltpu.CompilerParams(dimension_semantics=("parallel", "parallel"),
                                             vmem_limit_bytes=VMEM_LIMIT_BYTES),
        name="mix_out_proj",
    )(x, gate, rw, ml, hy, w_out.astype(jnp.bfloat16))


def peer_ffn(x, gain, scale, shift, w_q, sub_keys, exp_u, exp_v):
    B, L, D = x.shape
    N = B * L
    h, e_idx, gate = peer_retrieve(x, gain, scale, shift, w_q, sub_keys)
    dots = _sc_peer(exp_u, e_idx, h.reshape(N, D), "dot")
    return _sc_peer(exp_v, e_idx, peer_act(dots, gate), "wsum").reshape(B, L, D)


def _mix_and_retrieve(li, x, c, ctx, c_ctx, mod_w, mod_b, mix_norm, w_in, w_out, rw_conv, rw_decay_up, rw_decay0, rw_a_up, rw_a0, rw_gate_up, rw_k_k, rw_k_a, rw_r_k, rw_gn_g, rw_gn_b, mla_q_norm, mla_w_uq, mla_kv_norm, mla_w_ukv, mla_q_gain, mla_k_gain, hy_conv, hy_w1, hy_b1, hy_freq1, hy_w2, hy_b2, hy_freq2, hy_w3, hy_b3, hy_bias, ffn_norm, peer_wq, peer_keys, peer_u, peer_v):
    B, L, D = x.shape
    need_ctx = li < DEPTH - 1
    mod_l = (jax.nn.silu(c) @ mod_w[li] + mod_b[li])[:, None, :]
    mod_c = (jax.nn.silu(c_ctx) @ mod_w[li] + mod_b[li])[None, None, :]
    shm_l, scm_l, gm_l, shf_l, scf_l, gf_l = jnp.split(mod_l, N_MOD, axis=-1)
    shm_c, scm_c, gm_c, shf_c, scf_c, gf_c = jnp.split(mod_c, N_MOD, axis=-1)

    widths = (RW_PROJ, MLA_PROJ, HY_PROJ)
    prw_l, pml_l, phy_l = norm_mod_proj(x, mix_norm[li], scm_l, shm_l, w_in[li], widths, 512)
    prw_c, pml_c, phy_c = norm_mod_proj(ctx, mix_norm[li], jnp.broadcast_to(scm_c, (B, 1, D)),
                                        jnp.broadcast_to(shm_c, (B, 1, D)), w_in[li], widths, 256)
    rw_l, rw_c = rwkv7_mixer(prw_l, prw_c, rw_conv[li], rw_decay_up[li], rw_decay0[li],
                             rw_a_up[li], rw_a0[li], rw_gate_up[li], rw_k_k[li], rw_k_a[li], rw_r_k[li],
                             rw_gn_g[li], rw_gn_b[li], need_ctx)
    ml_l, ml_c = mla_mixer(pml_l, pml_c, mla_q_norm[li], mla_w_uq[li],
                           mla_kv_norm[li], mla_w_ukv[li], mla_q_gain[li], mla_k_gain[li], need_ctx)
    hy_prm = (hy_conv[li], hy_w1[li], hy_b1[li], hy_freq1[li], hy_w2[li], hy_b2[li], hy_freq2[li],
              hy_w3[li], hy_b3[li], hy_bias[li])
    hy_l = hyena_mixer(phy_l, *hy_prm)
    x = mix_out_proj(x, gm_l, rw_l, ml_l, hy_l, w_out[li])
    if need_ctx:
        hy_c = hyena_mixer(phy_c, *hy_prm)
        ctx = mix_out_proj(ctx, jnp.broadcast_to(gm_c, (B, 1, D)), rw_c, ml_c, hy_c, w_out[li])
        ctx = ctx + gf_c * peer_ffn(ctx, ffn_norm[li], jnp.broadcast_to(scf_c, (B, 1, D)),
                                    jnp.broadcast_to(shf_c, (B, 1, D)),
                                    peer_wq[li], peer_keys[li], peer_u[li], peer_v[li])
    h, e_idx, gate = peer_retrieve(x, ffn_norm[li], scf_l, shf_l, peer_wq[li], peer_keys[li])
    return x, ctx, gf_l, h.reshape(B * L, D), e_idx, gate


BATCH_GROUP_ROWS = (2, 2, 2, 2)


def kernel(x, c, ctx, c_ctx, mod_w, mod_b, mix_norm, w_in, w_out, rw_conv, rw_decay_up, rw_decay0, rw_a_up, rw_a0, rw_gate_up, rw_k_k, rw_k_a, rw_r_k, rw_gn_g, rw_gn_b, mla_q_norm, mla_w_uq, mla_kv_norm, mla_w_ukv, mla_q_gain, mla_k_gain, hy_conv, hy_w1, hy_b1, hy_freq1, hy_w2, hy_b2, hy_freq2, hy_w3, hy_b3, hy_bias, ffn_norm, peer_wq, peer_keys, peer_u, peer_v):
    params = (mod_w, mod_b, mix_norm, w_in, w_out, rw_conv, rw_decay_up, rw_decay0, rw_a_up, rw_a0, rw_gate_up,
              rw_k_k, rw_k_a, rw_r_k, rw_gn_g, rw_gn_b, mla_q_norm, mla_w_uq, mla_kv_norm, mla_w_ukv, mla_q_gain,
              mla_k_gain, hy_conv, hy_w1, hy_b1, hy_freq1, hy_w2, hy_b2, hy_freq2, hy_w3, hy_b3, hy_bias,
              ffn_norm, peer_wq, peer_keys)
    peer_u = [pack_expert_table(peer_u[li]) for li in range(DEPTH)]
    peer_v = [pack_expert_table(peer_v[li]) for li in range(DEPTH)]
    params = params + (peer_u, peer_v)
    assert sum(BATCH_GROUP_ROWS) == x.shape[0]
    G = len(BATCH_GROUP_ROWS)
    lo = [sum(BATCH_GROUP_ROWS[:g]) for g in range(G + 1)]
    L, D = x.shape[1:]
    xs = [x[lo[g]:lo[g + 1]] for g in range(G)]
    cs = [c[lo[g]:lo[g + 1]] for g in range(G)]
    ctxs = [ctx[lo[g]:lo[g + 1]] for g in range(G)]
    stages = [(li, g) for li in range(DEPTH) for g in range(G)]
    prev = None
    token = None
    for li, g in stages:
        ins = (xs[g], ctxs[g])
        if token is not None:
            token, ins = lax.optimization_barrier((token, ins))
        xm, ctxs[g], gf, h, e_idx, gate = _mix_and_retrieve(li, ins[0], cs[g], ins[1], c_ctx, *params)
        token = gate
        if prev is None:
            dots = _sc_peer(peer_u[li], e_idx, h, "dot")
        else:
            pl_, pg, pxm, pgf, pidx, pgate, pdots = prev
            token, (pdots, pgate) = lax.optimization_barrier((token, (pdots, pgate)))
            w = peer_act(pdots, pgate)
            token = w
            out, dots = _sc_peer_pair(peer_v[pl_], pidx, w, peer_u[li], e_idx, h)
            xs[pg] = pxm + pgf * out.reshape(pxm.shape)
        prev = (li, g, xm, gf, e_idx, gate, dots)
    pl_, pg, pxm, pgf, pidx, pgate, pdots = prev
    xs[pg] = pxm + pgf * _sc_peer(peer_v[pl_], pidx, peer_act(pdots, pgate), "wsum").reshape(pxm.shape)
    return jnp.concatenate(xs, axis=0)
```

```python
import functools
import math

import jax
import jax.numpy as jnp
import numpy as np
from jax import lax
from jax.experimental import pallas as pl
from jax.experimental.pallas import tpu as pltpu
from jax.experimental.pallas import tpu_sc as plsc

D_MODEL = 1024
DEPTH = 2
GRID_W = 64
N_MOD = 6
NORM_EPS = 1e-6
SHORT_CONV = 3

RW_HEADS = 6
RW_HEAD_DIM = 64
RW_WIDTH = RW_HEADS * RW_HEAD_DIM
RW_DECAY_RANK = 64
RW_A_RANK = 64
RW_GATE_RANK = 128
RW_DECAY_SCALE = 0.6065306597
RW_GN_EPS = 64e-5
L2_EPS = 1e-12

MLA_HEADS = 6
MLA_Q_RANK = 256
MLA_KV_RANK = 128
MLA_NOPE_DIM = 64
MLA_ROPE_DIM = 32
MLA_V_DIM = 64
MLA_QK_DIM = MLA_NOPE_DIM + MLA_ROPE_DIM
MLA_WIDTH = MLA_HEADS * MLA_V_DIM
AXIS_ROPE_DIM = MLA_ROPE_DIM // 2
ROPE_THETA = 10000.0

HY_WIDTH = 256
HY_ORDER = 2
HY_POS_BANDS = 16
HY_SHORT_DECAY_PCT = 0.3
HY_LONG_DECAY_PCT = 1.5
HY_DECAY_TARGET = 1e-2

PEER_HEADS = 8
PEER_N_KEYS = 128
PEER_TOPK = 16
PEER_QUERY_DIM = 256
PEER_HALF = PEER_QUERY_DIM // 2

RW_PROJ = 3 * RW_WIDTH + RW_DECAY_RANK + RW_A_RANK + RW_GATE_RANK
MLA_PROJ = MLA_Q_RANK + MLA_KV_RANK + MLA_ROPE_DIM
HY_PROJ = (HY_ORDER + 1) * HY_WIDTH
IN_PROJ = RW_PROJ + MLA_PROJ + HY_PROJ
MIX_WIDTH = RW_WIDTH + MLA_WIDTH + HY_WIDTH

VMEM_LIMIT_BYTES = 48 * 1024 * 1024


def _norm_mod_proj_kernel(x_ref, gain_ref, scale_ref, shift_ref, *refs):
    w_refs, o_refs = refs[:len(refs) // 2], refs[len(refs) // 2:]
    x = x_ref[0]
    y = x * lax.rsqrt(jnp.mean(x * x, axis=-1, keepdims=True) + NORM_EPS)
    y = y * gain_ref[...]
    y = (y * (1.0 + scale_ref[0]) + shift_ref[0]).astype(jnp.bfloat16)
    for w_ref, o_ref in zip(w_refs, o_refs):
        o_ref[0] = jnp.dot(y, w_ref[...], preferred_element_type=jnp.float32)


def norm_mod_proj(x, gain, scale, shift, w, widths, block_rows):
    B, L, D = x.shape
    assert sum(widths) == w.shape[1]
    offs = [sum(widths[:i]) for i in range(len(widths))]
    ws = [w[:, o:o + n].astype(jnp.bfloat16) for o, n in zip(offs, widths)]
    return pl.pallas_call(
        _norm_mod_proj_kernel,
        grid=(B, L // block_rows),
        in_specs=[
            pl.BlockSpec((1, block_rows, D), lambda b, i: (b, i, 0)),
            pl.BlockSpec((1, D), lambda b, i: (0, 0)),
            pl.BlockSpec((1, 1, D), lambda b, i: (b, 0, 0)),
            pl.BlockSpec((1, 1, D), lambda b, i: (b, 0, 0)),
        ] + [pl.BlockSpec((D, n), lambda b, i: (0, 0)) for n in widths],
        out_specs=[pl.BlockSpec((1, block_rows, n), lambda b, i: (b, i, 0)) for n in widths],
        out_shape=[jax.ShapeDtypeStruct((B, L, n), jnp.float32) for n in widths],
        compiler_params=pltpu.CompilerParams(
            dimension_semantics=("parallel", "parallel"), vmem_limit_bytes=VMEM_LIMIT_BYTES),
        name="norm_mod_proj",
    )(x, gain.reshape(1, D), scale, shift, *ws)


CONV_ROWS = 512
SUBLANES = 8


def _short_conv_kernel(x_ref, prev_ref, next_ref, w_ref, o_ref):
    i = pl.program_id(1)
    x = x_ref[0]
    T = x.shape[0]
    row = lax.broadcasted_iota(jnp.int32, x.shape, 0)
    before = jnp.where(i == 0, 0.0, prev_ref[0, SUBLANES - 1:SUBLANES, :])
    after = jnp.where(i == pl.num_programs(1) - 1, 0.0, next_ref[0, 0:1, :])
    up = jnp.where(row == 0, before, pltpu.roll(x, 1, 0))
    down = jnp.where(row == T - 1, after, pltpu.roll(x, T - 1, 0))
    o_ref[0] = up * w_ref[0:1, :] + x * w_ref[1:2, :] + down * w_ref[2:3, :]


def short_conv(x, w):
    B, L, C = x.shape
    T = min(CONV_ROWS, L)
    per = T // SUBLANES
    last = L // SUBLANES - 1
    return pl.pallas_call(
        _short_conv_kernel,
        grid=(B, L // T),
        in_specs=[pl.BlockSpec((1, T, C), lambda b, i: (b, i, 0)),
                  pl.BlockSpec((1, SUBLANES, C), lambda b, i: (b, jnp.maximum(i * per - 1, 0), 0)),
                  pl.BlockSpec((1, SUBLANES, C), lambda b, i: (b, jnp.minimum((i + 1) * per, last), 0)),
                  pl.BlockSpec((SHORT_CONV, C), lambda b, i: (0, 0))],
        out_specs=pl.BlockSpec((1, T, C), lambda b, i: (b, i, 0)),
        out_shape=jax.ShapeDtypeStruct((B, L, C), jnp.float32),
        compiler_params=pltpu.CompilerParams(dimension_semantics=("parallel", "parallel"),
                                             vmem_limit_bytes=VMEM_LIMIT_BYTES),
        name="short_conv",
    )(x, x, x, w)


RW_CHUNK = 64


def _rwkv_chunk_kernel(r_ref, kk_ref, v_ref, lw_ref, akk_ref, kr_ref, y_ref, h_ref):
    d = pl.program_id(0)
    n = pl.program_id(2)

    @pl.when(n == 0)
    def _():
        h_ref[...] = jnp.zeros_like(h_ref)

    C = RW_CHUNK
    row = lax.broadcasted_iota(jnp.int32, (C, C), 0)
    col = lax.broadcasted_iota(jnp.int32, (C, C), 1)
    lag = (row - col) * (1 - 2 * d)
    before = lag > 0
    upto = lag >= 0
    tri = upto.astype(jnp.float32)
    eye = (row == col).astype(jnp.float32)
    bf = jnp.bfloat16
    f32 = jnp.float32

    def mm(a, b):
        return jnp.dot(a.astype(bf), b.astype(bf), preferred_element_type=f32)

    def mm_nt(a, b):
        return lax.dot_general(a.astype(bf), b.astype(bf), (((1,), (1,)), ((), ())), preferred_element_type=f32)

    def mm_tn(a, b):
        return lax.dot_general(a.astype(bf), b.astype(bf), (((0,), (0,)), ((), ())), preferred_element_type=f32)

    hs = range(RW_HEADS)
    HD = RW_HEAD_DIM
    heads = lambda t: [t[:, h * HD:(h + 1) * HD] for h in hs]
    r = heads(r_ref[0])
    kk = heads(kk_ref[0])
    v = heads(v_ref[0])
    lw = heads(lw_ref[0, 0])
    akk = heads(akk_ref[0, 0])
    kr = heads(kr_ref[0, 0])
    G = [jnp.dot(tri, lw[h], preferred_element_type=f32, precision=lax.Precision.HIGHEST) for h in hs]
    gtot = [jnp.sum(lw[h], axis=0, keepdims=True) for h in hs]
    Einv = [jnp.exp(-G[h]) for h in hs]
    At = [-kk[h] * jnp.exp(G[h] - lw[h]) for h in hs]
    Rt = [r[h] * jnp.exp(G[h]) for h in hs]
    Bt = [akk[h] * Einv[h] for h in hs]
    Kt = [kr[h] * Einv[h] for h in hs]
    X = [mm_nt(jnp.concatenate([At[h], Rt[h]], axis=0), jnp.concatenate([Bt[h], Kt[h]], axis=0)) for h in hs]
    M_ab = [jnp.where(before, X[h][:C, :C], 0.0) for h in hs]
    M_ak = [jnp.where(before, X[h][:C, C:], 0.0) for h in hs]
    A_rb = [jnp.where(upto, X[h][C:, :C], 0.0) for h in hs]
    A_rk = [jnp.where(upto, X[h][C:, C:], 0.0) for h in hs]
    MV = [mm(M_ak[h], v[h]) for h in hs]
    Mp = M_ab
    T = [eye + Mp[h] for h in hs]
    for _ in range(5):
        Mp = [jnp.dot(Mp[h], Mp[h], preferred_element_type=f32) for h in hs]
        T = [T[h] + jnp.dot(T[h], Mp[h], preferred_element_type=f32) for h in hs]
    WU = [jnp.dot(T[h], jnp.concatenate([At[h], MV[h]], axis=1), preferred_element_type=f32) for h in hs]
    H0 = [h_ref[h] for h in hs]
    Ehat = [jnp.exp(gtot[h] - G[h]) for h in hs]
    Om = [Rt[h] + mm(A_rb[h], WU[h][:, :HD]) for h in hs]
    Y0 = [mm(A_rb[h], WU[h][:, HD:]) + mm(A_rk[h], v[h]) for h in hs]
    BW = [mm_tn(akk[h] * Ehat[h], WU[h]) for h in hs]
    KV = [mm_tn(kr[h] * Ehat[h], v[h]) for h in hs]
    y_ref[0, 0] = jnp.concatenate([jnp.dot(Om[h], H0[h], preferred_element_type=f32) + Y0[h] for h in hs], axis=1)
    for h in hs:
        P = eye * jnp.exp(gtot[h]) + BW[h][:, :HD]
        h_ref[h] = jnp.dot(P, H0[h], preferred_element_type=f32) + BW[h][:, HD:] + KV[h]


def rwkv_chunked(r, kk, v, lw, akk, kr, n_ctx):
    B, T, W = r.shape
    H = W // RW_HEAD_DIM
    nc = n_ctx // RW_CHUNK
    nt = T // RW_CHUNK

    def chunk_of(d, n):
        bwd = jnp.where(n < nc, nc - 1 - n, nt - 1 - (n - nc))
        return jnp.where(d == 0, n, bwd)

    spec1 = pl.BlockSpec((1, RW_CHUNK, W), lambda d, b, n: (b, chunk_of(d, n), 0))
    spec2 = pl.BlockSpec((1, 1, RW_CHUNK, W), lambda d, b, n: (d, b, chunk_of(d, n), 0))
    return pl.pallas_call(
        _rwkv_chunk_kernel,
        grid=(2, B, nt),
        in_specs=[spec1, spec1, spec1, spec2, spec2, spec2],
        out_specs=spec2,
        out_shape=jax.ShapeDtypeStruct((2, B, T, W), jnp.float32),
        scratch_shapes=[pltpu.VMEM((H, RW_HEAD_DIM, RW_HEAD_DIM), jnp.float32)],
        compiler_params=pltpu.CompilerParams(dimension_semantics=("parallel", "parallel", "arbitrary")),
        name="rwkv_chunked",
    )(r, kk, v, lw, akk, kr)


LANE = 128
RW_PREP_ROWS = 256
MLA_PAD_WIDTH = MLA_HEADS * LANE
MLA_PREP_ROWS = 256
ATTN_Q_ROWS = 512


def _split_dot(x, m):
    hi = x.astype(jnp.bfloat16)
    lo = (x - hi.astype(jnp.float32)).astype(jnp.bfloat16)
    return (jnp.dot(hi, m, preferred_element_type=jnp.float32) + jnp.dot(lo, m, preferred_element_type=jnp.float32))


def _rwkv_prep_kernel(z_ref, wda_ref, d0_ref, a0_ref, gup_ref, kk_ref_w, ka_ref, rk_ref, hsum_ref,
                      r_ref, kk_ref, v_ref, lw_ref, akk_ref, kr_ref, g_ref, bonus_ref):
    W = RW_WIDTH
    bf = jnp.bfloat16
    z = z_ref[0]
    r, k, v = z[:, :W], z[:, W:2 * W], z[:, 2 * W:3 * W]
    da = z[:, 3 * W:3 * W + LANE]
    lane = lax.broadcasted_iota(jnp.int32, da.shape, 1)
    da = jnp.where(lane < RW_DECAY_RANK, jnp.tanh(da), da)
    up = jnp.dot(da.astype(bf), wda_ref[...], preferred_element_type=jnp.float32)
    g_lo = z[:, 3 * W + LANE:]
    g_ref[0] = jnp.dot(jax.nn.sigmoid(g_lo).astype(bf), gup_ref[...], preferred_element_type=jnp.float32)
    hsum = hsum_ref[...]
    kk = k * kk_ref_w[...]
    kk = kk * lax.rsqrt(_split_dot(kk * kk, hsum) + L2_EPS)
    r_ref[0] = r
    v_ref[0] = v
    kk_ref[0] = kk
    bonus_ref[0] = _split_dot(r * k * rk_ref[...], hsum) * v
    for d in range(2):
        lw_ref[d, 0] = -RW_DECAY_SCALE * jax.nn.sigmoid(d0_ref[d:d + 1, :] + up[:, d * W:(d + 1) * W])
        a = jax.nn.sigmoid(a0_ref[d:d + 1, :] + up[:, (2 + d) * W:(3 + d) * W])
        akk_ref[d, 0] = kk * a
        kr_ref[d, 0] = k * (1.0 + (a - 1.0) * ka_ref[...])


def rwkv_prep(z, decay_up, decay0, a_up, a0, gate_up, k_k, k_a, r_k):
    B, L, _ = z.shape
    W = RW_WIDTH
    T = min(RW_PREP_ROWS, L)
    zero = jnp.zeros((RW_DECAY_RANK, 2 * W), jnp.float32)
    wda = jnp.concatenate([
        jnp.concatenate([decay_up[0], decay_up[1], zero], axis=1),
        jnp.concatenate([zero, a_up[0], a_up[1]], axis=1)], axis=0).astype(jnp.bfloat16)
    head = jnp.arange(W) // RW_HEAD_DIM
    hsum = (head[:, None] == head[None, :]).astype(jnp.bfloat16)
    row = lambda a: a.reshape(1, W)
    const = lambda a: pl.BlockSpec(a.shape, lambda b, i: (0,) * a.ndim)
    tok = pl.BlockSpec((1, T, W), lambda b, i: (b, i, 0))
    tok2 = pl.BlockSpec((2, 1, T, W), lambda b, i: (0, b, i, 0))
    f1 = jax.ShapeDtypeStruct((B, L, W), jnp.float32)
    f2 = jax.ShapeDtypeStruct((2, B, L, W), jnp.float32)
    args = (z, wda, decay0, a0, gate_up.astype(jnp.bfloat16), row(k_k), row(k_a), row(r_k), hsum)
    return pl.pallas_call(
        _rwkv_prep_kernel,
        grid=(B, L // T),
        in_specs=[pl.BlockSpec((1, T, RW_PROJ), lambda b, i: (b, i, 0))] + [const(a) for a in args[1:]],
        out_specs=[tok, tok, tok, tok2, tok2, tok2, tok, tok],
        out_shape=[f1, f1, f1, f2, f2, f2, f1, f1],
        compiler_params=pltpu.CompilerParams(dimension_semantics=("parallel", "parallel"),
                                             vmem_limit_bytes=VMEM_LIMIT_BYTES),
        name="rwkv_prep",
    )(*args)


def _rwkv_readout_kernel(y_ref, g_ref, bonus_ref, gng_ref, gnb_ref, hsum_ref, o_ref):
    y = y_ref[0, 0] + y_ref[1, 0]
    hsum = hsum_ref[...]
    mu = _split_dot(y, hsum) * (1.0 / RW_HEAD_DIM)
    d = y - mu
    var = _split_dot(d * d, hsum) * (1.0 / RW_HEAD_DIM)
    yn = d * lax.rsqrt(var + RW_GN_EPS) * gng_ref[...] + gnb_ref[...]
    o_ref[0] = (yn + bonus_ref[0]) * g_ref[0]


def rwkv_readout(y, g, bonus, gn_g, gn_b, t0):
    B, L, W = g.shape
    T = min(RW_PREP_ROWS, L)
    off = t0 // T
    head = jnp.arange(W) // RW_HEAD_DIM
    hsum = (head[:, None] == head[None, :]).astype(jnp.bfloat16)
    tok = pl.BlockSpec((1, T, W), lambda b, i: (b, i, 0))
    const = lambda a: pl.BlockSpec(a.shape, lambda b, i: (0,) * a.ndim)
    gg, gb = gn_g.reshape(1, W), gn_b.reshape(1, W)
    return pl.pallas_call(
        _rwkv_readout_kernel,
        grid=(B, L // T),
        in_specs=[pl.BlockSpec((2, 1, T, W), lambda b, i: (0, b, i + off, 0)), tok, tok, const(gg), const(gb), const(hsum)],
        out_specs=tok,
        out_shape=jax.ShapeDtypeStruct((B, L, W), jnp.float32),
        compiler_params=pltpu.CompilerParams(dimension_semantics=("parallel", "parallel")),
        name="rwkv_readout",
    )(y, g, bonus, gg, gb, hsum)


def rwkv7_mixer(p_lat, p_ctx, conv_w, decay_up, decay0, a_up, a0, gate_up, k_k, k_a, r_k, gn_g, gn_b, need_ctx):
    prm = (decay_up, decay0, a_up, a0, gate_up, k_k, k_a, r_k)
    lat = rwkv_prep(short_conv(p_lat, conv_w), *prm)
    ctx = rwkv_prep(short_conv(p_ctx, conv_w), *prm)
    n_ctx = p_ctx.shape[1]
    seq = lambda i: jnp.concatenate([ctx[i], lat[i]], axis=-2)
    y = rwkv_chunked(seq(0), seq(1), seq(2), seq(3), seq(4), seq(5), n_ctx)
    out_l = rwkv_readout(y, lat[6], lat[7], gn_g, gn_b, n_ctx)
    out_c = rwkv_readout(y, ctx[6], ctx[7], gn_g, gn_b, 0) if need_ctx else None
    return out_l, out_c


def _rope_tables(L, use_rope):
    lane = np.arange(LANE)
    in_rope = (lane >= MLA_NOPE_DIM) & (lane < MLA_QK_DIM)
    j = lane - MLA_NOPE_DIM
    axis = j // AXIS_ROPE_DIM
    half = AXIS_ROPE_DIM // 2
    f = j % half
    first = (j % AXIS_ROPE_DIM) < half
    inv = ROPE_THETA ** (-jnp.arange(0, AXIS_ROPE_DIM, 2, dtype=jnp.float32) / AXIS_ROPE_DIM)
    t = jnp.arange(L)
    pos = jnp.stack([t // GRID_W, t % GRID_W], axis=-1).astype(jnp.float32)
    ang = pos[:, np.clip(axis, 0, 1)] * inv[np.clip(f, 0, half - 1)][None, :]
    rope_on = jnp.asarray(in_rope)[None, :] & use_rope
    cos = jnp.where(rope_on, jnp.cos(ang), 1.0)
    sin = jnp.where(rope_on, jnp.sin(ang) * jnp.where(jnp.asarray(first), -1.0, 1.0)[None, :], 0.0)
    return jnp.tile(cos, (1, MLA_HEADS)), jnp.tile(sin, (1, MLA_HEADS))


def _mla_prep_kernel(p_ref, qn_ref, wq_ref, kvn_ref, wk_ref, wv_ref, place_ref, qg_ref, kg_ref, hsum_ref, cos_ref, sin_ref,
                     q_ref, k_ref, v_ref):
    bf = jnp.bfloat16
    p = p_ref[0]
    c_q = p[:, :MLA_Q_RANK]
    c_kv = p[:, MLA_Q_RANK:MLA_Q_RANK + MLA_KV_RANK]
    tail = p[:, MLA_Q_RANK + MLA_KV_RANK:]
    cqn = c_q * lax.rsqrt(jnp.mean(c_q * c_q, axis=-1, keepdims=True) + NORM_EPS) * qn_ref[...]
    ckn = c_kv * lax.rsqrt(jnp.mean(c_kv * c_kv, axis=-1, keepdims=True) + NORM_EPS) * kvn_ref[...]
    q = jnp.dot(cqn.astype(bf), wq_ref[...], preferred_element_type=jnp.float32)
    k = jnp.dot(ckn.astype(bf), wk_ref[...], preferred_element_type=jnp.float32) + _split_dot(tail, place_ref[...])
    v_ref[0] = jnp.dot(ckn.astype(bf), wv_ref[...], preferred_element_type=jnp.float32).astype(bf)
    hsum = hsum_ref[...]
    cos, sin = cos_ref[...], sin_ref[...]
    lane = lax.broadcasted_iota(jnp.int32, q.shape, 1)
    first = ((lane - MLA_NOPE_DIM) % AXIS_ROPE_DIM) < (AXIS_ROPE_DIM // 2)
    half = AXIS_ROPE_DIM // 2

    def finish(x, gain):
        x = x * lax.rsqrt(_split_dot(x * x, hsum) * (1.0 / MLA_QK_DIM) + NORM_EPS) * gain
        partner = jnp.where(first, pltpu.roll(x, MLA_PAD_WIDTH - half, 1), pltpu.roll(x, half, 1))
        return x * cos + partner * sin

    q_ref[0] = (finish(q, qg_ref[...]) * (MLA_QK_DIM ** -0.5)).astype(bf)
    k_ref[0] = finish(k, kg_ref[...]).astype(bf)


def mla_prep(p, use_rope, q_norm, w_uq, kv_norm, w_ukv, q_gain, k_gain):
    B, L, _ = p.shape
    T = min(MLA_PREP_ROWS, L)
    H = MLA_HEADS
    pad_cols = lambda w, d: jnp.pad(w.reshape(w.shape[0], H, d), ((0, 0), (0, 0), (0, LANE - d))).reshape(w.shape[0], H * LANE)
    wq = pad_cols(w_uq, MLA_QK_DIM).astype(jnp.bfloat16)
    ukv = w_ukv.reshape(MLA_KV_RANK, H, MLA_NOPE_DIM + MLA_V_DIM)
    wk = pad_cols(ukv[:, :, :MLA_NOPE_DIM].reshape(MLA_KV_RANK, H * MLA_NOPE_DIM), MLA_NOPE_DIM).astype(jnp.bfloat16)
    wv = ukv[:, :, MLA_NOPE_DIM:].reshape(MLA_KV_RANK, H * MLA_V_DIM).astype(jnp.bfloat16)
    lane = np.arange(H * LANE)
    place = jnp.asarray(((lane[None, :] % LANE) - MLA_NOPE_DIM == np.arange(MLA_ROPE_DIM)[:, None]), jnp.bfloat16)
    hsum = jnp.asarray((lane[:, None] // LANE) == (lane[None, :] // LANE), jnp.bfloat16)
    pad_gain = lambda g: jnp.tile(jnp.pad(g, (0, LANE - MLA_QK_DIM)), H).reshape(1, H * LANE)
    cos, sin = _rope_tables(L, use_rope)
    const = lambda a: pl.BlockSpec(a.shape, lambda b, i: (0,) * a.ndim)
    args = (p, q_norm.reshape(1, -1), wq, kv_norm.reshape(1, -1), wk, wv, place, pad_gain(q_gain), pad_gain(k_gain), hsum)
    pos = pl.BlockSpec((T, H * LANE), lambda b, i: (i, 0))
    return pl.pallas_call(
        _mla_prep_kernel,
        grid=(B, L // T),
        in_specs=[pl.BlockSpec((1, T, MLA_PROJ), lambda b, i: (b, i, 0))] + [const(a) for a in args[1:]] + [pos, pos],
        out_specs=[pl.BlockSpec((1, T, H * LANE), lambda b, i: (b, i, 0)), pl.BlockSpec((1, T, H * LANE), lambda b, i: (b, i, 0)),
                   pl.BlockSpec((1, T, MLA_WIDTH), lambda b, i: (b, i, 0))],
        out_shape=[jax.ShapeDtypeStruct((B, L, H * LANE), jnp.bfloat16), jax.ShapeDtypeStruct((B, L, H * LANE), jnp.bfloat16),
                   jax.ShapeDtypeStruct((B, L, MLA_WIDTH), jnp.bfloat16)],
        compiler_params=pltpu.CompilerParams(dimension_semantics=("parallel", "parallel"),
                                             vmem_limit_bytes=VMEM_LIMIT_BYTES),
        name="mla_prep",
    )(*args, cos, sin)


def _attn_kernel(q_ref, k_ref, v_ref, o_ref):
    lane = lax.broadcasted_iota(jnp.int32, (q_ref.shape[1], LANE), 1)
    for pair in range(MLA_HEADS // 2):
        v_pair = v_ref[0, :, pair * LANE:(pair + 1) * LANE]
        outs = []
        for h in (2 * pair, 2 * pair + 1):
            q = q_ref[0, :, h * LANE:(h + 1) * LANE]
            k = k_ref[0, :, h * LANE:(h + 1) * LANE]
            s = lax.dot_general(q, k, (((1,), (1,)), ((), ())), preferred_element_type=jnp.float32)
            e = jnp.exp(s - jnp.max(s, axis=-1, keepdims=True))
            o = jnp.dot(e.astype(jnp.bfloat16), v_pair, preferred_element_type=jnp.float32)
            outs.append(o / jnp.sum(e, axis=-1, keepdims=True))
        o_ref[0, :, pair * LANE:(pair + 1) * LANE] = jnp.where(lane < MLA_V_DIM, outs[0], outs[1])


def attention(q, k, v):
    B, Lq, P = q.shape
    Lk = k.shape[1]
    tq = min(ATTN_Q_ROWS, Lq)
    return pl.pallas_call(
        _attn_kernel,
        grid=(B, Lq // tq),
        in_specs=[pl.BlockSpec((1, tq, P), lambda b, i: (b, i, 0)),
                  pl.BlockSpec((1, Lk, P), lambda b, i: (b, 0, 0)),
                  pl.BlockSpec((1, Lk, MLA_WIDTH), lambda b, i: (b, 0, 0))],
        out_specs=pl.BlockSpec((1, tq, MLA_WIDTH), lambda b, i: (b, i, 0)),
        out_shape=jax.ShapeDtypeStruct((B, Lq, MLA_WIDTH), jnp.float32),
        compiler_params=pltpu.CompilerParams(dimension_semantics=("parallel", "parallel"),
                                             vmem_limit_bytes=VMEM_LIMIT_BYTES),
        name="mla_attention",
    )(q, k, v)


def mla_mixer(p_lat, p_ctx, q_norm, w_uq, kv_norm, w_ukv, q_gain, k_gain, need_ctx):
    prm = (q_norm, w_uq, kv_norm, w_ukv, q_gain, k_gain)
    q_l, k_l, v_l = mla_prep(p_lat, True, *prm)
    q_c, k_c, v_c = mla_prep(p_ctx, False, *prm)
    y_l = attention(q_l, jnp.concatenate([k_l, k_c], axis=1), jnp.concatenate([v_l, v_c], axis=1))
    y_c = attention(q_c, k_c, v_c) if need_ctx else None
    return y_l, y_c


HY_FILTER_ROWS = 256


def _hyena_filter_kernel(z_ref, tn_ref, w1_ref, b1_ref, f1_ref, w2_ref, b2_ref, f2_ref, w3_ref, b3_ref, rates_ref, o_ref):
    hp = lax.Precision.HIGHEST
    h = jnp.sin(f1_ref[...] * (jnp.dot(z_ref[...], w1_ref[...], precision=hp, preferred_element_type=jnp.float32) + b1_ref[...]))
    h = jnp.sin(f2_ref[...] * (jnp.dot(h, w2_ref[...], precision=hp, preferred_element_type=jnp.float32) + b2_ref[...]))
    h = jnp.dot(h, w3_ref[...], precision=hp, preferred_element_type=jnp.float32) + b3_ref[...]
    o_ref[...] = h * jnp.exp(-tn_ref[...] * rates_ref[...])


def hyena_filters(L, w1, b1, freq1, w2, b2, freq2, w3, b3):
    tn = jnp.arange(L, dtype=jnp.float32) / L
    bands = jnp.arange(1, HY_POS_BANDS + 1, dtype=jnp.float32)
    ang = 2.0 * math.pi * tn[:, None] * bands[None, :]
    z = jnp.concatenate([tn[:, None], jnp.cos(ang), jnp.sin(ang)], axis=-1)
    rates = jnp.abs(jnp.linspace(math.log(HY_DECAY_TARGET) / HY_LONG_DECAY_PCT,
                                 math.log(HY_DECAY_TARGET) / HY_SHORT_DECAY_PCT, HY_WIDTH))
    pad = -z.shape[1] % 8
    T = min(HY_FILTER_ROWS, L)
    n_out = w3.shape[1]
    row = lambda a: a.reshape(1, -1)
    const = lambda a: pl.BlockSpec(a.shape, lambda i: (0,) * a.ndim)
    args = (jnp.pad(w1, ((0, pad), (0, 0))), row(b1), row(freq1), w2, row(b2), row(freq2), w3, row(b3),
            row(jnp.tile(rates, n_out // HY_WIDTH)))
    h = pl.pallas_call(
        _hyena_filter_kernel,
        grid=(L // T,),
        in_specs=[pl.BlockSpec((T, z.shape[1] + pad), lambda i: (i, 0)), pl.BlockSpec((T, 1), lambda i: (i, 0))]
                 + [const(a) for a in args],
        out_specs=pl.BlockSpec((T, n_out), lambda i: (i, 0)),
        out_shape=jax.ShapeDtypeStruct((L, n_out), jnp.float32),
        compiler_params=pltpu.CompilerParams(dimension_semantics=("parallel",)),
        name="hyena_filter_mlp",
    )(jnp.pad(z, ((0, 0), (0, pad))), tn[:, None], *args)
    h = h.reshape(L, HY_ORDER, 2, HY_WIDTH)
    zero = jnp.zeros((1, HY_ORDER, HY_WIDTH), h.dtype)
    h_full = jnp.concatenate([h[:, :, 0], zero, h[:0:-1, :, 1]], axis=0)
    return h_full * lax.rsqrt(jnp.sum(jnp.square(h_full), axis=0, keepdims=True))


def fft_long_conv(u, h_full, bias):
    L = u.shape[1]
    uf = jnp.fft.rfft(u, n=2 * L, axis=1)
    hf = jnp.fft.rfft(h_full, n=2 * L, axis=0)
    y = jnp.fft.irfft(uf * hf[None], n=2 * L, axis=1)[:, :L]
    return y + u * bias


FFT_N1 = 64
FFT_N2 = 128
FFT_N = FFT_N1 * FFT_N2
HY_SEQS = 32


def _dft_tables(seqs):
    n1 = np.arange(FFT_N1)
    n2 = np.arange(FFT_N2)
    f64 = np.exp(-2j * np.pi * np.outer(n1, n1) / FFT_N1)
    f128 = np.exp(-2j * np.pi * np.outer(n2, n2) / FFT_N2)
    tw = np.exp(-2j * np.pi * np.outer(n1, n2) / FFT_N)
    half = FFT_N1 // 2
    fh = f64[:, :half]
    m1 = np.block([[fh.real, -fh.imag], [fh.imag, fh.real]])
    m1f = np.concatenate([f64.real, f64.imag], axis=0)
    m2 = np.block([[f128.real, f128.imag], [-f128.imag, f128.real]])
    m3 = np.block([[f128.real, -f128.imag], [f128.imag, f128.real]]) / FFT_N
    c = np.conj(f64)[:half, :]
    m4 = np.block([[c.real, -c.imag], [c.imag, c.real]])
    bf = lambda a: jnp.asarray(a, jnp.float32).astype(jnp.bfloat16)
    f32 = lambda a: jnp.asarray(a, jnp.float32)
    return dict(m1=bf(m1), m1f=bf(m1f), m2=bf(m2), m3=bf(m3), m4=bf(m4),
                twr_l=f32(np.tile(tw.real, (1, seqs))), twi_l=f32(np.tile(tw.imag, (1, seqs))),
                twr_s=f32(np.tile(tw.real, (seqs, 1))), twi_s=f32(np.tile(tw.imag, (seqs, 1))))


def _spectrum(cols, m1, twr_l, twi_l, m2, R):
    a = jnp.dot(m1, cols.astype(jnp.bfloat16), preferred_element_type=jnp.float32)
    ar, ai = a[:FFT_N1], a[FFT_N1:]
    pr = ar * twr_l - ai * twi_l
    pi = ar * twi_l + ai * twr_l
    lhs = jnp.concatenate(
        [jnp.concatenate([pr[:, r * FFT_N2:(r + 1) * FFT_N2], pi[:, r * FFT_N2:(r + 1) * FFT_N2]], axis=1)
         for r in range(R)], axis=0)
    return jnp.dot(lhs.astype(jnp.bfloat16), m2, preferred_element_type=jnp.float32)


def _filter_fft_kernel(h_ref, m1f_ref, twr_ref, twi_ref, m2_ref, o_ref):
    R = HY_SEQS
    cols = jnp.concatenate([h_ref[r] for r in range(R)], axis=1)
    x = _spectrum(cols, m1f_ref[...], twr_ref[...], twi_ref[...], m2_ref[...], R)
    o_ref[...] = x.reshape(R, FFT_N1, 2 * FFT_N2)


def _hyena_conv_kernel(y_ref, g_ref, hf_ref, bias_ref, m1_ref, twr_l_ref, twi_l_ref, m2_ref, m3_ref,
                       twr_s_ref, twi_s_ref, m4_ref, o_ref):
    R = HY_SEQS
    half = FFT_N1 // 2
    y = [y_ref[0], y_ref[1]]
    for o in range(HY_ORDER):
        top = jnp.concatenate([y[0][r] for r in range(R)], axis=1)
        bot = jnp.concatenate([y[1][r] for r in range(R)], axis=1)
        x = _spectrum(jnp.concatenate([top, bot], axis=0), m1_ref[...], twr_l_ref[...], twi_l_ref[...], m2_ref[...], R)
        hf = hf_ref[o].reshape(R * FFT_N1, 2 * FFT_N2)
        xr, xi = x[:, :FFT_N2], x[:, FFT_N2:]
        hr, hi = hf[:, :FFT_N2], hf[:, FFT_N2:]
        yc = jnp.concatenate([xr * hr - xi * hi, xr * hi + xi * hr], axis=1)
        b = jnp.dot(yc.astype(jnp.bfloat16), m3_ref[...], preferred_element_type=jnp.float32)
        br, bi = b[:, :FFT_N2], b[:, FFT_N2:]
        qr = br * twr_s_ref[...] + bi * twi_s_ref[...]
        qi = bi * twr_s_ref[...] - br * twi_s_ref[...]
        bc = jnp.concatenate(
            [jnp.concatenate([qr[r * FFT_N1:(r + 1) * FFT_N1], qi[r * FFT_N1:(r + 1) * FFT_N1]], axis=0)
             for r in range(R)], axis=1)
        yo = jnp.dot(m4_ref[...], bc.astype(jnp.bfloat16), preferred_element_type=jnp.float32)
        for p in range(2):
            conv = jnp.stack([yo[p * half:(p + 1) * half, r * FFT_N2:(r + 1) * FFT_N2] for r in range(R)], axis=0)
            y[p] = g_ref[o, p] * (conv + y[p] * bias_ref[o])
    o_ref[0] = y[0]
    o_ref[1] = y[1]


def hyena_long_conv(y_t, g_t, h_t, bias):
    B, C, L = y_t.shape
    assert 2 * L == FFT_N and B % 2 == 0 and C % HY_SEQS == 0
    R = HY_SEQS
    half = FFT_N1 // 2
    tb = _dft_tables(R)
    const = lambda a: pl.BlockSpec(a.shape, lambda *_: (0,) * a.ndim)
    hf = pl.pallas_call(
        _filter_fft_kernel,
        grid=(HY_ORDER * C // R,),
        in_specs=[pl.BlockSpec((R, FFT_N1, FFT_N2), lambda i: (i, 0, 0)),
                  const(tb['m1f']), const(tb['twr_l']), const(tb['twi_l']), const(tb['m2'])],
        out_specs=pl.BlockSpec((R, FFT_N1, 2 * FFT_N2), lambda i: (i, 0, 0)),
        out_shape=jax.ShapeDtypeStruct((HY_ORDER * C, FFT_N1, 2 * FFT_N2), jnp.float32),
        compiler_params=pltpu.CompilerParams(dimension_semantics=("parallel",), vmem_limit_bytes=VMEM_LIMIT_BYTES),
        name="hyena_filter_fft",
    )(h_t.reshape(HY_ORDER * C, FFT_N1, FFT_N2), tb['m1f'], tb['twr_l'], tb['twi_l'], tb['m2'])
    hf = hf.reshape(HY_ORDER, C, FFT_N1, 2 * FFT_N2)
    out = pl.pallas_call(
        _hyena_conv_kernel,
        grid=(B // 2, C // R),
        in_specs=[pl.BlockSpec((2, R, half, FFT_N2), lambda b, c: (b, c, 0, 0)),
                  pl.BlockSpec((HY_ORDER, 2, R, half, FFT_N2), lambda b, c: (0, b, c, 0, 0)),
                  pl.BlockSpec((HY_ORDER, R, FFT_N1, 2 * FFT_N2), lambda b, c: (0, c, 0, 0)),
                  pl.BlockSpec((HY_ORDER, R, 1, 1), lambda b, c: (0, c, 0, 0)),
                  const(tb['m1']), const(tb['twr_l']), const(tb['twi_l']), const(tb['m2']), const(tb['m3']),
                  const(tb['twr_s']), const(tb['twi_s']), const(tb['m4'])],
        out_specs=pl.BlockSpec((2, R, half, FFT_N2), lambda b, c: (b, c, 0, 0)),
        out_shape=jax.ShapeDtypeStruct((B, C, half, FFT_N2), jnp.float32),
        compiler_params=pltpu.CompilerParams(dimension_semantics=("parallel", "parallel"),
                                             vmem_limit_bytes=VMEM_LIMIT_BYTES),
        name="hyena_conv",
    )(y_t.reshape(B, C, half, FFT_N2), g_t.reshape(HY_ORDER, B, C, half, FFT_N2), hf,
      bias.reshape(HY_ORDER, C, 1, 1), tb['m1'], tb['twr_l'], tb['twi_l'], tb['m2'], tb['m3'],
      tb['twr_s'], tb['twi_s'], tb['m4'])
    return out.reshape(B, C, L)


def hyena_mixer(p, conv_w, w1, b1, freq1, w2, b2, freq2, w3, b3, bias):
    B, L = p.shape[:2]
    z = short_conv(p, conv_w)
    h_full = hyena_filters(L, w1, b1, freq1, w2, b2, freq2, w3, b3)
    if 2 * L == FFT_N:
        g_t = jnp.transpose(z[..., :HY_ORDER * HY_WIDTH].reshape(B, L, HY_ORDER, HY_WIDTH), (2, 0, 3, 1))
        y_t = jnp.swapaxes(z[..., HY_ORDER * HY_WIDTH:], 1, 2)
        y_t = hyena_long_conv(y_t, g_t, jnp.transpose(h_full, (1, 2, 0)), bias)
        return jnp.swapaxes(y_t, 1, 2)
    gates = (z[..., :HY_WIDTH], z[..., HY_WIDTH:2 * HY_WIDTH])
    y = z[..., 2 * HY_WIDTH:]
    for o in range(HY_ORDER):
        y = gates[o] * fft_long_conv(y, h_full[:, o], bias[o])
    return y


SC_CORES = 2
SC_SUBCORES = 16
SC_LANES = 16
SC_WORKERS = SC_CORES * SC_SUBCORES
PEER_SLOTS = PEER_HEADS * PEER_TOPK
PEER_GATHER_ROWS = 32
PEER_GATHERS = PEER_SLOTS // PEER_GATHER_ROWS
PEER_ACC_VREGS = 8
PEER_ROW_BUFFERS = 4
PEER_ROW_WORDS = D_MODEL // 2
HI_MASK = -65536


def pack_expert_table(t):
    b = lax.bitcast_convert_type(t.astype(jnp.bfloat16), jnp.uint16).astype(jnp.uint32)
    return lax.bitcast_convert_type(b[:, :PEER_ROW_WORDS] | (b[:, PEER_ROW_WORDS:] << 16), jnp.int32)


def _sc_peer_phase(phase, tpw):
    NBUF = PEER_ROW_BUFFERS
    AHEAD = NBUF - 1
    HW = PEER_ROW_WORDS

    def run(base, table_hbm, idx_hbm, aux_hbm, out_hbm, idx_v, aux_v, rows_v, out_v, sem_r, sem_i, sem_o):

        def gather(p, c, b):
            return pltpu.make_async_copy(table_hbm.at[idx_v.at[p, c]], rows_v.at[b], sem_r.at[b])

        def load_meta(t, p):
            return (pltpu.make_async_copy(idx_hbm.at[t], idx_v.at[p], sem_i.at[p]),
                    pltpu.make_async_copy(aux_hbm.at[t], aux_v.at[p], sem_i.at[p]))

        def store_out(t, p):
            return pltpu.make_async_copy(out_v.at[p], out_hbm.at[t], sem_o.at[p])

        def halves(word):
            return (plsc.bitcast(lax.shift_left(word, 16), jnp.float32), plsc.bitcast(word & HI_MASK, jnp.float32))

        def compute(p, c, b):
            if phase == "dot":
                lane = lax.iota(jnp.int32, SC_LANES)
                vec = jnp.zeros((SC_LANES,), jnp.float32)
                groups_per_vec = SC_LANES // PEER_ACC_VREGS
                for g in range(PEER_GATHER_ROWS // PEER_ACC_VREGS):
                    def body(cc, accs):
                        x_lo = aux_v[p, pl.ds(cc * SC_LANES, SC_LANES)]
                        x_hi = aux_v[p, pl.ds(HW + cc * SC_LANES, SC_LANES)]
                        out = []
                        for r in range(PEER_ACC_VREGS):
                            lo, hi = halves(rows_v[b, g * PEER_ACC_VREGS + r, pl.ds(cc * SC_LANES, SC_LANES)])
                            out.append(accs[r] + lo * x_lo + hi * x_hi)
                        return tuple(out)
                    accs = lax.fori_loop(0, HW // SC_LANES, body,
                                         tuple(jnp.zeros((SC_LANES,), jnp.float32) for _ in range(PEER_ACC_VREGS)))
                    for r in range(PEER_ACC_VREGS):
                        vec = jnp.where(lane == (g % groups_per_vec) * PEER_ACC_VREGS + r, jnp.sum(accs[r]), vec)
                    if g % groups_per_vec == groups_per_vec - 1:
                        out_v[p, pl.ds(c * PEER_GATHER_ROWS + (g // groups_per_vec) * SC_LANES, SC_LANES)] = vec
            else:
                words = PEER_ACC_VREGS // 2
                for db in range(HW // (words * SC_LANES)):
                    def body(kk, accs):
                        wv = plsc.load_gather(aux_v.at[p], [jnp.full((SC_LANES,), c * PEER_GATHER_ROWS + kk, jnp.int32)])
                        out = []
                        for j in range(words):
                            lo, hi = halves(rows_v[b, kk, pl.ds((db * words + j) * SC_LANES, SC_LANES)])
                            out += [accs[2 * j] + lo * wv, accs[2 * j + 1] + hi * wv]
                        return tuple(out)
                    if c == 0:
                        init = tuple(jnp.zeros((SC_LANES,), jnp.float32) for _ in range(2 * words))
                    else:
                        init = tuple(out_v[p, pl.ds(half * HW + (db * words + j) * SC_LANES, SC_LANES)]
                                     for j in range(words) for half in range(2))
                    accs = lax.fori_loop(0, PEER_GATHER_ROWS, body, init)
                    for j in range(words):
                        out_v[p, pl.ds((db * words + j) * SC_LANES, SC_LANES)] = accs[2 * j]
                        out_v[p, pl.ds(HW + (db * words + j) * SC_LANES, SC_LANES)] = accs[2 * j + 1]

        for d in load_meta(base, 0):
            d.start()
        for d in load_meta(base, 0):
            d.wait()
        for c in range(AHEAD):
            gather(0, c, c % NBUF).start()

        @pl.loop(0, tpw // 2)
        def _(i2):
            for p in range(2):
                i = i2 * 2 + p
                t = base + i
                nxt = base + jnp.minimum(i + 1, tpw - 1)
                for d in load_meta(nxt, 1 - p):
                    d.start()

                @pl.when(i2 > 0)
                def _():
                    store_out(t, p).wait()

                for c in range(PEER_GATHERS):
                    ahead = c + AHEAD
                    if ahead < PEER_GATHERS:
                        gather(p, ahead, ahead % NBUF).start()
                    else:
                        if ahead == PEER_GATHERS:
                            for d in load_meta(nxt, 1 - p):
                                d.wait()
                        gather(1 - p, ahead - PEER_GATHERS, ahead % NBUF).start()
                    gather(p, c, c % NBUF).wait()
                    compute(p, c, c % NBUF)
                store_out(t, p).start()

        for c in range(AHEAD):
            gather(0, c, c % NBUF).wait()
        for p in range(2):
            store_out(base, p).wait()

    return run


def _sc_tokens_per_worker(N):
    assert N % (2 * SC_WORKERS) == 0 and PEER_GATHERS % PEER_ROW_BUFFERS == 0
    return N // SC_WORKERS


_SC_AUX = {"dot": (D_MODEL,), "wsum": (PEER_SLOTS,)}
_SC_OUT = {"dot": (PEER_SLOTS,), "wsum": (D_MODEL,)}


def _sc_scratch(phases):
    s = [pltpu.VMEM((2, PEER_GATHERS, PEER_GATHER_ROWS), jnp.int32),
         pltpu.VMEM((PEER_ROW_BUFFERS, PEER_GATHER_ROWS, PEER_ROW_WORDS), jnp.int32),
         pltpu.SemaphoreType.DMA((PEER_ROW_BUFFERS,)), pltpu.SemaphoreType.DMA((2,)), pltpu.SemaphoreType.DMA((2,))]
    for ph in phases:
        s += [pltpu.VMEM((2,) + _SC_AUX[ph], jnp.float32), pltpu.VMEM((2,) + _SC_OUT[ph], jnp.float32)]
    return s


def _sc_peer_jobs(jobs):
    phases = [j[0] for j in jobs]
    kinds = sorted(set(phases))
    ns = [j[2].shape[0] for j in jobs]
    runs = [_sc_peer_phase(ph, _sc_tokens_per_worker(n)) for ph, n in zip(phases, ns)]
    nj = len(jobs)

    @functools.partial(
        pl.kernel, mesh=plsc.VectorSubcoreMesh(core_axis_name="c", subcore_axis_name="s"),
        out_type=tuple(jax.ShapeDtypeStruct((n,) + _SC_OUT[ph], jnp.float32) for ph, n in zip(phases, ns)),
        compiler_params=pltpu.CompilerParams(needs_layout_passes=False),
        scratch_types=_sc_scratch(kinds),
    )
    def k(*refs):
        ins, outs, scratch = refs[:3 * nj], refs[3 * nj:4 * nj], refs[4 * nj:]
        idx_v, rows_v, sem_r, sem_i, sem_o = scratch[:5]
        bufs = {kind: scratch[5 + 2 * i:7 + 2 * i] for i, kind in enumerate(kinds)}
        worker = lax.axis_index("s") * SC_CORES + lax.axis_index("c")
        for j in range(nj):
            table_hbm, idx_hbm, aux_hbm = ins[3 * j:3 * j + 3]
            aux_v, out_v = bufs[phases[j]]
            runs[j](worker * (ns[j] // SC_WORKERS), table_hbm, idx_hbm, aux_hbm, outs[j],
                    idx_v, aux_v, rows_v, out_v, sem_r, sem_i, sem_o)

    args = []
    for (_, table, idx, aux), n in zip(jobs, ns):
        args += [table, idx.reshape(n, PEER_GATHERS, PEER_GATHER_ROWS), aux]
    return list(k(*args))


PEER_TOKENS = 256
INT_BIG = 2 ** 30
PEER_CANDIDATES = -(-sum(PEER_TOPK // (i + 1) for i in range(PEER_TOPK)) // 8) * 8


def _extract_topk(cand_ref, ids_ref, val_out_ref, id_out_ref, row0):
    def body(r, carry):
        c = cand_ref[...]
        ids = ids_ref[...]
        m = jnp.max(c, axis=0, keepdims=True)
        sel = jnp.min(jnp.where(c == m, ids, INT_BIG), axis=0, keepdims=True)
        cand_ref[...] = jnp.where(ids == sel, -jnp.inf, c)
        val_out_ref[pl.ds(row0 + r, 1), :] = m
        id_out_ref[pl.ds(row0 + r, 1), :] = sel
        return carry
    lax.fori_loop(0, PEER_TOPK, body, 0)


def _peer_retrieve_kernel(x_ref, gain_ref, scale_ref, shift_ref, wq_ref, keys_ref,
                          h_ref, idx_out_ref, gate_out_ref,
                          s_ref, ids1_ref, sv_ref, si_ref, cand_ref, cid_ref, ts_ref, idx_ref, gate_ref):
    x = x_ref[0]
    y = x * lax.rsqrt(jnp.mean(x * x, axis=-1, keepdims=True) + NORM_EPS)
    h = (y * gain_ref[...]) * (1.0 + scale_ref[0]) + shift_ref[0]
    h_ref[0] = h
    q = jnp.dot(h.astype(jnp.bfloat16), wq_ref[...], preferred_element_type=jnp.float32)
    T = PEER_TOKENS
    K = PEER_TOPK
    ids1_ref[...] = lax.broadcasted_iota(jnp.int32, (PEER_N_KEYS, T), 0)
    for hd in range(PEER_HEADS):
        for p in range(2):
            hp = hd * 2 + p
            qs = q[:, hp * PEER_HALF:(hp + 1) * PEER_HALF].astype(jnp.bfloat16)
            s_ref[...] = lax.dot_general(keys_ref[hp], qs, (((1,), (1,)), ((), ())),
                                         preferred_element_type=jnp.float32)
            _extract_topk(s_ref, ids1_ref, sv_ref, si_ref, p * K)
        cand_ref[...] = jnp.full(cand_ref.shape, -jnp.inf, jnp.float32)
        cid_ref[...] = INT_BIG - 1 - lax.broadcasted_iota(jnp.int32, cid_ref.shape, 0)
        off = 0
        for i in range(K):
            n = K // (i + 1)
            cand_ref[off:off + n, :] = sv_ref[i:i + 1, :] + sv_ref[K:K + n, :]
            cid_ref[off:off + n, :] = si_ref[i:i + 1, :] * PEER_N_KEYS + si_ref[K:K + n, :]
            off += n
        _extract_topk(cand_ref, cid_ref, ts_ref, idx_ref, hd * K)
        ts = ts_ref[hd * K:(hd + 1) * K, :]
        e = jnp.exp(ts - jnp.max(ts, axis=0, keepdims=True))
        gate_ref[hd * K:(hd + 1) * K, :] = e / jnp.sum(e, axis=0, keepdims=True)
    idx_out_ref[...] = idx_ref[...].T
    gate_out_ref[...] = gate_ref[...].T


def peer_retrieve(x, gain, scale, shift, w_q, sub_keys):
    B, L, D = x.shape
    T = PEER_TOKENS
    nt = L // T
    keys = sub_keys.reshape(PEER_HEADS * 2, PEER_N_KEYS, PEER_HALF).astype(jnp.bfloat16)
    return pl.pallas_call(
        _peer_retrieve_kernel,
        grid=(B, nt),
        in_specs=[
            pl.BlockSpec((1, T, D), lambda b, i: (b, i, 0)),
            pl.BlockSpec((1, D), lambda b, i: (0, 0)),
            pl.BlockSpec((1, 1, D), lambda b, i: (b, 0, 0)),
            pl.BlockSpec((1, 1, D), lambda b, i: (b, 0, 0)),
            pl.BlockSpec((D, PEER_HEADS * 2 * PEER_HALF), lambda b, i: (0, 0)),
            pl.BlockSpec((PEER_HEADS * 2, PEER_N_KEYS, PEER_HALF), lambda b, i: (0, 0, 0)),
        ],
        out_specs=[
            pl.BlockSpec((1, T, D), lambda b, i: (b, i, 0)),
            pl.BlockSpec((T, PEER_SLOTS), lambda b, i: (b * nt + i, 0)),
            pl.BlockSpec((T, PEER_SLOTS), lambda b, i: (b * nt + i, 0)),
        ],
        out_shape=[
            jax.ShapeDtypeStruct((B, L, D), jnp.float32),
            jax.ShapeDtypeStruct((B * L, PEER_SLOTS), jnp.int32),
            jax.ShapeDtypeStruct((B * L, PEER_SLOTS), jnp.float32),
        ],
        scratch_shapes=[
            pltpu.VMEM((PEER_N_KEYS, T), jnp.float32),
            pltpu.VMEM((PEER_N_KEYS, T), jnp.int32),
            pltpu.VMEM((2 * PEER_TOPK, T), jnp.float32),
            pltpu.VMEM((2 * PEER_TOPK, T), jnp.int32),
            pltpu.VMEM((PEER_CANDIDATES, T), jnp.float32),
            pltpu.VMEM((PEER_CANDIDATES, T), jnp.int32),
            pltpu.VMEM((PEER_SLOTS, T), jnp.float32),
            pltpu.VMEM((PEER_SLOTS, T), jnp.int32),
            pltpu.VMEM((PEER_SLOTS, T), jnp.float32),
        ],
        compiler_params=pltpu.CompilerParams(dimension_semantics=("parallel", "parallel"),
                                             vmem_limit_bytes=VMEM_LIMIT_BYTES),
        name="peer_retrieve",
    )(x, gain.reshape(1, D), scale, shift, w_q.astype(jnp.bfloat16), keys)


PEER_ACT_ROWS = 256


def _peer_act_kernel(dots_ref, gate_ref, w_ref):
    a = dots_ref[...]
    w_ref[...] = gate_ref[...] * (0.5 * a * (1.0 + lax.erf(a * (2.0 ** -0.5))))


def peer_act(dots, gate):
    N = dots.shape[0]
    T = min(PEER_ACT_ROWS, N)
    spec = pl.BlockSpec((T, PEER_SLOTS), lambda i: (i, 0))
    return pl.pallas_call(
        _peer_act_kernel,
        grid=(N // T,),
        in_specs=[spec, spec],
        out_specs=spec,
        out_shape=jax.ShapeDtypeStruct((N, PEER_SLOTS), jnp.float32),
        compiler_params=pltpu.CompilerParams(dimension_semantics=("parallel",)),
        name="peer_act",
    )(dots, gate)


OUT_PROJ_ROWS = 512


def _out_proj_kernel(x_ref, g_ref, rw_ref, ml_ref, hy_ref, w_ref, o_ref):
    bf = jnp.bfloat16
    y = jnp.dot(rw_ref[0].astype(bf), w_ref[:RW_WIDTH, :], preferred_element_type=jnp.float32)
    y += jnp.dot(ml_ref[0].astype(bf), w_ref[RW_WIDTH:RW_WIDTH + MLA_WIDTH, :], preferred_element_type=jnp.float32)
    y += jnp.dot(hy_ref[0].astype(bf), w_ref[RW_WIDTH + MLA_WIDTH:, :], preferred_element_type=jnp.float32)
    o_ref[0] = x_ref[0] + g_ref[0] * y


def mix_out_proj(x, gate, rw, ml, hy, w_out):
    B, L, D = x.shape
    T = min(OUT_PROJ_ROWS, L)
    tok = lambda w: pl.BlockSpec((1, T, w), lambda b, i: (b, i, 0))
    return pl.pallas_call(
        _out_proj_kernel,
        grid=(B, L // T),
        in_specs=[tok(D), pl.BlockSpec((1, 1, D), lambda b, i: (b, 0, 0)), tok(RW_WIDTH), tok(MLA_WIDTH), tok(HY_WIDTH),
                  pl.BlockSpec((MIX_WIDTH, D), lambda b, i: (0, 0))],
        out_specs=tok(D),
        out_shape=jax.ShapeDtypeStruct((B, L, D), jnp.float32),
        compiler_params=pltpu.CompilerParams(dimension_semantics=("parallel", "parallel"),
                                             vmem_limit_bytes=VMEM_LIMIT_BYTES),
        name="mix_out_proj",
    )(x, gate, rw, ml, hy, w_out.astype(jnp.bfloat16))


def _mix_and_retrieve(li, x, c, ctx, c_ctx, mod_w, mod_b, mix_norm, w_in, w_out, rw_conv, rw_decay_up, rw_decay0, rw_a_up, rw_a0, rw_gate_up, rw_k_k, rw_k_a, rw_r_k, rw_gn_g, rw_gn_b, mla_q_norm, mla_w_uq, mla_kv_norm, mla_w_ukv, mla_q_gain, mla_k_gain, hy_conv, hy_w1, hy_b1, hy_freq1, hy_w2, hy_b2, hy_freq2, hy_w3, hy_b3, hy_bias, ffn_norm, peer_wq, peer_keys, peer_u, peer_v):
    B, L, D = x.shape
    need_ctx = li < DEPTH - 1
    mod_l = (jax.nn.silu(c) @ mod_w[li] + mod_b[li])[:, None, :]
    mod_c = (jax.nn.silu(c_ctx) @ mod_w[li] + mod_b[li])[None, None, :]
    shm_l, scm_l, gm_l, shf_l, scf_l, gf_l = jnp.split(mod_l, N_MOD, axis=-1)
    shm_c, scm_c, gm_c, shf_c, scf_c, gf_c = jnp.split(mod_c, N_MOD, axis=-1)

    widths = (RW_PROJ, MLA_PROJ, HY_PROJ)
    prw_l, pml_l, phy_l = norm_mod_proj(x, mix_norm[li], scm_l, shm_l, w_in[li], widths, 512)
    prw_c, pml_c, phy_c = norm_mod_proj(ctx, mix_norm[li], jnp.broadcast_to(scm_c, (B, 1, D)),
                                        jnp.broadcast_to(shm_c, (B, 1, D)), w_in[li], widths, 256)
    rw_l, rw_c = rwkv7_mixer(prw_l, prw_c, rw_conv[li], rw_decay_up[li], rw_decay0[li],
                             rw_a_up[li], rw_a0[li], rw_gate_up[li], rw_k_k[li], rw_k_a[li], rw_r_k[li],
                             rw_gn_g[li], rw_gn_b[li], need_ctx)
    ml_l, ml_c = mla_mixer(pml_l, pml_c, mla_q_norm[li], mla_w_uq[li],
                           mla_kv_norm[li], mla_w_ukv[li], mla_q_gain[li], mla_k_gain[li], need_ctx)
    hy_prm = (hy_conv[li], hy_w1[li], hy_b1[li], hy_freq1[li], hy_w2[li], hy_b2[li], hy_freq2[li],
              hy_w3[li], hy_b3[li], hy_bias[li])
    hy_l = hyena_mixer(phy_l, *hy_prm)
    x = mix_out_proj(x, gm_l, rw_l, ml_l, hy_l, w_out[li])
    h, e_idx, gate = peer_retrieve(x, ffn_norm[li], scf_l, shf_l, peer_wq[li], peer_keys[li])
    streams = [(x, gf_l, h.reshape(B * L, D), e_idx, gate)]
    if need_ctx:
        hy_c = hyena_mixer(phy_c, *hy_prm)
        ctx = mix_out_proj(ctx, jnp.broadcast_to(gm_c, (B, 1, D)), rw_c, ml_c, hy_c, w_out[li])
        h, e_idx, gate = peer_retrieve(ctx, ffn_norm[li], jnp.broadcast_to(scf_c, (B, 1, D)),
                                       jnp.broadcast_to(shf_c, (B, 1, D)), peer_wq[li], peer_keys[li])
        streams.append((ctx, gf_c, h.reshape(-1, D), e_idx, gate))
    return streams


BATCH_GROUP_ROWS = (2, 2, 2, 2)


def kernel(x, c, ctx, c_ctx, mod_w, mod_b, mix_norm, w_in, w_out, rw_conv, rw_decay_up, rw_decay0, rw_a_up, rw_a0, rw_gate_up, rw_k_k, rw_k_a, rw_r_k, rw_gn_g, rw_gn_b, mla_q_norm, mla_w_uq, mla_kv_norm, mla_w_ukv, mla_q_gain, mla_k_gain, hy_conv, hy_w1, hy_b1, hy_freq1, hy_w2, hy_b2, hy_freq2, hy_w3, hy_b3, hy_bias, ffn_norm, peer_wq, peer_keys, peer_u, peer_v):
    params = (mod_w, mod_b, mix_norm, w_in, w_out, rw_conv, rw_decay_up, rw_decay0, rw_a_up, rw_a0, rw_gate_up,
              rw_k_k, rw_k_a, rw_r_k, rw_gn_g, rw_gn_b, mla_q_norm, mla_w_uq, mla_kv_norm, mla_w_ukv, mla_q_gain,
              mla_k_gain, hy_conv, hy_w1, hy_b1, hy_freq1, hy_w2, hy_b2, hy_freq2, hy_w3, hy_b3, hy_bias,
              ffn_norm, peer_wq, peer_keys)
    peer_u = [pack_expert_table(peer_u[li]) for li in range(DEPTH)]
    peer_v = [pack_expert_table(peer_v[li]) for li in range(DEPTH)]
    params = params + (peer_u, peer_v)
    assert sum(BATCH_GROUP_ROWS) == x.shape[0]
    G = len(BATCH_GROUP_ROWS)
    lo = [sum(BATCH_GROUP_ROWS[:g]) for g in range(G + 1)]
    L, D = x.shape[1:]
    xs = [x[lo[g]:lo[g + 1]] for g in range(G)]
    cs = [c[lo[g]:lo[g + 1]] for g in range(G)]
    ctxs = [ctx[lo[g]:lo[g + 1]] for g in range(G)]
    stages = [(li, g) for li in range(DEPTH) for g in range(G)]
    prev = None
    token = None

    def advance(prev, token, li, streams):
        pstreams, pdots = ([], []) if prev is None else (prev[2], prev[3])
        if pdots:
            token, pdots = lax.optimization_barrier((token, pdots))
        dots = []
        for s in range(max(len(pstreams), len(streams))):
            jobs = []
            if s < len(pstreams):
                w = peer_act(pdots[s], pstreams[s][4])
                token = w if s == 0 else token
                jobs.append(("wsum", peer_v[prev[0]], pstreams[s][3], w))
            if s < len(streams):
                jobs.append(("dot", peer_u[li], streams[s][3], streams[s][2]))
            outs = _sc_peer_jobs(jobs)
            if s < len(pstreams):
                res = pstreams[s][0] + pstreams[s][1] * outs[0].reshape(pstreams[s][0].shape)
                (xs if s == 0 else ctxs)[prev[1]] = res
            if s < len(streams):
                dots.append(outs[-1])
        return token, dots

    for li, g in stages:
        ins = (xs[g], ctxs[g])
        if token is not None:
            token, ins = lax.optimization_barrier((token, ins))
        streams = _mix_and_retrieve(li, ins[0], cs[g], ins[1], c_ctx, *params)
        token, dots = advance(prev, streams[0][4], li, streams)
        prev = (li, g, streams, dots)
    advance(prev, token, None, [])
    return jnp.concatenate(xs, axis=0)
```

```python
import functools
import math

import jax
import jax.numpy as jnp
import numpy as np
from jax import lax
from jax.experimental import pallas as pl
from jax.experimental.pallas import tpu as pltpu
from jax.experimental.pallas import tpu_sc as plsc

D_MODEL = 1024
DEPTH = 2
GRID_W = 64
N_MOD = 6
NORM_EPS = 1e-6
SHORT_CONV = 3

RW_HEADS = 6
RW_HEAD_DIM = 64
RW_WIDTH = RW_HEADS * RW_HEAD_DIM
RW_DECAY_RANK = 64
RW_A_RANK = 64
RW_GATE_RANK = 128
RW_DECAY_SCALE = 0.6065306597
RW_GN_EPS = 64e-5
L2_EPS = 1e-12

MLA_HEADS = 6
MLA_Q_RANK = 256
MLA_KV_RANK = 128
MLA_NOPE_DIM = 64
MLA_ROPE_DIM = 32
MLA_V_DIM = 64
MLA_QK_DIM = MLA_NOPE_DIM + MLA_ROPE_DIM
MLA_WIDTH = MLA_HEADS * MLA_V_DIM
AXIS_ROPE_DIM = MLA_ROPE_DIM // 2
ROPE_THETA = 10000.0

HY_WIDTH = 256
HY_ORDER = 2
HY_POS_BANDS = 16
HY_SHORT_DECAY_PCT = 0.3
HY_LONG_DECAY_PCT = 1.5
HY_DECAY_TARGET = 1e-2

PEER_HEADS = 8
PEER_N_KEYS = 128
PEER_TOPK = 16
PEER_QUERY_DIM = 256
PEER_HALF = PEER_QUERY_DIM // 2

RW_PROJ = 3 * RW_WIDTH + RW_DECAY_RANK + RW_A_RANK + RW_GATE_RANK
MLA_PROJ = MLA_Q_RANK + MLA_KV_RANK + MLA_ROPE_DIM
HY_PROJ = (HY_ORDER + 1) * HY_WIDTH
MIX_WIDTH = RW_WIDTH + MLA_WIDTH + HY_WIDTH

VMEM_LIMIT_BYTES = 48 * 1024 * 1024


def _norm_mod_proj_kernel(x_ref, gain_ref, scale_ref, shift_ref, *refs):
    w_refs, o_refs = refs[:len(refs) // 2], refs[len(refs) // 2:]
    x = x_ref[0]
    y = x * lax.rsqrt(jnp.mean(x * x, axis=-1, keepdims=True) + NORM_EPS)
    y = y * gain_ref[...]
    y = (y * (1.0 + scale_ref[0]) + shift_ref[0]).astype(jnp.bfloat16)
    for w_ref, o_ref in zip(w_refs, o_refs):
        o_ref[0] = jnp.dot(y, w_ref[...], preferred_element_type=jnp.float32)


def norm_mod_proj(x, gain, scale, shift, w, widths, block_rows):
    B, L, D = x.shape
    assert sum(widths) == w.shape[1]
    offs = [sum(widths[:i]) for i in range(len(widths))]
    ws = [w[:, o:o + n].astype(jnp.bfloat16) for o, n in zip(offs, widths)]
    return pl.pallas_call(
        _norm_mod_proj_kernel,
        grid=(B, L // block_rows),
        in_specs=[
            pl.BlockSpec((1, block_rows, D), lambda b, i: (b, i, 0)),
            pl.BlockSpec((1, D), lambda b, i: (0, 0)),
            pl.BlockSpec((1, 1, D), lambda b, i: (b, 0, 0)),
            pl.BlockSpec((1, 1, D), lambda b, i: (b, 0, 0)),
        ] + [pl.BlockSpec((D, n), lambda b, i: (0, 0)) for n in widths],
        out_specs=[pl.BlockSpec((1, block_rows, n), lambda b, i: (b, i, 0)) for n in widths],
        out_shape=[jax.ShapeDtypeStruct((B, L, n), jnp.float32) for n in widths],
        compiler_params=pltpu.CompilerParams(
            dimension_semantics=("parallel", "parallel"), vmem_limit_bytes=VMEM_LIMIT_BYTES),
        name="norm_mod_proj",
    )(x, gain.reshape(1, D), scale, shift, *ws)


CONV_ROWS = 512
SUBLANES = 8


def _short_conv_kernel(x_ref, prev_ref, next_ref, w_ref, o_ref):
    i = pl.program_id(1)
    x = x_ref[0]
    T = x.shape[0]
    row = lax.broadcasted_iota(jnp.int32, x.shape, 0)
    before = jnp.where(i == 0, 0.0, prev_ref[0, SUBLANES - 1:SUBLANES, :])
    after = jnp.where(i == pl.num_programs(1) - 1, 0.0, next_ref[0, 0:1, :])
    up = jnp.where(row == 0, before, pltpu.roll(x, 1, 0))
    down = jnp.where(row == T - 1, after, pltpu.roll(x, T - 1, 0))
    o_ref[0] = up * w_ref[0:1, :] + x * w_ref[1:2, :] + down * w_ref[2:3, :]


def short_conv(x, w):
    B, L, C = x.shape
    T = min(CONV_ROWS, L)
    per = T // SUBLANES
    last = L // SUBLANES - 1
    return pl.pallas_call(
        _short_conv_kernel,
        grid=(B, L // T),
        in_specs=[pl.BlockSpec((1, T, C), lambda b, i: (b, i, 0)),
                  pl.BlockSpec((1, SUBLANES, C), lambda b, i: (b, jnp.maximum(i * per - 1, 0), 0)),
                  pl.BlockSpec((1, SUBLANES, C), lambda b, i: (b, jnp.minimum((i + 1) * per, last), 0)),
                  pl.BlockSpec((SHORT_CONV, C), lambda b, i: (0, 0))],
        out_specs=pl.BlockSpec((1, T, C), lambda b, i: (b, i, 0)),
        out_shape=jax.ShapeDtypeStruct((B, L, C), jnp.float32),
        compiler_params=pltpu.CompilerParams(dimension_semantics=("parallel", "parallel"),
                                             vmem_limit_bytes=VMEM_LIMIT_BYTES),
        name="short_conv",
    )(x, x, x, w)


RW_CHUNK = 64


def _rwkv_chunk_kernel(r_ref, kk_ref, v_ref, lw_ref, akk_ref, kr_ref, y_ref, h_ref):
    d = pl.program_id(0)
    n = pl.program_id(2)

    @pl.when(n == 0)
    def _():
        h_ref[...] = jnp.zeros_like(h_ref)

    C = RW_CHUNK
    row = lax.broadcasted_iota(jnp.int32, (C, C), 0)
    col = lax.broadcasted_iota(jnp.int32, (C, C), 1)
    lag = (row - col) * (1 - 2 * d)
    before = lag > 0
    upto = lag >= 0
    tri = upto.astype(jnp.float32)
    eye = (row == col).astype(jnp.float32)
    bf = jnp.bfloat16
    f32 = jnp.float32

    def mm(a, b):
        return jnp.dot(a.astype(bf), b.astype(bf), preferred_element_type=f32)

    def mm_nt(a, b):
        return lax.dot_general(a.astype(bf), b.astype(bf), (((1,), (1,)), ((), ())), preferred_element_type=f32)

    def mm_tn(a, b):
        return lax.dot_general(a.astype(bf), b.astype(bf), (((0,), (0,)), ((), ())), preferred_element_type=f32)

    hs = range(RW_HEADS)
    HD = RW_HEAD_DIM
    heads = lambda t: [t[:, h * HD:(h + 1) * HD] for h in hs]
    r = heads(r_ref[0])
    kk = heads(kk_ref[0])
    v = heads(v_ref[0])
    lw = heads(lw_ref[0, 0])
    akk = heads(akk_ref[0, 0])
    kr = heads(kr_ref[0, 0])
    G = [jnp.dot(tri, lw[h], preferred_element_type=f32, precision=lax.Precision.HIGHEST) for h in hs]
    gtot = [jnp.sum(lw[h], axis=0, keepdims=True) for h in hs]
    Einv = [jnp.exp(-G[h]) for h in hs]
    At = [-kk[h] * jnp.exp(G[h] - lw[h]) for h in hs]
    Rt = [r[h] * jnp.exp(G[h]) for h in hs]
    Bt = [akk[h] * Einv[h] for h in hs]
    Kt = [kr[h] * Einv[h] for h in hs]
    X = [mm_nt(jnp.concatenate([At[h], Rt[h]], axis=0), jnp.concatenate([Bt[h], Kt[h]], axis=0)) for h in hs]
    M_ab = [jnp.where(before, X[h][:C, :C], 0.0) for h in hs]
    M_ak = [jnp.where(before, X[h][:C, C:], 0.0) for h in hs]
    A_rb = [jnp.where(upto, X[h][C:, :C], 0.0) for h in hs]
    A_rk = [jnp.where(upto, X[h][C:, C:], 0.0) for h in hs]
    MV = [mm(M_ak[h], v[h]) for h in hs]
    Mp = M_ab
    T = [eye + Mp[h] for h in hs]
    for _ in range(5):
        Mp = [jnp.dot(Mp[h], Mp[h], preferred_element_type=f32) for h in hs]
        T = [T[h] + jnp.dot(T[h], Mp[h], preferred_element_type=f32) for h in hs]
    WU = [jnp.dot(T[h], jnp.concatenate([At[h], MV[h]], axis=1), preferred_element_type=f32) for h in hs]
    H0 = [h_ref[h] for h in hs]
    Ehat = [jnp.exp(gtot[h] - G[h]) for h in hs]
    Om = [Rt[h] + mm(A_rb[h], WU[h][:, :HD]) for h in hs]
    Y0 = [mm(A_rb[h], WU[h][:, HD:]) + mm(A_rk[h], v[h]) for h in hs]
    BW = [mm_tn(akk[h] * Ehat[h], WU[h]) for h in hs]
    KV = [mm_tn(kr[h] * Ehat[h], v[h]) for h in hs]
    y_ref[0, 0] = jnp.concatenate([jnp.dot(Om[h], H0[h], preferred_element_type=f32) + Y0[h] for h in hs], axis=1)
    for h in hs:
        P = eye * jnp.exp(gtot[h]) + BW[h][:, :HD]
        h_ref[h] = jnp.dot(P, H0[h], preferred_element_type=f32) + BW[h][:, HD:] + KV[h]


def rwkv_chunked(r, kk, v, lw, akk, kr, n_ctx):
    B, T, W = r.shape
    H = W // RW_HEAD_DIM
    nc = n_ctx // RW_CHUNK
    nt = T // RW_CHUNK

    def chunk_of(d, n):
        bwd = jnp.where(n < nc, nc - 1 - n, nt - 1 - (n - nc))
        return jnp.where(d == 0, n, bwd)

    spec1 = pl.BlockSpec((1, RW_CHUNK, W), lambda d, b, n: (b, chunk_of(d, n), 0))
    spec2 = pl.BlockSpec((1, 1, RW_CHUNK, W), lambda d, b, n: (d, b, chunk_of(d, n), 0))
    return pl.pallas_call(
        _rwkv_chunk_kernel,
        grid=(2, B, nt),
        in_specs=[spec1, spec1, spec1, spec2, spec2, spec2],
        out_specs=spec2,
        out_shape=jax.ShapeDtypeStruct((2, B, T, W), jnp.float32),
        scratch_shapes=[pltpu.VMEM((H, RW_HEAD_DIM, RW_HEAD_DIM), jnp.float32)],
        compiler_params=pltpu.CompilerParams(dimension_semantics=("parallel", "parallel", "arbitrary")),
        name="rwkv_chunked",
    )(r, kk, v, lw, akk, kr)


LANE = 128
RW_PREP_ROWS = 256
MLA_PAD_WIDTH = MLA_HEADS * LANE
MLA_PREP_ROWS = 256
ATTN_Q_ROWS = 512


def _split_dot(x, m):
    hi = x.astype(jnp.bfloat16)
    lo = (x - hi.astype(jnp.float32)).astype(jnp.bfloat16)
    return (jnp.dot(hi, m, preferred_element_type=jnp.float32) + jnp.dot(lo, m, preferred_element_type=jnp.float32))


def _rwkv_prep_kernel(z_ref, wda_ref, d0_ref, a0_ref, gup_ref, kk_ref_w, ka_ref, rk_ref, hsum_ref,
                      r_ref, kk_ref, v_ref, lw_ref, akk_ref, kr_ref, g_ref, bonus_ref):
    W = RW_WIDTH
    bf = jnp.bfloat16
    z = z_ref[0]
    r, k, v = z[:, :W], z[:, W:2 * W], z[:, 2 * W:3 * W]
    da = z[:, 3 * W:3 * W + LANE]
    lane = lax.broadcasted_iota(jnp.int32, da.shape, 1)
    da = jnp.where(lane < RW_DECAY_RANK, jnp.tanh(da), da)
    up = jnp.dot(da.astype(bf), wda_ref[...], preferred_element_type=jnp.float32)
    g_lo = z[:, 3 * W + LANE:]
    g_ref[0] = jnp.dot(jax.nn.sigmoid(g_lo).astype(bf), gup_ref[...], preferred_element_type=jnp.float32)
    hsum = hsum_ref[...]
    kk = k * kk_ref_w[...]
    kk = kk * lax.rsqrt(_split_dot(kk * kk, hsum) + L2_EPS)
    r_ref[0] = r
    v_ref[0] = v
    kk_ref[0] = kk
    bonus_ref[0] = _split_dot(r * k * rk_ref[...], hsum) * v
    for d in range(2):
        lw_ref[d, 0] = -RW_DECAY_SCALE * jax.nn.sigmoid(d0_ref[d:d + 1, :] + up[:, d * W:(d + 1) * W])
        a = jax.nn.sigmoid(a0_ref[d:d + 1, :] + up[:, (2 + d) * W:(3 + d) * W])
        akk_ref[d, 0] = kk * a
        kr_ref[d, 0] = k * (1.0 + (a - 1.0) * ka_ref[...])


def rwkv_prep(z, decay_up, decay0, a_up, a0, gate_up, k_k, k_a, r_k):
    B, L, _ = z.shape
    W = RW_WIDTH
    T = min(RW_PREP_ROWS, L)
    zero = jnp.zeros((RW_DECAY_RANK, 2 * W), jnp.float32)
    wda = jnp.concatenate([
        jnp.concatenate([decay_up[0], decay_up[1], zero], axis=1),
        jnp.concatenate([zero, a_up[0], a_up[1]], axis=1)], axis=0).astype(jnp.bfloat16)
    head = jnp.arange(W) // RW_HEAD_DIM
    hsum = (head[:, None] == head[None, :]).astype(jnp.bfloat16)
    row = lambda a: a.reshape(1, W)
    const = lambda a: pl.BlockSpec(a.shape, lambda b, i: (0,) * a.ndim)
    tok = pl.BlockSpec((1, T, W), lambda b, i: (b, i, 0))
    tok2 = pl.BlockSpec((2, 1, T, W), lambda b, i: (0, b, i, 0))
    f1 = jax.ShapeDtypeStruct((B, L, W), jnp.float32)
    f2 = jax.ShapeDtypeStruct((2, B, L, W), jnp.float32)
    args = (z, wda, decay0, a0, gate_up.astype(jnp.bfloat16), row(k_k), row(k_a), row(r_k), hsum)
    return pl.pallas_call(
        _rwkv_prep_kernel,
        grid=(B, L // T),
        in_specs=[pl.BlockSpec((1, T, RW_PROJ), lambda b, i: (b, i, 0))] + [const(a) for a in args[1:]],
        out_specs=[tok, tok, tok, tok2, tok2, tok2, tok, tok],
        out_shape=[f1, f1, f1, f2, f2, f2, f1, f1],
        compiler_params=pltpu.CompilerParams(dimension_semantics=("parallel", "parallel"),
                                             vmem_limit_bytes=VMEM_LIMIT_BYTES),
        name="rwkv_prep",
    )(*args)


def _rwkv_readout_kernel(y_ref, g_ref, bonus_ref, gng_ref, gnb_ref, hsum_ref, o_ref):
    y = y_ref[0, 0] + y_ref[1, 0]
    hsum = hsum_ref[...]
    mu = _split_dot(y, hsum) * (1.0 / RW_HEAD_DIM)
    d = y - mu
    var = _split_dot(d * d, hsum) * (1.0 / RW_HEAD_DIM)
    yn = d * lax.rsqrt(var + RW_GN_EPS) * gng_ref[...] + gnb_ref[...]
    o_ref[0] = (yn + bonus_ref[0]) * g_ref[0]


def rwkv_readout(y, g, bonus, gn_g, gn_b, t0):
    B, L, W = g.shape
    T = min(RW_PREP_ROWS, L)
    off = t0 // T
    head = jnp.arange(W) // RW_HEAD_DIM
    hsum = (head[:, None] == head[None, :]).astype(jnp.bfloat16)
    tok = pl.BlockSpec((1, T, W), lambda b, i: (b, i, 0))
    const = lambda a: pl.BlockSpec(a.shape, lambda b, i: (0,) * a.ndim)
    gg, gb = gn_g.reshape(1, W), gn_b.reshape(1, W)
    return pl.pallas_call(
        _rwkv_readout_kernel,
        grid=(B, L // T),
        in_specs=[pl.BlockSpec((2, 1, T, W), lambda b, i: (0, b, i + off, 0)), tok, tok, const(gg), const(gb), const(hsum)],
        out_specs=tok,
        out_shape=jax.ShapeDtypeStruct((B, L, W), jnp.float32),
        compiler_params=pltpu.CompilerParams(dimension_semantics=("parallel", "parallel")),
        name="rwkv_readout",
    )(y, g, bonus, gg, gb, hsum)


def rwkv7_mixer(p_lat, p_ctx, conv_w, decay_up, decay0, a_up, a0, gate_up, k_k, k_a, r_k, gn_g, gn_b, need_ctx):
    prm = (decay_up, decay0, a_up, a0, gate_up, k_k, k_a, r_k)
    lat = rwkv_prep(short_conv(p_lat, conv_w), *prm)
    ctx = rwkv_prep(short_conv(p_ctx, conv_w), *prm)
    n_ctx = p_ctx.shape[1]
    seq = lambda i: jnp.concatenate([ctx[i], lat[i]], axis=-2)
    y = rwkv_chunked(seq(0), seq(1), seq(2), seq(3), seq(4), seq(5), n_ctx)
    out_l = rwkv_readout(y, lat[6], lat[7], gn_g, gn_b, n_ctx)
    out_c = rwkv_readout(y, ctx[6], ctx[7], gn_g, gn_b, 0) if need_ctx else None
    return out_l, out_c


def _rope_tables(L, use_rope):
    lane = np.arange(LANE)
    in_rope = (lane >= MLA_NOPE_DIM) & (lane < MLA_QK_DIM)
    j = lane - MLA_NOPE_DIM
    axis = j // AXIS_ROPE_DIM
    half = AXIS_ROPE_DIM // 2
    f = j % half
    first = (j % AXIS_ROPE_DIM) < half
    inv = ROPE_THETA ** (-jnp.arange(0, AXIS_ROPE_DIM, 2, dtype=jnp.float32) / AXIS_ROPE_DIM)
    t = jnp.arange(L)
    pos = jnp.stack([t // GRID_W, t % GRID_W], axis=-1).astype(jnp.float32)
    ang = pos[:, np.clip(axis, 0, 1)] * inv[np.clip(f, 0, half - 1)][None, :]
    rope_on = jnp.asarray(in_rope)[None, :] & use_rope
    cos = jnp.where(rope_on, jnp.cos(ang), 1.0)
    sin = jnp.where(rope_on, jnp.sin(ang) * jnp.where(jnp.asarray(first), -1.0, 1.0)[None, :], 0.0)
    return jnp.tile(cos, (1, MLA_HEADS)), jnp.tile(sin, (1, MLA_HEADS))


def _mla_prep_kernel(p_ref, qn_ref, wq_ref, kvn_ref, wk_ref, wv_ref, place_ref, qg_ref, kg_ref, hsum_ref, cos_ref, sin_ref,
                     q_ref, k_ref, v_ref):
    bf = jnp.bfloat16
    p = p_ref[0]
    c_q = p[:, :MLA_Q_RANK]
    c_kv = p[:, MLA_Q_RANK:MLA_Q_RANK + MLA_KV_RANK]
    tail = p[:, MLA_Q_RANK + MLA_KV_RANK:]
    cqn = c_q * lax.rsqrt(jnp.mean(c_q * c_q, axis=-1, keepdims=True) + NORM_EPS) * qn_ref[...]
    ckn = c_kv * lax.rsqrt(jnp.mean(c_kv * c_kv, axis=-1, keepdims=True) + NORM_EPS) * kvn_ref[...]
    q = jnp.dot(cqn.astype(bf), wq_ref[...], preferred_element_type=jnp.float32)
    k = jnp.dot(ckn.astype(bf), wk_ref[...], preferred_element_type=jnp.float32) + _split_dot(tail, place_ref[...])
    v_ref[0] = jnp.dot(ckn.astype(bf), wv_ref[...], preferred_element_type=jnp.float32).astype(bf)
    hsum = hsum_ref[...]
    cos, sin = cos_ref[...], sin_ref[...]
    lane = lax.broadcasted_iota(jnp.int32, q.shape, 1)
    first = ((lane - MLA_NOPE_DIM) % AXIS_ROPE_DIM) < (AXIS_ROPE_DIM // 2)
    half = AXIS_ROPE_DIM // 2

    def finish(x, gain):
        x = x * lax.rsqrt(_split_dot(x * x, hsum) * (1.0 / MLA_QK_DIM) + NORM_EPS) * gain
        partner = jnp.where(first, pltpu.roll(x, MLA_PAD_WIDTH - half, 1), pltpu.roll(x, half, 1))
        return x * cos + partner * sin

    q_ref[0] = (finish(q, qg_ref[...]) * (MLA_QK_DIM ** -0.5)).astype(bf)
    k_ref[0] = finish(k, kg_ref[...]).astype(bf)


def mla_prep(p, use_rope, q_norm, w_uq, kv_norm, w_ukv, q_gain, k_gain):
    B, L, _ = p.shape
    T = min(MLA_PREP_ROWS, L)
    H = MLA_HEADS
    pad_cols = lambda w, d: jnp.pad(w.reshape(w.shape[0], H, d), ((0, 0), (0, 0), (0, LANE - d))).reshape(w.shape[0], H * LANE)
    wq = pad_cols(w_uq, MLA_QK_DIM).astype(jnp.bfloat16)
    ukv = w_ukv.reshape(MLA_KV_RANK, H, MLA_NOPE_DIM + MLA_V_DIM)
    wk = pad_cols(ukv[:, :, :MLA_NOPE_DIM].reshape(MLA_KV_RANK, H * MLA_NOPE_DIM), MLA_NOPE_DIM).astype(jnp.bfloat16)
    wv = ukv[:, :, MLA_NOPE_DIM:].reshape(MLA_KV_RANK, H * MLA_V_DIM).astype(jnp.bfloat16)
    lane = np.arange(H * LANE)
    place = jnp.asarray(((lane[None, :] % LANE) - MLA_NOPE_DIM == np.arange(MLA_ROPE_DIM)[:, None]), jnp.bfloat16)
    hsum = jnp.asarray((lane[:, None] // LANE) == (lane[None, :] // LANE), jnp.bfloat16)
    pad_gain = lambda g: jnp.tile(jnp.pad(g, (0, LANE - MLA_QK_DIM)), H).reshape(1, H * LANE)
    cos, sin = _rope_tables(L, use_rope)
    const = lambda a: pl.BlockSpec(a.shape, lambda b, i: (0,) * a.ndim)
    args = (p, q_norm.reshape(1, -1), wq, kv_norm.reshape(1, -1), wk, wv, place, pad_gain(q_gain), pad_gain(k_gain), hsum)
    pos = pl.BlockSpec((T, H * LANE), lambda b, i: (i, 0))
    return pl.pallas_call(
        _mla_prep_kernel,
        grid=(B, L // T),
        in_specs=[pl.BlockSpec((1, T, MLA_PROJ), lambda b, i: (b, i, 0))] + [const(a) for a in args[1:]] + [pos, pos],
        out_specs=[pl.BlockSpec((1, T, H * LANE), lambda b, i: (b, i, 0)), pl.BlockSpec((1, T, H * LANE), lambda b, i: (b, i, 0)),
                   pl.BlockSpec((1, T, MLA_WIDTH), lambda b, i: (b, i, 0))],
        out_shape=[jax.ShapeDtypeStruct((B, L, H * LANE), jnp.bfloat16), jax.ShapeDtypeStruct((B, L, H * LANE), jnp.bfloat16),
                   jax.ShapeDtypeStruct((B, L, MLA_WIDTH), jnp.bfloat16)],
        compiler_params=pltpu.CompilerParams(dimension_semantics=("parallel", "parallel"),
                                             vmem_limit_bytes=VMEM_LIMIT_BYTES),
        name="mla_prep",
    )(*args, cos, sin)


def _attn_kernel(q_ref, k_ref, v_ref, o_ref):
    lane = lax.broadcasted_iota(jnp.int32, (q_ref.shape[1], LANE), 1)
    for pair in range(MLA_HEADS // 2):
        v_pair = v_ref[0, :, pair * LANE:(pair + 1) * LANE]
        outs = []
        for h in (2 * pair, 2 * pair + 1):
            q = q_ref[0, :, h * LANE:(h + 1) * LANE]
            k = k_ref[0, :, h * LANE:(h + 1) * LANE]
            s = lax.dot_general(q, k, (((1,), (1,)), ((), ())), preferred_element_type=jnp.float32)
            e = jnp.exp(s - jnp.max(s, axis=-1, keepdims=True))
            o = jnp.dot(e.astype(jnp.bfloat16), v_pair, preferred_element_type=jnp.float32)
            outs.append(o / jnp.sum(e, axis=-1, keepdims=True))
        o_ref[0, :, pair * LANE:(pair + 1) * LANE] = jnp.where(lane < MLA_V_DIM, outs[0], outs[1])


def attention(q, k, v):
    B, Lq, P = q.shape
    Lk = k.shape[1]
    tq = min(ATTN_Q_ROWS, Lq)
    return pl.pallas_call(
        _attn_kernel,
        grid=(B, Lq // tq),
        in_specs=[pl.BlockSpec((1, tq, P), lambda b, i: (b, i, 0)),
                  pl.BlockSpec((1, Lk, P), lambda b, i: (b, 0, 0)),
                  pl.BlockSpec((1, Lk, MLA_WIDTH), lambda b, i: (b, 0, 0))],
        out_specs=pl.BlockSpec((1, tq, MLA_WIDTH), lambda b, i: (b, i, 0)),
        out_shape=jax.ShapeDtypeStruct((B, Lq, MLA_WIDTH), jnp.float32),
        compiler_params=pltpu.CompilerParams(dimension_semantics=("parallel", "parallel"),
                                             vmem_limit_bytes=VMEM_LIMIT_BYTES),
        name="mla_attention",
    )(q, k, v)


def mla_mixer(p_lat, p_ctx, q_norm, w_uq, kv_norm, w_ukv, q_gain, k_gain, need_ctx):
    prm = (q_norm, w_uq, kv_norm, w_ukv, q_gain, k_gain)
    q_l, k_l, v_l = mla_prep(p_lat, True, *prm)
    q_c, k_c, v_c = mla_prep(p_ctx, False, *prm)
    y_l = attention(q_l, jnp.concatenate([k_l, k_c], axis=1), jnp.concatenate([v_l, v_c], axis=1))
    y_c = attention(q_c, k_c, v_c) if need_ctx else None
    return y_l, y_c


HY_FILTER_ROWS = 256


def _hyena_filter_kernel(z_ref, tn_ref, w1_ref, b1_ref, f1_ref, w2_ref, b2_ref, f2_ref, w3_ref, b3_ref, rates_ref, o_ref):
    hp = lax.Precision.HIGHEST
    h = jnp.sin(f1_ref[...] * (jnp.dot(z_ref[...], w1_ref[...], precision=hp, preferred_element_type=jnp.float32) + b1_ref[...]))
    h = jnp.sin(f2_ref[...] * (jnp.dot(h, w2_ref[...], precision=hp, preferred_element_type=jnp.float32) + b2_ref[...]))
    h = jnp.dot(h, w3_ref[...], precision=hp, preferred_element_type=jnp.float32) + b3_ref[...]
    o_ref[...] = h * jnp.exp(-tn_ref[...] * rates_ref[...])


def hyena_filters(L, w1, b1, freq1, w2, b2, freq2, w3, b3):
    tn = jnp.arange(L, dtype=jnp.float32) / L
    bands = jnp.arange(1, HY_POS_BANDS + 1, dtype=jnp.float32)
    ang = 2.0 * math.pi * tn[:, None] * bands[None, :]
    z = jnp.concatenate([tn[:, None], jnp.cos(ang), jnp.sin(ang)], axis=-1)
    rates = jnp.abs(jnp.linspace(math.log(HY_DECAY_TARGET) / HY_LONG_DECAY_PCT,
                                 math.log(HY_DECAY_TARGET) / HY_SHORT_DECAY_PCT, HY_WIDTH))
    pad = -z.shape[1] % 8
    T = min(HY_FILTER_ROWS, L)
    n_out = w3.shape[1]
    row = lambda a: a.reshape(1, -1)
    const = lambda a: pl.BlockSpec(a.shape, lambda i: (0,) * a.ndim)
    args = (jnp.pad(w1, ((0, pad), (0, 0))), row(b1), row(freq1), w2, row(b2), row(freq2), w3, row(b3),
            row(jnp.tile(rates, n_out // HY_WIDTH)))
    h = pl.pallas_call(
        _hyena_filter_kernel,
        grid=(L // T,),
        in_specs=[pl.BlockSpec((T, z.shape[1] + pad), lambda i: (i, 0)), pl.BlockSpec((T, 1), lambda i: (i, 0))]
                 + [const(a) for a in args],
        out_specs=pl.BlockSpec((T, n_out), lambda i: (i, 0)),
        out_shape=jax.ShapeDtypeStruct((L, n_out), jnp.float32),
        compiler_params=pltpu.CompilerParams(dimension_semantics=("parallel",)),
        name="hyena_filter_mlp",
    )(jnp.pad(z, ((0, 0), (0, pad))), tn[:, None], *args)
    h = h.reshape(L, HY_ORDER, 2, HY_WIDTH)
    zero = jnp.zeros((1, HY_ORDER, HY_WIDTH), h.dtype)
    h_full = jnp.concatenate([h[:, :, 0], zero, h[:0:-1, :, 1]], axis=0)
    return h_full * lax.rsqrt(jnp.sum(jnp.square(h_full), axis=0, keepdims=True))


def fft_long_conv(u, h_full, bias):
    L = u.shape[1]
    uf = jnp.fft.rfft(u, n=2 * L, axis=1)
    hf = jnp.fft.rfft(h_full, n=2 * L, axis=0)
    y = jnp.fft.irfft(uf * hf[None], n=2 * L, axis=1)[:, :L]
    return y + u * bias


FFT_N1 = 64
FFT_N2 = 128
FFT_N = FFT_N1 * FFT_N2
HY_SEQS = 32


def _dft_tables(seqs):
    n1 = np.arange(FFT_N1)
    n2 = np.arange(FFT_N2)
    f64 = np.exp(-2j * np.pi * np.outer(n1, n1) / FFT_N1)
    f128 = np.exp(-2j * np.pi * np.outer(n2, n2) / FFT_N2)
    tw = np.exp(-2j * np.pi * np.outer(n1, n2) / FFT_N)
    half = FFT_N1 // 2
    fh = f64[:, :half]
    m1 = np.block([[fh.real, -fh.imag], [fh.imag, fh.real]])
    m1f = np.concatenate([f64.real, f64.imag], axis=0)
    m2 = np.block([[f128.real, f128.imag], [-f128.imag, f128.real]])
    m3 = np.block([[f128.real, -f128.imag], [f128.imag, f128.real]]) / FFT_N
    c = np.conj(f64)[:half, :]
    m4 = np.block([[c.real, -c.imag], [c.imag, c.real]])
    bf = lambda a: jnp.asarray(a, jnp.float32).astype(jnp.bfloat16)
    f32 = lambda a: jnp.asarray(a, jnp.float32)
    return dict(m1=bf(m1), m1f=bf(m1f), m2=bf(m2), m3=bf(m3), m4=bf(m4),
                twr_l=f32(np.tile(tw.real, (1, seqs))), twi_l=f32(np.tile(tw.imag, (1, seqs))),
                twr_s=f32(np.tile(tw.real, (seqs, 1))), twi_s=f32(np.tile(tw.imag, (seqs, 1))))


def _spectrum(cols, m1, twr_l, twi_l, m2, R):
    a = jnp.dot(m1, cols.astype(jnp.bfloat16), preferred_element_type=jnp.float32)
    ar, ai = a[:FFT_N1], a[FFT_N1:]
    pr = ar * twr_l - ai * twi_l
    pi = ar * twi_l + ai * twr_l
    lhs = jnp.concatenate(
        [jnp.concatenate([pr[:, r * FFT_N2:(r + 1) * FFT_N2], pi[:, r * FFT_N2:(r + 1) * FFT_N2]], axis=1)
         for r in range(R)], axis=0)
    return jnp.dot(lhs.astype(jnp.bfloat16), m2, preferred_element_type=jnp.float32)


def _filter_fft_kernel(h_ref, m1f_ref, twr_ref, twi_ref, m2_ref, o_ref):
    R = HY_SEQS
    cols = jnp.concatenate([h_ref[r] for r in range(R)], axis=1)
    x = _spectrum(cols, m1f_ref[...], twr_ref[...], twi_ref[...], m2_ref[...], R)
    o_ref[...] = x.reshape(R, FFT_N1, 2 * FFT_N2)


def _hyena_conv_kernel(y_ref, g_ref, hf_ref, bias_ref, m1_ref, twr_l_ref, twi_l_ref, m2_ref, m3_ref,
                       twr_s_ref, twi_s_ref, m4_ref, o_ref):
    R = HY_SEQS
    half = FFT_N1 // 2
    y = [y_ref[0], y_ref[1]]
    for o in range(HY_ORDER):
        top = jnp.concatenate([y[0][r] for r in range(R)], axis=1)
        bot = jnp.concatenate([y[1][r] for r in range(R)], axis=1)
        x = _spectrum(jnp.concatenate([top, bot], axis=0), m1_ref[...], twr_l_ref[...], twi_l_ref[...], m2_ref[...], R)
        hf = hf_ref[o].reshape(R * FFT_N1, 2 * FFT_N2)
        xr, xi = x[:, :FFT_N2], x[:, FFT_N2:]
        hr, hi = hf[:, :FFT_N2], hf[:, FFT_N2:]
        yc = jnp.concatenate([xr * hr - xi * hi, xr * hi + xi * hr], axis=1)
        b = jnp.dot(yc.astype(jnp.bfloat16), m3_ref[...], preferred_element_type=jnp.float32)
        br, bi = b[:, :FFT_N2], b[:, FFT_N2:]
        qr = br * twr_s_ref[...] + bi * twi_s_ref[...]
        qi = bi * twr_s_ref[...] - br * twi_s_ref[...]
        bc = jnp.concatenate(
            [jnp.concatenate([qr[r * FFT_N1:(r + 1) * FFT_N1], qi[r * FFT_N1:(r + 1) * FFT_N1]], axis=0)
             for r in range(R)], axis=1)
        yo = jnp.dot(m4_ref[...], bc.astype(jnp.bfloat16), preferred_element_type=jnp.float32)
        for p in range(2):
            conv = jnp.stack([yo[p * half:(p + 1) * half, r * FFT_N2:(r + 1) * FFT_N2] for r in range(R)], axis=0)
            y[p] = g_ref[o, p] * (conv + y[p] * bias_ref[o])
    o_ref[0] = y[0]
    o_ref[1] = y[1]


def hyena_long_conv(y_t, g_t, h_t, bias):
    B, C, L = y_t.shape
    assert 2 * L == FFT_N and B % 2 == 0 and C % HY_SEQS == 0
    R = HY_SEQS
    half = FFT_N1 // 2
    tb = _dft_tables(R)
    const = lambda a: pl.BlockSpec(a.shape, lambda *_: (0,) * a.ndim)
    hf = pl.pallas_call(
        _filter_fft_kernel,
        grid=(HY_ORDER * C // R,),
        in_specs=[pl.BlockSpec((R, FFT_N1, FFT_N2), lambda i: (i, 0, 0)),
                  const(tb['m1f']), const(tb['twr_l']), const(tb['twi_l']), const(tb['m2'])],
        out_specs=pl.BlockSpec((R, FFT_N1, 2 * FFT_N2), lambda i: (i, 0, 0)),
        out_shape=jax.ShapeDtypeStruct((HY_ORDER * C, FFT_N1, 2 * FFT_N2), jnp.float32),
        compiler_params=pltpu.CompilerParams(dimension_semantics=("parallel",), vmem_limit_bytes=VMEM_LIMIT_BYTES),
        name="hyena_filter_fft",
    )(h_t.reshape(HY_ORDER * C, FFT_N1, FFT_N2), tb['m1f'], tb['twr_l'], tb['twi_l'], tb['m2'])
    hf = hf.reshape(HY_ORDER, C, FFT_N1, 2 * FFT_N2)
    out = pl.pallas_call(
        _hyena_conv_kernel,
        grid=(B // 2, C // R),
        in_specs=[pl.BlockSpec((2, R, half, FFT_N2), lambda b, c: (b, c, 0, 0)),
                  pl.BlockSpec((HY_ORDER, 2, R, half, FFT_N2), lambda b, c: (0, b, c, 0, 0)),
                  pl.BlockSpec((HY_ORDER, R, FFT_N1, 2 * FFT_N2), lambda b, c: (0, c, 0, 0)),
                  pl.BlockSpec((HY_ORDER, R, 1, 1), lambda b, c: (0, c, 0, 0)),
                  const(tb['m1']), const(tb['twr_l']), const(tb['twi_l']), const(tb['m2']), const(tb['m3']),
                  const(tb['twr_s']), const(tb['twi_s']), const(tb['m4'])],
        out_specs=pl.BlockSpec((2, R, half, FFT_N2), lambda b, c: (b, c, 0, 0)),
        out_shape=jax.ShapeDtypeStruct((B, C, half, FFT_N2), jnp.float32),
        compiler_params=pltpu.CompilerParams(dimension_semantics=("parallel", "parallel"),
                                             vmem_limit_bytes=VMEM_LIMIT_BYTES),
        name="hyena_conv",
    )(y_t.reshape(B, C, half, FFT_N2), g_t.reshape(HY_ORDER, B, C, half, FFT_N2), hf,
      bias.reshape(HY_ORDER, C, 1, 1), tb['m1'], tb['twr_l'], tb['twi_l'], tb['m2'], tb['m3'],
      tb['twr_s'], tb['twi_s'], tb['m4'])
    return out.reshape(B, C, L)


def hyena_mixer(p, conv_w, w1, b1, freq1, w2, b2, freq2, w3, b3, bias):
    B, L = p.shape[:2]
    z = short_conv(p, conv_w)
    h_full = hyena_filters(L, w1, b1, freq1, w2, b2, freq2, w3, b3)
    if 2 * L == FFT_N:
        g_t = jnp.transpose(z[..., :HY_ORDER * HY_WIDTH].reshape(B, L, HY_ORDER, HY_WIDTH), (2, 0, 3, 1))
        y_t = jnp.swapaxes(z[..., HY_ORDER * HY_WIDTH:], 1, 2)
        y_t = hyena_long_conv(y_t, g_t, jnp.transpose(h_full, (1, 2, 0)), bias)
        return jnp.swapaxes(y_t, 1, 2)
    gates = (z[..., :HY_WIDTH], z[..., HY_WIDTH:2 * HY_WIDTH])
    y = z[..., 2 * HY_WIDTH:]
    for o in range(HY_ORDER):
        y = gates[o] * fft_long_conv(y, h_full[:, o], bias[o])
    return y


SC_CORES = 2
SC_SUBCORES = 16
SC_LANES = 16
SC_WORKERS = SC_CORES * SC_SUBCORES
PEER_SLOTS = PEER_HEADS * PEER_TOPK
PEER_GATHER_ROWS = 32
PEER_GATHERS = PEER_SLOTS // PEER_GATHER_ROWS
PEER_ACC_VREGS = 8
PEER_ROW_BUFFERS = 4
PEER_ROW_WORDS = D_MODEL // 2
HI_MASK = -65536


def pack_expert_table(t):
    b = lax.bitcast_convert_type(t.astype(jnp.bfloat16), jnp.uint16).astype(jnp.uint32)
    return lax.bitcast_convert_type(b[:, :PEER_ROW_WORDS] | (b[:, PEER_ROW_WORDS:] << 16), jnp.int32)


def _sc_peer_phase(phase, tpw):
    NBUF = PEER_ROW_BUFFERS
    AHEAD = NBUF - 1
    HW = PEER_ROW_WORDS

    def run(base, table_hbm, idx_hbm, aux_hbm, out_hbm, idx_v, aux_v, rows_v, out_v, sem_r, sem_i, sem_o):

        def gather(p, c, b):
            return pltpu.make_async_copy(table_hbm.at[idx_v.at[p, c]], rows_v.at[b], sem_r.at[b])

        def load_meta(t, p):
            return (pltpu.make_async_copy(idx_hbm.at[t], idx_v.at[p], sem_i.at[p]),
                    pltpu.make_async_copy(aux_hbm.at[t], aux_v.at[p], sem_i.at[p]))

        def store_out(t, p):
            return pltpu.make_async_copy(out_v.at[p], out_hbm.at[t], sem_o.at[p])

        def halves(word):
            return (plsc.bitcast(lax.shift_left(word, 16), jnp.float32), plsc.bitcast(word & HI_MASK, jnp.float32))

        def compute(p, c, b):
            if phase == "dot":
                lane = lax.iota(jnp.int32, SC_LANES)
                vec = jnp.zeros((SC_LANES,), jnp.float32)
                groups_per_vec = SC_LANES // PEER_ACC_VREGS
                for g in range(PEER_GATHER_ROWS // PEER_ACC_VREGS):
                    def body(cc, accs):
                        x_lo = aux_v[p, pl.ds(cc * SC_LANES, SC_LANES)]
                        x_hi = aux_v[p, pl.ds(HW + cc * SC_LANES, SC_LANES)]
                        out = []
                        for r in range(PEER_ACC_VREGS):
                            lo, hi = halves(rows_v[b, g * PEER_ACC_VREGS + r, pl.ds(cc * SC_LANES, SC_LANES)])
                            out.append(accs[r] + lo * x_lo + hi * x_hi)
                        return tuple(out)
                    accs = lax.fori_loop(0, HW // SC_LANES, body,
                                         tuple(jnp.zeros((SC_LANES,), jnp.float32) for _ in range(PEER_ACC_VREGS)))
                    for r in range(PEER_ACC_VREGS):
                        vec = jnp.where(lane == (g % groups_per_vec) * PEER_ACC_VREGS + r, jnp.sum(accs[r]), vec)
                    if g % groups_per_vec == groups_per_vec - 1:
                        out_v[p, pl.ds(c * PEER_GATHER_ROWS + (g // groups_per_vec) * SC_LANES, SC_LANES)] = vec
            else:
                words = PEER_ACC_VREGS // 2
                for db in range(HW // (words * SC_LANES)):
                    def body(kk, accs):
                        wv = plsc.load_gather(aux_v.at[p], [jnp.full((SC_LANES,), c * PEER_GATHER_ROWS + kk, jnp.int32)])
                        out = []
                        for j in range(words):
                            lo, hi = halves(rows_v[b, kk, pl.ds((db * words + j) * SC_LANES, SC_LANES)])
                            out += [accs[2 * j] + lo * wv, accs[2 * j + 1] + hi * wv]
                        return tuple(out)
                    if c == 0:
                        init = tuple(jnp.zeros((SC_LANES,), jnp.float32) for _ in range(2 * words))
                    else:
                        init = tuple(out_v[p, pl.ds(half * HW + (db * words + j) * SC_LANES, SC_LANES)]
                                     for j in range(words) for half in range(2))
                    accs = lax.fori_loop(0, PEER_GATHER_ROWS, body, init)
                    for j in range(words):
                        out_v[p, pl.ds((db * words + j) * SC_LANES, SC_LANES)] = accs[2 * j]
                        out_v[p, pl.ds(HW + (db * words + j) * SC_LANES, SC_LANES)] = accs[2 * j + 1]

        for d in load_meta(base, 0):
            d.start()
        for d in load_meta(base, 0):
            d.wait()
        for c in range(AHEAD):
            gather(0, c, c % NBUF).start()

        @pl.loop(0, tpw // 2)
        def _(i2):
            for p in range(2):
                i = i2 * 2 + p
                t = base + i
                nxt = base + jnp.minimum(i + 1, tpw - 1)
                for d in load_meta(nxt, 1 - p):
                    d.start()

                @pl.when(i2 > 0)
                def _():
                    store_out(t, p).wait()

                for c in range(PEER_GATHERS):
                    ahead = c + AHEAD
                    if ahead < PEER_GATHERS:
                        gather(p, ahead, ahead % NBUF).start()
                    else:
                        if ahead == PEER_GATHERS:
                            for d in load_meta(nxt, 1 - p):
                                d.wait()
                        gather(1 - p, ahead - PEER_GATHERS, ahead % NBUF).start()
                    gather(p, c, c % NBUF).wait()
                    compute(p, c, c % NBUF)
                store_out(t, p).start()

        for c in range(AHEAD):
            gather(0, c, c % NBUF).wait()
        for p in range(2):
            store_out(base, p).wait()

    return run


def _sc_tokens_per_worker(N):
    assert N % (2 * SC_WORKERS) == 0 and PEER_GATHERS % PEER_ROW_BUFFERS == 0
    return N // SC_WORKERS


_SC_AUX = {"dot": (D_MODEL,), "wsum": (PEER_SLOTS,)}
_SC_OUT = {"dot": (PEER_SLOTS,), "wsum": (D_MODEL,)}


def _sc_scratch(phases):
    s = [pltpu.VMEM((2, PEER_GATHERS, PEER_GATHER_ROWS), jnp.int32),
         pltpu.VMEM((PEER_ROW_BUFFERS, PEER_GATHER_ROWS, PEER_ROW_WORDS), jnp.int32),
         pltpu.SemaphoreType.DMA((PEER_ROW_BUFFERS,)), pltpu.SemaphoreType.DMA((2,)), pltpu.SemaphoreType.DMA((2,))]
    for ph in phases:
        s += [pltpu.VMEM((2,) + _SC_AUX[ph], jnp.float32), pltpu.VMEM((2,) + _SC_OUT[ph], jnp.float32)]
    return s


def _sc_peer_jobs(jobs):
    phases = [j[0] for j in jobs]
    kinds = sorted(set(phases))
    ns = [j[2].shape[0] for j in jobs]
    runs = [_sc_peer_phase(ph, _sc_tokens_per_worker(n)) for ph, n in zip(phases, ns)]
    nj = len(jobs)

    @functools.partial(
        pl.kernel, mesh=plsc.VectorSubcoreMesh(core_axis_name="c", subcore_axis_name="s"),
        out_type=tuple(jax.ShapeDtypeStruct((n,) + _SC_OUT[ph], jnp.float32) for ph, n in zip(phases, ns)),
        compiler_params=pltpu.CompilerParams(needs_layout_passes=False),
        scratch_types=_sc_scratch(kinds),
    )
    def k(*refs):
        ins, outs, scratch = refs[:3 * nj], refs[3 * nj:4 * nj], refs[4 * nj:]
        idx_v, rows_v, sem_r, sem_i, sem_o = scratch[:5]
        bufs = {kind: scratch[5 + 2 * i:7 + 2 * i] for i, kind in enumerate(kinds)}
        worker = lax.axis_index("s") * SC_CORES + lax.axis_index("c")
        for j in range(nj):
            table_hbm, idx_hbm, aux_hbm = ins[3 * j:3 * j + 3]
            aux_v, out_v = bufs[phases[j]]
            runs[j](worker * (ns[j] // SC_WORKERS), table_hbm, idx_hbm, aux_hbm, outs[j],
                    idx_v, aux_v, rows_v, out_v, sem_r, sem_i, sem_o)

    args = []
    for (_, table, idx, aux), n in zip(jobs, ns):
        args += [table, idx.reshape(n, PEER_GATHERS, PEER_GATHER_ROWS), aux]
    return list(k(*args))


PEER_TOKENS = 256
INT_BIG = 2 ** 30
PEER_CANDIDATES = -(-sum(PEER_TOPK // (i + 1) for i in range(PEER_TOPK)) // 8) * 8


def _extract_topk(cand_ref, ids_ref, val_out_ref, id_out_ref, row0):
    def body(r, carry):
        c = cand_ref[...]
        ids = ids_ref[...]
        m = jnp.max(c, axis=0, keepdims=True)
        sel = jnp.min(jnp.where(c == m, ids, INT_BIG), axis=0, keepdims=True)
        cand_ref[...] = jnp.where(ids == sel, -jnp.inf, c)
        val_out_ref[pl.ds(row0 + r, 1), :] = m
        id_out_ref[pl.ds(row0 + r, 1), :] = sel
        return carry
    lax.fori_loop(0, PEER_TOPK, body, 0)


def _peer_retrieve_kernel(x_ref, gain_ref, scale_ref, shift_ref, wq_ref, keys_ref,
                          h_ref, idx_out_ref, gate_out_ref,
                          s_ref, ids1_ref, sv_ref, si_ref, cand_ref, cid_ref, ts_ref, idx_ref, gate_ref):
    x = x_ref[0]
    y = x * lax.rsqrt(jnp.mean(x * x, axis=-1, keepdims=True) + NORM_EPS)
    h = (y * gain_ref[...]) * (1.0 + scale_ref[0]) + shift_ref[0]
    h_ref[0] = h
    q = jnp.dot(h.astype(jnp.bfloat16), wq_ref[...], preferred_element_type=jnp.float32)
    T = PEER_TOKENS
    K = PEER_TOPK
    ids1_ref[...] = lax.broadcasted_iota(jnp.int32, (PEER_N_KEYS, T), 0)
    for hd in range(PEER_HEADS):
        for p in range(2):
            hp = hd * 2 + p
            qs = q[:, hp * PEER_HALF:(hp + 1) * PEER_HALF].astype(jnp.bfloat16)
            s_ref[...] = lax.dot_general(keys_ref[hp], qs, (((1,), (1,)), ((), ())),
                                         preferred_element_type=jnp.float32)
            _extract_topk(s_ref, ids1_ref, sv_ref, si_ref, p * K)
        cand_ref[...] = jnp.full(cand_ref.shape, -jnp.inf, jnp.float32)
        cid_ref[...] = INT_BIG - 1 - lax.broadcasted_iota(jnp.int32, cid_ref.shape, 0)
        off = 0
        for i in range(K):
            n = K // (i + 1)
            cand_ref[off:off + n, :] = sv_ref[i:i + 1, :] + sv_ref[K:K + n, :]
            cid_ref[off:off + n, :] = si_ref[i:i + 1, :] * PEER_N_KEYS + si_ref[K:K + n, :]
            off += n
        _extract_topk(cand_ref, cid_ref, ts_ref, idx_ref, hd * K)
        ts = ts_ref[hd * K:(hd + 1) * K, :]
        e = jnp.exp(ts - jnp.max(ts, axis=0, keepdims=True))
        gate_ref[hd * K:(hd + 1) * K, :] = e / jnp.sum(e, axis=0, keepdims=True)
    idx_out_ref[...] = idx_ref[...].T
    gate_out_ref[...] = gate_ref[...].T


def peer_retrieve(x, gain, scale, shift, w_q, sub_keys):
    B, L, D = x.shape
    T = PEER_TOKENS
    nt = L // T
    keys = sub_keys.reshape(PEER_HEADS * 2, PEER_N_KEYS, PEER_HALF).astype(jnp.bfloat16)
    return pl.pallas_call(
        _peer_retrieve_kernel,
        grid=(B, nt),
        in_specs=[
            pl.BlockSpec((1, T, D), lambda b, i: (b, i, 0)),
            pl.BlockSpec((1, D), lambda b, i: (0, 0)),
            pl.BlockSpec((1, 1, D), lambda b, i: (b, 0, 0)),
            pl.BlockSpec((1, 1, D), lambda b, i: (b, 0, 0)),
            pl.BlockSpec((D, PEER_HEADS * 2 * PEER_HALF), lambda b, i: (0, 0)),
            pl.BlockSpec((PEER_HEADS * 2, PEER_N_KEYS, PEER_HALF), lambda b, i: (0, 0, 0)),
        ],
        out_specs=[
            pl.BlockSpec((1, T, D), lambda b, i: (b, i, 0)),
            pl.BlockSpec((T, PEER_SLOTS), lambda b, i: (b * nt + i, 0)),
            pl.BlockSpec((T, PEER_SLOTS), lambda b, i: (b * nt + i, 0)),
        ],
        out_shape=[
            jax.ShapeDtypeStruct((B, L, D), jnp.float32),
            jax.ShapeDtypeStruct((B * L, PEER_SLOTS), jnp.int32),
            jax.ShapeDtypeStruct((B * L, PEER_SLOTS), jnp.float32),
        ],
        scratch_shapes=[
            pltpu.VMEM((PEER_N_KEYS, T), jnp.float32),
            pltpu.VMEM((PEER_N_KEYS, T), jnp.int32),
            pltpu.VMEM((2 * PEER_TOPK, T), jnp.float32),
            pltpu.VMEM((2 * PEER_TOPK, T), jnp.int32),
            pltpu.VMEM((PEER_CANDIDATES, T), jnp.float32),
            pltpu.VMEM((PEER_CANDIDATES, T), jnp.int32),
            pltpu.VMEM((PEER_SLOTS, T), jnp.float32),
            pltpu.VMEM((PEER_SLOTS, T), jnp.int32),
            pltpu.VMEM((PEER_SLOTS, T), jnp.float32),
        ],
        compiler_params=pltpu.CompilerParams(dimension_semantics=("parallel", "parallel"),
                                             vmem_limit_bytes=VMEM_LIMIT_BYTES),
        name="peer_retrieve",
    )(x, gain.reshape(1, D), scale, shift, w_q.astype(jnp.bfloat16), keys)


PEER_ACT_ROWS = 256


def _peer_act_kernel(dots_ref, gate_ref, w_ref):
    a = dots_ref[...]
    w_ref[...] = gate_ref[...] * (0.5 * a * (1.0 + lax.erf(a * (2.0 ** -0.5))))


def peer_act(dots, gate):
    N = dots.shape[0]
    T = min(PEER_ACT_ROWS, N)
    spec = pl.BlockSpec((T, PEER_SLOTS), lambda i: (i, 0))
    return pl.pallas_call(
        _peer_act_kernel,
        grid=(N // T,),
        in_specs=[spec, spec],
        out_specs=spec,
        out_shape=jax.ShapeDtypeStruct((N, PEER_SLOTS), jnp.float32),
        compiler_params=pltpu.CompilerParams(dimension_semantics=("parallel",)),
        name="peer_act",
    )(dots, gate)


OUT_PROJ_ROWS = 512


def _out_proj_kernel(x_ref, g_ref, rw_ref, ml_ref, hy_ref, w_ref, o_ref):
    bf = jnp.bfloat16
    y = jnp.dot(rw_ref[0].astype(bf), w_ref[:RW_WIDTH, :], preferred_element_type=jnp.float32)
    y += jnp.dot(ml_ref[0].astype(bf), w_ref[RW_WIDTH:RW_WIDTH + MLA_WIDTH, :], preferred_element_type=jnp.float32)
    y += jnp.dot(hy_ref[0].astype(bf), w_ref[RW_WIDTH + MLA_WIDTH:, :], preferred_element_type=jnp.float32)
    o_ref[0] = x_ref[0] + g_ref[0] * y


def mix_out_proj(x, gate, rw, ml, hy, w_out):
    B, L, D = x.shape
    T = min(OUT_PROJ_ROWS, L)
    tok = lambda w: pl.BlockSpec((1, T, w), lambda b, i: (b, i, 0))
    return pl.pallas_call(
        _out_proj_kernel,
        grid=(B, L // T),
        in_specs=[tok(D), pl.BlockSpec((1, 1, D), lambda b, i: (b, 0, 0)), tok(RW_WIDTH), tok(MLA_WIDTH), tok(HY_WIDTH),
                  pl.BlockSpec((MIX_WIDTH, D), lambda b, i: (0, 0))],
        out_specs=tok(D),
        out_shape=jax.ShapeDtypeStruct((B, L, D), jnp.float32),
        compiler_params=pltpu.CompilerParams(dimension_semantics=("parallel", "parallel"),
                                             vmem_limit_bytes=VMEM_LIMIT_BYTES),
        name="mix_out_proj",
    )(x, gate, rw, ml, hy, w_out.astype(jnp.bfloat16))


def _mix_and_retrieve(li, x, c, ctx, c_ctx, mod_w, mod_b, mix_norm, w_in, w_out, rw_conv, rw_decay_up, rw_decay0, rw_a_up, rw_a0, rw_gate_up, rw_k_k, rw_k_a, rw_r_k, rw_gn_g, rw_gn_b, mla_q_norm, mla_w_uq, mla_kv_norm, mla_w_ukv, mla_q_gain, mla_k_gain, hy_conv, hy_w1, hy_b1, hy_freq1, hy_w2, hy_b2, hy_freq2, hy_w3, hy_b3, hy_bias, ffn_norm, peer_wq, peer_keys, peer_u, peer_v):
    B, L, D = x.shape
    need_ctx = li < DEPTH - 1
    mod_l = (jax.nn.silu(c) @ mod_w[li] + mod_b[li])[:, None, :]
    mod_c = (jax.nn.silu(c_ctx) @ mod_w[li] + mod_b[li])[None, None, :]
    shm_l, scm_l, gm_l, shf_l, scf_l, gf_l = jnp.split(mod_l, N_MOD, axis=-1)
    shm_c, scm_c, gm_c, shf_c, scf_c, gf_c = jnp.split(mod_c, N_MOD, axis=-1)

    widths = (RW_PROJ, MLA_PROJ, HY_PROJ)
    prw_l, pml_l, phy_l = norm_mod_proj(x, mix_norm[li], scm_l, shm_l, w_in[li], widths, 512)
    prw_c, pml_c, phy_c = norm_mod_proj(ctx, mix_norm[li], jnp.broadcast_to(scm_c, (B, 1, D)),
                                        jnp.broadcast_to(shm_c, (B, 1, D)), w_in[li], widths, 256)
    rw_l, rw_c = rwkv7_mixer(prw_l, prw_c, rw_conv[li], rw_decay_up[li], rw_decay0[li],
                             rw_a_up[li], rw_a0[li], rw_gate_up[li], rw_k_k[li], rw_k_a[li], rw_r_k[li],
                             rw_gn_g[li], rw_gn_b[li], need_ctx)
    ml_l, ml_c = mla_mixer(pml_l, pml_c, mla_q_norm[li], mla_w_uq[li],
                           mla_kv_norm[li], mla_w_ukv[li], mla_q_gain[li], mla_k_gain[li], need_ctx)
    hy_prm = (hy_conv[li], hy_w1[li], hy_b1[li], hy_freq1[li], hy_w2[li], hy_b2[li], hy_freq2[li],
              hy_w3[li], hy_b3[li], hy_bias[li])
    hy_l = hyena_mixer(phy_l, *hy_prm)
    x = mix_out_proj(x, gm_l, rw_l, ml_l, hy_l, w_out[li])
    h, e_idx, gate = peer_retrieve(x, ffn_norm[li], scf_l, shf_l, peer_wq[li], peer_keys[li])
    streams = [(x, gf_l, h.reshape(B * L, D), e_idx, gate)]
    if need_ctx:
        hy_c = hyena_mixer(phy_c, *hy_prm)
        ctx = mix_out_proj(ctx, jnp.broadcast_to(gm_c, (B, 1, D)), rw_c, ml_c, hy_c, w_out[li])
        h, e_idx, gate = peer_retrieve(ctx, ffn_norm[li], jnp.broadcast_to(scf_c, (B, 1, D)),
                                       jnp.broadcast_to(shf_c, (B, 1, D)), peer_wq[li], peer_keys[li])
        streams.append((ctx, gf_c, h.reshape(-1, D), e_idx, gate))
    return streams


BATCH_GROUP_ROWS = (2, 2, 2, 2)


def kernel(x, c, ctx, c_ctx, mod_w, mod_b, mix_norm, w_in, w_out, rw_conv, rw_decay_up, rw_decay0, rw_a_up, rw_a0, rw_gate_up, rw_k_k, rw_k_a, rw_r_k, rw_gn_g, rw_gn_b, mla_q_norm, mla_w_uq, mla_kv_norm, mla_w_ukv, mla_q_gain, mla_k_gain, hy_conv, hy_w1, hy_b1, hy_freq1, hy_w2, hy_b2, hy_freq2, hy_w3, hy_b3, hy_bias, ffn_norm, peer_wq, peer_keys, peer_u, peer_v):
    params = (mod_w, mod_b, mix_norm, w_in, w_out, rw_conv, rw_decay_up, rw_decay0, rw_a_up, rw_a0, rw_gate_up,
              rw_k_k, rw_k_a, rw_r_k, rw_gn_g, rw_gn_b, mla_q_norm, mla_w_uq, mla_kv_norm, mla_w_ukv, mla_q_gain,
              mla_k_gain, hy_conv, hy_w1, hy_b1, hy_freq1, hy_w2, hy_b2, hy_freq2, hy_w3, hy_b3, hy_bias,
              ffn_norm, peer_wq, peer_keys)
    peer_u = [pack_expert_table(peer_u[li]) for li in range(DEPTH)]
    peer_v = [pack_expert_table(peer_v[li]) for li in range(DEPTH)]
    params = params + (peer_u, peer_v)
    assert sum(BATCH_GROUP_ROWS) == x.shape[0]
    G = len(BATCH_GROUP_ROWS)
    lo = [sum(BATCH_GROUP_ROWS[:g]) for g in range(G + 1)]
    L, D = x.shape[1:]
    xs = [x[lo[g]:lo[g + 1]] for g in range(G)]
    cs = [c[lo[g]:lo[g + 1]] for g in range(G)]
    ctxs = [ctx[lo[g]:lo[g + 1]] for g in range(G)]
    stages = [(li, g) for li in range(DEPTH) for g in range(G)]
    prev = None
    token = None

    def advance(prev, token, li, streams):
        pstreams, pdots = ([], []) if prev is None else (prev[2], prev[3])
        if pdots:
            token, pdots = lax.optimization_barrier((token, pdots))
        dots = []
        for s in range(max(len(pstreams), len(streams))):
            jobs = []
            if s < len(pstreams):
                w = peer_act(pdots[s], pstreams[s][4])
                token = w if s == 0 else token
                jobs.append(("wsum", peer_v[prev[0]], pstreams[s][3], w))
            if s < len(streams):
                jobs.append(("dot", peer_u[li], streams[s][3], streams[s][2]))
            outs = _sc_peer_jobs(jobs)
            if s < len(pstreams):
                res = pstreams[s][0] + pstreams[s][1] * outs[0].reshape(pstreams[s][0].shape)
                (xs if s == 0 else ctxs)[prev[1]] = res
            if s < len(streams):
                dots.append(outs[-1])
        return token, dots

    for li, g in stages:
        ins = (xs[g], ctxs[g])
        if token is not None:
            token, ins = lax.optimization_barrier((token, ins))
        streams = _mix_and_retrieve(li, ins[0], cs[g], ins[1], c_ctx, *params)
        token, dots = advance(prev, streams[0][4], li, streams)
        prev = (li, g, streams, dots)
    advance(prev, token, None, [])
    return jnp.concatenate(xs, axis=0)
```

```python
import functools
import math

import jax
import jax.numpy as jnp
import numpy as np
from jax import lax
from jax.experimental import pallas as pl
from jax.experimental.pallas import tpu as pltpu
from jax.experimental.pallas import tpu_sc as plsc

D_MODEL = 1024
DEPTH = 2
GRID_W = 64
N_MOD = 6
NORM_EPS = 1e-6
SHORT_CONV = 3

RW_HEADS = 6
RW_HEAD_DIM = 64
RW_WIDTH = RW_HEADS * RW_HEAD_DIM
RW_DECAY_RANK = 64
RW_A_RANK = 64
RW_GATE_RANK = 128
RW_DECAY_SCALE = 0.6065306597
RW_GN_EPS = 64e-5
L2_EPS = 1e-12

MLA_HEADS = 6
MLA_Q_RANK = 256
MLA_KV_RANK = 128
MLA_NOPE_DIM = 64
MLA_ROPE_DIM = 32
MLA_V_DIM = 64
MLA_QK_DIM = MLA_NOPE_DIM + MLA_ROPE_DIM
MLA_WIDTH = MLA_HEADS * MLA_V_DIM
AXIS_ROPE_DIM = MLA_ROPE_DIM // 2
ROPE_THETA = 10000.0

HY_WIDTH = 256
HY_ORDER = 2
HY_POS_BANDS = 16
HY_SHORT_DECAY_PCT = 0.3
HY_LONG_DECAY_PCT = 1.5
HY_DECAY_TARGET = 1e-2

PEER_HEADS = 8
PEER_N_KEYS = 128
PEER_TOPK = 16
PEER_QUERY_DIM = 256
PEER_HALF = PEER_QUERY_DIM // 2

RW_PROJ = 3 * RW_WIDTH + RW_DECAY_RANK + RW_A_RANK + RW_GATE_RANK
MLA_PROJ = MLA_Q_RANK + MLA_KV_RANK + MLA_ROPE_DIM
HY_PROJ = (HY_ORDER + 1) * HY_WIDTH
MIX_WIDTH = RW_WIDTH + MLA_WIDTH + HY_WIDTH

VMEM_LIMIT_BYTES = 48 * 1024 * 1024


def _norm_mod_proj_kernel(x_ref, gain_ref, scale_ref, shift_ref, *refs):
    w_refs, o_refs = refs[:len(refs) // 2], refs[len(refs) // 2:]
    x = x_ref[0]
    y = x * lax.rsqrt(jnp.mean(x * x, axis=-1, keepdims=True) + NORM_EPS)
    y = y * gain_ref[...]
    y = (y * (1.0 + scale_ref[0]) + shift_ref[0]).astype(jnp.bfloat16)
    for w_ref, o_ref in zip(w_refs, o_refs):
        o_ref[0] = jnp.dot(y, w_ref[...], preferred_element_type=jnp.float32)


def norm_mod_proj(x, gain, scale, shift, w, widths, block_rows):
    B, L, D = x.shape
    assert sum(widths) == w.shape[1]
    offs = [sum(widths[:i]) for i in range(len(widths))]
    ws = [w[:, o:o + n].astype(jnp.bfloat16) for o, n in zip(offs, widths)]
    return pl.pallas_call(
        _norm_mod_proj_kernel,
        grid=(B, L // block_rows),
        in_specs=[
            pl.BlockSpec((1, block_rows, D), lambda b, i: (b, i, 0)),
            pl.BlockSpec((1, D), lambda b, i: (0, 0)),
            pl.BlockSpec((1, 1, D), lambda b, i: (b, 0, 0)),
            pl.BlockSpec((1, 1, D), lambda b, i: (b, 0, 0)),
        ] + [pl.BlockSpec((D, n), lambda b, i: (0, 0)) for n in widths],
        out_specs=[pl.BlockSpec((1, block_rows, n), lambda b, i: (b, i, 0)) for n in widths],
        out_shape=[jax.ShapeDtypeStruct((B, L, n), jnp.float32) for n in widths],
        compiler_params=pltpu.CompilerParams(
            dimension_semantics=("parallel", "parallel"), vmem_limit_bytes=VMEM_LIMIT_BYTES),
        name="norm_mod_proj",
    )(x, gain.reshape(1, D), scale, shift, *ws)


CONV_ROWS = 512
SUBLANES = 8


def _short_conv_kernel(x_ref, prev_ref, next_ref, w_ref, o_ref):
    i = pl.program_id(1)
    x = x_ref[0]
    T = x.shape[0]
    row = lax.broadcasted_iota(jnp.int32, x.shape, 0)
    before = jnp.where(i == 0, 0.0, prev_ref[0, SUBLANES - 1:SUBLANES, :])
    after = jnp.where(i == pl.num_programs(1) - 1, 0.0, next_ref[0, 0:1, :])
    up = jnp.where(row == 0, before, pltpu.roll(x, 1, 0))
    down = jnp.where(row == T - 1, after, pltpu.roll(x, T - 1, 0))
    o_ref[0] = up * w_ref[0:1, :] + x * w_ref[1:2, :] + down * w_ref[2:3, :]


def short_conv(x, w):
    B, L, C = x.shape
    T = min(CONV_ROWS, L)
    per = T // SUBLANES
    last = L // SUBLANES - 1
    return pl.pallas_call(
        _short_conv_kernel,
        grid=(B, L // T),
        in_specs=[pl.BlockSpec((1, T, C), lambda b, i: (b, i, 0)),
                  pl.BlockSpec((1, SUBLANES, C), lambda b, i: (b, jnp.maximum(i * per - 1, 0), 0)),
                  pl.BlockSpec((1, SUBLANES, C), lambda b, i: (b, jnp.minimum((i + 1) * per, last), 0)),
                  pl.BlockSpec((SHORT_CONV, C), lambda b, i: (0, 0))],
        out_specs=pl.BlockSpec((1, T, C), lambda b, i: (b, i, 0)),
        out_shape=jax.ShapeDtypeStruct((B, L, C), jnp.float32),
        compiler_params=pltpu.CompilerParams(dimension_semantics=("parallel", "parallel"),
                                             vmem_limit_bytes=VMEM_LIMIT_BYTES),
        name="short_conv",
    )(x, x, x, w)


RW_CHUNK = 64


def _rwkv_chunk_kernel(r_ref, kk_ref, v_ref, lw_ref, akk_ref, kr_ref, y_ref, h_ref):
    d = pl.program_id(0)
    n = pl.program_id(2)

    @pl.when(n == 0)
    def _():
        h_ref[...] = jnp.zeros_like(h_ref)

    C = RW_CHUNK
    row = lax.broadcasted_iota(jnp.int32, (C, C), 0)
    col = lax.broadcasted_iota(jnp.int32, (C, C), 1)
    lag = (row - col) * (1 - 2 * d)
    before = lag > 0
    upto = lag >= 0
    tri = upto.astype(jnp.float32)
    eye = (row == col).astype(jnp.float32)
    bf = jnp.bfloat16
    f32 = jnp.float32

    def mm(a, b):
        return jnp.dot(a.astype(bf), b.astype(bf), preferred_element_type=f32)

    def mm_nt(a, b):
        return lax.dot_general(a.astype(bf), b.astype(bf), (((1,), (1,)), ((), ())), preferred_element_type=f32)

    def mm_tn(a, b):
        return lax.dot_general(a.astype(bf), b.astype(bf), (((0,), (0,)), ((), ())), preferred_element_type=f32)

    hs = range(RW_HEADS)
    HD = RW_HEAD_DIM
    heads = lambda t: [t[:, h * HD:(h + 1) * HD] for h in hs]
    r = heads(r_ref[0])
    kk = heads(kk_ref[0])
    v = heads(v_ref[0])
    lw = heads(lw_ref[0, 0])
    akk = heads(akk_ref[0, 0])
    kr = heads(kr_ref[0, 0])
    G = [jnp.dot(tri, lw[h], preferred_element_type=f32, precision=lax.Precision.HIGHEST) for h in hs]
    gtot = [jnp.sum(lw[h], axis=0, keepdims=True) for h in hs]
    Einv = [jnp.exp(-G[h]) for h in hs]
    At = [-kk[h] * jnp.exp(G[h] - lw[h]) for h in hs]
    Rt = [r[h] * jnp.exp(G[h]) for h in hs]
    Bt = [akk[h] * Einv[h] for h in hs]
    Kt = [kr[h] * Einv[h] for h in hs]
    X = [mm_nt(jnp.concatenate([At[h], Rt[h]], axis=0), jnp.concatenate([Bt[h], Kt[h]], axis=0)) for h in hs]
    M_ab = [jnp.where(before, X[h][:C, :C], 0.0) for h in hs]
    M_ak = [jnp.where(before, X[h][:C, C:], 0.0) for h in hs]
    A_rb = [jnp.where(upto, X[h][C:, :C], 0.0) for h in hs]
    A_rk = [jnp.where(upto, X[h][C:, C:], 0.0) for h in hs]
    MV = [mm(M_ak[h], v[h]) for h in hs]
    Mp = M_ab
    T = [eye + Mp[h] for h in hs]
    for _ in range(5):
        Mp = [jnp.dot(Mp[h], Mp[h], preferred_element_type=f32) for h in hs]
        T = [T[h] + jnp.dot(T[h], Mp[h], preferred_element_type=f32) for h in hs]
    WU = [jnp.dot(T[h], jnp.concatenate([At[h], MV[h]], axis=1), preferred_element_type=f32) for h in hs]
    H0 = [h_ref[h] for h in hs]
    Ehat = [jnp.exp(gtot[h] - G[h]) for h in hs]
    Om = [Rt[h] + mm(A_rb[h], WU[h][:, :HD]) for h in hs]
    Y0 = [mm(A_rb[h], WU[h][:, HD:]) + mm(A_rk[h], v[h]) for h in hs]
    BW = [mm_tn(akk[h] * Ehat[h], WU[h]) for h in hs]
    KV = [mm_tn(kr[h] * Ehat[h], v[h]) for h in hs]
    y_ref[0, 0] = jnp.concatenate([jnp.dot(Om[h], H0[h], preferred_element_type=f32) + Y0[h] for h in hs], axis=1)
    for h in hs:
        P = eye * jnp.exp(gtot[h]) + BW[h][:, :HD]
        h_ref[h] = jnp.dot(P, H0[h], preferred_element_type=f32) + BW[h][:, HD:] + KV[h]


def rwkv_chunked(r, kk, v, lw, akk, kr, n_ctx):
    B, T, W = r.shape
    H = W // RW_HEAD_DIM
    nc = n_ctx // RW_CHUNK
    nt = T // RW_CHUNK

    def chunk_of(d, n):
        bwd = jnp.where(n < nc, nc - 1 - n, nt - 1 - (n - nc))
        return jnp.where(d == 0, n, bwd)

    spec1 = pl.BlockSpec((1, RW_CHUNK, W), lambda d, b, n: (b, chunk_of(d, n), 0))
    spec2 = pl.BlockSpec((1, 1, RW_CHUNK, W), lambda d, b, n: (d, b, chunk_of(d, n), 0))
    return pl.pallas_call(
        _rwkv_chunk_kernel,
        grid=(2, B, nt),
        in_specs=[spec1, spec1, spec1, spec2, spec2, spec2],
        out_specs=spec2,
        out_shape=jax.ShapeDtypeStruct((2, B, T, W), jnp.float32),
        scratch_shapes=[pltpu.VMEM((H, RW_HEAD_DIM, RW_HEAD_DIM), jnp.float32)],
        compiler_params=pltpu.CompilerParams(dimension_semantics=("parallel", "parallel", "arbitrary")),
        name="rwkv_chunked",
    )(r, kk, v, lw, akk, kr)


LANE = 128
RW_PREP_ROWS = 256
MLA_PAD_WIDTH = MLA_HEADS * LANE
MLA_PREP_ROWS = 256
ATTN_Q_ROWS = 512


def _split_dot(x, m):
    hi = x.astype(jnp.bfloat16)
    lo = (x - hi.astype(jnp.float32)).astype(jnp.bfloat16)
    return (jnp.dot(hi, m, preferred_element_type=jnp.float32) + jnp.dot(lo, m, preferred_element_type=jnp.float32))


def _rwkv_prep_kernel(z_ref, wda_ref, d0_ref, a0_ref, gup_ref, kk_ref_w, ka_ref, rk_ref, hsum_ref,
                      r_ref, kk_ref, v_ref, lw_ref, akk_ref, kr_ref, g_ref, bonus_ref):
    W = RW_WIDTH
    bf = jnp.bfloat16
    z = z_ref[0]
    r, k, v = z[:, :W], z[:, W:2 * W], z[:, 2 * W:3 * W]
    da = z[:, 3 * W:3 * W + LANE]
    lane = lax.broadcasted_iota(jnp.int32, da.shape, 1)
    da = jnp.where(lane < RW_DECAY_RANK, jnp.tanh(da), da)
    up = jnp.dot(da.astype(bf), wda_ref[...], preferred_element_type=jnp.float32)
    g_lo = z[:, 3 * W + LANE:]
    g_ref[0] = jnp.dot(jax.nn.sigmoid(g_lo).astype(bf), gup_ref[...], preferred_element_type=jnp.float32)
    hsum = hsum_ref[...]
    kk = k * kk_ref_w[...]
    kk = kk * lax.rsqrt(_split_dot(kk * kk, hsum) + L2_EPS)
    r_ref[0] = r
    v_ref[0] = v
    kk_ref[0] = kk
    bonus_ref[0] = _split_dot(r * k * rk_ref[...], hsum) * v
    for d in range(2):
        lw_ref[d, 0] = -RW_DECAY_SCALE * jax.nn.sigmoid(d0_ref[d:d + 1, :] + up[:, d * W:(d + 1) * W])
        a = jax.nn.sigmoid(a0_ref[d:d + 1, :] + up[:, (2 + d) * W:(3 + d) * W])
        akk_ref[d, 0] = kk * a
        kr_ref[d, 0] = k * (1.0 + (a - 1.0) * ka_ref[...])


def rwkv_prep(z, decay_up, decay0, a_up, a0, gate_up, k_k, k_a, r_k):
    B, L, _ = z.shape
    W = RW_WIDTH
    T = min(RW_PREP_ROWS, L)
    zero = jnp.zeros((RW_DECAY_RANK, 2 * W), jnp.float32)
    wda = jnp.concatenate([
        jnp.concatenate([decay_up[0], decay_up[1], zero], axis=1),
        jnp.concatenate([zero, a_up[0], a_up[1]], axis=1)], axis=0).astype(jnp.bfloat16)
    head = jnp.arange(W) // RW_HEAD_DIM
    hsum = (head[:, None] == head[None, :]).astype(jnp.bfloat16)
    row = lambda a: a.reshape(1, W)
    const = lambda a: pl.BlockSpec(a.shape, lambda b, i: (0,) * a.ndim)
    tok = pl.BlockSpec((1, T, W), lambda b, i: (b, i, 0))
    tok2 = pl.BlockSpec((2, 1, T, W), lambda b, i: (0, b, i, 0))
    f1 = jax.ShapeDtypeStruct((B, L, W), jnp.float32)
    f2 = jax.ShapeDtypeStruct((2, B, L, W), jnp.float32)
    args = (z, wda, decay0, a0, gate_up.astype(jnp.bfloat16), row(k_k), row(k_a), row(r_k), hsum)
    return pl.pallas_call(
        _rwkv_prep_kernel,
        grid=(B, L // T),
        in_specs=[pl.BlockSpec((1, T, RW_PROJ), lambda b, i: (b, i, 0))] + [const(a) for a in args[1:]],
        out_specs=[tok, tok, tok, tok2, tok2, tok2, tok, tok],
        out_shape=[f1, f1, f1, f2, f2, f2, f1, f1],
        compiler_params=pltpu.CompilerParams(dimension_semantics=("parallel", "parallel"),
                                             vmem_limit_bytes=VMEM_LIMIT_BYTES),
        name="rwkv_prep",
    )(*args)


def _rwkv_readout_kernel(y_ref, g_ref, bonus_ref, gng_ref, gnb_ref, hsum_ref, o_ref):
    y = y_ref[0, 0] + y_ref[1, 0]
    hsum = hsum_ref[...]
    mu = _split_dot(y, hsum) * (1.0 / RW_HEAD_DIM)
    d = y - mu
    var = _split_dot(d * d, hsum) * (1.0 / RW_HEAD_DIM)
    yn = d * lax.rsqrt(var + RW_GN_EPS) * gng_ref[...] + gnb_ref[...]
    o_ref[0] = (yn + bonus_ref[0]) * g_ref[0]


def rwkv_readout(y, g, bonus, gn_g, gn_b, t0):
    B, L, W = g.shape
    T = min(RW_PREP_ROWS, L)
    off = t0 // T
    head = jnp.arange(W) // RW_HEAD_DIM
    hsum = (head[:, None] == head[None, :]).astype(jnp.bfloat16)
    tok = pl.BlockSpec((1, T, W), lambda b, i: (b, i, 0))
    const = lambda a: pl.BlockSpec(a.shape, lambda b, i: (0,) * a.ndim)
    gg, gb = gn_g.reshape(1, W), gn_b.reshape(1, W)
    return pl.pallas_call(
        _rwkv_readout_kernel,
        grid=(B, L // T),
        in_specs=[pl.BlockSpec((2, 1, T, W), lambda b, i: (0, b, i + off, 0)), tok, tok, const(gg), const(gb), const(hsum)],
        out_specs=tok,
        out_shape=jax.ShapeDtypeStruct((B, L, W), jnp.float32),
        compiler_params=pltpu.CompilerParams(dimension_semantics=("parallel", "parallel")),
        name="rwkv_readout",
    )(y, g, bonus, gg, gb, hsum)


def rwkv7_mixer(p_lat, p_ctx, conv_w, decay_up, decay0, a_up, a0, gate_up, k_k, k_a, r_k, gn_g, gn_b, need_ctx):
    prm = (decay_up, decay0, a_up, a0, gate_up, k_k, k_a, r_k)
    lat = rwkv_prep(short_conv(p_lat, conv_w), *prm)
    ctx = rwkv_prep(short_conv(p_ctx, conv_w), *prm)
    n_ctx = p_ctx.shape[1]
    seq = lambda i: jnp.concatenate([ctx[i], lat[i]], axis=-2)
    y = rwkv_chunked(seq(0), seq(1), seq(2), seq(3), seq(4), seq(5), n_ctx)
    out_l = rwkv_readout(y, lat[6], lat[7], gn_g, gn_b, n_ctx)
    out_c = rwkv_readout(y, ctx[6], ctx[7], gn_g, gn_b, 0) if need_ctx else None
    return out_l, out_c


def _rope_tables(L, use_rope):
    lane = np.arange(LANE)
    in_rope = (lane >= MLA_NOPE_DIM) & (lane < MLA_QK_DIM)
    j = lane - MLA_NOPE_DIM
    axis = j // AXIS_ROPE_DIM
    half = AXIS_ROPE_DIM // 2
    f = j % half
    first = (j % AXIS_ROPE_DIM) < half
    inv = ROPE_THETA ** (-jnp.arange(0, AXIS_ROPE_DIM, 2, dtype=jnp.float32) / AXIS_ROPE_DIM)
    t = jnp.arange(L)
    pos = jnp.stack([t // GRID_W, t % GRID_W], axis=-1).astype(jnp.float32)
    ang = pos[:, np.clip(axis, 0, 1)] * inv[np.clip(f, 0, half - 1)][None, :]
    rope_on = jnp.asarray(in_rope)[None, :] & use_rope
    cos = jnp.where(rope_on, jnp.cos(ang), 1.0)
    sin = jnp.where(rope_on, jnp.sin(ang) * jnp.where(jnp.asarray(first), -1.0, 1.0)[None, :], 0.0)
    return jnp.tile(cos, (1, MLA_HEADS)), jnp.tile(sin, (1, MLA_HEADS))


def _mla_prep_kernel(p_ref, qn_ref, wq_ref, kvn_ref, wk_ref, wv_ref, place_ref, qg_ref, kg_ref, hsum_ref, cos_ref, sin_ref,
                     q_ref, k_ref, v_ref):
    bf = jnp.bfloat16
    p = p_ref[0]
    c_q = p[:, :MLA_Q_RANK]
    c_kv = p[:, MLA_Q_RANK:MLA_Q_RANK + MLA_KV_RANK]
    tail = p[:, MLA_Q_RANK + MLA_KV_RANK:]
    cqn = c_q * lax.rsqrt(jnp.mean(c_q * c_q, axis=-1, keepdims=True) + NORM_EPS) * qn_ref[...]
    ckn = c_kv * lax.rsqrt(jnp.mean(c_kv * c_kv, axis=-1, keepdims=True) + NORM_EPS) * kvn_ref[...]
    q = jnp.dot(cqn.astype(bf), wq_ref[...], preferred_element_type=jnp.float32)
    k = jnp.dot(ckn.astype(bf), wk_ref[...], preferred_element_type=jnp.float32) + _split_dot(tail, place_ref[...])
    v_ref[0] = jnp.dot(ckn.astype(bf), wv_ref[...], preferred_element_type=jnp.float32).astype(bf)
    hsum = hsum_ref[...]
    cos, sin = cos_ref[...], sin_ref[...]
    lane = lax.broadcasted_iota(jnp.int32, q.shape, 1)
    first = ((lane - MLA_NOPE_DIM) % AXIS_ROPE_DIM) < (AXIS_ROPE_DIM // 2)
    half = AXIS_ROPE_DIM // 2

    def finish(x, gain):
        x = x * lax.rsqrt(_split_dot(x * x, hsum) * (1.0 / MLA_QK_DIM) + NORM_EPS) * gain
        partner = jnp.where(first, pltpu.roll(x, MLA_PAD_WIDTH - half, 1), pltpu.roll(x, half, 1))
        return x * cos + partner * sin

    q_ref[0] = (finish(q, qg_ref[...]) * (MLA_QK_DIM ** -0.5)).astype(bf)
    k_ref[0] = finish(k, kg_ref[...]).astype(bf)


def mla_prep(p, use_rope, q_norm, w_uq, kv_norm, w_ukv, q_gain, k_gain):
    B, L, _ = p.shape
    T = min(MLA_PREP_ROWS, L)
    H = MLA_HEADS
    pad_cols = lambda w, d: jnp.pad(w.reshape(w.shape[0], H, d), ((0, 0), (0, 0), (0, LANE - d))).reshape(w.shape[0], H * LANE)
    wq = pad_cols(w_uq, MLA_QK_DIM).astype(jnp.bfloat16)
    ukv = w_ukv.reshape(MLA_KV_RANK, H, MLA_NOPE_DIM + MLA_V_DIM)
    wk = pad_cols(ukv[:, :, :MLA_NOPE_DIM].reshape(MLA_KV_RANK, H * MLA_NOPE_DIM), MLA_NOPE_DIM).astype(jnp.bfloat16)
    wv = ukv[:, :, MLA_NOPE_DIM:].reshape(MLA_KV_RANK, H * MLA_V_DIM).astype(jnp.bfloat16)
    lane = np.arange(H * LANE)
    place = jnp.asarray(((lane[None, :] % LANE) - MLA_NOPE_DIM == np.arange(MLA_ROPE_DIM)[:, None]), jnp.bfloat16)
    hsum = jnp.asarray((lane[:, None] // LANE) == (lane[None, :] // LANE), jnp.bfloat16)
    pad_gain = lambda g: jnp.tile(jnp.pad(g, (0, LANE - MLA_QK_DIM)), H).reshape(1, H * LANE)
    cos, sin = _rope_tables(L, use_rope)
    const = lambda a: pl.BlockSpec(a.shape, lambda b, i: (0,) * a.ndim)
    args = (p, q_norm.reshape(1, -1), wq, kv_norm.reshape(1, -1), wk, wv, place, pad_gain(q_gain), pad_gain(k_gain), hsum)
    pos = pl.BlockSpec((T, H * LANE), lambda b, i: (i, 0))
    return pl.pallas_call(
        _mla_prep_kernel,
        grid=(B, L // T),
        in_specs=[pl.BlockSpec((1, T, MLA_PROJ), lambda b, i: (b, i, 0))] + [const(a) for a in args[1:]] + [pos, pos],
        out_specs=[pl.BlockSpec((1, T, H * LANE), lambda b, i: (b, i, 0)), pl.BlockSpec((1, T, H * LANE), lambda b, i: (b, i, 0)),
                   pl.BlockSpec((1, T, MLA_WIDTH), lambda b, i: (b, i, 0))],
        out_shape=[jax.ShapeDtypeStruct((B, L, H * LANE), jnp.bfloat16), jax.ShapeDtypeStruct((B, L, H * LANE), jnp.bfloat16),
                   jax.ShapeDtypeStruct((B, L, MLA_WIDTH), jnp.bfloat16)],
        compiler_params=pltpu.CompilerParams(dimension_semantics=("parallel", "parallel"),
                                             vmem_limit_bytes=VMEM_LIMIT_BYTES),
        name="mla_prep",
    )(*args, cos, sin)


def _attn_kernel(q_ref, k_ref, v_ref, o_ref):
    lane = lax.broadcasted_iota(jnp.int32, (q_ref.shape[1], LANE), 1)
    for pair in range(MLA_HEADS // 2):
        v_pair = v_ref[0, :, pair * LANE:(pair + 1) * LANE]
        outs = []
        for h in (2 * pair, 2 * pair + 1):
            q = q_ref[0, :, h * LANE:(h + 1) * LANE]
            k = k_ref[0, :, h * LANE:(h + 1) * LANE]
            s = lax.dot_general(q, k, (((1,), (1,)), ((), ())), preferred_element_type=jnp.float32)
            e = jnp.exp(s - jnp.max(s, axis=-1, keepdims=True))
            o = jnp.dot(e.astype(jnp.bfloat16), v_pair, preferred_element_type=jnp.float32)
            outs.append(o / jnp.sum(e, axis=-1, keepdims=True))
        o_ref[0, :, pair * LANE:(pair + 1) * LANE] = jnp.where(lane < MLA_V_DIM, outs[0], outs[1])


def attention(q, k, v):
    B, Lq, P = q.shape
    Lk = k.shape[1]
    tq = min(ATTN_Q_ROWS, Lq)
    return pl.pallas_call(
        _attn_kernel,
        grid=(B, Lq // tq),
        in_specs=[pl.BlockSpec((1, tq, P), lambda b, i: (b, i, 0)),
                  pl.BlockSpec((1, Lk, P), lambda b, i: (b, 0, 0)),
                  pl.BlockSpec((1, Lk, MLA_WIDTH), lambda b, i: (b, 0, 0))],
        out_specs=pl.BlockSpec((1, tq, MLA_WIDTH), lambda b, i: (b, i, 0)),
        out_shape=jax.ShapeDtypeStruct((B, Lq, MLA_WIDTH), jnp.float32),
        compiler_params=pltpu.CompilerParams(dimension_semantics=("parallel", "parallel"),
                                             vmem_limit_bytes=VMEM_LIMIT_BYTES),
        name="mla_attention",
    )(q, k, v)


def mla_mixer(p_lat, p_ctx, q_norm, w_uq, kv_norm, w_ukv, q_gain, k_gain, need_ctx):
    prm = (q_norm, w_uq, kv_norm, w_ukv, q_gain, k_gain)
    q_l, k_l, v_l = mla_prep(p_lat, True, *prm)
    q_c, k_c, v_c = mla_prep(p_ctx, False, *prm)
    y_l = attention(q_l, jnp.concatenate([k_l, k_c], axis=1), jnp.concatenate([v_l, v_c], axis=1))
    y_c = attention(q_c, k_c, v_c) if need_ctx else None
    return y_l, y_c


HY_FILTER_ROWS = 256


def _hyena_filter_kernel(z_ref, tn_ref, w1_ref, b1_ref, f1_ref, w2_ref, b2_ref, f2_ref, w3_ref, b3_ref, rates_ref, o_ref):
    hp = lax.Precision.HIGHEST
    h = jnp.sin(f1_ref[...] * (jnp.dot(z_ref[...], w1_ref[...], precision=hp, preferred_element_type=jnp.float32) + b1_ref[...]))
    h = jnp.sin(f2_ref[...] * (jnp.dot(h, w2_ref[...], precision=hp, preferred_element_type=jnp.float32) + b2_ref[...]))
    h = jnp.dot(h, w3_ref[...], precision=hp, preferred_element_type=jnp.float32) + b3_ref[...]
    o_ref[...] = h * jnp.exp(-tn_ref[...] * rates_ref[...])


def hyena_filters(L, w1, b1, freq1, w2, b2, freq2, w3, b3):
    tn = jnp.arange(L, dtype=jnp.float32) / L
    bands = jnp.arange(1, HY_POS_BANDS + 1, dtype=jnp.float32)
    ang = 2.0 * math.pi * tn[:, None] * bands[None, :]
    z = jnp.concatenate([tn[:, None], jnp.cos(ang), jnp.sin(ang)], axis=-1)
    rates = jnp.abs(jnp.linspace(math.log(HY_DECAY_TARGET) / HY_LONG_DECAY_PCT,
                                 math.log(HY_DECAY_TARGET) / HY_SHORT_DECAY_PCT, HY_WIDTH))
    pad = -z.shape[1] % 8
    T = min(HY_FILTER_ROWS, L)
    n_out = w3.shape[1]
    row = lambda a: a.reshape(1, -1)
    const = lambda a: pl.BlockSpec(a.shape, lambda i: (0,) * a.ndim)
    args = (jnp.pad(w1, ((0, pad), (0, 0))), row(b1), row(freq1), w2, row(b2), row(freq2), w3, row(b3),
            row(jnp.tile(rates, n_out // HY_WIDTH)))
    h = pl.pallas_call(
        _hyena_filter_kernel,
        grid=(L // T,),
        in_specs=[pl.BlockSpec((T, z.shape[1] + pad), lambda i: (i, 0)), pl.BlockSpec((T, 1), lambda i: (i, 0))]
                 + [const(a) for a in args],
        out_specs=pl.BlockSpec((T, n_out), lambda i: (i, 0)),
        out_shape=jax.ShapeDtypeStruct((L, n_out), jnp.float32),
        compiler_params=pltpu.CompilerParams(dimension_semantics=("parallel",)),
        name="hyena_filter_mlp",
    )(jnp.pad(z, ((0, 0), (0, pad))), tn[:, None], *args)
    h = h.reshape(L, HY_ORDER, 2, HY_WIDTH)
    zero = jnp.zeros((1, HY_ORDER, HY_WIDTH), h.dtype)
    h_full = jnp.concatenate([h[:, :, 0], zero, h[:0:-1, :, 1]], axis=0)
    return h_full * lax.rsqrt(jnp.sum(jnp.square(h_full), axis=0, keepdims=True))


def fft_long_conv(u, h_full, bias):
    L = u.shape[1]
    uf = jnp.fft.rfft(u, n=2 * L, axis=1)
    hf = jnp.fft.rfft(h_full, n=2 * L, axis=0)
    y = jnp.fft.irfft(uf * hf[None], n=2 * L, axis=1)[:, :L]
    return y + u * bias


FFT_N1 = 64
FFT_N2 = 128
FFT_N = FFT_N1 * FFT_N2
HY_SEQS = 32


def _dft_tables(seqs):
    n1 = np.arange(FFT_N1)
    n2 = np.arange(FFT_N2)
    f64 = np.exp(-2j * np.pi * np.outer(n1, n1) / FFT_N1)
    f128 = np.exp(-2j * np.pi * np.outer(n2, n2) / FFT_N2)
    tw = np.exp(-2j * np.pi * np.outer(n1, n2) / FFT_N)
    half = FFT_N1 // 2
    fh = f64[:, :half]
    m1 = np.block([[fh.real, -fh.imag], [fh.imag, fh.real]])
    m1f = np.concatenate([f64.real, f64.imag], axis=0)
    m2 = np.block([[f128.real, f128.imag], [-f128.imag, f128.real]])
    m3 = np.block([[f128.real, -f128.imag], [f128.imag, f128.real]]) / FFT_N
    c = np.conj(f64)[:half, :]
    m4 = np.block([[c.real, -c.imag], [c.imag, c.real]])
    bf = lambda a: jnp.asarray(a, jnp.float32).astype(jnp.bfloat16)
    f32 = lambda a: jnp.asarray(a, jnp.float32)
    return dict(m1=bf(m1), m1f=bf(m1f), m2=bf(m2), m3=bf(m3), m4=bf(m4),
                twr_l=f32(np.tile(tw.real, (1, seqs))), twi_l=f32(np.tile(tw.imag, (1, seqs))),
                twr_s=f32(np.tile(tw.real, (seqs, 1))), twi_s=f32(np.tile(tw.imag, (seqs, 1))))


def _spectrum(cols, m1, twr_l, twi_l, m2, R):
    a = jnp.dot(m1, cols.astype(jnp.bfloat16), preferred_element_type=jnp.float32)
    ar, ai = a[:FFT_N1], a[FFT_N1:]
    pr = ar * twr_l - ai * twi_l
    pi = ar * twi_l + ai * twr_l
    lhs = jnp.concatenate(
        [jnp.concatenate([pr[:, r * FFT_N2:(r + 1) * FFT_N2], pi[:, r * FFT_N2:(r + 1) * FFT_N2]], axis=1)
         for r in range(R)], axis=0)
    return jnp.dot(lhs.astype(jnp.bfloat16), m2, preferred_element_type=jnp.float32)


def _filter_fft_kernel(h_ref, m1f_ref, twr_ref, twi_ref, m2_ref, o_ref):
    R = HY_SEQS
    cols = jnp.concatenate([h_ref[r] for r in range(R)], axis=1)
    x = _spectrum(cols, m1f_ref[...], twr_ref[...], twi_ref[...], m2_ref[...], R)
    o_ref[...] = x.reshape(R, FFT_N1, 2 * FFT_N2)


def _hyena_conv_kernel(y_ref, g_ref, hf_ref, bias_ref, m1_ref, twr_l_ref, twi_l_ref, m2_ref, m3_ref,
                       twr_s_ref, twi_s_ref, m4_ref, o_ref):
    R = HY_SEQS
    half = FFT_N1 // 2
    y = [y_ref[0], y_ref[1]]
    for o in range(HY_ORDER):
        top = jnp.concatenate([y[0][r] for r in range(R)], axis=1)
        bot = jnp.concatenate([y[1][r] for r in range(R)], axis=1)
        x = _spectrum(jnp.concatenate([top, bot], axis=0), m1_ref[...], twr_l_ref[...], twi_l_ref[...], m2_ref[...], R)
        hf = hf_ref[o].reshape(R * FFT_N1, 2 * FFT_N2)
        xr, xi = x[:, :FFT_N2], x[:, FFT_N2:]
        hr, hi = hf[:, :FFT_N2], hf[:, FFT_N2:]
        yc = jnp.concatenate([xr * hr - xi * hi, xr * hi + xi * hr], axis=1)
        b = jnp.dot(yc.astype(jnp.bfloat16), m3_ref[...], preferred_element_type=jnp.float32)
        br, bi = b[:, :FFT_N2], b[:, FFT_N2:]
        qr = br * twr_s_ref[...] + bi * twi_s_ref[...]
        qi = bi * twr_s_ref[...] - br * twi_s_ref[...]
        bc = jnp.concatenate(
            [jnp.concatenate([qr[r * FFT_N1:(r + 1) * FFT_N1], qi[r * FFT_N1:(r + 1) * FFT_N1]], axis=0)
             for r in range(R)], axis=1)
        yo = jnp.dot(m4_ref[...], bc.astype(jnp.bfloat16), preferred_element_type=jnp.float32)
        for p in range(2):
            conv = jnp.stack([yo[p * half:(p + 1) * half, r * FFT_N2:(r + 1) * FFT_N2] for r in range(R)], axis=0)
            y[p] = g_ref[o, p] * (conv + y[p] * bias_ref[o])
    o_ref[0] = y[0]
    o_ref[1] = y[1]


def hyena_long_conv(y_t, g_t, h_t, bias):
    B, C, L = y_t.shape
    assert 2 * L == FFT_N and B % 2 == 0 and C % HY_SEQS == 0
    R = HY_SEQS
    half = FFT_N1 // 2
    tb = _dft_tables(R)
    const = lambda a: pl.BlockSpec(a.shape, lambda *_: (0,) * a.ndim)
    hf = pl.pallas_call(
        _filter_fft_kernel,
        grid=(HY_ORDER * C // R,),
        in_specs=[pl.BlockSpec((R, FFT_N1, FFT_N2), lambda i: (i, 0, 0)),
                  const(tb['m1f']), const(tb['twr_l']), const(tb['twi_l']), const(tb['m2'])],
        out_specs=pl.BlockSpec((R, FFT_N1, 2 * FFT_N2), lambda i: (i, 0, 0)),
        out_shape=jax.ShapeDtypeStruct((HY_ORDER * C, FFT_N1, 2 * FFT_N2), jnp.float32),
        compiler_params=pltpu.CompilerParams(dimension_semantics=("parallel",), vmem_limit_bytes=VMEM_LIMIT_BYTES),
        name="hyena_filter_fft",
    )(h_t.reshape(HY_ORDER * C, FFT_N1, FFT_N2), tb['m1f'], tb['twr_l'], tb['twi_l'], tb['m2'])
    hf = hf.reshape(HY_ORDER, C, FFT_N1, 2 * FFT_N2)
    out = pl.pallas_call(
        _hyena_conv_kernel,
        grid=(B // 2, C // R),
        in_specs=[pl.BlockSpec((2, R, half, FFT_N2), lambda b, c: (b, c, 0, 0)),
                  pl.BlockSpec((HY_ORDER, 2, R, half, FFT_N2), lambda b, c: (0, b, c, 0, 0)),
                  pl.BlockSpec((HY_ORDER, R, FFT_N1, 2 * FFT_N2), lambda b, c: (0, c, 0, 0)),
                  pl.BlockSpec((HY_ORDER, R, 1, 1), lambda b, c: (0, c, 0, 0)),
                  const(tb['m1']), const(tb['twr_l']), const(tb['twi_l']), const(tb['m2']), const(tb['m3']),
                  const(tb['twr_s']), const(tb['twi_s']), const(tb['m4'])],
        out_specs=pl.BlockSpec((2, R, half, FFT_N2), lambda b, c: (b, c, 0, 0)),
        out_shape=jax.ShapeDtypeStruct((B, C, half, FFT_N2), jnp.float32),
        compiler_params=pltpu.CompilerParams(dimension_semantics=("parallel", "parallel"),
                                             vmem_limit_bytes=VMEM_LIMIT_BYTES),
        name="hyena_conv",
    )(y_t.reshape(B, C, half, FFT_N2), g_t.reshape(HY_ORDER, B, C, half, FFT_N2), hf,
      bias.reshape(HY_ORDER, C, 1, 1), tb['m1'], tb['twr_l'], tb['twi_l'], tb['m2'], tb['m3'],
      tb['twr_s'], tb['twi_s'], tb['m4'])
    return out.reshape(B, C, L)


def hyena_mixer(p, conv_w, w1, b1, freq1, w2, b2, freq2, w3, b3, bias):
    B, L = p.shape[:2]
    z = short_conv(p, conv_w)
    h_full = hyena_filters(L, w1, b1, freq1, w2, b2, freq2, w3, b3)
    if 2 * L == FFT_N:
        g_t = jnp.transpose(z[..., :HY_ORDER * HY_WIDTH].reshape(B, L, HY_ORDER, HY_WIDTH), (2, 0, 3, 1))
        y_t = jnp.swapaxes(z[..., HY_ORDER * HY_WIDTH:], 1, 2)
        y_t = hyena_long_conv(y_t, g_t, jnp.transpose(h_full, (1, 2, 0)), bias)
        return jnp.swapaxes(y_t, 1, 2)
    gates = (z[..., :HY_WIDTH], z[..., HY_WIDTH:2 * HY_WIDTH])
    y = z[..., 2 * HY_WIDTH:]
    for o in range(HY_ORDER):
        y = gates[o] * fft_long_conv(y, h_full[:, o], bias[o])
    return y


SC_CORES = 2
SC_SUBCORES = 16
SC_LANES = 16
SC_WORKERS = SC_CORES * SC_SUBCORES
PEER_SLOTS = PEER_HEADS * PEER_TOPK
PEER_GATHER_ROWS = 32
PEER_GATHERS = PEER_SLOTS // PEER_GATHER_ROWS
PEER_ACC_VREGS = 8
PEER_ROW_BUFFERS = 4
PEER_ROW_WORDS = D_MODEL // 2
HI_MASK = -65536


def pack_expert_table(t):
    b = lax.bitcast_convert_type(t.astype(jnp.bfloat16), jnp.uint16).astype(jnp.uint32)
    return lax.bitcast_convert_type(b[:, :PEER_ROW_WORDS] | (b[:, PEER_ROW_WORDS:] << 16), jnp.int32)


def _sc_peer_phase(phase, tpw):
    NBUF = PEER_ROW_BUFFERS
    AHEAD = NBUF - 1
    HW = PEER_ROW_WORDS

    def run(base, table_hbm, idx_hbm, aux_hbm, out_hbm, idx_v, aux_v, rows_v, out_v, sem_r, sem_i, sem_o):

        def gather(p, c, b):
            return pltpu.make_async_copy(table_hbm.at[idx_v.at[p, c]], rows_v.at[b], sem_r.at[b])

        def load_meta(t, p):
            return (pltpu.make_async_copy(idx_hbm.at[t], idx_v.at[p], sem_i.at[p]),
                    pltpu.make_async_copy(aux_hbm.at[t], aux_v.at[p], sem_i.at[p]))

        def store_out(t, p):
            return pltpu.make_async_copy(out_v.at[p], out_hbm.at[t], sem_o.at[p])

        def halves(word):
            return (plsc.bitcast(lax.shift_left(word, 16), jnp.float32), plsc.bitcast(word & HI_MASK, jnp.float32))

        def compute(p, c, b):
            if phase == "dot":
                lane = lax.iota(jnp.int32, SC_LANES)

                @pl.loop(0, PEER_GATHER_ROWS // SC_LANES)
                def _(g2):
                    vec = jnp.zeros((SC_LANES,), jnp.float32)
                    for gg in range(SC_LANES // PEER_ACC_VREGS):
                        row0 = g2 * SC_LANES + gg * PEER_ACC_VREGS

                        def body(cc, accs):
                            x_lo = aux_v[p, pl.ds(cc * SC_LANES, SC_LANES)]
                            x_hi = aux_v[p, pl.ds(HW + cc * SC_LANES, SC_LANES)]
                            out = []
                            for r in range(PEER_ACC_VREGS):
                                lo, hi = halves(rows_v[b, row0 + r, pl.ds(cc * SC_LANES, SC_LANES)])
                                out.append(accs[r] + lo * x_lo + hi * x_hi)
                            return tuple(out)
                        accs = lax.fori_loop(0, HW // SC_LANES, body,
                                             tuple(jnp.zeros((SC_LANES,), jnp.float32) for _ in range(PEER_ACC_VREGS)))
                        for r in range(PEER_ACC_VREGS):
                            vec = jnp.where(lane == gg * PEER_ACC_VREGS + r, jnp.sum(accs[r]), vec)
                    out_v[p, pl.ds(c * PEER_GATHER_ROWS + g2 * SC_LANES, SC_LANES)] = vec
            else:
                words = PEER_ACC_VREGS // 2

                @pl.loop(0, HW // (words * SC_LANES))
                def _(db):
                    def body(kk, accs):
                        wv = plsc.load_gather(aux_v.at[p], [jnp.full((SC_LANES,), c * PEER_GATHER_ROWS + kk, jnp.int32)])
                        out = []
                        for j in range(words):
                            lo, hi = halves(rows_v[b, kk, pl.ds((db * words + j) * SC_LANES, SC_LANES)])
                            out += [accs[2 * j] + lo * wv, accs[2 * j + 1] + hi * wv]
                        return tuple(out)
                    if c == 0:
                        init = tuple(jnp.zeros((SC_LANES,), jnp.float32) for _ in range(2 * words))
                    else:
                        init = tuple(out_v[p, pl.ds(half * HW + (db * words + j) * SC_LANES, SC_LANES)]
                                     for j in range(words) for half in range(2))
                    accs = lax.fori_loop(0, PEER_GATHER_ROWS, body, init)
                    for j in range(words):
                        out_v[p, pl.ds((db * words + j) * SC_LANES, SC_LANES)] = accs[2 * j]
                        out_v[p, pl.ds(HW + (db * words + j) * SC_LANES, SC_LANES)] = accs[2 * j + 1]

        for d in load_meta(base, 0):
            d.start()
        for d in load_meta(base, 0):
            d.wait()
        for c in range(AHEAD):
            gather(0, c, c % NBUF).start()

        @pl.loop(0, tpw // 2)
        def _(i2):
            for p in range(2):
                i = i2 * 2 + p
                t = base + i
                nxt = base + jnp.minimum(i + 1, tpw - 1)
                for d in load_meta(nxt, 1 - p):
                    d.start()

                @pl.when(i2 > 0)
                def _():
                    store_out(t, p).wait()

                for c in range(PEER_GATHERS):
                    ahead = c + AHEAD
                    if ahead < PEER_GATHERS:
                        gather(p, ahead, ahead % NBUF).start()
                    else:
                        if ahead == PEER_GATHERS:
                            for d in load_meta(nxt, 1 - p):
                                d.wait()
                        gather(1 - p, ahead - PEER_GATHERS, ahead % NBUF).start()
                    gather(p, c, c % NBUF).wait()
                    compute(p, c, c % NBUF)
                store_out(t, p).start()

        for c in range(AHEAD):
            gather(0, c, c % NBUF).wait()
        for p in range(2):
            store_out(base, p).wait()

    return run


def _sc_tokens_per_worker(N):
    assert N % (2 * SC_WORKERS) == 0 and PEER_GATHERS % PEER_ROW_BUFFERS == 0
    return N // SC_WORKERS


_SC_AUX = {"dot": (D_MODEL,), "wsum": (PEER_SLOTS,)}
_SC_OUT = {"dot": (PEER_SLOTS,), "wsum": (D_MODEL,)}


def _sc_scratch(phases):
    s = [pltpu.VMEM((2, PEER_GATHERS, PEER_GATHER_ROWS), jnp.int32),
         pltpu.VMEM((PEER_ROW_BUFFERS, PEER_GATHER_ROWS, PEER_ROW_WORDS), jnp.int32),
         pltpu.SemaphoreType.DMA((PEER_ROW_BUFFERS,)), pltpu.SemaphoreType.DMA((2,)), pltpu.SemaphoreType.DMA((2,))]
    for ph in phases:
        s += [pltpu.VMEM((2,) + _SC_AUX[ph], jnp.float32), pltpu.VMEM((2,) + _SC_OUT[ph], jnp.float32)]
    return s


def _sc_peer_jobs(jobs):
    phases = [j[0] for j in jobs]
    kinds = sorted(set(phases))
    ns = [j[2].shape[0] for j in jobs]
    runs = [_sc_peer_phase(ph, _sc_tokens_per_worker(n)) for ph, n in zip(phases, ns)]
    nj = len(jobs)

    @functools.partial(
        pl.kernel, mesh=plsc.VectorSubcoreMesh(core_axis_name="c", subcore_axis_name="s"),
        out_type=tuple(jax.ShapeDtypeStruct((n,) + _SC_OUT[ph], jnp.float32) for ph, n in zip(phases, ns)),
        compiler_params=pltpu.CompilerParams(needs_layout_passes=False),
        scratch_types=_sc_scratch(kinds),
    )
    def k(*refs):
        ins, outs, scratch = refs[:3 * nj], refs[3 * nj:4 * nj], refs[4 * nj:]
        idx_v, rows_v, sem_r, sem_i, sem_o = scratch[:5]
        bufs = {kind: scratch[5 + 2 * i:7 + 2 * i] for i, kind in enumerate(kinds)}
        worker = lax.axis_index("s") * SC_CORES + lax.axis_index("c")
        for j in range(nj):
            table_hbm, idx_hbm, aux_hbm = ins[3 * j:3 * j + 3]
            aux_v, out_v = bufs[phases[j]]
            runs[j](worker * (ns[j] // SC_WORKERS), table_hbm, idx_hbm, aux_hbm, outs[j],
                    idx_v, aux_v, rows_v, out_v, sem_r, sem_i, sem_o)

    args = []
    for (_, table, idx, aux), n in zip(jobs, ns):
        args += [table, idx.reshape(n, PEER_GATHERS, PEER_GATHER_ROWS), aux]
    return list(k(*args))


PEER_TOKENS = 256
INT_BIG = 2 ** 30
PEER_CANDIDATES = -(-sum(PEER_TOPK // (i + 1) for i in range(PEER_TOPK)) // 8) * 8


def _extract_topk(cand_ref, ids_ref, val_out_ref, id_out_ref, row0):
    def body(r, carry):
        c = cand_ref[...]
        ids = ids_ref[...]
        m = jnp.max(c, axis=0, keepdims=True)
        sel = jnp.min(jnp.where(c == m, ids, INT_BIG), axis=0, keepdims=True)
        cand_ref[...] = jnp.where(ids == sel, -jnp.inf, c)
        val_out_ref[pl.ds(row0 + r, 1), :] = m
        id_out_ref[pl.ds(row0 + r, 1), :] = sel
        return carry
    lax.fori_loop(0, PEER_TOPK, body, 0)


def _peer_retrieve_kernel(x_ref, gain_ref, scale_ref, shift_ref, wq_ref, keys_ref,
                          h_ref, idx_out_ref, gate_out_ref,
                          s_ref, ids1_ref, sv_ref, si_ref, cand_ref, cid_ref, ts_ref, idx_ref, gate_ref):
    x = x_ref[0]
    y = x * lax.rsqrt(jnp.mean(x * x, axis=-1, keepdims=True) + NORM_EPS)
    h = (y * gain_ref[...]) * (1.0 + scale_ref[0]) + shift_ref[0]
    h_ref[0] = h
    q = jnp.dot(h.astype(jnp.bfloat16), wq_ref[...], preferred_element_type=jnp.float32)
    T = PEER_TOKENS
    K = PEER_TOPK
    ids1_ref[...] = lax.broadcasted_iota(jnp.int32, (PEER_N_KEYS, T), 0)
    for hd in range(PEER_HEADS):
        for p in range(2):
            hp = hd * 2 + p
            qs = q[:, hp * PEER_HALF:(hp + 1) * PEER_HALF].astype(jnp.bfloat16)
            s_ref[...] = lax.dot_general(keys_ref[hp], qs, (((1,), (1,)), ((), ())),
                                         preferred_element_type=jnp.float32)
            _extract_topk(s_ref, ids1_ref, sv_ref, si_ref, p * K)
        cand_ref[...] = jnp.full(cand_ref.shape, -jnp.inf, jnp.float32)
        cid_ref[...] = INT_BIG - 1 - lax.broadcasted_iota(jnp.int32, cid_ref.shape, 0)
        off = 0
        for i in range(K):
            n = K // (i + 1)
            cand_ref[off:off + n, :] = sv_ref[i:i + 1, :] + sv_ref[K:K + n, :]
            cid_ref[off:off + n, :] = si_ref[i:i + 1, :] * PEER_N_KEYS + si_ref[K:K + n, :]
            off += n
        _extract_topk(cand_ref, cid_ref, ts_ref, idx_ref, hd * K)
        ts = ts_ref[hd * K:(hd + 1) * K, :]
        e = jnp.exp(ts - jnp.max(ts, axis=0, keepdims=True))
        gate_ref[hd * K:(hd + 1) * K, :] = e / jnp.sum(e, axis=0, keepdims=True)
    idx_out_ref[...] = idx_ref[...].T
    gate_out_ref[...] = gate_ref[...].T


def peer_retrieve(x, gain, scale, shift, w_q, sub_keys):
    B, L, D = x.shape
    T = PEER_TOKENS
    nt = L // T
    keys = sub_keys.reshape(PEER_HEADS * 2, PEER_N_KEYS, PEER_HALF).astype(jnp.bfloat16)
    return pl.pallas_call(
        _peer_retrieve_kernel,
        grid=(B, nt),
        in_specs=[
            pl.BlockSpec((1, T, D), lambda b, i: (b, i, 0)),
            pl.BlockSpec((1, D), lambda b, i: (0, 0)),
            pl.BlockSpec((1, 1, D), lambda b, i: (b, 0, 0)),
            pl.BlockSpec((1, 1, D), lambda b, i: (b, 0, 0)),
            pl.BlockSpec((D, PEER_HEADS * 2 * PEER_HALF), lambda b, i: (0, 0)),
            pl.BlockSpec((PEER_HEADS * 2, PEER_N_KEYS, PEER_HALF), lambda b, i: (0, 0, 0)),
        ],
        out_specs=[
            pl.BlockSpec((1, T, D), lambda b, i: (b, i, 0)),
            pl.BlockSpec((T, PEER_SLOTS), lambda b, i: (b * nt + i, 0)),
            pl.BlockSpec((T, PEER_SLOTS), lambda b, i: (b * nt + i, 0)),
        ],
        out_shape=[
            jax.ShapeDtypeStruct((B, L, D), jnp.float32),
            jax.ShapeDtypeStruct((B * L, PEER_SLOTS), jnp.int32),
            jax.ShapeDtypeStruct((B * L, PEER_SLOTS), jnp.float32),
        ],
        scratch_shapes=[
            pltpu.VMEM((PEER_N_KEYS, T), jnp.float32),
            pltpu.VMEM((PEER_N_KEYS, T), jnp.int32),
            pltpu.VMEM((2 * PEER_TOPK, T), jnp.float32),
            pltpu.VMEM((2 * PEER_TOPK, T), jnp.int32),
            pltpu.VMEM((PEER_CANDIDATES, T), jnp.float32),
            pltpu.VMEM((PEER_CANDIDATES, T), jnp.int32),
            pltpu.VMEM((PEER_SLOTS, T), jnp.float32),
            pltpu.VMEM((PEER_SLOTS, T), jnp.int32),
            pltpu.VMEM((PEER_SLOTS, T), jnp.float32),
        ],
        compiler_params=pltpu.CompilerParams(dimension_semantics=("parallel", "parallel"),
                                             vmem_limit_bytes=VMEM_LIMIT_BYTES),
        name="peer_retrieve",
    )(x, gain.reshape(1, D), scale, shift, w_q.astype(jnp.bfloat16), keys)


PEER_ACT_ROWS = 256


def _peer_act_kernel(dots_ref, gate_ref, w_ref):
    a = dots_ref[...]
    w_ref[...] = gate_ref[...] * (0.5 * a * (1.0 + lax.erf(a * (2.0 ** -0.5))))


def peer_act(dots, gate):
    N = dots.shape[0]
    T = min(PEER_ACT_ROWS, N)
    spec = pl.BlockSpec((T, PEER_SLOTS), lambda i: (i, 0))
    return pl.pallas_call(
        _peer_act_kernel,
        grid=(N // T,),
        in_specs=[spec, spec],
        out_specs=spec,
        out_shape=jax.ShapeDtypeStruct((N, PEER_SLOTS), jnp.float32),
        compiler_params=pltpu.CompilerParams(dimension_semantics=("parallel",)),
        name="peer_act",
    )(dots, gate)


OUT_PROJ_ROWS = 512


def _out_proj_kernel(x_ref, g_ref, rw_ref, ml_ref, hy_ref, w_ref, o_ref):
    bf = jnp.bfloat16
    y = jnp.dot(rw_ref[0].astype(bf), w_ref[:RW_WIDTH, :], preferred_element_type=jnp.float32)
    y += jnp.dot(ml_ref[0].astype(bf), w_ref[RW_WIDTH:RW_WIDTH + MLA_WIDTH, :], preferred_element_type=jnp.float32)
    y += jnp.dot(hy_ref[0].astype(bf), w_ref[RW_WIDTH + MLA_WIDTH:, :], preferred_element_type=jnp.float32)
    o_ref[0] = x_ref[0] + g_ref[0] * y


def mix_out_proj(x, gate, rw, ml, hy, w_out):
    B, L, D = x.shape
    T = min(OUT_PROJ_ROWS, L)
    tok = lambda w: pl.BlockSpec((1, T, w), lambda b, i: (b, i, 0))
    return pl.pallas_call(
        _out_proj_kernel,
        grid=(B, L // T),
        in_specs=[tok(D), pl.BlockSpec((1, 1, D), lambda b, i: (b, 0, 0)), tok(RW_WIDTH), tok(MLA_WIDTH), tok(HY_WIDTH),
                  pl.BlockSpec((MIX_WIDTH, D), lambda b, i: (0, 0))],
        out_specs=tok(D),
        out_shape=jax.ShapeDtypeStruct((B, L, D), jnp.float32),
        compiler_params=pltpu.CompilerParams(dimension_semantics=("parallel", "parallel"),
                                             vmem_limit_bytes=VMEM_LIMIT_BYTES),
        name="mix_out_proj",
    )(x, gate, rw, ml, hy, w_out.astype(jnp.bfloat16))


def _mix_and_retrieve(li, x, c, ctx, c_ctx, mod_w, mod_b, mix_norm, w_in, w_out, rw_conv, rw_decay_up, rw_decay0, rw_a_up, rw_a0, rw_gate_up, rw_k_k, rw_k_a, rw_r_k, rw_gn_g, rw_gn_b, mla_q_norm, mla_w_uq, mla_kv_norm, mla_w_ukv, mla_q_gain, mla_k_gain, hy_conv, hy_w1, hy_b1, hy_freq1, hy_w2, hy_b2, hy_freq2, hy_w3, hy_b3, hy_bias, ffn_norm, peer_wq, peer_keys, peer_u, peer_v):
    B, L, D = x.shape
    need_ctx = li < DEPTH - 1
    mod_l = (jax.nn.silu(c) @ mod_w[li] + mod_b[li])[:, None, :]
    mod_c = (jax.nn.silu(c_ctx) @ mod_w[li] + mod_b[li])[None, None, :]
    shm_l, scm_l, gm_l, shf_l, scf_l, gf_l = jnp.split(mod_l, N_MOD, axis=-1)
    shm_c, scm_c, gm_c, shf_c, scf_c, gf_c = jnp.split(mod_c, N_MOD, axis=-1)

    widths = (RW_PROJ, MLA_PROJ, HY_PROJ)
    prw_l, pml_l, phy_l = norm_mod_proj(x, mix_norm[li], scm_l, shm_l, w_in[li], widths, 512)
    prw_c, pml_c, phy_c = norm_mod_proj(ctx, mix_norm[li], jnp.broadcast_to(scm_c, (B, 1, D)),
                                        jnp.broadcast_to(shm_c, (B, 1, D)), w_in[li], widths, 256)
    rw_l, rw_c = rwkv7_mixer(prw_l, prw_c, rw_conv[li], rw_decay_up[li], rw_decay0[li],
                             rw_a_up[li], rw_a0[li], rw_gate_up[li], rw_k_k[li], rw_k_a[li], rw_r_k[li],
                             rw_gn_g[li], rw_gn_b[li], need_ctx)
    ml_l, ml_c = mla_mixer(pml_l, pml_c, mla_q_norm[li], mla_w_uq[li],
                           mla_kv_norm[li], mla_w_ukv[li], mla_q_gain[li], mla_k_gain[li], need_ctx)
    hy_prm = (hy_conv[li], hy_w1[li], hy_b1[li], hy_freq1[li], hy_w2[li], hy_b2[li], hy_freq2[li],
              hy_w3[li], hy_b3[li], hy_bias[li])
    hy_l = hyena_mixer(phy_l, *hy_prm)
    x = mix_out_proj(x, gm_l, rw_l, ml_l, hy_l, w_out[li])
    h, e_idx, gate = peer_retrieve(x, ffn_norm[li], scf_l, shf_l, peer_wq[li], peer_keys[li])
    streams = [(x, gf_l, h.reshape(B * L, D), e_idx, gate)]
    if need_ctx:
        hy_c = hyena_mixer(phy_c, *hy_prm)
        ctx = mix_out_proj(ctx, jnp.broadcast_to(gm_c, (B, 1, D)), rw_c, ml_c, hy_c, w_out[li])
        h, e_idx, gate = peer_retrieve(ctx, ffn_norm[li], jnp.broadcast_to(scf_c, (B, 1, D)),
                                       jnp.broadcast_to(shf_c, (B, 1, D)), peer_wq[li], peer_keys[li])
        streams.append((ctx, gf_c, h.reshape(-1, D), e_idx, gate))
    return streams


BATCH_GROUP_ROWS = (2, 2, 2, 2)


def kernel(x, c, ctx, c_ctx, mod_w, mod_b, mix_norm, w_in, w_out, rw_conv, rw_decay_up, rw_decay0, rw_a_up, rw_a0, rw_gate_up, rw_k_k, rw_k_a, rw_r_k, rw_gn_g, rw_gn_b, mla_q_norm, mla_w_uq, mla_kv_norm, mla_w_ukv, mla_q_gain, mla_k_gain, hy_conv, hy_w1, hy_b1, hy_freq1, hy_w2, hy_b2, hy_freq2, hy_w3, hy_b3, hy_bias, ffn_norm, peer_wq, peer_keys, peer_u, peer_v):
    params = (mod_w, mod_b, mix_norm, w_in, w_out, rw_conv, rw_decay_up, rw_decay0, rw_a_up, rw_a0, rw_gate_up,
              rw_k_k, rw_k_a, rw_r_k, rw_gn_g, rw_gn_b, mla_q_norm, mla_w_uq, mla_kv_norm, mla_w_ukv, mla_q_gain,
              mla_k_gain, hy_conv, hy_w1, hy_b1, hy_freq1, hy_w2, hy_b2, hy_freq2, hy_w3, hy_b3, hy_bias,
              ffn_norm, peer_wq, peer_keys)
    peer_u = [pack_expert_table(peer_u[li]) for li in range(DEPTH)]
    peer_v = [pack_expert_table(peer_v[li]) for li in range(DEPTH)]
    params = params + (peer_u, peer_v)
    assert sum(BATCH_GROUP_ROWS) == x.shape[0]
    G = len(BATCH_GROUP_ROWS)
    lo = [sum(BATCH_GROUP_ROWS[:g]) for g in range(G + 1)]
    L, D = x.shape[1:]
    xs = [x[lo[g]:lo[g + 1]] for g in range(G)]
    cs = [c[lo[g]:lo[g + 1]] for g in range(G)]
    ctxs = [ctx[lo[g]:lo[g + 1]] for g in range(G)]
    stages = [(li, g) for li in range(DEPTH) for g in range(G)]
    prev = None
    token = None

    def advance(prev, token, li, streams):
        pstreams, pdots = ([], []) if prev is None else (prev[2], prev[3])
        if pdots:
            token, pdots = lax.optimization_barrier((token, pdots))
        dots = []
        for s in range(max(len(pstreams), len(streams))):
            jobs = []
            if s < len(pstreams):
                w = peer_act(pdots[s], pstreams[s][4])
                token = w if s == 0 else token
                jobs.append(("wsum", peer_v[prev[0]], pstreams[s][3], w))
            if s < len(streams):
                jobs.append(("dot", peer_u[li], streams[s][3], streams[s][2]))
            outs = _sc_peer_jobs(jobs)
            if s < len(pstreams):
                res = pstreams[s][0] + pstreams[s][1] * outs[0].reshape(pstreams[s][0].shape)
                (xs if s == 0 else ctxs)[prev[1]] = res
            if s < len(streams):
                dots.append(outs[-1])
        return token, dots

    for li, g in stages:
        ins = (xs[g], ctxs[g])
        if token is not None:
            token, ins = lax.optimization_barrier((token, ins))
        streams = _mix_and_retrieve(li, ins[0], cs[g], ins[1], c_ctx, *params)
        token, dots = advance(prev, streams[0][4], li, streams)
        prev = (li, g, streams, dots)
    advance(prev, token, None, [])
    return jnp.concatenate(xs, axis=0)
```

```python
import functools
import math

import jax
import jax.numpy as jnp
import numpy as np
from jax import lax
from jax.experimental import pallas as pl
from jax.experimental.pallas import tpu as pltpu
from jax.experimental.pallas import tpu_sc as plsc

D_MODEL = 1024
DEPTH = 2
GRID_W = 64
N_MOD = 6
NORM_EPS = 1e-6
SHORT_CONV = 3

RW_HEADS = 6
RW_HEAD_DIM = 64
RW_WIDTH = RW_HEADS * RW_HEAD_DIM
RW_DECAY_RANK = 64
RW_A_RANK = 64
RW_GATE_RANK = 128
RW_DECAY_SCALE = 0.6065306597
RW_GN_EPS = 64e-5
L2_EPS = 1e-12

MLA_HEADS = 6
MLA_Q_RANK = 256
MLA_KV_RANK = 128
MLA_NOPE_DIM = 64
MLA_ROPE_DIM = 32
MLA_V_DIM = 64
MLA_QK_DIM = MLA_NOPE_DIM + MLA_ROPE_DIM
MLA_WIDTH = MLA_HEADS * MLA_V_DIM
AXIS_ROPE_DIM = MLA_ROPE_DIM // 2
ROPE_THETA = 10000.0

HY_WIDTH = 256
HY_ORDER = 2
HY_POS_BANDS = 16
HY_SHORT_DECAY_PCT = 0.3
HY_LONG_DECAY_PCT = 1.5
HY_DECAY_TARGET = 1e-2

PEER_HEADS = 8
PEER_N_KEYS = 128
PEER_TOPK = 16
PEER_QUERY_DIM = 256
PEER_HALF = PEER_QUERY_DIM // 2

RW_PROJ = 3 * RW_WIDTH + RW_DECAY_RANK + RW_A_RANK + RW_GATE_RANK
MLA_PROJ = MLA_Q_RANK + MLA_KV_RANK + MLA_ROPE_DIM
HY_PROJ = (HY_ORDER + 1) * HY_WIDTH
MIX_WIDTH = RW_WIDTH + MLA_WIDTH + HY_WIDTH

VMEM_LIMIT_BYTES = 48 * 1024 * 1024


def _norm_mod_proj_kernel(x_ref, gain_ref, scale_ref, shift_ref, *refs):
    w_refs, o_refs = refs[:len(refs) // 2], refs[len(refs) // 2:]
    x = x_ref[0]
    y = x * lax.rsqrt(jnp.mean(x * x, axis=-1, keepdims=True) + NORM_EPS)
    y = y * gain_ref[...]
    y = (y * (1.0 + scale_ref[0]) + shift_ref[0]).astype(jnp.bfloat16)
    for w_ref, o_ref in zip(w_refs, o_refs):
        o_ref[0] = jnp.dot(y, w_ref[...], preferred_element_type=jnp.float32)


def norm_mod_proj(x, gain, scale, shift, w, widths, block_rows):
    B, L, D = x.shape
    assert sum(widths) == w.shape[1]
    offs = [sum(widths[:i]) for i in range(len(widths))]
    ws = [w[:, o:o + n].astype(jnp.bfloat16) for o, n in zip(offs, widths)]
    return pl.pallas_call(
        _norm_mod_proj_kernel,
        grid=(B, L // block_rows),
        in_specs=[
            pl.BlockSpec((1, block_rows, D), lambda b, i: (b, i, 0)),
            pl.BlockSpec((1, D), lambda b, i: (0, 0)),
            pl.BlockSpec((1, 1, D), lambda b, i: (b, 0, 0)),
            pl.BlockSpec((1, 1, D), lambda b, i: (b, 0, 0)),
        ] + [pl.BlockSpec((D, n), lambda b, i: (0, 0)) for n in widths],
        out_specs=[pl.BlockSpec((1, block_rows, n), lambda b, i: (b, i, 0)) for n in widths],
        out_shape=[jax.ShapeDtypeStruct((B, L, n), jnp.float32) for n in widths],
        compiler_params=pltpu.CompilerParams(
            dimension_semantics=("parallel", "parallel"), vmem_limit_bytes=VMEM_LIMIT_BYTES),
        name="norm_mod_proj",
    )(x, gain.reshape(1, D), scale, shift, *ws)


CONV_ROWS = 512
SUBLANES = 8


def _short_conv_kernel(x_ref, prev_ref, next_ref, w_ref, o_ref):
    i = pl.program_id(1)
    x = x_ref[0]
    T = x.shape[0]
    row = lax.broadcasted_iota(jnp.int32, x.shape, 0)
    before = jnp.where(i == 0, 0.0, prev_ref[0, SUBLANES - 1:SUBLANES, :])
    after = jnp.where(i == pl.num_programs(1) - 1, 0.0, next_ref[0, 0:1, :])
    up = jnp.where(row == 0, before, pltpu.roll(x, 1, 0))
    down = jnp.where(row == T - 1, after, pltpu.roll(x, T - 1, 0))
    o_ref[0] = up * w_ref[0:1, :] + x * w_ref[1:2, :] + down * w_ref[2:3, :]


def short_conv(x, w):
    B, L, C = x.shape
    T = min(CONV_ROWS, L)
    per = T // SUBLANES
    last = L // SUBLANES - 1
    return pl.pallas_call(
        _short_conv_kernel,
        grid=(B, L // T),
        in_specs=[pl.BlockSpec((1, T, C), lambda b, i: (b, i, 0)),
                  pl.BlockSpec((1, SUBLANES, C), lambda b, i: (b, jnp.maximum(i * per - 1, 0), 0)),
                  pl.BlockSpec((1, SUBLANES, C), lambda b, i: (b, jnp.minimum((i + 1) * per, last), 0)),
                  pl.BlockSpec((SHORT_CONV, C), lambda b, i: (0, 0))],
        out_specs=pl.BlockSpec((1, T, C), lambda b, i: (b, i, 0)),
        out_shape=jax.ShapeDtypeStruct((B, L, C), jnp.float32),
        compiler_params=pltpu.CompilerParams(dimension_semantics=("parallel", "parallel"),
                                             vmem_limit_bytes=VMEM_LIMIT_BYTES),
        name="short_conv",
    )(x, x, x, w)


RW_CHUNK = 64


def _rwkv_chunk_kernel(r_ref, kk_ref, v_ref, lw_ref, akk_ref, kr_ref, y_ref, h_ref):
    d = pl.program_id(0)
    n = pl.program_id(2)

    @pl.when(n == 0)
    def _():
        h_ref[...] = jnp.zeros_like(h_ref)

    C = RW_CHUNK
    row = lax.broadcasted_iota(jnp.int32, (C, C), 0)
    col = lax.broadcasted_iota(jnp.int32, (C, C), 1)
    lag = (row - col) * (1 - 2 * d)
    before = lag > 0
    upto = lag >= 0
    tri = upto.astype(jnp.float32)
    eye = (row == col).astype(jnp.float32)
    bf = jnp.bfloat16
    f32 = jnp.float32

    def mm(a, b):
        return jnp.dot(a.astype(bf), b.astype(bf), preferred_element_type=f32)

    def mm_nt(a, b):
        return lax.dot_general(a.astype(bf), b.astype(bf), (((1,), (1,)), ((), ())), preferred_element_type=f32)

    def mm_tn(a, b):
        return lax.dot_general(a.astype(bf), b.astype(bf), (((0,), (0,)), ((), ())), preferred_element_type=f32)

    hs = range(RW_HEADS)
    HD = RW_HEAD_DIM
    heads = lambda t: [t[:, h * HD:(h + 1) * HD] for h in hs]
    r = heads(r_ref[0])
    kk = heads(kk_ref[0])
    v = heads(v_ref[0])
    lw = heads(lw_ref[0, 0])
    akk = heads(akk_ref[0, 0])
    kr = heads(kr_ref[0, 0])
    G = [jnp.dot(tri, lw[h], preferred_element_type=f32, precision=lax.Precision.HIGHEST) for h in hs]
    gtot = [jnp.sum(lw[h], axis=0, keepdims=True) for h in hs]
    Einv = [jnp.exp(-G[h]) for h in hs]
    At = [-kk[h] * jnp.exp(G[h] - lw[h]) for h in hs]
    Rt = [r[h] * jnp.exp(G[h]) for h in hs]
    Bt = [akk[h] * Einv[h] for h in hs]
    Kt = [kr[h] * Einv[h] for h in hs]
    X = [mm_nt(jnp.concatenate([At[h], Rt[h]], axis=0), jnp.concatenate([Bt[h], Kt[h]], axis=0)) for h in hs]
    M_ab = [jnp.where(before, X[h][:C, :C], 0.0) for h in hs]
    M_ak = [jnp.where(before, X[h][:C, C:], 0.0) for h in hs]
    A_rb = [jnp.where(upto, X[h][C:, :C], 0.0) for h in hs]
    A_rk = [jnp.where(upto, X[h][C:, C:], 0.0) for h in hs]
    MV = [mm(M_ak[h], v[h]) for h in hs]
    Mp = M_ab
    T = [eye + Mp[h] for h in hs]
    for _ in range(5):
        Mp = [jnp.dot(Mp[h], Mp[h], preferred_element_type=f32) for h in hs]
        T = [T[h] + jnp.dot(T[h], Mp[h], preferred_element_type=f32) for h in hs]
    WU = [jnp.dot(T[h], jnp.concatenate([At[h], MV[h]], axis=1), preferred_element_type=f32) for h in hs]
    H0 = [h_ref[h] for h in hs]
    Ehat = [jnp.exp(gtot[h] - G[h]) for h in hs]
    Om = [Rt[h] + mm(A_rb[h], WU[h][:, :HD]) for h in hs]
    Y0 = [mm(A_rb[h], WU[h][:, HD:]) + mm(A_rk[h], v[h]) for h in hs]
    BW = [mm_tn(akk[h] * Ehat[h], WU[h]) for h in hs]
    KV = [mm_tn(kr[h] * Ehat[h], v[h]) for h in hs]
    y_ref[0, 0] = jnp.concatenate([jnp.dot(Om[h], H0[h], preferred_element_type=f32) + Y0[h] for h in hs], axis=1)
    for h in hs:
        P = eye * jnp.exp(gtot[h]) + BW[h][:, :HD]
        h_ref[h] = jnp.dot(P, H0[h], preferred_element_type=f32) + BW[h][:, HD:] + KV[h]


def rwkv_chunked(r, kk, v, lw, akk, kr, n_ctx):
    B, T, W = r.shape
    H = W // RW_HEAD_DIM
    nc = n_ctx // RW_CHUNK
    nt = T // RW_CHUNK

    def chunk_of(d, n):
        bwd = jnp.where(n < nc, nc - 1 - n, nt - 1 - (n - nc))
        return jnp.where(d == 0, n, bwd)

    spec1 = pl.BlockSpec((1, RW_CHUNK, W), lambda d, b, n: (b, chunk_of(d, n), 0))
    spec2 = pl.BlockSpec((1, 1, RW_CHUNK, W), lambda d, b, n: (d, b, chunk_of(d, n), 0))
    return pl.pallas_call(
        _rwkv_chunk_kernel,
        grid=(2, B, nt),
        in_specs=[spec1, spec1, spec1, spec2, spec2, spec2],
        out_specs=spec2,
        out_shape=jax.ShapeDtypeStruct((2, B, T, W), jnp.float32),
        scratch_shapes=[pltpu.VMEM((H, RW_HEAD_DIM, RW_HEAD_DIM), jnp.float32)],
        compiler_params=pltpu.CompilerParams(dimension_semantics=("parallel", "parallel", "arbitrary")),
        name="rwkv_chunked",
    )(r, kk, v, lw, akk, kr)


LANE = 128
RW_PREP_ROWS = 256
MLA_PAD_WIDTH = MLA_HEADS * LANE
MLA_PREP_ROWS = 256
ATTN_Q_ROWS = 512


def _split_dot(x, m):
    hi = x.astype(jnp.bfloat16)
    lo = (x - hi.astype(jnp.float32)).astype(jnp.bfloat16)
    return (jnp.dot(hi, m, preferred_element_type=jnp.float32) + jnp.dot(lo, m, preferred_element_type=jnp.float32))


def _rwkv_prep_kernel(z_ref, wda_ref, d0_ref, a0_ref, gup_ref, kk_ref_w, ka_ref, rk_ref, hsum_ref,
                      r_ref, kk_ref, v_ref, lw_ref, akk_ref, kr_ref, g_ref, bonus_ref):
    W = RW_WIDTH
    bf = jnp.bfloat16
    z = z_ref[0]
    r, k, v = z[:, :W], z[:, W:2 * W], z[:, 2 * W:3 * W]
    da = z[:, 3 * W:3 * W + LANE]
    lane = lax.broadcasted_iota(jnp.int32, da.shape, 1)
    da = jnp.where(lane < RW_DECAY_RANK, jnp.tanh(da), da)
    up = jnp.dot(da.astype(bf), wda_ref[...], preferred_element_type=jnp.float32)
    g_lo = z[:, 3 * W + LANE:]
    g_ref[0] = jnp.dot(jax.nn.sigmoid(g_lo).astype(bf), gup_ref[...], preferred_element_type=jnp.float32)
    hsum = hsum_ref[...]
    kk = k * kk_ref_w[...]
    kk = kk * lax.rsqrt(_split_dot(kk * kk, hsum) + L2_EPS)
    r_ref[0] = r
    v_ref[0] = v
    kk_ref[0] = kk
    bonus_ref[0] = _split_dot(r * k * rk_ref[...], hsum) * v
    for d in range(2):
        lw_ref[d, 0] = -RW_DECAY_SCALE * jax.nn.sigmoid(d0_ref[d:d + 1, :] + up[:, d * W:(d + 1) * W])
        a = jax.nn.sigmoid(a0_ref[d:d + 1, :] + up[:, (2 + d) * W:(3 + d) * W])
        akk_ref[d, 0] = kk * a
        kr_ref[d, 0] = k * (1.0 + (a - 1.0) * ka_ref[...])


def rwkv_prep(z, decay_up, decay0, a_up, a0, gate_up, k_k, k_a, r_k):
    B, L, _ = z.shape
    W = RW_WIDTH
    T = min(RW_PREP_ROWS, L)
    zero = jnp.zeros((RW_DECAY_RANK, 2 * W), jnp.float32)
    wda = jnp.concatenate([
        jnp.concatenate([decay_up[0], decay_up[1], zero], axis=1),
        jnp.concatenate([zero, a_up[0], a_up[1]], axis=1)], axis=0).astype(jnp.bfloat16)
    head = jnp.arange(W) // RW_HEAD_DIM
    hsum = (head[:, None] == head[None, :]).astype(jnp.bfloat16)
    row = lambda a: a.reshape(1, W)
    const = lambda a: pl.BlockSpec(a.shape, lambda b, i: (0,) * a.ndim)
    tok = pl.BlockSpec((1, T, W), lambda b, i: (b, i, 0))
    tok2 = pl.BlockSpec((2, 1, T, W), lambda b, i: (0, b, i, 0))
    f1 = jax.ShapeDtypeStruct((B, L, W), jnp.float32)
    f2 = jax.ShapeDtypeStruct((2, B, L, W), jnp.float32)
    args = (z, wda, decay0, a0, gate_up.astype(jnp.bfloat16), row(k_k), row(k_a), row(r_k), hsum)
    return pl.pallas_call(
        _rwkv_prep_kernel,
        grid=(B, L // T),
        in_specs=[pl.BlockSpec((1, T, RW_PROJ), lambda b, i: (b, i, 0))] + [const(a) for a in args[1:]],
        out_specs=[tok, tok, tok, tok2, tok2, tok2, tok, tok],
        out_shape=[f1, f1, f1, f2, f2, f2, f1, f1],
        compiler_params=pltpu.CompilerParams(dimension_semantics=("parallel", "parallel"),
                                             vmem_limit_bytes=VMEM_LIMIT_BYTES),
        name="rwkv_prep",
    )(*args)


def _rwkv_readout_kernel(y_ref, g_ref, bonus_ref, gng_ref, gnb_ref, hsum_ref, o_ref):
    y = y_ref[0, 0] + y_ref[1, 0]
    hsum = hsum_ref[...]
    mu = _split_dot(y, hsum) * (1.0 / RW_HEAD_DIM)
    d = y - mu
    var = _split_dot(d * d, hsum) * (1.0 / RW_HEAD_DIM)
    yn = d * lax.rsqrt(var + RW_GN_EPS) * gng_ref[...] + gnb_ref[...]
    o_ref[0] = (yn + bonus_ref[0]) * g_ref[0]


def rwkv_readout(y, g, bonus, gn_g, gn_b, t0):
    B, L, W = g.shape
    T = min(RW_PREP_ROWS, L)
    off = t0 // T
    head = jnp.arange(W) // RW_HEAD_DIM
    hsum = (head[:, None] == head[None, :]).astype(jnp.bfloat16)
    tok = pl.BlockSpec((1, T, W), lambda b, i: (b, i, 0))
    const = lambda a: pl.BlockSpec(a.shape, lambda b, i: (0,) * a.ndim)
    gg, gb = gn_g.reshape(1, W), gn_b.reshape(1, W)
    return pl.pallas_call(
        _rwkv_readout_kernel,
        grid=(B, L // T),
        in_specs=[pl.BlockSpec((2, 1, T, W), lambda b, i: (0, b, i + off, 0)), tok, tok, const(gg), const(gb), const(hsum)],
        out_specs=tok,
        out_shape=jax.ShapeDtypeStruct((B, L, W), jnp.float32),
        compiler_params=pltpu.CompilerParams(dimension_semantics=("parallel", "parallel")),
        name="rwkv_readout",
    )(y, g, bonus, gg, gb, hsum)


def rwkv7_mixer(p_lat, p_ctx, conv_w, decay_up, decay0, a_up, a0, gate_up, k_k, k_a, r_k, gn_g, gn_b, need_ctx):
    prm = (decay_up, decay0, a_up, a0, gate_up, k_k, k_a, r_k)
    lat = rwkv_prep(short_conv(p_lat, conv_w), *prm)
    ctx = rwkv_prep(short_conv(p_ctx, conv_w), *prm)
    n_ctx = p_ctx.shape[1]
    seq = lambda i: jnp.concatenate([ctx[i], lat[i]], axis=-2)
    y = rwkv_chunked(seq(0), seq(1), seq(2), seq(3), seq(4), seq(5), n_ctx)
    out_l = rwkv_readout(y, lat[6], lat[7], gn_g, gn_b, n_ctx)
    out_c = rwkv_readout(y, ctx[6], ctx[7], gn_g, gn_b, 0) if need_ctx else None
    return out_l, out_c


def _rope_tables(L, use_rope):
    lane = np.arange(LANE)
    in_rope = (lane >= MLA_NOPE_DIM) & (lane < MLA_QK_DIM)
    j = lane - MLA_NOPE_DIM
    axis = j // AXIS_ROPE_DIM
    half = AXIS_ROPE_DIM // 2
    f = j % half
    first = (j % AXIS_ROPE_DIM) < half
    inv = ROPE_THETA ** (-jnp.arange(0, AXIS_ROPE_DIM, 2, dtype=jnp.float32) / AXIS_ROPE_DIM)
    t = jnp.arange(L)
    pos = jnp.stack([t // GRID_W, t % GRID_W], axis=-1).astype(jnp.float32)
    ang = pos[:, np.clip(axis, 0, 1)] * inv[np.clip(f, 0, half - 1)][None, :]
    rope_on = jnp.asarray(in_rope)[None, :] & use_rope
    cos = jnp.where(rope_on, jnp.cos(ang), 1.0)
    sin = jnp.where(rope_on, jnp.sin(ang) * jnp.where(jnp.asarray(first), -1.0, 1.0)[None, :], 0.0)
    return jnp.tile(cos, (1, MLA_HEADS)), jnp.tile(sin, (1, MLA_HEADS))


def _mla_prep_kernel(p_ref, qn_ref, wq_ref, kvn_ref, wk_ref, wv_ref, place_ref, qg_ref, kg_ref, hsum_ref, cos_ref, sin_ref,
                     q_ref, k_ref, v_ref):
    bf = jnp.bfloat16
    p = p_ref[0]
    c_q = p[:, :MLA_Q_RANK]
    c_kv = p[:, MLA_Q_RANK:MLA_Q_RANK + MLA_KV_RANK]
    tail = p[:, MLA_Q_RANK + MLA_KV_RANK:]
    cqn = c_q * lax.rsqrt(jnp.mean(c_q * c_q, axis=-1, keepdims=True) + NORM_EPS) * qn_ref[...]
    ckn = c_kv * lax.rsqrt(jnp.mean(c_kv * c_kv, axis=-1, keepdims=True) + NORM_EPS) * kvn_ref[...]
    q = jnp.dot(cqn.astype(bf), wq_ref[...], preferred_element_type=jnp.float32)
    k = jnp.dot(ckn.astype(bf), wk_ref[...], preferred_element_type=jnp.float32) + _split_dot(tail, place_ref[...])
    v_ref[0] = jnp.dot(ckn.astype(bf), wv_ref[...], preferred_element_type=jnp.float32).astype(bf)
    hsum = hsum_ref[...]
    cos, sin = cos_ref[...], sin_ref[...]
    lane = lax.broadcasted_iota(jnp.int32, q.shape, 1)
    first = ((lane - MLA_NOPE_DIM) % AXIS_ROPE_DIM) < (AXIS_ROPE_DIM // 2)
    half = AXIS_ROPE_DIM // 2

    def finish(x, gain):
        x = x * lax.rsqrt(_split_dot(x * x, hsum) * (1.0 / MLA_QK_DIM) + NORM_EPS) * gain
        partner = jnp.where(first, pltpu.roll(x, MLA_PAD_WIDTH - half, 1), pltpu.roll(x, half, 1))
        return x * cos + partner * sin

    q_ref[0] = (finish(q, qg_ref[...]) * (MLA_QK_DIM ** -0.5)).astype(bf)
    k_ref[0] = finish(k, kg_ref[...]).astype(bf)


def mla_prep(p, use_rope, q_norm, w_uq, kv_norm, w_ukv, q_gain, k_gain):
    B, L, _ = p.shape
    T = min(MLA_PREP_ROWS, L)
    H = MLA_HEADS
    pad_cols = lambda w, d: jnp.pad(w.reshape(w.shape[0], H, d), ((0, 0), (0, 0), (0, LANE - d))).reshape(w.shape[0], H * LANE)
    wq = pad_cols(w_uq, MLA_QK_DIM).astype(jnp.bfloat16)
    ukv = w_ukv.reshape(MLA_KV_RANK, H, MLA_NOPE_DIM + MLA_V_DIM)
    wk = pad_cols(ukv[:, :, :MLA_NOPE_DIM].reshape(MLA_KV_RANK, H * MLA_NOPE_DIM), MLA_NOPE_DIM).astype(jnp.bfloat16)
    wv = ukv[:, :, MLA_NOPE_DIM:].reshape(MLA_KV_RANK, H * MLA_V_DIM).astype(jnp.bfloat16)
    lane = np.arange(H * LANE)
    place = jnp.asarray(((lane[None, :] % LANE) - MLA_NOPE_DIM == np.arange(MLA_ROPE_DIM)[:, None]), jnp.bfloat16)
    hsum = jnp.asarray((lane[:, None] // LANE) == (lane[None, :] // LANE), jnp.bfloat16)
    pad_gain = lambda g: jnp.tile(jnp.pad(g, (0, LANE - MLA_QK_DIM)), H).reshape(1, H * LANE)
    cos, sin = _rope_tables(L, use_rope)
    const = lambda a: pl.BlockSpec(a.shape, lambda b, i: (0,) * a.ndim)
    args = (p, q_norm.reshape(1, -1), wq, kv_norm.reshape(1, -1), wk, wv, place, pad_gain(q_gain), pad_gain(k_gain), hsum)
    pos = pl.BlockSpec((T, H * LANE), lambda b, i: (i, 0))
    return pl.pallas_call(
        _mla_prep_kernel,
        grid=(B, L // T),
        in_specs=[pl.BlockSpec((1, T, MLA_PROJ), lambda b, i: (b, i, 0))] + [const(a) for a in args[1:]] + [pos, pos],
        out_specs=[pl.BlockSpec((1, T, H * LANE), lambda b, i: (b, i, 0)), pl.BlockSpec((1, T, H * LANE), lambda b, i: (b, i, 0)),
                   pl.BlockSpec((1, T, MLA_WIDTH), lambda b, i: (b, i, 0))],
        out_shape=[jax.ShapeDtypeStruct((B, L, H * LANE), jnp.bfloat16), jax.ShapeDtypeStruct((B, L, H * LANE), jnp.bfloat16),
                   jax.ShapeDtypeStruct((B, L, MLA_WIDTH), jnp.bfloat16)],
        compiler_params=pltpu.CompilerParams(dimension_semantics=("parallel", "parallel"),
                                             vmem_limit_bytes=VMEM_LIMIT_BYTES),
        name="mla_prep",
    )(*args, cos, sin)


def _attn_kernel(q_ref, k_ref, v_ref, o_ref):
    lane = lax.broadcasted_iota(jnp.int32, (q_ref.shape[1], LANE), 1)
    for pair in range(MLA_HEADS // 2):
        v_pair = v_ref[0, :, pair * LANE:(pair + 1) * LANE]
        outs = []
        for h in (2 * pair, 2 * pair + 1):
            q = q_ref[0, :, h * LANE:(h + 1) * LANE]
            k = k_ref[0, :, h * LANE:(h + 1) * LANE]
            s = lax.dot_general(q, k, (((1,), (1,)), ((), ())), preferred_element_type=jnp.float32)
            e = jnp.exp(s - jnp.max(s, axis=-1, keepdims=True))
            o = jnp.dot(e.astype(jnp.bfloat16), v_pair, preferred_element_type=jnp.float32)
            outs.append(o / jnp.sum(e, axis=-1, keepdims=True))
        o_ref[0, :, pair * LANE:(pair + 1) * LANE] = jnp.where(lane < MLA_V_DIM, outs[0], outs[1])


def attention(q, k, v):
    B, Lq, P = q.shape
    Lk = k.shape[1]
    tq = min(ATTN_Q_ROWS, Lq)
    return pl.pallas_call(
        _attn_kernel,
        grid=(B, Lq // tq),
        in_specs=[pl.BlockSpec((1, tq, P), lambda b, i: (b, i, 0)),
                  pl.BlockSpec((1, Lk, P), lambda b, i: (b, 0, 0)),
                  pl.BlockSpec((1, Lk, MLA_WIDTH), lambda b, i: (b, 0, 0))],
        out_specs=pl.BlockSpec((1, tq, MLA_WIDTH), lambda b, i: (b, i, 0)),
        out_shape=jax.ShapeDtypeStruct((B, Lq, MLA_WIDTH), jnp.float32),
        compiler_params=pltpu.CompilerParams(dimension_semantics=("parallel", "parallel"),
                                             vmem_limit_bytes=VMEM_LIMIT_BYTES),
        name="mla_attention",
    )(q, k, v)


def mla_mixer(p_lat, p_ctx, q_norm, w_uq, kv_norm, w_ukv, q_gain, k_gain, need_ctx):
    prm = (q_norm, w_uq, kv_norm, w_ukv, q_gain, k_gain)
    q_l, k_l, v_l = mla_prep(p_lat, True, *prm)
    q_c, k_c, v_c = mla_prep(p_ctx, False, *prm)
    y_l = attention(q_l, jnp.concatenate([k_l, k_c], axis=1), jnp.concatenate([v_l, v_c], axis=1))
    y_c = attention(q_c, k_c, v_c) if need_ctx else None
    return y_l, y_c


HY_FILTER_ROWS = 256


def _hyena_filter_kernel(z_ref, tn_ref, w1_ref, b1_ref, f1_ref, w2_ref, b2_ref, f2_ref, w3_ref, b3_ref, rates_ref, o_ref):
    hp = lax.Precision.HIGHEST
    h = jnp.sin(f1_ref[...] * (jnp.dot(z_ref[...], w1_ref[...], precision=hp, preferred_element_type=jnp.float32) + b1_ref[...]))
    h = jnp.sin(f2_ref[...] * (jnp.dot(h, w2_ref[...], precision=hp, preferred_element_type=jnp.float32) + b2_ref[...]))
    h = jnp.dot(h, w3_ref[...], precision=hp, preferred_element_type=jnp.float32) + b3_ref[...]
    o_ref[...] = h * jnp.exp(-tn_ref[...] * rates_ref[...])


def hyena_filters(L, w1, b1, freq1, w2, b2, freq2, w3, b3):
    tn = jnp.arange(L, dtype=jnp.float32) / L
    bands = jnp.arange(1, HY_POS_BANDS + 1, dtype=jnp.float32)
    ang = 2.0 * math.pi * tn[:, None] * bands[None, :]
    z = jnp.concatenate([tn[:, None], jnp.cos(ang), jnp.sin(ang)], axis=-1)
    rates = jnp.abs(jnp.linspace(math.log(HY_DECAY_TARGET) / HY_LONG_DECAY_PCT,
                                 math.log(HY_DECAY_TARGET) / HY_SHORT_DECAY_PCT, HY_WIDTH))
    pad = -z.shape[1] % 8
    T = min(HY_FILTER_ROWS, L)
    n_out = w3.shape[1]
    row = lambda a: a.reshape(1, -1)
    const = lambda a: pl.BlockSpec(a.shape, lambda i: (0,) * a.ndim)
    args = (jnp.pad(w1, ((0, pad), (0, 0))), row(b1), row(freq1), w2, row(b2), row(freq2), w3, row(b3),
            row(jnp.tile(rates, n_out // HY_WIDTH)))
    h = pl.pallas_call(
        _hyena_filter_kernel,
        grid=(L // T,),
        in_specs=[pl.BlockSpec((T, z.shape[1] + pad), lambda i: (i, 0)), pl.BlockSpec((T, 1), lambda i: (i, 0))]
                 + [const(a) for a in args],
        out_specs=pl.BlockSpec((T, n_out), lambda i: (i, 0)),
        out_shape=jax.ShapeDtypeStruct((L, n_out), jnp.float32),
        compiler_params=pltpu.CompilerParams(dimension_semantics=("parallel",)),
        name="hyena_filter_mlp",
    )(jnp.pad(z, ((0, 0), (0, pad))), tn[:, None], *args)
    h = h.reshape(L, HY_ORDER, 2, HY_WIDTH)
    zero = jnp.zeros((1, HY_ORDER, HY_WIDTH), h.dtype)
    h_full = jnp.concatenate([h[:, :, 0], zero, h[:0:-1, :, 1]], axis=0)
    return h_full * lax.rsqrt(jnp.sum(jnp.square(h_full), axis=0, keepdims=True))


def fft_long_conv(u, h_full, bias):
    L = u.shape[1]
    uf = jnp.fft.rfft(u, n=2 * L, axis=1)
    hf = jnp.fft.rfft(h_full, n=2 * L, axis=0)
    y = jnp.fft.irfft(uf * hf[None], n=2 * L, axis=1)[:, :L]
    return y + u * bias


FFT_N1 = 64
FFT_N2 = 128
FFT_N = FFT_N1 * FFT_N2
HY_SEQS = 32


def _dft_tables(seqs):
    n1 = np.arange(FFT_N1)
    n2 = np.arange(FFT_N2)
    f64 = np.exp(-2j * np.pi * np.outer(n1, n1) / FFT_N1)
    f128 = np.exp(-2j * np.pi * np.outer(n2, n2) / FFT_N2)
    tw = np.exp(-2j * np.pi * np.outer(n1, n2) / FFT_N)
    half = FFT_N1 // 2
    fh = f64[:, :half]
    m1 = np.block([[fh.real, -fh.imag], [fh.imag, fh.real]])
    m1f = np.concatenate([f64.real, f64.imag], axis=0)
    m2 = np.block([[f128.real, f128.imag], [-f128.imag, f128.real]])
    m3 = np.block([[f128.real, -f128.imag], [f128.imag, f128.real]]) / FFT_N
    c = np.conj(f64)[:half, :]
    m4 = np.block([[c.real, -c.imag], [c.imag, c.real]])
    bf = lambda a: jnp.asarray(a, jnp.float32).astype(jnp.bfloat16)
    f32 = lambda a: jnp.asarray(a, jnp.float32)
    return dict(m1=bf(m1), m1f=bf(m1f), m2=bf(m2), m3=bf(m3), m4=bf(m4),
                twr_l=f32(np.tile(tw.real, (1, seqs))), twi_l=f32(np.tile(tw.imag, (1, seqs))),
                twr_s=f32(np.tile(tw.real, (seqs, 1))), twi_s=f32(np.tile(tw.imag, (seqs, 1))))


def _spectrum(cols, m1, twr_l, twi_l, m2, R):
    a = jnp.dot(m1, cols.astype(jnp.bfloat16), preferred_element_type=jnp.float32)
    ar, ai = a[:FFT_N1], a[FFT_N1:]
    pr = ar * twr_l - ai * twi_l
    pi = ar * twi_l + ai * twr_l
    lhs = jnp.concatenate(
        [jnp.concatenate([pr[:, r * FFT_N2:(r + 1) * FFT_N2], pi[:, r * FFT_N2:(r + 1) * FFT_N2]], axis=1)
         for r in range(R)], axis=0)
    return jnp.dot(lhs.astype(jnp.bfloat16), m2, preferred_element_type=jnp.float32)


def _filter_fft_kernel(h_ref, m1f_ref, twr_ref, twi_ref, m2_ref, o_ref):
    R = HY_SEQS
    cols = jnp.concatenate([h_ref[r] for r in range(R)], axis=1)
    x = _spectrum(cols, m1f_ref[...], twr_ref[...], twi_ref[...], m2_ref[...], R)
    o_ref[...] = x.reshape(R, FFT_N1, 2 * FFT_N2)


def _hyena_conv_kernel(y_ref, g_ref, hf_ref, bias_ref, m1_ref, twr_l_ref, twi_l_ref, m2_ref, m3_ref,
                       twr_s_ref, twi_s_ref, m4_ref, o_ref):
    R = HY_SEQS
    half = FFT_N1 // 2
    y = [y_ref[0], y_ref[1]]
    for o in range(HY_ORDER):
        top = jnp.concatenate([y[0][r] for r in range(R)], axis=1)
        bot = jnp.concatenate([y[1][r] for r in range(R)], axis=1)
        x = _spectrum(jnp.concatenate([top, bot], axis=0), m1_ref[...], twr_l_ref[...], twi_l_ref[...], m2_ref[...], R)
        hf = hf_ref[o].reshape(R * FFT_N1, 2 * FFT_N2)
        xr, xi = x[:, :FFT_N2], x[:, FFT_N2:]
        hr, hi = hf[:, :FFT_N2], hf[:, FFT_N2:]
        yc = jnp.concatenate([xr * hr - xi * hi, xr * hi + xi * hr], axis=1)
        b = jnp.dot(yc.astype(jnp.bfloat16), m3_ref[...], preferred_element_type=jnp.float32)
        br, bi = b[:, :FFT_N2], b[:, FFT_N2:]
        qr = br * twr_s_ref[...] + bi * twi_s_ref[...]
        qi = bi * twr_s_ref[...] - br * twi_s_ref[...]
        bc = jnp.concatenate(
            [jnp.concatenate([qr[r * FFT_N1:(r + 1) * FFT_N1], qi[r * FFT_N1:(r + 1) * FFT_N1]], axis=0)
             for r in range(R)], axis=1)
        yo = jnp.dot(m4_ref[...], bc.astype(jnp.bfloat16), preferred_element_type=jnp.float32)
        for p in range(2):
            conv = jnp.stack([yo[p * half:(p + 1) * half, r * FFT_N2:(r + 1) * FFT_N2] for r in range(R)], axis=0)
            y[p] = g_ref[o, p] * (conv + y[p] * bias_ref[o])
    o_ref[0] = y[0]
    o_ref[1] = y[1]


def hyena_long_conv(y_t, g_t, h_t, bias):
    B, C, L = y_t.shape
    assert 2 * L == FFT_N and B % 2 == 0 and C % HY_SEQS == 0
    R = HY_SEQS
    half = FFT_N1 // 2
    tb = _dft_tables(R)
    const = lambda a: pl.BlockSpec(a.shape, lambda *_: (0,) * a.ndim)
    hf = pl.pallas_call(
        _filter_fft_kernel,
        grid=(HY_ORDER * C // R,),
        in_specs=[pl.BlockSpec((R, FFT_N1, FFT_N2), lambda i: (i, 0, 0)),
                  const(tb['m1f']), const(tb['twr_l']), const(tb['twi_l']), const(tb['m2'])],
        out_specs=pl.BlockSpec((R, FFT_N1, 2 * FFT_N2), lambda i: (i, 0, 0)),
        out_shape=jax.ShapeDtypeStruct((HY_ORDER * C, FFT_N1, 2 * FFT_N2), jnp.float32),
        compiler_params=pltpu.CompilerParams(dimension_semantics=("parallel",), vmem_limit_bytes=VMEM_LIMIT_BYTES),
        name="hyena_filter_fft",
    )(h_t.reshape(HY_ORDER * C, FFT_N1, FFT_N2), tb['m1f'], tb['twr_l'], tb['twi_l'], tb['m2'])
    hf = hf.reshape(HY_ORDER, C, FFT_N1, 2 * FFT_N2)
    out = pl.pallas_call(
        _hyena_conv_kernel,
        grid=(B // 2, C // R),
        in_specs=[pl.BlockSpec((2, R, half, FFT_N2), lambda b, c: (b, c, 0, 0)),
                  pl.BlockSpec((HY_ORDER, 2, R, half, FFT_N2), lambda b, c: (0, b, c, 0, 0)),
                  pl.BlockSpec((HY_ORDER, R, FFT_N1, 2 * FFT_N2), lambda b, c: (0, c, 0, 0)),
                  pl.BlockSpec((HY_ORDER, R, 1, 1), lambda b, c: (0, c, 0, 0)),
                  const(tb['m1']), const(tb['twr_l']), const(tb['twi_l']), const(tb['m2']), const(tb['m3']),
                  const(tb['twr_s']), const(tb['twi_s']), const(tb['m4'])],
        out_specs=pl.BlockSpec((2, R, half, FFT_N2), lambda b, c: (b, c, 0, 0)),
        out_shape=jax.ShapeDtypeStruct((B, C, half, FFT_N2), jnp.float32),
        compiler_params=pltpu.CompilerParams(dimension_semantics=("parallel", "parallel"),
                                             vmem_limit_bytes=VMEM_LIMIT_BYTES),
        name="hyena_conv",
    )(y_t.reshape(B, C, half, FFT_N2), g_t.reshape(HY_ORDER, B, C, half, FFT_N2), hf,
      bias.reshape(HY_ORDER, C, 1, 1), tb['m1'], tb['twr_l'], tb['twi_l'], tb['m2'], tb['m3'],
      tb['twr_s'], tb['twi_s'], tb['m4'])
    return out.reshape(B, C, L)


def hyena_mixer(p, conv_w, w1, b1, freq1, w2, b2, freq2, w3, b3, bias):
    B, L = p.shape[:2]
    z = short_conv(p, conv_w)
    h_full = hyena_filters(L, w1, b1, freq1, w2, b2, freq2, w3, b3)
    if 2 * L == FFT_N:
        g_t = jnp.transpose(z[..., :HY_ORDER * HY_WIDTH].reshape(B, L, HY_ORDER, HY_WIDTH), (2, 0, 3, 1))
        y_t = jnp.swapaxes(z[..., HY_ORDER * HY_WIDTH:], 1, 2)
        y_t = hyena_long_conv(y_t, g_t, jnp.transpose(h_full, (1, 2, 0)), bias)
        return jnp.swapaxes(y_t, 1, 2)
    gates = (z[..., :HY_WIDTH], z[..., HY_WIDTH:2 * HY_WIDTH])
    y = z[..., 2 * HY_WIDTH:]
    for o in range(HY_ORDER):
        y = gates[o] * fft_long_conv(y, h_full[:, o], bias[o])
    return y


SC_CORES = 2
SC_SUBCORES = 16
SC_LANES = 16
SC_WORKERS = SC_CORES * SC_SUBCORES
PEER_SLOTS = PEER_HEADS * PEER_TOPK
PEER_GATHER_ROWS = 32
PEER_GATHERS = PEER_SLOTS // PEER_GATHER_ROWS
PEER_ACC_VREGS = 8
PEER_ROW_BUFFERS = 4
PEER_ROW_WORDS = D_MODEL // 2
HI_MASK = -65536


PACK_ROWS = 512


def _pack_table_kernel(t_ref, o_ref):
    bits = lambda x: lax.bitcast_convert_type(x.astype(jnp.bfloat16).astype(jnp.float32), jnp.int32)
    lo = lax.shift_right_logical(bits(t_ref[:, :PEER_ROW_WORDS]), 16)
    o_ref[...] = bits(t_ref[:, PEER_ROW_WORDS:]) | lo


def pack_expert_table(t):
    E = t.shape[0]
    return pl.pallas_call(
        _pack_table_kernel,
        grid=(E // PACK_ROWS,),
        in_specs=[pl.BlockSpec((PACK_ROWS, D_MODEL), lambda i: (i, 0))],
        out_specs=pl.BlockSpec((PACK_ROWS, PEER_ROW_WORDS), lambda i: (i, 0)),
        out_shape=jax.ShapeDtypeStruct((E, PEER_ROW_WORDS), jnp.int32),
        compiler_params=pltpu.CompilerParams(dimension_semantics=("parallel",)),
        name="pack_expert_table",
    )(t)


def _sc_peer_phase(phase, tpw):
    NBUF = PEER_ROW_BUFFERS
    AHEAD = NBUF - 1
    HW = PEER_ROW_WORDS

    def run(base, table_hbm, idx_hbm, aux_hbm, out_hbm, idx_v, aux_v, rows_v, out_v, sem_r, sem_i, sem_o):

        def gather(p, c, b):
            return pltpu.make_async_copy(table_hbm.at[idx_v.at[p, c]], rows_v.at[b], sem_r.at[b])

        def load_meta(t, p):
            return (pltpu.make_async_copy(idx_hbm.at[t], idx_v.at[p], sem_i.at[p]),
                    pltpu.make_async_copy(aux_hbm.at[t], aux_v.at[p], sem_i.at[p]))

        def store_out(t, p):
            return pltpu.make_async_copy(out_v.at[p], out_hbm.at[t], sem_o.at[p])

        def halves(word):
            return (plsc.bitcast(lax.shift_left(word, 16), jnp.float32), plsc.bitcast(word & HI_MASK, jnp.float32))

        def compute(p, c, b):
            if phase == "dot":
                lane = lax.iota(jnp.int32, SC_LANES)

                @pl.loop(0, PEER_GATHER_ROWS // SC_LANES)
                def _(g2):
                    vec = jnp.zeros((SC_LANES,), jnp.float32)
                    for gg in range(SC_LANES // PEER_ACC_VREGS):
                        row0 = g2 * SC_LANES + gg * PEER_ACC_VREGS

                        def body(cc, accs):
                            x_lo = aux_v[p, pl.ds(cc * SC_LANES, SC_LANES)]
                            x_hi = aux_v[p, pl.ds(HW + cc * SC_LANES, SC_LANES)]
                            out = []
                            for r in range(PEER_ACC_VREGS):
                                lo, hi = halves(rows_v[b, row0 + r, pl.ds(cc * SC_LANES, SC_LANES)])
                                out.append(accs[r] + lo * x_lo + hi * x_hi)
                            return tuple(out)
                        accs = lax.fori_loop(0, HW // SC_LANES, body,
                                             tuple(jnp.zeros((SC_LANES,), jnp.float32) for _ in range(PEER_ACC_VREGS)))
                        for r in range(PEER_ACC_VREGS):
                            vec = jnp.where(lane == gg * PEER_ACC_VREGS + r, jnp.sum(accs[r]), vec)
                    out_v[p, pl.ds(c * PEER_GATHER_ROWS + g2 * SC_LANES, SC_LANES)] = vec
            else:
                words = PEER_ACC_VREGS // 2

                @pl.loop(0, HW // (words * SC_LANES))
                def _(db):
                    def body(kk, accs):
                        wv = plsc.load_gather(aux_v.at[p], [jnp.full((SC_LANES,), c * PEER_GATHER_ROWS + kk, jnp.int32)])
                        out = []
                        for j in range(words):
                            lo, hi = halves(rows_v[b, kk, pl.ds((db * words + j) * SC_LANES, SC_LANES)])
                            out += [accs[2 * j] + lo * wv, accs[2 * j + 1] + hi * wv]
                        return tuple(out)
                    if c == 0:
                        init = tuple(jnp.zeros((SC_LANES,), jnp.float32) for _ in range(2 * words))
                    else:
                        init = tuple(out_v[p, pl.ds(half * HW + (db * words + j) * SC_LANES, SC_LANES)]
                                     for j in range(words) for half in range(2))
                    accs = lax.fori_loop(0, PEER_GATHER_ROWS, body, init)
                    for j in range(words):
                        out_v[p, pl.ds((db * words + j) * SC_LANES, SC_LANES)] = accs[2 * j]
                        out_v[p, pl.ds(HW + (db * words + j) * SC_LANES, SC_LANES)] = accs[2 * j + 1]

        for d in load_meta(base, 0):
            d.start()
        for d in load_meta(base, 0):
            d.wait()
        for c in range(AHEAD):
            gather(0, c, c % NBUF).start()

        @pl.loop(0, tpw // 2)
        def _(i2):
            for p in range(2):
                i = i2 * 2 + p
                t = base + i
                nxt = base + jnp.minimum(i + 1, tpw - 1)
                for d in load_meta(nxt, 1 - p):
                    d.start()

                @pl.when(i2 > 0)
                def _():
                    store_out(t, p).wait()

                for c in range(PEER_GATHERS):
                    ahead = c + AHEAD
                    if ahead < PEER_GATHERS:
                        gather(p, ahead, ahead % NBUF).start()
                    else:
                        if ahead == PEER_GATHERS:
                            for d in load_meta(nxt, 1 - p):
                                d.wait()
                        gather(1 - p, ahead - PEER_GATHERS, ahead % NBUF).start()
                    gather(p, c, c % NBUF).wait()
                    compute(p, c, c % NBUF)
                store_out(t, p).start()

        for c in range(AHEAD):
            gather(0, c, c % NBUF).wait()
        for p in range(2):
            store_out(base, p).wait()

    return run


def _sc_tokens_per_worker(N):
    assert N % (2 * SC_WORKERS) == 0 and PEER_GATHERS % PEER_ROW_BUFFERS == 0
    return N // SC_WORKERS


_SC_AUX = {"dot": (D_MODEL,), "wsum": (PEER_SLOTS,)}
_SC_OUT = {"dot": (PEER_SLOTS,), "wsum": (D_MODEL,)}


def _sc_scratch(phases):
    s = [pltpu.VMEM((2, PEER_GATHERS, PEER_GATHER_ROWS), jnp.int32),
         pltpu.VMEM((PEER_ROW_BUFFERS, PEER_GATHER_ROWS, PEER_ROW_WORDS), jnp.int32),
         pltpu.SemaphoreType.DMA((PEER_ROW_BUFFERS,)), pltpu.SemaphoreType.DMA((2,)), pltpu.SemaphoreType.DMA((2,))]
    for ph in phases:
        s += [pltpu.VMEM((2,) + _SC_AUX[ph], jnp.float32), pltpu.VMEM((2,) + _SC_OUT[ph], jnp.float32)]
    return s


def _sc_peer_jobs(jobs):
    phases = [j[0] for j in jobs]
    kinds = sorted(set(phases))
    ns = [j[2].shape[0] for j in jobs]
    runs = [_sc_peer_phase(ph, _sc_tokens_per_worker(n)) for ph, n in zip(phases, ns)]
    nj = len(jobs)

    @functools.partial(
        pl.kernel, mesh=plsc.VectorSubcoreMesh(core_axis_name="c", subcore_axis_name="s"),
        out_type=tuple(jax.ShapeDtypeStruct((n,) + _SC_OUT[ph], jnp.float32) for ph, n in zip(phases, ns)),
        compiler_params=pltpu.CompilerParams(needs_layout_passes=False),
        scratch_types=_sc_scratch(kinds),
    )
    def k(*refs):
        ins, outs, scratch = refs[:3 * nj], refs[3 * nj:4 * nj], refs[4 * nj:]
        idx_v, rows_v, sem_r, sem_i, sem_o = scratch[:5]
        bufs = {kind: scratch[5 + 2 * i:7 + 2 * i] for i, kind in enumerate(kinds)}
        worker = lax.axis_index("s") * SC_CORES + lax.axis_index("c")
        for j in range(nj):
            table_hbm, idx_hbm, aux_hbm = ins[3 * j:3 * j + 3]
            aux_v, out_v = bufs[phases[j]]
            runs[j](worker * (ns[j] // SC_WORKERS), table_hbm, idx_hbm, aux_hbm, outs[j],
                    idx_v, aux_v, rows_v, out_v, sem_r, sem_i, sem_o)

    args = []
    for (_, table, idx, aux), n in zip(jobs, ns):
        args += [table, idx.reshape(n, PEER_GATHERS, PEER_GATHER_ROWS), aux]
    return list(k(*args))


PEER_TOKENS = 256
INT_BIG = 2 ** 30
PEER_CANDIDATES = -(-sum(PEER_TOPK // (i + 1) for i in range(PEER_TOPK)) // 8) * 8


def _extract_topk(cand_ref, ids_ref, val_out_ref, id_out_ref, row0):
    def body(r, carry):
        c = cand_ref[...]
        ids = ids_ref[...]
        m = jnp.max(c, axis=0, keepdims=True)
        sel = jnp.min(jnp.where(c == m, ids, INT_BIG), axis=0, keepdims=True)
        cand_ref[...] = jnp.where(ids == sel, -jnp.inf, c)
        val_out_ref[pl.ds(row0 + r, 1), :] = m
        id_out_ref[pl.ds(row0 + r, 1), :] = sel
        return carry
    lax.fori_loop(0, PEER_TOPK, body, 0)


def _peer_retrieve_kernel(x_ref, gain_ref, scale_ref, shift_ref, wq_ref, keys_ref,
                          h_ref, idx_out_ref, gate_out_ref,
                          s_ref, ids1_ref, sv_ref, si_ref, cand_ref, cid_ref, ts_ref, idx_ref, gate_ref):
    x = x_ref[0]
    y = x * lax.rsqrt(jnp.mean(x * x, axis=-1, keepdims=True) + NORM_EPS)
    h = (y * gain_ref[...]) * (1.0 + scale_ref[0]) + shift_ref[0]
    h_ref[0] = h
    q = jnp.dot(h.astype(jnp.bfloat16), wq_ref[...], preferred_element_type=jnp.float32)
    T = PEER_TOKENS
    K = PEER_TOPK
    ids1_ref[...] = lax.broadcasted_iota(jnp.int32, (PEER_N_KEYS, T), 0)
    for hd in range(PEER_HEADS):
        for p in range(2):
            hp = hd * 2 + p
            qs = q[:, hp * PEER_HALF:(hp + 1) * PEER_HALF].astype(jnp.bfloat16)
            s_ref[...] = lax.dot_general(keys_ref[hp], qs, (((1,), (1,)), ((), ())),
                                         preferred_element_type=jnp.float32)
            _extract_topk(s_ref, ids1_ref, sv_ref, si_ref, p * K)
        cand_ref[...] = jnp.full(cand_ref.shape, -jnp.inf, jnp.float32)
        cid_ref[...] = INT_BIG - 1 - lax.broadcasted_iota(jnp.int32, cid_ref.shape, 0)
        off = 0
        for i in range(K):
            n = K // (i + 1)
            cand_ref[off:off + n, :] = sv_ref[i:i + 1, :] + sv_ref[K:K + n, :]
            cid_ref[off:off + n, :] = si_ref[i:i + 1, :] * PEER_N_KEYS + si_ref[K:K + n, :]
            off += n
        _extract_topk(cand_ref, cid_ref, ts_ref, idx_ref, hd * K)
        ts = ts_ref[hd * K:(hd + 1) * K, :]
        e = jnp.exp(ts - jnp.max(ts, axis=0, keepdims=True))
        gate_ref[hd * K:(hd + 1) * K, :] = e / jnp.sum(e, axis=0, keepdims=True)
    idx_out_ref[...] = idx_ref[...].T
    gate_out_ref[...] = gate_ref[...].T


def peer_retrieve(x, gain, scale, shift, w_q, sub_keys):
    B, L, D = x.shape
    T = PEER_TOKENS
    nt = L // T
    keys = sub_keys.reshape(PEER_HEADS * 2, PEER_N_KEYS, PEER_HALF).astype(jnp.bfloat16)
    return pl.pallas_call(
        _peer_retrieve_kernel,
        grid=(B, nt),
        in_specs=[
            pl.BlockSpec((1, T, D), lambda b, i: (b, i, 0)),
            pl.BlockSpec((1, D), lambda b, i: (0, 0)),
            pl.BlockSpec((1, 1, D), lambda b, i: (b, 0, 0)),
            pl.BlockSpec((1, 1, D), lambda b, i: (b, 0, 0)),
            pl.BlockSpec((D, PEER_HEADS * 2 * PEER_HALF), lambda b, i: (0, 0)),
            pl.BlockSpec((PEER_HEADS * 2, PEER_N_KEYS, PEER_HALF), lambda b, i: (0, 0, 0)),
        ],
        out_specs=[
            pl.BlockSpec((1, T, D), lambda b, i: (b, i, 0)),
            pl.BlockSpec((T, PEER_SLOTS), lambda b, i: (b * nt + i, 0)),
            pl.BlockSpec((T, PEER_SLOTS), lambda b, i: (b * nt + i, 0)),
        ],
        out_shape=[
            jax.ShapeDtypeStruct((B, L, D), jnp.float32),
            jax.ShapeDtypeStruct((B * L, PEER_SLOTS), jnp.int32),
            jax.ShapeDtypeStruct((B * L, PEER_SLOTS), jnp.float32),
        ],
        scratch_shapes=[
            pltpu.VMEM((PEER_N_KEYS, T), jnp.float32),
            pltpu.VMEM((PEER_N_KEYS, T), jnp.int32),
            pltpu.VMEM((2 * PEER_TOPK, T), jnp.float32),
            pltpu.VMEM((2 * PEER_TOPK, T), jnp.int32),
            pltpu.VMEM((PEER_CANDIDATES, T), jnp.float32),
            pltpu.VMEM((PEER_CANDIDATES, T), jnp.int32),
            pltpu.VMEM((PEER_SLOTS, T), jnp.float32),
            pltpu.VMEM((PEER_SLOTS, T), jnp.int32),
            pltpu.VMEM((PEER_SLOTS, T), jnp.float32),
        ],
        compiler_params=pltpu.CompilerParams(dimension_semantics=("parallel", "parallel"),
                                             vmem_limit_bytes=VMEM_LIMIT_BYTES),
        name="peer_retrieve",
    )(x, gain.reshape(1, D), scale, shift, w_q.astype(jnp.bfloat16), keys)


PEER_ACT_ROWS = 256


def _peer_act_kernel(dots_ref, gate_ref, w_ref):
    a = dots_ref[...]
    w_ref[...] = gate_ref[...] * (0.5 * a * (1.0 + lax.erf(a * (2.0 ** -0.5))))


def peer_act(dots, gate):
    N = dots.shape[0]
    T = min(PEER_ACT_ROWS, N)
    spec = pl.BlockSpec((T, PEER_SLOTS), lambda i: (i, 0))
    return pl.pallas_call(
        _peer_act_kernel,
        grid=(N // T,),
        in_specs=[spec, spec],
        out_specs=spec,
        out_shape=jax.ShapeDtypeStruct((N, PEER_SLOTS), jnp.float32),
        compiler_params=pltpu.CompilerParams(dimension_semantics=("parallel",)),
        name="peer_act",
    )(dots, gate)


OUT_PROJ_ROWS = 512


def _out_proj_kernel(x_ref, g_ref, rw_ref, ml_ref, hy_ref, w_ref, o_ref):
    bf = jnp.bfloat16
    y = jnp.dot(rw_ref[0].astype(bf), w_ref[:RW_WIDTH, :], preferred_element_type=jnp.float32)
    y += jnp.dot(ml_ref[0].astype(bf), w_ref[RW_WIDTH:RW_WIDTH + MLA_WIDTH, :], preferred_element_type=jnp.float32)
    y += jnp.dot(hy_ref[0].astype(bf), w_ref[RW_WIDTH + MLA_WIDTH:, :], preferred_element_type=jnp.float32)
    o_ref[0] = x_ref[0] + g_ref[0] * y


def mix_out_proj(x, gate, rw, ml, hy, w_out):
    B, L, D = x.shape
    T = min(OUT_PROJ_ROWS, L)
    tok = lambda w: pl.BlockSpec((1, T, w), lambda b, i: (b, i, 0))
    return pl.pallas_call(
        _out_proj_kernel,
        grid=(B, L // T),
        in_specs=[tok(D), pl.BlockSpec((1, 1, D), lambda b, i: (b, 0, 0)), tok(RW_WIDTH), tok(MLA_WIDTH), tok(HY_WIDTH),
                  pl.BlockSpec((MIX_WIDTH, D), lambda b, i: (0, 0))],
        out_specs=tok(D),
        out_shape=jax.ShapeDtypeStruct((B, L, D), jnp.float32),
        compiler_params=pltpu.CompilerParams(dimension_semantics=("parallel", "parallel"),
                                             vmem_limit_bytes=VMEM_LIMIT_BYTES),
        name="mix_out_proj",
    )(x, gate, rw, ml, hy, w_out.astype(jnp.bfloat16))


def _mix_and_retrieve(li, x, c, ctx, c_ctx, mod_w, mod_b, mix_norm, w_in, w_out, rw_conv, rw_decay_up, rw_decay0, rw_a_up, rw_a0, rw_gate_up, rw_k_k, rw_k_a, rw_r_k, rw_gn_g, rw_gn_b, mla_q_norm, mla_w_uq, mla_kv_norm, mla_w_ukv, mla_q_gain, mla_k_gain, hy_conv, hy_w1, hy_b1, hy_freq1, hy_w2, hy_b2, hy_freq2, hy_w3, hy_b3, hy_bias, ffn_norm, peer_wq, peer_keys, peer_u, peer_v):
    B, L, D = x.shape
    need_ctx = li < DEPTH - 1
    mod_l = (jax.nn.silu(c) @ mod_w[li] + mod_b[li])[:, None, :]
    mod_c = (jax.nn.silu(c_ctx) @ mod_w[li] + mod_b[li])[None, None, :]
    shm_l, scm_l, gm_l, shf_l, scf_l, gf_l = jnp.split(mod_l, N_MOD, axis=-1)
    shm_c, scm_c, gm_c, shf_c, scf_c, gf_c = jnp.split(mod_c, N_MOD, axis=-1)

    widths = (RW_PROJ, MLA_PROJ, HY_PROJ)
    prw_l, pml_l, phy_l = norm_mod_proj(x, mix_norm[li], scm_l, shm_l, w_in[li], widths, 512)
    prw_c, pml_c, phy_c = norm_mod_proj(ctx, mix_norm[li], jnp.broadcast_to(scm_c, (B, 1, D)),
                                        jnp.broadcast_to(shm_c, (B, 1, D)), w_in[li], widths, 256)
    rw_l, rw_c = rwkv7_mixer(prw_l, prw_c, rw_conv[li], rw_decay_up[li], rw_decay0[li],
                             rw_a_up[li], rw_a0[li], rw_gate_up[li], rw_k_k[li], rw_k_a[li], rw_r_k[li],
                             rw_gn_g[li], rw_gn_b[li], need_ctx)
    ml_l, ml_c = mla_mixer(pml_l, pml_c, mla_q_norm[li], mla_w_uq[li],
                           mla_kv_norm[li], mla_w_ukv[li], mla_q_gain[li], mla_k_gain[li], need_ctx)
    hy_prm = (hy_conv[li], hy_w1[li], hy_b1[li], hy_freq1[li], hy_w2[li], hy_b2[li], hy_freq2[li],
              hy_w3[li], hy_b3[li], hy_bias[li])
    hy_l = hyena_mixer(phy_l, *hy_prm)
    x = mix_out_proj(x, gm_l, rw_l, ml_l, hy_l, w_out[li])
    h, e_idx, gate = peer_retrieve(x, ffn_norm[li], scf_l, shf_l, peer_wq[li], peer_keys[li])
    streams = [(x, gf_l, h.reshape(B * L, D), e_idx, gate)]
    if need_ctx:
        hy_c = hyena_mixer(phy_c, *hy_prm)
        ctx = mix_out_proj(ctx, jnp.broadcast_to(gm_c, (B, 1, D)), rw_c, ml_c, hy_c, w_out[li])
        h, e_idx, gate = peer_retrieve(ctx, ffn_norm[li], jnp.broadcast_to(scf_c, (B, 1, D)),
                                       jnp.broadcast_to(shf_c, (B, 1, D)), peer_wq[li], peer_keys[li])
        streams.append((ctx, gf_c, h.reshape(-1, D), e_idx, gate))
    return streams


BATCH_GROUP_ROWS = (2, 2, 2, 2)


def kernel(x, c, ctx, c_ctx, mod_w, mod_b, mix_norm, w_in, w_out, rw_conv, rw_decay_up, rw_decay0, rw_a_up, rw_a0, rw_gate_up, rw_k_k, rw_k_a, rw_r_k, rw_gn_g, rw_gn_b, mla_q_norm, mla_w_uq, mla_kv_norm, mla_w_ukv, mla_q_gain, mla_k_gain, hy_conv, hy_w1, hy_b1, hy_freq1, hy_w2, hy_b2, hy_freq2, hy_w3, hy_b3, hy_bias, ffn_norm, peer_wq, peer_keys, peer_u, peer_v):
    params = (mod_w, mod_b, mix_norm, w_in, w_out, rw_conv, rw_decay_up, rw_decay0, rw_a_up, rw_a0, rw_gate_up,
              rw_k_k, rw_k_a, rw_r_k, rw_gn_g, rw_gn_b, mla_q_norm, mla_w_uq, mla_kv_norm, mla_w_ukv, mla_q_gain,
              mla_k_gain, hy_conv, hy_w1, hy_b1, hy_freq1, hy_w2, hy_b2, hy_freq2, hy_w3, hy_b3, hy_bias,
              ffn_norm, peer_wq, peer_keys)
    peer_u = [pack_expert_table(peer_u[li]) for li in range(DEPTH)]
    peer_v = [pack_expert_table(peer_v[li]) for li in range(DEPTH)]
    params = params + (peer_u, peer_v)
    assert sum(BATCH_GROUP_ROWS) == x.shape[0]
    G = len(BATCH_GROUP_ROWS)
    lo = [sum(BATCH_GROUP_ROWS[:g]) for g in range(G + 1)]
    L, D = x.shape[1:]
    xs = [x[lo[g]:lo[g + 1]] for g in range(G)]
    cs = [c[lo[g]:lo[g + 1]] for g in range(G)]
    ctxs = [ctx[lo[g]:lo[g + 1]] for g in range(G)]
    stages = [(li, g) for li in range(DEPTH) for g in range(G)]
    prev = None
    token = None

    def advance(prev, token, li, streams):
        pstreams, pdots = ([], []) if prev is None else (prev[2], prev[3])
        if pdots:
            token, pdots = lax.optimization_barrier((token, pdots))
        dots = []
        for s in range(max(len(pstreams), len(streams))):
            jobs = []
            if s < len(pstreams):
                w = peer_act(pdots[s], pstreams[s][4])
                token = w if s == 0 else token
                jobs.append(("wsum", peer_v[prev[0]], pstreams[s][3], w))
            if s < len(streams):
                jobs.append(("dot", peer_u[li], streams[s][3], streams[s][2]))
            outs = _sc_peer_jobs(jobs)
            if s < len(pstreams):
                res = pstreams[s][0] + pstreams[s][1] * outs[0].reshape(pstreams[s][0].shape)
                (xs if s == 0 else ctxs)[prev[1]] = res
            if s < len(streams):
                dots.append(outs[-1])
        return token, dots

    for li, g in stages:
        ins = (xs[g], ctxs[g])
        if token is not None:
            token, ins = lax.optimization_barrier((token, ins))
        streams = _mix_and_retrieve(li, ins[0], cs[g], ins[1], c_ctx, *params)
        token, dots = advance(prev, streams[0][4], li, streams)
        prev = (li, g, streams, dots)
    advance(prev, token, None, [])
    return jnp.concatenate(xs, axis=0)
```

```python
import functools
import math

import jax
import jax.numpy as jnp
import numpy as np
from jax import lax
from jax.experimental import pallas as pl
from jax.experimental.pallas import tpu as pltpu
from jax.experimental.pallas import tpu_sc as plsc

D_MODEL = 1024
DEPTH = 2
GRID_W = 64
N_MOD = 6
NORM_EPS = 1e-6
SHORT_CONV = 3

RW_HEADS = 6
RW_HEAD_DIM = 64
RW_WIDTH = RW_HEADS * RW_HEAD_DIM
RW_DECAY_RANK = 64
RW_A_RANK = 64
RW_GATE_RANK = 128
RW_DECAY_SCALE = 0.6065306597
RW_GN_EPS = 64e-5
L2_EPS = 1e-12

MLA_HEADS = 6
MLA_Q_RANK = 256
MLA_KV_RANK = 128
MLA_NOPE_DIM = 64
MLA_ROPE_DIM = 32
MLA_V_DIM = 64
MLA_QK_DIM = MLA_NOPE_DIM + MLA_ROPE_DIM
MLA_WIDTH = MLA_HEADS * MLA_V_DIM
AXIS_ROPE_DIM = MLA_ROPE_DIM // 2
ROPE_THETA = 10000.0

HY_WIDTH = 256
HY_ORDER = 2
HY_POS_BANDS = 16
HY_SHORT_DECAY_PCT = 0.3
HY_LONG_DECAY_PCT = 1.5
HY_DECAY_TARGET = 1e-2

PEER_HEADS = 8
PEER_N_KEYS = 128
PEER_TOPK = 16
PEER_QUERY_DIM = 256
PEER_HALF = PEER_QUERY_DIM // 2

RW_PROJ = 3 * RW_WIDTH + RW_DECAY_RANK + RW_A_RANK + RW_GATE_RANK
MLA_PROJ = MLA_Q_RANK + MLA_KV_RANK + MLA_ROPE_DIM
HY_PROJ = (HY_ORDER + 1) * HY_WIDTH
MIX_WIDTH = RW_WIDTH + MLA_WIDTH + HY_WIDTH

VMEM_LIMIT_BYTES = 48 * 1024 * 1024


def _norm_mod_proj_kernel(x_ref, gain_ref, scale_ref, shift_ref, *refs):
    w_refs, o_refs = refs[:len(refs) // 2], refs[len(refs) // 2:]
    x = x_ref[0]
    y = x * lax.rsqrt(jnp.mean(x * x, axis=-1, keepdims=True) + NORM_EPS)
    y = y * gain_ref[...]
    y = (y * (1.0 + scale_ref[0]) + shift_ref[0]).astype(jnp.bfloat16)
    for w_ref, o_ref in zip(w_refs, o_refs):
        o_ref[0] = jnp.dot(y, w_ref[...], preferred_element_type=jnp.float32)


def norm_mod_proj(x, gain, scale, shift, w, widths, block_rows):
    B, L, D = x.shape
    assert sum(widths) == w.shape[1]
    offs = [sum(widths[:i]) for i in range(len(widths))]
    ws = [w[:, o:o + n].astype(jnp.bfloat16) for o, n in zip(offs, widths)]
    return pl.pallas_call(
        _norm_mod_proj_kernel,
        grid=(B, L // block_rows),
        in_specs=[
            pl.BlockSpec((1, block_rows, D), lambda b, i: (b, i, 0)),
            pl.BlockSpec((1, D), lambda b, i: (0, 0)),
            pl.BlockSpec((1, 1, D), lambda b, i: (b, 0, 0)),
            pl.BlockSpec((1, 1, D), lambda b, i: (b, 0, 0)),
        ] + [pl.BlockSpec((D, n), lambda b, i: (0, 0)) for n in widths],
        out_specs=[pl.BlockSpec((1, block_rows, n), lambda b, i: (b, i, 0)) for n in widths],
        out_shape=[jax.ShapeDtypeStruct((B, L, n), jnp.float32) for n in widths],
        compiler_params=pltpu.CompilerParams(
            dimension_semantics=("parallel", "parallel"), vmem_limit_bytes=VMEM_LIMIT_BYTES),
        name="norm_mod_proj",
    )(x, gain.reshape(1, D), scale, shift, *ws)


CONV_ROWS = 512
SUBLANES = 8


def _short_conv_kernel(x_ref, prev_ref, next_ref, w_ref, o_ref):
    i = pl.program_id(1)
    x = x_ref[0]
    T = x.shape[0]
    row = lax.broadcasted_iota(jnp.int32, x.shape, 0)
    before = jnp.where(i == 0, 0.0, prev_ref[0, SUBLANES - 1:SUBLANES, :])
    after = jnp.where(i == pl.num_programs(1) - 1, 0.0, next_ref[0, 0:1, :])
    up = jnp.where(row == 0, before, pltpu.roll(x, 1, 0))
    down = jnp.where(row == T - 1, after, pltpu.roll(x, T - 1, 0))
    o_ref[0] = up * w_ref[0:1, :] + x * w_ref[1:2, :] + down * w_ref[2:3, :]


def short_conv(x, w):
    B, L, C = x.shape
    T = min(CONV_ROWS, L)
    per = T // SUBLANES
    last = L // SUBLANES - 1
    return pl.pallas_call(
        _short_conv_kernel,
        grid=(B, L // T),
        in_specs=[pl.BlockSpec((1, T, C), lambda b, i: (b, i, 0)),
                  pl.BlockSpec((1, SUBLANES, C), lambda b, i: (b, jnp.maximum(i * per - 1, 0), 0)),
                  pl.BlockSpec((1, SUBLANES, C), lambda b, i: (b, jnp.minimum((i + 1) * per, last), 0)),
                  pl.BlockSpec((SHORT_CONV, C), lambda b, i: (0, 0))],
        out_specs=pl.BlockSpec((1, T, C), lambda b, i: (b, i, 0)),
        out_shape=jax.ShapeDtypeStruct((B, L, C), jnp.float32),
        compiler_params=pltpu.CompilerParams(dimension_semantics=("parallel", "parallel"),
                                             vmem_limit_bytes=VMEM_LIMIT_BYTES),
        name="short_conv",
    )(x, x, x, w)


RW_CHUNK = 64


def _rwkv_chunk_kernel(r_ref, kk_ref, v_ref, lw_ref, akk_ref, kr_ref, y_ref, h_ref):
    d = pl.program_id(0)
    n = pl.program_id(2)

    @pl.when(n == 0)
    def _():
        h_ref[...] = jnp.zeros_like(h_ref)

    C = RW_CHUNK
    row = lax.broadcasted_iota(jnp.int32, (C, C), 0)
    col = lax.broadcasted_iota(jnp.int32, (C, C), 1)
    lag = (row - col) * (1 - 2 * d)
    before = lag > 0
    upto = lag >= 0
    tri = upto.astype(jnp.float32)
    eye = (row == col).astype(jnp.float32)
    bf = jnp.bfloat16
    f32 = jnp.float32

    def mm(a, b):
        return jnp.dot(a.astype(bf), b.astype(bf), preferred_element_type=f32)

    def mm_nt(a, b):
        return lax.dot_general(a.astype(bf), b.astype(bf), (((1,), (1,)), ((), ())), preferred_element_type=f32)

    def mm_tn(a, b):
        return lax.dot_general(a.astype(bf), b.astype(bf), (((0,), (0,)), ((), ())), preferred_element_type=f32)

    hs = range(RW_HEADS)
    HD = RW_HEAD_DIM
    heads = lambda t: [t[:, h * HD:(h + 1) * HD] for h in hs]
    r = heads(r_ref[0])
    kk = heads(kk_ref[0])
    v = heads(v_ref[0])
    lw = heads(lw_ref[0, 0])
    akk = heads(akk_ref[0, 0])
    kr = heads(kr_ref[0, 0])
    G = [jnp.dot(tri, lw[h], preferred_element_type=f32, precision=lax.Precision.HIGHEST) for h in hs]
    gtot = [jnp.sum(lw[h], axis=0, keepdims=True) for h in hs]
    Einv = [jnp.exp(-G[h]) for h in hs]
    At = [-kk[h] * jnp.exp(G[h] - lw[h]) for h in hs]
    Rt = [r[h] * jnp.exp(G[h]) for h in hs]
    Bt = [akk[h] * Einv[h] for h in hs]
    Kt = [kr[h] * Einv[h] for h in hs]
    X = [mm_nt(jnp.concatenate([At[h], Rt[h]], axis=0), jnp.concatenate([Bt[h], Kt[h]], axis=0)) for h in hs]
    M_ab = [jnp.where(before, X[h][:C, :C], 0.0) for h in hs]
    M_ak = [jnp.where(before, X[h][:C, C:], 0.0) for h in hs]
    A_rb = [jnp.where(upto, X[h][C:, :C], 0.0) for h in hs]
    A_rk = [jnp.where(upto, X[h][C:, C:], 0.0) for h in hs]
    MV = [mm(M_ak[h], v[h]) for h in hs]
    Mp = M_ab
    T = [eye + Mp[h] for h in hs]
    for _ in range(5):
        Mp = [jnp.dot(Mp[h], Mp[h], preferred_element_type=f32) for h in hs]
        T = [T[h] + jnp.dot(T[h], Mp[h], preferred_element_type=f32) for h in hs]
    WU = [jnp.dot(T[h], jnp.concatenate([At[h], MV[h]], axis=1), preferred_element_type=f32) for h in hs]
    H0 = [h_ref[h] for h in hs]
    Ehat = [jnp.exp(gtot[h] - G[h]) for h in hs]
    Om = [Rt[h] + mm(A_rb[h], WU[h][:, :HD]) for h in hs]
    Y0 = [mm(A_rb[h], WU[h][:, HD:]) + mm(A_rk[h], v[h]) for h in hs]
    BW = [mm_tn(akk[h] * Ehat[h], WU[h]) for h in hs]
    KV = [mm_tn(kr[h] * Ehat[h], v[h]) for h in hs]
    y_ref[0, 0] = jnp.concatenate([jnp.dot(Om[h], H0[h], preferred_element_type=f32) + Y0[h] for h in hs], axis=1)
    for h in hs:
        P = eye * jnp.exp(gtot[h]) + BW[h][:, :HD]
        h_ref[h] = jnp.dot(P, H0[h], preferred_element_type=f32) + BW[h][:, HD:] + KV[h]


def rwkv_chunked(r, kk, v, lw, akk, kr, n_ctx):
    B, T, W = r.shape
    H = W // RW_HEAD_DIM
    nc = n_ctx // RW_CHUNK
    nt = T // RW_CHUNK

    def chunk_of(d, n):
        bwd = jnp.where(n < nc, nc - 1 - n, nt - 1 - (n - nc))
        return jnp.where(d == 0, n, bwd)

    spec1 = pl.BlockSpec((1, RW_CHUNK, W), lambda d, b, n: (b, chunk_of(d, n), 0))
    spec2 = pl.BlockSpec((1, 1, RW_CHUNK, W), lambda d, b, n: (d, b, chunk_of(d, n), 0))
    return pl.pallas_call(
        _rwkv_chunk_kernel,
        grid=(2, B, nt),
        in_specs=[spec1, spec1, spec1, spec2, spec2, spec2],
        out_specs=spec2,
        out_shape=jax.ShapeDtypeStruct((2, B, T, W), jnp.float32),
        scratch_shapes=[pltpu.VMEM((H, RW_HEAD_DIM, RW_HEAD_DIM), jnp.float32)],
        compiler_params=pltpu.CompilerParams(dimension_semantics=("parallel", "parallel", "arbitrary")),
        name="rwkv_chunked",
    )(r, kk, v, lw, akk, kr)


LANE = 128
RW_PREP_ROWS = 256
MLA_PAD_WIDTH = MLA_HEADS * LANE
MLA_PREP_ROWS = 256
ATTN_Q_ROWS = 512


def _split_dot(x, m):
    hi = x.astype(jnp.bfloat16)
    lo = (x - hi.astype(jnp.float32)).astype(jnp.bfloat16)
    return (jnp.dot(hi, m, preferred_element_type=jnp.float32) + jnp.dot(lo, m, preferred_element_type=jnp.float32))


def _rwkv_prep_kernel(z_ref, wda_ref, d0_ref, a0_ref, gup_ref, kk_ref_w, ka_ref, rk_ref, hsum_ref,
                      r_ref, kk_ref, v_ref, lw_ref, akk_ref, kr_ref, g_ref, bonus_ref):
    W = RW_WIDTH
    bf = jnp.bfloat16
    z = z_ref[0]
    r, k, v = z[:, :W], z[:, W:2 * W], z[:, 2 * W:3 * W]
    da = z[:, 3 * W:3 * W + LANE]
    lane = lax.broadcasted_iota(jnp.int32, da.shape, 1)
    da = jnp.where(lane < RW_DECAY_RANK, jnp.tanh(da), da)
    up = jnp.dot(da.astype(bf), wda_ref[...], preferred_element_type=jnp.float32)
    g_lo = z[:, 3 * W + LANE:]
    g_ref[0] = jnp.dot(jax.nn.sigmoid(g_lo).astype(bf), gup_ref[...], preferred_element_type=jnp.float32)
    hsum = hsum_ref[...]
    kk = k * kk_ref_w[...]
    kk = kk * lax.rsqrt(_split_dot(kk * kk, hsum) + L2_EPS)
    r_ref[0] = r
    v_ref[0] = v
    kk_ref[0] = kk
    bonus_ref[0] = _split_dot(r * k * rk_ref[...], hsum) * v
    for d in range(2):
        lw_ref[d, 0] = -RW_DECAY_SCALE * jax.nn.sigmoid(d0_ref[d:d + 1, :] + up[:, d * W:(d + 1) * W])
        a = jax.nn.sigmoid(a0_ref[d:d + 1, :] + up[:, (2 + d) * W:(3 + d) * W])
        akk_ref[d, 0] = kk * a
        kr_ref[d, 0] = k * (1.0 + (a - 1.0) * ka_ref[...])


def rwkv_prep(z, decay_up, decay0, a_up, a0, gate_up, k_k, k_a, r_k):
    B, L, _ = z.shape
    W = RW_WIDTH
    T = min(RW_PREP_ROWS, L)
    zero = jnp.zeros((RW_DECAY_RANK, 2 * W), jnp.float32)
    wda = jnp.concatenate([
        jnp.concatenate([decay_up[0], decay_up[1], zero], axis=1),
        jnp.concatenate([zero, a_up[0], a_up[1]], axis=1)], axis=0).astype(jnp.bfloat16)
    head = jnp.arange(W) // RW_HEAD_DIM
    hsum = (head[:, None] == head[None, :]).astype(jnp.bfloat16)
    row = lambda a: a.reshape(1, W)
    const = lambda a: pl.BlockSpec(a.shape, lambda b, i: (0,) * a.ndim)
    tok = pl.BlockSpec((1, T, W), lambda b, i: (b, i, 0))
    tok2 = pl.BlockSpec((2, 1, T, W), lambda b, i: (0, b, i, 0))
    f1 = jax.ShapeDtypeStruct((B, L, W), jnp.float32)
    f2 = jax.ShapeDtypeStruct((2, B, L, W), jnp.float32)
    args = (z, wda, decay0, a0, gate_up.astype(jnp.bfloat16), row(k_k), row(k_a), row(r_k), hsum)
    return pl.pallas_call(
        _rwkv_prep_kernel,
        grid=(B, L // T),
        in_specs=[pl.BlockSpec((1, T, RW_PROJ), lambda b, i: (b, i, 0))] + [const(a) for a in args[1:]],
        out_specs=[tok, tok, tok, tok2, tok2, tok2, tok, tok],
        out_shape=[f1, f1, f1, f2, f2, f2, f1, f1],
        compiler_params=pltpu.CompilerParams(dimension_semantics=("parallel", "parallel"),
                                             vmem_limit_bytes=VMEM_LIMIT_BYTES),
        name="rwkv_prep",
    )(*args)


def _rwkv_readout_kernel(y_ref, g_ref, bonus_ref, gng_ref, gnb_ref, hsum_ref, o_ref):
    y = y_ref[0, 0] + y_ref[1, 0]
    hsum = hsum_ref[...]
    mu = _split_dot(y, hsum) * (1.0 / RW_HEAD_DIM)
    d = y - mu
    var = _split_dot(d * d, hsum) * (1.0 / RW_HEAD_DIM)
    yn = d * lax.rsqrt(var + RW_GN_EPS) * gng_ref[...] + gnb_ref[...]
    o_ref[0] = (yn + bonus_ref[0]) * g_ref[0]


def rwkv_readout(y, g, bonus, gn_g, gn_b, t0):
    B, L, W = g.shape
    T = min(RW_PREP_ROWS, L)
    off = t0 // T
    head = jnp.arange(W) // RW_HEAD_DIM
    hsum = (head[:, None] == head[None, :]).astype(jnp.bfloat16)
    tok = pl.BlockSpec((1, T, W), lambda b, i: (b, i, 0))
    const = lambda a: pl.BlockSpec(a.shape, lambda b, i: (0,) * a.ndim)
    gg, gb = gn_g.reshape(1, W), gn_b.reshape(1, W)
    return pl.pallas_call(
        _rwkv_readout_kernel,
        grid=(B, L // T),
        in_specs=[pl.BlockSpec((2, 1, T, W), lambda b, i: (0, b, i + off, 0)), tok, tok, const(gg), const(gb), const(hsum)],
        out_specs=tok,
        out_shape=jax.ShapeDtypeStruct((B, L, W), jnp.float32),
        compiler_params=pltpu.CompilerParams(dimension_semantics=("parallel", "parallel")),
        name="rwkv_readout",
    )(y, g, bonus, gg, gb, hsum)


def rwkv7_mixer(p_lat, p_ctx, conv_w, decay_up, decay0, a_up, a0, gate_up, k_k, k_a, r_k, gn_g, gn_b, need_ctx):
    prm = (decay_up, decay0, a_up, a0, gate_up, k_k, k_a, r_k)
    lat = rwkv_prep(short_conv(p_lat, conv_w), *prm)
    ctx = rwkv_prep(short_conv(p_ctx, conv_w), *prm)
    n_ctx = p_ctx.shape[1]
    seq = lambda i: jnp.concatenate([ctx[i], lat[i]], axis=-2)
    y = rwkv_chunked(seq(0), seq(1), seq(2), seq(3), seq(4), seq(5), n_ctx)
    out_l = rwkv_readout(y, lat[6], lat[7], gn_g, gn_b, n_ctx)
    out_c = rwkv_readout(y, ctx[6], ctx[7], gn_g, gn_b, 0) if need_ctx else None
    return out_l, out_c


def _rope_tables(L, use_rope):
    lane = np.arange(LANE)
    in_rope = (lane >= MLA_NOPE_DIM) & (lane < MLA_QK_DIM)
    j = lane - MLA_NOPE_DIM
    axis = j // AXIS_ROPE_DIM
    half = AXIS_ROPE_DIM // 2
    f = j % half
    first = (j % AXIS_ROPE_DIM) < half
    inv = ROPE_THETA ** (-jnp.arange(0, AXIS_ROPE_DIM, 2, dtype=jnp.float32) / AXIS_ROPE_DIM)
    t = jnp.arange(L)
    pos = jnp.stack([t // GRID_W, t % GRID_W], axis=-1).astype(jnp.float32)
    ang = pos[:, np.clip(axis, 0, 1)] * inv[np.clip(f, 0, half - 1)][None, :]
    rope_on = jnp.asarray(in_rope)[None, :] & use_rope
    cos = jnp.where(rope_on, jnp.cos(ang), 1.0)
    sin = jnp.where(rope_on, jnp.sin(ang) * jnp.where(jnp.asarray(first), -1.0, 1.0)[None, :], 0.0)
    return jnp.tile(cos, (1, MLA_HEADS)), jnp.tile(sin, (1, MLA_HEADS))


def _mla_prep_kernel(p_ref, qn_ref, wq_ref, kvn_ref, wk_ref, wv_ref, place_ref, qg_ref, kg_ref, hsum_ref, cos_ref, sin_ref,
                     q_ref, k_ref, v_ref):
    bf = jnp.bfloat16
    p = p_ref[0]
    c_q = p[:, :MLA_Q_RANK]
    c_kv = p[:, MLA_Q_RANK:MLA_Q_RANK + MLA_KV_RANK]
    tail = p[:, MLA_Q_RANK + MLA_KV_RANK:]
    cqn = c_q * lax.rsqrt(jnp.mean(c_q * c_q, axis=-1, keepdims=True) + NORM_EPS) * qn_ref[...]
    ckn = c_kv * lax.rsqrt(jnp.mean(c_kv * c_kv, axis=-1, keepdims=True) + NORM_EPS) * kvn_ref[...]
    q = jnp.dot(cqn.astype(bf), wq_ref[...], preferred_element_type=jnp.float32)
    k = jnp.dot(ckn.astype(bf), wk_ref[...], preferred_element_type=jnp.float32) + _split_dot(tail, place_ref[...])
    v_ref[0] = jnp.dot(ckn.astype(bf), wv_ref[...], preferred_element_type=jnp.float32).astype(bf)
    hsum = hsum_ref[...]
    cos, sin = cos_ref[...], sin_ref[...]
    lane = lax.broadcasted_iota(jnp.int32, q.shape, 1)
    first = ((lane - MLA_NOPE_DIM) % AXIS_ROPE_DIM) < (AXIS_ROPE_DIM // 2)
    half = AXIS_ROPE_DIM // 2

    def finish(x, gain):
        x = x * lax.rsqrt(_split_dot(x * x, hsum) * (1.0 / MLA_QK_DIM) + NORM_EPS) * gain
        partner = jnp.where(first, pltpu.roll(x, MLA_PAD_WIDTH - half, 1), pltpu.roll(x, half, 1))
        return x * cos + partner * sin

    q_ref[0] = (finish(q, qg_ref[...]) * (MLA_QK_DIM ** -0.5)).astype(bf)
    k_ref[0] = finish(k, kg_ref[...]).astype(bf)


def mla_prep(p, use_rope, q_norm, w_uq, kv_norm, w_ukv, q_gain, k_gain):
    B, L, _ = p.shape
    T = min(MLA_PREP_ROWS, L)
    H = MLA_HEADS
    pad_cols = lambda w, d: jnp.pad(w.reshape(w.shape[0], H, d), ((0, 0), (0, 0), (0, LANE - d))).reshape(w.shape[0], H * LANE)
    wq = pad_cols(w_uq, MLA_QK_DIM).astype(jnp.bfloat16)
    ukv = w_ukv.reshape(MLA_KV_RANK, H, MLA_NOPE_DIM + MLA_V_DIM)
    wk = pad_cols(ukv[:, :, :MLA_NOPE_DIM].reshape(MLA_KV_RANK, H * MLA_NOPE_DIM), MLA_NOPE_DIM).astype(jnp.bfloat16)
    wv = ukv[:, :, MLA_NOPE_DIM:].reshape(MLA_KV_RANK, H * MLA_V_DIM).astype(jnp.bfloat16)
    lane = np.arange(H * LANE)
    place = jnp.asarray(((lane[None, :] % LANE) - MLA_NOPE_DIM == np.arange(MLA_ROPE_DIM)[:, None]), jnp.bfloat16)
    hsum = jnp.asarray((lane[:, None] // LANE) == (lane[None, :] // LANE), jnp.bfloat16)
    pad_gain = lambda g: jnp.tile(jnp.pad(g, (0, LANE - MLA_QK_DIM)), H).reshape(1, H * LANE)
    cos, sin = _rope_tables(L, use_rope)
    const = lambda a: pl.BlockSpec(a.shape, lambda b, i: (0,) * a.ndim)
    args = (p, q_norm.reshape(1, -1), wq, kv_norm.reshape(1, -1), wk, wv, place, pad_gain(q_gain), pad_gain(k_gain), hsum)
    pos = pl.BlockSpec((T, H * LANE), lambda b, i: (i, 0))
    return pl.pallas_call(
        _mla_prep_kernel,
        grid=(B, L // T),
        in_specs=[pl.BlockSpec((1, T, MLA_PROJ), lambda b, i: (b, i, 0))] + [const(a) for a in args[1:]] + [pos, pos],
        out_specs=[pl.BlockSpec((1, T, H * LANE), lambda b, i: (b, i, 0)), pl.BlockSpec((1, T, H * LANE), lambda b, i: (b, i, 0)),
                   pl.BlockSpec((1, T, MLA_WIDTH), lambda b, i: (b, i, 0))],
        out_shape=[jax.ShapeDtypeStruct((B, L, H * LANE), jnp.bfloat16), jax.ShapeDtypeStruct((B, L, H * LANE), jnp.bfloat16),
                   jax.ShapeDtypeStruct((B, L, MLA_WIDTH), jnp.bfloat16)],
        compiler_params=pltpu.CompilerParams(dimension_semantics=("parallel", "parallel"),
                                             vmem_limit_bytes=VMEM_LIMIT_BYTES),
        name="mla_prep",
    )(*args, cos, sin)


def _attn_kernel(q_ref, k_ref, v_ref, o_ref):
    lane = lax.broadcasted_iota(jnp.int32, (q_ref.shape[1], LANE), 1)
    for pair in range(MLA_HEADS // 2):
        v_pair = v_ref[0, :, pair * LANE:(pair + 1) * LANE]
        outs = []
        for h in (2 * pair, 2 * pair + 1):
            q = q_ref[0, :, h * LANE:(h + 1) * LANE]
            k = k_ref[0, :, h * LANE:(h + 1) * LANE]
            s = lax.dot_general(q, k, (((1,), (1,)), ((), ())), preferred_element_type=jnp.float32)
            e = jnp.exp(s - jnp.max(s, axis=-1, keepdims=True))
            o = jnp.dot(e.astype(jnp.bfloat16), v_pair, preferred_element_type=jnp.float32)
            outs.append(o / jnp.sum(e, axis=-1, keepdims=True))
        o_ref[0, :, pair * LANE:(pair + 1) * LANE] = jnp.where(lane < MLA_V_DIM, outs[0], outs[1])


def attention(q, k, v):
    B, Lq, P = q.shape
    Lk = k.shape[1]
    tq = min(ATTN_Q_ROWS, Lq)
    return pl.pallas_call(
        _attn_kernel,
        grid=(B, Lq // tq),
        in_specs=[pl.BlockSpec((1, tq, P), lambda b, i: (b, i, 0)),
                  pl.BlockSpec((1, Lk, P), lambda b, i: (b, 0, 0)),
                  pl.BlockSpec((1, Lk, MLA_WIDTH), lambda b, i: (b, 0, 0))],
        out_specs=pl.BlockSpec((1, tq, MLA_WIDTH), lambda b, i: (b, i, 0)),
        out_shape=jax.ShapeDtypeStruct((B, Lq, MLA_WIDTH), jnp.float32),
        compiler_params=pltpu.CompilerParams(dimension_semantics=("parallel", "parallel"),
                                             vmem_limit_bytes=VMEM_LIMIT_BYTES),
        name="mla_attention",
    )(q, k, v)


def mla_mixer(p_lat, p_ctx, q_norm, w_uq, kv_norm, w_ukv, q_gain, k_gain, need_ctx):
    prm = (q_norm, w_uq, kv_norm, w_ukv, q_gain, k_gain)
    q_l, k_l, v_l = mla_prep(p_lat, True, *prm)
    q_c, k_c, v_c = mla_prep(p_ctx, False, *prm)
    y_l = attention(q_l, jnp.concatenate([k_l, k_c], axis=1), jnp.concatenate([v_l, v_c], axis=1))
    y_c = attention(q_c, k_c, v_c) if need_ctx else None
    return y_l, y_c


HY_FILTER_ROWS = 256


def _hyena_filter_kernel(z_ref, tn_ref, w1_ref, b1_ref, f1_ref, w2_ref, b2_ref, f2_ref, w3_ref, b3_ref, rates_ref, o_ref):
    hp = lax.Precision.HIGHEST
    h = jnp.sin(f1_ref[...] * (jnp.dot(z_ref[...], w1_ref[...], precision=hp, preferred_element_type=jnp.float32) + b1_ref[...]))
    h = jnp.sin(f2_ref[...] * (jnp.dot(h, w2_ref[...], precision=hp, preferred_element_type=jnp.float32) + b2_ref[...]))
    h = jnp.dot(h, w3_ref[...], precision=hp, preferred_element_type=jnp.float32) + b3_ref[...]
    o_ref[...] = h * jnp.exp(-tn_ref[...] * rates_ref[...])


def hyena_filters(L, w1, b1, freq1, w2, b2, freq2, w3, b3):
    tn = jnp.arange(L, dtype=jnp.float32) / L
    bands = jnp.arange(1, HY_POS_BANDS + 1, dtype=jnp.float32)
    ang = 2.0 * math.pi * tn[:, None] * bands[None, :]
    z = jnp.concatenate([tn[:, None], jnp.cos(ang), jnp.sin(ang)], axis=-1)
    rates = jnp.abs(jnp.linspace(math.log(HY_DECAY_TARGET) / HY_LONG_DECAY_PCT,
                                 math.log(HY_DECAY_TARGET) / HY_SHORT_DECAY_PCT, HY_WIDTH))
    pad = -z.shape[1] % 8
    T = min(HY_FILTER_ROWS, L)
    n_out = w3.shape[1]
    row = lambda a: a.reshape(1, -1)
    const = lambda a: pl.BlockSpec(a.shape, lambda i: (0,) * a.ndim)
    args = (jnp.pad(w1, ((0, pad), (0, 0))), row(b1), row(freq1), w2, row(b2), row(freq2), w3, row(b3),
            row(jnp.tile(rates, n_out // HY_WIDTH)))
    h = pl.pallas_call(
        _hyena_filter_kernel,
        grid=(L // T,),
        in_specs=[pl.BlockSpec((T, z.shape[1] + pad), lambda i: (i, 0)), pl.BlockSpec((T, 1), lambda i: (i, 0))]
                 + [const(a) for a in args],
        out_specs=pl.BlockSpec((T, n_out), lambda i: (i, 0)),
        out_shape=jax.ShapeDtypeStruct((L, n_out), jnp.float32),
        compiler_params=pltpu.CompilerParams(dimension_semantics=("parallel",)),
        name="hyena_filter_mlp",
    )(jnp.pad(z, ((0, 0), (0, pad))), tn[:, None], *args)
    h = h.reshape(L, HY_ORDER, 2, HY_WIDTH)
    zero = jnp.zeros((1, HY_ORDER, HY_WIDTH), h.dtype)
    h_full = jnp.concatenate([h[:, :, 0], zero, h[:0:-1, :, 1]], axis=0)
    return h_full * lax.rsqrt(jnp.sum(jnp.square(h_full), axis=0, keepdims=True))


def fft_long_conv(u, h_full, bias):
    L = u.shape[1]
    uf = jnp.fft.rfft(u, n=2 * L, axis=1)
    hf = jnp.fft.rfft(h_full, n=2 * L, axis=0)
    y = jnp.fft.irfft(uf * hf[None], n=2 * L, axis=1)[:, :L]
    return y + u * bias


FFT_N1 = 64
FFT_N2 = 128
FFT_N = FFT_N1 * FFT_N2
HY_SEQS = 32


def _dft_tables(seqs):
    n1 = np.arange(FFT_N1)
    n2 = np.arange(FFT_N2)
    f64 = np.exp(-2j * np.pi * np.outer(n1, n1) / FFT_N1)
    f128 = np.exp(-2j * np.pi * np.outer(n2, n2) / FFT_N2)
    tw = np.exp(-2j * np.pi * np.outer(n1, n2) / FFT_N)
    half = FFT_N1 // 2
    fh = f64[:, :half]
    m1 = np.block([[fh.real, -fh.imag], [fh.imag, fh.real]])
    m1f = np.concatenate([f64.real, f64.imag], axis=0)
    m2 = np.block([[f128.real, f128.imag], [-f128.imag, f128.real]])
    m3 = np.block([[f128.real, -f128.imag], [f128.imag, f128.real]]) / FFT_N
    c = np.conj(f64)[:half, :]
    m4 = np.block([[c.real, -c.imag], [c.imag, c.real]])
    bf = lambda a: jnp.asarray(a, jnp.float32).astype(jnp.bfloat16)
    f32 = lambda a: jnp.asarray(a, jnp.float32)
    return dict(m1=bf(m1), m1f=bf(m1f), m2=bf(m2), m3=bf(m3), m4=bf(m4),
                twr_l=f32(np.tile(tw.real, (1, seqs))), twi_l=f32(np.tile(tw.imag, (1, seqs))),
                twr_s=f32(np.tile(tw.real, (seqs, 1))), twi_s=f32(np.tile(tw.imag, (seqs, 1))))


def _spectrum(cols, m1, twr_l, twi_l, m2, R):
    a = jnp.dot(m1, cols.astype(jnp.bfloat16), preferred_element_type=jnp.float32)
    ar, ai = a[:FFT_N1], a[FFT_N1:]
    pr = ar * twr_l - ai * twi_l
    pi = ar * twi_l + ai * twr_l
    lhs = jnp.concatenate(
        [jnp.concatenate([pr[:, r * FFT_N2:(r + 1) * FFT_N2], pi[:, r * FFT_N2:(r + 1) * FFT_N2]], axis=1)
         for r in range(R)], axis=0)
    return jnp.dot(lhs.astype(jnp.bfloat16), m2, preferred_element_type=jnp.float32)


def _filter_fft_kernel(h_ref, m1f_ref, twr_ref, twi_ref, m2_ref, o_ref):
    R = HY_SEQS
    cols = jnp.concatenate([h_ref[r] for r in range(R)], axis=1)
    x = _spectrum(cols, m1f_ref[...], twr_ref[...], twi_ref[...], m2_ref[...], R)
    o_ref[...] = x.reshape(R, FFT_N1, 2 * FFT_N2)


def _hyena_conv_kernel(y_ref, g_ref, hf_ref, bias_ref, m1_ref, twr_l_ref, twi_l_ref, m2_ref, m3_ref,
                       twr_s_ref, twi_s_ref, m4_ref, o_ref):
    R = HY_SEQS
    half = FFT_N1 // 2
    y = [y_ref[0], y_ref[1]]
    for o in range(HY_ORDER):
        top = jnp.concatenate([y[0][r] for r in range(R)], axis=1)
        bot = jnp.concatenate([y[1][r] for r in range(R)], axis=1)
        x = _spectrum(jnp.concatenate([top, bot], axis=0), m1_ref[...], twr_l_ref[...], twi_l_ref[...], m2_ref[...], R)
        hf = hf_ref[o].reshape(R * FFT_N1, 2 * FFT_N2)
        xr, xi = x[:, :FFT_N2], x[:, FFT_N2:]
        hr, hi = hf[:, :FFT_N2], hf[:, FFT_N2:]
        yc = jnp.concatenate([xr * hr - xi * hi, xr * hi + xi * hr], axis=1)
        b = jnp.dot(yc.astype(jnp.bfloat16), m3_ref[...], preferred_element_type=jnp.float32)
        br, bi = b[:, :FFT_N2], b[:, FFT_N2:]
        qr = br * twr_s_ref[...] + bi * twi_s_ref[...]
        qi = bi * twr_s_ref[...] - br * twi_s_ref[...]
        bc = jnp.concatenate(
            [jnp.concatenate([qr[r * FFT_N1:(r + 1) * FFT_N1], qi[r * FFT_N1:(r + 1) * FFT_N1]], axis=0)
             for r in range(R)], axis=1)
        yo = jnp.dot(m4_ref[...], bc.astype(jnp.bfloat16), preferred_element_type=jnp.float32)
        for p in range(2):
            conv = jnp.stack([yo[p * half:(p + 1) * half, r * FFT_N2:(r + 1) * FFT_N2] for r in range(R)], axis=0)
            y[p] = g_ref[o, p] * (conv + y[p] * bias_ref[o])
    o_ref[0] = y[0]
    o_ref[1] = y[1]


def hyena_long_conv(y_t, g_t, h_t, bias):
    B, C, L = y_t.shape
    assert 2 * L == FFT_N and B % 2 == 0 and C % HY_SEQS == 0
    R = HY_SEQS
    half = FFT_N1 // 2
    tb = _dft_tables(R)
    const = lambda a: pl.BlockSpec(a.shape, lambda *_: (0,) * a.ndim)
    hf = pl.pallas_call(
        _filter_fft_kernel,
        grid=(HY_ORDER * C // R,),
        in_specs=[pl.BlockSpec((R, FFT_N1, FFT_N2), lambda i: (i, 0, 0)),
                  const(tb['m1f']), const(tb['twr_l']), const(tb['twi_l']), const(tb['m2'])],
        out_specs=pl.BlockSpec((R, FFT_N1, 2 * FFT_N2), lambda i: (i, 0, 0)),
        out_shape=jax.ShapeDtypeStruct((HY_ORDER * C, FFT_N1, 2 * FFT_N2), jnp.float32),
        compiler_params=pltpu.CompilerParams(dimension_semantics=("parallel",), vmem_limit_bytes=VMEM_LIMIT_BYTES),
        name="hyena_filter_fft",
    )(h_t.reshape(HY_ORDER * C, FFT_N1, FFT_N2), tb['m1f'], tb['twr_l'], tb['twi_l'], tb['m2'])
    hf = hf.reshape(HY_ORDER, C, FFT_N1, 2 * FFT_N2)
    out = pl.pallas_call(
        _hyena_conv_kernel,
        grid=(B // 2, C // R),
        in_specs=[pl.BlockSpec((2, R, half, FFT_N2), lambda b, c: (b, c, 0, 0)),
                  pl.BlockSpec((HY_ORDER, 2, R, half, FFT_N2), lambda b, c: (0, b, c, 0, 0)),
                  pl.BlockSpec((HY_ORDER, R, FFT_N1, 2 * FFT_N2), lambda b, c: (0, c, 0, 0)),
                  pl.BlockSpec((HY_ORDER, R, 1, 1), lambda b, c: (0, c, 0, 0)),
                  const(tb['m1']), const(tb['twr_l']), const(tb['twi_l']), const(tb['m2']), const(tb['m3']),
                  const(tb['twr_s']), const(tb['twi_s']), const(tb['m4'])],
        out_specs=pl.BlockSpec((2, R, half, FFT_N2), lambda b, c: (b, c, 0, 0)),
        out_shape=jax.ShapeDtypeStruct((B, C, half, FFT_N2), jnp.float32),
        compiler_params=pltpu.CompilerParams(dimension_semantics=("parallel", "parallel"),
                                             vmem_limit_bytes=VMEM_LIMIT_BYTES),
        name="hyena_conv",
    )(y_t.reshape(B, C, half, FFT_N2), g_t.reshape(HY_ORDER, B, C, half, FFT_N2), hf,
      bias.reshape(HY_ORDER, C, 1, 1), tb['m1'], tb['twr_l'], tb['twi_l'], tb['m2'], tb['m3'],
      tb['twr_s'], tb['twi_s'], tb['m4'])
    return out.reshape(B, C, L)


HY_TRANSPOSE_ROWS = 512


def _hyena_to_channels_kernel(z_ref, y_ref, g_ref):
    z = z_ref[0]
    for o in range(HY_ORDER):
        g_ref[o, 0] = z[:, o * HY_WIDTH:(o + 1) * HY_WIDTH].T
    y_ref[0] = z[:, HY_ORDER * HY_WIDTH:].T


def hyena_to_channels(z):
    B, L, _ = z.shape
    T = HY_TRANSPOSE_ROWS
    return pl.pallas_call(
        _hyena_to_channels_kernel,
        grid=(B, L // T),
        in_specs=[pl.BlockSpec((1, T, HY_PROJ), lambda b, i: (b, i, 0))],
        out_specs=[pl.BlockSpec((1, HY_WIDTH, T), lambda b, i: (b, 0, i)),
                   pl.BlockSpec((HY_ORDER, 1, HY_WIDTH, T), lambda b, i: (0, b, 0, i))],
        out_shape=[jax.ShapeDtypeStruct((B, HY_WIDTH, L), jnp.float32),
                   jax.ShapeDtypeStruct((HY_ORDER, B, HY_WIDTH, L), jnp.float32)],
        compiler_params=pltpu.CompilerParams(dimension_semantics=("parallel", "parallel")),
        name="hyena_to_channels",
    )(z)


def _hyena_to_tokens_kernel(y_ref, o_ref):
    o_ref[0] = y_ref[0].T


def hyena_to_tokens(y_t):
    B, C, L = y_t.shape
    T = HY_TRANSPOSE_ROWS
    return pl.pallas_call(
        _hyena_to_tokens_kernel,
        grid=(B, L // T),
        in_specs=[pl.BlockSpec((1, C, T), lambda b, i: (b, 0, i))],
        out_specs=pl.BlockSpec((1, T, C), lambda b, i: (b, i, 0)),
        out_shape=jax.ShapeDtypeStruct((B, L, C), jnp.float32),
        compiler_params=pltpu.CompilerParams(dimension_semantics=("parallel", "parallel")),
        name="hyena_to_tokens",
    )(y_t)


def hyena_mixer(p, conv_w, w1, b1, freq1, w2, b2, freq2, w3, b3, bias):
    B, L = p.shape[:2]
    z = short_conv(p, conv_w)
    h_full = hyena_filters(L, w1, b1, freq1, w2, b2, freq2, w3, b3)
    if 2 * L == FFT_N:
        y_t, g_t = hyena_to_channels(z)
        y_t = hyena_long_conv(y_t, g_t, jnp.transpose(h_full, (1, 2, 0)), bias)
        return hyena_to_tokens(y_t)
    gates = (z[..., :HY_WIDTH], z[..., HY_WIDTH:2 * HY_WIDTH])
    y = z[..., 2 * HY_WIDTH:]
    for o in range(HY_ORDER):
        y = gates[o] * fft_long_conv(y, h_full[:, o], bias[o])
    return y


SC_CORES = 2
SC_SUBCORES = 16
SC_LANES = 16
SC_WORKERS = SC_CORES * SC_SUBCORES
PEER_SLOTS = PEER_HEADS * PEER_TOPK
PEER_GATHER_ROWS = 32
PEER_GATHERS = PEER_SLOTS // PEER_GATHER_ROWS
PEER_ACC_VREGS = 8
PEER_ROW_BUFFERS = 4
PEER_ROW_WORDS = D_MODEL // 2
HI_MASK = -65536


PACK_ROWS = 512


def _pack_table_kernel(t_ref, o_ref):
    bits = lambda x: lax.bitcast_convert_type(x.astype(jnp.bfloat16).astype(jnp.float32), jnp.int32)
    lo = lax.shift_right_logical(bits(t_ref[:, :PEER_ROW_WORDS]), 16)
    o_ref[...] = bits(t_ref[:, PEER_ROW_WORDS:]) | lo


def pack_expert_table(t):
    E = t.shape[0]
    return pl.pallas_call(
        _pack_table_kernel,
        grid=(E // PACK_ROWS,),
        in_specs=[pl.BlockSpec((PACK_ROWS, D_MODEL), lambda i: (i, 0))],
        out_specs=pl.BlockSpec((PACK_ROWS, PEER_ROW_WORDS), lambda i: (i, 0)),
        out_shape=jax.ShapeDtypeStruct((E, PEER_ROW_WORDS), jnp.int32),
        compiler_params=pltpu.CompilerParams(dimension_semantics=("parallel",)),
        name="pack_expert_table",
    )(t)


def _sc_peer_phase(phase, tpw):
    NBUF = PEER_ROW_BUFFERS
    AHEAD = NBUF - 1
    HW = PEER_ROW_WORDS

    def run(base, table_hbm, idx_hbm, aux_hbm, out_hbm, idx_v, aux_v, rows_v, out_v, sem_r, sem_i, sem_o):

        def gather(p, c, b):
            return pltpu.make_async_copy(table_hbm.at[idx_v.at[p, c]], rows_v.at[b], sem_r.at[b])

        def load_meta(t, p):
            return (pltpu.make_async_copy(idx_hbm.at[t], idx_v.at[p], sem_i.at[p]),
                    pltpu.make_async_copy(aux_hbm.at[t], aux_v.at[p], sem_i.at[p]))

        def store_out(t, p):
            return pltpu.make_async_copy(out_v.at[p], out_hbm.at[t], sem_o.at[p])

        def halves(word):
            return (plsc.bitcast(lax.shift_left(word, 16), jnp.float32), plsc.bitcast(word & HI_MASK, jnp.float32))

        def compute(p, c, b):
            if phase == "dot":
                lane = lax.iota(jnp.int32, SC_LANES)

                @pl.loop(0, PEER_GATHER_ROWS // SC_LANES)
                def _(g2):
                    vec = jnp.zeros((SC_LANES,), jnp.float32)
                    for gg in range(SC_LANES // PEER_ACC_VREGS):
                        row0 = g2 * SC_LANES + gg * PEER_ACC_VREGS

                        def body(cc, accs):
                            x_lo = aux_v[p, pl.ds(cc * SC_LANES, SC_LANES)]
                            x_hi = aux_v[p, pl.ds(HW + cc * SC_LANES, SC_LANES)]
                            out = []
                            for r in range(PEER_ACC_VREGS):
                                lo, hi = halves(rows_v[b, row0 + r, pl.ds(cc * SC_LANES, SC_LANES)])
                                out.append(accs[r] + lo * x_lo + hi * x_hi)
                            return tuple(out)
                        accs = lax.fori_loop(0, HW // SC_LANES, body,
                                             tuple(jnp.zeros((SC_LANES,), jnp.float32) for _ in range(PEER_ACC_VREGS)))
                        for r in range(PEER_ACC_VREGS):
                            vec = jnp.where(lane == gg * PEER_ACC_VREGS + r, jnp.sum(accs[r]), vec)
                    out_v[p, pl.ds(c * PEER_GATHER_ROWS + g2 * SC_LANES, SC_LANES)] = vec
            else:
                words = PEER_ACC_VREGS // 2

                @pl.loop(0, HW // (words * SC_LANES))
                def _(db):
                    def body(kk, accs):
                        wv = plsc.load_gather(aux_v.at[p], [jnp.full((SC_LANES,), c * PEER_GATHER_ROWS + kk, jnp.int32)])
                        out = []
                        for j in range(words):
                            lo, hi = halves(rows_v[b, kk, pl.ds((db * words + j) * SC_LANES, SC_LANES)])
                            out += [accs[2 * j] + lo * wv, accs[2 * j + 1] + hi * wv]
                        return tuple(out)
                    if c == 0:
                        init = tuple(jnp.zeros((SC_LANES,), jnp.float32) for _ in range(2 * words))
                    else:
                        init = tuple(out_v[p, pl.ds(half * HW + (db * words + j) * SC_LANES, SC_LANES)]
                                     for j in range(words) for half in range(2))
                    accs = lax.fori_loop(0, PEER_GATHER_ROWS, body, init)
                    for j in range(words):
                        out_v[p, pl.ds((db * words + j) * SC_LANES, SC_LANES)] = accs[2 * j]
                        out_v[p, pl.ds(HW + (db * words + j) * SC_LANES, SC_LANES)] = accs[2 * j + 1]

        for d in load_meta(base, 0):
            d.start()
        for d in load_meta(base, 0):
            d.wait()
        for c in range(AHEAD):
            gather(0, c, c % NBUF).start()

        @pl.loop(0, tpw // 2)
        def _(i2):
            for p in range(2):
                i = i2 * 2 + p
                t = base + i
                nxt = base + jnp.minimum(i + 1, tpw - 1)
                for d in load_meta(nxt, 1 - p):
                    d.start()

                @pl.when(i2 > 0)
                def _():
                    store_out(t, p).wait()

                for c in range(PEER_GATHERS):
                    ahead = c + AHEAD
                    if ahead < PEER_GATHERS:
                        gather(p, ahead, ahead % NBUF).start()
                    else:
                        if ahead == PEER_GATHERS:
                            for d in load_meta(nxt, 1 - p):
                                d.wait()
                        gather(1 - p, ahead - PEER_GATHERS, ahead % NBUF).start()
                    gather(p, c, c % NBUF).wait()
                    compute(p, c, c % NBUF)
                store_out(t, p).start()

        for c in range(AHEAD):
            gather(0, c, c % NBUF).wait()
        for p in range(2):
            store_out(base, p).wait()

    return run


def _sc_tokens_per_worker(N):
    assert N % (2 * SC_WORKERS) == 0 and PEER_GATHERS % PEER_ROW_BUFFERS == 0
    return N // SC_WORKERS


_SC_AUX = {"dot": (D_MODEL,), "wsum": (PEER_SLOTS,)}
_SC_OUT = {"dot": (PEER_SLOTS,), "wsum": (D_MODEL,)}


def _sc_scratch(phases):
    s = [pltpu.VMEM((2, PEER_GATHERS, PEER_GATHER_ROWS), jnp.int32),
         pltpu.VMEM((PEER_ROW_BUFFERS, PEER_GATHER_ROWS, PEER_ROW_WORDS), jnp.int32),
         pltpu.SemaphoreType.DMA((PEER_ROW_BUFFERS,)), pltpu.SemaphoreType.DMA((2,)), pltpu.SemaphoreType.DMA((2,))]
    for ph in phases:
        s += [pltpu.VMEM((2,) + _SC_AUX[ph], jnp.float32), pltpu.VMEM((2,) + _SC_OUT[ph], jnp.float32)]
    return s


def _sc_peer_jobs(jobs):
    phases = [j[0] for j in jobs]
    kinds = sorted(set(phases))
    ns = [j[2].shape[0] for j in jobs]
    runs = [_sc_peer_phase(ph, _sc_tokens_per_worker(n)) for ph, n in zip(phases, ns)]
    nj = len(jobs)

    @functools.partial(
        pl.kernel, mesh=plsc.VectorSubcoreMesh(core_axis_name="c", subcore_axis_name="s"),
        out_type=tuple(jax.ShapeDtypeStruct((n,) + _SC_OUT[ph], jnp.float32) for ph, n in zip(phases, ns)),
        compiler_params=pltpu.CompilerParams(needs_layout_passes=False),
        scratch_types=_sc_scratch(kinds),
    )
    def k(*refs):
        ins, outs, scratch = refs[:3 * nj], refs[3 * nj:4 * nj], refs[4 * nj:]
        idx_v, rows_v, sem_r, sem_i, sem_o = scratch[:5]
        bufs = {kind: scratch[5 + 2 * i:7 + 2 * i] for i, kind in enumerate(kinds)}
        worker = lax.axis_index("s") * SC_CORES + lax.axis_index("c")
        for j in range(nj):
            table_hbm, idx_hbm, aux_hbm = ins[3 * j:3 * j + 3]
            aux_v, out_v = bufs[phases[j]]
            runs[j](worker * (ns[j] // SC_WORKERS), table_hbm, idx_hbm, aux_hbm, outs[j],
                    idx_v, aux_v, rows_v, out_v, sem_r, sem_i, sem_o)

    args = []
    for (_, table, idx, aux), n in zip(jobs, ns):
        args += [table, idx.reshape(n, PEER_GATHERS, PEER_GATHER_ROWS), aux]
    return list(k(*args))


PEER_TOKENS = 256
INT_BIG = 2 ** 30
PEER_CANDIDATES = -(-sum(PEER_TOPK // (i + 1) for i in range(PEER_TOPK)) // 8) * 8


def _extract_topk(cand_ref, ids_ref, val_out_ref, id_out_ref, row0):
    def body(r, carry):
        c = cand_ref[...]
        ids = ids_ref[...]
        m = jnp.max(c, axis=0, keepdims=True)
        sel = jnp.min(jnp.where(c == m, ids, INT_BIG), axis=0, keepdims=True)
        cand_ref[...] = jnp.where(ids == sel, -jnp.inf, c)
        val_out_ref[pl.ds(row0 + r, 1), :] = m
        id_out_ref[pl.ds(row0 + r, 1), :] = sel
        return carry
    lax.fori_loop(0, PEER_TOPK, body, 0)


def _peer_retrieve_kernel(x_ref, gain_ref, scale_ref, shift_ref, wq_ref, keys_ref,
                          h_ref, idx_out_ref, gate_out_ref,
                          s_ref, ids1_ref, sv_ref, si_ref, cand_ref, cid_ref, ts_ref, idx_ref, gate_ref):
    x = x_ref[0]
    y = x * lax.rsqrt(jnp.mean(x * x, axis=-1, keepdims=True) + NORM_EPS)
    h = (y * gain_ref[...]) * (1.0 + scale_ref[0]) + shift_ref[0]
    h_ref[0] = h
    q = jnp.dot(h.astype(jnp.bfloat16), wq_ref[...], preferred_element_type=jnp.float32)
    T = PEER_TOKENS
    K = PEER_TOPK
    ids1_ref[...] = lax.broadcasted_iota(jnp.int32, (PEER_N_KEYS, T), 0)
    for hd in range(PEER_HEADS):
        for p in range(2):
            hp = hd * 2 + p
            qs = q[:, hp * PEER_HALF:(hp + 1) * PEER_HALF].astype(jnp.bfloat16)
            s_ref[...] = lax.dot_general(keys_ref[hp], qs, (((1,), (1,)), ((), ())),
                                         preferred_element_type=jnp.float32)
            _extract_topk(s_ref, ids1_ref, sv_ref, si_ref, p * K)
        cand_ref[...] = jnp.full(cand_ref.shape, -jnp.inf, jnp.float32)
        cid_ref[...] = INT_BIG - 1 - lax.broadcasted_iota(jnp.int32, cid_ref.shape, 0)
        off = 0
        for i in range(K):
            n = K // (i + 1)
            cand_ref[off:off + n, :] = sv_ref[i:i + 1, :] + sv_ref[K:K + n, :]
            cid_ref[off:off + n, :] = si_ref[i:i + 1, :] * PEER_N_KEYS + si_ref[K:K + n, :]
            off += n
        _extract_topk(cand_ref, cid_ref, ts_ref, idx_ref, hd * K)
        ts = ts_ref[hd * K:(hd + 1) * K, :]
        e = jnp.exp(ts - jnp.max(ts, axis=0, keepdims=True))
        gate_ref[hd * K:(hd + 1) * K, :] = e / jnp.sum(e, axis=0, keepdims=True)
    idx_out_ref[...] = idx_ref[...].T
    gate_out_ref[...] = gate_ref[...].T


def peer_retrieve(x, gain, scale, shift, w_q, sub_keys):
    B, L, D = x.shape
    T = PEER_TOKENS
    nt = L // T
    keys = sub_keys.reshape(PEER_HEADS * 2, PEER_N_KEYS, PEER_HALF).astype(jnp.bfloat16)
    return pl.pallas_call(
        _peer_retrieve_kernel,
        grid=(B, nt),
        in_specs=[
            pl.BlockSpec((1, T, D), lambda b, i: (b, i, 0)),
            pl.BlockSpec((1, D), lambda b, i: (0, 0)),
            pl.BlockSpec((1, 1, D), lambda b, i: (b, 0, 0)),
            pl.BlockSpec((1, 1, D), lambda b, i: (b, 0, 0)),
            pl.BlockSpec((D, PEER_HEADS * 2 * PEER_HALF), lambda b, i: (0, 0)),
            pl.BlockSpec((PEER_HEADS * 2, PEER_N_KEYS, PEER_HALF), lambda b, i: (0, 0, 0)),
        ],
        out_specs=[
            pl.BlockSpec((1, T, D), lambda b, i: (b, i, 0)),
            pl.BlockSpec((T, PEER_SLOTS), lambda b, i: (b * nt + i, 0)),
            pl.BlockSpec((T, PEER_SLOTS), lambda b, i: (b * nt + i, 0)),
        ],
        out_shape=[
            jax.ShapeDtypeStruct((B, L, D), jnp.float32),
            jax.ShapeDtypeStruct((B * L, PEER_SLOTS), jnp.int32),
            jax.ShapeDtypeStruct((B * L, PEER_SLOTS), jnp.float32),
        ],
        scratch_shapes=[
            pltpu.VMEM((PEER_N_KEYS, T), jnp.float32),
            pltpu.VMEM((PEER_N_KEYS, T), jnp.int32),
            pltpu.VMEM((2 * PEER_TOPK, T), jnp.float32),
            pltpu.VMEM((2 * PEER_TOPK, T), jnp.int32),
            pltpu.VMEM((PEER_CANDIDATES, T), jnp.float32),
            pltpu.VMEM((PEER_CANDIDATES, T), jnp.int32),
            pltpu.VMEM((PEER_SLOTS, T), jnp.float32),
            pltpu.VMEM((PEER_SLOTS, T), jnp.int32),
            pltpu.VMEM((PEER_SLOTS, T), jnp.float32),
        ],
        compiler_params=pltpu.CompilerParams(dimension_semantics=("parallel", "parallel"),
                                             vmem_limit_bytes=VMEM_LIMIT_BYTES),
        name="peer_retrieve",
    )(x, gain.reshape(1, D), scale, shift, w_q.astype(jnp.bfloat16), keys)


PEER_ACT_ROWS = 256


def _peer_act_kernel(dots_ref, gate_ref, w_ref):
    a = dots_ref[...]
    w_ref[...] = gate_ref[...] * (0.5 * a * (1.0 + lax.erf(a * (2.0 ** -0.5))))


def peer_act(dots, gate):
    N = dots.shape[0]
    T = min(PEER_ACT_ROWS, N)
    spec = pl.BlockSpec((T, PEER_SLOTS), lambda i: (i, 0))
    return pl.pallas_call(
        _peer_act_kernel,
        grid=(N // T,),
        in_specs=[spec, spec],
        out_specs=spec,
        out_shape=jax.ShapeDtypeStruct((N, PEER_SLOTS), jnp.float32),
        compiler_params=pltpu.CompilerParams(dimension_semantics=("parallel",)),
        name="peer_act",
    )(dots, gate)


OUT_PROJ_ROWS = 512


def _out_proj_kernel(x_ref, g_ref, rw_ref, ml_ref, hy_ref, w_ref, o_ref):
    bf = jnp.bfloat16
    y = jnp.dot(rw_ref[0].astype(bf), w_ref[:RW_WIDTH, :], preferred_element_type=jnp.float32)
    y += jnp.dot(ml_ref[0].astype(bf), w_ref[RW_WIDTH:RW_WIDTH + MLA_WIDTH, :], preferred_element_type=jnp.float32)
    y += jnp.dot(hy_ref[0].astype(bf), w_ref[RW_WIDTH + MLA_WIDTH:, :], preferred_element_type=jnp.float32)
    o_ref[0] = x_ref[0] + g_ref[0] * y


def mix_out_proj(x, gate, rw, ml, hy, w_out):
    B, L, D = x.shape
    T = min(OUT_PROJ_ROWS, L)
    tok = lambda w: pl.BlockSpec((1, T, w), lambda b, i: (b, i, 0))
    return pl.pallas_call(
        _out_proj_kernel,
        grid=(B, L // T),
        in_specs=[tok(D), pl.BlockSpec((1, 1, D), lambda b, i: (b, 0, 0)), tok(RW_WIDTH), tok(MLA_WIDTH), tok(HY_WIDTH),
                  pl.BlockSpec((MIX_WIDTH, D), lambda b, i: (0, 0))],
        out_specs=tok(D),
        out_shape=jax.ShapeDtypeStruct((B, L, D), jnp.float32),
        compiler_params=pltpu.CompilerParams(dimension_semantics=("parallel", "parallel"),
                                             vmem_limit_bytes=VMEM_LIMIT_BYTES),
        name="mix_out_proj",
    )(x, gate, rw, ml, hy, w_out.astype(jnp.bfloat16))


def _mix_and_retrieve(li, x, c, ctx, c_ctx, mod_w, mod_b, mix_norm, w_in, w_out, rw_conv, rw_decay_up, rw_decay0, rw_a_up, rw_a0, rw_gate_up, rw_k_k, rw_k_a, rw_r_k, rw_gn_g, rw_gn_b, mla_q_norm, mla_w_uq, mla_kv_norm, mla_w_ukv, mla_q_gain, mla_k_gain, hy_conv, hy_w1, hy_b1, hy_freq1, hy_w2, hy_b2, hy_freq2, hy_w3, hy_b3, hy_bias, ffn_norm, peer_wq, peer_keys, peer_u, peer_v):
    B, L, D = x.shape
    need_ctx = li < DEPTH - 1
    mod_l = (jax.nn.silu(c) @ mod_w[li] + mod_b[li])[:, None, :]
    mod_c = (jax.nn.silu(c_ctx) @ mod_w[li] + mod_b[li])[None, None, :]
    shm_l, scm_l, gm_l, shf_l, scf_l, gf_l = jnp.split(mod_l, N_MOD, axis=-1)
    shm_c, scm_c, gm_c, shf_c, scf_c, gf_c = jnp.split(mod_c, N_MOD, axis=-1)

    widths = (RW_PROJ, MLA_PROJ, HY_PROJ)
    prw_l, pml_l, phy_l = norm_mod_proj(x, mix_norm[li], scm_l, shm_l, w_in[li], widths, 512)
    prw_c, pml_c, phy_c = norm_mod_proj(ctx, mix_norm[li], jnp.broadcast_to(scm_c, (B, 1, D)),
                                        jnp.broadcast_to(shm_c, (B, 1, D)), w_in[li], widths, 256)
    rw_l, rw_c = rwkv7_mixer(prw_l, prw_c, rw_conv[li], rw_decay_up[li], rw_decay0[li],
                             rw_a_up[li], rw_a0[li], rw_gate_up[li], rw_k_k[li], rw_k_a[li], rw_r_k[li],
                             rw_gn_g[li], rw_gn_b[li], need_ctx)
    ml_l, ml_c = mla_mixer(pml_l, pml_c, mla_q_norm[li], mla_w_uq[li],
                           mla_kv_norm[li], mla_w_ukv[li], mla_q_gain[li], mla_k_gain[li], need_ctx)
    hy_prm = (hy_conv[li], hy_w1[li], hy_b1[li], hy_freq1[li], hy_w2[li], hy_b2[li], hy_freq2[li],
              hy_w3[li], hy_b3[li], hy_bias[li])
    hy_l = hyena_mixer(phy_l, *hy_prm)
    x = mix_out_proj(x, gm_l, rw_l, ml_l, hy_l, w_out[li])
    h, e_idx, gate = peer_retrieve(x, ffn_norm[li], scf_l, shf_l, peer_wq[li], peer_keys[li])
    streams = [(x, gf_l, h.reshape(B * L, D), e_idx, gate)]
    if need_ctx:
        hy_c = hyena_mixer(phy_c, *hy_prm)
        ctx = mix_out_proj(ctx, jnp.broadcast_to(gm_c, (B, 1, D)), rw_c, ml_c, hy_c, w_out[li])
        h, e_idx, gate = peer_retrieve(ctx, ffn_norm[li], jnp.broadcast_to(scf_c, (B, 1, D)),
                                       jnp.broadcast_to(shf_c, (B, 1, D)), peer_wq[li], peer_keys[li])
        streams.append((ctx, gf_c, h.reshape(-1, D), e_idx, gate))
    return streams


BATCH_GROUP_ROWS = (2, 2, 2, 2)


def kernel(x, c, ctx, c_ctx, mod_w, mod_b, mix_norm, w_in, w_out, rw_conv, rw_decay_up, rw_decay0, rw_a_up, rw_a0, rw_gate_up, rw_k_k, rw_k_a, rw_r_k, rw_gn_g, rw_gn_b, mla_q_norm, mla_w_uq, mla_kv_norm, mla_w_ukv, mla_q_gain, mla_k_gain, hy_conv, hy_w1, hy_b1, hy_freq1, hy_w2, hy_b2, hy_freq2, hy_w3, hy_b3, hy_bias, ffn_norm, peer_wq, peer_keys, peer_u, peer_v):
    params = (mod_w, mod_b, mix_norm, w_in, w_out, rw_conv, rw_decay_up, rw_decay0, rw_a_up, rw_a0, rw_gate_up,
              rw_k_k, rw_k_a, rw_r_k, rw_gn_g, rw_gn_b, mla_q_norm, mla_w_uq, mla_kv_norm, mla_w_ukv, mla_q_gain,
              mla_k_gain, hy_conv, hy_w1, hy_b1, hy_freq1, hy_w2, hy_b2, hy_freq2, hy_w3, hy_b3, hy_bias,
              ffn_norm, peer_wq, peer_keys)
    peer_u = [pack_expert_table(peer_u[li]) for li in range(DEPTH)]
    peer_v = [pack_expert_table(peer_v[li]) for li in range(DEPTH)]
    params = params + (peer_u, peer_v)
    assert sum(BATCH_GROUP_ROWS) == x.shape[0]
    G = len(BATCH_GROUP_ROWS)
    lo = [sum(BATCH_GROUP_ROWS[:g]) for g in range(G + 1)]
    L, D = x.shape[1:]
    xs = [x[lo[g]:lo[g + 1]] for g in range(G)]
    cs = [c[lo[g]:lo[g + 1]] for g in range(G)]
    ctxs = [ctx[lo[g]:lo[g + 1]] for g in range(G)]
    stages = [(li, g) for li in range(DEPTH) for g in range(G)]
    prev = None
    token = None

    def advance(prev, token, li, streams):
        pstreams, pdots = ([], []) if prev is None else (prev[2], prev[3])
        if pdots:
            token, pdots = lax.optimization_barrier((token, pdots))
        dots = []
        for s in range(max(len(pstreams), len(streams))):
            jobs = []
            if s < len(pstreams):
                w = peer_act(pdots[s], pstreams[s][4])
                token = w if s == 0 else token
                jobs.append(("wsum", peer_v[prev[0]], pstreams[s][3], w))
            if s < len(streams):
                jobs.append(("dot", peer_u[li], streams[s][3], streams[s][2]))
            outs = _sc_peer_jobs(jobs)
            if s < len(pstreams):
                res = pstreams[s][0] + pstreams[s][1] * outs[0].reshape(pstreams[s][0].shape)
                (xs if s == 0 else ctxs)[prev[1]] = res
            if s < len(streams):
                dots.append(outs[-1])
        return token, dots

    for li, g in stages:
        ins = (xs[g], ctxs[g])
        if token is not None:
            token, ins = lax.optimization_barrier((token, ins))
        streams = _mix_and_retrieve(li, ins[0], cs[g], ins[1], c_ctx, *params)
        token, dots = advance(prev, streams[0][4], li, streams)
        prev = (li, g, streams, dots)
    advance(prev, token, None, [])
    return jnp.concatenate(xs, axis=0)
```
